```python
import jax, jax.numpy as jnp
from jax import lax
import numpy as np

D_MODEL = 1024
BATCH = 8
SEQ = 8192
DEPTH = 4

N_MIXERS = 4
N_MEM = 256
MIX_WIDTH = D_MODEL
XATTN_HEADS = 4
XATTN_HEAD_DIM = 64
XATTN_WIDTH = XATTN_HEADS * XATTN_HEAD_DIM
TOK_WIDTH = MIX_WIDTH - XATTN_WIDTH

GMLP_CHUNK = 128
GMLP_HEAD_DIM = 128
GMLP_HEADS = TOK_WIDTH // GMLP_HEAD_DIM

HGRN_HEAD_DIM = 128
HGRN_HEADS = TOK_WIDTH // HGRN_HEAD_DIM
HGRN_CHUNK = 16

POOL_WINDOWS = (2, 4, 8, 16)
POOL_GROUP = TOK_WIDTH // len(POOL_WINDOWS)

LRU_HEAD_DIM = 128
LRU_HEADS = TOK_WIDTH // LRU_HEAD_DIM
CONV_WIDTH = 4
LRU_C = 8.0

ALPHA = (2 * DEPTH) ** 0.25
BETA = (8 * DEPTH) ** -0.25
LN_EPS = 1e-5
RMS_EPS = 1e-6

IN_WIDTH_A = 2 * TOK_WIDTH + XATTN_WIDTH + MIX_WIDTH
IN_WIDTH_B = 3 * TOK_WIDTH + XATTN_WIDTH + MIX_WIDTH
IN_WIDTH_C = TOK_WIDTH + XATTN_WIDTH + MIX_WIDTH
IN_WIDTH_D = TOK_WIDTH + XATTN_WIDTH + MIX_WIDTH

kernel_name = 'hybrid_gmlp_hgrn2_pool_rglru_trunk'


def n_of_kind(kind):
    return len(range(kind, DEPTH, N_MIXERS))


def split_cols(t, widths):
    idx = [int(v) for v in np.cumsum(widths)[:-1]]
    return jnp.split(t, idx, axis=-1)


def layer_norm(x, g, b):
    xf = x.astype(jnp.float32)
    mu = jnp.mean(xf, -1, keepdims=True)
    var = jnp.mean(jnp.square(xf - mu), -1, keepdims=True)
    y = (xf - mu) * lax.rsqrt(var + LN_EPS) * g.astype(jnp.float32) + b.astype(jnp.float32)
    return y.astype(x.dtype)


def memory_cross_attention(q, mem_k, mem_v):
    B, S, _ = q.shape
    qh = q.reshape(B, S, XATTN_HEADS, XATTN_HEAD_DIM)
    s = jnp.einsum('bshd,bmhd->bhsm', qh, mem_k).astype(jnp.float32) * (XATTN_HEAD_DIM ** -0.5)
    p = jax.nn.softmax(s, axis=-1).astype(mem_v.dtype)
    o = jnp.einsum('bhsm,bmhd->bshd', p, mem_v)
    return o.reshape(B, S, XATTN_WIDTH).astype(q.dtype)


def chunked_spatial_gating(u, v, w_s, b_s):
    dt = u.dtype
    B, S, _ = v.shape
    n = S // GMLP_CHUNK
    u = jax.nn.gelu(u.astype(jnp.float32))
    v = jax.nn.gelu(v.astype(jnp.float32))
    vg = v.reshape(B, n, GMLP_CHUNK, GMLP_HEADS, GMLP_HEAD_DIM)
    mu = jnp.mean(vg, -1, keepdims=True)
    var = jnp.mean(jnp.square(vg - mu), -1, keepdims=True)
    vn = (vg - mu) * lax.rsqrt(var + LN_EPS)
    causal = jnp.tril(jnp.ones((GMLP_CHUNK, GMLP_CHUNK), dtype=bool))
    w = jnp.where(causal[None], w_s.astype(jnp.float32), 0.0)
    mixed = jnp.einsum('gts,bnsgc->bntgc', w, vn) + b_s.astype(jnp.float32).T[:, :, None]
    return (u * mixed.reshape(B, S, TOK_WIDTH)).astype(dt)


def hgrn2(q, f_logit, i, lb, norm_g):
    dt = q.dtype
    B, S, _ = q.shape
    H, K, C = HGRN_HEADS, HGRN_HEAD_DIM, HGRN_CHUNK
    n = S // C
    f = lb + (1.0 - lb) * jax.nn.sigmoid(f_logit.astype(jnp.float32))
    log_f = jnp.log(f)
    k = 1.0 - f
    qf = jax.nn.silu(q.astype(jnp.float32))
    vf = i.astype(jnp.float32)

    def chunks(t):
        return t.reshape(B, n, C, H, K).transpose(1, 0, 2, 3, 4)

    causal = jnp.tril(jnp.ones((C, C), dtype=bool))

    def step(state, xs):
        qc, kc, vc, lfc = xs
        g = jnp.cumsum(lfc, axis=1)
        g_last = g[:, -1]
        q_dec = qc * jnp.exp(g)
        k_inv = kc * jnp.exp(-g)
        scores = jnp.einsum('bthk,bshk->bhts', q_dec, k_inv)
        scores = jnp.where(causal, scores, 0.0)
        o = (jnp.einsum('bhts,bshv->bthv', scores, vc)
             + jnp.einsum('bthk,bhkv->bthv', q_dec, state))
        k_end = kc * jnp.exp(g_last[:, None] - g)
        state = jnp.exp(g_last)[..., None] * state + jnp.einsum('bshk,bshv->bhkv', k_end, vc)
        return state, o

    s0 = jnp.zeros((B, H, K, K), jnp.float32)
    _, o = lax.scan(step, s0, (chunks(qf), chunks(k), chunks(vf), chunks(log_f)))
    o = o.transpose(1, 0, 2, 3, 4).reshape(B, S, H, K)
    o = o * lax.rsqrt(jnp.mean(jnp.square(o), -1, keepdims=True) + RMS_EPS)
    return (o.reshape(B, S, TOK_WIDTH) * norm_g.astype(jnp.float32)).astype(dt)


def multiscale_pool(p, w_pool, scale):
    dt = p.dtype
    B, S, _ = p.shape
    grp = p.astype(jnp.float32).reshape(B, S, len(POOL_WINDOWS), POOL_GROUP)
    cs = jnp.cumsum(grp, axis=1)
    pos = jnp.arange(S)
    pooled = []
    for g, w in enumerate(POOL_WINDOWS):
        csg = cs[:, :, g]
        prev = jnp.pad(csg, ((0, 0), (w, 0), (0, 0)))[:, :S]
        cnt = jnp.minimum(pos + 1, w).astype(jnp.float32)[None, :, None]
        pooled.append((csg - prev) / cnt)
    pooled = jnp.stack(pooled, axis=2)
    y = jnp.einsum('bsgc,gcd->bsgd', pooled - grp, w_pool.astype(jnp.float32))
    return (y.reshape(B, S, TOK_WIDTH) * scale.astype(jnp.float32)).astype(dt)


def rg_lru_branch(xb, conv_w, conv_b, w_gx, b_gx, w_ga, b_ga, a_param):
    dt = xb.dtype
    B, S, _ = xb.shape
    xc = lax.conv_general_dilated(
        xb, conv_w.astype(dt)[:, None, :], window_strides=(1,),
        padding=[(CONV_WIDTH - 1, 0)], dimension_numbers=('NWC', 'WIO', 'NWC'),
        feature_group_count=TOK_WIDTH) + conv_b.astype(dt)
    xh = xc.astype(jnp.float32).reshape(B, S, LRU_HEADS, LRU_HEAD_DIM)
    gate_x = jax.nn.sigmoid(jnp.einsum('bshi,hij->bshj', xh, w_gx.astype(jnp.float32)) + b_gx.astype(jnp.float32))
    gate_a = jax.nn.sigmoid(jnp.einsum('bshi,hij->bshj', xh, w_ga.astype(jnp.float32)) + b_ga.astype(jnp.float32))
    log_a = -LRU_C * gate_a * jax.nn.softplus(-a_param.astype(jnp.float32).reshape(LRU_HEADS, LRU_HEAD_DIM))
    a = jnp.exp(log_a)
    mult = jnp.sqrt(-jnp.expm1(2.0 * log_a))
    first = (jnp.arange(S) == 0)[None, :, None, None]
    mult = jnp.where(first, 1.0, mult)
    b_term = mult * gate_x * xh

    def combine(l, r):
        a1, b1 = l
        a2, b2 = r
        return a1 * a2, a2 * b1 + b2

    _, h = lax.associative_scan(combine, (a, b_term), axis=1)
    return h.reshape(B, S, TOK_WIDTH).astype(dt)


def _fwd_setup_inputs(seed: int = 0) -> dict:
    key = jax.random.key(seed)
    ks = jax.random.split(key, 24)
    f32 = jnp.float32

    def nrm(k, shape, scale):
        return jax.random.normal(k, shape, f32) * scale

    nA, nB, nC, nD = n_of_kind(0), n_of_kind(1), n_of_kind(2), n_of_kind(3)
    u = jax.random.uniform(ks[22], (nD, TOK_WIDTH), f32, minval=0.9, maxval=0.999)
    s = u ** (1.0 / LRU_C)
    return {
        'x': nrm(ks[0], (BATCH, SEQ, D_MODEL), 1.0),
        'mem': nrm(ks[1], (BATCH, N_MEM, D_MODEL), 1.0),
        'mem_kv_w': nrm(ks[2], (D_MODEL, 2 * XATTN_WIDTH), D_MODEL ** -0.5),
        'ln_g': 1.0 + nrm(ks[3], (DEPTH, D_MODEL), 0.02),
        'ln_b': nrm(ks[4], (DEPTH, D_MODEL), 0.02),
        'w_out': nrm(ks[5], (DEPTH, MIX_WIDTH, D_MODEL), BETA * MIX_WIDTH ** -0.5),
        'hgrn_lb_logits': 1.0 + nrm(ks[6], (DEPTH, TOK_WIDTH), 0.1),
        'a_w_in': nrm(ks[7], (nA, D_MODEL, IN_WIDTH_A), D_MODEL ** -0.5),
        'a_w_s': nrm(ks[8], (nA, GMLP_HEADS, GMLP_CHUNK, GMLP_CHUNK), GMLP_CHUNK ** -0.5),
        'a_b_s': 1.0 + nrm(ks[9], (nA, GMLP_HEADS, GMLP_CHUNK), 0.02),
        'b_w_in': nrm(ks[10], (nB, D_MODEL, IN_WIDTH_B), D_MODEL ** -0.5),
        'b_norm_g': 1.0 + nrm(ks[11], (nB, TOK_WIDTH), 0.02),
        'c_w_in': nrm(ks[12], (nC, D_MODEL, IN_WIDTH_C), D_MODEL ** -0.5),
        'c_w_pool': nrm(ks[13], (nC, len(POOL_WINDOWS), POOL_GROUP, POOL_GROUP), POOL_GROUP ** -0.5),
        'c_scale': 1.0 + nrm(ks[14], (nC, TOK_WIDTH), 0.02),
        'd_w_in': nrm(ks[15], (nD, D_MODEL, IN_WIDTH_D), D_MODEL ** -0.5),
        'd_conv_w': nrm(ks[16], (nD, CONV_WIDTH, TOK_WIDTH), CONV_WIDTH ** -0.5),
        'd_conv_b': nrm(ks[17], (nD, TOK_WIDTH), 0.02),
        'd_w_gx': nrm(ks[18], (nD, LRU_HEADS, LRU_HEAD_DIM, LRU_HEAD_DIM), LRU_HEAD_DIM ** -0.5),
        'd_b_gx': nrm(ks[19], (nD, LRU_HEADS, LRU_HEAD_DIM), 0.02),
        'd_w_ga': nrm(ks[20], (nD, LRU_HEADS, LRU_HEAD_DIM, LRU_HEAD_DIM), LRU_HEAD_DIM ** -0.5),
        'd_b_ga': nrm(ks[21], (nD, LRU_HEADS, LRU_HEAD_DIM), 0.02),
        'd_a_param': jnp.log(s) - jnp.log1p(-s),
    }


def _fwd_reference(x, mem, mem_kv_w, ln_g, ln_b, w_out, hgrn_lb_logits,
              a_w_in, a_w_s, a_b_s,
              b_w_in, b_norm_g,
              c_w_in, c_w_pool, c_scale,
              d_w_in, d_conv_w, d_conv_b, d_w_gx, d_b_gx, d_w_ga, d_b_ga, d_a_param):
    B = x.shape[0]
    M = mem.shape[1]
    kv = jnp.einsum('bmd,de->bme', mem, mem_kv_w)
    mem_k = kv[..., :XATTN_WIDTH].reshape(B, M, XATTN_HEADS, XATTN_HEAD_DIM)
    mem_v = kv[..., XATTN_WIDTH:].reshape(B, M, XATTN_HEADS, XATTN_HEAD_DIM)
    lb_p = jax.nn.softmax(hgrn_lb_logits.astype(jnp.float32), axis=0)
    lower_bounds = jnp.cumsum(lb_p, axis=0) - lb_p[0]

    for i in range(DEPTH):
        kind, j = i % N_MIXERS, i // N_MIXERS
        if kind == 0:
            proj = jnp.einsum('bsd,de->bse', x, a_w_in[j])
            u, v, q_x, gate = split_cols(proj, [TOK_WIDTH, TOK_WIDTH, XATTN_WIDTH, MIX_WIDTH])
            tok = chunked_spatial_gating(u, v, a_w_s[j], a_b_s[j])
        elif kind == 1:
            proj = jnp.einsum('bsd,de->bse', x, b_w_in[j])
            q, f_logit, inp, q_x, gate = split_cols(
                proj, [TOK_WIDTH, TOK_WIDTH, TOK_WIDTH, XATTN_WIDTH, MIX_WIDTH])
            tok = hgrn2(q, f_logit, inp, lower_bounds[i], b_norm_g[j])
        elif kind == 2:
            proj = jnp.einsum('bsd,de->bse', x, c_w_in[j])
            p, q_x, gate = split_cols(proj, [TOK_WIDTH, XATTN_WIDTH, MIX_WIDTH])
            tok = multiscale_pool(p, c_w_pool[j], c_scale[j])
        else:
            proj = jnp.einsum('bsd,de->bse', x, d_w_in[j])
            xb, q_x, gate = split_cols(proj, [TOK_WIDTH, XATTN_WIDTH, MIX_WIDTH])
            tok = rg_lru_branch(xb, d_conv_w[j], d_conv_b[j], d_w_gx[j], d_b_gx[j],
                                d_w_ga[j], d_b_ga[j], d_a_param[j])
        xo = memory_cross_attention(q_x, mem_k, mem_v)
        mixed = jnp.concatenate([tok.astype(x.dtype), xo.astype(x.dtype)], axis=-1) * jax.nn.silu(gate)
        y = jnp.einsum('bse,ed->bsd', mixed, w_out[i]).astype(x.dtype)
        x = layer_norm(ALPHA * x + y, ln_g[i], ln_b[i])
    return x


import jax as _jax
import jax.numpy as _jnp

TWIN_FORMAT = 'train_step'
FWD_PARAMS = ['x', 'mem', 'mem_kv_w', 'ln_g', 'ln_b', 'w_out', 'hgrn_lb_logits', 'a_w_in', 'a_w_s', 'a_b_s', 'b_w_in', 'b_norm_g', 'c_w_in', 'c_w_pool', 'c_scale', 'd_w_in', 'd_conv_w', 'd_conv_b', 'd_w_gx', 'd_b_gx', 'd_w_ga', 'd_b_ga', 'd_a_param']
TWIN_WEIGHTS = ['mem_kv_w', 'ln_g', 'ln_b', 'w_out', 'hgrn_lb_logits', 'a_w_in', 'a_w_s', 'a_b_s', 'b_w_in', 'b_norm_g', 'c_w_in', 'c_w_pool', 'c_scale', 'd_w_in', 'd_conv_w', 'd_conv_b', 'd_w_gx', 'd_b_gx', 'd_w_ga', 'd_b_ga', 'd_a_param']
TWIN_DIFF_INPUT = 'x'
TWIN_INPUTS = ['x', 'mem', 'mem_kv_w', 'ln_g', 'ln_b', 'w_out', 'hgrn_lb_logits', 'a_w_in', 'a_w_s', 'a_b_s', 'b_w_in', 'b_norm_g', 'c_w_in', 'c_w_pool', 'c_scale', 'd_w_in', 'd_conv_w', 'd_conv_b', 'd_w_gx', 'd_b_gx', 'd_w_ga', 'd_b_ga', 'd_a_param', 'loss_target', 'm_mem_kv_w', 'm_ln_g', 'm_ln_b', 'm_w_out', 'm_hgrn_lb_logits', 'm_a_w_in', 'm_a_w_s', 'm_a_b_s', 'm_b_w_in', 'm_b_norm_g', 'm_c_w_in', 'm_c_w_pool', 'm_c_scale', 'm_d_w_in', 'm_d_conv_w', 'm_d_conv_b', 'm_d_w_gx', 'm_d_b_gx', 'm_d_w_ga', 'm_d_b_ga', 'm_d_a_param', 'v_mem_kv_w', 'v_ln_g', 'v_ln_b', 'v_w_out', 'v_hgrn_lb_logits', 'v_a_w_in', 'v_a_w_s', 'v_a_b_s', 'v_b_w_in', 'v_b_norm_g', 'v_c_w_in', 'v_c_w_pool', 'v_c_scale', 'v_d_w_in', 'v_d_conv_w', 'v_d_conv_b', 'v_d_w_gx', 'v_d_b_gx', 'v_d_w_ga', 'v_d_b_ga', 'v_d_a_param']
TWIN_OUTPUTS = ['loss', 'grad_x', 'grad_mem_kv_w', 'grad_ln_g', 'grad_ln_b', 'grad_w_out', 'grad_hgrn_lb_logits', 'grad_a_w_in', 'grad_a_w_s', 'grad_a_b_s', 'grad_b_w_in', 'grad_b_norm_g', 'grad_c_w_in', 'grad_c_w_pool', 'grad_c_scale', 'grad_d_w_in', 'grad_d_conv_w', 'grad_d_conv_b', 'grad_d_w_gx', 'grad_d_b_gx', 'grad_d_w_ga', 'grad_d_b_ga', 'grad_d_a_param', 'delta_mem_kv_w', 'delta_ln_g', 'delta_ln_b', 'delta_w_out', 'delta_hgrn_lb_logits', 'delta_a_w_in', 'delta_a_w_s', 'delta_a_b_s', 'delta_b_w_in', 'delta_b_norm_g', 'delta_c_w_in', 'delta_c_w_pool', 'delta_c_scale', 'delta_d_w_in', 'delta_d_conv_w', 'delta_d_conv_b', 'delta_d_w_gx', 'delta_d_b_gx', 'delta_d_w_ga', 'delta_d_b_ga', 'delta_d_a_param', 'new_m_mem_kv_w', 'new_m_ln_g', 'new_m_ln_b', 'new_m_w_out', 'new_m_hgrn_lb_logits', 'new_m_a_w_in', 'new_m_a_w_s', 'new_m_a_b_s', 'new_m_b_w_in', 'new_m_b_norm_g', 'new_m_c_w_in', 'new_m_c_w_pool', 'new_m_c_scale', 'new_m_d_w_in', 'new_m_d_conv_w', 'new_m_d_conv_b', 'new_m_d_w_gx', 'new_m_d_b_gx', 'new_m_d_w_ga', 'new_m_d_b_ga', 'new_m_d_a_param', 'new_v_mem_kv_w', 'new_v_ln_g', 'new_v_ln_b', 'new_v_w_out', 'new_v_hgrn_lb_logits', 'new_v_a_w_in', 'new_v_a_w_s', 'new_v_a_b_s', 'new_v_b_w_in', 'new_v_b_norm_g', 'new_v_c_w_in', 'new_v_c_w_pool', 'new_v_c_scale', 'new_v_d_w_in', 'new_v_d_conv_w', 'new_v_d_conv_b', 'new_v_d_w_gx', 'new_v_d_b_gx', 'new_v_d_w_ga', 'new_v_d_b_ga', 'new_v_d_a_param']
TWIN_LEAF_KINDS = {'loss': 'loss', 'grad_x': 'grad_x', 'grad_mem_kv_w': 'grad_w', 'grad_ln_g': 'grad_w', 'grad_ln_b': 'grad_w', 'grad_w_out': 'grad_w', 'grad_hgrn_lb_logits': 'grad_w', 'grad_a_w_in': 'grad_w', 'grad_a_w_s': 'grad_w', 'grad_a_b_s': 'grad_w', 'grad_b_w_in': 'grad_w', 'grad_b_norm_g': 'grad_w', 'grad_c_w_in': 'grad_w', 'grad_c_w_pool': 'grad_w', 'grad_c_scale': 'grad_w', 'grad_d_w_in': 'grad_w', 'grad_d_conv_w': 'grad_w', 'grad_d_conv_b': 'grad_w', 'grad_d_w_gx': 'grad_w', 'grad_d_b_gx': 'grad_w', 'grad_d_w_ga': 'grad_w', 'grad_d_b_ga': 'grad_w', 'grad_d_a_param': 'grad_w', 'delta_mem_kv_w': 'delta_w', 'delta_ln_g': 'delta_w', 'delta_ln_b': 'delta_w', 'delta_w_out': 'delta_w', 'delta_hgrn_lb_logits': 'delta_w', 'delta_a_w_in': 'delta_w', 'delta_a_w_s': 'delta_w', 'delta_a_b_s': 'delta_w', 'delta_b_w_in': 'delta_w', 'delta_b_norm_g': 'delta_w', 'delta_c_w_in': 'delta_w', 'delta_c_w_pool': 'delta_w', 'delta_c_scale': 'delta_w', 'delta_d_w_in': 'delta_w', 'delta_d_conv_w': 'delta_w', 'delta_d_conv_b': 'delta_w', 'delta_d_w_gx': 'delta_w', 'delta_d_b_gx': 'delta_w', 'delta_d_w_ga': 'delta_w', 'delta_d_b_ga': 'delta_w', 'delta_d_a_param': 'delta_w', 'new_m_mem_kv_w': 'new_m', 'new_m_ln_g': 'new_m', 'new_m_ln_b': 'new_m', 'new_m_w_out': 'new_m', 'new_m_hgrn_lb_logits': 'new_m', 'new_m_a_w_in': 'new_m', 'new_m_a_w_s': 'new_m', 'new_m_a_b_s': 'new_m', 'new_m_b_w_in': 'new_m', 'new_m_b_norm_g': 'new_m', 'new_m_c_w_in': 'new_m', 'new_m_c_w_pool': 'new_m', 'new_m_c_scale': 'new_m', 'new_m_d_w_in': 'new_m', 'new_m_d_conv_w': 'new_m', 'new_m_d_conv_b': 'new_m', 'new_m_d_w_gx': 'new_m', 'new_m_d_b_gx': 'new_m', 'new_m_d_w_ga': 'new_m', 'new_m_d_b_ga': 'new_m', 'new_m_d_a_param': 'new_m', 'new_v_mem_kv_w': 'new_v', 'new_v_ln_g': 'new_v', 'new_v_ln_b': 'new_v', 'new_v_w_out': 'new_v', 'new_v_hgrn_lb_logits': 'new_v', 'new_v_a_w_in': 'new_v', 'new_v_a_w_s': 'new_v', 'new_v_a_b_s': 'new_v', 'new_v_b_w_in': 'new_v', 'new_v_b_norm_g': 'new_v', 'new_v_c_w_in': 'new_v', 'new_v_c_w_pool': 'new_v', 'new_v_c_scale': 'new_v', 'new_v_d_w_in': 'new_v', 'new_v_d_conv_w': 'new_v', 'new_v_d_conv_b': 'new_v', 'new_v_d_w_gx': 'new_v', 'new_v_d_b_gx': 'new_v', 'new_v_d_w_ga': 'new_v', 'new_v_d_b_ga': 'new_v', 'new_v_d_a_param': 'new_v'}


def _forward(args):
    return _fwd_reference(*[args[k] for k in FWD_PARAMS])


def _output_shape():
    def fwd():
        inp = _fwd_setup_inputs(0)
        return _fwd_reference(*[inp[k] for k in FWD_PARAMS])
    out = _jax.eval_shape(fwd)
    return out.shape, out.dtype

N_MICROBATCH = 1
ADAM_LR = 0.001
ADAM_B1 = 0.9
ADAM_B2 = 0.999
ADAM_EPS = 1e-08
ADAM_WD = 0.01
ADAM_STEP = 10
PER_EXAMPLE_BATCH_AXIS = {'x': 0, 'mem': 0, 'loss_target': 0}
SHARED_INPUTS = []
_WEIGHT_DTYPES = {'mem_kv_w': _jnp.float32, 'ln_g': _jnp.float32, 'ln_b': _jnp.float32, 'w_out': _jnp.float32, 'hgrn_lb_logits': _jnp.float32, 'a_w_in': _jnp.float32, 'a_w_s': _jnp.float32, 'a_b_s': _jnp.float32, 'b_w_in': _jnp.float32, 'b_norm_g': _jnp.float32, 'c_w_in': _jnp.float32, 'c_w_pool': _jnp.float32, 'c_scale': _jnp.float32, 'd_w_in': _jnp.float32, 'd_conv_w': _jnp.float32, 'd_conv_b': _jnp.float32, 'd_w_gx': _jnp.float32, 'd_b_gx': _jnp.float32, 'd_w_ga': _jnp.float32, 'd_b_ga': _jnp.float32, 'd_a_param': _jnp.float32}
MOMENT_SCALE = {'mem_kv_w': 8.112955e-03, 'ln_g': 3.211333e+01, 'ln_b': 1.501147e+00, 'w_out': 6.874053e-02, 'hgrn_lb_logits': 1.147915e-03, 'a_w_in': 2.552602e-02, 'a_w_s': 1.774291e-02, 'a_b_s': 2.460304e-02, 'b_w_in': 2.525895e-02, 'b_norm_g': 3.891039e-02, 'c_w_in': 2.881890e-02, 'c_w_pool': 3.270126e-02, 'c_scale': 3.198449e-02, 'd_w_in': 2.436643e-02, 'd_conv_w': 3.182155e-02, 'd_conv_b': 3.759565e-01, 'd_w_gx': 1.679635e-02, 'd_b_gx': 1.129587e-02, 'd_w_ga': 9.191687e-03, 'd_b_ga': 8.388946e-03, 'd_a_param': 1.800789e-02}


def _to_microbatches(a, axis):
    t = _jnp.moveaxis(a, axis, 0)
    t = t.reshape((N_MICROBATCH, t.shape[0] // N_MICROBATCH) + t.shape[1:])
    return _jnp.moveaxis(t, 1, axis + 1)


def setup_inputs(seed: int = 0) -> dict:
    inp = _fwd_setup_inputs(seed)
    key = _jax.random.fold_in(_jax.random.key(seed), 7919)
    shape, _ = _output_shape()
    out = dict(inp)
    out["loss_target"] = _jax.random.normal(_jax.random.fold_in(key, 0), shape, _jnp.float32)
    for i, name in enumerate(TWIN_WEIGHTS):
        w = inp[name].astype(_jnp.float32)
        if MOMENT_SCALE is None:
            s = _jnp.sqrt(_jnp.mean(_jnp.square(w)) + 1e-30)
        else:
            s = MOMENT_SCALE[name]
        km, kv = _jax.random.split(_jax.random.fold_in(key, i + 1))
        out[name] = w
        out["m_" + name] = s * _jax.random.normal(km, w.shape, _jnp.float32)
        out["v_" + name] = (s * s) * _jax.random.uniform(kv, w.shape, _jnp.float32, 0.5, 1.5)
    if N_MICROBATCH > 1:
        for name, axis in PER_EXAMPLE_BATCH_AXIS.items():
            out[name] = _to_microbatches(out[name], axis)
    return {'x': out['x'], 'mem': out['mem'], 'mem_kv_w': out['mem_kv_w'], 'ln_g': out['ln_g'], 'ln_b': out['ln_b'], 'w_out': out['w_out'], 'hgrn_lb_logits': out['hgrn_lb_logits'], 'a_w_in': out['a_w_in'], 'a_w_s': out['a_w_s'], 'a_b_s': out['a_b_s'], 'b_w_in': out['b_w_in'], 'b_norm_g': out['b_norm_g'], 'c_w_in': out['c_w_in'], 'c_w_pool': out['c_w_pool'], 'c_scale': out['c_scale'], 'd_w_in': out['d_w_in'], 'd_conv_w': out['d_conv_w'], 'd_conv_b': out['d_conv_b'], 'd_w_gx': out['d_w_gx'], 'd_b_gx': out['d_b_gx'], 'd_w_ga': out['d_w_ga'], 'd_b_ga': out['d_b_ga'], 'd_a_param': out['d_a_param'], 'loss_target': out['loss_target'], 'm_mem_kv_w': out['m_mem_kv_w'], 'm_ln_g': out['m_ln_g'], 'm_ln_b': out['m_ln_b'], 'm_w_out': out['m_w_out'], 'm_hgrn_lb_logits': out['m_hgrn_lb_logits'], 'm_a_w_in': out['m_a_w_in'], 'm_a_w_s': out['m_a_w_s'], 'm_a_b_s': out['m_a_b_s'], 'm_b_w_in': out['m_b_w_in'], 'm_b_norm_g': out['m_b_norm_g'], 'm_c_w_in': out['m_c_w_in'], 'm_c_w_pool': out['m_c_w_pool'], 'm_c_scale': out['m_c_scale'], 'm_d_w_in': out['m_d_w_in'], 'm_d_conv_w': out['m_d_conv_w'], 'm_d_conv_b': out['m_d_conv_b'], 'm_d_w_gx': out['m_d_w_gx'], 'm_d_b_gx': out['m_d_b_gx'], 'm_d_w_ga': out['m_d_w_ga'], 'm_d_b_ga': out['m_d_b_ga'], 'm_d_a_param': out['m_d_a_param'], 'v_mem_kv_w': out['v_mem_kv_w'], 'v_ln_g': out['v_ln_g'], 'v_ln_b': out['v_ln_b'], 'v_w_out': out['v_w_out'], 'v_hgrn_lb_logits': out['v_hgrn_lb_logits'], 'v_a_w_in': out['v_a_w_in'], 'v_a_w_s': out['v_a_w_s'], 'v_a_b_s': out['v_a_b_s'], 'v_b_w_in': out['v_b_w_in'], 'v_b_norm_g': out['v_b_norm_g'], 'v_c_w_in': out['v_c_w_in'], 'v_c_w_pool': out['v_c_w_pool'], 'v_c_scale': out['v_c_scale'], 'v_d_w_in': out['v_d_w_in'], 'v_d_conv_w': out['v_d_conv_w'], 'v_d_conv_b': out['v_d_conv_b'], 'v_d_w_gx': out['v_d_w_gx'], 'v_d_b_gx': out['v_d_b_gx'], 'v_d_w_ga': out['v_d_w_ga'], 'v_d_b_ga': out['v_d_b_ga'], 'v_d_a_param': out['v_d_a_param']}


def _loss(weights, diff, rest, loss_target):
    with _jax.named_scope("forward"):
        args = {**rest, TWIN_DIFF_INPUT: diff, **{k: w.astype(_WEIGHT_DTYPES[k]) for k, w in weights.items()}}
        y = _forward(args)
    with _jax.named_scope("loss_head"):
        err = _jnp.square(y.astype(_jnp.float32) - loss_target)
        return 0.5 * _jnp.sum(_jnp.mean(err, axis=-1)) if err.ndim else 0.5 * err


def _adamw(w, g, m, v):
    m = ADAM_B1 * m + (1.0 - ADAM_B1) * g
    v = ADAM_B2 * v + (1.0 - ADAM_B2) * _jnp.square(g)
    m_hat = m / (1.0 - ADAM_B1 ** ADAM_STEP)
    v_hat = v / (1.0 - ADAM_B2 ** ADAM_STEP)
    delta = -ADAM_LR * (m_hat / (_jnp.sqrt(v_hat) + ADAM_EPS) + ADAM_WD * w)
    return delta, m, v


def reference(x, mem, mem_kv_w, ln_g, ln_b, w_out, hgrn_lb_logits, a_w_in, a_w_s, a_b_s, b_w_in, b_norm_g, c_w_in, c_w_pool, c_scale, d_w_in, d_conv_w, d_conv_b, d_w_gx, d_b_gx, d_w_ga, d_b_ga, d_a_param, loss_target, m_mem_kv_w, m_ln_g, m_ln_b, m_w_out, m_hgrn_lb_logits, m_a_w_in, m_a_w_s, m_a_b_s, m_b_w_in, m_b_norm_g, m_c_w_in, m_c_w_pool, m_c_scale, m_d_w_in, m_d_conv_w, m_d_conv_b, m_d_w_gx, m_d_b_gx, m_d_w_ga, m_d_b_ga, m_d_a_param, v_mem_kv_w, v_ln_g, v_ln_b, v_w_out, v_hgrn_lb_logits, v_a_w_in, v_a_w_s, v_a_b_s, v_b_w_in, v_b_norm_g, v_c_w_in, v_c_w_pool, v_c_scale, v_d_w_in, v_d_conv_w, v_d_conv_b, v_d_w_gx, v_d_b_gx, v_d_w_ga, v_d_b_ga, v_d_a_param):
    given = dict(x=x, mem=mem, mem_kv_w=mem_kv_w, ln_g=ln_g, ln_b=ln_b, w_out=w_out, hgrn_lb_logits=hgrn_lb_logits, a_w_in=a_w_in, a_w_s=a_w_s, a_b_s=a_b_s, b_w_in=b_w_in, b_norm_g=b_norm_g, c_w_in=c_w_in, c_w_pool=c_w_pool, c_scale=c_scale, d_w_in=d_w_in, d_conv_w=d_conv_w, d_conv_b=d_conv_b, d_w_gx=d_w_gx, d_b_gx=d_b_gx, d_w_ga=d_w_ga, d_b_ga=d_b_ga, d_a_param=d_a_param, loss_target=loss_target, m_mem_kv_w=m_mem_kv_w, m_ln_g=m_ln_g, m_ln_b=m_ln_b, m_w_out=m_w_out, m_hgrn_lb_logits=m_hgrn_lb_logits, m_a_w_in=m_a_w_in, m_a_w_s=m_a_w_s, m_a_b_s=m_a_b_s, m_b_w_in=m_b_w_in, m_b_norm_g=m_b_norm_g, m_c_w_in=m_c_w_in, m_c_w_pool=m_c_w_pool, m_c_scale=m_c_scale, m_d_w_in=m_d_w_in, m_d_conv_w=m_d_conv_w, m_d_conv_b=m_d_conv_b, m_d_w_gx=m_d_w_gx, m_d_b_gx=m_d_b_gx, m_d_w_ga=m_d_w_ga, m_d_b_ga=m_d_b_ga, m_d_a_param=m_d_a_param, v_mem_kv_w=v_mem_kv_w, v_ln_g=v_ln_g, v_ln_b=v_ln_b, v_w_out=v_w_out, v_hgrn_lb_logits=v_hgrn_lb_logits, v_a_w_in=v_a_w_in, v_a_w_s=v_a_w_s, v_a_b_s=v_a_b_s, v_b_w_in=v_b_w_in, v_b_norm_g=v_b_norm_g, v_c_w_in=v_c_w_in, v_c_w_pool=v_c_w_pool, v_c_scale=v_c_scale, v_d_w_in=v_d_w_in, v_d_conv_w=v_d_conv_w, v_d_conv_b=v_d_conv_b, v_d_w_gx=v_d_w_gx, v_d_b_gx=v_d_b_gx, v_d_w_ga=v_d_w_ga, v_d_b_ga=v_d_b_ga, v_d_a_param=v_d_a_param)
    weights = {n: given[n] for n in TWIN_WEIGHTS}
    shared = {n: given[n] for n in SHARED_INPUTS}
    per_example = {n: given[n] for n in ['x', 'mem']}
    grad_fn = _jax.value_and_grad(_loss, argnums=(0, 1))

    def one_microbatch(ex, loss_target):
        ex = dict(ex)
        diff = ex.pop(TWIN_DIFF_INPUT)
        return grad_fn(weights, diff, {**shared, **ex}, loss_target)

    if N_MICROBATCH == 1:
        loss, (grad_w, grad_x) = one_microbatch(per_example, given["loss_target"])
    else:
        def body(carry, xs):
            loss_sum, grad_sum = carry
            l_k, (gw_k, gx_k) = one_microbatch(xs[0], xs[1])
            with _jax.named_scope("update"):
                return (loss_sum + l_k, _jax.tree.map(_jnp.add, grad_sum, gw_k)), gx_k

        init = (_jnp.zeros((), _jnp.float32), _jax.tree.map(_jnp.zeros_like, weights))
        (loss, grad_w), grad_x = _jax.lax.scan(body, init, (per_example, given["loss_target"]))
    with _jax.named_scope("update"):
        delta_w, new_m, new_v = {}, {}, {}
        for n in TWIN_WEIGHTS:
            delta_w[n], new_m[n], new_v[n] = _adamw(weights[n], grad_w[n], given["m_" + n], given["v_" + n])
    return (loss, grad_x, *[grad_w[n] for n in TWIN_WEIGHTS], *[delta_w[n] for n in TWIN_WEIGHTS],
            *[new_m[n] for n in TWIN_WEIGHTS], *[new_v[n] for n in TWIN_WEIGHTS])
```

```python
import functools

import jax
import jax.numpy as jnp
from jax import lax
from jax.experimental import pallas as pl
from jax.experimental.pallas import tpu as pltpu

F32 = jnp.float32
MXU = jnp.bfloat16

D = 1024
TOK = 768
XW = 256
NMEM = 256
XHEADS = 4
XSCALE = 64 ** -0.5
NH = 6
HD = 128
CH = 16
POOL_WINDOWS = (2, 4, 8, 16)
POOL_GROUP = 192
DEPTH = 4
ALPHA = (2 * DEPTH) ** 0.25
LN_EPS = 1e-5
RMS_EPS = 1e-6
LRU_C = 8.0
B1, B2, LR, EPS, WD, STEP = 0.9, 0.999, 0.001, 1e-8, 0.01, 10

NDEV = 8
TS = 256
TSB = 512
VMEM_LIMIT = 58 * 1024 * 1024

KIND_WIDTHS = {0: 2 * TOK + XW + D, 1: 3 * TOK + XW + D, 2: TOK + XW + D, 3: TOK + XW + D}


def _mm(a, b, ca, cb):
    return lax.dot_general(a.astype(MXU), b.astype(MXU), (((ca,), (cb,)), ((), ())), preferred_element_type=F32)


def _nn(a, b):
    return _mm(a, b, 1, 0)


def _nt(a, b):
    return _mm(a, b, 1, 1)


def _tn(a, b):
    return _mm(a, b, 0, 0)


def _bmm(a, b, ca, cb):
    return lax.dot_general(a.astype(MXU), b.astype(MXU), (((ca,), (cb,)), ((0,), (0,))), preferred_element_type=F32)


def _sigmoid(x):
    return 1.0 / (1.0 + jnp.exp(-x))


def _vjp1(fn, x, dy):
    return jax.vjp(fn, x)[1](dy)[0]


def _rowsum(x):
    return jnp.sum(x, axis=0, keepdims=True)


def _row(x, r):
    sel = lax.broadcasted_iota(jnp.int32, x.shape, 0) == r
    return jnp.sum(jnp.where(sel, x, 0.0), axis=0, keepdims=True)


def _acc(ref, val):
    ref[...] += val


def _cparams(sem=None):
    return pltpu.CompilerParams(dimension_semantics=sem, vmem_limit_bytes=VMEM_LIMIT)


def _res(a):
    nd = a.ndim
    return pl.BlockSpec(a.shape, lambda i: (0,) * nd)


def _res_sds(shape):
    nd = len(shape)
    return pl.BlockSpec(shape, lambda i: (0,) * nd)


def _rows(width, nt, rev, ts=TS):
    if rev:
        return pl.BlockSpec((ts, width), lambda i: (nt - 1 - i, 0))
    return pl.BlockSpec((ts, width), lambda i: (i, 0))


def _call(body, name, grid, ins, outs, scratch=(), sem=("arbitrary",)):
    arrays = [a for a, _ in ins]
    return pl.pallas_call(
        body, name=name, grid=grid,
        in_specs=[s for _, s in ins],
        out_specs=[s for _, s in outs],
        out_shape=[o for o, _ in outs],
        scratch_shapes=list(scratch),
        compiler_params=_cparams(sem),
    )(*arrays)


def _xattn_fwd(qx, ks_ref, vs_ref):
    o = None
    ps = []
    for h in range(XHEADS):
        s = _nt(qx, ks_ref[h]) * XSCALE
        s = s - jnp.max(s, axis=-1, keepdims=True)
        e = jnp.exp(s)
        p = e / jnp.sum(e, axis=-1, keepdims=True)
        ps.append(p)
        oh = _nn(p, vs_ref[h])
        o = oh if o is None else o + oh
    return o, ps


def _xattn_bwd(qx, ps, dxo, ks_ref, vs_ref, dks_ref, dvs_ref):
    dq = None
    for h in range(XHEADS):
        p = ps[h]
        dp = _nt(dxo, vs_ref[h])
        ds = p * (dp - jnp.sum(dp * p, axis=-1, keepdims=True))
        dqh = _nn(ds, ks_ref[h]) * XSCALE
        dq = dqh if dq is None else dq + dqh
        dks_ref[h] += _tn(ds, qx) * XSCALE
        dvs_ref[h] += _tn(p, dxo)
    return dq


def _tril128():
    r = lax.broadcasted_iota(jnp.int32, (HD, HD), 0)
    c = lax.broadcasted_iota(jnp.int32, (HD, HD), 1)
    return c <= r


def _gmlp_fwd(u, v, ws_ref, bs_ref):
    ts = u.shape[0]
    ug = jax.nn.gelu(u)
    vg = jax.nn.gelu(v)
    tri = _tril128()
    toks, res = [], []
    for g in range(NH):
        sl = slice(g * HD, (g + 1) * HD)
        vgh = vg[:, sl]
        cen = vgh - jnp.mean(vgh, axis=-1, keepdims=True)
        rstd = lax.rsqrt(jnp.mean(cen * cen, axis=-1, keepdims=True) + LN_EPS)
        vn = cen * rstd
        w = jnp.where(tri, ws_ref[g], 0.0).astype(MXU)
        mix = jnp.concatenate([_nn(w, vn[n * HD:(n + 1) * HD]) + bs_ref[g] for n in range(ts // HD)], axis=0)
        toks.append(ug[:, sl] * mix)
        res.append((vn, rstd, mix, w))
    return jnp.concatenate(toks, axis=1), (ug, res)


def _gmlp_bwd(u, v, fres, dtok, dws_ref, dbs_ref):
    ts = u.shape[0]
    ug, res = fres
    tri = _tril128()
    dugs, dvgs = [], []
    for g in range(NH):
        sl = slice(g * HD, (g + 1) * HD)
        vn, rstd, mix, w = res[g]
        dmix = dtok[:, sl] * ug[:, sl]
        dugs.append(dtok[:, sl] * mix)
        dvn_rows = []
        dw = None
        dbs = None
        for n in range(ts // HD):
            dm = dmix[n * HD:(n + 1) * HD]
            dvn_rows.append(_tn(w, dm))
            t = _nt(dm, vn[n * HD:(n + 1) * HD])
            dw = t if dw is None else dw + t
            dbs = dm if dbs is None else dbs + dm
        dws_ref[g] += jnp.where(tri, dw, 0.0)
        dbs_ref[g] += dbs
        dvn = jnp.concatenate(dvn_rows, axis=0)
        dvgs.append(rstd * (dvn - jnp.mean(dvn, axis=-1, keepdims=True) - vn * jnp.mean(dvn * vn, axis=-1, keepdims=True)))
    du = _vjp1(jax.nn.gelu, u, jnp.concatenate(dugs, axis=1))
    dv = _vjp1(jax.nn.gelu, v, jnp.concatenate(dvgs, axis=1))
    return du, dv


def _chunk_cumsum(x):
    row = lax.broadcasted_iota(jnp.int32, x.shape, 0) % CH
    for s in (1, 2, 4, 8):
        x = x + jnp.where(row >= s, pltpu.roll(x, s, 0), 0.0)
    return x


def _chunk_revcumsum(x):
    n = x.shape[0]
    row = lax.broadcasted_iota(jnp.int32, x.shape, 0) % CH
    for s in (1, 2, 4, 8):
        x = x + jnp.where(row < CH - s, pltpu.roll(x, n - s, 0), 0.0)
    return x


def _chunk_sum(x):
    n, w = x.shape
    return jnp.sum(x.reshape(n // CH, CH, w), axis=1)


def _chunk_bcast(c, n):
    nch, w = c.shape
    return jnp.broadcast_to(c[:, None, :], (nch, CH, w)).reshape(n, w)


def _lower_bound(lb_logits, layer):
    lg = lb_logits
    e = jnp.exp(lg - jnp.max(lg, axis=0, keepdims=True))
    p = e / jnp.sum(e, axis=0, keepdims=True)
    row = lax.broadcasted_iota(jnp.int32, p.shape, 0)
    lb = jnp.sum(jnp.where((row >= 1) & (row <= layer), p, 0.0), axis=0, keepdims=True)
    return lb, p


def _hgrn_prep(q, fl, lb):
    n = q.shape[0]
    sg = _sigmoid(fl)
    f = lb + (1.0 - lb) * sg
    lf = jnp.log(f)
    k = 1.0 - f
    sq = _sigmoid(q)
    qf = q * sq
    g = _chunk_cumsum(lf)
    tot = _chunk_sum(lf)
    gl = _chunk_bcast(tot, n)
    eg = jnp.exp(g)
    eng = jnp.exp(-g)
    egl = jnp.exp(gl - g)
    return dict(sg=sg, f=f, k=k, sq=sq, qf=qf, eg=eg, eng=eng, egl=egl,
                qd=qf * eg, ki=k * eng, ke=k * egl, dch=jnp.exp(tot))


def _hgrn_mask():
    r = lax.broadcasted_iota(jnp.int32, (HD, HD), 0)
    c = lax.broadcasted_iota(jnp.int32, (HD, HD), 1)
    return (r // CH == c // CH) & (c <= r)


def _hgrn_states(h, v3, ke3, dch_h, st_in, sts_s, ut_s, dch_s):
    nch = v3.shape[0]
    ut_s[...] = _bmm(v3, ke3, 1, 1)
    dch_s[...] = dch_h

    def step(c, st):
        sts_s[c] = st
        return st * dch_s[pl.ds(c, 1), :] + ut_s[c]

    return lax.fori_loop(0, nch, step, st_in)


def _hgrn_fwd(q, fl, inp, lb, ng, st_ref, sts_s, ut_s, dch_s):
    n = q.shape[0]
    nch = n // CH
    pr = _hgrn_prep(q, fl, lb)
    mask = _hgrn_mask()
    toks = []
    for h in range(NH):
        sl = slice(h * HD, (h + 1) * HD)
        qd, ki, ke, v = pr["qd"][:, sl], pr["ki"][:, sl], pr["ke"][:, sl], inp[:, sl]
        qd3 = qd.astype(MXU).reshape(nch, CH, HD)
        v3 = v.astype(MXU).reshape(nch, CH, HD)
        ke3 = ke.astype(MXU).reshape(nch, CH, HD)
        st_ref[h] = _hgrn_states(h, v3, ke3, pr["dch"][:, sl], st_ref[h], sts_s, ut_s, dch_s)
        o = _bmm(qd3, sts_s[...], 2, 2).reshape(n, HD)
        intra = []
        for b in range(n // HD):
            bs = slice(b * HD, (b + 1) * HD)
            a = jnp.where(mask, _nt(qd[bs], ki[bs]), 0.0)
            intra.append(_nn(a, v[bs]))
        o = o + jnp.concatenate(intra, axis=0)
        r = lax.rsqrt(jnp.mean(o * o, axis=-1, keepdims=True) + RMS_EPS)
        toks.append(o * r * ng[:, sl])
    return jnp.concatenate(toks, axis=1)


def _hgrn_bwd(q, fl, inp, lb, ng, dtok, ststart_ref, dst_ref, sts_s, ut_s, dch_s, dstn_s, dng_ref, dlb_ref):
    n = q.shape[0]
    nch = n // CH
    pr = _hgrn_prep(q, fl, lb)
    mask = _hgrn_mask()
    dqd_l, dki_l, dke_l, dv_l, ddch_l, dng_l = [], [], [], [], [], []
    for h in range(NH):
        sl = slice(h * HD, (h + 1) * HD)
        qd, ki, ke, v = pr["qd"][:, sl], pr["ki"][:, sl], pr["ke"][:, sl], inp[:, sl]
        qd3 = qd.astype(MXU).reshape(nch, CH, HD)
        v3 = v.astype(MXU).reshape(nch, CH, HD)
        ke3 = ke.astype(MXU).reshape(nch, CH, HD)
        _hgrn_states(h, v3, ke3, pr["dch"][:, sl], ststart_ref[h], sts_s, ut_s, dch_s)
        sts = sts_s[...]
        o = _bmm(qd3, sts, 2, 2).reshape(n, HD)
        a_l = []
        intra = []
        for b in range(n // HD):
            bs = slice(b * HD, (b + 1) * HD)
            a = jnp.where(mask, _nt(qd[bs], ki[bs]), 0.0)
            a_l.append(a)
            intra.append(_nn(a, v[bs]))
        o = o + jnp.concatenate(intra, axis=0)
        r = lax.rsqrt(jnp.mean(o * o, axis=-1, keepdims=True) + RMS_EPS)
        dt = dtok[:, sl]
        dng_l.append(_rowsum(dt * o * r))
        dn = dt * ng[:, sl]
        do = r * dn - o * (r * r * r) * jnp.mean(dn * o, axis=-1, keepdims=True)
        do3 = do.astype(MXU).reshape(nch, CH, HD)
        dqd_rows, dki_rows, dv_rows = [], [], []
        for b in range(n // HD):
            bs = slice(b * HD, (b + 1) * HD)
            da = jnp.where(mask, _nt(do[bs], v[bs]), 0.0)
            dqd_rows.append(_nn(da, ki[bs]))
            dki_rows.append(_tn(da, qd[bs]))
            dv_rows.append(_tn(a_l[b], do[bs]))
        dqd = jnp.concatenate(dqd_rows, axis=0) + _bmm(do3, sts, 2, 1).reshape(n, HD)
        dki = jnp.concatenate(dki_rows, axis=0)
        dv = jnp.concatenate(dv_rows, axis=0)
        ut_s[...] = _bmm(do3, qd3, 1, 1)

        def step(i, dst):
            c = nch - 1 - i
            dstn_s[c] = dst
            return ut_s[c] + dst * dch_s[pl.ds(c, 1), :]

        dst_ref[h] = lax.fori_loop(0, nch, step, dst_ref[h])
        dstn = dstn_s[...]
        dv = dv + _bmm(ke3, dstn, 2, 2).reshape(n, HD)
        dke = _bmm(v3, dstn, 2, 1).reshape(n, HD)
        ddch_l.append(jnp.sum(sts * dstn, axis=1))
        dqd_l.append(dqd)
        dki_l.append(dki)
        dke_l.append(dke)
        dv_l.append(dv)
    dqd = jnp.concatenate(dqd_l, axis=1)
    dki = jnp.concatenate(dki_l, axis=1)
    dke = jnp.concatenate(dke_l, axis=1)
    dinp = jnp.concatenate(dv_l, axis=1)
    ddch = jnp.concatenate(ddch_l, axis=1)
    _acc(dng_ref, jnp.concatenate(dng_l, axis=1))
    dqf = dqd * pr["eg"]
    dke_ke = dke * pr["ke"]
    dg = dqd * pr["qd"] - dki * pr["ki"] - dke_ke
    dk = dki * pr["eng"] + dke * pr["egl"]
    dgl = _chunk_sum(dke_ke) + ddch * pr["dch"]
    dlf = _chunk_revcumsum(dg) + _chunk_bcast(dgl, n)
    df = dlf / pr["f"] - dk
    sg = pr["sg"]
    dfl = df * (1.0 - lb) * sg * (1.0 - sg)
    _acc(dlb_ref, _rowsum(df * (1.0 - sg)))
    sq = pr["sq"]
    dq = dqf * (sq * (1.0 + q * (1.0 - sq)))
    return dq, dfl, dinp


def _pool_select(s2, s4, s8, s16):
    col = lax.broadcasted_iota(jnp.int32, (1, TOK), 1)
    return jnp.where(col < POOL_GROUP, s2, jnp.where(col < 2 * POOL_GROUP, s4, jnp.where(col < 3 * POOL_GROUP, s8, s16)))


def _pool_cnt(pos0, n):
    pos = pos0 + lax.broadcasted_iota(jnp.int32, (n, TOK), 0) + 1
    col = lax.broadcasted_iota(jnp.int32, (n, TOK), 1)
    w = jnp.where(col < POOL_GROUP, 2, jnp.where(col < 2 * POOL_GROUP, 4, jnp.where(col < 3 * POOL_GROUP, 8, 16)))
    return jnp.minimum(pos, w).astype(F32)


def _pool_fwd(p, halo, pos0, wbd, scale):
    n = p.shape[0]
    ext = jnp.concatenate([halo, p], axis=0)
    s2 = ext + pltpu.roll(ext, 1, 0)
    s4 = s2 + pltpu.roll(s2, 2, 0)
    s8 = s4 + pltpu.roll(s4, 4, 0)
    s16 = s8 + pltpu.roll(s8, 8, 0)
    win = _pool_select(s2, s4, s8, s16)[16:]
    cnt = _pool_cnt(pos0, n)
    diff = win / cnt - p
    y = _nn(diff, wbd)
    return y * scale, (diff, y, cnt)


def _pool_bwd(fres, dtok, nxt_ref, wbd, scale, dwbd_ref, dscale_ref):
    diff, y, cnt = fres
    n = diff.shape[0]
    _acc(dscale_ref, _rowsum(dtok * y))
    dy = dtok * scale
    ddiff = _nt(dy, wbd)
    dwbd_ref[...] += _tn(diff, dy)
    qv = ddiff / cnt
    ext = jnp.concatenate([qv, nxt_ref[...]], axis=0)
    m = n + 16
    s2 = ext + pltpu.roll(ext, m - 1, 0)
    s4 = s2 + pltpu.roll(s2, m - 2, 0)
    s8 = s4 + pltpu.roll(s4, m - 4, 0)
    s16 = s8 + pltpu.roll(s8, m - 8, 0)
    adj = _pool_select(s2, s4, s8, s16)[:n]
    nxt_ref[...] = qv[:16]
    return adj - ddiff


def _neg_expm1(x):
    return jnp.where(jnp.abs(x) < 1e-2, -x * (1.0 + x * (0.5 + x * (1.0 / 6.0))), 1.0 - jnp.exp(x))


def _lru_gates(xc, zx, za, ap, first):
    gx = _sigmoid(zx)
    ga = _sigmoid(za)
    sp = jnp.maximum(-ap, 0.0) + jnp.log(1.0 + jnp.exp(-jnp.abs(ap)))
    log_a = -LRU_C * ga * sp
    a = jnp.exp(log_a)
    mult = jnp.sqrt(_neg_expm1(2.0 * log_a))
    mult = jnp.where(first, 1.0, mult)
    return a, mult * gx * xc


def _scan_fwd(a, b, h0):
    n = a.shape[0]
    row = lax.broadcasted_iota(jnp.int32, a.shape, 0)
    s = 1
    while s < n:
        keep = row >= s
        b = b + a * jnp.where(keep, pltpu.roll(b, s, 0), 0.0)
        a = a * jnp.where(keep, pltpu.roll(a, s, 0), 1.0)
        s *= 2
    return b + a * h0


def _scan_bwd(an, d, dh_next):
    n = an.shape[0]
    row = lax.broadcasted_iota(jnp.int32, an.shape, 0)
    s = 1
    while s < n:
        keep = row < n - s
        d = d + an * jnp.where(keep, pltpu.roll(d, n - s, 0), 0.0)
        an = an * jnp.where(keep, pltpu.roll(an, n - s, 0), 1.0)
        s *= 2
    return d + an * dh_next


def _lru_conv(xb, halo, cw_ref, cb):
    ext = jnp.concatenate([halo, xb], axis=0)
    sh = [pltpu.roll(ext, 3 - j, 0)[8:] if j < 3 else xb for j in range(4)]
    xc = cb
    for j in range(4):
        xc = xc + cw_ref[pl.ds(j, 1), :] * sh[j]
    return xc, sh


def _lru_fwd(xb, halo, pos0, prm, h0):
    cw, cb, wgx, bgx, wga, bga, ap = prm
    n = xb.shape[0]
    xc, sh = _lru_conv(xb, halo, cw, cb[...])
    zx = jnp.concatenate([_nn(xc[:, h * HD:(h + 1) * HD], wgx[h]) for h in range(NH)], axis=1) + bgx[...]
    za = jnp.concatenate([_nn(xc[:, h * HD:(h + 1) * HD], wga[h]) for h in range(NH)], axis=1) + bga[...]
    first = (pos0 + lax.broadcasted_iota(jnp.int32, (n, 1), 0)) == 0
    a, b = _lru_gates(xc, zx, za, ap[...], first)
    hseq = _scan_fwd(a, b, h0)
    return hseq, (xc, sh, zx, za, first, a)


def _lru_bwd(fres, hseq, h0, dtok, prm, carry_refs, grad_refs):
    cw, cb, wgx, bgx, wga, bga, ap = prm
    xc, sh, zx, za, first, a = fres
    anext_ref, dhnext_ref, dxcnext_ref = carry_refs
    dcw_ref, dcb_ref, dwgx_ref, dbgx_ref, dwga_ref, dbga_ref, dap_ref = grad_refs
    n = xc.shape[0]
    an = jnp.where(lax.broadcasted_iota(jnp.int32, a.shape, 0) == n - 1, anext_ref[...], pltpu.roll(a, n - 1, 0))
    dh = _scan_bwd(an, dtok, dhnext_ref[...])
    hprev = jnp.where(lax.broadcasted_iota(jnp.int32, hseq.shape, 0) == 0, h0, pltpu.roll(hseq, 1, 0))
    da = dh * hprev
    anext_ref[...] = _row(a, 0)
    dhnext_ref[...] = _row(dh, 0)
    _, vjp = jax.vjp(lambda xc_, zx_, za_, ap_: _lru_gates(xc_, zx_, za_, ap_, first), xc, zx, za, ap[...])
    dxc, dzx, dza, dap = vjp((da, dh))
    _acc(dap_ref, dap)
    _acc(dbgx_ref, _rowsum(dzx))
    _acc(dbga_ref, _rowsum(dza))
    parts = []
    for h in range(NH):
        sl = slice(h * HD, (h + 1) * HD)
        parts.append(_nt(dzx[:, sl], wgx[h]) + _nt(dza[:, sl], wga[h]))
        dwgx_ref[h] += _tn(xc[:, sl], dzx[:, sl])
        dwga_ref[h] += _tn(xc[:, sl], dza[:, sl])
    dxc = dxc + jnp.concatenate(parts, axis=1)
    _acc(dcb_ref, _rowsum(dxc))
    for j in range(4):
        dcw_ref[pl.ds(j, 1), :] += _rowsum(dxc * sh[j])
    ext = jnp.concatenate([dxc, dxcnext_ref[...]], axis=0)
    m = n + 8
    dxb = cw[pl.ds(3, 1), :] * dxc
    for j in range(3):
        dxb = dxb + cw[pl.ds(j, 1), :] * pltpu.roll(ext, m - (3 - j), 0)[:n]
    dxcnext_ref[...] = dxc[:8]
    return dxb


def _mixer_prm_count(kind):
    return {0: 2, 1: 2, 2: 2, 3: 7}[kind]


def _layer_fwd(kind, layer, xprev, gprev, bprev, wt, wout, ks, vs, prm):
    S = xprev.shape[0]
    nt = S // TS
    N = wt.shape[0]
    nprm = len(prm)
    nch = TS // CH

    outs = [(jax.ShapeDtypeStruct((S, N), F32), _rows(N, nt, False)),
            (jax.ShapeDtypeStruct((S, D), F32), _rows(D, nt, False)),
            (jax.ShapeDtypeStruct((S, 1), F32), _rows(1, nt, False))]
    scratch = []
    if kind == 1:
        outs.append((jax.ShapeDtypeStruct((nt, NH, HD, HD), F32), pl.BlockSpec((None, NH, HD, HD), lambda i: (i, 0, 0, 0))))
        scratch = [pltpu.VMEM((NH, HD, HD), F32), pltpu.VMEM((nch, HD, HD), F32), pltpu.VMEM((nch, HD, HD), F32),
                   pltpu.VMEM((nch, HD), F32)]
    elif kind == 2:
        scratch = [pltpu.VMEM((16, TOK), F32)]
    elif kind == 3:
        outs.append((jax.ShapeDtypeStruct((nt * 8, TOK), F32), pl.BlockSpec((8, TOK), lambda i: (i, 0))))
        scratch = [pltpu.VMEM((8, TOK), F32), pltpu.VMEM((1, TOK), F32)]
    nout = len(outs)

    def body(*refs):
        x_ref, g_ref, b_ref, wt_ref, wout_ref, ks_ref, vs_ref = refs[:7]
        prm_refs = refs[7:7 + nprm]
        out_refs = refs[7 + nprm:7 + nprm + nout]
        scr = refs[7 + nprm + nout:]
        proj_ref, xhat_ref, rstd_ref = out_refs[:3]
        i = pl.program_id(0)
        xin = x_ref[...] * g_ref[...] + b_ref[...]
        proj = _nt(xin, wt_ref[...])
        proj_ref[...] = proj
        if kind == 0:
            tok, _ = _gmlp_fwd(proj[:, :TOK], proj[:, TOK:2 * TOK], prm_refs[0], prm_refs[1])
        elif kind == 1:
            st_ref, sts_s, ut_s, dch_s = scr

            @pl.when(i == 0)
            def _():
                st_ref[...] = jnp.zeros_like(st_ref)

            out_refs[3][...] = st_ref[...]
            lb, _ = _lower_bound(prm_refs[0][...], layer)
            tok = _hgrn_fwd(proj[:, :TOK], proj[:, TOK:2 * TOK], proj[:, 2 * TOK:3 * TOK], lb, prm_refs[1][...],
                            st_ref, sts_s, ut_s, dch_s)
        elif kind == 2:
            halo_ref, = scr

            @pl.when(i == 0)
            def _():
                halo_ref[...] = jnp.zeros_like(halo_ref)

            p = proj[:, :TOK]
            tok, _ = _pool_fwd(p, halo_ref[...], i * TS, prm_refs[0][...], prm_refs[1][...])
            halo_ref[...] = p[TS - 16:]
        else:
            halo_ref, h_ref = scr

            @pl.when(i == 0)
            def _():
                halo_ref[...] = jnp.zeros_like(halo_ref)
                h_ref[...] = jnp.zeros_like(h_ref)

            out_refs[3][...] = jnp.broadcast_to(h_ref[...], (8, TOK))
            xb = proj[:, :TOK]
            tok, _ = _lru_fwd(xb, halo_ref[...], i * TS, prm_refs, h_ref[...])
            halo_ref[...] = xb[TS - 8:]
            h_ref[...] = _row(tok, TS - 1)
        qx = proj[:, N - D - XW:N - D]
        gate = proj[:, N - D:]
        xo, _ = _xattn_fwd(qx, ks_ref, vs_ref)
        mixed = jnp.concatenate([tok, xo], axis=1) * (gate * _sigmoid(gate))
        z = ALPHA * xin + _nn(mixed, wout_ref[...])
        cen = z - jnp.mean(z, axis=-1, keepdims=True)
        rstd = lax.rsqrt(jnp.mean(cen * cen, axis=-1, keepdims=True) + LN_EPS)
        xhat_ref[...] = cen * rstd
        rstd_ref[...] = rstd

    ins = [(xprev, _rows(D, nt, False)), (gprev, _res(gprev)), (bprev, _res(bprev)), (wt, _res(wt)), (wout, _res(wout)),
           (ks, _res(ks)), (vs, _res(vs))] + [(p, _res(p)) for p in prm]
    return _call(body, f"layer{layer}_fwd", (nt,), ins, outs, scratch)


def _layer_bwd(kind, layer, up, is_last, xhat, rstd, g_i, b_i, proj, wout, ks, vs, prm, extra):
    S = xhat.shape[0]
    nt = S // TS
    N = proj.shape[1]
    nprm = len(prm)
    nch = TS // CH

    ins = [(up, _rows(D, nt, True)), (xhat, _rows(D, nt, True)), (rstd, _rows(1, nt, True)), (g_i, _res(g_i)), (b_i, _res(b_i)),
           (proj, _rows(N, nt, True)), (wout, _res(wout)), (ks, _res(ks)), (vs, _res(vs))] + [(p, _res(p)) for p in prm]
    nfixed = 9
    if kind == 1:
        ins.append((extra, pl.BlockSpec((None, NH, HD, HD), lambda i: (nt - 1 - i, 0, 0, 0))))
    elif kind == 2:
        hb = TS // 16
        ins.append((proj, pl.BlockSpec((16, TOK), lambda i: (jnp.maximum((nt - 1 - i) * hb - 1, 0), 0))))
    elif kind == 3:
        hb = TS // 8
        ins.append((proj, pl.BlockSpec((8, TOK), lambda i: (jnp.maximum((nt - 1 - i) * hb - 1, 0), 0))))
        ins.append((extra, pl.BlockSpec((8, TOK), lambda i: (nt - 1 - i, 0))))
    nin = len(ins)

    def acc(shape):
        return (jax.ShapeDtypeStruct(shape, F32), _res_sds(shape))

    outs = [(jax.ShapeDtypeStruct((S, D), F32), _rows(D, nt, True)),
            (jax.ShapeDtypeStruct((S, N), MXU), _rows(N, nt, True)),
            acc((D, D)), acc((XHEADS, NMEM, XW)), acc((XHEADS, NMEM, XW)), acc((1, D)), acc((1, D)), acc((1, HD))]
    scratch = []
    if kind == 0:
        outs += [acc((NH, HD, HD)), acc((NH, HD, HD))]
    elif kind == 1:
        outs += [acc((1, TOK)), acc((1, TOK))]
        scratch = [pltpu.VMEM((NH, HD, HD), F32), pltpu.VMEM((nch, HD, HD), F32), pltpu.VMEM((nch, HD, HD), F32),
                   pltpu.VMEM((nch, HD), F32), pltpu.VMEM((nch, HD, HD), F32)]
    elif kind == 2:
        outs += [acc((TOK, TOK)), acc((1, TOK))]
        scratch = [pltpu.VMEM((16, TOK), F32)]
    else:
        outs += [acc((4, TOK)), acc((1, TOK)), acc((NH, HD, HD)), acc((1, TOK)), acc((NH, HD, HD)), acc((1, TOK)), acc((1, TOK))]
        scratch = [pltpu.VMEM((1, TOK), F32), pltpu.VMEM((1, TOK), F32), pltpu.VMEM((8, TOK), F32)]
    nout = len(outs)

    def body(*refs):
        up_ref, xhat_ref, rstd_ref, g_ref, b_ref, proj_ref, wout_ref, ks_ref, vs_ref = refs[:nfixed]
        prm_refs = refs[nfixed:nfixed + nprm]
        ext_refs = refs[nfixed + nprm:nin]
        out_refs = refs[nin:nin + nout]
        scr = refs[nin + nout:]
        dres_ref, dproj_ref, dwout_ref, dks_ref, dvs_ref, dg_ref, db_ref, loss_ref = out_refs[:8]
        pgrad = out_refs[8:]
        i = pl.program_id(0)
        tile = nt - 1 - i

        @pl.when(i == 0)
        def _():
            for r in out_refs[2:]:
                r[...] = jnp.zeros_like(r)
            for r in scr:
                if kind != 1 or r is scr[0]:
                    r[...] = jnp.zeros_like(r)

        xhat_v = xhat_ref[...]
        if is_last:
            err = xhat_v * g_ref[...] + b_ref[...] - up_ref[...]
            dxo = err * (1.0 / D)
            loss_ref[...] += jnp.sum(0.5 * jnp.mean(err * err, axis=-1, keepdims=True), axis=0, keepdims=True)
        else:
            dxo = up_ref[...]
        _acc(dg_ref, _rowsum(dxo * xhat_v))
        _acc(db_ref, _rowsum(dxo))
        dxh = dxo * g_ref[...]
        dz = rstd_ref[...] * (dxh - jnp.mean(dxh, axis=-1, keepdims=True)
                              - xhat_v * jnp.mean(dxh * xhat_v, axis=-1, keepdims=True))
        dres_ref[...] = ALPHA * dz

        proj = proj_ref[...]
        qx = proj[:, N - D - XW:N - D]
        gate = proj[:, N - D:]
        if kind == 0:
            u, v = proj[:, :TOK], proj[:, TOK:2 * TOK]
            tok, fres = _gmlp_fwd(u, v, prm_refs[0], prm_refs[1])
        elif kind == 1:
            dst_ref, sts_s, ut_s, dch_s, dstn_s = scr
            lb, _ = _lower_bound(prm_refs[0][...], layer)
            ng = prm_refs[1][...]
            q, fl, inp = proj[:, :TOK], proj[:, TOK:2 * TOK], proj[:, 2 * TOK:3 * TOK]
            scr_st = ext_refs[0]
            tok = _hgrn_fwd_from(q, fl, inp, lb, ng, scr_st, sts_s, ut_s, dch_s)
        elif kind == 2:
            p = proj[:, :TOK]
            halo = jnp.where(tile == 0, 0.0, ext_refs[0][...])
            tok, fres = _pool_fwd(p, halo, tile * TS, prm_refs[0][...], prm_refs[1][...])
        else:
            xb = proj[:, :TOK]
            halo = jnp.where(tile == 0, 0.0, ext_refs[0][...])
            h0 = ext_refs[1][0:1]
            tok, fres = _lru_fwd(xb, halo, tile * TS, prm_refs, h0)
        xo, ps = _xattn_fwd(qx, ks_ref, vs_ref)
        sgate = _sigmoid(gate)
        silu = gate * sgate
        cat = jnp.concatenate([tok, xo], axis=1)
        mixed = cat * silu
        dwout_ref[...] += _tn(mixed, dz)
        dmixed = _nt(dz, wout_ref[...])
        dcat = dmixed * silu
        dgate = dmixed * cat * (sgate * (1.0 + gate * (1.0 - sgate)))
        dtok = dcat[:, :TOK]
        dqx = _xattn_bwd(qx, ps, dcat[:, TOK:], ks_ref, vs_ref, dks_ref, dvs_ref)
        if kind == 0:
            du, dv = _gmlp_bwd(u, v, fres, dtok, pgrad[0], pgrad[1])
            dproj_ref[:, :TOK] = du.astype(MXU)
            dproj_ref[:, TOK:2 * TOK] = dv.astype(MXU)
        elif kind == 1:
            dq, dfl, dinp = _hgrn_bwd(q, fl, inp, lb, ng, dtok, scr_st, dst_ref, sts_s, ut_s, dch_s, dstn_s,
                                      pgrad[1], pgrad[0])
            dproj_ref[:, :TOK] = dq.astype(MXU)
            dproj_ref[:, TOK:2 * TOK] = dfl.astype(MXU)
            dproj_ref[:, 2 * TOK:3 * TOK] = dinp.astype(MXU)
        elif kind == 2:
            dp = _pool_bwd(fres, dtok, scr[0], prm_refs[0][...], prm_refs[1][...], pgrad[0], pgrad[1])
            dproj_ref[:, :TOK] = dp.astype(MXU)
        else:
            dxb = _lru_bwd(fres, tok, h0, dtok, prm_refs, scr, pgrad)
            dproj_ref[:, :TOK] = dxb.astype(MXU)
        dproj_ref[:, N - D - XW:N - D] = dqx.astype(MXU)
        dproj_ref[:, N - D:] = dgate.astype(MXU)

    return _call(body, f"layer{layer}_bwd", (nt,), ins, outs, scratch)


def _hgrn_fwd_from(q, fl, inp, lb, ng, ststart_ref, sts_s, ut_s, dch_s):
    n = q.shape[0]
    nch = n // CH
    pr = _hgrn_prep(q, fl, lb)
    mask = _hgrn_mask()
    toks = []
    for h in range(NH):
        sl = slice(h * HD, (h + 1) * HD)
        qd, ki, ke, v = pr["qd"][:, sl], pr["ki"][:, sl], pr["ke"][:, sl], inp[:, sl]
        qd3 = qd.astype(MXU).reshape(nch, CH, HD)
        v3 = v.astype(MXU).reshape(nch, CH, HD)
        ke3 = ke.astype(MXU).reshape(nch, CH, HD)
        _hgrn_states(h, v3, ke3, pr["dch"][:, sl], ststart_ref[h], sts_s, ut_s, dch_s)
        o = _bmm(qd3, sts_s[...], 2, 2).reshape(n, HD)
        intra = []
        for b in range(n // HD):
            bs = slice(b * HD, (b + 1) * HD)
            a = jnp.where(mask, _nt(qd[bs], ki[bs]), 0.0)
            intra.append(_nn(a, v[bs]))
        o = o + jnp.concatenate(intra, axis=0)
        r = lax.rsqrt(jnp.mean(o * o, axis=-1, keepdims=True) + RMS_EPS)
        toks.append(o * r * ng[:, sl])
    return jnp.concatenate(toks, axis=1)


def _proj_bwd(layer, dproj, dres, xprev, gprev, bprev, wt):
    S = xprev.shape[0]
    nt = S // TSB
    N = wt.shape[0]

    def body(dproj_ref, dres_ref, x_ref, g_ref, b_ref, wt_ref, dx_ref, dwt_ref):
        @pl.when(pl.program_id(0) == 0)
        def _():
            dwt_ref[...] = jnp.zeros_like(dwt_ref)

        dp = dproj_ref[...]
        xin = x_ref[...] * g_ref[...] + b_ref[...]
        dx_ref[...] = dres_ref[...] + _nn(dp, wt_ref[...])
        dwt_ref[...] += _tn(dp, xin)

    ins = [(dproj, _rows(N, nt, False, TSB)), (dres, _rows(D, nt, False, TSB)), (xprev, _rows(D, nt, False, TSB)),
           (gprev, _res(gprev)), (bprev, _res(bprev)), (wt, _res(wt))]
    outs = [(jax.ShapeDtypeStruct((S, D), F32), _rows(D, nt, False, TSB)),
            (jax.ShapeDtypeStruct((N, D), F32), _res_sds((N, D)))]
    return _call(body, f"layer{layer}_projbwd", (nt,), ins, outs)


def _head_mask(h):
    col = lax.broadcasted_iota(jnp.int32, (1, XW), 1)
    return (col // 64) == h


def _kv_fwd(mem, wkv):
    def body(mem_ref, w_ref, ks_ref, vs_ref):
        kv = _nn(mem_ref[...], w_ref[...])
        k, v = kv[:, :XW], kv[:, XW:]
        for h in range(XHEADS):
            ks_ref[h] = jnp.where(_head_mask(h), k, 0.0).astype(MXU)
            vs_ref[h] = jnp.where(_head_mask(h), v, 0.0).astype(MXU)

    sds = jax.ShapeDtypeStruct((XHEADS, NMEM, XW), MXU)
    return pl.pallas_call(body, name="kv_fwd", out_shape=(sds, sds), compiler_params=_cparams())(mem, wkv)


def _kv_bwd(mem, dks_l, dvs_l):
    def body(mem_ref, *refs):
        dks_refs, dvs_refs, out_ref = refs[:DEPTH], refs[DEPTH:2 * DEPTH], refs[2 * DEPTH]
        dk = jnp.zeros((NMEM, XW), F32)
        dv = jnp.zeros((NMEM, XW), F32)
        for h in range(XHEADS):
            m = _head_mask(h)
            for l in range(DEPTH):
                dk = dk + jnp.where(m, dks_refs[l][h], 0.0)
                dv = dv + jnp.where(m, dvs_refs[l][h], 0.0)
        out_ref[...] = _tn(mem_ref[...], jnp.concatenate([dk, dv], axis=1))

    return pl.pallas_call(body, name="kv_bwd", out_shape=jax.ShapeDtypeStruct((D, 2 * XW), F32),
                          compiler_params=_cparams())(mem, *dks_l, *dvs_l)


def _prep_weights(w_ins, w_out, wkv):
    def body(a_ref, b_ref, c_ref, d_ref, wo_ref, kv_ref, ao, bo, co, do, woo, kvo):
        for src, dst in ((a_ref, ao), (b_ref, bo), (c_ref, co), (d_ref, do)):
            dst[...] = src[...].T.astype(MXU)
        woo[...] = wo_ref[...].astype(MXU)
        kvo[...] = kv_ref[...].astype(MXU)

    outs = [jax.ShapeDtypeStruct((w.shape[1], w.shape[0]), MXU) for w in w_ins]
    outs += [jax.ShapeDtypeStruct(w_out.shape, MXU), jax.ShapeDtypeStruct(wkv.shape, MXU)]
    return pl.pallas_call(body, name="prep_weights", out_shape=outs, compiler_params=_cparams())(*w_ins, w_out, wkv)


def _adam_math(w, g, m, v):
    m = B1 * m + (1.0 - B1) * g
    v = B2 * v + (1.0 - B2) * (g * g)
    m_hat = m / (1.0 - B1 ** STEP)
    v_hat = v / (1.0 - B2 ** STEP)
    delta = -LR * (m_hat / (jnp.sqrt(v_hat) + EPS) + WD * w)
    return delta, m, v


def _sum_adam(name, recv, w, m, v, transpose):
    rows, cols = recv.shape[1], recv.shape[2]

    def body(r_ref, w_ref, m_ref, v_ref, g_out, d_out, m_out, v_out, acc_ref):
        s = pl.program_id(0)

        @pl.when(s == 0)
        def _():
            acc_ref[...] = r_ref[...]

        @pl.when(s > 0)
        def _():
            acc_ref[...] += r_ref[...]

        @pl.when(s == NDEV - 1)
        def _():
            g = acc_ref[...].T if transpose else acc_ref[...]
            d, mn, vn = _adam_math(w_ref[...], g, m_ref[...], v_ref[...])
            g_out[...] = g
            d_out[...] = d
            m_out[...] = mn
            v_out[...] = vn

    sds = jax.ShapeDtypeStruct(w.shape, F32)
    ins = [(recv, pl.BlockSpec((None, rows, cols), lambda s: (s, 0, 0))), (w, _res(w)), (m, _res(m)), (v, _res(v))]
    outs = [(sds, _res_sds(w.shape))] * 4
    return _call(body, name, (NDEV,), ins, outs, [pltpu.VMEM((rows, cols), F32)])


def _small_finalize(dbs_exp, dlb, lb_logits):
    def body(dbs_ref, dlb_ref, lg_ref, dabs_ref, dlg_ref):
        dabs_ref[...] = jnp.sum(dbs_ref[...], axis=-1)
        total = jnp.zeros((DEPTH, TOK), F32)
        lg = lg_ref[...]
        e = jnp.exp(lg - jnp.max(lg, axis=0, keepdims=True))
        p = e / jnp.sum(e, axis=0, keepdims=True)
        row = lax.broadcasted_iota(jnp.int32, (DEPTH, TOK), 0)
        for layer in range(DEPTH):
            if layer % 4 != 1:
                continue
            dp = jnp.where((row >= 1) & (row <= layer), dlb_ref[...], 0.0)
            total = total + p * (dp - jnp.sum(p * dp, axis=0, keepdims=True))
        dlg_ref[...] = total

    return pl.pallas_call(body, name="small_finalize",
                          out_shape=(jax.ShapeDtypeStruct((NH, HD), F32), jax.ShapeDtypeStruct((DEPTH, TOK), F32)),
                          compiler_params=_cparams())(dbs_exp, dlb, lb_logits)


def _small_sum_adam(gathered, w, m, v):
    rows = w.shape[0]

    def body(r_ref, w_ref, m_ref, v_ref, g_out, d_out, m_out, v_out):
        g = r_ref[0]
        for s in range(1, NDEV):
            g = g + r_ref[s]
        d, mn, vn = _adam_math(w_ref[...], g, m_ref[...], v_ref[...])
        g_out[...] = g
        d_out[...] = d
        m_out[...] = mn
        v_out[...] = vn

    sds = jax.ShapeDtypeStruct((rows, 128), F32)
    return pl.pallas_call(body, name="small_sum_adam", out_shape=(sds,) * 4, compiler_params=_cparams())(gathered, w, m, v)


def _adam_only(g, w, m, v):
    def body(g_ref, w_ref, m_ref, v_ref, d_out, m_out, v_out):
        d, mn, vn = _adam_math(w_ref[...], g_ref[...], m_ref[...], v_ref[...])
        d_out[...] = d
        m_out[...] = mn
        v_out[...] = vn

    sds = jax.ShapeDtypeStruct(w.shape, F32)
    return pl.pallas_call(body, name="shard_adam", out_shape=(sds,) * 3, compiler_params=_cparams())(g, w, m, v)


def _me_and_peers():
    x, y, c = lax.axis_index("x"), lax.axis_index("y"), lax.axis_index("c")
    me = 4 * x + 2 * y + c
    peers = []
    for k in range(1, NDEV):
        kx, ky, kc = (k >> 2) & 1, (k >> 1) & 1, k & 1
        px = x + kx - 2 * x * kx
        py = y + ky - 2 * y * ky
        pc = c + kc - 2 * c * kc
        peers.append(((px, py, pc), 4 * px + 2 * py + pc))
    return me, peers


_ANY = pl.BlockSpec(memory_space=pl.ANY)


def _all_gather(shards):
    nt = len(shards)

    def body(*refs):
        src, dst = refs[:nt], refs[nt:2 * nt]
        send_sems, recv_sems, local_sems = refs[2 * nt:]
        me, peers = _me_and_peers()

        def slot(t, d):
            rows = shards[t].shape[0]
            return dst[t].at[pl.ds(d * rows, rows)]

        local = [pltpu.make_async_copy(src[t], slot(t, me), local_sems.at[t]) for t in range(nt)]
        for cp in local:
            cp.start()
        sends = []
        for k, (dev, _) in enumerate(peers):
            for t in range(nt):
                cp = pltpu.make_async_remote_copy(src_ref=src[t], dst_ref=slot(t, me), send_sem=send_sems.at[t, k],
                                                  recv_sem=recv_sems.at[t, k], device_id=dev, device_id_type=pl.DeviceIdType.MESH)
                cp.start()
                sends.append(cp)
        for k, (dev, idx) in enumerate(peers):
            for t in range(nt):
                pltpu.make_async_remote_copy(src_ref=src[t], dst_ref=slot(t, idx), send_sem=send_sems.at[t, k],
                                             recv_sem=recv_sems.at[t, k], device_id=dev,
                                             device_id_type=pl.DeviceIdType.MESH).wait_recv()
        for cp in sends:
            cp.wait_send()
        for cp in local:
            cp.wait()

    outs = [jax.ShapeDtypeStruct((NDEV * s.shape[0],) + s.shape[1:], s.dtype) for s in shards]
    return pl.pallas_call(
        body, name="all_gather", out_shape=outs, in_specs=[_ANY] * nt, out_specs=[_ANY] * nt,
        scratch_shapes=[pltpu.SemaphoreType.DMA((nt, NDEV - 1)), pltpu.SemaphoreType.DMA((nt, NDEV - 1)),
                        pltpu.SemaphoreType.DMA((nt,))],
    )(*shards)


def _reduce_scatter_send(fulls, npieces):
    nt = len(fulls)

    def body(*refs):
        src, dst = refs[:nt], refs[nt:2 * nt]
        send_sems, recv_sems, local_sems = refs[2 * nt:]
        me, peers = _me_and_peers()

        def block(t, d, p):
            if npieces[t] == "whole":
                return src[t]
            rows = fulls[t].shape[-2] // NDEV
            if npieces[t] is None:
                return src[t].at[pl.ds(d * rows, rows)]
            return src[t].at[p, pl.ds(d * rows, rows)]

        def slot(t, s, p):
            return dst[t].at[s, p] if isinstance(npieces[t], int) else dst[t].at[s]

        def pieces(t):
            return list(range(npieces[t])) if isinstance(npieces[t], int) else [None]

        local = [pltpu.make_async_copy(block(t, me, p), slot(t, me, p), local_sems.at[t]) for t in range(nt) for p in pieces(t)]
        for cp in local:
            cp.start()
        sends = []
        for k, (dev, idx) in enumerate(peers):
            for t in range(nt):
                for p in pieces(t):
                    cp = pltpu.make_async_remote_copy(src_ref=block(t, idx, p), dst_ref=slot(t, me, p),
                                                      send_sem=send_sems.at[t, k], recv_sem=recv_sems.at[t, k],
                                                      device_id=dev, device_id_type=pl.DeviceIdType.MESH)
                    cp.start()
                    sends.append(cp)
        for k, (dev, idx) in enumerate(peers):
            for t in range(nt):
                pltpu.make_async_remote_copy(src_ref=dst[t].at[idx], dst_ref=dst[t].at[idx], send_sem=send_sems.at[t, k],
                                             recv_sem=recv_sems.at[t, k], device_id=dev,
                                             device_id_type=pl.DeviceIdType.MESH).wait_recv()
        for k, (dev, idx) in enumerate(peers):
            for t in range(nt):
                pltpu.make_async_remote_copy(src_ref=dst[t].at[idx], dst_ref=dst[t].at[idx], send_sem=send_sems.at[t, k],
                                             recv_sem=recv_sems.at[t, k], device_id=dev,
                                             device_id_type=pl.DeviceIdType.MESH).wait_send()
        for t in range(nt):
            pltpu.make_async_copy(dst[t].at[me], dst[t].at[me], local_sems.at[t]).wait()

    outs = []
    for t, f in enumerate(fulls):
        rows = f.shape[-2] if npieces[t] == "whole" else f.shape[-2] // NDEV
        shape = (NDEV, npieces[t], rows, f.shape[-1]) if isinstance(npieces[t], int) else (NDEV, rows, f.shape[-1])
        outs.append(jax.ShapeDtypeStruct(shape, f.dtype))
    return pl.pallas_call(
        body, name="reduce_scatter", out_shape=outs, in_specs=[_ANY] * nt, out_specs=[_ANY] * nt,
        scratch_shapes=[pltpu.SemaphoreType.DMA((nt, NDEV - 1)), pltpu.SemaphoreType.DMA((nt, NDEV - 1)),
                        pltpu.SemaphoreType.DMA((nt,))],
    )(*fulls)


SMALL = [("ln_g", (DEPTH, D), False), ("ln_b", (DEPTH, D), False), ("hgrn_lb_logits", (DEPTH, TOK), False),
         ("a_w_s", (1, NH, HD, HD), False), ("a_b_s", (1, NH, HD), False), ("b_norm_g", (1, TOK), True),
         ("c_w_pool", (1, 4, POOL_GROUP, POOL_GROUP), False), ("c_scale", (1, TOK), True),
         ("d_conv_w", (1, 4, TOK), True), ("d_conv_b", (1, TOK), True),
         ("d_w_gx", (1, NH, HD, HD), False), ("d_b_gx", (1, NH, HD), False),
         ("d_w_ga", (1, NH, HD, HD), False), ("d_b_ga", (1, NH, HD), False), ("d_a_param", (1, TOK), True)]


def _pack(parts, total_rows):
    flat = jnp.concatenate([p.reshape(-1).astype(F32) for p in parts])
    flat = jnp.pad(flat, (0, total_rows * 128 - flat.shape[0]))
    return flat.reshape(total_rows, 128)


def _size(shape):
    n = 1
    for s in shape:
        n *= s
    return n


def _rows_for(n):
    return -(-n // 1024) * 8


def kernel(x, mem, mem_kv_w, ln_g, ln_b, w_out, hgrn_lb_logits, a_w_in, a_w_s, a_b_s, b_w_in, b_norm_g, c_w_in, c_w_pool, c_scale, d_w_in, d_conv_w, d_conv_b, d_w_gx, d_b_gx, d_w_ga, d_b_ga, d_a_param, loss_target, m_mem_kv_w, m_ln_g, m_ln_b, m_w_out, m_hgrn_lb_logits, m_a_w_in, m_a_w_s, m_a_b_s, m_b_w_in, m_b_norm_g, m_c_w_in, m_c_w_pool, m_c_scale, m_d_w_in, m_d_conv_w, m_d_conv_b, m_d_w_gx, m_d_b_gx, m_d_w_ga, m_d_b_ga, m_d_a_param, v_mem_kv_w, v_ln_g, v_ln_b, v_w_out, v_hgrn_lb_logits, v_a_w_in, v_a_w_s, v_a_b_s, v_b_w_in, v_b_norm_g, v_c_w_in, v_c_w_pool, v_c_scale, v_d_w_in, v_d_conv_w, v_d_conv_b, v_d_w_gx, v_d_b_gx, v_d_w_ga, v_d_b_ga, v_d_a_param):
    W = dict(mem_kv_w=mem_kv_w, ln_g=ln_g, ln_b=ln_b, w_out=w_out, hgrn_lb_logits=hgrn_lb_logits, a_w_in=a_w_in, a_w_s=a_w_s,
             a_b_s=a_b_s, b_w_in=b_w_in, b_norm_g=b_norm_g, c_w_in=c_w_in, c_w_pool=c_w_pool, c_scale=c_scale, d_w_in=d_w_in,
             d_conv_w=d_conv_w, d_conv_b=d_conv_b, d_w_gx=d_w_gx, d_b_gx=d_b_gx, d_w_ga=d_w_ga, d_b_ga=d_b_ga, d_a_param=d_a_param)
    M = dict(mem_kv_w=m_mem_kv_w, ln_g=m_ln_g, ln_b=m_ln_b, w_out=m_w_out, hgrn_lb_logits=m_hgrn_lb_logits, a_w_in=m_a_w_in,
             a_w_s=m_a_w_s, a_b_s=m_a_b_s, b_w_in=m_b_w_in, b_norm_g=m_b_norm_g, c_w_in=m_c_w_in, c_w_pool=m_c_w_pool,
             c_scale=m_c_scale, d_w_in=m_d_w_in, d_conv_w=m_d_conv_w, d_conv_b=m_d_conv_b, d_w_gx=m_d_w_gx, d_b_gx=m_d_b_gx,
             d_w_ga=m_d_w_ga, d_b_ga=m_d_b_ga, d_a_param=m_d_a_param)
    V = dict(mem_kv_w=v_mem_kv_w, ln_g=v_ln_g, ln_b=v_ln_b, w_out=v_w_out, hgrn_lb_logits=v_hgrn_lb_logits, a_w_in=v_a_w_in,
             a_w_s=v_a_w_s, a_b_s=v_a_b_s, b_w_in=v_b_w_in, b_norm_g=v_b_norm_g, c_w_in=v_c_w_in, c_w_pool=v_c_w_pool,
             c_scale=v_c_scale, d_w_in=v_d_w_in, d_conv_w=v_d_conv_w, d_conv_b=v_d_conv_b, d_w_gx=v_d_w_gx, d_b_gx=v_d_b_gx,
             d_w_ga=v_d_w_ga, d_b_ga=v_d_b_ga, d_a_param=v_d_a_param)
    me = 4 * lax.axis_index("x") + 2 * lax.axis_index("y") + lax.axis_index("c")
    x2, mem2, tgt2 = x[0], mem[0], loss_target[0]
    in_names = ["a_w_in", "b_w_in", "c_w_in", "d_w_in"]

    shard_names = [n for n, _, sh in SMALL if sh]
    small_shard = _pack([W[n] for n in shard_names], 8)
    wts = _prep_weights([W[n][0] for n in in_names], w_out.reshape(DEPTH * (D // NDEV), D), mem_kv_w)
    gathered = _all_gather(list(wts) + [small_shard])
    wt_full = gathered[:4]
    wout_full = gathered[4].reshape(NDEV, DEPTH, D // NDEV, D).transpose(1, 0, 2, 3).reshape(DEPTH, D, D)
    wkv_full = gathered[5]
    sm = gathered[6].reshape(NDEV, 1024)
    full_small = {}
    off = 0
    for n, shape, _ in [s for s in SMALL if s[2]]:
        per = _size(shape) // NDEV
        blk = sm[:, off:off + per]
        if n == "d_conv_w":
            full_small[n] = blk.reshape(NDEV, 4, TOK // NDEV).transpose(1, 0, 2).reshape(4, TOK)
        else:
            full_small[n] = blk.reshape(1, TOK)
        off += per

    ks, vs = _kv_fwd(mem2, wkv_full)
    tri_bs = jnp.broadcast_to(a_b_s[0][:, :, None], (NH, HD, HD))
    wbd = jnp.zeros((TOK, TOK), F32)
    for g in range(4):
        wbd = lax.dynamic_update_slice(wbd, c_w_pool[0, g], (g * POOL_GROUP, g * POOL_GROUP))
    wbd = wbd.astype(MXU)
    prm = {0: [a_w_s[0], tri_bs],
           1: [hgrn_lb_logits, full_small["b_norm_g"]],
           2: [wbd, full_small["c_scale"]],
           3: [full_small["d_conv_w"], full_small["d_conv_b"], d_w_gx[0].astype(MXU), d_b_gx[0].reshape(1, TOK),
               d_w_ga[0].astype(MXU), d_b_ga[0].reshape(1, TOK), full_small["d_a_param"]]}
    ones = jnp.ones((1, D), F32)
    zeros = jnp.zeros((1, D), F32)
    xs, gs, bs = [x2], [ones], [zeros]
    saved = []
    for i in range(DEPTH):
        res = _layer_fwd(i, i, xs[i], gs[i], bs[i], wt_full[i], wout_full[i], ks, vs, prm[i])
        saved.append(res)
        xs.append(res[1])
        gs.append(ln_g[i:i + 1])
        bs.append(ln_b[i:i + 1])

    up = tgt2
    grads = {}
    dks_l, dvs_l, dwt_l, dwout_l, dlng_l, dlnb_l = [], [], [], [], [], []
    loss_part = None
    for i in reversed(range(DEPTH)):
        res = saved[i]
        extra = res[3] if len(res) > 3 else None
        out = _layer_bwd(i, i, up, i == DEPTH - 1, res[1], res[2], gs[i + 1], bs[i + 1], res[0], wout_full[i], ks, vs,
                         prm[i], extra)
        dres, dproj, dwout_i, dks_i, dvs_i, dg_i, db_i, loss_i = out[:8]
        pg = out[8:]
        if i == DEPTH - 1:
            loss_part = loss_i
        dks_l.append(dks_i)
        dvs_l.append(dvs_i)
        dwout_l.append(dwout_i)
        dlng_l.append(dg_i)
        dlnb_l.append(db_i)
        if i == 0:
            grads["a_w_s"], dbs_exp = pg
        elif i == 1:
            dlb, grads["b_norm_g"] = pg
        elif i == 2:
            dwbd, grads["c_scale"] = pg
            grads["c_w_pool"] = jnp.stack([lax.dynamic_slice(dwbd, (g * POOL_GROUP, g * POOL_GROUP), (POOL_GROUP, POOL_GROUP))
                                           for g in range(4)])
        else:
            (grads["d_conv_w"], grads["d_conv_b"], grads["d_w_gx"], grads["d_b_gx"], grads["d_w_ga"], grads["d_b_ga"],
             grads["d_a_param"]) = pg
        up, dwt = _proj_bwd(i, dproj, dres, xs[i], gs[i], bs[i], wt_full[i])
        dwt_l.append(dwt)
    grad_x = up[None]
    dwt_l = dwt_l[::-1]
    dwout_full = jnp.stack(dwout_l[::-1])
    grads["ln_g"] = jnp.concatenate(dlng_l[::-1], axis=0)
    grads["ln_b"] = jnp.concatenate(dlnb_l[::-1], axis=0)
    dwkv = _kv_bwd(mem2, dks_l, dvs_l)
    grads["a_b_s"], grads["hgrn_lb_logits"] = _small_finalize(dbs_exp, dlb, hgrn_lb_logits)

    nsmall = sum(_size(s) for _, s, _ in SMALL) + 128
    small_rows = _rows_for(nsmall)
    small_vec = _pack([grads[n] for n, _, _ in SMALL] + [jnp.broadcast_to(loss_part[:, :1], (1, 128))], small_rows)
    recv = _reduce_scatter_send(dwt_l + [dwout_full, dwkv, small_vec], [None] * 4 + [DEPTH, None, "whole"])

    outs = {}
    for t, n in enumerate(in_names):
        g, d, mn, vn = _sum_adam(f"adam_{n}", recv[t], W[n][0], M[n][0], V[n][0], True)
        outs[n] = (g[None], d[None], mn[None], vn[None])
    rwo = recv[4].reshape(NDEV, DEPTH * (D // NDEV), D)
    g, d, mn, vn = _sum_adam("adam_w_out", rwo, *[t.reshape(DEPTH * (D // NDEV), D) for t in (w_out, m_w_out, v_w_out)], False)
    outs["w_out"] = tuple(t.reshape(DEPTH, D // NDEV, D) for t in (g, d, mn, vn))
    outs["mem_kv_w"] = _sum_adam("adam_mem_kv_w", recv[5], mem_kv_w, m_mem_kv_w, v_mem_kv_w, False)

    rep = [(n, s) for n, s, sh in SMALL if not sh]
    wrep = _pack([W[n] if sh is False else jnp.zeros(s, F32) for n, s, sh in SMALL] + [jnp.zeros((128,), F32)], small_rows)
    mrep = _pack([M[n] if sh is False else jnp.zeros(s, F32) for n, s, sh in SMALL] + [jnp.zeros((128,), F32)], small_rows)
    vrep = _pack([V[n] if sh is False else jnp.ones(s, F32) for n, s, sh in SMALL] + [jnp.ones((128,), F32)], small_rows)
    g_all, d_all, m_all, v_all = _small_sum_adam(recv[6], wrep, mrep, vrep)

    def unpack(vec):
        flat = vec.reshape(-1)
        res, o = {}, 0
        for n, s, _ in SMALL:
            res[n] = flat[o:o + _size(s)].reshape(s)
            o += _size(s)
        return res, flat[o]

    g_small, loss = unpack(g_all)
    d_small, _ = unpack(d_all)
    m_small, _ = unpack(m_all)
    v_small, _ = unpack(v_all)
    for n, s in rep:
        outs[n] = (g_small[n], d_small[n], m_small[n], v_small[n])
    per = TOK // NDEV
    g_sh = {n: lax.dynamic_slice_in_dim(g_small[n], me * per, per, axis=len(s) - 1) for n, s, sh in SMALL if sh}
    gp = _pack([g_sh[n] for n in shard_names], 8)
    d_p, m_p, v_p = _adam_only(gp, small_shard, _pack([M[n] for n in shard_names], 8), _pack([V[n] for n in shard_names], 8))
    o = 0
    for n in shard_names:
        cnt = _size(W[n].shape)
        outs[n] = (g_sh[n],) + tuple(t.reshape(-1)[o:o + cnt].reshape(W[n].shape) for t in (d_p, m_p, v_p))
        o += cnt

    order = ["mem_kv_w", "ln_g", "ln_b", "w_out", "hgrn_lb_logits", "a_w_in", "a_w_s", "a_b_s", "b_w_in", "b_norm_g", "c_w_in",
             "c_w_pool", "c_scale", "d_w_in", "d_conv_w", "d_conv_b", "d_w_gx", "d_b_gx", "d_w_ga", "d_b_ga", "d_a_param"]
    result = [loss, grad_x]
    for j in range(4):
        result += [outs[n][j].reshape(W[n].shape) for n in order]
    return tuple(result)
```

```python
import functools

import jax
import jax.numpy as jnp
from jax import lax
from jax.experimental import pallas as pl
from jax.experimental.pallas import tpu as pltpu

F32 = jnp.float32
MXU = jnp.bfloat16

D = 1024
TOK = 768
XW = 256
NMEM = 256
XHEADS = 4
XSCALE = 64 ** -0.5
NH = 6
HD = 128
CH = 16
POOL_WINDOWS = (2, 4, 8, 16)
POOL_GROUP = 192
DEPTH = 4
ALPHA = (2 * DEPTH) ** 0.25
LN_EPS = 1e-5
RMS_EPS = 1e-6
LRU_C = 8.0
B1, B2, LR, EPS, WD, STEP = 0.9, 0.999, 0.001, 1e-8, 0.01, 10

NDEV = 8
TS = 256
TSB = 512
VMEM_LIMIT = 58 * 1024 * 1024

KIND_WIDTHS = {0: 2 * TOK + XW + D, 1: 3 * TOK + XW + D, 2: TOK + XW + D, 3: TOK + XW + D}


def _mm(a, b, ca, cb):
    return lax.dot_general(a.astype(MXU), b.astype(MXU), (((ca,), (cb,)), ((), ())), preferred_element_type=F32)


def _nn(a, b):
    return _mm(a, b, 1, 0)


def _nt(a, b):
    return _mm(a, b, 1, 1)


def _tn(a, b):
    return _mm(a, b, 0, 0)


def _bmm(a, b, ca, cb):
    return lax.dot_general(a.astype(MXU), b.astype(MXU), (((ca,), (cb,)), ((0,), (0,))), preferred_element_type=F32)


def _sigmoid(x):
    return 1.0 / (1.0 + jnp.exp(-x))


def _vjp1(fn, x, dy):
    return jax.vjp(fn, x)[1](dy)[0]


def _rowsum(x):
    return jnp.sum(x, axis=0, keepdims=True)


def _row(x, r):
    sel = lax.broadcasted_iota(jnp.int32, x.shape, 0) == r
    return jnp.sum(jnp.where(sel, x, 0.0), axis=0, keepdims=True)


def _acc(ref, val):
    ref[...] += val


def _cparams(sem=None):
    return pltpu.CompilerParams(dimension_semantics=sem, vmem_limit_bytes=VMEM_LIMIT)


def _res(a):
    nd = a.ndim
    return pl.BlockSpec(a.shape, lambda i: (0,) * nd)


def _res_sds(shape):
    nd = len(shape)
    return pl.BlockSpec(shape, lambda i: (0,) * nd)


def _rows(width, nt, rev, ts=TS):
    if rev:
        return pl.BlockSpec((ts, width), lambda i: (nt - 1 - i, 0))
    return pl.BlockSpec((ts, width), lambda i: (i, 0))


def _call(body, name, grid, ins, outs, scratch=(), sem=("arbitrary",)):
    arrays = [a for a, _ in ins]
    return pl.pallas_call(
        body, name=name, grid=grid,
        in_specs=[s for _, s in ins],
        out_specs=[s for _, s in outs],
        out_shape=[o for o, _ in outs],
        scratch_shapes=list(scratch),
        compiler_params=_cparams(sem),
    )(*arrays)


def _xattn_fwd(qx, ks_ref, vs_ref):
    o = None
    ps = []
    for h in range(XHEADS):
        s = _nt(qx, ks_ref[h]) * XSCALE
        s = s - jnp.max(s, axis=-1, keepdims=True)
        e = jnp.exp(s)
        p = e / jnp.sum(e, axis=-1, keepdims=True)
        ps.append(p)
        oh = _nn(p, vs_ref[h])
        o = oh if o is None else o + oh
    return o, ps


def _xattn_bwd(qx, ps, dxo, ks_ref, vs_ref, dks_ref, dvs_ref):
    dq = None
    for h in range(XHEADS):
        p = ps[h]
        dp = _nt(dxo, vs_ref[h])
        ds = p * (dp - jnp.sum(dp * p, axis=-1, keepdims=True))
        dqh = _nn(ds, ks_ref[h]) * XSCALE
        dq = dqh if dq is None else dq + dqh
        dks_ref[h] += _tn(ds, qx) * XSCALE
        dvs_ref[h] += _tn(p, dxo)
    return dq


def _tril128():
    r = lax.broadcasted_iota(jnp.int32, (HD, HD), 0)
    c = lax.broadcasted_iota(jnp.int32, (HD, HD), 1)
    return c <= r


def _gmlp_fwd(u, v, ws_ref, bs_ref):
    ts = u.shape[0]
    ug = jax.nn.gelu(u)
    vg = jax.nn.gelu(v)
    tri = _tril128()
    toks, res = [], []
    for g in range(NH):
        sl = slice(g * HD, (g + 1) * HD)
        vgh = vg[:, sl]
        cen = vgh - jnp.mean(vgh, axis=-1, keepdims=True)
        rstd = lax.rsqrt(jnp.mean(cen * cen, axis=-1, keepdims=True) + LN_EPS)
        vn = cen * rstd
        w = jnp.where(tri, ws_ref[g], 0.0).astype(MXU)
        mix = jnp.concatenate([_nn(w, vn[n * HD:(n + 1) * HD]) + bs_ref[g] for n in range(ts // HD)], axis=0)
        toks.append(ug[:, sl] * mix)
        res.append((vn, rstd, mix, w))
    return jnp.concatenate(toks, axis=1), (ug, res)


def _gmlp_bwd(u, v, fres, dtok, dws_ref, dbs_ref):
    ts = u.shape[0]
    ug, res = fres
    tri = _tril128()
    dugs, dvgs = [], []
    for g in range(NH):
        sl = slice(g * HD, (g + 1) * HD)
        vn, rstd, mix, w = res[g]
        dmix = dtok[:, sl] * ug[:, sl]
        dugs.append(dtok[:, sl] * mix)
        dvn_rows = []
        dw = None
        dbs = None
        for n in range(ts // HD):
            dm = dmix[n * HD:(n + 1) * HD]
            dvn_rows.append(_tn(w, dm))
            t = _nt(dm, vn[n * HD:(n + 1) * HD])
            dw = t if dw is None else dw + t
            dbs = dm if dbs is None else dbs + dm
        dws_ref[g] += jnp.where(tri, dw, 0.0)
        dbs_ref[g] += dbs
        dvn = jnp.concatenate(dvn_rows, axis=0)
        dvgs.append(rstd * (dvn - jnp.mean(dvn, axis=-1, keepdims=True) - vn * jnp.mean(dvn * vn, axis=-1, keepdims=True)))
    du = _vjp1(jax.nn.gelu, u, jnp.concatenate(dugs, axis=1))
    dv = _vjp1(jax.nn.gelu, v, jnp.concatenate(dvgs, axis=1))
    return du, dv


def _chunk_cumsum(x):
    row = lax.broadcasted_iota(jnp.int32, x.shape, 0) % CH
    for s in (1, 2, 4, 8):
        x = x + jnp.where(row >= s, pltpu.roll(x, s, 0), 0.0)
    return x


def _chunk_revcumsum(x):
    n = x.shape[0]
    row = lax.broadcasted_iota(jnp.int32, x.shape, 0) % CH
    for s in (1, 2, 4, 8):
        x = x + jnp.where(row < CH - s, pltpu.roll(x, n - s, 0), 0.0)
    return x


def _chunk_sum(x):
    n, w = x.shape
    return jnp.sum(x.reshape(n // CH, CH, w), axis=1)


def _chunk_bcast(c, n):
    nch, w = c.shape
    return jnp.broadcast_to(c[:, None, :], (nch, CH, w)).reshape(n, w)


def _lower_bound(lb_logits, layer):
    lg = lb_logits
    e = jnp.exp(lg - jnp.max(lg, axis=0, keepdims=True))
    p = e / jnp.sum(e, axis=0, keepdims=True)
    row = lax.broadcasted_iota(jnp.int32, p.shape, 0)
    lb = jnp.sum(jnp.where((row >= 1) & (row <= layer), p, 0.0), axis=0, keepdims=True)
    return lb, p


def _hgrn_prep(q, fl, lb):
    n = q.shape[0]
    sg = _sigmoid(fl)
    f = lb + (1.0 - lb) * sg
    lf = jnp.log(f)
    k = 1.0 - f
    sq = _sigmoid(q)
    qf = q * sq
    g = _chunk_cumsum(lf)
    tot = _chunk_sum(lf)
    gl = _chunk_bcast(tot, n)
    eg = jnp.exp(g)
    eng = jnp.exp(-g)
    egl = jnp.exp(gl - g)
    return dict(sg=sg, f=f, k=k, sq=sq, qf=qf, eg=eg, eng=eng, egl=egl,
                qd=qf * eg, ki=k * eng, ke=k * egl, dch=jnp.exp(tot))


def _hgrn_mask():
    r = lax.broadcasted_iota(jnp.int32, (HD, HD), 0)
    c = lax.broadcasted_iota(jnp.int32, (HD, HD), 1)
    return (r // CH == c // CH) & (c <= r)


def _hgrn_states(h, v3, ke3, dch_h, st_in, sts_s, ut_s, dch_s):
    nch = v3.shape[0]
    ut_s[...] = _bmm(v3, ke3, 1, 1)
    dch_s[...] = dch_h

    def step(c, st):
        sts_s[c] = st
        return st * dch_s[pl.ds(c, 1), :] + ut_s[c]

    return lax.fori_loop(0, nch, step, st_in)


def _hgrn_fwd(q, fl, inp, lb, ng, st_ref, sts_s, ut_s, dch_s):
    n = q.shape[0]
    nch = n // CH
    pr = _hgrn_prep(q, fl, lb)
    mask = _hgrn_mask()
    toks = []
    for h in range(NH):
        sl = slice(h * HD, (h + 1) * HD)
        qd, ki, ke, v = pr["qd"][:, sl], pr["ki"][:, sl], pr["ke"][:, sl], inp[:, sl]
        qd3 = qd.astype(MXU).reshape(nch, CH, HD)
        v3 = v.astype(MXU).reshape(nch, CH, HD)
        ke3 = ke.astype(MXU).reshape(nch, CH, HD)
        st_ref[h] = _hgrn_states(h, v3, ke3, pr["dch"][:, sl], st_ref[h], sts_s, ut_s, dch_s)
        o = _bmm(qd3, sts_s[...], 2, 2).reshape(n, HD)
        intra = []
        for b in range(n // HD):
            bs = slice(b * HD, (b + 1) * HD)
            a = jnp.where(mask, _nt(qd[bs], ki[bs]), 0.0)
            intra.append(_nn(a, v[bs]))
        o = o + jnp.concatenate(intra, axis=0)
        r = lax.rsqrt(jnp.mean(o * o, axis=-1, keepdims=True) + RMS_EPS)
        toks.append(o * r * ng[:, sl])
    return jnp.concatenate(toks, axis=1)


def _hgrn_bwd(q, fl, inp, lb, ng, dtok, ststart_ref, dst_ref, sts_s, ut_s, dch_s, dstn_s, dng_ref, dlb_ref):
    n = q.shape[0]
    nch = n // CH
    pr = _hgrn_prep(q, fl, lb)
    mask = _hgrn_mask()
    dqd_l, dki_l, dke_l, dv_l, ddch_l, dng_l, toks = [], [], [], [], [], [], []
    for h in range(NH):
        sl = slice(h * HD, (h + 1) * HD)
        qd, ki, ke, v = pr["qd"][:, sl], pr["ki"][:, sl], pr["ke"][:, sl], inp[:, sl]
        qd3 = qd.astype(MXU).reshape(nch, CH, HD)
        v3 = v.astype(MXU).reshape(nch, CH, HD)
        ke3 = ke.astype(MXU).reshape(nch, CH, HD)
        _hgrn_states(h, v3, ke3, pr["dch"][:, sl], ststart_ref[h], sts_s, ut_s, dch_s)
        sts = sts_s[...]
        o = _bmm(qd3, sts, 2, 2).reshape(n, HD)
        a_l = []
        intra = []
        for b in range(n // HD):
            bs = slice(b * HD, (b + 1) * HD)
            a = jnp.where(mask, _nt(qd[bs], ki[bs]), 0.0)
            a_l.append(a)
            intra.append(_nn(a, v[bs]))
        o = o + jnp.concatenate(intra, axis=0)
        r = lax.rsqrt(jnp.mean(o * o, axis=-1, keepdims=True) + RMS_EPS)
        toks.append(o * r * ng[:, sl])
        dt = dtok[:, sl]
        dng_l.append(_rowsum(dt * o * r))
        dn = dt * ng[:, sl]
        do = r * dn - o * (r * r * r) * jnp.mean(dn * o, axis=-1, keepdims=True)
        do3 = do.astype(MXU).reshape(nch, CH, HD)
        dqd_rows, dki_rows, dv_rows = [], [], []
        for b in range(n // HD):
            bs = slice(b * HD, (b + 1) * HD)
            da = jnp.where(mask, _nt(do[bs], v[bs]), 0.0)
            dqd_rows.append(_nn(da, ki[bs]))
            dki_rows.append(_tn(da, qd[bs]))
            dv_rows.append(_tn(a_l[b], do[bs]))
        dqd = jnp.concatenate(dqd_rows, axis=0) + _bmm(do3, sts, 2, 1).reshape(n, HD)
        dki = jnp.concatenate(dki_rows, axis=0)
        dv = jnp.concatenate(dv_rows, axis=0)
        ut_s[...] = _bmm(do3, qd3, 1, 1)

        def step(i, dst):
            c = nch - 1 - i
            dstn_s[c] = dst
            return ut_s[c] + dst * dch_s[pl.ds(c, 1), :]

        dst_ref[h] = lax.fori_loop(0, nch, step, dst_ref[h])
        dstn = dstn_s[...]
        dv = dv + _bmm(ke3, dstn, 2, 2).reshape(n, HD)
        dke = _bmm(v3, dstn, 2, 1).reshape(n, HD)
        ddch_l.append(jnp.sum(sts * dstn, axis=1))
        dqd_l.append(dqd)
        dki_l.append(dki)
        dke_l.append(dke)
        dv_l.append(dv)
    dqd = jnp.concatenate(dqd_l, axis=1)
    dki = jnp.concatenate(dki_l, axis=1)
    dke = jnp.concatenate(dke_l, axis=1)
    dinp = jnp.concatenate(dv_l, axis=1)
    ddch = jnp.concatenate(ddch_l, axis=1)
    _acc(dng_ref, jnp.concatenate(dng_l, axis=1))
    dqf = dqd * pr["eg"]
    dke_ke = dke * pr["ke"]
    dg = dqd * pr["qd"] - dki * pr["ki"] - dke_ke
    dk = dki * pr["eng"] + dke * pr["egl"]
    dgl = _chunk_sum(dke_ke) + ddch * pr["dch"]
    dlf = _chunk_revcumsum(dg) + _chunk_bcast(dgl, n)
    df = dlf / pr["f"] - dk
    sg = pr["sg"]
    dfl = df * (1.0 - lb) * sg * (1.0 - sg)
    _acc(dlb_ref, _rowsum(df * (1.0 - sg)))
    sq = pr["sq"]
    dq = dqf * (sq * (1.0 + q * (1.0 - sq)))
    return jnp.concatenate(toks, axis=1), dq, dfl, dinp


def _pool_select(s2, s4, s8, s16):
    col = lax.broadcasted_iota(jnp.int32, (1, TOK), 1)
    return jnp.where(col < POOL_GROUP, s2, jnp.where(col < 2 * POOL_GROUP, s4, jnp.where(col < 3 * POOL_GROUP, s8, s16)))


def _pool_cnt(pos0, n):
    pos = pos0 + lax.broadcasted_iota(jnp.int32, (n, TOK), 0) + 1
    col = lax.broadcasted_iota(jnp.int32, (n, TOK), 1)
    w = jnp.where(col < POOL_GROUP, 2, jnp.where(col < 2 * POOL_GROUP, 4, jnp.where(col < 3 * POOL_GROUP, 8, 16)))
    return jnp.minimum(pos, w).astype(F32)


def _pool_fwd(p, halo, pos0, wbd, scale):
    n = p.shape[0]
    ext = jnp.concatenate([halo, p], axis=0)
    s2 = ext + pltpu.roll(ext, 1, 0)
    s4 = s2 + pltpu.roll(s2, 2, 0)
    s8 = s4 + pltpu.roll(s4, 4, 0)
    s16 = s8 + pltpu.roll(s8, 8, 0)
    win = _pool_select(s2, s4, s8, s16)[16:]
    cnt = _pool_cnt(pos0, n)
    diff = win / cnt - p
    y = _nn(diff, wbd)
    return y * scale, (diff, y, cnt)


def _pool_bwd(fres, dtok, nxt_ref, wbd, scale, dwbd_ref, dscale_ref):
    diff, y, cnt = fres
    n = diff.shape[0]
    _acc(dscale_ref, _rowsum(dtok * y))
    dy = dtok * scale
    ddiff = _nt(dy, wbd)
    dwbd_ref[...] += _tn(diff, dy)
    qv = ddiff / cnt
    ext = jnp.concatenate([qv, nxt_ref[...]], axis=0)
    m = n + 16
    s2 = ext + pltpu.roll(ext, m - 1, 0)
    s4 = s2 + pltpu.roll(s2, m - 2, 0)
    s8 = s4 + pltpu.roll(s4, m - 4, 0)
    s16 = s8 + pltpu.roll(s8, m - 8, 0)
    adj = _pool_select(s2, s4, s8, s16)[:n]
    nxt_ref[...] = qv[:16]
    return adj - ddiff


def _neg_expm1(x):
    return jnp.where(jnp.abs(x) < 1e-2, -x * (1.0 + x * (0.5 + x * (1.0 / 6.0))), 1.0 - jnp.exp(x))


def _lru_gates(xc, zx, za, ap, first):
    gx = _sigmoid(zx)
    ga = _sigmoid(za)
    sp = jnp.maximum(-ap, 0.0) + jnp.log(1.0 + jnp.exp(-jnp.abs(ap)))
    log_a = -LRU_C * ga * sp
    a = jnp.exp(log_a)
    mult = jnp.sqrt(_neg_expm1(2.0 * log_a))
    mult = jnp.where(first, 1.0, mult)
    return a, mult * gx * xc


def _scan_fwd(a, b, h0):
    n = a.shape[0]
    row = lax.broadcasted_iota(jnp.int32, a.shape, 0)
    s = 1
    while s < n:
        keep = row >= s
        b = b + a * jnp.where(keep, pltpu.roll(b, s, 0), 0.0)
        a = a * jnp.where(keep, pltpu.roll(a, s, 0), 1.0)
        s *= 2
    return b + a * h0


def _scan_bwd(an, d, dh_next):
    n = an.shape[0]
    row = lax.broadcasted_iota(jnp.int32, an.shape, 0)
    s = 1
    while s < n:
        keep = row < n - s
        d = d + an * jnp.where(keep, pltpu.roll(d, n - s, 0), 0.0)
        an = an * jnp.where(keep, pltpu.roll(an, n - s, 0), 1.0)
        s *= 2
    return d + an * dh_next


def _lru_conv(xb, halo, cw_ref, cb):
    ext = jnp.concatenate([halo, xb], axis=0)
    sh = [pltpu.roll(ext, 3 - j, 0)[8:] if j < 3 else xb for j in range(4)]
    xc = cb
    for j in range(4):
        xc = xc + cw_ref[pl.ds(j, 1), :] * sh[j]
    return xc, sh


def _lru_fwd(xb, halo, pos0, prm, h0):
    cw, cb, wgx, bgx, wga, bga, ap = prm
    n = xb.shape[0]
    xc, sh = _lru_conv(xb, halo, cw, cb[...])
    zx = jnp.concatenate([_nn(xc[:, h * HD:(h + 1) * HD], wgx[h]) for h in range(NH)], axis=1) + bgx[...]
    za = jnp.concatenate([_nn(xc[:, h * HD:(h + 1) * HD], wga[h]) for h in range(NH)], axis=1) + bga[...]
    first = (pos0 + lax.broadcasted_iota(jnp.int32, (n, 1), 0)) == 0
    a, b = _lru_gates(xc, zx, za, ap[...], first)
    hseq = _scan_fwd(a, b, h0)
    return hseq, (xc, sh, zx, za, first, a)


def _lru_bwd(fres, hseq, h0, dtok, prm, carry_refs, grad_refs):
    cw, cb, wgx, bgx, wga, bga, ap = prm
    xc, sh, zx, za, first, a = fres
    anext_ref, dhnext_ref, dxcnext_ref = carry_refs
    dcw_ref, dcb_ref, dwgx_ref, dbgx_ref, dwga_ref, dbga_ref, dap_ref = grad_refs
    n = xc.shape[0]
    an = jnp.where(lax.broadcasted_iota(jnp.int32, a.shape, 0) == n - 1, anext_ref[...], pltpu.roll(a, n - 1, 0))
    dh = _scan_bwd(an, dtok, dhnext_ref[...])
    hprev = jnp.where(lax.broadcasted_iota(jnp.int32, hseq.shape, 0) == 0, h0, pltpu.roll(hseq, 1, 0))
    da = dh * hprev
    anext_ref[...] = _row(a, 0)
    dhnext_ref[...] = _row(dh, 0)
    _, vjp = jax.vjp(lambda xc_, zx_, za_, ap_: _lru_gates(xc_, zx_, za_, ap_, first), xc, zx, za, ap[...])
    dxc, dzx, dza, dap = vjp((da, dh))
    _acc(dap_ref, dap)
    _acc(dbgx_ref, _rowsum(dzx))
    _acc(dbga_ref, _rowsum(dza))
    parts = []
    for h in range(NH):
        sl = slice(h * HD, (h + 1) * HD)
        parts.append(_nt(dzx[:, sl], wgx[h]) + _nt(dza[:, sl], wga[h]))
        dwgx_ref[h] += _tn(xc[:, sl], dzx[:, sl])
        dwga_ref[h] += _tn(xc[:, sl], dza[:, sl])
    dxc = dxc + jnp.concatenate(parts, axis=1)
    _acc(dcb_ref, _rowsum(dxc))
    for j in range(4):
        dcw_ref[pl.ds(j, 1), :] += _rowsum(dxc * sh[j])
    ext = jnp.concatenate([dxc, dxcnext_ref[...]], axis=0)
    m = n + 8
    dxb = cw[pl.ds(3, 1), :] * dxc
    for j in range(3):
        dxb = dxb + cw[pl.ds(j, 1), :] * pltpu.roll(ext, m - (3 - j), 0)[:n]
    dxcnext_ref[...] = dxc[:8]
    return dxb


def _layer_fwd(kind, layer, xprev, gprev, bprev, wt, wout, ks, vs, prm, ride=None):
    S = xprev.shape[0]
    nt = S // TS
    N = wt.shape[0]
    nprm = len(prm)
    nch = TS // CH

    outs = [(jax.ShapeDtypeStruct((S, N), F32), _rows(N, nt, False)),
            (jax.ShapeDtypeStruct((S, D), F32), _rows(D, nt, False)),
            (jax.ShapeDtypeStruct((S, 1), F32), _rows(1, nt, False))]
    scratch = []
    if kind == 1:
        outs.append((jax.ShapeDtypeStruct((nt, NH, HD, HD), F32), pl.BlockSpec((None, NH, HD, HD), lambda i: (i, 0, 0, 0))))
        scratch = [pltpu.VMEM((NH, HD, HD), F32), pltpu.VMEM((nch, HD, HD), F32), pltpu.VMEM((nch, HD, HD), F32),
                   pltpu.VMEM((nch, HD), F32)]
    elif kind == 2:
        scratch = [pltpu.VMEM((16, TOK), F32)]
    elif kind == 3:
        outs.append((jax.ShapeDtypeStruct((nt * 8, TOK), F32), pl.BlockSpec((8, TOK), lambda i: (i, 0))))
        scratch = [pltpu.VMEM((8, TOK), F32), pltpu.VMEM((1, TOK), F32)]
    nout = len(outs)
    nscr = len(scratch)
    nride = len(ride.arrays) if ride else 0

    def body(*refs):
        x_ref, g_ref, b_ref, wt_ref, wout_ref, ks_ref, vs_ref = refs[:7]
        prm_refs = refs[7:7 + nprm]
        nin = 7 + nprm + nride
        ride_src = refs[7 + nprm:nin]
        out_refs = refs[nin:nin + nout]
        ride_dst = refs[nin + nout:nin + nout + nride]
        scr = refs[nin + nout + nride:nin + nout + nride + nscr]
        ride_sems = refs[nin + nout + nride + nscr:]
        proj_ref, xhat_ref, rstd_ref = out_refs[:3]
        i = pl.program_id(0)
        if ride:
            @pl.when(i == 0)
            def _():
                ride.start(ride_src, ride_dst, ride_sems)

        xin = x_ref[...] * g_ref[...] + b_ref[...]
        proj = _nt(xin, wt_ref[...])
        proj_ref[...] = proj
        if kind == 0:
            tok, _ = _gmlp_fwd(proj[:, :TOK], proj[:, TOK:2 * TOK], prm_refs[0], prm_refs[1])
        elif kind == 1:
            st_ref, sts_s, ut_s, dch_s = scr

            @pl.when(i == 0)
            def _():
                st_ref[...] = jnp.zeros_like(st_ref)

            out_refs[3][...] = st_ref[...]
            lb, _ = _lower_bound(prm_refs[0][...], layer)
            tok = _hgrn_fwd(proj[:, :TOK], proj[:, TOK:2 * TOK], proj[:, 2 * TOK:3 * TOK], lb, prm_refs[1][...],
                            st_ref, sts_s, ut_s, dch_s)
        elif kind == 2:
            halo_ref, = scr

            @pl.when(i == 0)
            def _():
                halo_ref[...] = jnp.zeros_like(halo_ref)

            p = proj[:, :TOK]
            tok, _ = _pool_fwd(p, halo_ref[...], i * TS, prm_refs[0][...], prm_refs[1][...])
            halo_ref[...] = p[TS - 16:]
        else:
            halo_ref, h_ref = scr

            @pl.when(i == 0)
            def _():
                halo_ref[...] = jnp.zeros_like(halo_ref)
                h_ref[...] = jnp.zeros_like(h_ref)

            out_refs[3][...] = jnp.broadcast_to(h_ref[...], (8, TOK))
            xb = proj[:, :TOK]
            tok, _ = _lru_fwd(xb, halo_ref[...], i * TS, prm_refs, h_ref[...])
            halo_ref[...] = xb[TS - 8:]
            h_ref[...] = _row(tok, TS - 1)
        qx = proj[:, N - D - XW:N - D]
        gate = proj[:, N - D:]
        xo, _ = _xattn_fwd(qx, ks_ref, vs_ref)
        mixed = jnp.concatenate([tok, xo], axis=1) * (gate * _sigmoid(gate))
        z = ALPHA * xin + _nn(mixed, wout_ref[...])
        cen = z - jnp.mean(z, axis=-1, keepdims=True)
        rstd = lax.rsqrt(jnp.mean(cen * cen, axis=-1, keepdims=True) + LN_EPS)
        xhat_ref[...] = cen * rstd
        rstd_ref[...] = rstd
        if ride:
            @pl.when(i == nt - 1)
            def _():
                ride.wait(ride_src, ride_dst, ride_sems)

    ins = [(xprev, _rows(D, nt, False)), (gprev, _res(gprev)), (bprev, _res(bprev)), (wt, _res(wt)), (wout, _res(wout)),
           (ks, _res(ks)), (vs, _res(vs))] + [(p, _res(p)) for p in prm]
    if ride:
        ins += [(a, _ANY) for a in ride.arrays]
        outs += [(s, _ANY) for s in ride.out_shapes]
        scratch = scratch + ride.scratch
    return _call(body, f"layer{layer}_fwd", (nt,), ins, outs, scratch)


def _layer_bwd(kind, layer, up, is_last, xhat, rstd, g_i, b_i, proj, wout, ks, vs, prm, extra, ride=None):
    S = xhat.shape[0]
    nt = S // TS
    N = proj.shape[1]
    nprm = len(prm)
    nch = TS // CH

    ins = [(up, _rows(D, nt, True)), (xhat, _rows(D, nt, True)), (rstd, _rows(1, nt, True)), (g_i, _res(g_i)), (b_i, _res(b_i)),
           (proj, _rows(N, nt, True)), (wout, _res(wout)), (ks, _res(ks)), (vs, _res(vs))] + [(p, _res(p)) for p in prm]
    nfixed = 9
    if kind == 1:
        ins.append((extra, pl.BlockSpec((None, NH, HD, HD), lambda i: (nt - 1 - i, 0, 0, 0))))
    elif kind == 2:
        hb = TS // 16
        ins.append((proj, pl.BlockSpec((16, TOK), lambda i: (jnp.maximum((nt - 1 - i) * hb - 1, 0), 0))))
    elif kind == 3:
        hb = TS // 8
        ins.append((proj, pl.BlockSpec((8, TOK), lambda i: (jnp.maximum((nt - 1 - i) * hb - 1, 0), 0))))
        ins.append((extra, pl.BlockSpec((8, TOK), lambda i: (nt - 1 - i, 0))))
    nin = len(ins)

    def acc(shape):
        return (jax.ShapeDtypeStruct(shape, F32), _res_sds(shape))

    outs = [(jax.ShapeDtypeStruct((S, D), F32), _rows(D, nt, True)),
            (jax.ShapeDtypeStruct((S, N), MXU), _rows(N, nt, True)),
            acc((D, D)), acc((XHEADS, NMEM, XW)), acc((XHEADS, NMEM, XW)), acc((1, D)), acc((1, D)), acc((1, HD))]
    scratch = []
    if kind == 0:
        outs += [acc((NH, HD, HD)), acc((NH, HD, HD))]
    elif kind == 1:
        outs += [acc((1, TOK)), acc((1, TOK))]
        scratch = [pltpu.VMEM((NH, HD, HD), F32), pltpu.VMEM((nch, HD, HD), F32), pltpu.VMEM((nch, HD, HD), F32),
                   pltpu.VMEM((nch, HD), F32), pltpu.VMEM((nch, HD, HD), F32)]
    elif kind == 2:
        outs += [acc((TOK, TOK)), acc((1, TOK))]
        scratch = [pltpu.VMEM((16, TOK), F32)]
    else:
        outs += [acc((4, TOK)), acc((1, TOK)), acc((NH, HD, HD)), acc((1, TOK)), acc((NH, HD, HD)), acc((1, TOK)), acc((1, TOK))]
        scratch = [pltpu.VMEM((1, TOK), F32), pltpu.VMEM((1, TOK), F32), pltpu.VMEM((8, TOK), F32)]
    nout = len(outs)
    nscr = len(scratch)
    nride = len(ride.arrays) if ride else 0

    def body(*refs):
        up_ref, xhat_ref, rstd_ref, g_ref, b_ref, proj_ref, wout_ref, ks_ref, vs_ref = refs[:nfixed]
        prm_refs = refs[nfixed:nfixed + nprm]
        ext_refs = refs[nfixed + nprm:nin]
        ride_src = refs[nin:nin + nride]
        o0 = nin + nride
        out_refs = refs[o0:o0 + nout]
        ride_dst = refs[o0 + nout:o0 + nout + nride]
        scr = refs[o0 + nout + nride:o0 + nout + nride + nscr]
        ride_sems = refs[o0 + nout + nride + nscr:]
        dres_ref, dproj_ref, dwout_ref, dks_ref, dvs_ref, dg_ref, db_ref, loss_ref = out_refs[:8]
        pgrad = out_refs[8:]
        i = pl.program_id(0)
        tile = nt - 1 - i

        @pl.when(i == 0)
        def _():
            if ride:
                ride.start(ride_src, ride_dst, ride_sems)
            for r in out_refs[2:]:
                r[...] = jnp.zeros_like(r)
            for r in scr:
                if kind != 1 or r is scr[0]:
                    r[...] = jnp.zeros_like(r)

        xhat_v = xhat_ref[...]
        if is_last:
            err = xhat_v * g_ref[...] + b_ref[...] - up_ref[...]
            dxo = err * (1.0 / D)
            loss_ref[...] += jnp.sum(0.5 * jnp.mean(err * err, axis=-1, keepdims=True), axis=0, keepdims=True)
        else:
            dxo = up_ref[...]
        _acc(dg_ref, _rowsum(dxo * xhat_v))
        _acc(db_ref, _rowsum(dxo))
        dxh = dxo * g_ref[...]
        dz = rstd_ref[...] * (dxh - jnp.mean(dxh, axis=-1, keepdims=True)
                              - xhat_v * jnp.mean(dxh * xhat_v, axis=-1, keepdims=True))
        dres_ref[...] = ALPHA * dz

        proj = proj_ref[...]
        qx = proj[:, N - D - XW:N - D]
        gate = proj[:, N - D:]
        sgate = _sigmoid(gate)
        silu = gate * sgate
        dmixed = _nt(dz, wout_ref[...])
        dcat = dmixed * silu
        dtok = dcat[:, :TOK]
        if kind == 0:
            u, v = proj[:, :TOK], proj[:, TOK:2 * TOK]
            tok, fres = _gmlp_fwd(u, v, prm_refs[0], prm_refs[1])
            du, dv = _gmlp_bwd(u, v, fres, dtok, pgrad[0], pgrad[1])
            dproj_ref[:, :TOK] = du.astype(MXU)
            dproj_ref[:, TOK:2 * TOK] = dv.astype(MXU)
        elif kind == 1:
            dst_ref, sts_s, ut_s, dch_s, dstn_s = scr
            lb, _ = _lower_bound(prm_refs[0][...], layer)
            q, fl, inp = proj[:, :TOK], proj[:, TOK:2 * TOK], proj[:, 2 * TOK:3 * TOK]
            tok, dq, dfl, dinp = _hgrn_bwd(q, fl, inp, lb, prm_refs[1][...], dtok, ext_refs[0], dst_ref, sts_s, ut_s, dch_s,
                                           dstn_s, pgrad[1], pgrad[0])
            dproj_ref[:, :TOK] = dq.astype(MXU)
            dproj_ref[:, TOK:2 * TOK] = dfl.astype(MXU)
            dproj_ref[:, 2 * TOK:3 * TOK] = dinp.astype(MXU)
        elif kind == 2:
            p = proj[:, :TOK]
            halo = jnp.where(tile == 0, 0.0, ext_refs[0][...])
            tok, fres = _pool_fwd(p, halo, tile * TS, prm_refs[0][...], prm_refs[1][...])
            dp = _pool_bwd(fres, dtok, scr[0], prm_refs[0][...], prm_refs[1][...], pgrad[0], pgrad[1])
            dproj_ref[:, :TOK] = dp.astype(MXU)
        else:
            xb = proj[:, :TOK]
            halo = jnp.where(tile == 0, 0.0, ext_refs[0][...])
            h0 = ext_refs[1][0:1]
            tok, fres = _lru_fwd(xb, halo, tile * TS, prm_refs, h0)
            dxb = _lru_bwd(fres, tok, h0, dtok, prm_refs, scr, pgrad)
            dproj_ref[:, :TOK] = dxb.astype(MXU)
        xo, ps = _xattn_fwd(qx, ks_ref, vs_ref)
        dqx = _xattn_bwd(qx, ps, dcat[:, TOK:], ks_ref, vs_ref, dks_ref, dvs_ref)
        cat = jnp.concatenate([tok, xo], axis=1)
        dwout_ref[...] += _tn(cat * silu, dz)
        dgate = dmixed * cat * (sgate * (1.0 + gate * (1.0 - sgate)))
        dproj_ref[:, N - D - XW:N - D] = dqx.astype(MXU)
        dproj_ref[:, N - D:] = dgate.astype(MXU)
        if ride:
            @pl.when(i == nt - 1)
            def _():
                ride.wait(ride_src, ride_dst, ride_sems)

    if ride:
        ins += [(a, _ANY) for a in ride.arrays]
        outs += [(s, _ANY) for s in ride.out_shapes]
        scratch = scratch + ride.scratch
    return _call(body, f"layer{layer}_bwd", (nt,), ins, outs, scratch)


def _proj_bwd(layer, dproj, dres, xprev, gprev, bprev, wt):
    S = xprev.shape[0]
    nt = S // TSB
    N = wt.shape[0]

    def body(dproj_ref, dres_ref, x_ref, g_ref, b_ref, wt_ref, dx_ref, dwt_ref):
        @pl.when(pl.program_id(0) == 0)
        def _():
            dwt_ref[...] = jnp.zeros_like(dwt_ref)

        dp = dproj_ref[...]
        xin = x_ref[...] * g_ref[...] + b_ref[...]
        dx_ref[...] = dres_ref[...] + _nn(dp, wt_ref[...])
        dwt_ref[...] += _tn(dp, xin)

    ins = [(dproj, _rows(N, nt, False, TSB)), (dres, _rows(D, nt, False, TSB)), (xprev, _rows(D, nt, False, TSB)),
           (gprev, _res(gprev)), (bprev, _res(bprev)), (wt, _res(wt))]
    outs = [(jax.ShapeDtypeStruct((S, D), F32), _rows(D, nt, False, TSB)),
            (jax.ShapeDtypeStruct((N, D), F32), _res_sds((N, D)))]
    return _call(body, f"layer{layer}_projbwd", (nt,), ins, outs)


def _head_mask(h):
    col = lax.broadcasted_iota(jnp.int32, (1, XW), 1)
    return (col // 64) == h


def _kv_fwd(mem, wkv):
    def body(mem_ref, w_ref, ks_ref, vs_ref):
        kv = _nn(mem_ref[...], w_ref[...])
        k, v = kv[:, :XW], kv[:, XW:]
        for h in range(XHEADS):
            ks_ref[h] = jnp.where(_head_mask(h), k, 0.0).astype(MXU)
            vs_ref[h] = jnp.where(_head_mask(h), v, 0.0).astype(MXU)

    sds = jax.ShapeDtypeStruct((XHEADS, NMEM, XW), MXU)
    return pl.pallas_call(body, name="kv_fwd", out_shape=(sds, sds), compiler_params=_cparams())(mem, wkv)


def _kv_bwd(mem, dks_l, dvs_l):
    def body(mem_ref, *refs):
        dks_refs, dvs_refs, out_ref = refs[:DEPTH], refs[DEPTH:2 * DEPTH], refs[2 * DEPTH]
        dk = jnp.zeros((NMEM, XW), F32)
        dv = jnp.zeros((NMEM, XW), F32)
        for h in range(XHEADS):
            m = _head_mask(h)
            for l in range(DEPTH):
                dk = dk + jnp.where(m, dks_refs[l][h], 0.0)
                dv = dv + jnp.where(m, dvs_refs[l][h], 0.0)
        out_ref[...] = _tn(mem_ref[...], jnp.concatenate([dk, dv], axis=1))

    return pl.pallas_call(body, name="kv_bwd", out_shape=jax.ShapeDtypeStruct((D, 2 * XW), F32),
                          compiler_params=_cparams())(mem, *dks_l, *dvs_l)


def _prep_weights(w_ins, w_out, wkv):
    def body(a_ref, b_ref, c_ref, d_ref, wo_ref, kv_ref, ao, bo, co, do, wo0, wo1, wo2, wo3, kvo):
        for src, dst in ((a_ref, ao), (b_ref, bo), (c_ref, co), (d_ref, do)):
            dst[...] = src[...].T.astype(MXU)
        for l, dst in enumerate((wo0, wo1, wo2, wo3)):
            dst[...] = wo_ref[l].astype(MXU)
        kvo[...] = kv_ref[...].astype(MXU)

    outs = [jax.ShapeDtypeStruct((w.shape[1], w.shape[0]), MXU) for w in w_ins]
    outs += [jax.ShapeDtypeStruct(w_out.shape[1:], MXU)] * DEPTH + [jax.ShapeDtypeStruct(wkv.shape, MXU)]
    return pl.pallas_call(body, name="prep_weights", out_shape=outs, compiler_params=_cparams())(*w_ins, w_out, wkv)


def _adam_math(w, g, m, v):
    m = B1 * m + (1.0 - B1) * g
    v = B2 * v + (1.0 - B2) * (g * g)
    m_hat = m / (1.0 - B1 ** STEP)
    v_hat = v / (1.0 - B2 ** STEP)
    delta = -LR * (m_hat / (jnp.sqrt(v_hat) + EPS) + WD * w)
    return delta, m, v


def _sum_adam(name, recv, w, m, v, transpose):
    rows, cols = recv.shape[1], recv.shape[2]

    def body(r_ref, w_ref, m_ref, v_ref, g_out, d_out, m_out, v_out, acc_ref):
        s = pl.program_id(0)

        @pl.when(s == 0)
        def _():
            acc_ref[...] = r_ref[...]

        @pl.when(s > 0)
        def _():
            acc_ref[...] += r_ref[...]

        @pl.when(s == NDEV - 1)
        def _():
            g = acc_ref[...].T if transpose else acc_ref[...]
            d, mn, vn = _adam_math(w_ref[...], g, m_ref[...], v_ref[...])
            g_out[...] = g
            d_out[...] = d
            m_out[...] = mn
            v_out[...] = vn

    sds = jax.ShapeDtypeStruct(w.shape, F32)
    ins = [(recv, pl.BlockSpec((None, rows, cols), lambda s: (s, 0, 0))), (w, _res(w)), (m, _res(m)), (v, _res(v))]
    outs = [(sds, _res_sds(w.shape))] * 4
    return _call(body, name, (NDEV,), ins, outs, [pltpu.VMEM((rows, cols), F32)])


def _small_finalize(dbs_exp, dlb, lb_logits):
    def body(dbs_ref, dlb_ref, lg_ref, dabs_ref, dlg_ref):
        dabs_ref[...] = jnp.sum(dbs_ref[...], axis=-1)
        total = jnp.zeros((DEPTH, TOK), F32)
        lg = lg_ref[...]
        e = jnp.exp(lg - jnp.max(lg, axis=0, keepdims=True))
        p = e / jnp.sum(e, axis=0, keepdims=True)
        row = lax.broadcasted_iota(jnp.int32, (DEPTH, TOK), 0)
        for layer in range(DEPTH):
            if layer % 4 != 1:
                continue
            dp = jnp.where((row >= 1) & (row <= layer), dlb_ref[...], 0.0)
            total = total + p * (dp - jnp.sum(p * dp, axis=0, keepdims=True))
        dlg_ref[...] = total

    return pl.pallas_call(body, name="small_finalize",
                          out_shape=(jax.ShapeDtypeStruct((NH, HD), F32), jax.ShapeDtypeStruct((DEPTH, TOK), F32)),
                          compiler_params=_cparams())(dbs_exp, dlb, lb_logits)


def _small_sum_adam(gathered, w, m, v):
    rows = w.shape[0]

    def body(r_ref, w_ref, m_ref, v_ref, g_out, d_out, m_out, v_out):
        g = r_ref[0]
        for s in range(1, NDEV):
            g = g + r_ref[s]
        d, mn, vn = _adam_math(w_ref[...], g, m_ref[...], v_ref[...])
        g_out[...] = g
        d_out[...] = d
        m_out[...] = mn
        v_out[...] = vn

    sds = jax.ShapeDtypeStruct((rows, 128), F32)
    return pl.pallas_call(body, name="small_sum_adam", out_shape=(sds,) * 4, compiler_params=_cparams())(gathered, w, m, v)


def _adam_only(g, w, m, v):
    def body(g_ref, w_ref, m_ref, v_ref, d_out, m_out, v_out):
        d, mn, vn = _adam_math(w_ref[...], g_ref[...], m_ref[...], v_ref[...])
        d_out[...] = d
        m_out[...] = mn
        v_out[...] = vn

    sds = jax.ShapeDtypeStruct(w.shape, F32)
    return pl.pallas_call(body, name="shard_adam", out_shape=(sds,) * 3, compiler_params=_cparams())(g, w, m, v)


def _me_and_peers():
    x, y, c = lax.axis_index("x"), lax.axis_index("y"), lax.axis_index("c")
    me = 4 * x + 2 * y + c
    peers = []
    for k in range(1, NDEV):
        kx, ky, kc = (k >> 2) & 1, (k >> 1) & 1, k & 1
        px = x + kx - 2 * x * kx
        py = y + ky - 2 * y * ky
        pc = c + kc - 2 * c * kc
        peers.append(((px, py, pc), 4 * px + 2 * py + pc))
    return me, peers


_ANY = pl.BlockSpec(memory_space=pl.ANY)


class _Exchange:
    def __init__(self, arrays, split):
        self.arrays = list(arrays)
        self.split = list(split)
        n = len(self.arrays)
        self.out_shapes = []
        for a, sp in zip(self.arrays, self.split):
            rows = a.shape[0] // NDEV if sp else a.shape[0]
            self.out_shapes.append(jax.ShapeDtypeStruct((NDEV, rows, a.shape[1]), a.dtype))
        self.scratch = [pltpu.SemaphoreType.DMA((n, NDEV - 1)), pltpu.SemaphoreType.DMA((n, NDEV - 1)),
                        pltpu.SemaphoreType.DMA((n,))]

    def _block(self, src, t, d):
        if not self.split[t]:
            return src[t]
        rows = self.arrays[t].shape[0] // NDEV
        return src[t].at[pl.ds(d * rows, rows)]

    def start(self, src, dst, sems):
        send_sems, recv_sems, local_sems = sems
        me, peers = _me_and_peers()
        for t in range(len(self.arrays)):
            pltpu.make_async_copy(self._block(src, t, me), dst[t].at[me], local_sems.at[t]).start()
        for k, (dev, idx) in enumerate(peers):
            for t in range(len(self.arrays)):
                pltpu.make_async_remote_copy(src_ref=self._block(src, t, idx), dst_ref=dst[t].at[me],
                                             send_sem=send_sems.at[t, k], recv_sem=recv_sems.at[t, k],
                                             device_id=dev, device_id_type=pl.DeviceIdType.MESH).start()

    def wait(self, src, dst, sems):
        send_sems, recv_sems, local_sems = sems
        me, peers = _me_and_peers()

        def slot_copy(t, k, dev, idx):
            return pltpu.make_async_remote_copy(src_ref=dst[t].at[idx], dst_ref=dst[t].at[idx], send_sem=send_sems.at[t, k],
                                                recv_sem=recv_sems.at[t, k], device_id=dev,
                                                device_id_type=pl.DeviceIdType.MESH)

        for k, (dev, idx) in enumerate(peers):
            for t in range(len(self.arrays)):
                slot_copy(t, k, dev, idx).wait_recv()
        for k, (dev, idx) in enumerate(peers):
            for t in range(len(self.arrays)):
                slot_copy(t, k, dev, idx).wait_send()
        for t in range(len(self.arrays)):
            pltpu.make_async_copy(dst[t].at[me], dst[t].at[me], local_sems.at[t]).wait()

    def run(self, name):
        n = len(self.arrays)

        def body(*refs):
            src, dst, sems = refs[:n], refs[n:2 * n], refs[2 * n:]
            self.start(src, dst, sems)
            self.wait(src, dst, sems)

        return pl.pallas_call(
            body, name=name, out_shape=self.out_shapes, in_specs=[_ANY] * n, out_specs=[_ANY] * n,
            scratch_shapes=self.scratch,
        )(*self.arrays)


SMALL = [("ln_g", (DEPTH, D), False), ("ln_b", (DEPTH, D), False), ("hgrn_lb_logits", (DEPTH, TOK), False),
         ("a_w_s", (1, NH, HD, HD), False), ("a_b_s", (1, NH, HD), False), ("b_norm_g", (1, TOK), True),
         ("c_w_pool", (1, 4, POOL_GROUP, POOL_GROUP), False), ("c_scale", (1, TOK), True),
         ("d_conv_w", (1, 4, TOK), True), ("d_conv_b", (1, TOK), True),
         ("d_w_gx", (1, NH, HD, HD), False), ("d_b_gx", (1, NH, HD), False),
         ("d_w_ga", (1, NH, HD, HD), False), ("d_b_ga", (1, NH, HD), False), ("d_a_param", (1, TOK), True)]


def _pack(parts, total_rows):
    flat = jnp.concatenate([p.reshape(-1).astype(F32) for p in parts])
    flat = jnp.pad(flat, (0, total_rows * 128 - flat.shape[0]))
    return flat.reshape(total_rows, 128)


def _size(shape):
    n = 1
    for s in shape:
        n *= s
    return n


def _rows_for(n):
    return -(-n // 1024) * 8


def kernel(x, mem, mem_kv_w, ln_g, ln_b, w_out, hgrn_lb_logits, a_w_in, a_w_s, a_b_s, b_w_in, b_norm_g, c_w_in, c_w_pool, c_scale, d_w_in, d_conv_w, d_conv_b, d_w_gx, d_b_gx, d_w_ga, d_b_ga, d_a_param, loss_target, m_mem_kv_w, m_ln_g, m_ln_b, m_w_out, m_hgrn_lb_logits, m_a_w_in, m_a_w_s, m_a_b_s, m_b_w_in, m_b_norm_g, m_c_w_in, m_c_w_pool, m_c_scale, m_d_w_in, m_d_conv_w, m_d_conv_b, m_d_w_gx, m_d_b_gx, m_d_w_ga, m_d_b_ga, m_d_a_param, v_mem_kv_w, v_ln_g, v_ln_b, v_w_out, v_hgrn_lb_logits, v_a_w_in, v_a_w_s, v_a_b_s, v_b_w_in, v_b_norm_g, v_c_w_in, v_c_w_pool, v_c_scale, v_d_w_in, v_d_conv_w, v_d_conv_b, v_d_w_gx, v_d_b_gx, v_d_w_ga, v_d_b_ga, v_d_a_param):
    W = dict(mem_kv_w=mem_kv_w, ln_g=ln_g, ln_b=ln_b, w_out=w_out, hgrn_lb_logits=hgrn_lb_logits, a_w_in=a_w_in, a_w_s=a_w_s,
             a_b_s=a_b_s, b_w_in=b_w_in, b_norm_g=b_norm_g, c_w_in=c_w_in, c_w_pool=c_w_pool, c_scale=c_scale, d_w_in=d_w_in,
             d_conv_w=d_conv_w, d_conv_b=d_conv_b, d_w_gx=d_w_gx, d_b_gx=d_b_gx, d_w_ga=d_w_ga, d_b_ga=d_b_ga, d_a_param=d_a_param)
    M = dict(mem_kv_w=m_mem_kv_w, ln_g=m_ln_g, ln_b=m_ln_b, w_out=m_w_out, hgrn_lb_logits=m_hgrn_lb_logits, a_w_in=m_a_w_in,
             a_w_s=m_a_w_s, a_b_s=m_a_b_s, b_w_in=m_b_w_in, b_norm_g=m_b_norm_g, c_w_in=m_c_w_in, c_w_pool=m_c_w_pool,
             c_scale=m_c_scale, d_w_in=m_d_w_in, d_conv_w=m_d_conv_w, d_conv_b=m_d_conv_b, d_w_gx=m_d_w_gx, d_b_gx=m_d_b_gx,
             d_w_ga=m_d_w_ga, d_b_ga=m_d_b_ga, d_a_param=m_d_a_param)
    V = dict(mem_kv_w=v_mem_kv_w, ln_g=v_ln_g, ln_b=v_ln_b, w_out=v_w_out, hgrn_lb_logits=v_hgrn_lb_logits, a_w_in=v_a_w_in,
             a_w_s=v_a_w_s, a_b_s=v_a_b_s, b_w_in=v_b_w_in, b_norm_g=v_b_norm_g, c_w_in=v_c_w_in, c_w_pool=v_c_w_pool,
             c_scale=v_c_scale, d_w_in=v_d_w_in, d_conv_w=v_d_conv_w, d_conv_b=v_d_conv_b, d_w_gx=v_d_w_gx, d_b_gx=v_d_b_gx,
             d_w_ga=v_d_w_ga, d_b_ga=v_d_b_ga, d_a_param=v_d_a_param)
    me = 4 * lax.axis_index("x") + 2 * lax.axis_index("y") + lax.axis_index("c")
    x2, mem2, tgt2 = x[0], mem[0], loss_target[0]
    in_names = ["a_w_in", "b_w_in", "c_w_in", "d_w_in"]

    shard_names = [n for n, _, sh in SMALL if sh]
    small_shard = _pack([W[n] for n in shard_names], 8)
    wts = _prep_weights([W[n][0] for n in in_names], w_out, mem_kv_w)
    wt_sh, wo_sh, wkv_sh = wts[:4], wts[4:8], wts[8]
    g0 = _Exchange([wt_sh[0], wo_sh[0], wkv_sh, small_shard], [False] * 4).run("gather_first")
    wt_full = [g0[0].reshape(-1, D)]
    wout_full = [g0[1].reshape(D, D)]
    wkv_full = g0[2].reshape(D, 2 * XW)
    sm = g0[3].reshape(NDEV, 1024)
    full_small = {}
    off = 0
    for n, shape, _ in [s for s in SMALL if s[2]]:
        per = _size(shape) // NDEV
        blk = sm[:, off:off + per]
        if n == "d_conv_w":
            full_small[n] = blk.reshape(NDEV, 4, TOK // NDEV).transpose(1, 0, 2).reshape(4, TOK)
        else:
            full_small[n] = blk.reshape(1, TOK)
        off += per

    ks, vs = _kv_fwd(mem2, wkv_full)
    tri_bs = jnp.broadcast_to(a_b_s[0][:, :, None], (NH, HD, HD))
    wbd = jnp.zeros((TOK, TOK), F32)
    for g in range(4):
        wbd = lax.dynamic_update_slice(wbd, c_w_pool[0, g], (g * POOL_GROUP, g * POOL_GROUP))
    wbd = wbd.astype(MXU)
    prm = {0: [a_w_s[0], tri_bs],
           1: [hgrn_lb_logits, full_small["b_norm_g"]],
           2: [wbd, full_small["c_scale"]],
           3: [full_small["d_conv_w"], full_small["d_conv_b"], d_w_gx[0].astype(MXU), d_b_gx[0].reshape(1, TOK),
               d_w_ga[0].astype(MXU), d_b_ga[0].reshape(1, TOK), full_small["d_a_param"]]}
    ones = jnp.ones((1, D), F32)
    zeros = jnp.zeros((1, D), F32)
    xs, gs, bs = [x2], [ones], [zeros]
    saved = []
    for i in range(DEPTH):
        ride = _Exchange([wt_sh[i + 1], wo_sh[i + 1]], [False, False]) if i + 1 < DEPTH else None
        res = _layer_fwd(i, i, xs[i], gs[i], bs[i], wt_full[i], wout_full[i], ks, vs, prm[i], ride)
        if ride:
            wt_full.append(res[-2].reshape(-1, D))
            wout_full.append(res[-1].reshape(D, D))
            res = res[:-2]
        saved.append(res)
        xs.append(res[1])
        gs.append(ln_g[i:i + 1])
        bs.append(ln_b[i:i + 1])

    up = tgt2
    grads = {}
    dks_l, dvs_l, dwt_l, dwout_l, dlng_l, dlnb_l = [], [], [], [], [], []
    recv_wt, recv_wo = [None] * DEPTH, [None] * DEPTH
    loss_part = None
    for i in reversed(range(DEPTH)):
        res = saved[i]
        extra = res[3] if len(res) > 3 else None
        ride = _Exchange([dwt_l[-1], dwout_l[-1]], [True, True]) if i + 1 < DEPTH else None
        out = _layer_bwd(i, i, up, i == DEPTH - 1, res[1], res[2], gs[i + 1], bs[i + 1], res[0], wout_full[i], ks, vs,
                         prm[i], extra, ride)
        if ride:
            recv_wt[i + 1], recv_wo[i + 1] = out[-2], out[-1]
            out = out[:-2]
        dres, dproj, dwout_i, dks_i, dvs_i, dg_i, db_i, loss_i = out[:8]
        pg = out[8:]
        if i == DEPTH - 1:
            loss_part = loss_i
        dks_l.append(dks_i)
        dvs_l.append(dvs_i)
        dwout_l.append(dwout_i)
        dlng_l.append(dg_i)
        dlnb_l.append(db_i)
        if i == 0:
            grads["a_w_s"], dbs_exp = pg
        elif i == 1:
            dlb, grads["b_norm_g"] = pg
        elif i == 2:
            dwbd, grads["c_scale"] = pg
            grads["c_w_pool"] = jnp.stack([lax.dynamic_slice(dwbd, (g * POOL_GROUP, g * POOL_GROUP), (POOL_GROUP, POOL_GROUP))
                                           for g in range(4)])
        else:
            (grads["d_conv_w"], grads["d_conv_b"], grads["d_w_gx"], grads["d_b_gx"], grads["d_w_ga"], grads["d_b_ga"],
             grads["d_a_param"]) = pg
        up, dwt = _proj_bwd(i, dproj, dres, xs[i], gs[i], bs[i], wt_full[i])
        dwt_l.append(dwt)
    grad_x = up[None]
    grads["ln_g"] = jnp.concatenate(dlng_l[::-1], axis=0)
    grads["ln_b"] = jnp.concatenate(dlnb_l[::-1], axis=0)
    dwkv = _kv_bwd(mem2, dks_l, dvs_l)
    grads["a_b_s"], grads["hgrn_lb_logits"] = _small_finalize(dbs_exp, dlb, hgrn_lb_logits)

    nsmall = sum(_size(s) for _, s, _ in SMALL) + 128
    small_rows = _rows_for(nsmall)
    small_vec = _pack([grads[n] for n, _, _ in SMALL] + [jnp.broadcast_to(loss_part[:, :1], (1, 128))], small_rows)
    recv_wt[0], recv_wo[0], recv_kv, recv_small = _Exchange(
        [dwt_l[-1], dwout_l[-1], dwkv, small_vec], [True, True, True, False]).run("scatter_last")

    outs = {}
    for t, n in enumerate(in_names):
        g, d, mn, vn = _sum_adam(f"adam_{n}", recv_wt[t], W[n][0], M[n][0], V[n][0], True)
        outs[n] = (g[None], d[None], mn[None], vn[None])
    wo_res = [_sum_adam(f"adam_w_out{l}", recv_wo[l], w_out[l], m_w_out[l], v_w_out[l], False) for l in range(DEPTH)]
    outs["w_out"] = tuple(jnp.stack([wo_res[l][j] for l in range(DEPTH)]) for j in range(4))
    outs["mem_kv_w"] = _sum_adam("adam_mem_kv_w", recv_kv, mem_kv_w, m_mem_kv_w, v_mem_kv_w, False)

    rep = [(n, s) for n, s, sh in SMALL if not sh]
    wrep = _pack([W[n] if sh is False else jnp.zeros(s, F32) for n, s, sh in SMALL] + [jnp.zeros((128,), F32)], small_rows)
    mrep = _pack([M[n] if sh is False else jnp.zeros(s, F32) for n, s, sh in SMALL] + [jnp.zeros((128,), F32)], small_rows)
    vrep = _pack([V[n] if sh is False else jnp.ones(s, F32) for n, s, sh in SMALL] + [jnp.ones((128,), F32)], small_rows)
    g_all, d_all, m_all, v_all = _small_sum_adam(recv_small, wrep, mrep, vrep)

    def unpack(vec):
        flat = vec.reshape(-1)
        res, o = {}, 0
        for n, s, _ in SMALL:
            res[n] = flat[o:o + _size(s)].reshape(s)
            o += _size(s)
        return res, flat[o]

    g_small, loss = unpack(g_all)
    d_small, _ = unpack(d_all)
    m_small, _ = unpack(m_all)
    v_small, _ = unpack(v_all)
    for n, s in rep:
        outs[n] = (g_small[n], d_small[n], m_small[n], v_small[n])
    per = TOK // NDEV
    g_sh = {n: lax.dynamic_slice_in_dim(g_small[n], me * per, per, axis=len(s) - 1) for n, s, sh in SMALL if sh}
    gp = _pack([g_sh[n] for n in shard_names], 8)
    d_p, m_p, v_p = _adam_only(gp, small_shard, _pack([M[n] for n in shard_names], 8), _pack([V[n] for n in shard_names], 8))
    o = 0
    for n in shard_names:
        cnt = _size(W[n].shape)
        outs[n] = (g_sh[n],) + tuple(t.reshape(-1)[o:o + cnt].reshape(W[n].shape) for t in (d_p, m_p, v_p))
        o += cnt

    order = ["mem_kv_w", "ln_g", "ln_b", "w_out", "hgrn_lb_logits", "a_w_in", "a_w_s", "a_b_s", "b_w_in", "b_norm_g", "c_w_in",
             "c_w_pool", "c_scale", "d_w_in", "d_conv_w", "d_conv_b", "d_w_gx", "d_b_gx", "d_w_ga", "d_b_ga", "d_a_param"]
    result = [loss, grad_x]
    for j in range(4):
        result += [outs[n][j].reshape(W[n].shape) for n in order]
    return tuple(result)
```

```python
import functools

import jax
import jax.numpy as jnp
from jax import lax
from jax.experimental import pallas as pl
from jax.experimental.pallas import tpu as pltpu

F32 = jnp.float32
MXU = jnp.bfloat16
WIRE = jnp.bfloat16

D = 1024
TOK = 768
XW = 256
NMEM = 256
XHEADS = 4
XSCALE = 64 ** -0.5
NH = 6
HD = 128
CH = 16
POOL_WINDOWS = (2, 4, 8, 16)
POOL_GROUP = 192
DEPTH = 4
ALPHA = (2 * DEPTH) ** 0.25
LN_EPS = 1e-5
RMS_EPS = 1e-6
LRU_C = 8.0
B1, B2, LR, EPS, WD, STEP = 0.9, 0.999, 0.001, 1e-8, 0.01, 10

NDEV = 8
TS = 256
TSB = 512
VMEM_LIMIT = 58 * 1024 * 1024

KIND_WIDTHS = {0: 2 * TOK + XW + D, 1: 3 * TOK + XW + D, 2: TOK + XW + D, 3: TOK + XW + D}


def _mm(a, b, ca, cb):
    return lax.dot_general(a.astype(MXU), b.astype(MXU), (((ca,), (cb,)), ((), ())), preferred_element_type=F32)


def _nn(a, b):
    return _mm(a, b, 1, 0)


def _nt(a, b):
    return _mm(a, b, 1, 1)


def _tn(a, b):
    return _mm(a, b, 0, 0)


def _bmm(a, b, ca, cb):
    return lax.dot_general(a.astype(MXU), b.astype(MXU), (((ca,), (cb,)), ((0,), (0,))), preferred_element_type=F32)


def _sigmoid(x):
    return 1.0 / (1.0 + jnp.exp(-x))


def _vjp1(fn, x, dy):
    return jax.vjp(fn, x)[1](dy)[0]


def _rowsum(x):
    return jnp.sum(x, axis=0, keepdims=True)


def _row(x, r):
    sel = lax.broadcasted_iota(jnp.int32, x.shape, 0) == r
    return jnp.sum(jnp.where(sel, x, 0.0), axis=0, keepdims=True)


def _acc(ref, val):
    ref[...] += val


def _cparams(sem=None):
    return pltpu.CompilerParams(dimension_semantics=sem, vmem_limit_bytes=VMEM_LIMIT)


def _res(a):
    nd = a.ndim
    return pl.BlockSpec(a.shape, lambda i: (0,) * nd)


def _res_sds(shape):
    nd = len(shape)
    return pl.BlockSpec(shape, lambda i: (0,) * nd)


def _rows(width, nt, rev, ts=TS):
    if rev:
        return pl.BlockSpec((ts, width), lambda i: (nt - 1 - i, 0))
    return pl.BlockSpec((ts, width), lambda i: (i, 0))


def _call(body, name, grid, ins, outs, scratch=(), sem=("arbitrary",)):
    arrays = [a for a, _ in ins]
    return pl.pallas_call(
        body, name=name, grid=grid,
        in_specs=[s for _, s in ins],
        out_specs=[s for _, s in outs],
        out_shape=[o for o, _ in outs],
        scratch_shapes=list(scratch),
        compiler_params=_cparams(sem),
    )(*arrays)


def _xattn_fwd(qx, ks_ref, vs_ref):
    o = None
    ps = []
    for h in range(XHEADS):
        s = _nt(qx, ks_ref[h]) * XSCALE
        s = s - jnp.max(s, axis=-1, keepdims=True)
        e = jnp.exp(s)
        p = e / jnp.sum(e, axis=-1, keepdims=True)
        ps.append(p)
        oh = _nn(p, vs_ref[h])
        o = oh if o is None else o + oh
    return o, ps


def _xattn_bwd(qx, ps, dxo, ks_ref, vs_ref, dks_ref, dvs_ref):
    dq = None
    for h in range(XHEADS):
        p = ps[h]
        dp = _nt(dxo, vs_ref[h])
        ds = p * (dp - jnp.sum(dp * p, axis=-1, keepdims=True))
        dqh = _nn(ds, ks_ref[h]) * XSCALE
        dq = dqh if dq is None else dq + dqh
        dks_ref[h] += _tn(ds, qx) * XSCALE
        dvs_ref[h] += _tn(p, dxo)
    return dq


def _tril128():
    r = lax.broadcasted_iota(jnp.int32, (HD, HD), 0)
    c = lax.broadcasted_iota(jnp.int32, (HD, HD), 1)
    return c <= r


def _gmlp_fwd(u, v, ws_ref, bs_ref):
    ts = u.shape[0]
    ug = jax.nn.gelu(u)
    vg = jax.nn.gelu(v)
    tri = _tril128()
    toks, res = [], []
    for g in range(NH):
        sl = slice(g * HD, (g + 1) * HD)
        vgh = vg[:, sl]
        cen = vgh - jnp.mean(vgh, axis=-1, keepdims=True)
        rstd = lax.rsqrt(jnp.mean(cen * cen, axis=-1, keepdims=True) + LN_EPS)
        vn = cen * rstd
        w = jnp.where(tri, ws_ref[g], 0.0).astype(MXU)
        mix = jnp.concatenate([_nn(w, vn[n * HD:(n + 1) * HD]) + bs_ref[g] for n in range(ts // HD)], axis=0)
        toks.append(ug[:, sl] * mix)
        res.append((vn, rstd, mix, w))
    return jnp.concatenate(toks, axis=1), (ug, res)


def _gmlp_bwd(u, v, fres, dtok, dws_ref, dbs_ref):
    ts = u.shape[0]
    ug, res = fres
    tri = _tril128()
    dugs, dvgs = [], []
    for g in range(NH):
        sl = slice(g * HD, (g + 1) * HD)
        vn, rstd, mix, w = res[g]
        dmix = dtok[:, sl] * ug[:, sl]
        dugs.append(dtok[:, sl] * mix)
        dvn_rows = []
        dw = None
        dbs = None
        for n in range(ts // HD):
            dm = dmix[n * HD:(n + 1) * HD]
            dvn_rows.append(_tn(w, dm))
            t = _nt(dm, vn[n * HD:(n + 1) * HD])
            dw = t if dw is None else dw + t
            dbs = dm if dbs is None else dbs + dm
        dws_ref[g] += jnp.where(tri, dw, 0.0)
        dbs_ref[g] += dbs
        dvn = jnp.concatenate(dvn_rows, axis=0)
        dvgs.append(rstd * (dvn - jnp.mean(dvn, axis=-1, keepdims=True) - vn * jnp.mean(dvn * vn, axis=-1, keepdims=True)))
    du = _vjp1(jax.nn.gelu, u, jnp.concatenate(dugs, axis=1))
    dv = _vjp1(jax.nn.gelu, v, jnp.concatenate(dvgs, axis=1))
    return du, dv


def _chunk_cumsum(x):
    row = lax.broadcasted_iota(jnp.int32, x.shape, 0) % CH
    for s in (1, 2, 4, 8):
        x = x + jnp.where(row >= s, pltpu.roll(x, s, 0), 0.0)
    return x


def _chunk_revcumsum(x):
    n = x.shape[0]
    row = lax.broadcasted_iota(jnp.int32, x.shape, 0) % CH
    for s in (1, 2, 4, 8):
        x = x + jnp.where(row < CH - s, pltpu.roll(x, n - s, 0), 0.0)
    return x


def _chunk_sum(x):
    n, w = x.shape
    return jnp.sum(x.reshape(n // CH, CH, w), axis=1)


def _chunk_bcast(c, n):
    nch, w = c.shape
    return jnp.broadcast_to(c[:, None, :], (nch, CH, w)).reshape(n, w)


def _lower_bound(lb_logits, layer):
    lg = lb_logits
    e = jnp.exp(lg - jnp.max(lg, axis=0, keepdims=True))
    p = e / jnp.sum(e, axis=0, keepdims=True)
    row = lax.broadcasted_iota(jnp.int32, p.shape, 0)
    lb = jnp.sum(jnp.where((row >= 1) & (row <= layer), p, 0.0), axis=0, keepdims=True)
    return lb, p


def _hgrn_prep(q, fl, lb):
    n = q.shape[0]
    sg = _sigmoid(fl)
    f = lb + (1.0 - lb) * sg
    lf = jnp.log(f)
    k = 1.0 - f
    sq = _sigmoid(q)
    qf = q * sq
    g = _chunk_cumsum(lf)
    tot = _chunk_sum(lf)
    gl = _chunk_bcast(tot, n)
    eg = jnp.exp(g)
    eng = jnp.exp(-g)
    egl = jnp.exp(gl - g)
    return dict(sg=sg, f=f, k=k, sq=sq, qf=qf, eg=eg, eng=eng, egl=egl,
                qd=qf * eg, ki=k * eng, ke=k * egl, dch=jnp.exp(tot))


def _hgrn_mask():
    r = lax.broadcasted_iota(jnp.int32, (HD, HD), 0)
    c = lax.broadcasted_iota(jnp.int32, (HD, HD), 1)
    return (r // CH == c // CH) & (c <= r)


def _hgrn_states(h, v3, ke3, dch_h, st_in, sts_s, ut_s, dch_s):
    nch = v3.shape[0]
    ut_s[...] = _bmm(v3, ke3, 1, 1)
    dch_s[...] = dch_h

    def step(c, st):
        sts_s[c] = st
        return st * dch_s[pl.ds(c, 1), :] + ut_s[c]

    return lax.fori_loop(0, nch, step, st_in)


def _hgrn_fwd(q, fl, inp, lb, ng, st_ref, sts_s, ut_s, dch_s):
    n = q.shape[0]
    nch = n // CH
    pr = _hgrn_prep(q, fl, lb)
    mask = _hgrn_mask()
    toks = []
    for h in range(NH):
        sl = slice(h * HD, (h + 1) * HD)
        qd, ki, ke, v = pr["qd"][:, sl], pr["ki"][:, sl], pr["ke"][:, sl], inp[:, sl]
        qd3 = qd.astype(MXU).reshape(nch, CH, HD)
        v3 = v.astype(MXU).reshape(nch, CH, HD)
        ke3 = ke.astype(MXU).reshape(nch, CH, HD)
        st_ref[h] = _hgrn_states(h, v3, ke3, pr["dch"][:, sl], st_ref[h], sts_s, ut_s, dch_s)
        o = _bmm(qd3, sts_s[...], 2, 2).reshape(n, HD)
        intra = []
        for b in range(n // HD):
            bs = slice(b * HD, (b + 1) * HD)
            a = jnp.where(mask, _nt(qd[bs], ki[bs]), 0.0)
            intra.append(_nn(a, v[bs]))
        o = o + jnp.concatenate(intra, axis=0)
        r = lax.rsqrt(jnp.mean(o * o, axis=-1, keepdims=True) + RMS_EPS)
        toks.append(o * r * ng[:, sl])
    return jnp.concatenate(toks, axis=1)


def _hgrn_bwd(q, fl, inp, lb, ng, dtok, ststart_ref, dst_ref, sts_s, ut_s, dch_s, dstn_s, dng_ref, dlb_ref):
    n = q.shape[0]
    nch = n // CH
    pr = _hgrn_prep(q, fl, lb)
    mask = _hgrn_mask()
    dqd_l, dki_l, dke_l, dv_l, ddch_l, dng_l, toks = [], [], [], [], [], [], []
    for h in range(NH):
        sl = slice(h * HD, (h + 1) * HD)
        qd, ki, ke, v = pr["qd"][:, sl], pr["ki"][:, sl], pr["ke"][:, sl], inp[:, sl]
        qd3 = qd.astype(MXU).reshape(nch, CH, HD)
        v3 = v.astype(MXU).reshape(nch, CH, HD)
        ke3 = ke.astype(MXU).reshape(nch, CH, HD)
        _hgrn_states(h, v3, ke3, pr["dch"][:, sl], ststart_ref[h], sts_s, ut_s, dch_s)
        sts = sts_s[...]
        o = _bmm(qd3, sts, 2, 2).reshape(n, HD)
        a_l = []
        intra = []
        for b in range(n // HD):
            bs = slice(b * HD, (b + 1) * HD)
            a = jnp.where(mask, _nt(qd[bs], ki[bs]), 0.0)
            a_l.append(a)
            intra.append(_nn(a, v[bs]))
        o = o + jnp.concatenate(intra, axis=0)
        r = lax.rsqrt(jnp.mean(o * o, axis=-1, keepdims=True) + RMS_EPS)
        toks.append(o * r * ng[:, sl])
        dt = dtok[:, sl]
        dng_l.append(_rowsum(dt * o * r))
        dn = dt * ng[:, sl]
        do = r * dn - o * (r * r * r) * jnp.mean(dn * o, axis=-1, keepdims=True)
        do3 = do.astype(MXU).reshape(nch, CH, HD)
        dqd_rows, dki_rows, dv_rows = [], [], []
        for b in range(n // HD):
            bs = slice(b * HD, (b + 1) * HD)
            da = jnp.where(mask, _nt(do[bs], v[bs]), 0.0)
            dqd_rows.append(_nn(da, ki[bs]))
            dki_rows.append(_tn(da, qd[bs]))
            dv_rows.append(_tn(a_l[b], do[bs]))
        dqd = jnp.concatenate(dqd_rows, axis=0) + _bmm(do3, sts, 2, 1).reshape(n, HD)
        dki = jnp.concatenate(dki_rows, axis=0)
        dv = jnp.concatenate(dv_rows, axis=0)
        ut_s[...] = _bmm(do3, qd3, 1, 1)

        def step(i, dst):
            c = nch - 1 - i
            dstn_s[c] = dst
            return ut_s[c] + dst * dch_s[pl.ds(c, 1), :]

        dst_ref[h] = lax.fori_loop(0, nch, step, dst_ref[h])
        dstn = dstn_s[...]
        dv = dv + _bmm(ke3, dstn, 2, 2).reshape(n, HD)
        dke = _bmm(v3, dstn, 2, 1).reshape(n, HD)
        ddch_l.append(jnp.sum(sts * dstn, axis=1))
        dqd_l.append(dqd)
        dki_l.append(dki)
        dke_l.append(dke)
        dv_l.append(dv)
    dqd = jnp.concatenate(dqd_l, axis=1)
    dki = jnp.concatenate(dki_l, axis=1)
    dke = jnp.concatenate(dke_l, axis=1)
    dinp = jnp.concatenate(dv_l, axis=1)
    ddch = jnp.concatenate(ddch_l, axis=1)
    _acc(dng_ref, jnp.concatenate(dng_l, axis=1))
    dqf = dqd * pr["eg"]
    dke_ke = dke * pr["ke"]
    dg = dqd * pr["qd"] - dki * pr["ki"] - dke_ke
    dk = dki * pr["eng"] + dke * pr["egl"]
    dgl = _chunk_sum(dke_ke) + ddch * pr["dch"]
    dlf = _chunk_revcumsum(dg) + _chunk_bcast(dgl, n)
    df = dlf / pr["f"] - dk
    sg = pr["sg"]
    dfl = df * (1.0 - lb) * sg * (1.0 - sg)
    _acc(dlb_ref, _rowsum(df * (1.0 - sg)))
    sq = pr["sq"]
    dq = dqf * (sq * (1.0 + q * (1.0 - sq)))
    return jnp.concatenate(toks, axis=1), dq, dfl, dinp


def _pool_select(s2, s4, s8, s16):
    col = lax.broadcasted_iota(jnp.int32, (1, TOK), 1)
    return jnp.where(col < POOL_GROUP, s2, jnp.where(col < 2 * POOL_GROUP, s4, jnp.where(col < 3 * POOL_GROUP, s8, s16)))


def _pool_cnt(pos0, n):
    pos = pos0 + lax.broadcasted_iota(jnp.int32, (n, TOK), 0) + 1
    col = lax.broadcasted_iota(jnp.int32, (n, TOK), 1)
    w = jnp.where(col < POOL_GROUP, 2, jnp.where(col < 2 * POOL_GROUP, 4, jnp.where(col < 3 * POOL_GROUP, 8, 16)))
    return jnp.minimum(pos, w).astype(F32)


def _pool_fwd(p, halo, pos0, wbd, scale):
    n = p.shape[0]
    ext = jnp.concatenate([halo, p], axis=0)
    s2 = ext + pltpu.roll(ext, 1, 0)
    s4 = s2 + pltpu.roll(s2, 2, 0)
    s8 = s4 + pltpu.roll(s4, 4, 0)
    s16 = s8 + pltpu.roll(s8, 8, 0)
    win = _pool_select(s2, s4, s8, s16)[16:]
    cnt = _pool_cnt(pos0, n)
    diff = win / cnt - p
    y = _nn(diff, wbd)
    return y * scale, (diff, y, cnt)


def _pool_bwd(fres, dtok, nxt_ref, wbd, scale, dwbd_ref, dscale_ref):
    diff, y, cnt = fres
    n = diff.shape[0]
    _acc(dscale_ref, _rowsum(dtok * y))
    dy = dtok * scale
    ddiff = _nt(dy, wbd)
    dwbd_ref[...] += _tn(diff, dy)
    qv = ddiff / cnt
    ext = jnp.concatenate([qv, nxt_ref[...]], axis=0)
    m = n + 16
    s2 = ext + pltpu.roll(ext, m - 1, 0)
    s4 = s2 + pltpu.roll(s2, m - 2, 0)
    s8 = s4 + pltpu.roll(s4, m - 4, 0)
    s16 = s8 + pltpu.roll(s8, m - 8, 0)
    adj = _pool_select(s2, s4, s8, s16)[:n]
    nxt_ref[...] = qv[:16]
    return adj - ddiff


def _neg_expm1(x):
    return jnp.where(jnp.abs(x) < 1e-2, -x * (1.0 + x * (0.5 + x * (1.0 / 6.0))), 1.0 - jnp.exp(x))


def _lru_gates(xc, zx, za, ap, first):
    gx = _sigmoid(zx)
    ga = _sigmoid(za)
    sp = jnp.maximum(-ap, 0.0) + jnp.log(1.0 + jnp.exp(-jnp.abs(ap)))
    log_a = -LRU_C * ga * sp
    a = jnp.exp(log_a)
    mult = jnp.sqrt(_neg_expm1(2.0 * log_a))
    mult = jnp.where(first, 1.0, mult)
    return a, mult * gx * xc


SUB = 8


def _scan_fwd(a, b, h0, scan_s):
    a_s, b_s, h_s = scan_s
    n = a.shape[0]
    row = lax.broadcasted_iota(jnp.int32, a.shape, 0) % SUB
    for s in (1, 2, 4):
        keep = row >= s
        b = b + a * jnp.where(keep, pltpu.roll(b, s, 0), 0.0)
        a = a * jnp.where(keep, pltpu.roll(a, s, 0), 1.0)
    a_s[...] = a
    b_s[...] = b

    def step(g, carry):
        r = pl.multiple_of(g * SUB, SUB)
        h_s[pl.ds(r, SUB), :] = b_s[pl.ds(r, SUB), :] + a_s[pl.ds(r, SUB), :] * carry
        return h_s[pl.ds(r + SUB - 1, 1), :]

    lax.fori_loop(0, n // SUB, step, h0, unroll=4)
    return h_s[...]


def _scan_bwd(an, d, dh_next, scan_s):
    a_s, b_s, h_s = scan_s
    n = an.shape[0]
    row = lax.broadcasted_iota(jnp.int32, an.shape, 0) % SUB
    for s in (1, 2, 4):
        keep = row < SUB - s
        d = d + an * jnp.where(keep, pltpu.roll(d, n - s, 0), 0.0)
        an = an * jnp.where(keep, pltpu.roll(an, n - s, 0), 1.0)
    a_s[...] = an
    b_s[...] = d

    def step(i, carry):
        r = pl.multiple_of((n // SUB - 1 - i) * SUB, SUB)
        h_s[pl.ds(r, SUB), :] = b_s[pl.ds(r, SUB), :] + a_s[pl.ds(r, SUB), :] * carry
        return h_s[pl.ds(r, 1), :]

    lax.fori_loop(0, n // SUB, step, dh_next, unroll=4)
    return h_s[...]


def _lru_conv(xb, halo, cw_ref, cb):
    ext = jnp.concatenate([halo, xb], axis=0)
    sh = [pltpu.roll(ext, 3 - j, 0)[8:] if j < 3 else xb for j in range(4)]
    xc = cb
    for j in range(4):
        xc = xc + cw_ref[pl.ds(j, 1), :] * sh[j]
    return xc, sh


def _lru_fwd(xb, halo, pos0, prm, h0, scan_s):
    cw, cb, wgx, bgx, wga, bga, ap = prm
    n = xb.shape[0]
    xc, sh = _lru_conv(xb, halo, cw, cb[...])
    zx = jnp.concatenate([_nn(xc[:, h * HD:(h + 1) * HD], wgx[h]) for h in range(NH)], axis=1) + bgx[...]
    za = jnp.concatenate([_nn(xc[:, h * HD:(h + 1) * HD], wga[h]) for h in range(NH)], axis=1) + bga[...]
    first = (pos0 + lax.broadcasted_iota(jnp.int32, (n, 1), 0)) == 0
    a, b = _lru_gates(xc, zx, za, ap[...], first)
    hseq = _scan_fwd(a, b, h0, scan_s)
    return hseq, (xc, sh, zx, za, first, a)


def _lru_bwd(fres, hseq, h0, dtok, prm, carry_refs, grad_refs):
    cw, cb, wgx, bgx, wga, bga, ap = prm
    xc, sh, zx, za, first, a = fres
    anext_ref, dhnext_ref, dxcnext_ref = carry_refs[:3]
    scan_s = (carry_refs[3], carry_refs[4], carry_refs[6])
    dcw_ref, dcb_ref, dwgx_ref, dbgx_ref, dwga_ref, dbga_ref, dap_ref = grad_refs
    n = xc.shape[0]
    an = jnp.where(lax.broadcasted_iota(jnp.int32, a.shape, 0) == n - 1, anext_ref[...], pltpu.roll(a, n - 1, 0))
    dh = _scan_bwd(an, dtok, dhnext_ref[...], scan_s)
    hprev = jnp.where(lax.broadcasted_iota(jnp.int32, hseq.shape, 0) == 0, h0, pltpu.roll(hseq, 1, 0))
    da = dh * hprev
    anext_ref[...] = _row(a, 0)
    dhnext_ref[...] = _row(dh, 0)
    _, vjp = jax.vjp(lambda xc_, zx_, za_, ap_: _lru_gates(xc_, zx_, za_, ap_, first), xc, zx, za, ap[...])
    dxc, dzx, dza, dap = vjp((da, dh))
    _acc(dap_ref, dap)
    _acc(dbgx_ref, _rowsum(dzx))
    _acc(dbga_ref, _rowsum(dza))
    parts = []
    for h in range(NH):
        sl = slice(h * HD, (h + 1) * HD)
        parts.append(_nt(dzx[:, sl], wgx[h]) + _nt(dza[:, sl], wga[h]))
        dwgx_ref[h] += _tn(xc[:, sl], dzx[:, sl])
        dwga_ref[h] += _tn(xc[:, sl], dza[:, sl])
    dxc = dxc + jnp.concatenate(parts, axis=1)
    _acc(dcb_ref, _rowsum(dxc))
    for j in range(4):
        dcw_ref[pl.ds(j, 1), :] += _rowsum(dxc * sh[j])
    ext = jnp.concatenate([dxc, dxcnext_ref[...]], axis=0)
    m = n + 8
    dxb = cw[pl.ds(3, 1), :] * dxc
    for j in range(3):
        dxb = dxb + cw[pl.ds(j, 1), :] * pltpu.roll(ext, m - (3 - j), 0)[:n]
    dxcnext_ref[...] = dxc[:8]
    return dxb


def _layer_fwd(kind, layer, xprev, gprev, bprev, wt, wout, ks, vs, prm, ride=None):
    S = xprev.shape[0]
    nt = S // TS
    N = wt.shape[0]
    nprm = len(prm)
    nch = TS // CH

    outs = [(jax.ShapeDtypeStruct((S, N), F32), _rows(N, nt, False)),
            (jax.ShapeDtypeStruct((S, D), F32), _rows(D, nt, False)),
            (jax.ShapeDtypeStruct((S, 1), F32), _rows(1, nt, False))]
    scratch = []
    if kind == 1:
        outs.append((jax.ShapeDtypeStruct((nt, NH, HD, HD), F32), pl.BlockSpec((None, NH, HD, HD), lambda i: (i, 0, 0, 0))))
        scratch = [pltpu.VMEM((NH, HD, HD), F32), pltpu.VMEM((nch, HD, HD), F32), pltpu.VMEM((nch, HD, HD), F32),
                   pltpu.VMEM((nch, HD), F32)]
    elif kind == 2:
        scratch = [pltpu.VMEM((16, TOK), F32)]
    elif kind == 3:
        outs.append((jax.ShapeDtypeStruct((nt * 8, TOK), F32), pl.BlockSpec((8, TOK), lambda i: (i, 0))))
        scratch = [pltpu.VMEM((8, TOK), F32), pltpu.VMEM((1, TOK), F32)] + [pltpu.VMEM((TS, TOK), F32)] * 3
    nout = len(outs)
    nscr = len(scratch)
    nride = len(ride.arrays) if ride else 0

    def body(*refs):
        x_ref, g_ref, b_ref, wt_ref, wout_ref, ks_ref, vs_ref = refs[:7]
        prm_refs = refs[7:7 + nprm]
        nin = 7 + nprm + nride
        ride_src = refs[7 + nprm:nin]
        out_refs = refs[nin:nin + nout]
        ride_dst = refs[nin + nout:nin + nout + nride]
        scr = refs[nin + nout + nride:nin + nout + nride + nscr]
        ride_sems = refs[nin + nout + nride + nscr:]
        proj_ref, xhat_ref, rstd_ref = out_refs[:3]
        i = pl.program_id(0)
        if ride:
            @pl.when(i == 0)
            def _():
                ride.start(ride_src, ride_dst, ride_sems)

        xin = x_ref[...] * g_ref[...] + b_ref[...]
        proj = _nt(xin, wt_ref[...])
        proj_ref[...] = proj
        if kind == 0:
            tok, _ = _gmlp_fwd(proj[:, :TOK], proj[:, TOK:2 * TOK], prm_refs[0], prm_refs[1])
        elif kind == 1:
            st_ref, sts_s, ut_s, dch_s = scr

            @pl.when(i == 0)
            def _():
                st_ref[...] = jnp.zeros_like(st_ref)

            out_refs[3][...] = st_ref[...]
            lb, _ = _lower_bound(prm_refs[0][...], layer)
            tok = _hgrn_fwd(proj[:, :TOK], proj[:, TOK:2 * TOK], proj[:, 2 * TOK:3 * TOK], lb, prm_refs[1][...],
                            st_ref, sts_s, ut_s, dch_s)
        elif kind == 2:
            halo_ref, = scr

            @pl.when(i == 0)
            def _():
                halo_ref[...] = jnp.zeros_like(halo_ref)

            p = proj[:, :TOK]
            tok, _ = _pool_fwd(p, halo_ref[...], i * TS, prm_refs[0][...], prm_refs[1][...])
            halo_ref[...] = p[TS - 16:]
        else:
            halo_ref, h_ref = scr[:2]

            @pl.when(i == 0)
            def _():
                halo_ref[...] = jnp.zeros_like(halo_ref)
                h_ref[...] = jnp.zeros_like(h_ref)

            out_refs[3][...] = jnp.broadcast_to(h_ref[...], (8, TOK))
            xb = proj[:, :TOK]
            tok, _ = _lru_fwd(xb, halo_ref[...], i * TS, prm_refs, h_ref[...], scr[2:])
            halo_ref[...] = xb[TS - 8:]
            h_ref[...] = _row(tok, TS - 1)
        qx = proj[:, N - D - XW:N - D]
        gate = proj[:, N - D:]
        xo, _ = _xattn_fwd(qx, ks_ref, vs_ref)
        mixed = jnp.concatenate([tok, xo], axis=1) * (gate * _sigmoid(gate))
        z = ALPHA * xin + _nn(mixed, wout_ref[...])
        cen = z - jnp.mean(z, axis=-1, keepdims=True)
        rstd = lax.rsqrt(jnp.mean(cen * cen, axis=-1, keepdims=True) + LN_EPS)
        xhat_ref[...] = cen * rstd
        rstd_ref[...] = rstd
        if ride:
            @pl.when(i == nt - 1)
            def _():
                ride.wait(ride_src, ride_dst, ride_sems)

    ins = [(xprev, _rows(D, nt, False)), (gprev, _res(gprev)), (bprev, _res(bprev)), (wt, _res(wt)), (wout, _res(wout)),
           (ks, _res(ks)), (vs, _res(vs))] + [(p, _res(p)) for p in prm]
    if ride:
        ins += [(a, _ANY) for a in ride.arrays]
        outs += [(s, _ANY) for s in ride.out_shapes]
        scratch = scratch + ride.scratch
    return _call(body, f"layer{layer}_fwd", (nt,), ins, outs, scratch)


def _layer_bwd(kind, layer, up, is_last, xhat, rstd, g_i, b_i, proj, wout, ks, vs, prm, extra, ride=None):
    S = xhat.shape[0]
    nt = S // TS
    N = proj.shape[1]
    nprm = len(prm)
    nch = TS // CH

    ins = [(up, _rows(D, nt, True)), (xhat, _rows(D, nt, True)), (rstd, _rows(1, nt, True)), (g_i, _res(g_i)), (b_i, _res(b_i)),
           (proj, _rows(N, nt, True)), (wout, _res(wout)), (ks, _res(ks)), (vs, _res(vs))] + [(p, _res(p)) for p in prm]
    nfixed = 9
    if kind == 1:
        ins.append((extra, pl.BlockSpec((None, NH, HD, HD), lambda i: (nt - 1 - i, 0, 0, 0))))
    elif kind == 2:
        hb = TS // 16
        ins.append((proj, pl.BlockSpec((16, TOK), lambda i: (jnp.maximum((nt - 1 - i) * hb - 1, 0), 0))))
    elif kind == 3:
        hb = TS // 8
        ins.append((proj, pl.BlockSpec((8, TOK), lambda i: (jnp.maximum((nt - 1 - i) * hb - 1, 0), 0))))
        ins.append((extra, pl.BlockSpec((8, TOK), lambda i: (nt - 1 - i, 0))))
    nin = len(ins)

    def acc(shape):
        return (jax.ShapeDtypeStruct(shape, F32), _res_sds(shape))

    outs = [(jax.ShapeDtypeStruct((S, D), F32), _rows(D, nt, True)),
            (jax.ShapeDtypeStruct((S, N), MXU), _rows(N, nt, True)),
            (jax.ShapeDtypeStruct((D, D), WIRE), _res_sds((D, D))),
            acc((XHEADS, NMEM, XW)), acc((XHEADS, NMEM, XW)), acc((1, D)), acc((1, D)), acc((1, HD))]
    scratch = []
    if kind == 0:
        outs += [acc((NH, HD, HD)), acc((NH, HD, HD))]
    elif kind == 1:
        outs += [acc((1, TOK)), acc((1, TOK))]
        scratch = [pltpu.VMEM((NH, HD, HD), F32), pltpu.VMEM((nch, HD, HD), F32), pltpu.VMEM((nch, HD, HD), F32),
                   pltpu.VMEM((nch, HD), F32), pltpu.VMEM((nch, HD, HD), F32)]
    elif kind == 2:
        outs += [acc((TOK, TOK)), acc((1, TOK))]
        scratch = [pltpu.VMEM((16, TOK), F32)]
    else:
        outs += [acc((4, TOK)), acc((1, TOK)), acc((NH, HD, HD)), acc((1, TOK)), acc((NH, HD, HD)), acc((1, TOK)), acc((1, TOK))]
        scratch = [pltpu.VMEM((1, TOK), F32), pltpu.VMEM((1, TOK), F32), pltpu.VMEM((8, TOK), F32)] + [pltpu.VMEM((TS, TOK), F32)] * 4
    scratch = scratch + [pltpu.VMEM((D, D), F32)]
    nout = len(outs)
    nscr = len(scratch)
    nride = len(ride.arrays) if ride else 0

    def body(*refs):
        up_ref, xhat_ref, rstd_ref, g_ref, b_ref, proj_ref, wout_ref, ks_ref, vs_ref = refs[:nfixed]
        prm_refs = refs[nfixed:nfixed + nprm]
        ext_refs = refs[nfixed + nprm:nin]
        ride_src = refs[nin:nin + nride]
        o0 = nin + nride
        out_refs = refs[o0:o0 + nout]
        ride_dst = refs[o0 + nout:o0 + nout + nride]
        scr = refs[o0 + nout + nride:o0 + nout + nride + nscr - 1]
        dwout_acc = refs[o0 + nout + nride + nscr - 1]
        ride_sems = refs[o0 + nout + nride + nscr:]
        dres_ref, dproj_ref, dwout_ref, dks_ref, dvs_ref, dg_ref, db_ref, loss_ref = out_refs[:8]
        pgrad = out_refs[8:]
        i = pl.program_id(0)
        tile = nt - 1 - i

        @pl.when(i == 0)
        def _():
            if ride:
                ride.start(ride_src, ride_dst, ride_sems)
            for r in out_refs[3:]:
                r[...] = jnp.zeros_like(r)
            dwout_acc[...] = jnp.zeros_like(dwout_acc)
            for r in scr:
                if kind != 1 or r is scr[0]:
                    r[...] = jnp.zeros_like(r)

        xhat_v = xhat_ref[...]
        if is_last:
            err = xhat_v * g_ref[...] + b_ref[...] - up_ref[...]
            dxo = err * (1.0 / D)
            loss_ref[...] += jnp.sum(0.5 * jnp.mean(err * err, axis=-1, keepdims=True), axis=0, keepdims=True)
        else:
            dxo = up_ref[...]
        _acc(dg_ref, _rowsum(dxo * xhat_v))
        _acc(db_ref, _rowsum(dxo))
        dxh = dxo * g_ref[...]
        dz = rstd_ref[...] * (dxh - jnp.mean(dxh, axis=-1, keepdims=True)
                              - xhat_v * jnp.mean(dxh * xhat_v, axis=-1, keepdims=True))
        dres_ref[...] = ALPHA * dz

        proj = proj_ref[...]
        qx = proj[:, N - D - XW:N - D]
        gate = proj[:, N - D:]
        sgate = _sigmoid(gate)
        silu = gate * sgate
        dmixed = _nt(dz, wout_ref[...])
        dcat = dmixed * silu
        dtok = dcat[:, :TOK]
        if kind == 0:
            u, v = proj[:, :TOK], proj[:, TOK:2 * TOK]
            tok, fres = _gmlp_fwd(u, v, prm_refs[0], prm_refs[1])
            du, dv = _gmlp_bwd(u, v, fres, dtok, pgrad[0], pgrad[1])
            dproj_ref[:, :TOK] = du.astype(MXU)
            dproj_ref[:, TOK:2 * TOK] = dv.astype(MXU)
        elif kind == 1:
            dst_ref, sts_s, ut_s, dch_s, dstn_s = scr
            lb, _ = _lower_bound(prm_refs[0][...], layer)
            q, fl, inp = proj[:, :TOK], proj[:, TOK:2 * TOK], proj[:, 2 * TOK:3 * TOK]
            tok, dq, dfl, dinp = _hgrn_bwd(q, fl, inp, lb, prm_refs[1][...], dtok, ext_refs[0], dst_ref, sts_s, ut_s, dch_s,
                                           dstn_s, pgrad[1], pgrad[0])
            dproj_ref[:, :TOK] = dq.astype(MXU)
            dproj_ref[:, TOK:2 * TOK] = dfl.astype(MXU)
            dproj_ref[:, 2 * TOK:3 * TOK] = dinp.astype(MXU)
        elif kind == 2:
            p = proj[:, :TOK]
            halo = jnp.where(tile == 0, 0.0, ext_refs[0][...])
            tok, fres = _pool_fwd(p, halo, tile * TS, prm_refs[0][...], prm_refs[1][...])
            dp = _pool_bwd(fres, dtok, scr[0], prm_refs[0][...], prm_refs[1][...], pgrad[0], pgrad[1])
            dproj_ref[:, :TOK] = dp.astype(MXU)
        else:
            xb = proj[:, :TOK]
            halo = jnp.where(tile == 0, 0.0, ext_refs[0][...])
            h0 = ext_refs[1][0:1]
            tok, fres = _lru_fwd(xb, halo, tile * TS, prm_refs, h0, scr[3:6])
            dxb = _lru_bwd(fres, tok, h0, dtok, prm_refs, scr, pgrad)
            dproj_ref[:, :TOK] = dxb.astype(MXU)
        xo, ps = _xattn_fwd(qx, ks_ref, vs_ref)
        dqx = _xattn_bwd(qx, ps, dcat[:, TOK:], ks_ref, vs_ref, dks_ref, dvs_ref)
        cat = jnp.concatenate([tok, xo], axis=1)
        dwout_acc[...] += _tn(cat * silu, dz)
        dgate = dmixed * cat * (sgate * (1.0 + gate * (1.0 - sgate)))
        dproj_ref[:, N - D - XW:N - D] = dqx.astype(MXU)
        dproj_ref[:, N - D:] = dgate.astype(MXU)

        @pl.when(i == nt - 1)
        def _():
            dwout_ref[...] = dwout_acc[...].astype(WIRE)
            if ride:
                ride.wait(ride_src, ride_dst, ride_sems)

    if ride:
        ins += [(a, _ANY) for a in ride.arrays]
        outs += [(s, _ANY) for s in ride.out_shapes]
        scratch = scratch + ride.scratch
    return _call(body, f"layer{layer}_bwd", (nt,), ins, outs, scratch)


def _proj_bwd(layer, dproj, dres, xprev, gprev, bprev, wt):
    S = xprev.shape[0]
    nt = S // TSB
    N = wt.shape[0]

    def body(dproj_ref, dres_ref, x_ref, g_ref, b_ref, wt_ref, dx_ref, dwt_ref, acc_ref):
        @pl.when(pl.program_id(0) == 0)
        def _():
            acc_ref[...] = jnp.zeros_like(acc_ref)

        dp = dproj_ref[...]
        xin = x_ref[...] * g_ref[...] + b_ref[...]
        dx_ref[...] = dres_ref[...] + _nn(dp, wt_ref[...])
        acc_ref[...] += _tn(dp, xin)

        @pl.when(pl.program_id(0) == nt - 1)
        def _():
            dwt_ref[...] = acc_ref[...].astype(WIRE)

    ins = [(dproj, _rows(N, nt, False, TSB)), (dres, _rows(D, nt, False, TSB)), (xprev, _rows(D, nt, False, TSB)),
           (gprev, _res(gprev)), (bprev, _res(bprev)), (wt, _res(wt))]
    outs = [(jax.ShapeDtypeStruct((S, D), F32), _rows(D, nt, False, TSB)),
            (jax.ShapeDtypeStruct((N, D), WIRE), _res_sds((N, D)))]
    return _call(body, f"layer{layer}_projbwd", (nt,), ins, outs, [pltpu.VMEM((N, D), F32)])


def _head_mask(h):
    col = lax.broadcasted_iota(jnp.int32, (1, XW), 1)
    return (col // 64) == h


def _kv_fwd(mem, wkv):
    def body(mem_ref, w_ref, ks_ref, vs_ref):
        kv = _nn(mem_ref[...], w_ref[...])
        k, v = kv[:, :XW], kv[:, XW:]
        for h in range(XHEADS):
            ks_ref[h] = jnp.where(_head_mask(h), k, 0.0).astype(MXU)
            vs_ref[h] = jnp.where(_head_mask(h), v, 0.0).astype(MXU)

    sds = jax.ShapeDtypeStruct((XHEADS, NMEM, XW), MXU)
    return pl.pallas_call(body, name="kv_fwd", out_shape=(sds, sds), compiler_params=_cparams())(mem, wkv)


def _kv_bwd(mem, dks_l, dvs_l):
    def body(mem_ref, *refs):
        dks_refs, dvs_refs, out_ref = refs[:DEPTH], refs[DEPTH:2 * DEPTH], refs[2 * DEPTH]
        dk = jnp.zeros((NMEM, XW), F32)
        dv = jnp.zeros((NMEM, XW), F32)
        for h in range(XHEADS):
            m = _head_mask(h)
            for l in range(DEPTH):
                dk = dk + jnp.where(m, dks_refs[l][h], 0.0)
                dv = dv + jnp.where(m, dvs_refs[l][h], 0.0)
        out_ref[...] = _tn(mem_ref[...], jnp.concatenate([dk, dv], axis=1)).astype(WIRE)

    return pl.pallas_call(body, name="kv_bwd", out_shape=jax.ShapeDtypeStruct((D, 2 * XW), WIRE),
                          compiler_params=_cparams())(mem, *dks_l, *dvs_l)


def _prep_weights(w_ins, w_out, wkv):
    def body(a_ref, b_ref, c_ref, d_ref, wo_ref, kv_ref, ao, bo, co, do, wo0, wo1, wo2, wo3, kvo):
        for src, dst in ((a_ref, ao), (b_ref, bo), (c_ref, co), (d_ref, do)):
            dst[...] = src[...].T.astype(MXU)
        for l, dst in enumerate((wo0, wo1, wo2, wo3)):
            dst[...] = wo_ref[l].astype(MXU)
        kvo[...] = kv_ref[...].astype(MXU)

    outs = [jax.ShapeDtypeStruct((w.shape[1], w.shape[0]), MXU) for w in w_ins]
    outs += [jax.ShapeDtypeStruct(w_out.shape[1:], MXU)] * DEPTH + [jax.ShapeDtypeStruct(wkv.shape, MXU)]
    return pl.pallas_call(body, name="prep_weights", out_shape=outs, compiler_params=_cparams())(*w_ins, w_out, wkv)


def _adam_math(w, g, m, v):
    m = B1 * m + (1.0 - B1) * g
    v = B2 * v + (1.0 - B2) * (g * g)
    m_hat = m / (1.0 - B1 ** STEP)
    v_hat = v / (1.0 - B2 ** STEP)
    delta = -LR * (m_hat / (jnp.sqrt(v_hat) + EPS) + WD * w)
    return delta, m, v


def _sum_adam(name, recv, w, m, v, transpose):
    rows, cols = recv.shape[1], recv.shape[2]

    def body(r_ref, w_ref, m_ref, v_ref, g_out, d_out, m_out, v_out, acc_ref):
        s = pl.program_id(0)

        @pl.when(s == 0)
        def _():
            acc_ref[...] = r_ref[...].astype(F32)

        @pl.when(s > 0)
        def _():
            acc_ref[...] += r_ref[...].astype(F32)

        @pl.when(s == NDEV - 1)
        def _():
            g = acc_ref[...].T if transpose else acc_ref[...]
            d, mn, vn = _adam_math(w_ref[...], g, m_ref[...], v_ref[...])
            g_out[...] = g
            d_out[...] = d
            m_out[...] = mn
            v_out[...] = vn

    sds = jax.ShapeDtypeStruct(w.shape, F32)
    ins = [(recv, pl.BlockSpec((None, rows, cols), lambda s: (s, 0, 0))), (w, _res(w)), (m, _res(m)), (v, _res(v))]
    outs = [(sds, _res_sds(w.shape))] * 4
    return _call(body, name, (NDEV,), ins, outs, [pltpu.VMEM((rows, cols), F32)])


def _bias_finalize(dbs_exp):
    def body(dbs_ref, dabs_ref):
        dabs_ref[...] = jnp.sum(dbs_ref[...], axis=-1)

    return pl.pallas_call(body, name="bias_finalize", out_shape=jax.ShapeDtypeStruct((NH, HD), F32),
                          compiler_params=_cparams())(dbs_exp)


def _lb_finalize(dlb, lb_logits):
    def body(dlb_ref, lg_ref, dlg_ref):
        total = jnp.zeros((DEPTH, TOK), F32)
        lg = lg_ref[...]
        e = jnp.exp(lg - jnp.max(lg, axis=0, keepdims=True))
        p = e / jnp.sum(e, axis=0, keepdims=True)
        row = lax.broadcasted_iota(jnp.int32, (DEPTH, TOK), 0)
        for layer in range(DEPTH):
            if layer % 4 != 1:
                continue
            dp = jnp.where((row >= 1) & (row <= layer), dlb_ref[...], 0.0)
            total = total + p * (dp - jnp.sum(p * dp, axis=0, keepdims=True))
        dlg_ref[...] = total

    return pl.pallas_call(body, name="lb_finalize", out_shape=jax.ShapeDtypeStruct((DEPTH, TOK), F32),
                          compiler_params=_cparams())(dlb, lb_logits)


def _small_sum_adam(name, gathered, w, m, v):
    rows = w.shape[0]

    def body(r_ref, w_ref, m_ref, v_ref, g_out, d_out, m_out, v_out):
        g = r_ref[0]
        for s in range(1, NDEV):
            g = g + r_ref[s]
        d, mn, vn = _adam_math(w_ref[...], g, m_ref[...], v_ref[...])
        g_out[...] = g
        d_out[...] = d
        m_out[...] = mn
        v_out[...] = vn

    sds = jax.ShapeDtypeStruct((rows, 128), F32)
    return pl.pallas_call(body, name=name, out_shape=(sds,) * 4, compiler_params=_cparams())(gathered, w, m, v)


def _adam_only(g, w, m, v):
    def body(g_ref, w_ref, m_ref, v_ref, d_out, m_out, v_out):
        d, mn, vn = _adam_math(w_ref[...], g_ref[...], m_ref[...], v_ref[...])
        d_out[...] = d
        m_out[...] = mn
        v_out[...] = vn

    sds = jax.ShapeDtypeStruct(w.shape, F32)
    return pl.pallas_call(body, name="shard_adam", out_shape=(sds,) * 3, compiler_params=_cparams())(g, w, m, v)


def _me_and_peers():
    x, y, c = lax.axis_index("x"), lax.axis_index("y"), lax.axis_index("c")
    me = 4 * x + 2 * y + c
    peers = []
    for k in range(1, NDEV):
        kx, ky, kc = (k >> 2) & 1, (k >> 1) & 1, k & 1
        px = x + kx - 2 * x * kx
        py = y + ky - 2 * y * ky
        pc = c + kc - 2 * c * kc
        peers.append(((px, py, pc), 4 * px + 2 * py + pc))
    return me, peers


_ANY = pl.BlockSpec(memory_space=pl.ANY)


class _Exchange:
    def __init__(self, arrays, split):
        self.arrays = list(arrays)
        self.split = list(split)
        n = len(self.arrays)
        self.out_shapes = []
        for a, sp in zip(self.arrays, self.split):
            rows = a.shape[0] // NDEV if sp else a.shape[0]
            self.out_shapes.append(jax.ShapeDtypeStruct((NDEV, rows, a.shape[1]), a.dtype))
        self.scratch = [pltpu.SemaphoreType.DMA((n, NDEV - 1)), pltpu.SemaphoreType.DMA((n, NDEV - 1)),
                        pltpu.SemaphoreType.DMA((n,))]

    def _block(self, src, t, d):
        if not self.split[t]:
            return src[t]
        rows = self.arrays[t].shape[0] // NDEV
        return src[t].at[pl.ds(d * rows, rows)]

    def start(self, src, dst, sems):
        send_sems, recv_sems, local_sems = sems
        me, peers = _me_and_peers()
        for t in range(len(self.arrays)):
            pltpu.make_async_copy(self._block(src, t, me), dst[t].at[me], local_sems.at[t]).start()
        for k, (dev, idx) in enumerate(peers):
            for t in range(len(self.arrays)):
                pltpu.make_async_remote_copy(src_ref=self._block(src, t, idx), dst_ref=dst[t].at[me],
                                             send_sem=send_sems.at[t, k], recv_sem=recv_sems.at[t, k],
                                             device_id=dev, device_id_type=pl.DeviceIdType.MESH).start()

    def wait(self, src, dst, sems):
        send_sems, recv_sems, local_sems = sems
        me, peers = _me_and_peers()

        def slot_copy(t, k, dev, idx):
            return pltpu.make_async_remote_copy(src_ref=dst[t].at[idx], dst_ref=dst[t].at[idx], send_sem=send_sems.at[t, k],
                                                recv_sem=recv_sems.at[t, k], device_id=dev,
                                                device_id_type=pl.DeviceIdType.MESH)

        for k, (dev, idx) in enumerate(peers):
            for t in range(len(self.arrays)):
                slot_copy(t, k, dev, idx).wait_recv()
        for k, (dev, idx) in enumerate(peers):
            for t in range(len(self.arrays)):
                slot_copy(t, k, dev, idx).wait_send()
        for t in range(len(self.arrays)):
            pltpu.make_async_copy(dst[t].at[me], dst[t].at[me], local_sems.at[t]).wait()

    def run(self, name):
        n = len(self.arrays)

        def body(*refs):
            src, dst, sems = refs[:n], refs[n:2 * n], refs[2 * n:]
            self.start(src, dst, sems)
            self.wait(src, dst, sems)

        return pl.pallas_call(
            body, name=name, out_shape=self.out_shapes, in_specs=[_ANY] * n, out_specs=[_ANY] * n,
            scratch_shapes=self.scratch,
        )(*self.arrays)


SMALL = [("ln_g", (DEPTH, D), False), ("ln_b", (DEPTH, D), False), ("hgrn_lb_logits", (DEPTH, TOK), False),
         ("a_w_s", (1, NH, HD, HD), False), ("a_b_s", (1, NH, HD), False), ("b_norm_g", (1, TOK), True),
         ("c_w_pool", (1, 4, POOL_GROUP, POOL_GROUP), False), ("c_scale", (1, TOK), True),
         ("d_conv_w", (1, 4, TOK), True), ("d_conv_b", (1, TOK), True),
         ("d_w_gx", (1, NH, HD, HD), False), ("d_b_gx", (1, NH, HD), False),
         ("d_w_ga", (1, NH, HD, HD), False), ("d_b_ga", (1, NH, HD), False), ("d_a_param", (1, TOK), True)]


def _pack(parts, total_rows):
    flat = jnp.concatenate([p.reshape(-1).astype(F32) for p in parts])
    flat = jnp.pad(flat, (0, total_rows * 128 - flat.shape[0]))
    return flat.reshape(total_rows, 128)


def _size(shape):
    n = 1
    for s in shape:
        n *= s
    return n


def _rows_for(n):
    return -(-n // 1024) * 8


def kernel(x, mem, mem_kv_w, ln_g, ln_b, w_out, hgrn_lb_logits, a_w_in, a_w_s, a_b_s, b_w_in, b_norm_g, c_w_in, c_w_pool, c_scale, d_w_in, d_conv_w, d_conv_b, d_w_gx, d_b_gx, d_w_ga, d_b_ga, d_a_param, loss_target, m_mem_kv_w, m_ln_g, m_ln_b, m_w_out, m_hgrn_lb_logits, m_a_w_in, m_a_w_s, m_a_b_s, m_b_w_in, m_b_norm_g, m_c_w_in, m_c_w_pool, m_c_scale, m_d_w_in, m_d_conv_w, m_d_conv_b, m_d_w_gx, m_d_b_gx, m_d_w_ga, m_d_b_ga, m_d_a_param, v_mem_kv_w, v_ln_g, v_ln_b, v_w_out, v_hgrn_lb_logits, v_a_w_in, v_a_w_s, v_a_b_s, v_b_w_in, v_b_norm_g, v_c_w_in, v_c_w_pool, v_c_scale, v_d_w_in, v_d_conv_w, v_d_conv_b, v_d_w_gx, v_d_b_gx, v_d_w_ga, v_d_b_ga, v_d_a_param):
    W = dict(mem_kv_w=mem_kv_w, ln_g=ln_g, ln_b=ln_b, w_out=w_out, hgrn_lb_logits=hgrn_lb_logits, a_w_in=a_w_in, a_w_s=a_w_s,
             a_b_s=a_b_s, b_w_in=b_w_in, b_norm_g=b_norm_g, c_w_in=c_w_in, c_w_pool=c_w_pool, c_scale=c_scale, d_w_in=d_w_in,
             d_conv_w=d_conv_w, d_conv_b=d_conv_b, d_w_gx=d_w_gx, d_b_gx=d_b_gx, d_w_ga=d_w_ga, d_b_ga=d_b_ga, d_a_param=d_a_param)
    M = dict(mem_kv_w=m_mem_kv_w, ln_g=m_ln_g, ln_b=m_ln_b, w_out=m_w_out, hgrn_lb_logits=m_hgrn_lb_logits, a_w_in=m_a_w_in,
             a_w_s=m_a_w_s, a_b_s=m_a_b_s, b_w_in=m_b_w_in, b_norm_g=m_b_norm_g, c_w_in=m_c_w_in, c_w_pool=m_c_w_pool,
             c_scale=m_c_scale, d_w_in=m_d_w_in, d_conv_w=m_d_conv_w, d_conv_b=m_d_conv_b, d_w_gx=m_d_w_gx, d_b_gx=m_d_b_gx,
             d_w_ga=m_d_w_ga, d_b_ga=m_d_b_ga, d_a_param=m_d_a_param)
    V = dict(mem_kv_w=v_mem_kv_w, ln_g=v_ln_g, ln_b=v_ln_b, w_out=v_w_out, hgrn_lb_logits=v_hgrn_lb_logits, a_w_in=v_a_w_in,
             a_w_s=v_a_w_s, a_b_s=v_a_b_s, b_w_in=v_b_w_in, b_norm_g=v_b_norm_g, c_w_in=v_c_w_in, c_w_pool=v_c_w_pool,
             c_scale=v_c_scale, d_w_in=v_d_w_in, d_conv_w=v_d_conv_w, d_conv_b=v_d_conv_b, d_w_gx=v_d_w_gx, d_b_gx=v_d_b_gx,
             d_w_ga=v_d_w_ga, d_b_ga=v_d_b_ga, d_a_param=v_d_a_param)
    me = 4 * lax.axis_index("x") + 2 * lax.axis_index("y") + lax.axis_index("c")
    x2, mem2, tgt2 = x[0], mem[0], loss_target[0]
    in_names = ["a_w_in", "b_w_in", "c_w_in", "d_w_in"]

    shard_names = [n for n, _, sh in SMALL if sh]
    small_shard = _pack([W[n] for n in shard_names], 8)
    wts = _prep_weights([W[n][0] for n in in_names], w_out, mem_kv_w)
    wt_sh, wo_sh, wkv_sh = wts[:4], wts[4:8], wts[8]
    g0 = _Exchange([wt_sh[0], wo_sh[0], wkv_sh, small_shard], [False] * 4).run("gather_first")
    wt_full = [g0[0].reshape(-1, D)]
    wout_full = [g0[1].reshape(D, D)]
    wkv_full = g0[2].reshape(D, 2 * XW)
    sm = g0[3].reshape(NDEV, 1024)
    full_small = {}
    off = 0
    for n, shape, _ in [s for s in SMALL if s[2]]:
        per = _size(shape) // NDEV
        blk = sm[:, off:off + per]
        if n == "d_conv_w":
            full_small[n] = blk.reshape(NDEV, 4, TOK // NDEV).transpose(1, 0, 2).reshape(4, TOK)
        else:
            full_small[n] = blk.reshape(1, TOK)
        off += per

    ks, vs = _kv_fwd(mem2, wkv_full)
    tri_bs = jnp.broadcast_to(a_b_s[0][:, :, None], (NH, HD, HD))
    wbd = jnp.zeros((TOK, TOK), F32)
    for g in range(4):
        wbd = lax.dynamic_update_slice(wbd, c_w_pool[0, g], (g * POOL_GROUP, g * POOL_GROUP))
    wbd = wbd.astype(MXU)
    prm = {0: [a_w_s[0], tri_bs],
           1: [hgrn_lb_logits, full_small["b_norm_g"]],
           2: [wbd, full_small["c_scale"]],
           3: [full_small["d_conv_w"], full_small["d_conv_b"], d_w_gx[0].astype(MXU), d_b_gx[0].reshape(1, TOK),
               d_w_ga[0].astype(MXU), d_b_ga[0].reshape(1, TOK), full_small["d_a_param"]]}
    ones = jnp.ones((1, D), F32)
    zeros = jnp.zeros((1, D), F32)
    xs, gs, bs = [x2], [ones], [zeros]
    saved = []
    for i in range(DEPTH):
        ride = _Exchange([wt_sh[i + 1], wo_sh[i + 1]], [False, False]) if i + 1 < DEPTH else None
        res = _layer_fwd(i, i, xs[i], gs[i], bs[i], wt_full[i], wout_full[i], ks, vs, prm[i], ride)
        if ride:
            wt_full.append(res[-2].reshape(-1, D))
            wout_full.append(res[-1].reshape(D, D))
            res = res[:-2]
        saved.append(res)
        xs.append(res[1])
        gs.append(ln_g[i:i + 1])
        bs.append(ln_b[i:i + 1])

    up = tgt2
    grads = {}
    dks_l, dvs_l, dwt_l, dwout_l, dlng_l, dlnb_l = [], [], [], [], [], []
    recv_wt, recv_wo = [None] * DEPTH, [None] * DEPTH
    loss_part = None
    small_shape = {n: s for n, s, _ in SMALL}
    small_shape.update({f"ln_g#{l}": (1, D) for l in range(DEPTH)})
    small_shape.update({f"ln_b#{l}": (1, D) for l in range(DEPTH)})
    small_shape["loss"] = (128,)
    sharded = {n for n, _, sh in SMALL if sh}
    group = {3: ["ln_g#3", "ln_b#3", "d_conv_w", "d_conv_b", "d_w_gx", "d_b_gx", "d_w_ga", "d_b_ga", "d_a_param"],
             2: ["ln_g#2", "ln_b#2", "c_w_pool", "c_scale"],
             1: ["ln_g#1", "ln_b#1", "hgrn_lb_logits", "b_norm_g"],
             0: ["ln_g#0", "ln_b#0", "a_w_s", "a_b_s", "loss"]}
    group_rows = {l: _rows_for(sum(_size(small_shape[e]) for e in group[l])) for l in range(DEPTH)}
    recv_small = [None] * DEPTH
    for i in reversed(range(DEPTH)):
        res = saved[i]
        extra = res[3] if len(res) > 3 else None
        ride = None
        if i + 1 < DEPTH:
            small_vec = _pack([grads[e] for e in group[i + 1]], group_rows[i + 1])
            ride = _Exchange([dwt_l[-1], dwout_l[-1], small_vec], [True, True, False])
        out = _layer_bwd(i, i, up, i == DEPTH - 1, res[1], res[2], gs[i + 1], bs[i + 1], res[0], wout_full[i], ks, vs,
                         prm[i], extra, ride)
        if ride:
            recv_wt[i + 1], recv_wo[i + 1], recv_small[i + 1] = out[-3], out[-2], out[-1]
            out = out[:-3]
        dres, dproj, dwout_i, dks_i, dvs_i, dg_i, db_i, loss_i = out[:8]
        pg = out[8:]
        if i == DEPTH - 1:
            grads["loss"] = loss_i[0]
        dks_l.append(dks_i)
        dvs_l.append(dvs_i)
        dwout_l.append(dwout_i)
        grads[f"ln_g#{i}"], grads[f"ln_b#{i}"] = dg_i, db_i
        if i == 0:
            grads["a_w_s"], dbs_exp = pg
            grads["a_b_s"] = _bias_finalize(dbs_exp)
        elif i == 1:
            dlb, grads["b_norm_g"] = pg
            grads["hgrn_lb_logits"] = _lb_finalize(dlb, hgrn_lb_logits)
        elif i == 2:
            dwbd, grads["c_scale"] = pg
            grads["c_w_pool"] = jnp.stack([lax.dynamic_slice(dwbd, (g * POOL_GROUP, g * POOL_GROUP), (POOL_GROUP, POOL_GROUP))
                                           for g in range(4)])
        else:
            (grads["d_conv_w"], grads["d_conv_b"], grads["d_w_gx"], grads["d_b_gx"], grads["d_w_ga"], grads["d_b_ga"],
             grads["d_a_param"]) = pg
        up, dwt = _proj_bwd(i, dproj, dres, xs[i], gs[i], bs[i], wt_full[i])
        dwt_l.append(dwt)
    grad_x = up[None]
    dwkv = _kv_bwd(mem2, dks_l, dvs_l)

    small_vec = _pack([grads[e] for e in group[0]], group_rows[0])
    recv_wt[0], recv_wo[0], recv_kv, recv_small[0] = _Exchange(
        [dwt_l[-1], dwout_l[-1], dwkv, small_vec], [True, True, True, False]).run("scatter_last")

    outs = {}
    for t, n in enumerate(in_names):
        g, d, mn, vn = _sum_adam(f"adam_{n}", recv_wt[t], W[n][0], M[n][0], V[n][0], True)
        outs[n] = (g[None], d[None], mn[None], vn[None])
    wo_res = [_sum_adam(f"adam_w_out{l}", recv_wo[l], w_out[l], m_w_out[l], v_w_out[l], False) for l in range(DEPTH)]
    outs["w_out"] = tuple(jnp.stack([wo_res[l][j] for l in range(DEPTH)]) for j in range(4))
    outs["mem_kv_w"] = _sum_adam("adam_mem_kv_w", recv_kv, mem_kv_w, m_mem_kv_w, v_mem_kv_w, False)

    def entry_of(tree, e):
        if "#" in e:
            n, l = e.split("#")
            return tree[n][int(l):int(l) + 1]
        if e == "loss" or e in sharded:
            return jnp.zeros(small_shape[e], F32)
        return tree[e]

    small = [{}, {}, {}, {}]
    for l in range(DEPTH):
        packed = [_pack([entry_of(t, e) for e in group[l]], group_rows[l]) for t in (W, M, V)]
        res = _small_sum_adam(f"small_adam{l}", recv_small[l], *packed)
        for j in range(4):
            flat, o = res[j].reshape(-1), 0
            for e in group[l]:
                small[j][e] = flat[o:o + _size(small_shape[e])].reshape(small_shape[e])
                o += _size(small_shape[e])
    loss = small[0]["loss"][0]
    for j in range(4):
        for n in ("ln_g", "ln_b"):
            small[j][n] = jnp.concatenate([small[j][f"{n}#{l}"] for l in range(DEPTH)], axis=0)
    g_small = small[0]
    for n, _, sh in SMALL:
        if not sh:
            outs[n] = tuple(small[j][n] for j in range(4))
    per = TOK // NDEV
    g_sh = {n: lax.dynamic_slice_in_dim(g_small[n], me * per, per, axis=len(s) - 1) for n, s, sh in SMALL if sh}
    gp = _pack([g_sh[n] for n in shard_names], 8)
    d_p, m_p, v_p = _adam_only(gp, small_shard, _pack([M[n] for n in shard_names], 8), _pack([V[n] for n in shard_names], 8))
    o = 0
    for n in shard_names:
        cnt = _size(W[n].shape)
        outs[n] = (g_sh[n],) + tuple(t.reshape(-1)[o:o + cnt].reshape(W[n].shape) for t in (d_p, m_p, v_p))
        o += cnt

    order = ["mem_kv_w", "ln_g", "ln_b", "w_out", "hgrn_lb_logits", "a_w_in", "a_w_s", "a_b_s", "b_w_in", "b_norm_g", "c_w_in",
             "c_w_pool", "c_scale", "d_w_in", "d_conv_w", "d_conv_b", "d_w_gx", "d_b_gx", "d_w_ga", "d_b_ga", "d_a_param"]
    result = [loss, grad_x]
    for j in range(4):
        result += [outs[n][j].reshape(W[n].shape) for n in order]
    return tuple(result)
```

```python
import functools

import jax
import jax.numpy as jnp
from jax import lax
from jax.experimental import pallas as pl
from jax.experimental.pallas import tpu as pltpu

F32 = jnp.float32
MXU = jnp.bfloat16
WIRE = jnp.bfloat16

D = 1024
TOK = 768
XW = 256
NMEM = 256
XHEADS = 4
XSCALE = 64 ** -0.5
NH = 6
HD = 128
CH = 16
POOL_WINDOWS = (2, 4, 8, 16)
POOL_GROUP = 192
DEPTH = 4
ALPHA = (2 * DEPTH) ** 0.25
LN_EPS = 1e-5
RMS_EPS = 1e-6
LRU_C = 8.0
B1, B2, LR, EPS, WD, STEP = 0.9, 0.999, 0.001, 1e-8, 0.01, 10

NDEV = 8
TS = 256
TSB = 512
VMEM_LIMIT = 58 * 1024 * 1024

KIND_WIDTHS = {0: 2 * TOK + XW + D, 1: 3 * TOK + XW + D, 2: TOK + XW + D, 3: TOK + XW + D}


def _mm(a, b, ca, cb):
    return lax.dot_general(a.astype(MXU), b.astype(MXU), (((ca,), (cb,)), ((), ())), preferred_element_type=F32)


def _nn(a, b):
    return _mm(a, b, 1, 0)


def _nt(a, b):
    return _mm(a, b, 1, 1)


def _tn(a, b):
    return _mm(a, b, 0, 0)


def _bmm(a, b, ca, cb):
    return lax.dot_general(a.astype(MXU), b.astype(MXU), (((ca,), (cb,)), ((0,), (0,))), preferred_element_type=F32)


def _sigmoid(x):
    return 1.0 / (1.0 + jnp.exp(-x))


def _vjp1(fn, x, dy):
    return jax.vjp(fn, x)[1](dy)[0]


def _rowsum(x):
    return jnp.sum(x, axis=0, keepdims=True)


def _row(x, r):
    sel = lax.broadcasted_iota(jnp.int32, x.shape, 0) == r
    return jnp.sum(jnp.where(sel, x, 0.0), axis=0, keepdims=True)


def _acc(ref, val):
    ref[...] += val


def _cparams(sem=None):
    return pltpu.CompilerParams(dimension_semantics=sem, vmem_limit_bytes=VMEM_LIMIT)


def _res(a):
    nd = a.ndim
    return pl.BlockSpec(a.shape, lambda i: (0,) * nd)


def _res_sds(shape):
    nd = len(shape)
    return pl.BlockSpec(shape, lambda i: (0,) * nd)


def _rows(width, nt, rev, ts=TS):
    if rev:
        return pl.BlockSpec((ts, width), lambda i: (nt - 1 - i, 0))
    return pl.BlockSpec((ts, width), lambda i: (i, 0))


def _call(body, name, grid, ins, outs, scratch=(), sem=("arbitrary",)):
    arrays = [a for a, _ in ins]
    return pl.pallas_call(
        body, name=name, grid=grid,
        in_specs=[s for _, s in ins],
        out_specs=[s for _, s in outs],
        out_shape=[o for o, _ in outs],
        scratch_shapes=list(scratch),
        compiler_params=_cparams(sem),
    )(*arrays)


def _xattn_fwd(qx, ks_ref, vs_ref):
    o = None
    ps = []
    for h in range(XHEADS):
        s = _nt(qx, ks_ref[h]) * XSCALE
        s = s - jnp.max(s, axis=-1, keepdims=True)
        e = jnp.exp(s)
        p = e / jnp.sum(e, axis=-1, keepdims=True)
        ps.append(p)
        oh = _nn(p, vs_ref[h])
        o = oh if o is None else o + oh
    return o, ps


def _xattn_bwd(qx, ps, dxo, ks_ref, vs_ref, dks_ref, dvs_ref):
    dq = None
    for h in range(XHEADS):
        p = ps[h]
        dp = _nt(dxo, vs_ref[h])
        ds = p * (dp - jnp.sum(dp * p, axis=-1, keepdims=True))
        dqh = _nn(ds, ks_ref[h]) * XSCALE
        dq = dqh if dq is None else dq + dqh
        dks_ref[h] += _tn(ds, qx) * XSCALE
        dvs_ref[h] += _tn(p, dxo)
    return dq


def _tril128():
    r = lax.broadcasted_iota(jnp.int32, (HD, HD), 0)
    c = lax.broadcasted_iota(jnp.int32, (HD, HD), 1)
    return c <= r


def _gmlp_fwd(u, v, ws_ref, bs_ref):
    ts = u.shape[0]
    ug = jax.nn.gelu(u)
    vg = jax.nn.gelu(v)
    tri = _tril128()
    toks, res = [], []
    for g in range(NH):
        sl = slice(g * HD, (g + 1) * HD)
        vgh = vg[:, sl]
        cen = vgh - jnp.mean(vgh, axis=-1, keepdims=True)
        rstd = lax.rsqrt(jnp.mean(cen * cen, axis=-1, keepdims=True) + LN_EPS)
        vn = cen * rstd
        w = jnp.where(tri, ws_ref[g], 0.0).astype(MXU)
        mix = jnp.concatenate([_nn(w, vn[n * HD:(n + 1) * HD]) + bs_ref[g] for n in range(ts // HD)], axis=0)
        toks.append(ug[:, sl] * mix)
        res.append((vn, rstd, mix, w))
    return jnp.concatenate(toks, axis=1), (ug, res)


def _gmlp_bwd(u, v, fres, dtok, dws_ref, dbs_ref):
    ts = u.shape[0]
    ug, res = fres
    tri = _tril128()
    dugs, dvgs = [], []
    for g in range(NH):
        sl = slice(g * HD, (g + 1) * HD)
        vn, rstd, mix, w = res[g]
        dmix = dtok[:, sl] * ug[:, sl]
        dugs.append(dtok[:, sl] * mix)
        dvn_rows = []
        dw = None
        dbs = None
        for n in range(ts // HD):
            dm = dmix[n * HD:(n + 1) * HD]
            dvn_rows.append(_tn(w, dm))
            t = _nt(dm, vn[n * HD:(n + 1) * HD])
            dw = t if dw is None else dw + t
            dbs = dm if dbs is None else dbs + dm
        dws_ref[g] += jnp.where(tri, dw, 0.0)
        dbs_ref[g] += dbs
        dvn = jnp.concatenate(dvn_rows, axis=0)
        dvgs.append(rstd * (dvn - jnp.mean(dvn, axis=-1, keepdims=True) - vn * jnp.mean(dvn * vn, axis=-1, keepdims=True)))
    du = _vjp1(jax.nn.gelu, u, jnp.concatenate(dugs, axis=1))
    dv = _vjp1(jax.nn.gelu, v, jnp.concatenate(dvgs, axis=1))
    return du, dv


def _chunk_cumsum(x):
    row = lax.broadcasted_iota(jnp.int32, x.shape, 0) % CH
    for s in (1, 2, 4, 8):
        x = x + jnp.where(row >= s, pltpu.roll(x, s, 0), 0.0)
    return x


def _chunk_revcumsum(x):
    n = x.shape[0]
    row = lax.broadcasted_iota(jnp.int32, x.shape, 0) % CH
    for s in (1, 2, 4, 8):
        x = x + jnp.where(row < CH - s, pltpu.roll(x, n - s, 0), 0.0)
    return x


def _chunk_sum(x):
    n, w = x.shape
    return jnp.sum(x.reshape(n // CH, CH, w), axis=1)


def _chunk_bcast(c, n):
    nch, w = c.shape
    return jnp.broadcast_to(c[:, None, :], (nch, CH, w)).reshape(n, w)


def _lower_bound(lb_logits, layer):
    lg = lb_logits
    e = jnp.exp(lg - jnp.max(lg, axis=0, keepdims=True))
    p = e / jnp.sum(e, axis=0, keepdims=True)
    row = lax.broadcasted_iota(jnp.int32, p.shape, 0)
    lb = jnp.sum(jnp.where((row >= 1) & (row <= layer), p, 0.0), axis=0, keepdims=True)
    return lb, p


def _hgrn_prep(q, fl, lb):
    n = q.shape[0]
    sg = _sigmoid(fl)
    f = lb + (1.0 - lb) * sg
    lf = jnp.log(f)
    k = 1.0 - f
    sq = _sigmoid(q)
    qf = q * sq
    g = _chunk_cumsum(lf)
    tot = _chunk_sum(lf)
    gl = _chunk_bcast(tot, n)
    eg = jnp.exp(g)
    eng = jnp.exp(-g)
    egl = jnp.exp(gl - g)
    return dict(sg=sg, f=f, k=k, sq=sq, qf=qf, eg=eg, eng=eng, egl=egl,
                qd=qf * eg, ki=k * eng, ke=k * egl, dch=jnp.exp(tot))


def _hgrn_mask():
    r = lax.broadcasted_iota(jnp.int32, (HD, HD), 0)
    c = lax.broadcasted_iota(jnp.int32, (HD, HD), 1)
    return (r // CH == c // CH) & (c <= r)


def _hgrn_states(h, v3, ke3, dch_h, st_in, sts_s, ut_s, dch_s):
    nch = v3.shape[0]
    ut_s[...] = _bmm(v3, ke3, 1, 1)
    dch_s[...] = dch_h

    def step(c, st):
        sts_s[c] = st
        return st * dch_s[pl.ds(c, 1), :] + ut_s[c]

    return lax.fori_loop(0, nch, step, st_in)


def _hgrn_fwd(q, fl, inp, lb, ng, st_ref, sts_s, ut_s, dch_s):
    n = q.shape[0]
    nch = n // CH
    pr = _hgrn_prep(q, fl, lb)
    mask = _hgrn_mask()
    toks = []
    for h in range(NH):
        sl = slice(h * HD, (h + 1) * HD)
        qd, ki, ke, v = pr["qd"][:, sl], pr["ki"][:, sl], pr["ke"][:, sl], inp[:, sl]
        qd3 = qd.astype(MXU).reshape(nch, CH, HD)
        v3 = v.astype(MXU).reshape(nch, CH, HD)
        ke3 = ke.astype(MXU).reshape(nch, CH, HD)
        st_ref[h] = _hgrn_states(h, v3, ke3, pr["dch"][:, sl], st_ref[h], sts_s, ut_s, dch_s)
        o = _bmm(qd3, sts_s[...], 2, 2).reshape(n, HD)
        intra = []
        for b in range(n // HD):
            bs = slice(b * HD, (b + 1) * HD)
            a = jnp.where(mask, _nt(qd[bs], ki[bs]), 0.0)
            intra.append(_nn(a, v[bs]))
        o = o + jnp.concatenate(intra, axis=0)
        r = lax.rsqrt(jnp.mean(o * o, axis=-1, keepdims=True) + RMS_EPS)
        toks.append(o * r * ng[:, sl])
    return jnp.concatenate(toks, axis=1)


def _hgrn_bwd(q, fl, inp, lb, ng, dtok, ststart_ref, dst_ref, sts_s, ut_s, dch_s, dstn_s, dng_ref, dlb_ref):
    n = q.shape[0]
    nch = n // CH
    pr = _hgrn_prep(q, fl, lb)
    mask = _hgrn_mask()
    dqd_l, dki_l, dke_l, dv_l, ddch_l, dng_l, toks = [], [], [], [], [], [], []
    for h in range(NH):
        sl = slice(h * HD, (h + 1) * HD)
        qd, ki, ke, v = pr["qd"][:, sl], pr["ki"][:, sl], pr["ke"][:, sl], inp[:, sl]
        qd3 = qd.astype(MXU).reshape(nch, CH, HD)
        v3 = v.astype(MXU).reshape(nch, CH, HD)
        ke3 = ke.astype(MXU).reshape(nch, CH, HD)
        _hgrn_states(h, v3, ke3, pr["dch"][:, sl], ststart_ref[h], sts_s, ut_s, dch_s)
        sts = sts_s[...]
        o = _bmm(qd3, sts, 2, 2).reshape(n, HD)
        a_l = []
        intra = []
        for b in range(n // HD):
            bs = slice(b * HD, (b + 1) * HD)
            a = jnp.where(mask, _nt(qd[bs], ki[bs]), 0.0)
            a_l.append(a)
            intra.append(_nn(a, v[bs]))
        o = o + jnp.concatenate(intra, axis=0)
        r = lax.rsqrt(jnp.mean(o * o, axis=-1, keepdims=True) + RMS_EPS)
        toks.append(o * r * ng[:, sl])
        dt = dtok[:, sl]
        dng_l.append(_rowsum(dt * o * r))
        dn = dt * ng[:, sl]
        do = r * dn - o * (r * r * r) * jnp.mean(dn * o, axis=-1, keepdims=True)
        do3 = do.astype(MXU).reshape(nch, CH, HD)
        dqd_rows, dki_rows, dv_rows = [], [], []
        for b in range(n // HD):
            bs = slice(b * HD, (b + 1) * HD)
            da = jnp.where(mask, _nt(do[bs], v[bs]), 0.0)
            dqd_rows.append(_nn(da, ki[bs]))
            dki_rows.append(_tn(da, qd[bs]))
            dv_rows.append(_tn(a_l[b], do[bs]))
        dqd = jnp.concatenate(dqd_rows, axis=0) + _bmm(do3, sts, 2, 1).reshape(n, HD)
        dki = jnp.concatenate(dki_rows, axis=0)
        dv = jnp.concatenate(dv_rows, axis=0)
        ut_s[...] = _bmm(do3, qd3, 1, 1)

        def step(i, dst):
            c = nch - 1 - i
            dstn_s[c] = dst
            return ut_s[c] + dst * dch_s[pl.ds(c, 1), :]

        dst_ref[h] = lax.fori_loop(0, nch, step, dst_ref[h])
        dstn = dstn_s[...]
        dv = dv + _bmm(ke3, dstn, 2, 2).reshape(n, HD)
        dke = _bmm(v3, dstn, 2, 1).reshape(n, HD)
        ddch_l.append(jnp.sum(sts * dstn, axis=1))
        dqd_l.append(dqd)
        dki_l.append(dki)
        dke_l.append(dke)
        dv_l.append(dv)
    dqd = jnp.concatenate(dqd_l, axis=1)
    dki = jnp.concatenate(dki_l, axis=1)
    dke = jnp.concatenate(dke_l, axis=1)
    dinp = jnp.concatenate(dv_l, axis=1)
    ddch = jnp.concatenate(ddch_l, axis=1)
    _acc(dng_ref, jnp.concatenate(dng_l, axis=1))
    dqf = dqd * pr["eg"]
    dke_ke = dke * pr["ke"]
    dg = dqd * pr["qd"] - dki * pr["ki"] - dke_ke
    dk = dki * pr["eng"] + dke * pr["egl"]
    dgl = _chunk_sum(dke_ke) + ddch * pr["dch"]
    dlf = _chunk_revcumsum(dg) + _chunk_bcast(dgl, n)
    df = dlf / pr["f"] - dk
    sg = pr["sg"]
    dfl = df * (1.0 - lb) * sg * (1.0 - sg)
    _acc(dlb_ref, _rowsum(df * (1.0 - sg)))
    sq = pr["sq"]
    dq = dqf * (sq * (1.0 + q * (1.0 - sq)))
    return jnp.concatenate(toks, axis=1), dq, dfl, dinp


def _pool_select(s2, s4, s8, s16):
    col = lax.broadcasted_iota(jnp.int32, (1, TOK), 1)
    return jnp.where(col < POOL_GROUP, s2, jnp.where(col < 2 * POOL_GROUP, s4, jnp.where(col < 3 * POOL_GROUP, s8, s16)))


def _pool_cnt(pos0, n):
    pos = pos0 + lax.broadcasted_iota(jnp.int32, (n, TOK), 0) + 1
    col = lax.broadcasted_iota(jnp.int32, (n, TOK), 1)
    w = jnp.where(col < POOL_GROUP, 2, jnp.where(col < 2 * POOL_GROUP, 4, jnp.where(col < 3 * POOL_GROUP, 8, 16)))
    return jnp.minimum(pos, w).astype(F32)


def _pool_fwd(p, halo, pos0, wbd, scale):
    n = p.shape[0]
    ext = jnp.concatenate([halo, p], axis=0)
    s2 = ext + pltpu.roll(ext, 1, 0)
    s4 = s2 + pltpu.roll(s2, 2, 0)
    s8 = s4 + pltpu.roll(s4, 4, 0)
    s16 = s8 + pltpu.roll(s8, 8, 0)
    win = _pool_select(s2, s4, s8, s16)[16:]
    cnt = _pool_cnt(pos0, n)
    diff = win / cnt - p
    y = _nn(diff, wbd)
    return y * scale, (diff, y, cnt)


def _pool_bwd(fres, dtok, nxt_ref, wbd, scale, dwbd_ref, dscale_ref):
    diff, y, cnt = fres
    n = diff.shape[0]
    _acc(dscale_ref, _rowsum(dtok * y))
    dy = dtok * scale
    ddiff = _nt(dy, wbd)
    dwbd_ref[...] += _tn(diff, dy)
    qv = ddiff / cnt
    ext = jnp.concatenate([qv, nxt_ref[...]], axis=0)
    m = n + 16
    s2 = ext + pltpu.roll(ext, m - 1, 0)
    s4 = s2 + pltpu.roll(s2, m - 2, 0)
    s8 = s4 + pltpu.roll(s4, m - 4, 0)
    s16 = s8 + pltpu.roll(s8, m - 8, 0)
    adj = _pool_select(s2, s4, s8, s16)[:n]
    nxt_ref[...] = qv[:16]
    return adj - ddiff


def _neg_expm1(x):
    return jnp.where(jnp.abs(x) < 1e-2, -x * (1.0 + x * (0.5 + x * (1.0 / 6.0))), 1.0 - jnp.exp(x))


def _lru_gates(xc, zx, za, ap, first):
    gx = _sigmoid(zx)
    ga = _sigmoid(za)
    sp = jnp.maximum(-ap, 0.0) + jnp.log(1.0 + jnp.exp(-jnp.abs(ap)))
    log_a = -LRU_C * ga * sp
    a = jnp.exp(log_a)
    mult = jnp.sqrt(_neg_expm1(2.0 * log_a))
    mult = jnp.where(first, 1.0, mult)
    return a, mult * gx * xc


def _scan_fwd(a, b, h0):
    n = a.shape[0]
    row = lax.broadcasted_iota(jnp.int32, a.shape, 0)
    s = 1
    while s < n:
        keep = row >= s
        b = b + a * jnp.where(keep, pltpu.roll(b, s, 0), 0.0)
        a = a * jnp.where(keep, pltpu.roll(a, s, 0), 1.0)
        s *= 2
    return b + a * h0


def _scan_bwd(an, d, dh_next):
    n = an.shape[0]
    row = lax.broadcasted_iota(jnp.int32, an.shape, 0)
    s = 1
    while s < n:
        keep = row < n - s
        d = d + an * jnp.where(keep, pltpu.roll(d, n - s, 0), 0.0)
        an = an * jnp.where(keep, pltpu.roll(an, n - s, 0), 1.0)
        s *= 2
    return d + an * dh_next


def _lru_conv(xb, halo, cw_ref, cb):
    ext = jnp.concatenate([halo, xb], axis=0)
    sh = [pltpu.roll(ext, 3 - j, 0)[8:] if j < 3 else xb for j in range(4)]
    xc = cb
    for j in range(4):
        xc = xc + cw_ref[pl.ds(j, 1), :] * sh[j]
    return xc, sh


def _lru_fwd(xb, halo, pos0, prm, h0):
    cw, cb, wgx, bgx, wga, bga, ap = prm
    n = xb.shape[0]
    xc, sh = _lru_conv(xb, halo, cw, cb[...])
    zx = jnp.concatenate([_nn(xc[:, h * HD:(h + 1) * HD], wgx[h]) for h in range(NH)], axis=1) + bgx[...]
    za = jnp.concatenate([_nn(xc[:, h * HD:(h + 1) * HD], wga[h]) for h in range(NH)], axis=1) + bga[...]
    first = (pos0 + lax.broadcasted_iota(jnp.int32, (n, 1), 0)) == 0
    a, b = _lru_gates(xc, zx, za, ap[...], first)
    hseq = _scan_fwd(a, b, h0)
    return hseq, (xc, sh, zx, za, first, a)


def _lru_bwd(fres, hseq, h0, dtok, prm, carry_refs, grad_refs):
    cw, cb, wgx, bgx, wga, bga, ap = prm
    xc, sh, zx, za, first, a = fres
    anext_ref, dhnext_ref, dxcnext_ref = carry_refs
    dcw_ref, dcb_ref, dwgx_ref, dbgx_ref, dwga_ref, dbga_ref, dap_ref = grad_refs
    n = xc.shape[0]
    an = jnp.where(lax.broadcasted_iota(jnp.int32, a.shape, 0) == n - 1, anext_ref[...], pltpu.roll(a, n - 1, 0))
    dh = _scan_bwd(an, dtok, dhnext_ref[...])
    hprev = jnp.where(lax.broadcasted_iota(jnp.int32, hseq.shape, 0) == 0, h0, pltpu.roll(hseq, 1, 0))
    da = dh * hprev
    anext_ref[...] = _row(a, 0)
    dhnext_ref[...] = _row(dh, 0)
    _, vjp = jax.vjp(lambda xc_, zx_, za_, ap_: _lru_gates(xc_, zx_, za_, ap_, first), xc, zx, za, ap[...])
    dxc, dzx, dza, dap = vjp((da, dh))
    _acc(dap_ref, dap)
    _acc(dbgx_ref, _rowsum(dzx))
    _acc(dbga_ref, _rowsum(dza))
    parts = []
    for h in range(NH):
        sl = slice(h * HD, (h + 1) * HD)
        parts.append(_nt(dzx[:, sl], wgx[h]) + _nt(dza[:, sl], wga[h]))
        dwgx_ref[h] += _tn(xc[:, sl], dzx[:, sl])
        dwga_ref[h] += _tn(xc[:, sl], dza[:, sl])
    dxc = dxc + jnp.concatenate(parts, axis=1)
    _acc(dcb_ref, _rowsum(dxc))
    for j in range(4):
        dcw_ref[pl.ds(j, 1), :] += _rowsum(dxc * sh[j])
    ext = jnp.concatenate([dxc, dxcnext_ref[...]], axis=0)
    m = n + 8
    dxb = cw[pl.ds(3, 1), :] * dxc
    for j in range(3):
        dxb = dxb + cw[pl.ds(j, 1), :] * pltpu.roll(ext, m - (3 - j), 0)[:n]
    dxcnext_ref[...] = dxc[:8]
    return dxb


def _layer_fwd(kind, layer, xprev, gprev, bprev, wt, wout, ks, vs, prm, ride=None):
    S = xprev.shape[0]
    nt = S // TS
    N = wt.shape[0]
    nprm = len(prm)
    nch = TS // CH

    outs = [(jax.ShapeDtypeStruct((S, N), F32), _rows(N, nt, False)),
            (jax.ShapeDtypeStruct((S, D), F32), _rows(D, nt, False)),
            (jax.ShapeDtypeStruct((S, 1), F32), _rows(1, nt, False)),
            (jax.ShapeDtypeStruct((S, XHEADS * NMEM), MXU), _rows(XHEADS * NMEM, nt, False))]
    scratch = []
    if kind == 1:
        outs.append((jax.ShapeDtypeStruct((nt, NH, HD, HD), F32), pl.BlockSpec((None, NH, HD, HD), lambda i: (i, 0, 0, 0))))
        scratch = [pltpu.VMEM((NH, HD, HD), F32), pltpu.VMEM((nch, HD, HD), F32), pltpu.VMEM((nch, HD, HD), F32),
                   pltpu.VMEM((nch, HD), F32)]
    elif kind == 2:
        scratch = [pltpu.VMEM((16, TOK), F32)]
    elif kind == 3:
        outs.append((jax.ShapeDtypeStruct((nt * 8, TOK), F32), pl.BlockSpec((8, TOK), lambda i: (i, 0))))
        scratch = [pltpu.VMEM((8, TOK), F32), pltpu.VMEM((1, TOK), F32)]
    nout = len(outs)
    nscr = len(scratch)
    nride = len(ride.arrays) if ride else 0

    def body(*refs):
        x_ref, g_ref, b_ref, wt_ref, wout_ref, ks_ref, vs_ref = refs[:7]
        prm_refs = refs[7:7 + nprm]
        nin = 7 + nprm + nride
        ride_src = refs[7 + nprm:nin]
        out_refs = refs[nin:nin + nout]
        ride_dst = refs[nin + nout:nin + nout + nride]
        scr = refs[nin + nout + nride:nin + nout + nride + nscr]
        ride_sems = refs[nin + nout + nride + nscr:]
        proj_ref, xhat_ref, rstd_ref = out_refs[:3]
        i = pl.program_id(0)
        if ride:
            @pl.when(i == 0)
            def _():
                ride.start(ride_src, ride_dst, ride_sems)

        xin = x_ref[...] * g_ref[...] + b_ref[...]
        proj = _nt(xin, wt_ref[...])
        proj_ref[...] = proj
        if kind == 0:
            tok, _ = _gmlp_fwd(proj[:, :TOK], proj[:, TOK:2 * TOK], prm_refs[0], prm_refs[1])
        elif kind == 1:
            st_ref, sts_s, ut_s, dch_s = scr

            @pl.when(i == 0)
            def _():
                st_ref[...] = jnp.zeros_like(st_ref)

            out_refs[4][...] = st_ref[...]
            lb, _ = _lower_bound(prm_refs[0][...], layer)
            tok = _hgrn_fwd(proj[:, :TOK], proj[:, TOK:2 * TOK], proj[:, 2 * TOK:3 * TOK], lb, prm_refs[1][...],
                            st_ref, sts_s, ut_s, dch_s)
        elif kind == 2:
            halo_ref, = scr

            @pl.when(i == 0)
            def _():
                halo_ref[...] = jnp.zeros_like(halo_ref)

            p = proj[:, :TOK]
            tok, _ = _pool_fwd(p, halo_ref[...], i * TS, prm_refs[0][...], prm_refs[1][...])
            halo_ref[...] = p[TS - 16:]
        else:
            halo_ref, h_ref = scr

            @pl.when(i == 0)
            def _():
                halo_ref[...] = jnp.zeros_like(halo_ref)
                h_ref[...] = jnp.zeros_like(h_ref)

            out_refs[4][...] = jnp.broadcast_to(h_ref[...], (8, TOK))
            xb = proj[:, :TOK]
            tok, _ = _lru_fwd(xb, halo_ref[...], i * TS, prm_refs, h_ref[...])
            halo_ref[...] = xb[TS - 8:]
            h_ref[...] = _row(tok, TS - 1)
        qx = proj[:, N - D - XW:N - D]
        gate = proj[:, N - D:]
        xo, ps = _xattn_fwd(qx, ks_ref, vs_ref)
        out_refs[3][...] = jnp.concatenate(ps, axis=1).astype(MXU)
        mixed = jnp.concatenate([tok, xo], axis=1) * (gate * _sigmoid(gate))
        z = ALPHA * xin + _nn(mixed, wout_ref[...])
        cen = z - jnp.mean(z, axis=-1, keepdims=True)
        rstd = lax.rsqrt(jnp.mean(cen * cen, axis=-1, keepdims=True) + LN_EPS)
        xhat_ref[...] = cen * rstd
        rstd_ref[...] = rstd
        if ride:
            @pl.when(i == nt - 1)
            def _():
                ride.wait(ride_src, ride_dst, ride_sems)

    ins = [(xprev, _rows(D, nt, False)), (gprev, _res(gprev)), (bprev, _res(bprev)), (wt, _res(wt)), (wout, _res(wout)),
           (ks, _res(ks)), (vs, _res(vs))] + [(p, _res(p)) for p in prm]
    if ride:
        ins += [(a, _ANY) for a in ride.arrays]
        outs += [(s, _ANY) for s in ride.out_shapes]
        scratch = scratch + ride.scratch
    return _call(body, f"layer{layer}_fwd", (nt,), ins, outs, scratch)


def _layer_bwd(kind, layer, up, is_last, xhat, rstd, probs, g_i, b_i, proj, wout, ks, vs, prm, extra, ride=None):
    S = xhat.shape[0]
    nt = S // TS
    N = proj.shape[1]
    nprm = len(prm)
    nch = TS // CH

    ins = [(up, _rows(D, nt, True)), (xhat, _rows(D, nt, True)), (rstd, _rows(1, nt, True)), (g_i, _res(g_i)), (b_i, _res(b_i)),
           (proj, _rows(N, nt, True)), (wout, _res(wout)), (ks, _res(ks)), (vs, _res(vs)),
           (probs, _rows(XHEADS * NMEM, nt, True))] + [(p, _res(p)) for p in prm]
    nfixed = 10
    if kind == 1:
        ins.append((extra, pl.BlockSpec((None, NH, HD, HD), lambda i: (nt - 1 - i, 0, 0, 0))))
    elif kind == 2:
        hb = TS // 16
        ins.append((proj, pl.BlockSpec((16, TOK), lambda i: (jnp.maximum((nt - 1 - i) * hb - 1, 0), 0))))
    elif kind == 3:
        hb = TS // 8
        ins.append((proj, pl.BlockSpec((8, TOK), lambda i: (jnp.maximum((nt - 1 - i) * hb - 1, 0), 0))))
        ins.append((extra, pl.BlockSpec((8, TOK), lambda i: (nt - 1 - i, 0))))
    nin = len(ins)

    def acc(shape):
        return (jax.ShapeDtypeStruct(shape, F32), _res_sds(shape))

    outs = [(jax.ShapeDtypeStruct((S, D), F32), _rows(D, nt, True)),
            (jax.ShapeDtypeStruct((S, N), MXU), _rows(N, nt, True)),
            (jax.ShapeDtypeStruct((D, D), WIRE), _res_sds((D, D))),
            acc((XHEADS, NMEM, XW)), acc((XHEADS, NMEM, XW)), acc((1, D)), acc((1, D)), acc((1, HD))]
    scratch = []
    if kind == 0:
        outs += [acc((NH, HD, HD)), acc((NH, HD, HD))]
    elif kind == 1:
        outs += [acc((1, TOK)), acc((1, TOK))]
        scratch = [pltpu.VMEM((NH, HD, HD), F32), pltpu.VMEM((nch, HD, HD), F32), pltpu.VMEM((nch, HD, HD), F32),
                   pltpu.VMEM((nch, HD), F32), pltpu.VMEM((nch, HD, HD), F32)]
    elif kind == 2:
        outs += [acc((TOK, TOK)), acc((1, TOK))]
        scratch = [pltpu.VMEM((16, TOK), F32)]
    else:
        outs += [acc((4, TOK)), acc((1, TOK)), acc((NH, HD, HD)), acc((1, TOK)), acc((NH, HD, HD)), acc((1, TOK)), acc((1, TOK))]
        scratch = [pltpu.VMEM((1, TOK), F32), pltpu.VMEM((1, TOK), F32), pltpu.VMEM((8, TOK), F32)]
    scratch = scratch + [pltpu.VMEM((D, D), F32)]
    nout = len(outs)
    nscr = len(scratch)
    nride = len(ride.arrays) if ride else 0

    def body(*refs):
        up_ref, xhat_ref, rstd_ref, g_ref, b_ref, proj_ref, wout_ref, ks_ref, vs_ref, probs_ref = refs[:nfixed]
        prm_refs = refs[nfixed:nfixed + nprm]
        ext_refs = refs[nfixed + nprm:nin]
        ride_src = refs[nin:nin + nride]
        o0 = nin + nride
        out_refs = refs[o0:o0 + nout]
        ride_dst = refs[o0 + nout:o0 + nout + nride]
        scr = refs[o0 + nout + nride:o0 + nout + nride + nscr - 1]
        dwout_acc = refs[o0 + nout + nride + nscr - 1]
        ride_sems = refs[o0 + nout + nride + nscr:]
        dres_ref, dproj_ref, dwout_ref, dks_ref, dvs_ref, dg_ref, db_ref, loss_ref = out_refs[:8]
        pgrad = out_refs[8:]
        i = pl.program_id(0)
        tile = nt - 1 - i

        @pl.when(i == 0)
        def _():
            if ride:
                ride.start(ride_src, ride_dst, ride_sems)
            for r in out_refs[3:]:
                r[...] = jnp.zeros_like(r)
            dwout_acc[...] = jnp.zeros_like(dwout_acc)
            for r in scr:
                if kind != 1 or r is scr[0]:
                    r[...] = jnp.zeros_like(r)

        xhat_v = xhat_ref[...]
        if is_last:
            err = xhat_v * g_ref[...] + b_ref[...] - up_ref[...]
            dxo = err * (1.0 / D)
            loss_ref[...] += jnp.sum(0.5 * jnp.mean(err * err, axis=-1, keepdims=True), axis=0, keepdims=True)
        else:
            dxo = up_ref[...]
        _acc(dg_ref, _rowsum(dxo * xhat_v))
        _acc(db_ref, _rowsum(dxo))
        dxh = dxo * g_ref[...]
        dz = rstd_ref[...] * (dxh - jnp.mean(dxh, axis=-1, keepdims=True)
                              - xhat_v * jnp.mean(dxh * xhat_v, axis=-1, keepdims=True))
        dres_ref[...] = ALPHA * dz

        proj = proj_ref[...]
        qx = proj[:, N - D - XW:N - D]
        gate = proj[:, N - D:]
        sgate = _sigmoid(gate)
        silu = gate * sgate
        dmixed = _nt(dz, wout_ref[...])
        dcat = dmixed * silu
        dtok = dcat[:, :TOK]
        if kind == 0:
            u, v = proj[:, :TOK], proj[:, TOK:2 * TOK]
            tok, fres = _gmlp_fwd(u, v, prm_refs[0], prm_refs[1])
            du, dv = _gmlp_bwd(u, v, fres, dtok, pgrad[0], pgrad[1])
            dproj_ref[:, :TOK] = du.astype(MXU)
            dproj_ref[:, TOK:2 * TOK] = dv.astype(MXU)
        elif kind == 1:
            dst_ref, sts_s, ut_s, dch_s, dstn_s = scr
            lb, _ = _lower_bound(prm_refs[0][...], layer)
            q, fl, inp = proj[:, :TOK], proj[:, TOK:2 * TOK], proj[:, 2 * TOK:3 * TOK]
            tok, dq, dfl, dinp = _hgrn_bwd(q, fl, inp, lb, prm_refs[1][...], dtok, ext_refs[0], dst_ref, sts_s, ut_s, dch_s,
                                           dstn_s, pgrad[1], pgrad[0])
            dproj_ref[:, :TOK] = dq.astype(MXU)
            dproj_ref[:, TOK:2 * TOK] = dfl.astype(MXU)
            dproj_ref[:, 2 * TOK:3 * TOK] = dinp.astype(MXU)
        elif kind == 2:
            p = proj[:, :TOK]
            halo = jnp.where(tile == 0, 0.0, ext_refs[0][...])
            tok, fres = _pool_fwd(p, halo, tile * TS, prm_refs[0][...], prm_refs[1][...])
            dp = _pool_bwd(fres, dtok, scr[0], prm_refs[0][...], prm_refs[1][...], pgrad[0], pgrad[1])
            dproj_ref[:, :TOK] = dp.astype(MXU)
        else:
            xb = proj[:, :TOK]
            halo = jnp.where(tile == 0, 0.0, ext_refs[0][...])
            h0 = ext_refs[1][0:1]
            tok, fres = _lru_fwd(xb, halo, tile * TS, prm_refs, h0)
            dxb = _lru_bwd(fres, tok, h0, dtok, prm_refs, scr, pgrad)
            dproj_ref[:, :TOK] = dxb.astype(MXU)
        ps = [probs_ref[:, h * NMEM:(h + 1) * NMEM].astype(F32) for h in range(XHEADS)]
        xo = _nn(ps[0], vs_ref[0])
        for h in range(1, XHEADS):
            xo = xo + _nn(ps[h], vs_ref[h])
        dqx = _xattn_bwd(qx, ps, dcat[:, TOK:], ks_ref, vs_ref, dks_ref, dvs_ref)
        cat = jnp.concatenate([tok, xo], axis=1)
        dwout_acc[...] += _tn(cat * silu, dz)
        dgate = dmixed * cat * (sgate * (1.0 + gate * (1.0 - sgate)))
        dproj_ref[:, N - D - XW:N - D] = dqx.astype(MXU)
        dproj_ref[:, N - D:] = dgate.astype(MXU)

        @pl.when(i == nt - 1)
        def _():
            dwout_ref[...] = dwout_acc[...].astype(WIRE)
            if ride:
                ride.wait(ride_src, ride_dst, ride_sems)

    if ride:
        ins += [(a, _ANY) for a in ride.arrays]
        outs += [(s, _ANY) for s in ride.out_shapes]
        scratch = scratch + ride.scratch
    return _call(body, f"layer{layer}_bwd", (nt,), ins, outs, scratch)


def _proj_bwd(layer, dproj, dres, xprev, gprev, bprev, wt, ride=None):
    S = xprev.shape[0]
    nt = S // TSB
    N = wt.shape[0]

    nride = len(ride.arrays) if ride else 0

    def body(*refs):
        dproj_ref, dres_ref, x_ref, g_ref, b_ref, wt_ref = refs[:6]
        ride_src = refs[6:6 + nride]
        dx_ref, dwt_ref = refs[6 + nride:8 + nride]
        ride_dst = refs[8 + nride:8 + 2 * nride]
        acc_ref = refs[8 + 2 * nride]
        ride_sems = refs[9 + 2 * nride:]

        @pl.when(pl.program_id(0) == 0)
        def _():
            if ride:
                ride.start(ride_src, ride_dst, ride_sems)
            acc_ref[...] = jnp.zeros_like(acc_ref)

        dp = dproj_ref[...]
        xin = x_ref[...] * g_ref[...] + b_ref[...]
        dx_ref[...] = dres_ref[...] + _nn(dp, wt_ref[...])
        acc_ref[...] += _tn(dp, xin)

        @pl.when(pl.program_id(0) == nt - 1)
        def _():
            dwt_ref[...] = acc_ref[...].astype(WIRE)
            if ride:
                ride.wait(ride_src, ride_dst, ride_sems)

    ins = [(dproj, _rows(N, nt, False, TSB)), (dres, _rows(D, nt, False, TSB)), (xprev, _rows(D, nt, False, TSB)),
           (gprev, _res(gprev)), (bprev, _res(bprev)), (wt, _res(wt))]
    outs = [(jax.ShapeDtypeStruct((S, D), F32), _rows(D, nt, False, TSB)),
            (jax.ShapeDtypeStruct((N, D), WIRE), _res_sds((N, D)))]
    scratch = [pltpu.VMEM((N, D), F32)]
    if ride:
        ins += [(a, _ANY) for a in ride.arrays]
        outs += [(s, _ANY) for s in ride.out_shapes]
        scratch = scratch + ride.scratch
    return _call(body, f"layer{layer}_projbwd", (nt,), ins, outs, scratch)


def _head_mask(h):
    col = lax.broadcasted_iota(jnp.int32, (1, XW), 1)
    return (col // 64) == h


def _kv_fwd(mem, wkv):
    def body(mem_ref, w_ref, ks_ref, vs_ref):
        kv = _nn(mem_ref[...], w_ref[...])
        k, v = kv[:, :XW], kv[:, XW:]
        for h in range(XHEADS):
            ks_ref[h] = jnp.where(_head_mask(h), k, 0.0).astype(MXU)
            vs_ref[h] = jnp.where(_head_mask(h), v, 0.0).astype(MXU)

    sds = jax.ShapeDtypeStruct((XHEADS, NMEM, XW), MXU)
    return pl.pallas_call(body, name="kv_fwd", out_shape=(sds, sds), compiler_params=_cparams())(mem, wkv)


def _kv_bwd(mem, dks_l, dvs_l):
    def body(mem_ref, *refs):
        dks_refs, dvs_refs, out_ref = refs[:DEPTH], refs[DEPTH:2 * DEPTH], refs[2 * DEPTH]
        dk = jnp.zeros((NMEM, XW), F32)
        dv = jnp.zeros((NMEM, XW), F32)
        for h in range(XHEADS):
            m = _head_mask(h)
            for l in range(DEPTH):
                dk = dk + jnp.where(m, dks_refs[l][h], 0.0)
                dv = dv + jnp.where(m, dvs_refs[l][h], 0.0)
        out_ref[...] = _tn(mem_ref[...], jnp.concatenate([dk, dv], axis=1)).astype(WIRE)

    return pl.pallas_call(body, name="kv_bwd", out_shape=jax.ShapeDtypeStruct((D, 2 * XW), WIRE),
                          compiler_params=_cparams())(mem, *dks_l, *dvs_l)


def _prep_weights(w_ins, w_out, wkv):
    def body(a_ref, b_ref, c_ref, d_ref, wo_ref, kv_ref, ao, bo, co, do, wo0, wo1, wo2, wo3, kvo):
        for src, dst in ((a_ref, ao), (b_ref, bo), (c_ref, co), (d_ref, do)):
            dst[...] = src[...].T.astype(MXU)
        for l, dst in enumerate((wo0, wo1, wo2, wo3)):
            dst[...] = wo_ref[l].astype(MXU)
        kvo[...] = kv_ref[...].astype(MXU)

    outs = [jax.ShapeDtypeStruct((w.shape[1], w.shape[0]), MXU) for w in w_ins]
    outs += [jax.ShapeDtypeStruct(w_out.shape[1:], MXU)] * DEPTH + [jax.ShapeDtypeStruct(wkv.shape, MXU)]
    return pl.pallas_call(body, name="prep_weights", out_shape=outs, compiler_params=_cparams())(*w_ins, w_out, wkv)


def _adam_math(w, g, m, v):
    m = B1 * m + (1.0 - B1) * g
    v = B2 * v + (1.0 - B2) * (g * g)
    m_hat = m / (1.0 - B1 ** STEP)
    v_hat = v / (1.0 - B2 ** STEP)
    delta = -LR * (m_hat / (jnp.sqrt(v_hat) + EPS) + WD * w)
    return delta, m, v


def _sum_adam(name, recv, w, m, v, transpose):
    rows, cols = recv.shape[1], recv.shape[2]

    def body(r_ref, w_ref, m_ref, v_ref, g_out, d_out, m_out, v_out, acc_ref):
        s = pl.program_id(0)

        @pl.when(s == 0)
        def _():
            acc_ref[...] = r_ref[...].astype(F32)

        @pl.when(s > 0)
        def _():
            acc_ref[...] += r_ref[...].astype(F32)

        @pl.when(s == NDEV - 1)
        def _():
            g = acc_ref[...].T if transpose else acc_ref[...]
            d, mn, vn = _adam_math(w_ref[...], g, m_ref[...], v_ref[...])
            g_out[...] = g
            d_out[...] = d
            m_out[...] = mn
            v_out[...] = vn

    sds = jax.ShapeDtypeStruct(w.shape, F32)
    ins = [(recv, pl.BlockSpec((None, rows, cols), lambda s: (s, 0, 0))), (w, _res(w)), (m, _res(m)), (v, _res(v))]
    outs = [(sds, _res_sds(w.shape))] * 4
    return _call(body, name, (NDEV,), ins, outs, [pltpu.VMEM((rows, cols), F32)])


def _bias_finalize(dbs_exp):
    def body(dbs_ref, dabs_ref):
        dabs_ref[...] = jnp.sum(dbs_ref[...], axis=-1)

    return pl.pallas_call(body, name="bias_finalize", out_shape=jax.ShapeDtypeStruct((NH, HD), F32),
                          compiler_params=_cparams())(dbs_exp)


def _lb_finalize(dlb, lb_logits):
    def body(dlb_ref, lg_ref, dlg_ref):
        total = jnp.zeros((DEPTH, TOK), F32)
        lg = lg_ref[...]
        e = jnp.exp(lg - jnp.max(lg, axis=0, keepdims=True))
        p = e / jnp.sum(e, axis=0, keepdims=True)
        row = lax.broadcasted_iota(jnp.int32, (DEPTH, TOK), 0)
        for layer in range(DEPTH):
            if layer % 4 != 1:
                continue
            dp = jnp.where((row >= 1) & (row <= layer), dlb_ref[...], 0.0)
            total = total + p * (dp - jnp.sum(p * dp, axis=0, keepdims=True))
        dlg_ref[...] = total

    return pl.pallas_call(body, name="lb_finalize", out_shape=jax.ShapeDtypeStruct((DEPTH, TOK), F32),
                          compiler_params=_cparams())(dlb, lb_logits)


def _small_sum_adam(name, gathered, w, m, v):
    rows = w.shape[0]

    def body(r_ref, w_ref, m_ref, v_ref, g_out, d_out, m_out, v_out):
        g = r_ref[0]
        for s in range(1, NDEV):
            g = g + r_ref[s]
        d, mn, vn = _adam_math(w_ref[...], g, m_ref[...], v_ref[...])
        g_out[...] = g
        d_out[...] = d
        m_out[...] = mn
        v_out[...] = vn

    sds = jax.ShapeDtypeStruct((rows, 128), F32)
    return pl.pallas_call(body, name=name, out_shape=(sds,) * 4, compiler_params=_cparams())(gathered, w, m, v)


def _adam_only(g, w, m, v):
    def body(g_ref, w_ref, m_ref, v_ref, d_out, m_out, v_out):
        d, mn, vn = _adam_math(w_ref[...], g_ref[...], m_ref[...], v_ref[...])
        d_out[...] = d
        m_out[...] = mn
        v_out[...] = vn

    sds = jax.ShapeDtypeStruct(w.shape, F32)
    return pl.pallas_call(body, name="shard_adam", out_shape=(sds,) * 3, compiler_params=_cparams())(g, w, m, v)


def _me_and_peers():
    x, y, c = lax.axis_index("x"), lax.axis_index("y"), lax.axis_index("c")
    me = 4 * x + 2 * y + c
    peers = []
    for k in range(1, NDEV):
        kx, ky, kc = (k >> 2) & 1, (k >> 1) & 1, k & 1
        px = x + kx - 2 * x * kx
        py = y + ky - 2 * y * ky
        pc = c + kc - 2 * c * kc
        peers.append(((px, py, pc), 4 * px + 2 * py + pc))
    return me, peers


_ANY = pl.BlockSpec(memory_space=pl.ANY)


class _Exchange:
    def __init__(self, arrays, split):
        self.arrays = list(arrays)
        self.split = list(split)
        n = len(self.arrays)
        self.out_shapes = []
        for a, sp in zip(self.arrays, self.split):
            rows = a.shape[0] // NDEV if sp else a.shape[0]
            self.out_shapes.append(jax.ShapeDtypeStruct((NDEV, rows, a.shape[1]), a.dtype))
        self.scratch = [pltpu.SemaphoreType.DMA((n, NDEV - 1)), pltpu.SemaphoreType.DMA((n, NDEV - 1)),
                        pltpu.SemaphoreType.DMA((n,))]

    def _block(self, src, t, d):
        if not self.split[t]:
            return src[t]
        rows = self.arrays[t].shape[0] // NDEV
        return src[t].at[pl.ds(d * rows, rows)]

    def start(self, src, dst, sems):
        send_sems, recv_sems, local_sems = sems
        me, peers = _me_and_peers()
        for t in range(len(self.arrays)):
            pltpu.make_async_copy(self._block(src, t, me), dst[t].at[me], local_sems.at[t]).start()
        for k, (dev, idx) in enumerate(peers):
            for t in range(len(self.arrays)):
                pltpu.make_async_remote_copy(src_ref=self._block(src, t, idx), dst_ref=dst[t].at[me],
                                             send_sem=send_sems.at[t, k], recv_sem=recv_sems.at[t, k],
                                             device_id=dev, device_id_type=pl.DeviceIdType.MESH).start()

    def wait(self, src, dst, sems):
        send_sems, recv_sems, local_sems = sems
        me, peers = _me_and_peers()

        def slot_copy(t, k, dev, idx):
            return pltpu.make_async_remote_copy(src_ref=dst[t].at[idx], dst_ref=dst[t].at[idx], send_sem=send_sems.at[t, k],
                                                recv_sem=recv_sems.at[t, k], device_id=dev,
                                                device_id_type=pl.DeviceIdType.MESH)

        for k, (dev, idx) in enumerate(peers):
            for t in range(len(self.arrays)):
                slot_copy(t, k, dev, idx).wait_recv()
        for k, (dev, idx) in enumerate(peers):
            for t in range(len(self.arrays)):
                slot_copy(t, k, dev, idx).wait_send()
        for t in range(len(self.arrays)):
            pltpu.make_async_copy(dst[t].at[me], dst[t].at[me], local_sems.at[t]).wait()

    def run(self, name):
        n = len(self.arrays)

        def body(*refs):
            src, dst, sems = refs[:n], refs[n:2 * n], refs[2 * n:]
            self.start(src, dst, sems)
            self.wait(src, dst, sems)

        return pl.pallas_call(
            body, name=name, out_shape=self.out_shapes, in_specs=[_ANY] * n, out_specs=[_ANY] * n,
            scratch_shapes=self.scratch,
        )(*self.arrays)


SMALL = [("ln_g", (DEPTH, D), False), ("ln_b", (DEPTH, D), False), ("hgrn_lb_logits", (DEPTH, TOK), False),
         ("a_w_s", (1, NH, HD, HD), False), ("a_b_s", (1, NH, HD), False), ("b_norm_g", (1, TOK), True),
         ("c_w_pool", (1, 4, POOL_GROUP, POOL_GROUP), False), ("c_scale", (1, TOK), True),
         ("d_conv_w", (1, 4, TOK), True), ("d_conv_b", (1, TOK), True),
         ("d_w_gx", (1, NH, HD, HD), False), ("d_b_gx", (1, NH, HD), False),
         ("d_w_ga", (1, NH, HD, HD), False), ("d_b_ga", (1, NH, HD), False), ("d_a_param", (1, TOK), True)]


def _pack(parts, total_rows):
    flat = jnp.concatenate([p.reshape(-1).astype(F32) for p in parts])
    flat = jnp.pad(flat, (0, total_rows * 128 - flat.shape[0]))
    return flat.reshape(total_rows, 128)


def _size(shape):
    n = 1
    for s in shape:
        n *= s
    return n


def _rows_for(n):
    return -(-n // 1024) * 8


def kernel(x, mem, mem_kv_w, ln_g, ln_b, w_out, hgrn_lb_logits, a_w_in, a_w_s, a_b_s, b_w_in, b_norm_g, c_w_in, c_w_pool, c_scale, d_w_in, d_conv_w, d_conv_b, d_w_gx, d_b_gx, d_w_ga, d_b_ga, d_a_param, loss_target, m_mem_kv_w, m_ln_g, m_ln_b, m_w_out, m_hgrn_lb_logits, m_a_w_in, m_a_w_s, m_a_b_s, m_b_w_in, m_b_norm_g, m_c_w_in, m_c_w_pool, m_c_scale, m_d_w_in, m_d_conv_w, m_d_conv_b, m_d_w_gx, m_d_b_gx, m_d_w_ga, m_d_b_ga, m_d_a_param, v_mem_kv_w, v_ln_g, v_ln_b, v_w_out, v_hgrn_lb_logits, v_a_w_in, v_a_w_s, v_a_b_s, v_b_w_in, v_b_norm_g, v_c_w_in, v_c_w_pool, v_c_scale, v_d_w_in, v_d_conv_w, v_d_conv_b, v_d_w_gx, v_d_b_gx, v_d_w_ga, v_d_b_ga, v_d_a_param):
    W = dict(mem_kv_w=mem_kv_w, ln_g=ln_g, ln_b=ln_b, w_out=w_out, hgrn_lb_logits=hgrn_lb_logits, a_w_in=a_w_in, a_w_s=a_w_s,
             a_b_s=a_b_s, b_w_in=b_w_in, b_norm_g=b_norm_g, c_w_in=c_w_in, c_w_pool=c_w_pool, c_scale=c_scale, d_w_in=d_w_in,
             d_conv_w=d_conv_w, d_conv_b=d_conv_b, d_w_gx=d_w_gx, d_b_gx=d_b_gx, d_w_ga=d_w_ga, d_b_ga=d_b_ga, d_a_param=d_a_param)
    M = dict(mem_kv_w=m_mem_kv_w, ln_g=m_ln_g, ln_b=m_ln_b, w_out=m_w_out, hgrn_lb_logits=m_hgrn_lb_logits, a_w_in=m_a_w_in,
             a_w_s=m_a_w_s, a_b_s=m_a_b_s, b_w_in=m_b_w_in, b_norm_g=m_b_norm_g, c_w_in=m_c_w_in, c_w_pool=m_c_w_pool,
             c_scale=m_c_scale, d_w_in=m_d_w_in, d_conv_w=m_d_conv_w, d_conv_b=m_d_conv_b, d_w_gx=m_d_w_gx, d_b_gx=m_d_b_gx,
             d_w_ga=m_d_w_ga, d_b_ga=m_d_b_ga, d_a_param=m_d_a_param)
    V = dict(mem_kv_w=v_mem_kv_w, ln_g=v_ln_g, ln_b=v_ln_b, w_out=v_w_out, hgrn_lb_logits=v_hgrn_lb_logits, a_w_in=v_a_w_in,
             a_w_s=v_a_w_s, a_b_s=v_a_b_s, b_w_in=v_b_w_in, b_norm_g=v_b_norm_g, c_w_in=v_c_w_in, c_w_pool=v_c_w_pool,
             c_scale=v_c_scale, d_w_in=v_d_w_in, d_conv_w=v_d_conv_w, d_conv_b=v_d_conv_b, d_w_gx=v_d_w_gx, d_b_gx=v_d_b_gx,
             d_w_ga=v_d_w_ga, d_b_ga=v_d_b_ga, d_a_param=v_d_a_param)
    me = 4 * lax.axis_index("x") + 2 * lax.axis_index("y") + lax.axis_index("c")
    x2, mem2, tgt2 = x[0], mem[0], loss_target[0]
    in_names = ["a_w_in", "b_w_in", "c_w_in", "d_w_in"]

    shard_names = [n for n, _, sh in SMALL if sh]
    small_shard = _pack([W[n] for n in shard_names], 8)
    wts = _prep_weights([W[n][0] for n in in_names], w_out, mem_kv_w)
    wt_sh, wo_sh, wkv_sh = wts[:4], wts[4:8], wts[8]
    g0 = _Exchange([wt_sh[0], wo_sh[0], wkv_sh, small_shard], [False] * 4).run("gather_first")
    wt_full = [g0[0].reshape(-1, D)]
    wout_full = [g0[1].reshape(D, D)]
    wkv_full = g0[2].reshape(D, 2 * XW)
    sm = g0[3].reshape(NDEV, 1024)
    full_small = {}
    off = 0
    for n, shape, _ in [s for s in SMALL if s[2]]:
        per = _size(shape) // NDEV
        blk = sm[:, off:off + per]
        if n == "d_conv_w":
            full_small[n] = blk.reshape(NDEV, 4, TOK // NDEV).transpose(1, 0, 2).reshape(4, TOK)
        else:
            full_small[n] = blk.reshape(1, TOK)
        off += per

    ks, vs = _kv_fwd(mem2, wkv_full)
    tri_bs = jnp.broadcast_to(a_b_s[0][:, :, None], (NH, HD, HD))
    wbd = jnp.zeros((TOK, TOK), F32)
    for g in range(4):
        wbd = lax.dynamic_update_slice(wbd, c_w_pool[0, g], (g * POOL_GROUP, g * POOL_GROUP))
    wbd = wbd.astype(MXU)
    prm = {0: [a_w_s[0], tri_bs],
           1: [hgrn_lb_logits, full_small["b_norm_g"]],
           2: [wbd, full_small["c_scale"]],
           3: [full_small["d_conv_w"], full_small["d_conv_b"], d_w_gx[0].astype(MXU), d_b_gx[0].reshape(1, TOK),
               d_w_ga[0].astype(MXU), d_b_ga[0].reshape(1, TOK), full_small["d_a_param"]]}
    ones = jnp.ones((1, D), F32)
    zeros = jnp.zeros((1, D), F32)
    xs, gs, bs = [x2], [ones], [zeros]
    saved = []
    for i in range(DEPTH):
        ride = _Exchange([wt_sh[i + 1], wo_sh[i + 1]], [False, False]) if i + 1 < DEPTH else None
        res = _layer_fwd(i, i, xs[i], gs[i], bs[i], wt_full[i], wout_full[i], ks, vs, prm[i], ride)
        if ride:
            wt_full.append(res[-2].reshape(-1, D))
            wout_full.append(res[-1].reshape(D, D))
            res = res[:-2]
        saved.append(res)
        xs.append(res[1])
        gs.append(ln_g[i:i + 1])
        bs.append(ln_b[i:i + 1])

    up = tgt2
    grads = {}
    dks_l, dvs_l, dwt_l, dwout_l, dlng_l, dlnb_l = [], [], [], [], [], []
    recv_wt, recv_wo = [None] * DEPTH, [None] * DEPTH
    loss_part = None
    small_shape = {n: s for n, s, _ in SMALL}
    small_shape.update({f"ln_g#{l}": (1, D) for l in range(DEPTH)})
    small_shape.update({f"ln_b#{l}": (1, D) for l in range(DEPTH)})
    small_shape["loss"] = (128,)
    sharded = {n for n, _, sh in SMALL if sh}
    group = {3: ["ln_g#3", "ln_b#3", "d_conv_w", "d_conv_b", "d_w_gx", "d_b_gx", "d_w_ga", "d_b_ga", "d_a_param"],
             2: ["ln_g#2", "ln_b#2", "c_w_pool", "c_scale"],
             1: ["ln_g#1", "ln_b#1", "hgrn_lb_logits", "b_norm_g"],
             0: ["ln_g#0", "ln_b#0", "a_w_s", "a_b_s", "loss"]}
    group_rows = {l: _rows_for(sum(_size(small_shape[e]) for e in group[l])) for l in range(DEPTH)}
    recv_small = [None] * DEPTH
    for i in reversed(range(DEPTH)):
        res = saved[i]
        extra = res[4] if len(res) > 4 else None
        ride = None
        if i + 1 < DEPTH:
            small_vec = _pack([grads[e] for e in group[i + 1]], group_rows[i + 1])
            ride = _Exchange([dwt_l[-1], dwout_l[-1], small_vec], [True, True, False])
        out = _layer_bwd(i, i, up, i == DEPTH - 1, res[1], res[2], res[3], gs[i + 1], bs[i + 1], res[0], wout_full[i], ks, vs,
                         prm[i], extra, ride)
        if ride:
            recv_wt[i + 1], recv_wo[i + 1], recv_small[i + 1] = out[-3], out[-2], out[-1]
            out = out[:-3]
        dres, dproj, dwout_i, dks_i, dvs_i, dg_i, db_i, loss_i = out[:8]
        pg = out[8:]
        if i == DEPTH - 1:
            grads["loss"] = loss_i[0]
        dks_l.append(dks_i)
        dvs_l.append(dvs_i)
        dwout_l.append(dwout_i)
        grads[f"ln_g#{i}"], grads[f"ln_b#{i}"] = dg_i, db_i
        if i == 0:
            grads["a_w_s"], dbs_exp = pg
            grads["a_b_s"] = _bias_finalize(dbs_exp)
        elif i == 1:
            dlb, grads["b_norm_g"] = pg
            grads["hgrn_lb_logits"] = _lb_finalize(dlb, hgrn_lb_logits)
        elif i == 2:
            dwbd, grads["c_scale"] = pg
            grads["c_w_pool"] = jnp.stack([lax.dynamic_slice(dwbd, (g * POOL_GROUP, g * POOL_GROUP), (POOL_GROUP, POOL_GROUP))
                                           for g in range(4)])
        else:
            (grads["d_conv_w"], grads["d_conv_b"], grads["d_w_gx"], grads["d_b_gx"], grads["d_w_ga"], grads["d_b_ga"],
             grads["d_a_param"]) = pg
        ride = None
        if i == 0:
            dwkv = _kv_bwd(mem2, dks_l, dvs_l)
            small_vec = _pack([grads[e] for e in group[0]], group_rows[0])
            ride = _Exchange([dwout_i, dwkv, small_vec], [True, True, False])
        pb = _proj_bwd(i, dproj, dres, xs[i], gs[i], bs[i], wt_full[i], ride)
        up, dwt = pb[:2]
        if ride:
            recv_wo[0], recv_kv, recv_small[0] = pb[2:]
        dwt_l.append(dwt)
    grad_x = up[None]

    recv_wt[0], = _Exchange([dwt_l[-1]], [True]).run("scatter_last")

    outs = {}
    for t, n in enumerate(in_names):
        g, d, mn, vn = _sum_adam(f"adam_{n}", recv_wt[t], W[n][0], M[n][0], V[n][0], True)
        outs[n] = (g[None], d[None], mn[None], vn[None])
    wo_res = [_sum_adam(f"adam_w_out{l}", recv_wo[l], w_out[l], m_w_out[l], v_w_out[l], False) for l in range(DEPTH)]
    outs["w_out"] = tuple(jnp.stack([wo_res[l][j] for l in range(DEPTH)]) for j in range(4))
    outs["mem_kv_w"] = _sum_adam("adam_mem_kv_w", recv_kv, mem_kv_w, m_mem_kv_w, v_mem_kv_w, False)

    def entry_of(tree, e):
        if "#" in e:
            n, l = e.split("#")
            return tree[n][int(l):int(l) + 1]
        if e == "loss" or e in sharded:
            return jnp.zeros(small_shape[e], F32)
        return tree[e]

    small = [{}, {}, {}, {}]
    for l in range(DEPTH):
        packed = [_pack([entry_of(t, e) for e in group[l]], group_rows[l]) for t in (W, M, V)]
        res = _small_sum_adam(f"small_adam{l}", recv_small[l], *packed)
        for j in range(4):
            flat, o = res[j].reshape(-1), 0
            for e in group[l]:
                small[j][e] = flat[o:o + _size(small_shape[e])].reshape(small_shape[e])
                o += _size(small_shape[e])
    loss = small[0]["loss"][0]
    for j in range(4):
        for n in ("ln_g", "ln_b"):
            small[j][n] = jnp.concatenate([small[j][f"{n}#{l}"] for l in range(DEPTH)], axis=0)
    g_small = small[0]
    for n, _, sh in SMALL:
        if not sh:
            outs[n] = tuple(small[j][n] for j in range(4))
    per = TOK // NDEV
    g_sh = {n: lax.dynamic_slice_in_dim(g_small[n], me * per, per, axis=len(s) - 1) for n, s, sh in SMALL if sh}
    gp = _pack([g_sh[n] for n in shard_names], 8)
    d_p, m_p, v_p = _adam_only(gp, small_shard, _pack([M[n] for n in shard_names], 8), _pack([V[n] for n in shard_names], 8))
    o = 0
    for n in shard_names:
        cnt = _size(W[n].shape)
        outs[n] = (g_sh[n],) + tuple(t.reshape(-1)[o:o + cnt].reshape(W[n].shape) for t in (d_p, m_p, v_p))
        o += cnt

    order = ["mem_kv_w", "ln_g", "ln_b", "w_out", "hgrn_lb_logits", "a_w_in", "a_w_s", "a_b_s", "b_w_in", "b_norm_g", "c_w_in",
             "c_w_pool", "c_scale", "d_w_in", "d_conv_w", "d_conv_b", "d_w_gx", "d_b_gx", "d_w_ga", "d_b_ga", "d_a_param"]
    result = [loss, grad_x]
    for j in range(4):
        result += [outs[n][j].reshape(W[n].shape) for n in order]
    return tuple(result)
```

```python
import functools

import jax
import jax.numpy as jnp
from jax import lax
from jax.experimental import pallas as pl
from jax.experimental.pallas import tpu as pltpu

F32 = jnp.float32
MXU = jnp.bfloat16
WIRE = jnp.bfloat16

D = 1024
TOK = 768
XW = 256
NMEM = 256
XHEADS = 4
XSCALE = 64 ** -0.5
NH = 6
HD = 128
CH = 16
POOL_WINDOWS = (2, 4, 8, 16)
POOL_GROUP = 192
DEPTH = 4
ALPHA = (2 * DEPTH) ** 0.25
LN_EPS = 1e-5
RMS_EPS = 1e-6
LRU_C = 8.0
B1, B2, LR, EPS, WD, STEP = 0.9, 0.999, 0.001, 1e-8, 0.01, 10

NDEV = 8
TS_FWD = {0: 512, 1: 256, 2: 512, 3: 256}
TS_BWD = {0: 256, 1: 256, 2: 256, 3: 256}
TSB = 512
VMEM_LIMIT = 58 * 1024 * 1024

KIND_WIDTHS = {0: 2 * TOK + XW + D, 1: 3 * TOK + XW + D, 2: TOK + XW + D, 3: TOK + XW + D}


def _mm(a, b, ca, cb):
    return lax.dot_general(a.astype(MXU), b.astype(MXU), (((ca,), (cb,)), ((), ())), preferred_element_type=F32)


def _nn(a, b):
    return _mm(a, b, 1, 0)


def _nt(a, b):
    return _mm(a, b, 1, 1)


def _tn(a, b):
    return _mm(a, b, 0, 0)


def _bmm(a, b, ca, cb):
    return lax.dot_general(a.astype(MXU), b.astype(MXU), (((ca,), (cb,)), ((0,), (0,))), preferred_element_type=F32)


def _sigmoid(x):
    return 1.0 / (1.0 + jnp.exp(-x))


def _vjp1(fn, x, dy):
    return jax.vjp(fn, x)[1](dy)[0]


def _rowsum(x):
    return jnp.sum(x, axis=0, keepdims=True)


def _row(x, r):
    sel = lax.broadcasted_iota(jnp.int32, x.shape, 0) == r
    return jnp.sum(jnp.where(sel, x, 0.0), axis=0, keepdims=True)


def _acc(ref, val):
    ref[...] += val


def _cparams(sem=None):
    return pltpu.CompilerParams(dimension_semantics=sem, vmem_limit_bytes=VMEM_LIMIT)


def _res(a):
    nd = a.ndim
    return pl.BlockSpec(a.shape, lambda i: (0,) * nd)


def _res_sds(shape):
    nd = len(shape)
    return pl.BlockSpec(shape, lambda i: (0,) * nd)


def _row_spec(width, nt, rev, ts):
    if rev:
        return pl.BlockSpec((ts, width), lambda i: (nt - 1 - i, 0))
    return pl.BlockSpec((ts, width), lambda i: (i, 0))


def _call(body, name, grid, ins, outs, scratch=(), sem=("arbitrary",)):
    arrays = [a for a, _ in ins]
    return pl.pallas_call(
        body, name=name, grid=grid,
        in_specs=[s for _, s in ins],
        out_specs=[s for _, s in outs],
        out_shape=[o for o, _ in outs],
        scratch_shapes=list(scratch),
        compiler_params=_cparams(sem),
    )(*arrays)


def _xattn_fwd(qx, ks_ref, vs_ref):
    o = None
    ps = []
    for h in range(XHEADS):
        s = _nt(qx, ks_ref[h]) * XSCALE
        s = s - jnp.max(s, axis=-1, keepdims=True)
        e = jnp.exp(s)
        p = e / jnp.sum(e, axis=-1, keepdims=True)
        ps.append(p)
        oh = _nn(p, vs_ref[h])
        o = oh if o is None else o + oh
    return o, ps


def _xattn_bwd(qx, ps, dxo, ks_ref, vs_ref, dks_ref, dvs_ref):
    dq = None
    for h in range(XHEADS):
        p = ps[h]
        dp = _nt(dxo, vs_ref[h])
        ds = p * (dp - jnp.sum(dp * p, axis=-1, keepdims=True))
        dqh = _nn(ds, ks_ref[h]) * XSCALE
        dq = dqh if dq is None else dq + dqh
        dks_ref[h] += _tn(ds, qx) * XSCALE
        dvs_ref[h] += _tn(p, dxo)
    return dq


def _tril128():
    r = lax.broadcasted_iota(jnp.int32, (HD, HD), 0)
    c = lax.broadcasted_iota(jnp.int32, (HD, HD), 1)
    return c <= r


def _gmlp_fwd(u, v, ws_ref, bs_ref):
    ts = u.shape[0]
    ug = jax.nn.gelu(u)
    vg = jax.nn.gelu(v)
    tri = _tril128()
    toks, res = [], []
    for g in range(NH):
        sl = slice(g * HD, (g + 1) * HD)
        vgh = vg[:, sl]
        cen = vgh - jnp.mean(vgh, axis=-1, keepdims=True)
        rstd = lax.rsqrt(jnp.mean(cen * cen, axis=-1, keepdims=True) + LN_EPS)
        vn = cen * rstd
        w = jnp.where(tri, ws_ref[g], 0.0).astype(MXU)
        mix = jnp.concatenate([_nn(w, vn[n * HD:(n + 1) * HD]) + bs_ref[g] for n in range(ts // HD)], axis=0)
        toks.append(ug[:, sl] * mix)
        res.append((vn, rstd, mix, w))
    return jnp.concatenate(toks, axis=1), (ug, res)


def _gmlp_bwd(u, v, fres, dtok, dws_ref, dbs_ref):
    ts = u.shape[0]
    ug, res = fres
    tri = _tril128()
    dugs, dvgs = [], []
    for g in range(NH):
        sl = slice(g * HD, (g + 1) * HD)
        vn, rstd, mix, w = res[g]
        dmix = dtok[:, sl] * ug[:, sl]
        dugs.append(dtok[:, sl] * mix)
        dvn_rows = []
        dw = None
        dbs = None
        for n in range(ts // HD):
            dm = dmix[n * HD:(n + 1) * HD]
            dvn_rows.append(_tn(w, dm))
            t = _nt(dm, vn[n * HD:(n + 1) * HD])
            dw = t if dw is None else dw + t
            dbs = dm if dbs is None else dbs + dm
        dws_ref[g] += jnp.where(tri, dw, 0.0)
        dbs_ref[g] += dbs
        dvn = jnp.concatenate(dvn_rows, axis=0)
        dvgs.append(rstd * (dvn - jnp.mean(dvn, axis=-1, keepdims=True) - vn * jnp.mean(dvn * vn, axis=-1, keepdims=True)))
    du = _vjp1(jax.nn.gelu, u, jnp.concatenate(dugs, axis=1))
    dv = _vjp1(jax.nn.gelu, v, jnp.concatenate(dvgs, axis=1))
    return du, dv


def _chunk_cumsum(x):
    row = lax.broadcasted_iota(jnp.int32, x.shape, 0) % CH
    for s in (1, 2, 4, 8):
        x = x + jnp.where(row >= s, pltpu.roll(x, s, 0), 0.0)
    return x


def _chunk_revcumsum(x):
    n = x.shape[0]
    row = lax.broadcasted_iota(jnp.int32, x.shape, 0) % CH
    for s in (1, 2, 4, 8):
        x = x + jnp.where(row < CH - s, pltpu.roll(x, n - s, 0), 0.0)
    return x


def _chunk_sum(x):
    n, w = x.shape
    return jnp.sum(x.reshape(n // CH, CH, w), axis=1)


def _chunk_bcast(c, n):
    nch, w = c.shape
    return jnp.broadcast_to(c[:, None, :], (nch, CH, w)).reshape(n, w)


def _lower_bound(lb_logits, layer):
    lg = lb_logits
    e = jnp.exp(lg - jnp.max(lg, axis=0, keepdims=True))
    p = e / jnp.sum(e, axis=0, keepdims=True)
    row = lax.broadcasted_iota(jnp.int32, p.shape, 0)
    lb = jnp.sum(jnp.where((row >= 1) & (row <= layer), p, 0.0), axis=0, keepdims=True)
    return lb, p


def _hgrn_prep(q, fl, lb):
    n = q.shape[0]
    sg = _sigmoid(fl)
    f = lb + (1.0 - lb) * sg
    lf = jnp.log(f)
    k = 1.0 - f
    sq = _sigmoid(q)
    qf = q * sq
    g = _chunk_cumsum(lf)
    tot = _chunk_sum(lf)
    gl = _chunk_bcast(tot, n)
    eg = jnp.exp(g)
    eng = jnp.exp(-g)
    egl = jnp.exp(gl - g)
    return dict(sg=sg, f=f, k=k, sq=sq, qf=qf, eg=eg, eng=eng, egl=egl,
                qd=qf * eg, ki=k * eng, ke=k * egl, dch=jnp.exp(tot))


def _hgrn_mask():
    r = lax.broadcasted_iota(jnp.int32, (HD, HD), 0)
    c = lax.broadcasted_iota(jnp.int32, (HD, HD), 1)
    return (r // CH == c // CH) & (c <= r)


def _hgrn_states(h, v3, ke3, dch_h, st_in, sts_s, ut_s, dch_s):
    nch = v3.shape[0]
    ut_s[...] = _bmm(v3, ke3, 1, 1)
    dch_s[...] = dch_h

    def step(c, st):
        sts_s[c] = st
        return st * dch_s[pl.ds(c, 1), :] + ut_s[c]

    return lax.fori_loop(0, nch, step, st_in, unroll=True)


def _hgrn_fwd(q, fl, inp, lb, ng, st_ref, sts_s, ut_s, dch_s):
    n = q.shape[0]
    nch = n // CH
    pr = _hgrn_prep(q, fl, lb)
    mask = _hgrn_mask()
    toks = []
    for h in range(NH):
        sl = slice(h * HD, (h + 1) * HD)
        qd, ki, ke, v = pr["qd"][:, sl], pr["ki"][:, sl], pr["ke"][:, sl], inp[:, sl]
        qd3 = qd.astype(MXU).reshape(nch, CH, HD)
        v3 = v.astype(MXU).reshape(nch, CH, HD)
        ke3 = ke.astype(MXU).reshape(nch, CH, HD)
        st_ref[h] = _hgrn_states(h, v3, ke3, pr["dch"][:, sl], st_ref[h], sts_s, ut_s, dch_s)
        o = _bmm(qd3, sts_s[...], 2, 2).reshape(n, HD)
        intra = []
        for b in range(n // HD):
            bs = slice(b * HD, (b + 1) * HD)
            a = jnp.where(mask, _nt(qd[bs], ki[bs]), 0.0)
            intra.append(_nn(a, v[bs]))
        o = o + jnp.concatenate(intra, axis=0)
        r = lax.rsqrt(jnp.mean(o * o, axis=-1, keepdims=True) + RMS_EPS)
        toks.append(o * r * ng[:, sl])
    return jnp.concatenate(toks, axis=1)


def _hgrn_bwd(q, fl, inp, lb, ng, dtok, ststart_ref, dst_ref, sts_s, ut_s, dch_s, dstn_s, dng_ref, dlb_ref):
    n = q.shape[0]
    nch = n // CH
    pr = _hgrn_prep(q, fl, lb)
    mask = _hgrn_mask()
    dqd_l, dki_l, dke_l, dv_l, ddch_l, dng_l, toks = [], [], [], [], [], [], []
    for h in range(NH):
        sl = slice(h * HD, (h + 1) * HD)
        qd, ki, ke, v = pr["qd"][:, sl], pr["ki"][:, sl], pr["ke"][:, sl], inp[:, sl]
        qd3 = qd.astype(MXU).reshape(nch, CH, HD)
        v3 = v.astype(MXU).reshape(nch, CH, HD)
        ke3 = ke.astype(MXU).reshape(nch, CH, HD)
        _hgrn_states(h, v3, ke3, pr["dch"][:, sl], ststart_ref[h], sts_s, ut_s, dch_s)
        sts = sts_s[...]
        o = _bmm(qd3, sts, 2, 2).reshape(n, HD)
        a_l = []
        intra = []
        for b in range(n // HD):
            bs = slice(b * HD, (b + 1) * HD)
            a = jnp.where(mask, _nt(qd[bs], ki[bs]), 0.0)
            a_l.append(a)
            intra.append(_nn(a, v[bs]))
        o = o + jnp.concatenate(intra, axis=0)
        r = lax.rsqrt(jnp.mean(o * o, axis=-1, keepdims=True) + RMS_EPS)
        toks.append(o * r * ng[:, sl])
        dt = dtok[:, sl]
        dng_l.append(_rowsum(dt * o * r))
        dn = dt * ng[:, sl]
        do = r * dn - o * (r * r * r) * jnp.mean(dn * o, axis=-1, keepdims=True)
        do3 = do.astype(MXU).reshape(nch, CH, HD)
        dqd_rows, dki_rows, dv_rows = [], [], []
        for b in range(n // HD):
            bs = slice(b * HD, (b + 1) * HD)
            da = jnp.where(mask, _nt(do[bs], v[bs]), 0.0)
            dqd_rows.append(_nn(da, ki[bs]))
            dki_rows.append(_tn(da, qd[bs]))
            dv_rows.append(_tn(a_l[b], do[bs]))
        dqd = jnp.concatenate(dqd_rows, axis=0) + _bmm(do3, sts, 2, 1).reshape(n, HD)
        dki = jnp.concatenate(dki_rows, axis=0)
        dv = jnp.concatenate(dv_rows, axis=0)
        ut_s[...] = _bmm(do3, qd3, 1, 1)

        def step(i, dst):
            c = nch - 1 - i
            dstn_s[c] = dst
            return ut_s[c] + dst * dch_s[pl.ds(c, 1), :]

        dst_ref[h] = lax.fori_loop(0, nch, step, dst_ref[h], unroll=True)
        dstn = dstn_s[...]
        dv = dv + _bmm(ke3, dstn, 2, 2).reshape(n, HD)
        dke = _bmm(v3, dstn, 2, 1).reshape(n, HD)
        ddch_l.append(jnp.sum(sts * dstn, axis=1))
        dqd_l.append(dqd)
        dki_l.append(dki)
        dke_l.append(dke)
        dv_l.append(dv)
    dqd = jnp.concatenate(dqd_l, axis=1)
    dki = jnp.concatenate(dki_l, axis=1)
    dke = jnp.concatenate(dke_l, axis=1)
    dinp = jnp.concatenate(dv_l, axis=1)
    ddch = jnp.concatenate(ddch_l, axis=1)
    _acc(dng_ref, jnp.concatenate(dng_l, axis=1))
    dqf = dqd * pr["eg"]
    dke_ke = dke * pr["ke"]
    dg = dqd * pr["qd"] - dki * pr["ki"] - dke_ke
    dk = dki * pr["eng"] + dke * pr["egl"]
    dgl = _chunk_sum(dke_ke) + ddch * pr["dch"]
    dlf = _chunk_revcumsum(dg) + _chunk_bcast(dgl, n)
    df = dlf / pr["f"] - dk
    sg = pr["sg"]
    dfl = df * (1.0 - lb) * sg * (1.0 - sg)
    _acc(dlb_ref, _rowsum(df * (1.0 - sg)))
    sq = pr["sq"]
    dq = dqf * (sq * (1.0 + q * (1.0 - sq)))
    return jnp.concatenate(toks, axis=1), dq, dfl, dinp


def _pool_select(s2, s4, s8, s16):
    col = lax.broadcasted_iota(jnp.int32, (1, TOK), 1)
    return jnp.where(col < POOL_GROUP, s2, jnp.where(col < 2 * POOL_GROUP, s4, jnp.where(col < 3 * POOL_GROUP, s8, s16)))


def _pool_cnt(pos0, n):
    pos = pos0 + lax.broadcasted_iota(jnp.int32, (n, TOK), 0) + 1
    col = lax.broadcasted_iota(jnp.int32, (n, TOK), 1)
    w = jnp.where(col < POOL_GROUP, 2, jnp.where(col < 2 * POOL_GROUP, 4, jnp.where(col < 3 * POOL_GROUP, 8, 16)))
    return jnp.minimum(pos, w).astype(F32)


def _pool_fwd(p, halo, pos0, wbd, scale):
    n = p.shape[0]
    ext = jnp.concatenate([halo, p], axis=0)
    s2 = ext + pltpu.roll(ext, 1, 0)
    s4 = s2 + pltpu.roll(s2, 2, 0)
    s8 = s4 + pltpu.roll(s4, 4, 0)
    s16 = s8 + pltpu.roll(s8, 8, 0)
    win = _pool_select(s2, s4, s8, s16)[16:]
    cnt = _pool_cnt(pos0, n)
    diff = win / cnt - p
    y = _nn(diff, wbd)
    return y * scale, (diff, y, cnt)


def _pool_bwd(fres, dtok, nxt_ref, wbd, scale, dwbd_ref, dscale_ref):
    diff, y, cnt = fres
    n = diff.shape[0]
    _acc(dscale_ref, _rowsum(dtok * y))
    dy = dtok * scale
    ddiff = _nt(dy, wbd)
    dwbd_ref[...] += _tn(diff, dy)
    qv = ddiff / cnt
    ext = jnp.concatenate([qv, nxt_ref[...]], axis=0)
    m = n + 16
    s2 = ext + pltpu.roll(ext, m - 1, 0)
    s4 = s2 + pltpu.roll(s2, m - 2, 0)
    s8 = s4 + pltpu.roll(s4, m - 4, 0)
    s16 = s8 + pltpu.roll(s8, m - 8, 0)
    adj = _pool_select(s2, s4, s8, s16)[:n]
    nxt_ref[...] = qv[:16]
    return adj - ddiff


def _neg_expm1(x):
    return jnp.where(jnp.abs(x) < 1e-2, -x * (1.0 + x * (0.5 + x * (1.0 / 6.0))), 1.0 - jnp.exp(x))


def _lru_gates(xc, zx, za, ap, first):
    gx = _sigmoid(zx)
    ga = _sigmoid(za)
    sp = jnp.maximum(-ap, 0.0) + jnp.log(1.0 + jnp.exp(-jnp.abs(ap)))
    log_a = -LRU_C * ga * sp
    a = jnp.exp(log_a)
    mult = jnp.sqrt(_neg_expm1(2.0 * log_a))
    mult = jnp.where(first, 1.0, mult)
    return a, mult * gx * xc


def _scan_fwd(a, b, h0):
    n = a.shape[0]
    row = lax.broadcasted_iota(jnp.int32, a.shape, 0)
    s = 1
    while s < n:
        keep = row >= s
        b = b + a * jnp.where(keep, pltpu.roll(b, s, 0), 0.0)
        a = a * jnp.where(keep, pltpu.roll(a, s, 0), 1.0)
        s *= 2
    return b + a * h0


def _scan_bwd(an, d, dh_next):
    n = an.shape[0]
    row = lax.broadcasted_iota(jnp.int32, an.shape, 0)
    s = 1
    while s < n:
        keep = row < n - s
        d = d + an * jnp.where(keep, pltpu.roll(d, n - s, 0), 0.0)
        an = an * jnp.where(keep, pltpu.roll(an, n - s, 0), 1.0)
        s *= 2
    return d + an * dh_next


def _lru_conv(xb, halo, cw_ref, cb):
    ext = jnp.concatenate([halo, xb], axis=0)
    sh = [pltpu.roll(ext, 3 - j, 0)[8:] if j < 3 else xb for j in range(4)]
    xc = cb
    for j in range(4):
        xc = xc + cw_ref[pl.ds(j, 1), :] * sh[j]
    return xc, sh


def _lru_fwd(xb, halo, pos0, prm, h0):
    cw, cb, wgx, bgx, wga, bga, ap = prm
    n = xb.shape[0]
    xc, sh = _lru_conv(xb, halo, cw, cb[...])
    zx = jnp.concatenate([_nn(xc[:, h * HD:(h + 1) * HD], wgx[h]) for h in range(NH)], axis=1) + bgx[...]
    za = jnp.concatenate([_nn(xc[:, h * HD:(h + 1) * HD], wga[h]) for h in range(NH)], axis=1) + bga[...]
    first = (pos0 + lax.broadcasted_iota(jnp.int32, (n, 1), 0)) == 0
    a, b = _lru_gates(xc, zx, za, ap[...], first)
    hseq = _scan_fwd(a, b, h0)
    return hseq, (xc, sh, zx, za, first, a)


def _lru_bwd(fres, hseq, h0, dtok, prm, carry_refs, grad_refs):
    cw, cb, wgx, bgx, wga, bga, ap = prm
    xc, sh, zx, za, first, a = fres
    anext_ref, dhnext_ref, dxcnext_ref = carry_refs
    dcw_ref, dcb_ref, dwgx_ref, dbgx_ref, dwga_ref, dbga_ref, dap_ref = grad_refs
    n = xc.shape[0]
    an = jnp.where(lax.broadcasted_iota(jnp.int32, a.shape, 0) == n - 1, anext_ref[...], pltpu.roll(a, n - 1, 0))
    dh = _scan_bwd(an, dtok, dhnext_ref[...])
    hprev = jnp.where(lax.broadcasted_iota(jnp.int32, hseq.shape, 0) == 0, h0, pltpu.roll(hseq, 1, 0))
    da = dh * hprev
    anext_ref[...] = _row(a, 0)
    dhnext_ref[...] = _row(dh, 0)
    _, vjp = jax.vjp(lambda xc_, zx_, za_, ap_: _lru_gates(xc_, zx_, za_, ap_, first), xc, zx, za, ap[...])
    dxc, dzx, dza, dap = vjp((da, dh))
    _acc(dap_ref, dap)
    _acc(dbgx_ref, _rowsum(dzx))
    _acc(dbga_ref, _rowsum(dza))
    parts = []
    for h in range(NH):
        sl = slice(h * HD, (h + 1) * HD)
        parts.append(_nt(dzx[:, sl], wgx[h]) + _nt(dza[:, sl], wga[h]))
        dwgx_ref[h] += _tn(xc[:, sl], dzx[:, sl])
        dwga_ref[h] += _tn(xc[:, sl], dza[:, sl])
    dxc = dxc + jnp.concatenate(parts, axis=1)
    _acc(dcb_ref, _rowsum(dxc))
    for j in range(4):
        dcw_ref[pl.ds(j, 1), :] += _rowsum(dxc * sh[j])
    ext = jnp.concatenate([dxc, dxcnext_ref[...]], axis=0)
    m = n + 8
    dxb = cw[pl.ds(3, 1), :] * dxc
    for j in range(3):
        dxb = dxb + cw[pl.ds(j, 1), :] * pltpu.roll(ext, m - (3 - j), 0)[:n]
    dxcnext_ref[...] = dxc[:8]
    return dxb


def _layer_fwd(kind, layer, xprev, gprev, bprev, wt, wout, ks, vs, prm, ride=None):
    TS = TS_FWD[kind]
    _rows = functools.partial(_row_spec, ts=TS)
    S = xprev.shape[0]
    nt = S // TS
    N = wt.shape[0]
    nprm = len(prm)
    nch = TS // CH

    outs = [(jax.ShapeDtypeStruct((S, N), F32), _rows(N, nt, False)),
            (jax.ShapeDtypeStruct((S, D), F32), _rows(D, nt, False)),
            (jax.ShapeDtypeStruct((S, 1), F32), _rows(1, nt, False)),
            (jax.ShapeDtypeStruct((S, XHEADS * NMEM), MXU), _rows(XHEADS * NMEM, nt, False))]
    scratch = []
    if kind == 1:
        outs.append((jax.ShapeDtypeStruct((nt, NH, HD, HD), F32), pl.BlockSpec((None, NH, HD, HD), lambda i: (i, 0, 0, 0))))
        scratch = [pltpu.VMEM((NH, HD, HD), F32), pltpu.VMEM((nch, HD, HD), F32), pltpu.VMEM((nch, HD, HD), F32),
                   pltpu.VMEM((nch, HD), F32)]
    elif kind == 2:
        scratch = [pltpu.VMEM((16, TOK), F32)]
    elif kind == 3:
        outs.append((jax.ShapeDtypeStruct((nt * 8, TOK), F32), pl.BlockSpec((8, TOK), lambda i: (i, 0))))
        scratch = [pltpu.VMEM((8, TOK), F32), pltpu.VMEM((1, TOK), F32)]
    nout = len(outs)
    nscr = len(scratch)
    nride = len(ride.arrays) if ride else 0

    def body(*refs):
        x_ref, g_ref, b_ref, wt_ref, wout_ref, ks_ref, vs_ref = refs[:7]
        prm_refs = refs[7:7 + nprm]
        nin = 7 + nprm + nride
        ride_src = refs[7 + nprm:nin]
        out_refs = refs[nin:nin + nout]
        ride_dst = refs[nin + nout:nin + nout + nride]
        scr = refs[nin + nout + nride:nin + nout + nride + nscr]
        ride_sems = refs[nin + nout + nride + nscr:]
        proj_ref, xhat_ref, rstd_ref = out_refs[:3]
        i = pl.program_id(0)
        if ride:
            @pl.when(i == 0)
            def _():
                ride.start(ride_src, ride_dst, ride_sems)

        xin = x_ref[...] * g_ref[...] + b_ref[...]
        proj = _nt(xin, wt_ref[...])
        proj_ref[...] = proj
        if kind == 0:
            tok, _ = _gmlp_fwd(proj[:, :TOK], proj[:, TOK:2 * TOK], prm_refs[0], prm_refs[1])
        elif kind == 1:
            st_ref, sts_s, ut_s, dch_s = scr

            @pl.when(i == 0)
            def _():
                st_ref[...] = jnp.zeros_like(st_ref)

            out_refs[4][...] = st_ref[...]
            lb, _ = _lower_bound(prm_refs[0][...], layer)
            tok = _hgrn_fwd(proj[:, :TOK], proj[:, TOK:2 * TOK], proj[:, 2 * TOK:3 * TOK], lb, prm_refs[1][...],
                            st_ref, sts_s, ut_s, dch_s)
        elif kind == 2:
            halo_ref, = scr

            @pl.when(i == 0)
            def _():
                halo_ref[...] = jnp.zeros_like(halo_ref)

            p = proj[:, :TOK]
            tok, _ = _pool_fwd(p, halo_ref[...], i * TS, prm_refs[0][...], prm_refs[1][...])
            halo_ref[...] = p[TS - 16:]
        else:
            halo_ref, h_ref = scr

            @pl.when(i == 0)
            def _():
                halo_ref[...] = jnp.zeros_like(halo_ref)
                h_ref[...] = jnp.zeros_like(h_ref)

            out_refs[4][...] = jnp.broadcast_to(h_ref[...], (8, TOK))
            xb = proj[:, :TOK]
            tok, _ = _lru_fwd(xb, halo_ref[...], i * TS, prm_refs, h_ref[...])
            halo_ref[...] = xb[TS - 8:]
            h_ref[...] = _row(tok, TS - 1)
        qx = proj[:, N - D - XW:N - D]
        gate = proj[:, N - D:]
        xo, ps = _xattn_fwd(qx, ks_ref, vs_ref)
        out_refs[3][...] = jnp.concatenate(ps, axis=1).astype(MXU)
        mixed = jnp.concatenate([tok, xo], axis=1) * (gate * _sigmoid(gate))
        z = ALPHA * xin + _nn(mixed, wout_ref[...])
        cen = z - jnp.mean(z, axis=-1, keepdims=True)
        rstd = lax.rsqrt(jnp.mean(cen * cen, axis=-1, keepdims=True) + LN_EPS)
        xhat_ref[...] = cen * rstd
        rstd_ref[...] = rstd
        if ride:
            @pl.when(i == nt - 1)
            def _():
                ride.wait(ride_src, ride_dst, ride_sems)

    ins = [(xprev, _rows(D, nt, False)), (gprev, _res(gprev)), (bprev, _res(bprev)), (wt, _res(wt)), (wout, _res(wout)),
           (ks, _res(ks)), (vs, _res(vs))] + [(p, _res(p)) for p in prm]
    if ride:
        ins += [(a, _ANY) for a in ride.arrays]
        outs += [(s, _ANY) for s in ride.out_shapes]
        scratch = scratch + ride.scratch
    return _call(body, f"layer{layer}_fwd", (nt,), ins, outs, scratch)


def _layer_bwd(kind, layer, up, is_last, xhat, rstd, probs, g_i, b_i, proj, wout, ks, vs, prm, extra, ride=None):
    TS = TS_BWD[kind]
    _rows = functools.partial(_row_spec, ts=TS)
    S = xhat.shape[0]
    nt = S // TS
    N = proj.shape[1]
    nprm = len(prm)
    nch = TS // CH

    ins = [(up, _rows(D, nt, True)), (xhat, _rows(D, nt, True)), (rstd, _rows(1, nt, True)), (g_i, _res(g_i)), (b_i, _res(b_i)),
           (proj, _rows(N, nt, True)), (wout, _res(wout)), (ks, _res(ks)), (vs, _res(vs)),
           (probs, _rows(XHEADS * NMEM, nt, True))] + [(p, _res(p)) for p in prm]
    nfixed = 10
    if kind == 1:
        ins.append((extra, pl.BlockSpec((None, NH, HD, HD), lambda i: (nt - 1 - i, 0, 0, 0))))
    elif kind == 2:
        hb = TS // 16
        ins.append((proj, pl.BlockSpec((16, TOK), lambda i: (jnp.maximum((nt - 1 - i) * hb - 1, 0), 0))))
    elif kind == 3:
        hb = TS // 8
        ins.append((proj, pl.BlockSpec((8, TOK), lambda i: (jnp.maximum((nt - 1 - i) * hb - 1, 0), 0))))
        ins.append((extra, pl.BlockSpec((8, TOK), lambda i: (nt - 1 - i, 0))))
    nin = len(ins)

    def acc(shape):
        return (jax.ShapeDtypeStruct(shape, F32), _res_sds(shape))

    outs = [(jax.ShapeDtypeStruct((S, D), F32), _rows(D, nt, True)),
            (jax.ShapeDtypeStruct((S, N), MXU), _rows(N, nt, True)),
            (jax.ShapeDtypeStruct((D, D), WIRE), _res_sds((D, D))),
            acc((XHEADS, NMEM, XW)), acc((XHEADS, NMEM, XW)), acc((1, D)), acc((1, D)), acc((1, HD))]
    scratch = []
    if kind == 0:
        outs += [acc((NH, HD, HD)), acc((NH, HD, HD))]
    elif kind == 1:
        outs += [acc((1, TOK)), acc((1, TOK))]
        scratch = [pltpu.VMEM((NH, HD, HD), F32), pltpu.VMEM((nch, HD, HD), F32), pltpu.VMEM((nch, HD, HD), F32),
                   pltpu.VMEM((nch, HD), F32), pltpu.VMEM((nch, HD, HD), F32)]
    elif kind == 2:
        outs += [acc((TOK, TOK)), acc((1, TOK))]
        scratch = [pltpu.VMEM((16, TOK), F32)]
    else:
        outs += [acc((4, TOK)), acc((1, TOK)), acc((NH, HD, HD)), acc((1, TOK)), acc((NH, HD, HD)), acc((1, TOK)), acc((1, TOK))]
        scratch = [pltpu.VMEM((1, TOK), F32), pltpu.VMEM((1, TOK), F32), pltpu.VMEM((8, TOK), F32)]
    scratch = scratch + [pltpu.VMEM((D, D), F32)]
    nout = len(outs)
    nscr = len(scratch)
    nride = len(ride.arrays) if ride else 0

    def body(*refs):
        up_ref, xhat_ref, rstd_ref, g_ref, b_ref, proj_ref, wout_ref, ks_ref, vs_ref, probs_ref = refs[:nfixed]
        prm_refs = refs[nfixed:nfixed + nprm]
        ext_refs = refs[nfixed + nprm:nin]
        ride_src = refs[nin:nin + nride]
        o0 = nin + nride
        out_refs = refs[o0:o0 + nout]
        ride_dst = refs[o0 + nout:o0 + nout + nride]
        scr = refs[o0 + nout + nride:o0 + nout + nride + nscr - 1]
        dwout_acc = refs[o0 + nout + nride + nscr - 1]
        ride_sems = refs[o0 + nout + nride + nscr:]
        dres_ref, dproj_ref, dwout_ref, dks_ref, dvs_ref, dg_ref, db_ref, loss_ref = out_refs[:8]
        pgrad = out_refs[8:]
        i = pl.program_id(0)
        tile = nt - 1 - i

        @pl.when(i == 0)
        def _():
            if ride:
                ride.start(ride_src, ride_dst, ride_sems)
            for r in out_refs[3:]:
                r[...] = jnp.zeros_like(r)
            dwout_acc[...] = jnp.zeros_like(dwout_acc)
            for r in scr:
                if kind != 1 or r is scr[0]:
                    r[...] = jnp.zeros_like(r)

        xhat_v = xhat_ref[...]
        if is_last:
            err = xhat_v * g_ref[...] + b_ref[...] - up_ref[...]
            dxo = err * (1.0 / D)
            loss_ref[...] += jnp.sum(0.5 * jnp.mean(err * err, axis=-1, keepdims=True), axis=0, keepdims=True)
        else:
            dxo = up_ref[...]
        _acc(dg_ref, _rowsum(dxo * xhat_v))
        _acc(db_ref, _rowsum(dxo))
        dxh = dxo * g_ref[...]
        dz = rstd_ref[...] * (dxh - jnp.mean(dxh, axis=-1, keepdims=True)
                              - xhat_v * jnp.mean(dxh * xhat_v, axis=-1, keepdims=True))
        dres_ref[...] = ALPHA * dz

        proj = proj_ref[...]
        qx = proj[:, N - D - XW:N - D]
        gate = proj[:, N - D:]
        sgate = _sigmoid(gate)
        silu = gate * sgate
        dmixed = _nt(dz, wout_ref[...])
        dcat = dmixed * silu
        dtok = dcat[:, :TOK]
        if kind == 0:
            u, v = proj[:, :TOK], proj[:, TOK:2 * TOK]
            tok, fres = _gmlp_fwd(u, v, prm_refs[0], prm_refs[1])
            du, dv = _gmlp_bwd(u, v, fres, dtok, pgrad[0], pgrad[1])
            dproj_ref[:, :TOK] = du.astype(MXU)
            dproj_ref[:, TOK:2 * TOK] = dv.astype(MXU)
        elif kind == 1:
            dst_ref, sts_s, ut_s, dch_s, dstn_s = scr
            lb, _ = _lower_bound(prm_refs[0][...], layer)
            q, fl, inp = proj[:, :TOK], proj[:, TOK:2 * TOK], proj[:, 2 * TOK:3 * TOK]
            tok, dq, dfl, dinp = _hgrn_bwd(q, fl, inp, lb, prm_refs[1][...], dtok, ext_refs[0], dst_ref, sts_s, ut_s, dch_s,
                                           dstn_s, pgrad[1], pgrad[0])
            dproj_ref[:, :TOK] = dq.astype(MXU)
            dproj_ref[:, TOK:2 * TOK] = dfl.astype(MXU)
            dproj_ref[:, 2 * TOK:3 * TOK] = dinp.astype(MXU)
        elif kind == 2:
            p = proj[:, :TOK]
            halo = jnp.where(tile == 0, 0.0, ext_refs[0][...])
            tok, fres = _pool_fwd(p, halo, tile * TS, prm_refs[0][...], prm_refs[1][...])
            dp = _pool_bwd(fres, dtok, scr[0], prm_refs[0][...], prm_refs[1][...], pgrad[0], pgrad[1])
            dproj_ref[:, :TOK] = dp.astype(MXU)
        else:
            xb = proj[:, :TOK]
            halo = jnp.where(tile == 0, 0.0, ext_refs[0][...])
            h0 = ext_refs[1][0:1]
            tok, fres = _lru_fwd(xb, halo, tile * TS, prm_refs, h0)
            dxb = _lru_bwd(fres, tok, h0, dtok, prm_refs, scr, pgrad)
            dproj_ref[:, :TOK] = dxb.astype(MXU)
        ps = [probs_ref[:, h * NMEM:(h + 1) * NMEM].astype(F32) for h in range(XHEADS)]
        xo = _nn(ps[0], vs_ref[0])
        for h in range(1, XHEADS):
            xo = xo + _nn(ps[h], vs_ref[h])
        dqx = _xattn_bwd(qx, ps, dcat[:, TOK:], ks_ref, vs_ref, dks_ref, dvs_ref)
        cat = jnp.concatenate([tok, xo], axis=1)
        dwout_acc[...] += _tn(cat * silu, dz)
        dgate = dmixed * cat * (sgate * (1.0 + gate * (1.0 - sgate)))
        dproj_ref[:, N - D - XW:N - D] = dqx.astype(MXU)
        dproj_ref[:, N - D:] = dgate.astype(MXU)

        @pl.when(i == nt - 1)
        def _():
            dwout_ref[...] = dwout_acc[...].astype(WIRE)
            if ride:
                ride.wait(ride_src, ride_dst, ride_sems)

    if ride:
        ins += [(a, _ANY) for a in ride.arrays]
        outs += [(s, _ANY) for s in ride.out_shapes]
        scratch = scratch + ride.scratch
    return _call(body, f"layer{layer}_bwd", (nt,), ins, outs, scratch)


def _proj_bwd(layer, dproj, dres, xprev, gprev, bprev, wt, ride=None):
    S = xprev.shape[0]
    nt = S // TSB
    N = wt.shape[0]

    nride = len(ride.arrays) if ride else 0

    def body(*refs):
        dproj_ref, dres_ref, x_ref, g_ref, b_ref, wt_ref = refs[:6]
        ride_src = refs[6:6 + nride]
        dx_ref, dwt_ref = refs[6 + nride:8 + nride]
        ride_dst = refs[8 + nride:8 + 2 * nride]
        acc_ref = refs[8 + 2 * nride]
        ride_sems = refs[9 + 2 * nride:]

        @pl.when(pl.program_id(0) == 0)
        def _():
            if ride:
                ride.start(ride_src, ride_dst, ride_sems)
            acc_ref[...] = jnp.zeros_like(acc_ref)

        dp = dproj_ref[...]
        xin = x_ref[...] * g_ref[...] + b_ref[...]
        dx_ref[...] = dres_ref[...] + _nn(dp, wt_ref[...])
        acc_ref[...] += _tn(dp, xin)

        @pl.when(pl.program_id(0) == nt - 1)
        def _():
            dwt_ref[...] = acc_ref[...].astype(WIRE)
            if ride:
                ride.wait(ride_src, ride_dst, ride_sems)

    ins = [(dproj, _row_spec(N, nt, False, TSB)), (dres, _row_spec(D, nt, False, TSB)), (xprev, _row_spec(D, nt, False, TSB)),
           (gprev, _res(gprev)), (bprev, _res(bprev)), (wt, _res(wt))]
    outs = [(jax.ShapeDtypeStruct((S, D), F32), _row_spec(D, nt, False, TSB)),
            (jax.ShapeDtypeStruct((N, D), WIRE), _res_sds((N, D)))]
    scratch = [pltpu.VMEM((N, D), F32)]
    if ride:
        ins += [(a, _ANY) for a in ride.arrays]
        outs += [(s, _ANY) for s in ride.out_shapes]
        scratch = scratch + ride.scratch
    return _call(body, f"layer{layer}_projbwd", (nt,), ins, outs, scratch)


def _head_mask(h):
    col = lax.broadcasted_iota(jnp.int32, (1, XW), 1)
    return (col // 64) == h


def _kv_fwd(mem, wkv):
    def body(mem_ref, w_ref, ks_ref, vs_ref):
        kv = _nn(mem_ref[...], w_ref[...])
        k, v = kv[:, :XW], kv[:, XW:]
        for h in range(XHEADS):
            ks_ref[h] = jnp.where(_head_mask(h), k, 0.0).astype(MXU)
            vs_ref[h] = jnp.where(_head_mask(h), v, 0.0).astype(MXU)

    sds = jax.ShapeDtypeStruct((XHEADS, NMEM, XW), MXU)
    return pl.pallas_call(body, name="kv_fwd", out_shape=(sds, sds), compiler_params=_cparams())(mem, wkv)


def _kv_bwd(mem, dks_l, dvs_l):
    def body(mem_ref, *refs):
        dks_refs, dvs_refs, out_ref = refs[:DEPTH], refs[DEPTH:2 * DEPTH], refs[2 * DEPTH]
        dk = jnp.zeros((NMEM, XW), F32)
        dv = jnp.zeros((NMEM, XW), F32)
        for h in range(XHEADS):
            m = _head_mask(h)
            for l in range(DEPTH):
                dk = dk + jnp.where(m, dks_refs[l][h], 0.0)
                dv = dv + jnp.where(m, dvs_refs[l][h], 0.0)
        out_ref[...] = _tn(mem_ref[...], jnp.concatenate([dk, dv], axis=1)).astype(WIRE)

    return pl.pallas_call(body, name="kv_bwd", out_shape=jax.ShapeDtypeStruct((D, 2 * XW), WIRE),
                          compiler_params=_cparams())(mem, *dks_l, *dvs_l)


def _prep_weights(w_ins, w_out, wkv):
    def body(a_ref, b_ref, c_ref, d_ref, wo_ref, kv_ref, ao, bo, co, do, wo0, wo1, wo2, wo3, kvo):
        for src, dst in ((a_ref, ao), (b_ref, bo), (c_ref, co), (d_ref, do)):
            dst[...] = src[...].T.astype(MXU)
        for l, dst in enumerate((wo0, wo1, wo2, wo3)):
            dst[...] = wo_ref[l].astype(MXU)
        kvo[...] = kv_ref[...].astype(MXU)

    outs = [jax.ShapeDtypeStruct((w.shape[1], w.shape[0]), MXU) for w in w_ins]
    outs += [jax.ShapeDtypeStruct(w_out.shape[1:], MXU)] * DEPTH + [jax.ShapeDtypeStruct(wkv.shape, MXU)]
    return pl.pallas_call(body, name="prep_weights", out_shape=outs, compiler_params=_cparams())(*w_ins, w_out, wkv)


def _adam_math(w, g, m, v):
    m = B1 * m + (1.0 - B1) * g
    v = B2 * v + (1.0 - B2) * (g * g)
    m_hat = m / (1.0 - B1 ** STEP)
    v_hat = v / (1.0 - B2 ** STEP)
    delta = -LR * (m_hat / (jnp.sqrt(v_hat) + EPS) + WD * w)
    return delta, m, v


def _sum_adam(name, recv, w, m, v, transpose):
    rows, cols = recv.shape[1], recv.shape[2]

    def body(r_ref, w_ref, m_ref, v_ref, g_out, d_out, m_out, v_out, acc_ref):
        s = pl.program_id(0)

        @pl.when(s == 0)
        def _():
            acc_ref[...] = r_ref[...].astype(F32)

        @pl.when(s > 0)
        def _():
            acc_ref[...] += r_ref[...].astype(F32)

        @pl.when(s == NDEV - 1)
        def _():
            g = acc_ref[...].T if transpose else acc_ref[...]
            d, mn, vn = _adam_math(w_ref[...], g, m_ref[...], v_ref[...])
            g_out[...] = g
            d_out[...] = d
            m_out[...] = mn
            v_out[...] = vn

    sds = jax.ShapeDtypeStruct(w.shape, F32)
    ins = [(recv, pl.BlockSpec((None, rows, cols), lambda s: (s, 0, 0))), (w, _res(w)), (m, _res(m)), (v, _res(v))]
    outs = [(sds, _res_sds(w.shape))] * 4
    return _call(body, name, (NDEV,), ins, outs, [pltpu.VMEM((rows, cols), F32)])


def _bias_finalize(dbs_exp):
    def body(dbs_ref, dabs_ref):
        dabs_ref[...] = jnp.sum(dbs_ref[...], axis=-1)

    return pl.pallas_call(body, name="bias_finalize", out_shape=jax.ShapeDtypeStruct((NH, HD), F32),
                          compiler_params=_cparams())(dbs_exp)


def _lb_finalize(dlb, lb_logits):
    def body(dlb_ref, lg_ref, dlg_ref):
        total = jnp.zeros((DEPTH, TOK), F32)
        lg = lg_ref[...]
        e = jnp.exp(lg - jnp.max(lg, axis=0, keepdims=True))
        p = e / jnp.sum(e, axis=0, keepdims=True)
        row = lax.broadcasted_iota(jnp.int32, (DEPTH, TOK), 0)
        for layer in range(DEPTH):
            if layer % 4 != 1:
                continue
            dp = jnp.where((row >= 1) & (row <= layer), dlb_ref[...], 0.0)
            total = total + p * (dp - jnp.sum(p * dp, axis=0, keepdims=True))
        dlg_ref[...] = total

    return pl.pallas_call(body, name="lb_finalize", out_shape=jax.ShapeDtypeStruct((DEPTH, TOK), F32),
                          compiler_params=_cparams())(dlb, lb_logits)


def _small_sum_adam(name, gathered, w, m, v):
    rows = w.shape[0]

    def body(r_ref, w_ref, m_ref, v_ref, g_out, d_out, m_out, v_out):
        g = r_ref[0]
        for s in range(1, NDEV):
            g = g + r_ref[s]
        d, mn, vn = _adam_math(w_ref[...], g, m_ref[...], v_ref[...])
        g_out[...] = g
        d_out[...] = d
        m_out[...] = mn
        v_out[...] = vn

    sds = jax.ShapeDtypeStruct((rows, 128), F32)
    return pl.pallas_call(body, name=name, out_shape=(sds,) * 4, compiler_params=_cparams())(gathered, w, m, v)


def _adam_only(g, w, m, v):
    def body(g_ref, w_ref, m_ref, v_ref, d_out, m_out, v_out):
        d, mn, vn = _adam_math(w_ref[...], g_ref[...], m_ref[...], v_ref[...])
        d_out[...] = d
        m_out[...] = mn
        v_out[...] = vn

    sds = jax.ShapeDtypeStruct(w.shape, F32)
    return pl.pallas_call(body, name="shard_adam", out_shape=(sds,) * 3, compiler_params=_cparams())(g, w, m, v)


def _me_and_peers():
    x, y, c = lax.axis_index("x"), lax.axis_index("y"), lax.axis_index("c")
    me = 4 * x + 2 * y + c
    peers = []
    for k in range(1, NDEV):
        kx, ky, kc = (k >> 2) & 1, (k >> 1) & 1, k & 1
        px = x + kx - 2 * x * kx
        py = y + ky - 2 * y * ky
        pc = c + kc - 2 * c * kc
        peers.append(((px, py, pc), 4 * px + 2 * py + pc))
    return me, peers


_ANY = pl.BlockSpec(memory_space=pl.ANY)


class _Exchange:
    def __init__(self, arrays, split):
        self.arrays = list(arrays)
        self.split = list(split)
        n = len(self.arrays)
        self.out_shapes = []
        for a, sp in zip(self.arrays, self.split):
            rows = a.shape[0] // NDEV if sp else a.shape[0]
            self.out_shapes.append(jax.ShapeDtypeStruct((NDEV, rows, a.shape[1]), a.dtype))
        self.scratch = [pltpu.SemaphoreType.DMA((n, NDEV - 1)), pltpu.SemaphoreType.DMA((n, NDEV - 1)),
                        pltpu.SemaphoreType.DMA((n,))]

    def _block(self, src, t, d):
        if not self.split[t]:
            return src[t]
        rows = self.arrays[t].shape[0] // NDEV
        return src[t].at[pl.ds(d * rows, rows)]

    def start(self, src, dst, sems):
        send_sems, recv_sems, local_sems = sems
        me, peers = _me_and_peers()
        for t in range(len(self.arrays)):
            pltpu.make_async_copy(self._block(src, t, me), dst[t].at[me], local_sems.at[t]).start()
        for k, (dev, idx) in enumerate(peers):
            for t in range(len(self.arrays)):
                pltpu.make_async_remote_copy(src_ref=self._block(src, t, idx), dst_ref=dst[t].at[me],
                                             send_sem=send_sems.at[t, k], recv_sem=recv_sems.at[t, k],
                                             device_id=dev, device_id_type=pl.DeviceIdType.MESH).start()

    def wait(self, src, dst, sems):
        send_sems, recv_sems, local_sems = sems
        me, peers = _me_and_peers()

        def slot_copy(t, k, dev, idx):
            return pltpu.make_async_remote_copy(src_ref=dst[t].at[idx], dst_ref=dst[t].at[idx], send_sem=send_sems.at[t, k],
                                                recv_sem=recv_sems.at[t, k], device_id=dev,
                                                device_id_type=pl.DeviceIdType.MESH)

        for k, (dev, idx) in enumerate(peers):
            for t in range(len(self.arrays)):
                slot_copy(t, k, dev, idx).wait_recv()
        for k, (dev, idx) in enumerate(peers):
            for t in range(len(self.arrays)):
                slot_copy(t, k, dev, idx).wait_send()
        for t in range(len(self.arrays)):
            pltpu.make_async_copy(dst[t].at[me], dst[t].at[me], local_sems.at[t]).wait()

    def run(self, name):
        n = len(self.arrays)

        def body(*refs):
            src, dst, sems = refs[:n], refs[n:2 * n], refs[2 * n:]
            self.start(src, dst, sems)
            self.wait(src, dst, sems)

        return pl.pallas_call(
            body, name=name, out_shape=self.out_shapes, in_specs=[_ANY] * n, out_specs=[_ANY] * n,
            scratch_shapes=self.scratch,
        )(*self.arrays)


SMALL = [("ln_g", (DEPTH, D), False), ("ln_b", (DEPTH, D), False), ("hgrn_lb_logits", (DEPTH, TOK), False),
         ("a_w_s", (1, NH, HD, HD), False), ("a_b_s", (1, NH, HD), False), ("b_norm_g", (1, TOK), True),
         ("c_w_pool", (1, 4, POOL_GROUP, POOL_GROUP), False), ("c_scale", (1, TOK), True),
         ("d_conv_w", (1, 4, TOK), True), ("d_conv_b", (1, TOK), True),
         ("d_w_gx", (1, NH, HD, HD), False), ("d_b_gx", (1, NH, HD), False),
         ("d_w_ga", (1, NH, HD, HD), False), ("d_b_ga", (1, NH, HD), False), ("d_a_param", (1, TOK), True)]


def _pack(parts, total_rows):
    flat = jnp.concatenate([p.reshape(-1).astype(F32) for p in parts])
    flat = jnp.pad(flat, (0, total_rows * 128 - flat.shape[0]))
    return flat.reshape(total_rows, 128)


def _size(shape):
    n = 1
    for s in shape:
        n *= s
    return n


def _rows_for(n):
    return -(-n // 1024) * 8


def kernel(x, mem, mem_kv_w, ln_g, ln_b, w_out, hgrn_lb_logits, a_w_in, a_w_s, a_b_s, b_w_in, b_norm_g, c_w_in, c_w_pool, c_scale, d_w_in, d_conv_w, d_conv_b, d_w_gx, d_b_gx, d_w_ga, d_b_ga, d_a_param, loss_target, m_mem_kv_w, m_ln_g, m_ln_b, m_w_out, m_hgrn_lb_logits, m_a_w_in, m_a_w_s, m_a_b_s, m_b_w_in, m_b_norm_g, m_c_w_in, m_c_w_pool, m_c_scale, m_d_w_in, m_d_conv_w, m_d_conv_b, m_d_w_gx, m_d_b_gx, m_d_w_ga, m_d_b_ga, m_d_a_param, v_mem_kv_w, v_ln_g, v_ln_b, v_w_out, v_hgrn_lb_logits, v_a_w_in, v_a_w_s, v_a_b_s, v_b_w_in, v_b_norm_g, v_c_w_in, v_c_w_pool, v_c_scale, v_d_w_in, v_d_conv_w, v_d_conv_b, v_d_w_gx, v_d_b_gx, v_d_w_ga, v_d_b_ga, v_d_a_param):
    W = dict(mem_kv_w=mem_kv_w, ln_g=ln_g, ln_b=ln_b, w_out=w_out, hgrn_lb_logits=hgrn_lb_logits, a_w_in=a_w_in, a_w_s=a_w_s,
             a_b_s=a_b_s, b_w_in=b_w_in, b_norm_g=b_norm_g, c_w_in=c_w_in, c_w_pool=c_w_pool, c_scale=c_scale, d_w_in=d_w_in,
             d_conv_w=d_conv_w, d_conv_b=d_conv_b, d_w_gx=d_w_gx, d_b_gx=d_b_gx, d_w_ga=d_w_ga, d_b_ga=d_b_ga, d_a_param=d_a_param)
    M = dict(mem_kv_w=m_mem_kv_w, ln_g=m_ln_g, ln_b=m_ln_b, w_out=m_w_out, hgrn_lb_logits=m_hgrn_lb_logits, a_w_in=m_a_w_in,
             a_w_s=m_a_w_s, a_b_s=m_a_b_s, b_w_in=m_b_w_in, b_norm_g=m_b_norm_g, c_w_in=m_c_w_in, c_w_pool=m_c_w_pool,
             c_scale=m_c_scale, d_w_in=m_d_w_in, d_conv_w=m_d_conv_w, d_conv_b=m_d_conv_b, d_w_gx=m_d_w_gx, d_b_gx=m_d_b_gx,
             d_w_ga=m_d_w_ga, d_b_ga=m_d_b_ga, d_a_param=m_d_a_param)
    V = dict(mem_kv_w=v_mem_kv_w, ln_g=v_ln_g, ln_b=v_ln_b, w_out=v_w_out, hgrn_lb_logits=v_hgrn_lb_logits, a_w_in=v_a_w_in,
             a_w_s=v_a_w_s, a_b_s=v_a_b_s, b_w_in=v_b_w_in, b_norm_g=v_b_norm_g, c_w_in=v_c_w_in, c_w_pool=v_c_w_pool,
             c_scale=v_c_scale, d_w_in=v_d_w_in, d_conv_w=v_d_conv_w, d_conv_b=v_d_conv_b, d_w_gx=v_d_w_gx, d_b_gx=v_d_b_gx,
             d_w_ga=v_d_w_ga, d_b_ga=v_d_b_ga, d_a_param=v_d_a_param)
    me = 4 * lax.axis_index("x") + 2 * lax.axis_index("y") + lax.axis_index("c")
    x2, mem2, tgt2 = x[0], mem[0], loss_target[0]
    in_names = ["a_w_in", "b_w_in", "c_w_in", "d_w_in"]

    shard_names = [n for n, _, sh in SMALL if sh]
    small_shard = _pack([W[n] for n in shard_names], 8)
    wts = _prep_weights([W[n][0] for n in in_names], w_out, mem_kv_w)
    wt_sh, wo_sh, wkv_sh = wts[:4], wts[4:8], wts[8]
    g0 = _Exchange([wt_sh[0], wo_sh[0], wkv_sh, small_shard], [False] * 4).run("gather_first")
    wt_full = [g0[0].reshape(-1, D)]
    wout_full = [g0[1].reshape(D, D)]
    wkv_full = g0[2].reshape(D, 2 * XW)
    sm = g0[3].reshape(NDEV, 1024)
    full_small = {}
    off = 0
    for n, shape, _ in [s for s in SMALL if s[2]]:
        per = _size(shape) // NDEV
        blk = sm[:, off:off + per]
        if n == "d_conv_w":
            full_small[n] = blk.reshape(NDEV, 4, TOK // NDEV).transpose(1, 0, 2).reshape(4, TOK)
        else:
            full_small[n] = blk.reshape(1, TOK)
        off += per

    ks, vs = _kv_fwd(mem2, wkv_full)
    tri_bs = jnp.broadcast_to(a_b_s[0][:, :, None], (NH, HD, HD))
    wbd = jnp.zeros((TOK, TOK), F32)
    for g in range(4):
        wbd = lax.dynamic_update_slice(wbd, c_w_pool[0, g], (g * POOL_GROUP, g * POOL_GROUP))
    wbd = wbd.astype(MXU)
    prm = {0: [a_w_s[0], tri_bs],
           1: [hgrn_lb_logits, full_small["b_norm_g"]],
           2: [wbd, full_small["c_scale"]],
           3: [full_small["d_conv_w"], full_small["d_conv_b"], d_w_gx[0].astype(MXU), d_b_gx[0].reshape(1, TOK),
               d_w_ga[0].astype(MXU), d_b_ga[0].reshape(1, TOK), full_small["d_a_param"]]}
    ones = jnp.ones((1, D), F32)
    zeros = jnp.zeros((1, D), F32)
    xs, gs, bs = [x2], [ones], [zeros]
    saved = []
    for i in range(DEPTH):
        ride = _Exchange([wt_sh[i + 1], wo_sh[i + 1]], [False, False]) if i + 1 < DEPTH else None
        res = _layer_fwd(i, i, xs[i], gs[i], bs[i], wt_full[i], wout_full[i], ks, vs, prm[i], ride)
        if ride:
            wt_full.append(res[-2].reshape(-1, D))
            wout_full.append(res[-1].reshape(D, D))
            res = res[:-2]
        saved.append(res)
        xs.append(res[1])
        gs.append(ln_g[i:i + 1])
        bs.append(ln_b[i:i + 1])

    up = tgt2
    grads = {}
    dks_l, dvs_l, dwt_l, dwout_l, dlng_l, dlnb_l = [], [], [], [], [], []
    recv_wt, recv_wo = [None] * DEPTH, [None] * DEPTH
    loss_part = None
    small_shape = {n: s for n, s, _ in SMALL}
    small_shape.update({f"ln_g#{l}": (1, D) for l in range(DEPTH)})
    small_shape.update({f"ln_b#{l}": (1, D) for l in range(DEPTH)})
    small_shape["loss"] = (128,)
    sharded = {n for n, _, sh in SMALL if sh}
    group = {3: ["ln_g#3", "ln_b#3", "d_conv_w", "d_conv_b", "d_w_gx", "d_b_gx", "d_w_ga", "d_b_ga", "d_a_param"],
             2: ["ln_g#2", "ln_b#2", "c_w_pool", "c_scale"],
             1: ["ln_g#1", "ln_b#1", "hgrn_lb_logits", "b_norm_g"],
             0: ["ln_g#0", "ln_b#0", "a_w_s", "a_b_s", "loss"]}
    group_rows = {l: _rows_for(sum(_size(small_shape[e]) for e in group[l])) for l in range(DEPTH)}
    recv_small = [None] * DEPTH
    for i in reversed(range(DEPTH)):
        res = saved[i]
        extra = res[4] if len(res) > 4 else None
        ride = None
        if i + 1 < DEPTH:
            small_vec = _pack([grads[e] for e in group[i + 1]], group_rows[i + 1])
            ride = _Exchange([dwt_l[-1], dwout_l[-1], small_vec], [True, True, False])
        out = _layer_bwd(i, i, up, i == DEPTH - 1, res[1], res[2], res[3], gs[i + 1], bs[i + 1], res[0], wout_full[i], ks, vs,
                         prm[i], extra, ride)
        if ride:
            recv_wt[i + 1], recv_wo[i + 1], recv_small[i + 1] = out[-3], out[-2], out[-1]
            out = out[:-3]
        dres, dproj, dwout_i, dks_i, dvs_i, dg_i, db_i, loss_i = out[:8]
        pg = out[8:]
        if i == DEPTH - 1:
            grads["loss"] = loss_i[0]
        dks_l.append(dks_i)
        dvs_l.append(dvs_i)
        dwout_l.append(dwout_i)
        grads[f"ln_g#{i}"], grads[f"ln_b#{i}"] = dg_i, db_i
        if i == 0:
            grads["a_w_s"], dbs_exp = pg
            grads["a_b_s"] = _bias_finalize(dbs_exp)
        elif i == 1:
            dlb, grads["b_norm_g"] = pg
            grads["hgrn_lb_logits"] = _lb_finalize(dlb, hgrn_lb_logits)
        elif i == 2:
            dwbd, grads["c_scale"] = pg
            grads["c_w_pool"] = jnp.stack([lax.dynamic_slice(dwbd, (g * POOL_GROUP, g * POOL_GROUP), (POOL_GROUP, POOL_GROUP))
                                           for g in range(4)])
        else:
            (grads["d_conv_w"], grads["d_conv_b"], grads["d_w_gx"], grads["d_b_gx"], grads["d_w_ga"], grads["d_b_ga"],
             grads["d_a_param"]) = pg
        ride = None
        if i == 0:
            dwkv = _kv_bwd(mem2, dks_l, dvs_l)
            small_vec = _pack([grads[e] for e in group[0]], group_rows[0])
            ride = _Exchange([dwout_i, dwkv, small_vec], [True, True, False])
        pb = _proj_bwd(i, dproj, dres, xs[i], gs[i], bs[i], wt_full[i], ride)
        up, dwt = pb[:2]
        if ride:
            recv_wo[0], recv_kv, recv_small[0] = pb[2:]
        dwt_l.append(dwt)
    grad_x = up[None]

    recv_wt[0], = _Exchange([dwt_l[-1]], [True]).run("scatter_last")

    outs = {}
    for t, n in enumerate(in_names):
        g, d, mn, vn = _sum_adam(f"adam_{n}", recv_wt[t], W[n][0], M[n][0], V[n][0], True)
        outs[n] = (g[None], d[None], mn[None], vn[None])
    wo_res = [_sum_adam(f"adam_w_out{l}", recv_wo[l], w_out[l], m_w_out[l], v_w_out[l], False) for l in range(DEPTH)]
    outs["w_out"] = tuple(jnp.stack([wo_res[l][j] for l in range(DEPTH)]) for j in range(4))
    outs["mem_kv_w"] = _sum_adam("adam_mem_kv_w", recv_kv, mem_kv_w, m_mem_kv_w, v_mem_kv_w, False)

    def entry_of(tree, e):
        if "#" in e:
            n, l = e.split("#")
            return tree[n][int(l):int(l) + 1]
        if e == "loss" or e in sharded:
            return jnp.zeros(small_shape[e], F32)
        return tree[e]

    small = [{}, {}, {}, {}]
    for l in range(DEPTH):
        packed = [_pack([entry_of(t, e) for e in group[l]], group_rows[l]) for t in (W, M, V)]
        res = _small_sum_adam(f"small_adam{l}", recv_small[l], *packed)
        for j in range(4):
            flat, o = res[j].reshape(-1), 0
            for e in group[l]:
                small[j][e] = flat[o:o + _size(small_shape[e])].reshape(small_shape[e])
                o += _size(small_shape[e])
    loss = small[0]["loss"][0]
    for j in range(4):
        for n in ("ln_g", "ln_b"):
            small[j][n] = jnp.concatenate([small[j][f"{n}#{l}"] for l in range(DEPTH)], axis=0)
    g_small = small[0]
    for n, _, sh in SMALL:
        if not sh:
            outs[n] = tuple(small[j][n] for j in range(4))
    per = TOK // NDEV
    g_sh = {n: lax.dynamic_slice_in_dim(g_small[n], me * per, per, axis=len(s) - 1) for n, s, sh in SMALL if sh}
    gp = _pack([g_sh[n] for n in shard_names], 8)
    d_p, m_p, v_p = _adam_only(gp, small_shard, _pack([M[n] for n in shard_names], 8), _pack([V[n] for n in shard_names], 8))
    o = 0
    for n in shard_names:
        cnt = _size(W[n].shape)
        outs[n] = (g_sh[n],) + tuple(t.reshape(-1)[o:o + cnt].reshape(W[n].shape) for t in (d_p, m_p, v_p))
        o += cnt

    order = ["mem_kv_w", "ln_g", "ln_b", "w_out", "hgrn_lb_logits", "a_w_in", "a_w_s", "a_b_s", "b_w_in", "b_norm_g", "c_w_in",
             "c_w_pool", "c_scale", "d_w_in", "d_conv_w", "d_conv_b", "d_w_gx", "d_b_gx", "d_w_ga", "d_b_ga", "d_a_param"]
    result = [loss, grad_x]
    for j in range(4):
        result += [outs[n][j].reshape(W[n].shape) for n in order]
    return tuple(result)
```

```python
import functools

import jax
import jax.numpy as jnp
from jax import lax
from jax.experimental import pallas as pl
from jax.experimental.pallas import tpu as pltpu

F32 = jnp.float32
MXU = jnp.bfloat16
WIRE = jnp.bfloat16

D = 1024
TOK = 768
XW = 256
NMEM = 256
XHEADS = 4
XSCALE = 64 ** -0.5
NH = 6
HD = 128
CH = 16
POOL_WINDOWS = (2, 4, 8, 16)
POOL_GROUP = 192
DEPTH = 4
ALPHA = (2 * DEPTH) ** 0.25
LN_EPS = 1e-5
RMS_EPS = 1e-6
LRU_C = 8.0
B1, B2, LR, EPS, WD, STEP = 0.9, 0.999, 0.001, 1e-8, 0.01, 10

NDEV = 8
TS_FWD = {0: 512, 1: 256, 2: 512, 3: 256}
TS_BWD = {0: 256, 1: 256, 2: 256, 3: 256}
TSB = 512
VMEM_LIMIT = 58 * 1024 * 1024

KIND_WIDTHS = {0: 2 * TOK + XW + D, 1: 3 * TOK + XW + D, 2: TOK + XW + D, 3: TOK + XW + D}


def _mm(a, b, ca, cb):
    return lax.dot_general(a.astype(MXU), b.astype(MXU), (((ca,), (cb,)), ((), ())), preferred_element_type=F32)


def _nn(a, b):
    return _mm(a, b, 1, 0)


def _nt(a, b):
    return _mm(a, b, 1, 1)


def _tn(a, b):
    return _mm(a, b, 0, 0)


def _bmm(a, b, ca, cb):
    return lax.dot_general(a.astype(MXU), b.astype(MXU), (((ca,), (cb,)), ((0,), (0,))), preferred_element_type=F32)


def _sigmoid(x):
    return 1.0 / (1.0 + jnp.exp(-x))


def _vjp1(fn, x, dy):
    return jax.vjp(fn, x)[1](dy)[0]


def _rowsum(x):
    return jnp.sum(x, axis=0, keepdims=True)


def _row(x, r):
    sel = lax.broadcasted_iota(jnp.int32, x.shape, 0) == r
    return jnp.sum(jnp.where(sel, x, 0.0), axis=0, keepdims=True)


def _acc(ref, val):
    ref[...] += val


def _cparams(sem=None):
    return pltpu.CompilerParams(dimension_semantics=sem, vmem_limit_bytes=VMEM_LIMIT)


def _res(a):
    nd = a.ndim
    return pl.BlockSpec(a.shape, lambda i: (0,) * nd)


def _res_sds(shape):
    nd = len(shape)
    return pl.BlockSpec(shape, lambda i: (0,) * nd)


def _row_spec(width, nt, rev, ts):
    if rev:
        return pl.BlockSpec((ts, width), lambda i: (nt - 1 - i, 0))
    return pl.BlockSpec((ts, width), lambda i: (i, 0))


def _call(body, name, grid, ins, outs, scratch=(), sem=("arbitrary",)):
    arrays = [a for a, _ in ins]
    return pl.pallas_call(
        body, name=name, grid=grid,
        in_specs=[s for _, s in ins],
        out_specs=[s for _, s in outs],
        out_shape=[o for o, _ in outs],
        scratch_shapes=list(scratch),
        compiler_params=_cparams(sem),
    )(*arrays)


def _xattn_fwd(qx, ks_ref, vs_ref):
    o = None
    ps = []
    for h in range(XHEADS):
        s = _nt(qx, ks_ref[h]) * XSCALE
        s = s - jnp.max(s, axis=-1, keepdims=True)
        e = jnp.exp(s)
        p = e / jnp.sum(e, axis=-1, keepdims=True)
        ps.append(p)
        oh = _nn(p, vs_ref[h])
        o = oh if o is None else o + oh
    return o, ps


def _xattn_bwd(qx, ps, dxo, ks_ref, vs_ref, dks_ref, dvs_ref):
    dq = None
    for h in range(XHEADS):
        p = ps[h]
        dp = _nt(dxo, vs_ref[h])
        ds = p * (dp - jnp.sum(dp * p, axis=-1, keepdims=True))
        dqh = _nn(ds, ks_ref[h]) * XSCALE
        dq = dqh if dq is None else dq + dqh
        dks_ref[h] += _tn(ds, qx) * XSCALE
        dvs_ref[h] += _tn(p, dxo)
    return dq


def _tril128():
    r = lax.broadcasted_iota(jnp.int32, (HD, HD), 0)
    c = lax.broadcasted_iota(jnp.int32, (HD, HD), 1)
    return c <= r


def _gmlp_fwd(u, v, ws_ref, bs_ref):
    ts = u.shape[0]
    ug = jax.nn.gelu(u)
    vg = jax.nn.gelu(v)
    tri = _tril128()
    toks, res = [], []
    for g in range(NH):
        sl = slice(g * HD, (g + 1) * HD)
        vgh = vg[:, sl]
        cen = vgh - jnp.mean(vgh, axis=-1, keepdims=True)
        rstd = lax.rsqrt(jnp.mean(cen * cen, axis=-1, keepdims=True) + LN_EPS)
        vn = cen * rstd
        w = jnp.where(tri, ws_ref[g], 0.0).astype(MXU)
        mix = jnp.concatenate([_nn(w, vn[n * HD:(n + 1) * HD]) + bs_ref[g] for n in range(ts // HD)], axis=0)
        toks.append(ug[:, sl] * mix)
        res.append((vn, rstd, mix, w))
    return jnp.concatenate(toks, axis=1), (ug, res)


def _gmlp_bwd(u, v, fres, dtok, dws_ref, dbs_ref):
    ts = u.shape[0]
    ug, res = fres
    tri = _tril128()
    dugs, dvgs = [], []
    for g in range(NH):
        sl = slice(g * HD, (g + 1) * HD)
        vn, rstd, mix, w = res[g]
        dmix = dtok[:, sl] * ug[:, sl]
        dugs.append(dtok[:, sl] * mix)
        dvn_rows = []
        dw = None
        dbs = None
        for n in range(ts // HD):
            dm = dmix[n * HD:(n + 1) * HD]
            dvn_rows.append(_tn(w, dm))
            t = _nt(dm, vn[n * HD:(n + 1) * HD])
            dw = t if dw is None else dw + t
            dbs = dm if dbs is None else dbs + dm
        dws_ref[g] += jnp.where(tri, dw, 0.0)
        dbs_ref[g] += dbs
        dvn = jnp.concatenate(dvn_rows, axis=0)
        dvgs.append(rstd * (dvn - jnp.mean(dvn, axis=-1, keepdims=True) - vn * jnp.mean(dvn * vn, axis=-1, keepdims=True)))
    du = _vjp1(jax.nn.gelu, u, jnp.concatenate(dugs, axis=1))
    dv = _vjp1(jax.nn.gelu, v, jnp.concatenate(dvgs, axis=1))
    return du, dv


def _chunk_cumsum(x):
    row = lax.broadcasted_iota(jnp.int32, x.shape, 0) % CH
    for s in (1, 2, 4, 8):
        x = x + jnp.where(row >= s, pltpu.roll(x, s, 0), 0.0)
    return x


def _chunk_revcumsum(x):
    n = x.shape[0]
    row = lax.broadcasted_iota(jnp.int32, x.shape, 0) % CH
    for s in (1, 2, 4, 8):
        x = x + jnp.where(row < CH - s, pltpu.roll(x, n - s, 0), 0.0)
    return x


def _chunk_sum(x):
    n, w = x.shape
    return jnp.sum(x.reshape(n // CH, CH, w), axis=1)


def _chunk_bcast(c, n):
    nch, w = c.shape
    return jnp.broadcast_to(c[:, None, :], (nch, CH, w)).reshape(n, w)


def _lower_bound(lb_logits, layer):
    lg = lb_logits
    e = jnp.exp(lg - jnp.max(lg, axis=0, keepdims=True))
    p = e / jnp.sum(e, axis=0, keepdims=True)
    row = lax.broadcasted_iota(jnp.int32, p.shape, 0)
    lb = jnp.sum(jnp.where((row >= 1) & (row <= layer), p, 0.0), axis=0, keepdims=True)
    return lb, p


def _hgrn_prep(q, fl, lb):
    n = q.shape[0]
    sg = _sigmoid(fl)
    f = lb + (1.0 - lb) * sg
    lf = jnp.log(f)
    k = 1.0 - f
    sq = _sigmoid(q)
    qf = q * sq
    g = _chunk_cumsum(lf)
    tot = _chunk_sum(lf)
    gl = _chunk_bcast(tot, n)
    eg = jnp.exp(g)
    eng = jnp.exp(-g)
    egl = jnp.exp(gl - g)
    return dict(sg=sg, f=f, k=k, sq=sq, qf=qf, eg=eg, eng=eng, egl=egl,
                qd=qf * eg, ki=k * eng, ke=k * egl, dch=jnp.exp(tot))


def _hgrn_mask():
    r = lax.broadcasted_iota(jnp.int32, (HD, HD), 0)
    c = lax.broadcasted_iota(jnp.int32, (HD, HD), 1)
    return (r // CH == c // CH) & (c <= r)


def _hgrn_states(v3, ke3, dch_h, st_in):
    nch = v3.shape[0]
    ut = _bmm(v3, ke3, 1, 1)
    dfull = jnp.broadcast_to(dch_h[:, None, :], (nch, HD, HD))
    st, sts = st_in, []
    for c in range(nch):
        sts.append(st)
        st = st * dfull[c] + ut[c]
    return jnp.stack(sts), st, dfull


def _hgrn_fwd(q, fl, inp, lb, ng, st_ref):
    n = q.shape[0]
    nch = n // CH
    pr = _hgrn_prep(q, fl, lb)
    mask = _hgrn_mask()
    toks = []
    for h in range(NH):
        sl = slice(h * HD, (h + 1) * HD)
        qd, ki, ke, v = pr["qd"][:, sl], pr["ki"][:, sl], pr["ke"][:, sl], inp[:, sl]
        qd3 = qd.astype(MXU).reshape(nch, CH, HD)
        v3 = v.astype(MXU).reshape(nch, CH, HD)
        ke3 = ke.astype(MXU).reshape(nch, CH, HD)
        sts, st_ref[h], _ = _hgrn_states(v3, ke3, pr["dch"][:, sl], st_ref[h])
        o = _bmm(qd3, sts, 2, 2).reshape(n, HD)
        intra = []
        for b in range(n // HD):
            bs = slice(b * HD, (b + 1) * HD)
            a = jnp.where(mask, _nt(qd[bs], ki[bs]), 0.0)
            intra.append(_nn(a, v[bs]))
        o = o + jnp.concatenate(intra, axis=0)
        r = lax.rsqrt(jnp.mean(o * o, axis=-1, keepdims=True) + RMS_EPS)
        toks.append(o * r * ng[:, sl])
    return jnp.concatenate(toks, axis=1)


def _hgrn_bwd(q, fl, inp, lb, ng, dtok, ststart_ref, dst_ref, dng_ref, dlb_ref):
    n = q.shape[0]
    nch = n // CH
    pr = _hgrn_prep(q, fl, lb)
    mask = _hgrn_mask()
    dqd_l, dki_l, dke_l, dv_l, ddch_l, dng_l, toks = [], [], [], [], [], [], []
    for h in range(NH):
        sl = slice(h * HD, (h + 1) * HD)
        qd, ki, ke, v = pr["qd"][:, sl], pr["ki"][:, sl], pr["ke"][:, sl], inp[:, sl]
        qd3 = qd.astype(MXU).reshape(nch, CH, HD)
        v3 = v.astype(MXU).reshape(nch, CH, HD)
        ke3 = ke.astype(MXU).reshape(nch, CH, HD)
        sts, _, dfull = _hgrn_states(v3, ke3, pr["dch"][:, sl], ststart_ref[h])
        o = _bmm(qd3, sts, 2, 2).reshape(n, HD)
        a_l = []
        intra = []
        for b in range(n // HD):
            bs = slice(b * HD, (b + 1) * HD)
            a = jnp.where(mask, _nt(qd[bs], ki[bs]), 0.0)
            a_l.append(a)
            intra.append(_nn(a, v[bs]))
        o = o + jnp.concatenate(intra, axis=0)
        r = lax.rsqrt(jnp.mean(o * o, axis=-1, keepdims=True) + RMS_EPS)
        toks.append(o * r * ng[:, sl])
        dt = dtok[:, sl]
        dng_l.append(_rowsum(dt * o * r))
        dn = dt * ng[:, sl]
        do = r * dn - o * (r * r * r) * jnp.mean(dn * o, axis=-1, keepdims=True)
        do3 = do.astype(MXU).reshape(nch, CH, HD)
        dqd_rows, dki_rows, dv_rows = [], [], []
        for b in range(n // HD):
            bs = slice(b * HD, (b + 1) * HD)
            da = jnp.where(mask, _nt(do[bs], v[bs]), 0.0)
            dqd_rows.append(_nn(da, ki[bs]))
            dki_rows.append(_tn(da, qd[bs]))
            dv_rows.append(_tn(a_l[b], do[bs]))
        dqd = jnp.concatenate(dqd_rows, axis=0) + _bmm(do3, sts, 2, 1).reshape(n, HD)
        dki = jnp.concatenate(dki_rows, axis=0)
        dv = jnp.concatenate(dv_rows, axis=0)
        wt = _bmm(do3, qd3, 1, 1)
        dst, dstn_l = dst_ref[h], [None] * nch
        for c in reversed(range(nch)):
            dstn_l[c] = dst
            dst = wt[c] + dst * dfull[c]
        dst_ref[h] = dst
        dstn = jnp.stack(dstn_l)
        dv = dv + _bmm(ke3, dstn, 2, 2).reshape(n, HD)
        dke = _bmm(v3, dstn, 2, 1).reshape(n, HD)
        ddch_l.append(jnp.sum(sts * dstn, axis=1))
        dqd_l.append(dqd)
        dki_l.append(dki)
        dke_l.append(dke)
        dv_l.append(dv)
    dqd = jnp.concatenate(dqd_l, axis=1)
    dki = jnp.concatenate(dki_l, axis=1)
    dke = jnp.concatenate(dke_l, axis=1)
    dinp = jnp.concatenate(dv_l, axis=1)
    ddch = jnp.concatenate(ddch_l, axis=1)
    _acc(dng_ref, jnp.concatenate(dng_l, axis=1))
    dqf = dqd * pr["eg"]
    dke_ke = dke * pr["ke"]
    dg = dqd * pr["qd"] - dki * pr["ki"] - dke_ke
    dk = dki * pr["eng"] + dke * pr["egl"]
    dgl = _chunk_sum(dke_ke) + ddch * pr["dch"]
    dlf = _chunk_revcumsum(dg) + _chunk_bcast(dgl, n)
    df = dlf / pr["f"] - dk
    sg = pr["sg"]
    dfl = df * (1.0 - lb) * sg * (1.0 - sg)
    _acc(dlb_ref, _rowsum(df * (1.0 - sg)))
    sq = pr["sq"]
    dq = dqf * (sq * (1.0 + q * (1.0 - sq)))
    return jnp.concatenate(toks, axis=1), dq, dfl, dinp


def _pool_select(s2, s4, s8, s16):
    col = lax.broadcasted_iota(jnp.int32, (1, TOK), 1)
    return jnp.where(col < POOL_GROUP, s2, jnp.where(col < 2 * POOL_GROUP, s4, jnp.where(col < 3 * POOL_GROUP, s8, s16)))


def _pool_cnt(pos0, n):
    pos = pos0 + lax.broadcasted_iota(jnp.int32, (n, TOK), 0) + 1
    col = lax.broadcasted_iota(jnp.int32, (n, TOK), 1)
    w = jnp.where(col < POOL_GROUP, 2, jnp.where(col < 2 * POOL_GROUP, 4, jnp.where(col < 3 * POOL_GROUP, 8, 16)))
    return jnp.minimum(pos, w).astype(F32)


def _pool_fwd(p, halo, pos0, wbd, scale):
    n = p.shape[0]
    ext = jnp.concatenate([halo, p], axis=0)
    s2 = ext + pltpu.roll(ext, 1, 0)
    s4 = s2 + pltpu.roll(s2, 2, 0)
    s8 = s4 + pltpu.roll(s4, 4, 0)
    s16 = s8 + pltpu.roll(s8, 8, 0)
    win = _pool_select(s2, s4, s8, s16)[16:]
    cnt = _pool_cnt(pos0, n)
    diff = win / cnt - p
    y = _nn(diff, wbd)
    return y * scale, (diff, y, cnt)


def _pool_bwd(fres, dtok, nxt_ref, wbd, scale, dwbd_ref, dscale_ref):
    diff, y, cnt = fres
    n = diff.shape[0]
    _acc(dscale_ref, _rowsum(dtok * y))
    dy = dtok * scale
    ddiff = _nt(dy, wbd)
    dwbd_ref[...] += _tn(diff, dy)
    qv = ddiff / cnt
    ext = jnp.concatenate([qv, nxt_ref[...]], axis=0)
    m = n + 16
    s2 = ext + pltpu.roll(ext, m - 1, 0)
    s4 = s2 + pltpu.roll(s2, m - 2, 0)
    s8 = s4 + pltpu.roll(s4, m - 4, 0)
    s16 = s8 + pltpu.roll(s8, m - 8, 0)
    adj = _pool_select(s2, s4, s8, s16)[:n]
    nxt_ref[...] = qv[:16]
    return adj - ddiff


def _neg_expm1(x):
    return jnp.where(jnp.abs(x) < 1e-2, -x * (1.0 + x * (0.5 + x * (1.0 / 6.0))), 1.0 - jnp.exp(x))


def _lru_gates(xc, zx, za, ap, first):
    gx = _sigmoid(zx)
    ga = _sigmoid(za)
    sp = jnp.maximum(-ap, 0.0) + jnp.log(1.0 + jnp.exp(-jnp.abs(ap)))
    log_a = -LRU_C * ga * sp
    a = jnp.exp(log_a)
    mult = jnp.sqrt(_neg_expm1(2.0 * log_a))
    mult = jnp.where(first, 1.0, mult)
    return a, mult * gx * xc


def _scan_fwd(a, b, h0):
    n = a.shape[0]
    row = lax.broadcasted_iota(jnp.int32, a.shape, 0)
    s = 1
    while s < n:
        keep = row >= s
        b = b + a * jnp.where(keep, pltpu.roll(b, s, 0), 0.0)
        a = a * jnp.where(keep, pltpu.roll(a, s, 0), 1.0)
        s *= 2
    return b + a * h0


def _scan_bwd(an, d, dh_next):
    n = an.shape[0]
    row = lax.broadcasted_iota(jnp.int32, an.shape, 0)
    s = 1
    while s < n:
        keep = row < n - s
        d = d + an * jnp.where(keep, pltpu.roll(d, n - s, 0), 0.0)
        an = an * jnp.where(keep, pltpu.roll(an, n - s, 0), 1.0)
        s *= 2
    return d + an * dh_next


def _lru_conv(xb, halo, cw_ref, cb):
    ext = jnp.concatenate([halo, xb], axis=0)
    sh = [pltpu.roll(ext, 3 - j, 0)[8:] if j < 3 else xb for j in range(4)]
    xc = cb
    for j in range(4):
        xc = xc + cw_ref[pl.ds(j, 1), :] * sh[j]
    return xc, sh


def _lru_fwd(xb, halo, pos0, prm, h0):
    cw, cb, wgx, bgx, wga, bga, ap = prm
    n = xb.shape[0]
    xc, sh = _lru_conv(xb, halo, cw, cb[...])
    zx = jnp.concatenate([_nn(xc[:, h * HD:(h + 1) * HD], wgx[h]) for h in range(NH)], axis=1) + bgx[...]
    za = jnp.concatenate([_nn(xc[:, h * HD:(h + 1) * HD], wga[h]) for h in range(NH)], axis=1) + bga[...]
    first = (pos0 + lax.broadcasted_iota(jnp.int32, (n, 1), 0)) == 0
    a, b = _lru_gates(xc, zx, za, ap[...], first)
    hseq = _scan_fwd(a, b, h0)
    return hseq, (xc, sh, zx, za, first, a)


def _lru_bwd(fres, hseq, h0, dtok, prm, carry_refs, grad_refs):
    cw, cb, wgx, bgx, wga, bga, ap = prm
    xc, sh, zx, za, first, a = fres
    anext_ref, dhnext_ref, dxcnext_ref = carry_refs
    dcw_ref, dcb_ref, dwgx_ref, dbgx_ref, dwga_ref, dbga_ref, dap_ref = grad_refs
    n = xc.shape[0]
    an = jnp.where(lax.broadcasted_iota(jnp.int32, a.shape, 0) == n - 1, anext_ref[...], pltpu.roll(a, n - 1, 0))
    dh = _scan_bwd(an, dtok, dhnext_ref[...])
    hprev = jnp.where(lax.broadcasted_iota(jnp.int32, hseq.shape, 0) == 0, h0, pltpu.roll(hseq, 1, 0))
    da = dh * hprev
    anext_ref[...] = _row(a, 0)
    dhnext_ref[...] = _row(dh, 0)
    _, vjp = jax.vjp(lambda xc_, zx_, za_, ap_: _lru_gates(xc_, zx_, za_, ap_, first), xc, zx, za, ap[...])
    dxc, dzx, dza, dap = vjp((da, dh))
    _acc(dap_ref, dap)
    _acc(dbgx_ref, _rowsum(dzx))
    _acc(dbga_ref, _rowsum(dza))
    parts = []
    for h in range(NH):
        sl = slice(h * HD, (h + 1) * HD)
        parts.append(_nt(dzx[:, sl], wgx[h]) + _nt(dza[:, sl], wga[h]))
        dwgx_ref[h] += _tn(xc[:, sl], dzx[:, sl])
        dwga_ref[h] += _tn(xc[:, sl], dza[:, sl])
    dxc = dxc + jnp.concatenate(parts, axis=1)
    _acc(dcb_ref, _rowsum(dxc))
    for j in range(4):
        dcw_ref[pl.ds(j, 1), :] += _rowsum(dxc * sh[j])
    ext = jnp.concatenate([dxc, dxcnext_ref[...]], axis=0)
    m = n + 8
    dxb = cw[pl.ds(3, 1), :] * dxc
    for j in range(3):
        dxb = dxb + cw[pl.ds(j, 1), :] * pltpu.roll(ext, m - (3 - j), 0)[:n]
    dxcnext_ref[...] = dxc[:8]
    return dxb


def _layer_fwd(kind, layer, xprev, gprev, bprev, wt, wout, ks, vs, prm, ride=None):
    TS = TS_FWD[kind]
    _rows = functools.partial(_row_spec, ts=TS)
    S = xprev.shape[0]
    nt = S // TS
    N = wt.shape[0]
    nprm = len(prm)
    nch = TS // CH

    outs = [(jax.ShapeDtypeStruct((S, N), F32), _rows(N, nt, False)),
            (jax.ShapeDtypeStruct((S, D), F32), _rows(D, nt, False)),
            (jax.ShapeDtypeStruct((S, 1), F32), _rows(1, nt, False)),
            (jax.ShapeDtypeStruct((S, XHEADS * NMEM), MXU), _rows(XHEADS * NMEM, nt, False))]
    scratch = []
    if kind == 1:
        outs.append((jax.ShapeDtypeStruct((nt, NH, HD, HD), F32), pl.BlockSpec((None, NH, HD, HD), lambda i: (i, 0, 0, 0))))
        scratch = [pltpu.VMEM((NH, HD, HD), F32)]
    elif kind == 2:
        scratch = [pltpu.VMEM((16, TOK), F32)]
    elif kind == 3:
        outs.append((jax.ShapeDtypeStruct((nt * 8, TOK), F32), pl.BlockSpec((8, TOK), lambda i: (i, 0))))
        scratch = [pltpu.VMEM((8, TOK), F32), pltpu.VMEM((1, TOK), F32)]
    nout = len(outs)
    nscr = len(scratch)
    nride = len(ride.arrays) if ride else 0

    def body(*refs):
        x_ref, g_ref, b_ref, wt_ref, wout_ref, ks_ref, vs_ref = refs[:7]
        prm_refs = refs[7:7 + nprm]
        nin = 7 + nprm + nride
        ride_src = refs[7 + nprm:nin]
        out_refs = refs[nin:nin + nout]
        ride_dst = refs[nin + nout:nin + nout + nride]
        scr = refs[nin + nout + nride:nin + nout + nride + nscr]
        ride_sems = refs[nin + nout + nride + nscr:]
        proj_ref, xhat_ref, rstd_ref = out_refs[:3]
        i = pl.program_id(0)
        if ride:
            @pl.when(i == 0)
            def _():
                ride.start(ride_src, ride_dst, ride_sems)

        xin = x_ref[...] * g_ref[...] + b_ref[...]
        proj = _nt(xin, wt_ref[...])
        proj_ref[...] = proj
        if kind == 0:
            tok, _ = _gmlp_fwd(proj[:, :TOK], proj[:, TOK:2 * TOK], prm_refs[0], prm_refs[1])
        elif kind == 1:
            st_ref, = scr

            @pl.when(i == 0)
            def _():
                st_ref[...] = jnp.zeros_like(st_ref)

            out_refs[4][...] = st_ref[...]
            lb, _ = _lower_bound(prm_refs[0][...], layer)
            tok = _hgrn_fwd(proj[:, :TOK], proj[:, TOK:2 * TOK], proj[:, 2 * TOK:3 * TOK], lb, prm_refs[1][...],
                            st_ref)
        elif kind == 2:
            halo_ref, = scr

            @pl.when(i == 0)
            def _():
                halo_ref[...] = jnp.zeros_like(halo_ref)

            p = proj[:, :TOK]
            tok, _ = _pool_fwd(p, halo_ref[...], i * TS, prm_refs[0][...], prm_refs[1][...])
            halo_ref[...] = p[TS - 16:]
        else:
            halo_ref, h_ref = scr

            @pl.when(i == 0)
            def _():
                halo_ref[...] = jnp.zeros_like(halo_ref)
                h_ref[...] = jnp.zeros_like(h_ref)

            out_refs[4][...] = jnp.broadcast_to(h_ref[...], (8, TOK))
            xb = proj[:, :TOK]
            tok, _ = _lru_fwd(xb, halo_ref[...], i * TS, prm_refs, h_ref[...])
            halo_ref[...] = xb[TS - 8:]
            h_ref[...] = _row(tok, TS - 1)
        qx = proj[:, N - D - XW:N - D]
        gate = proj[:, N - D:]
        xo, ps = _xattn_fwd(qx, ks_ref, vs_ref)
        out_refs[3][...] = jnp.concatenate(ps, axis=1).astype(MXU)
        mixed = jnp.concatenate([tok, xo], axis=1) * (gate * _sigmoid(gate))
        z = ALPHA * xin + _nn(mixed, wout_ref[...])
        cen = z - jnp.mean(z, axis=-1, keepdims=True)
        rstd = lax.rsqrt(jnp.mean(cen * cen, axis=-1, keepdims=True) + LN_EPS)
        xhat_ref[...] = cen * rstd
        rstd_ref[...] = rstd
        if ride:
            @pl.when(i == nt - 1)
            def _():
                ride.wait(ride_src, ride_dst, ride_sems)

    ins = [(xprev, _rows(D, nt, False)), (gprev, _res(gprev)), (bprev, _res(bprev)), (wt, _res(wt)), (wout, _res(wout)),
           (ks, _res(ks)), (vs, _res(vs))] + [(p, _res(p)) for p in prm]
    if ride:
        ins += [(a, _ANY) for a in ride.arrays]
        outs += [(s, _ANY) for s in ride.out_shapes]
        scratch = scratch + ride.scratch
    return _call(body, f"layer{layer}_fwd", (nt,), ins, outs, scratch)


def _layer_bwd(kind, layer, up, is_last, xhat, rstd, probs, g_i, b_i, proj, wout, ks, vs, prm, extra, ride=None):
    TS = TS_BWD[kind]
    _rows = functools.partial(_row_spec, ts=TS)
    S = xhat.shape[0]
    nt = S // TS
    N = proj.shape[1]
    nprm = len(prm)
    nch = TS // CH

    ins = [(up, _rows(D, nt, True)), (xhat, _rows(D, nt, True)), (rstd, _rows(1, nt, True)), (g_i, _res(g_i)), (b_i, _res(b_i)),
           (proj, _rows(N, nt, True)), (wout, _res(wout)), (ks, _res(ks)), (vs, _res(vs)),
           (probs, _rows(XHEADS * NMEM, nt, True))] + [(p, _res(p)) for p in prm]
    nfixed = 10
    if kind == 1:
        ins.append((extra, pl.BlockSpec((None, NH, HD, HD), lambda i: (nt - 1 - i, 0, 0, 0))))
    elif kind == 2:
        hb = TS // 16
        ins.append((proj, pl.BlockSpec((16, TOK), lambda i: (jnp.maximum((nt - 1 - i) * hb - 1, 0), 0))))
    elif kind == 3:
        hb = TS // 8
        ins.append((proj, pl.BlockSpec((8, TOK), lambda i: (jnp.maximum((nt - 1 - i) * hb - 1, 0), 0))))
        ins.append((extra, pl.BlockSpec((8, TOK), lambda i: (nt - 1 - i, 0))))
    nin = len(ins)

    def acc(shape):
        return (jax.ShapeDtypeStruct(shape, F32), _res_sds(shape))

    outs = [(jax.ShapeDtypeStruct((S, D), F32), _rows(D, nt, True)),
            (jax.ShapeDtypeStruct((S, N), MXU), _rows(N, nt, True)),
            (jax.ShapeDtypeStruct((D, D), WIRE), _res_sds((D, D))),
            acc((XHEADS, NMEM, XW)), acc((XHEADS, NMEM, XW)), acc((1, D)), acc((1, D)), acc((1, HD))]
    scratch = []
    if kind == 0:
        outs += [acc((NH, HD, HD)), acc((NH, HD, HD))]
    elif kind == 1:
        outs += [acc((1, TOK)), acc((1, TOK))]
        scratch = [pltpu.VMEM((NH, HD, HD), F32)]
    elif kind == 2:
        outs += [acc((TOK, TOK)), acc((1, TOK))]
        scratch = [pltpu.VMEM((16, TOK), F32)]
    else:
        outs += [acc((4, TOK)), acc((1, TOK)), acc((NH, HD, HD)), acc((1, TOK)), acc((NH, HD, HD)), acc((1, TOK)), acc((1, TOK))]
        scratch = [pltpu.VMEM((1, TOK), F32), pltpu.VMEM((1, TOK), F32), pltpu.VMEM((8, TOK), F32)]
    scratch = scratch + [pltpu.VMEM((D, D), F32)]
    nout = len(outs)
    nscr = len(scratch)
    nride = len(ride.arrays) if ride else 0

    def body(*refs):
        up_ref, xhat_ref, rstd_ref, g_ref, b_ref, proj_ref, wout_ref, ks_ref, vs_ref, probs_ref = refs[:nfixed]
        prm_refs = refs[nfixed:nfixed + nprm]
        ext_refs = refs[nfixed + nprm:nin]
        ride_src = refs[nin:nin + nride]
        o0 = nin + nride
        out_refs = refs[o0:o0 + nout]
        ride_dst = refs[o0 + nout:o0 + nout + nride]
        scr = refs[o0 + nout + nride:o0 + nout + nride + nscr - 1]
        dwout_acc = refs[o0 + nout + nride + nscr - 1]
        ride_sems = refs[o0 + nout + nride + nscr:]
        dres_ref, dproj_ref, dwout_ref, dks_ref, dvs_ref, dg_ref, db_ref, loss_ref = out_refs[:8]
        pgrad = out_refs[8:]
        i = pl.program_id(0)
        tile = nt - 1 - i

        @pl.when(i == 0)
        def _():
            if ride:
                ride.start(ride_src, ride_dst, ride_sems)
            for r in out_refs[3:]:
                r[...] = jnp.zeros_like(r)
            dwout_acc[...] = jnp.zeros_like(dwout_acc)
            for r in scr:
                if kind != 1 or r is scr[0]:
                    r[...] = jnp.zeros_like(r)

        xhat_v = xhat_ref[...]
        if is_last:
            err = xhat_v * g_ref[...] + b_ref[...] - up_ref[...]
            dxo = err * (1.0 / D)
            loss_ref[...] += jnp.sum(0.5 * jnp.mean(err * err, axis=-1, keepdims=True), axis=0, keepdims=True)
        else:
            dxo = up_ref[...]
        _acc(dg_ref, _rowsum(dxo * xhat_v))
        _acc(db_ref, _rowsum(dxo))
        dxh = dxo * g_ref[...]
        dz = rstd_ref[...] * (dxh - jnp.mean(dxh, axis=-1, keepdims=True)
                              - xhat_v * jnp.mean(dxh * xhat_v, axis=-1, keepdims=True))
        dres_ref[...] = ALPHA * dz

        proj = proj_ref[...]
        qx = proj[:, N - D - XW:N - D]
        gate = proj[:, N - D:]
        sgate = _sigmoid(gate)
        silu = gate * sgate
        dmixed = _nt(dz, wout_ref[...])
        dcat = dmixed * silu
        dtok = dcat[:, :TOK]
        if kind == 0:
            u, v = proj[:, :TOK], proj[:, TOK:2 * TOK]
            tok, fres = _gmlp_fwd(u, v, prm_refs[0], prm_refs[1])
            du, dv = _gmlp_bwd(u, v, fres, dtok, pgrad[0], pgrad[1])
            dproj_ref[:, :TOK] = du.astype(MXU)
            dproj_ref[:, TOK:2 * TOK] = dv.astype(MXU)
        elif kind == 1:
            dst_ref, = scr
            lb, _ = _lower_bound(prm_refs[0][...], layer)
            q, fl, inp = proj[:, :TOK], proj[:, TOK:2 * TOK], proj[:, 2 * TOK:3 * TOK]
            tok, dq, dfl, dinp = _hgrn_bwd(q, fl, inp, lb, prm_refs[1][...], dtok, ext_refs[0], dst_ref, pgrad[1], pgrad[0])
            dproj_ref[:, :TOK] = dq.astype(MXU)
            dproj_ref[:, TOK:2 * TOK] = dfl.astype(MXU)
            dproj_ref[:, 2 * TOK:3 * TOK] = dinp.astype(MXU)
        elif kind == 2:
            p = proj[:, :TOK]
            halo = jnp.where(tile == 0, 0.0, ext_refs[0][...])
            tok, fres = _pool_fwd(p, halo, tile * TS, prm_refs[0][...], prm_refs[1][...])
            dp = _pool_bwd(fres, dtok, scr[0], prm_refs[0][...], prm_refs[1][...], pgrad[0], pgrad[1])
            dproj_ref[:, :TOK] = dp.astype(MXU)
        else:
            xb = proj[:, :TOK]
            halo = jnp.where(tile == 0, 0.0, ext_refs[0][...])
            h0 = ext_refs[1][0:1]
            tok, fres = _lru_fwd(xb, halo, tile * TS, prm_refs, h0)
            dxb = _lru_bwd(fres, tok, h0, dtok, prm_refs, scr, pgrad)
            dproj_ref[:, :TOK] = dxb.astype(MXU)
        ps = [probs_ref[:, h * NMEM:(h + 1) * NMEM].astype(F32) for h in range(XHEADS)]
        xo = _nn(ps[0], vs_ref[0])
        for h in range(1, XHEADS):
            xo = xo + _nn(ps[h], vs_ref[h])
        dqx = _xattn_bwd(qx, ps, dcat[:, TOK:], ks_ref, vs_ref, dks_ref, dvs_ref)
        cat = jnp.concatenate([tok, xo], axis=1)
        dwout_acc[...] += _tn(cat * silu, dz)
        dgate = dmixed * cat * (sgate * (1.0 + gate * (1.0 - sgate)))
        dproj_ref[:, N - D - XW:N - D] = dqx.astype(MXU)
        dproj_ref[:, N - D:] = dgate.astype(MXU)

        @pl.when(i == nt - 1)
        def _():
            dwout_ref[...] = dwout_acc[...].astype(WIRE)
            if ride:
                ride.wait(ride_src, ride_dst, ride_sems)

    if ride:
        ins += [(a, _ANY) for a in ride.arrays]
        outs += [(s, _ANY) for s in ride.out_shapes]
        scratch = scratch + ride.scratch
    return _call(body, f"layer{layer}_bwd", (nt,), ins, outs, scratch)


def _proj_bwd(layer, dproj, dres, xprev, gprev, bprev, wt, ride=None):
    S = xprev.shape[0]
    nt = S // TSB
    N = wt.shape[0]

    nride = len(ride.arrays) if ride else 0

    def body(*refs):
        dproj_ref, dres_ref, x_ref, g_ref, b_ref, wt_ref = refs[:6]
        ride_src = refs[6:6 + nride]
        dx_ref, dwt_ref = refs[6 + nride:8 + nride]
        ride_dst = refs[8 + nride:8 + 2 * nride]
        acc_ref = refs[8 + 2 * nride]
        ride_sems = refs[9 + 2 * nride:]

        @pl.when(pl.program_id(0) == 0)
        def _():
            if ride:
                ride.start(ride_src, ride_dst, ride_sems)
            acc_ref[...] = jnp.zeros_like(acc_ref)

        dp = dproj_ref[...]
        xin = x_ref[...] * g_ref[...] + b_ref[...]
        dx_ref[...] = dres_ref[...] + _nn(dp, wt_ref[...])
        acc_ref[...] += _tn(dp, xin)

        @pl.when(pl.program_id(0) == nt - 1)
        def _():
            dwt_ref[...] = acc_ref[...].astype(WIRE)
            if ride:
                ride.wait(ride_src, ride_dst, ride_sems)

    ins = [(dproj, _row_spec(N, nt, False, TSB)), (dres, _row_spec(D, nt, False, TSB)), (xprev, _row_spec(D, nt, False, TSB)),
           (gprev, _res(gprev)), (bprev, _res(bprev)), (wt, _res(wt))]
    outs = [(jax.ShapeDtypeStruct((S, D), F32), _row_spec(D, nt, False, TSB)),
            (jax.ShapeDtypeStruct((N, D), WIRE), _res_sds((N, D)))]
    scratch = [pltpu.VMEM((N, D), F32)]
    if ride:
        ins += [(a, _ANY) for a in ride.arrays]
        outs += [(s, _ANY) for s in ride.out_shapes]
        scratch = scratch + ride.scratch
    return _call(body, f"layer{layer}_projbwd", (nt,), ins, outs, scratch)


def _head_mask(h):
    col = lax.broadcasted_iota(jnp.int32, (1, XW), 1)
    return (col // 64) == h


def _kv_fwd(mem, wkv):
    def body(mem_ref, w_ref, ks_ref, vs_ref):
        kv = _nn(mem_ref[...], w_ref[...])
        k, v = kv[:, :XW], kv[:, XW:]
        for h in range(XHEADS):
            ks_ref[h] = jnp.where(_head_mask(h), k, 0.0).astype(MXU)
            vs_ref[h] = jnp.where(_head_mask(h), v, 0.0).astype(MXU)

    sds = jax.ShapeDtypeStruct((XHEADS, NMEM, XW), MXU)
    return pl.pallas_call(body, name="kv_fwd", out_shape=(sds, sds), compiler_params=_cparams())(mem, wkv)


def _kv_bwd(mem, dks_l, dvs_l):
    def body(mem_ref, *refs):
        dks_refs, dvs_refs, out_ref = refs[:DEPTH], refs[DEPTH:2 * DEPTH], refs[2 * DEPTH]
        dk = jnp.zeros((NMEM, XW), F32)
        dv = jnp.zeros((NMEM, XW), F32)
        for h in range(XHEADS):
            m = _head_mask(h)
            for l in range(DEPTH):
                dk = dk + jnp.where(m, dks_refs[l][h], 0.0)
                dv = dv + jnp.where(m, dvs_refs[l][h], 0.0)
        out_ref[...] = _tn(mem_ref[...], jnp.concatenate([dk, dv], axis=1)).astype(WIRE)

    return pl.pallas_call(body, name="kv_bwd", out_shape=jax.ShapeDtypeStruct((D, 2 * XW), WIRE),
                          compiler_params=_cparams())(mem, *dks_l, *dvs_l)


def _prep_weights(w_ins, w_out, wkv):
    def body(a_ref, b_ref, c_ref, d_ref, wo_ref, kv_ref, ao, bo, co, do, wo0, wo1, wo2, wo3, kvo):
        for src, dst in ((a_ref, ao), (b_ref, bo), (c_ref, co), (d_ref, do)):
            dst[...] = src[...].T.astype(MXU)
        for l, dst in enumerate((wo0, wo1, wo2, wo3)):
            dst[...] = wo_ref[l].astype(MXU)
        kvo[...] = kv_ref[...].astype(MXU)

    outs = [jax.ShapeDtypeStruct((w.shape[1], w.shape[0]), MXU) for w in w_ins]
    outs += [jax.ShapeDtypeStruct(w_out.shape[1:], MXU)] * DEPTH + [jax.ShapeDtypeStruct(wkv.shape, MXU)]
    return pl.pallas_call(body, name="prep_weights", out_shape=outs, compiler_params=_cparams())(*w_ins, w_out, wkv)


def _adam_math(w, g, m, v):
    m = B1 * m + (1.0 - B1) * g
    v = B2 * v + (1.0 - B2) * (g * g)
    m_hat = m / (1.0 - B1 ** STEP)
    v_hat = v / (1.0 - B2 ** STEP)
    delta = -LR * (m_hat / (jnp.sqrt(v_hat) + EPS) + WD * w)
    return delta, m, v


def _sum_adam(name, recv, w, m, v, transpose):
    rows, cols = recv.shape[1], recv.shape[2]

    def body(r_ref, w_ref, m_ref, v_ref, g_out, d_out, m_out, v_out, acc_ref):
        s = pl.program_id(0)

        @pl.when(s == 0)
        def _():
            acc_ref[...] = r_ref[...].astype(F32)

        @pl.when(s > 0)
        def _():
            acc_ref[...] += r_ref[...].astype(F32)

        @pl.when(s == NDEV - 1)
        def _():
            g = acc_ref[...].T if transpose else acc_ref[...]
            d, mn, vn = _adam_math(w_ref[...], g, m_ref[...], v_ref[...])
            g_out[...] = g
            d_out[...] = d
            m_out[...] = mn
            v_out[...] = vn

    sds = jax.ShapeDtypeStruct(w.shape, F32)
    ins = [(recv, pl.BlockSpec((None, rows, cols), lambda s: (s, 0, 0))), (w, _res(w)), (m, _res(m)), (v, _res(v))]
    outs = [(sds, _res_sds(w.shape))] * 4
    return _call(body, name, (NDEV,), ins, outs, [pltpu.VMEM((rows, cols), F32)])


def _bias_finalize(dbs_exp):
    def body(dbs_ref, dabs_ref):
        dabs_ref[...] = jnp.sum(dbs_ref[...], axis=-1)

    return pl.pallas_call(body, name="bias_finalize", out_shape=jax.ShapeDtypeStruct((NH, HD), F32),
                          compiler_params=_cparams())(dbs_exp)


def _lb_finalize(dlb, lb_logits):
    def body(dlb_ref, lg_ref, dlg_ref):
        total = jnp.zeros((DEPTH, TOK), F32)
        lg = lg_ref[...]
        e = jnp.exp(lg - jnp.max(lg, axis=0, keepdims=True))
        p = e / jnp.sum(e, axis=0, keepdims=True)
        row = lax.broadcasted_iota(jnp.int32, (DEPTH, TOK), 0)
        for layer in range(DEPTH):
            if layer % 4 != 1:
                continue
            dp = jnp.where((row >= 1) & (row <= layer), dlb_ref[...], 0.0)
            total = total + p * (dp - jnp.sum(p * dp, axis=0, keepdims=True))
        dlg_ref[...] = total

    return pl.pallas_call(body, name="lb_finalize", out_shape=jax.ShapeDtypeStruct((DEPTH, TOK), F32),
                          compiler_params=_cparams())(dlb, lb_logits)


def _small_sum_adam(name, gathered, w, m, v):
    rows = w.shape[0]

    def body(r_ref, w_ref, m_ref, v_ref, g_out, d_out, m_out, v_out):
        g = r_ref[0]
        for s in range(1, NDEV):
            g = g + r_ref[s]
        d, mn, vn = _adam_math(w_ref[...], g, m_ref[...], v_ref[...])
        g_out[...] = g
        d_out[...] = d
        m_out[...] = mn
        v_out[...] = vn

    sds = jax.ShapeDtypeStruct((rows, 128), F32)
    return pl.pallas_call(body, name=name, out_shape=(sds,) * 4, compiler_params=_cparams())(gathered, w, m, v)


def _adam_only(g, w, m, v):
    def body(g_ref, w_ref, m_ref, v_ref, d_out, m_out, v_out):
        d, mn, vn = _adam_math(w_ref[...], g_ref[...], m_ref[...], v_ref[...])
        d_out[...] = d
        m_out[...] = mn
        v_out[...] = vn

    sds = jax.ShapeDtypeStruct(w.shape, F32)
    return pl.pallas_call(body, name="shard_adam", out_shape=(sds,) * 3, compiler_params=_cparams())(g, w, m, v)


def _me_and_peers():
    x, y, c = lax.axis_index("x"), lax.axis_index("y"), lax.axis_index("c")
    me = 4 * x + 2 * y + c
    peers = []
    for k in range(1, NDEV):
        kx, ky, kc = (k >> 2) & 1, (k >> 1) & 1, k & 1
        px = x + kx - 2 * x * kx
        py = y + ky - 2 * y * ky
        pc = c + kc - 2 * c * kc
        peers.append(((px, py, pc), 4 * px + 2 * py + pc))
    return me, peers


_ANY = pl.BlockSpec(memory_space=pl.ANY)


class _Exchange:
    def __init__(self, arrays, split):
        self.arrays = list(arrays)
        self.split = list(split)
        n = len(self.arrays)
        self.out_shapes = []
        for a, sp in zip(self.arrays, self.split):
            rows = a.shape[0] // NDEV if sp else a.shape[0]
            self.out_shapes.append(jax.ShapeDtypeStruct((NDEV, rows, a.shape[1]), a.dtype))
        self.scratch = [pltpu.SemaphoreType.DMA((n, NDEV - 1)), pltpu.SemaphoreType.DMA((n, NDEV - 1)),
                        pltpu.SemaphoreType.DMA((n,))]

    def _block(self, src, t, d):
        if not self.split[t]:
            return src[t]
        rows = self.arrays[t].shape[0] // NDEV
        return src[t].at[pl.ds(d * rows, rows)]

    def start(self, src, dst, sems):
        send_sems, recv_sems, local_sems = sems
        me, peers = _me_and_peers()
        for t in range(len(self.arrays)):
            pltpu.make_async_copy(self._block(src, t, me), dst[t].at[me], local_sems.at[t]).start()
        for k, (dev, idx) in enumerate(peers):
            for t in range(len(self.arrays)):
                pltpu.make_async_remote_copy(src_ref=self._block(src, t, idx), dst_ref=dst[t].at[me],
                                             send_sem=send_sems.at[t, k], recv_sem=recv_sems.at[t, k],
                                             device_id=dev, device_id_type=pl.DeviceIdType.MESH).start()

    def wait(self, src, dst, sems):
        send_sems, recv_sems, local_sems = sems
        me, peers = _me_and_peers()

        def slot_copy(t, k, dev, idx):
            return pltpu.make_async_remote_copy(src_ref=dst[t].at[idx], dst_ref=dst[t].at[idx], send_sem=send_sems.at[t, k],
                                                recv_sem=recv_sems.at[t, k], device_id=dev,
                                                device_id_type=pl.DeviceIdType.MESH)

        for k, (dev, idx) in enumerate(peers):
            for t in range(len(self.arrays)):
                slot_copy(t, k, dev, idx).wait_recv()
        for k, (dev, idx) in enumerate(peers):
            for t in range(len(self.arrays)):
                slot_copy(t, k, dev, idx).wait_send()
        for t in range(len(self.arrays)):
            pltpu.make_async_copy(dst[t].at[me], dst[t].at[me], local_sems.at[t]).wait()

    def gather_by_chip(self, src, dst, sems):
        assert not any(self.split)
        send_sems, recv_sems, local_sems = sems
        n = len(self.arrays)
        x, y, c = lax.axis_index("x"), lax.axis_index("y"), lax.axis_index("c")
        me, sibling = 4 * x + 2 * y + c, (x, y, 1 - c)
        chips = [(1 - x, y), (x, 1 - y), (1 - x, 1 - y)]

        def index(chip, core):
            return 4 * chip[0] + 2 * chip[1] + core

        def copy(t, k, block, to, from_src):
            return pltpu.make_async_remote_copy(src_ref=src[t] if from_src else dst[t].at[block], dst_ref=dst[t].at[block],
                                                send_sem=send_sems.at[t, k], recv_sem=recv_sems.at[t, k],
                                                device_id=to, device_id_type=pl.DeviceIdType.MESH)

        local = [pltpu.make_async_copy(src[t], dst[t].at[me], local_sems.at[t]) for t in range(n)]
        for cp in local:
            cp.start()
        sends = []
        for t in range(n):
            sends.append(copy(t, 0, me, sibling, True))
            sends += [copy(t, 1 + j, me, (*chip, c), True) for j, chip in enumerate(chips)]
        for cp in sends:
            cp.start()
        for j, chip in enumerate(chips):
            for t in range(n):
                copy(t, 1 + j, index(chip, c), sibling, False).wait_recv()
                passed = copy(t, 4 + j, index(chip, c), sibling, False)
                passed.start()
                sends.append(passed)
        for t in range(n):
            copy(t, 0, index((x, y), 1 - c), sibling, False).wait_recv()
            for j, chip in enumerate(chips):
                copy(t, 4 + j, index(chip, 1 - c), sibling, False).wait_recv()
        for cp in sends:
            cp.wait_send()
        for cp in local:
            cp.wait()

    def run(self, name, by_chip=False):
        n = len(self.arrays)

        def body(*refs):
            src, dst, sems = refs[:n], refs[n:2 * n], refs[2 * n:]
            if by_chip:
                self.gather_by_chip(src, dst, sems)
                return
            self.start(src, dst, sems)
            self.wait(src, dst, sems)

        return pl.pallas_call(
            body, name=name, out_shape=self.out_shapes, in_specs=[_ANY] * n, out_specs=[_ANY] * n,
            scratch_shapes=self.scratch,
        )(*self.arrays)


SMALL = [("ln_g", (DEPTH, D), False), ("ln_b", (DEPTH, D), False), ("hgrn_lb_logits", (DEPTH, TOK), False),
         ("a_w_s", (1, NH, HD, HD), False), ("a_b_s", (1, NH, HD), False), ("b_norm_g", (1, TOK), True),
         ("c_w_pool", (1, 4, POOL_GROUP, POOL_GROUP), False), ("c_scale", (1, TOK), True),
         ("d_conv_w", (1, 4, TOK), True), ("d_conv_b", (1, TOK), True),
         ("d_w_gx", (1, NH, HD, HD), False), ("d_b_gx", (1, NH, HD), False),
         ("d_w_ga", (1, NH, HD, HD), False), ("d_b_ga", (1, NH, HD), False), ("d_a_param", (1, TOK), True)]


def _pack(parts, total_rows):
    flat = jnp.concatenate([p.reshape(-1).astype(F32) for p in parts])
    flat = jnp.pad(flat, (0, total_rows * 128 - flat.shape[0]))
    return flat.reshape(total_rows, 128)


def _size(shape):
    n = 1
    for s in shape:
        n *= s
    return n


def _rows_for(n):
    return -(-n // 1024) * 8


def kernel(x, mem, mem_kv_w, ln_g, ln_b, w_out, hgrn_lb_logits, a_w_in, a_w_s, a_b_s, b_w_in, b_norm_g, c_w_in, c_w_pool, c_scale, d_w_in, d_conv_w, d_conv_b, d_w_gx, d_b_gx, d_w_ga, d_b_ga, d_a_param, loss_target, m_mem_kv_w, m_ln_g, m_ln_b, m_w_out, m_hgrn_lb_logits, m_a_w_in, m_a_w_s, m_a_b_s, m_b_w_in, m_b_norm_g, m_c_w_in, m_c_w_pool, m_c_scale, m_d_w_in, m_d_conv_w, m_d_conv_b, m_d_w_gx, m_d_b_gx, m_d_w_ga, m_d_b_ga, m_d_a_param, v_mem_kv_w, v_ln_g, v_ln_b, v_w_out, v_hgrn_lb_logits, v_a_w_in, v_a_w_s, v_a_b_s, v_b_w_in, v_b_norm_g, v_c_w_in, v_c_w_pool, v_c_scale, v_d_w_in, v_d_conv_w, v_d_conv_b, v_d_w_gx, v_d_b_gx, v_d_w_ga, v_d_b_ga, v_d_a_param):
    W = dict(mem_kv_w=mem_kv_w, ln_g=ln_g, ln_b=ln_b, w_out=w_out, hgrn_lb_logits=hgrn_lb_logits, a_w_in=a_w_in, a_w_s=a_w_s,
             a_b_s=a_b_s, b_w_in=b_w_in, b_norm_g=b_norm_g, c_w_in=c_w_in, c_w_pool=c_w_pool, c_scale=c_scale, d_w_in=d_w_in,
             d_conv_w=d_conv_w, d_conv_b=d_conv_b, d_w_gx=d_w_gx, d_b_gx=d_b_gx, d_w_ga=d_w_ga, d_b_ga=d_b_ga, d_a_param=d_a_param)
    M = dict(mem_kv_w=m_mem_kv_w, ln_g=m_ln_g, ln_b=m_ln_b, w_out=m_w_out, hgrn_lb_logits=m_hgrn_lb_logits, a_w_in=m_a_w_in,
             a_w_s=m_a_w_s, a_b_s=m_a_b_s, b_w_in=m_b_w_in, b_norm_g=m_b_norm_g, c_w_in=m_c_w_in, c_w_pool=m_c_w_pool,
             c_scale=m_c_scale, d_w_in=m_d_w_in, d_conv_w=m_d_conv_w, d_conv_b=m_d_conv_b, d_w_gx=m_d_w_gx, d_b_gx=m_d_b_gx,
             d_w_ga=m_d_w_ga, d_b_ga=m_d_b_ga, d_a_param=m_d_a_param)
    V = dict(mem_kv_w=v_mem_kv_w, ln_g=v_ln_g, ln_b=v_ln_b, w_out=v_w_out, hgrn_lb_logits=v_hgrn_lb_logits, a_w_in=v_a_w_in,
             a_w_s=v_a_w_s, a_b_s=v_a_b_s, b_w_in=v_b_w_in, b_norm_g=v_b_norm_g, c_w_in=v_c_w_in, c_w_pool=v_c_w_pool,
             c_scale=v_c_scale, d_w_in=v_d_w_in, d_conv_w=v_d_conv_w, d_conv_b=v_d_conv_b, d_w_gx=v_d_w_gx, d_b_gx=v_d_b_gx,
             d_w_ga=v_d_w_ga, d_b_ga=v_d_b_ga, d_a_param=v_d_a_param)
    me = 4 * lax.axis_index("x") + 2 * lax.axis_index("y") + lax.axis_index("c")
    x2, mem2, tgt2 = x[0], mem[0], loss_target[0]
    in_names = ["a_w_in", "b_w_in", "c_w_in", "d_w_in"]

    shard_names = [n for n, _, sh in SMALL if sh]
    small_shard = _pack([W[n] for n in shard_names], 8)
    wts = _prep_weights([W[n][0] for n in in_names], w_out, mem_kv_w)
    wt_sh, wo_sh, wkv_sh = wts[:4], wts[4:8], wts[8]
    g0 = _Exchange([wt_sh[0], wo_sh[0], wkv_sh, small_shard], [False] * 4).run("gather_first", by_chip=True)
    wt_full = [g0[0].reshape(-1, D)]
    wout_full = [g0[1].reshape(D, D)]
    wkv_full = g0[2].reshape(D, 2 * XW)
    sm = g0[3].reshape(NDEV, 1024)
    full_small = {}
    off = 0
    for n, shape, _ in [s for s in SMALL if s[2]]:
        per = _size(shape) // NDEV
        blk = sm[:, off:off + per]
        if n == "d_conv_w":
            full_small[n] = blk.reshape(NDEV, 4, TOK // NDEV).transpose(1, 0, 2).reshape(4, TOK)
        else:
            full_small[n] = blk.reshape(1, TOK)
        off += per

    ks, vs = _kv_fwd(mem2, wkv_full)
    tri_bs = jnp.broadcast_to(a_b_s[0][:, :, None], (NH, HD, HD))
    wbd = jnp.zeros((TOK, TOK), F32)
    for g in range(4):
        wbd = lax.dynamic_update_slice(wbd, c_w_pool[0, g], (g * POOL_GROUP, g * POOL_GROUP))
    wbd = wbd.astype(MXU)
    prm = {0: [a_w_s[0], tri_bs],
           1: [hgrn_lb_logits, full_small["b_norm_g"]],
           2: [wbd, full_small["c_scale"]],
           3: [full_small["d_conv_w"], full_small["d_conv_b"], d_w_gx[0].astype(MXU), d_b_gx[0].reshape(1, TOK),
               d_w_ga[0].astype(MXU), d_b_ga[0].reshape(1, TOK), full_small["d_a_param"]]}
    ones = jnp.ones((1, D), F32)
    zeros = jnp.zeros((1, D), F32)
    xs, gs, bs = [x2], [ones], [zeros]
    saved = []
    for i in range(DEPTH):
        ride = _Exchange([wt_sh[i + 1], wo_sh[i + 1]], [False, False]) if i + 1 < DEPTH else None
        res = _layer_fwd(i, i, xs[i], gs[i], bs[i], wt_full[i], wout_full[i], ks, vs, prm[i], ride)
        if ride:
            wt_full.append(res[-2].reshape(-1, D))
            wout_full.append(res[-1].reshape(D, D))
            res = res[:-2]
        saved.append(res)
        xs.append(res[1])
        gs.append(ln_g[i:i + 1])
        bs.append(ln_b[i:i + 1])

    up = tgt2
    grads = {}
    dks_l, dvs_l, dwt_l, dwout_l, dlng_l, dlnb_l = [], [], [], [], [], []
    recv_wt, recv_wo = [None] * DEPTH, [None] * DEPTH
    loss_part = None
    small_shape = {n: s for n, s, _ in SMALL}
    small_shape.update({f"ln_g#{l}": (1, D) for l in range(DEPTH)})
    small_shape.update({f"ln_b#{l}": (1, D) for l in range(DEPTH)})
    small_shape["loss"] = (128,)
    sharded = {n for n, _, sh in SMALL if sh}
    group = {3: ["ln_g#3", "ln_b#3", "d_conv_w", "d_conv_b", "d_w_gx", "d_b_gx", "d_w_ga", "d_b_ga", "d_a_param"],
             2: ["ln_g#2", "ln_b#2", "c_w_pool", "c_scale"],
             1: ["ln_g#1", "ln_b#1", "hgrn_lb_logits", "b_norm_g"],
             0: ["ln_g#0", "ln_b#0", "a_w_s", "a_b_s", "loss"]}
    group_rows = {l: _rows_for(sum(_size(small_shape[e]) for e in group[l])) for l in range(DEPTH)}
    recv_small = [None] * DEPTH
    for i in reversed(range(DEPTH)):
        res = saved[i]
        extra = res[4] if len(res) > 4 else None
        ride = None
        if i + 1 < DEPTH:
            small_vec = _pack([grads[e] for e in group[i + 1]], group_rows[i + 1])
            ride = _Exchange([dwt_l[-1], dwout_l[-1], small_vec], [True, True, False])
        out = _layer_bwd(i, i, up, i == DEPTH - 1, res[1], res[2], res[3], gs[i + 1], bs[i + 1], res[0], wout_full[i], ks, vs,
                         prm[i], extra, ride)
        if ride:
            recv_wt[i + 1], recv_wo[i + 1], recv_small[i + 1] = out[-3], out[-2], out[-1]
            out = out[:-3]
        dres, dproj, dwout_i, dks_i, dvs_i, dg_i, db_i, loss_i = out[:8]
        pg = out[8:]
        if i == DEPTH - 1:
            grads["loss"] = loss_i[0]
        dks_l.append(dks_i)
        dvs_l.append(dvs_i)
        dwout_l.append(dwout_i)
        grads[f"ln_g#{i}"], grads[f"ln_b#{i}"] = dg_i, db_i
        if i == 0:
            grads["a_w_s"], dbs_exp = pg
            grads["a_b_s"] = _bias_finalize(dbs_exp)
        elif i == 1:
            dlb, grads["b_norm_g"] = pg
            grads["hgrn_lb_logits"] = _lb_finalize(dlb, hgrn_lb_logits)
        elif i == 2:
            dwbd, grads["c_scale"] = pg
            grads["c_w_pool"] = jnp.stack([lax.dynamic_slice(dwbd, (g * POOL_GROUP, g * POOL_GROUP), (POOL_GROUP, POOL_GROUP))
                                           for g in range(4)])
        else:
            (grads["d_conv_w"], grads["d_conv_b"], grads["d_w_gx"], grads["d_b_gx"], grads["d_w_ga"], grads["d_b_ga"],
             grads["d_a_param"]) = pg
        ride = None
        if i == 0:
            dwkv = _kv_bwd(mem2, dks_l, dvs_l)
            small_vec = _pack([grads[e] for e in group[0]], group_rows[0])
            ride = _Exchange([dwout_i, dwkv, small_vec], [True, True, False])
        pb = _proj_bwd(i, dproj, dres, xs[i], gs[i], bs[i], wt_full[i], ride)
        up, dwt = pb[:2]
        if ride:
            recv_wo[0], recv_kv, recv_small[0] = pb[2:]
        dwt_l.append(dwt)
    grad_x = up[None]

    recv_wt[0], = _Exchange([dwt_l[-1]], [True]).run("scatter_last")

    outs = {}
    for t, n in enumerate(in_names):
        g, d, mn, vn = _sum_adam(f"adam_{n}", recv_wt[t], W[n][0], M[n][0], V[n][0], True)
        outs[n] = (g[None], d[None], mn[None], vn[None])
    wo_res = [_sum_adam(f"adam_w_out{l}", recv_wo[l], w_out[l], m_w_out[l], v_w_out[l], False) for l in range(DEPTH)]
    outs["w_out"] = tuple(jnp.stack([wo_res[l][j] for l in range(DEPTH)]) for j in range(4))
    outs["mem_kv_w"] = _sum_adam("adam_mem_kv_w", recv_kv, mem_kv_w, m_mem_kv_w, v_mem_kv_w, False)

    def entry_of(tree, e):
        if "#" in e:
            n, l = e.split("#")
            return tree[n][int(l):int(l) + 1]
        if e == "loss" or e in sharded:
            return jnp.zeros(small_shape[e], F32)
        return tree[e]

    small = [{}, {}, {}, {}]
    for l in range(DEPTH):
        packed = [_pack([entry_of(t, e) for e in group[l]], group_rows[l]) for t in (W, M, V)]
        res = _small_sum_adam(f"small_adam{l}", recv_small[l], *packed)
        for j in range(4):
            flat, o = res[j].reshape(-1), 0
            for e in group[l]:
                small[j][e] = flat[o:o + _size(small_shape[e])].reshape(small_shape[e])
                o += _size(small_shape[e])
    loss = small[0]["loss"][0]
    for j in range(4):
        for n in ("ln_g", "ln_b"):
            small[j][n] = jnp.concatenate([small[j][f"{n}#{l}"] for l in range(DEPTH)], axis=0)
    g_small = small[0]
    for n, _, sh in SMALL:
        if not sh:
            outs[n] = tuple(small[j][n] for j in range(4))
    per = TOK // NDEV
    g_sh = {n: lax.dynamic_slice_in_dim(g_small[n], me * per, per, axis=len(s) - 1) for n, s, sh in SMALL if sh}
    gp = _pack([g_sh[n] for n in shard_names], 8)
    d_p, m_p, v_p = _adam_only(gp, small_shard, _pack([M[n] for n in shard_names], 8), _pack([V[n] for n in shard_names], 8))
    o = 0
    for n in shard_names:
        cnt = _size(W[n].shape)
        outs[n] = (g_sh[n],) + tuple(t.reshape(-1)[o:o + cnt].reshape(W[n].shape) for t in (d_p, m_p, v_p))
        o += cnt

    order = ["mem_kv_w", "ln_g", "ln_b", "w_out", "hgrn_lb_logits", "a_w_in", "a_w_s", "a_b_s", "b_w_in", "b_norm_g", "c_w_in",
             "c_w_pool", "c_scale", "d_w_in", "d_conv_w", "d_conv_b", "d_w_gx", "d_b_gx", "d_w_ga", "d_b_ga", "d_a_param"]
    result = [loss, grad_x]
    for j in range(4):
        result += [outs[n][j].reshape(W[n].shape) for n in order]
    return tuple(result)
```

```python
import functools

import jax
import jax.numpy as jnp
from jax import lax
from jax.experimental import pallas as pl
from jax.experimental.pallas import tpu as pltpu

F32 = jnp.float32
MXU = jnp.bfloat16
WIRE = jnp.bfloat16

D = 1024
TOK = 768
XW = 256
NMEM = 256
XHEADS = 4
XSCALE = 64 ** -0.5
NH = 6
HD = 128
CH = 16
POOL_WINDOWS = (2, 4, 8, 16)
POOL_GROUP = 192
DEPTH = 4
ALPHA = (2 * DEPTH) ** 0.25
LN_EPS = 1e-5
RMS_EPS = 1e-6
LRU_C = 8.0
B1, B2, LR, EPS, WD, STEP = 0.9, 0.999, 0.001, 1e-8, 0.01, 10

NDEV = 8
TS_FWD = {0: 512, 1: 256, 2: 512, 3: 256}
TS_BWD = {0: 256, 1: 256, 2: 512, 3: 256}
TSB = 512
VMEM_LIMIT = 58 * 1024 * 1024

KIND_WIDTHS = {0: 2 * TOK + XW + D, 1: 3 * TOK + XW + D, 2: TOK + XW + D, 3: TOK + XW + D}


def _mm(a, b, ca, cb):
    return lax.dot_general(a.astype(MXU), b.astype(MXU), (((ca,), (cb,)), ((), ())), preferred_element_type=F32)


def _nn(a, b):
    return _mm(a, b, 1, 0)


def _nt(a, b):
    return _mm(a, b, 1, 1)


def _tn(a, b):
    return _mm(a, b, 0, 0)


def _bmm(a, b, ca, cb):
    return lax.dot_general(a.astype(MXU), b.astype(MXU), (((ca,), (cb,)), ((0,), (0,))), preferred_element_type=F32)


def _sigmoid(x):
    return 1.0 / (1.0 + jnp.exp(-x))


def _vjp1(fn, x, dy):
    return jax.vjp(fn, x)[1](dy)[0]


def _rowsum(x):
    return jnp.sum(x, axis=0, keepdims=True)


def _row(x, r):
    sel = lax.broadcasted_iota(jnp.int32, x.shape, 0) == r
    return jnp.sum(jnp.where(sel, x, 0.0), axis=0, keepdims=True)


def _acc(ref, val):
    ref[...] += val


def _cparams(sem=None):
    return pltpu.CompilerParams(dimension_semantics=sem, vmem_limit_bytes=VMEM_LIMIT)


def _res(a):
    nd = a.ndim
    return pl.BlockSpec(a.shape, lambda i: (0,) * nd)


def _res_sds(shape):
    nd = len(shape)
    return pl.BlockSpec(shape, lambda i: (0,) * nd)


def _row_spec(width, nt, rev, ts):
    if rev:
        return pl.BlockSpec((ts, width), lambda i: (nt - 1 - i, 0))
    return pl.BlockSpec((ts, width), lambda i: (i, 0))


def _call(body, name, grid, ins, outs, scratch=(), sem=("arbitrary",)):
    arrays = [a for a, _ in ins]
    return pl.pallas_call(
        body, name=name, grid=grid,
        in_specs=[s for _, s in ins],
        out_specs=[s for _, s in outs],
        out_shape=[o for o, _ in outs],
        scratch_shapes=list(scratch),
        compiler_params=_cparams(sem),
    )(*arrays)


def _xattn_fwd(qx, ks_ref, vs_ref):
    o = None
    ps = []
    for h in range(XHEADS):
        s = _nt(qx, ks_ref[h]) * XSCALE
        s = s - jnp.max(s, axis=-1, keepdims=True)
        e = jnp.exp(s)
        p = e * (1.0 / jnp.sum(e, axis=-1, keepdims=True))
        ps.append(p)
        oh = _nn(p, vs_ref[h])
        o = oh if o is None else o + oh
    return o, ps


def _xattn_bwd(qx, ps, dxo, ks_ref, vs_ref, dks_ref, dvs_ref):
    dq = None
    for h in range(XHEADS):
        p = ps[h]
        dp = _nt(dxo, vs_ref[h])
        ds = p * (dp - jnp.sum(dp * p, axis=-1, keepdims=True))
        dqh = _nn(ds, ks_ref[h]) * XSCALE
        dq = dqh if dq is None else dq + dqh
        dks_ref[h] += _tn(ds, qx) * XSCALE
        dvs_ref[h] += _tn(p, dxo)
    return dq


def _tril128():
    r = lax.broadcasted_iota(jnp.int32, (HD, HD), 0)
    c = lax.broadcasted_iota(jnp.int32, (HD, HD), 1)
    return c <= r


def _gmlp_fwd(u, v, ws_ref, bs_ref):
    ts = u.shape[0]
    ug = jax.nn.gelu(u)
    vg = jax.nn.gelu(v)
    tri = _tril128()
    toks, res = [], []
    for g in range(NH):
        sl = slice(g * HD, (g + 1) * HD)
        vgh = vg[:, sl]
        cen = vgh - jnp.mean(vgh, axis=-1, keepdims=True)
        rstd = lax.rsqrt(jnp.mean(cen * cen, axis=-1, keepdims=True) + LN_EPS)
        vn = cen * rstd
        w = jnp.where(tri, ws_ref[g], 0.0).astype(MXU)
        mix = jnp.concatenate([_nn(w, vn[n * HD:(n + 1) * HD]) + bs_ref[g] for n in range(ts // HD)], axis=0)
        toks.append(ug[:, sl] * mix)
        res.append((vn, rstd, mix, w))
    return jnp.concatenate(toks, axis=1), (ug, res)


def _gmlp_bwd(u, v, fres, dtok, dws_ref, dbs_ref):
    ts = u.shape[0]
    ug, res = fres
    tri = _tril128()
    dugs, dvgs = [], []
    for g in range(NH):
        sl = slice(g * HD, (g + 1) * HD)
        vn, rstd, mix, w = res[g]
        dmix = dtok[:, sl] * ug[:, sl]
        dugs.append(dtok[:, sl] * mix)
        dvn_rows = []
        dw = None
        dbs = None
        for n in range(ts // HD):
            dm = dmix[n * HD:(n + 1) * HD]
            dvn_rows.append(_tn(w, dm))
            t = _nt(dm, vn[n * HD:(n + 1) * HD])
            dw = t if dw is None else dw + t
            dbs = dm if dbs is None else dbs + dm
        dws_ref[g] += jnp.where(tri, dw, 0.0)
        dbs_ref[g] += dbs
        dvn = jnp.concatenate(dvn_rows, axis=0)
        dvgs.append(rstd * (dvn - jnp.mean(dvn, axis=-1, keepdims=True) - vn * jnp.mean(dvn * vn, axis=-1, keepdims=True)))
    du = _vjp1(jax.nn.gelu, u, jnp.concatenate(dugs, axis=1))
    dv = _vjp1(jax.nn.gelu, v, jnp.concatenate(dvgs, axis=1))
    return du, dv


def _chunk_cumsum(x):
    row = lax.broadcasted_iota(jnp.int32, x.shape, 0) % CH
    for s in (1, 2, 4, 8):
        x = x + jnp.where(row >= s, pltpu.roll(x, s, 0), 0.0)
    return x


def _chunk_revcumsum(x):
    n = x.shape[0]
    row = lax.broadcasted_iota(jnp.int32, x.shape, 0) % CH
    for s in (1, 2, 4, 8):
        x = x + jnp.where(row < CH - s, pltpu.roll(x, n - s, 0), 0.0)
    return x


def _chunk_sum(x):
    n, w = x.shape
    return jnp.sum(x.reshape(n // CH, CH, w), axis=1)


def _chunk_bcast(c, n):
    nch, w = c.shape
    return jnp.broadcast_to(c[:, None, :], (nch, CH, w)).reshape(n, w)


def _lower_bound(lb_logits, layer):
    lg = lb_logits
    e = jnp.exp(lg - jnp.max(lg, axis=0, keepdims=True))
    p = e / jnp.sum(e, axis=0, keepdims=True)
    row = lax.broadcasted_iota(jnp.int32, p.shape, 0)
    lb = jnp.sum(jnp.where((row >= 1) & (row <= layer), p, 0.0), axis=0, keepdims=True)
    return lb, p


def _hgrn_prep(q, fl, lb):
    n = q.shape[0]
    sg = _sigmoid(fl)
    f = lb + (1.0 - lb) * sg
    lf = jnp.log(f)
    k = 1.0 - f
    sq = _sigmoid(q)
    qf = q * sq
    g = _chunk_cumsum(lf)
    tot = _chunk_sum(lf)
    gl = _chunk_bcast(tot, n)
    eg = jnp.exp(g)
    eng = jnp.exp(-g)
    egl = jnp.exp(gl - g)
    return dict(sg=sg, f=f, k=k, sq=sq, qf=qf, eg=eg, eng=eng, egl=egl,
                qd=qf * eg, ki=k * eng, ke=k * egl, dch=jnp.exp(tot))


def _hgrn_mask():
    r = lax.broadcasted_iota(jnp.int32, (HD, HD), 0)
    c = lax.broadcasted_iota(jnp.int32, (HD, HD), 1)
    return (r // CH == c // CH) & (c <= r)


def _hgrn_states(v3, ke3, dch_h, st_in):
    nch = v3.shape[0]
    ut = _bmm(v3, ke3, 1, 1)
    dfull = jnp.broadcast_to(dch_h[:, None, :], (nch, HD, HD))
    st, sts = st_in, []
    for c in range(nch):
        sts.append(st)
        st = st * dfull[c] + ut[c]
    return jnp.stack(sts), st, dfull


def _hgrn_fwd(q, fl, inp, lb, ng, st_ref):
    n = q.shape[0]
    nch = n // CH
    pr = _hgrn_prep(q, fl, lb)
    mask = _hgrn_mask()
    toks = []
    for h in range(NH):
        sl = slice(h * HD, (h + 1) * HD)
        qd, ki, ke, v = pr["qd"][:, sl], pr["ki"][:, sl], pr["ke"][:, sl], inp[:, sl]
        qd3 = qd.astype(MXU).reshape(nch, CH, HD)
        v3 = v.astype(MXU).reshape(nch, CH, HD)
        ke3 = ke.astype(MXU).reshape(nch, CH, HD)
        sts, st_ref[h], _ = _hgrn_states(v3, ke3, pr["dch"][:, sl], st_ref[h])
        o = _bmm(qd3, sts, 2, 2).reshape(n, HD)
        intra = []
        for b in range(n // HD):
            bs = slice(b * HD, (b + 1) * HD)
            a = jnp.where(mask, _nt(qd[bs], ki[bs]), 0.0)
            intra.append(_nn(a, v[bs]))
        o = o + jnp.concatenate(intra, axis=0)
        r = lax.rsqrt(jnp.mean(o * o, axis=-1, keepdims=True) + RMS_EPS)
        toks.append(o * r * ng[:, sl])
    return jnp.concatenate(toks, axis=1)


def _hgrn_bwd(q, fl, inp, lb, ng, dtok, ststart_ref, dst_ref, dng_ref, dlb_ref):
    n = q.shape[0]
    nch = n // CH
    pr = _hgrn_prep(q, fl, lb)
    mask = _hgrn_mask()
    dqd_l, dki_l, dke_l, dv_l, ddch_l, dng_l, toks = [], [], [], [], [], [], []
    for h in range(NH):
        sl = slice(h * HD, (h + 1) * HD)
        qd, ki, ke, v = pr["qd"][:, sl], pr["ki"][:, sl], pr["ke"][:, sl], inp[:, sl]
        qd3 = qd.astype(MXU).reshape(nch, CH, HD)
        v3 = v.astype(MXU).reshape(nch, CH, HD)
        ke3 = ke.astype(MXU).reshape(nch, CH, HD)
        sts, _, dfull = _hgrn_states(v3, ke3, pr["dch"][:, sl], ststart_ref[h])
        o = _bmm(qd3, sts, 2, 2).reshape(n, HD)
        a_l = []
        intra = []
        for b in range(n // HD):
            bs = slice(b * HD, (b + 1) * HD)
            a = jnp.where(mask, _nt(qd[bs], ki[bs]), 0.0)
            a_l.append(a)
            intra.append(_nn(a, v[bs]))
        o = o + jnp.concatenate(intra, axis=0)
        r = lax.rsqrt(jnp.mean(o * o, axis=-1, keepdims=True) + RMS_EPS)
        toks.append(o * r * ng[:, sl])
        dt = dtok[:, sl]
        dng_l.append(_rowsum(dt * o * r))
        dn = dt * ng[:, sl]
        do = r * dn - o * (r * r * r) * jnp.mean(dn * o, axis=-1, keepdims=True)
        do3 = do.astype(MXU).reshape(nch, CH, HD)
        dqd_rows, dki_rows, dv_rows = [], [], []
        for b in range(n // HD):
            bs = slice(b * HD, (b + 1) * HD)
            da = jnp.where(mask, _nt(do[bs], v[bs]), 0.0)
            dqd_rows.append(_nn(da, ki[bs]))
            dki_rows.append(_tn(da, qd[bs]))
            dv_rows.append(_tn(a_l[b], do[bs]))
        dqd = jnp.concatenate(dqd_rows, axis=0) + _bmm(do3, sts, 2, 1).reshape(n, HD)
        dki = jnp.concatenate(dki_rows, axis=0)
        dv = jnp.concatenate(dv_rows, axis=0)
        wt = _bmm(do3, qd3, 1, 1)
        dst, dstn_l = dst_ref[h], [None] * nch
        for c in reversed(range(nch)):
            dstn_l[c] = dst
            dst = wt[c] + dst * dfull[c]
        dst_ref[h] = dst
        dstn = jnp.stack(dstn_l)
        dv = dv + _bmm(ke3, dstn, 2, 2).reshape(n, HD)
        dke = _bmm(v3, dstn, 2, 1).reshape(n, HD)
        ddch_l.append(jnp.sum(sts * dstn, axis=1))
        dqd_l.append(dqd)
        dki_l.append(dki)
        dke_l.append(dke)
        dv_l.append(dv)
    dqd = jnp.concatenate(dqd_l, axis=1)
    dki = jnp.concatenate(dki_l, axis=1)
    dke = jnp.concatenate(dke_l, axis=1)
    dinp = jnp.concatenate(dv_l, axis=1)
    ddch = jnp.concatenate(ddch_l, axis=1)
    _acc(dng_ref, jnp.concatenate(dng_l, axis=1))
    dqf = dqd * pr["eg"]
    dke_ke = dke * pr["ke"]
    dg = dqd * pr["qd"] - dki * pr["ki"] - dke_ke
    dk = dki * pr["eng"] + dke * pr["egl"]
    dgl = _chunk_sum(dke_ke) + ddch * pr["dch"]
    dlf = _chunk_revcumsum(dg) + _chunk_bcast(dgl, n)
    df = dlf / pr["f"] - dk
    sg = pr["sg"]
    dfl = df * (1.0 - lb) * sg * (1.0 - sg)
    _acc(dlb_ref, _rowsum(df * (1.0 - sg)))
    sq = pr["sq"]
    dq = dqf * (sq * (1.0 + q * (1.0 - sq)))
    return jnp.concatenate(toks, axis=1), dq, dfl, dinp


def _pool_select(s2, s4, s8, s16):
    col = lax.broadcasted_iota(jnp.int32, (1, TOK), 1)
    return jnp.where(col < POOL_GROUP, s2, jnp.where(col < 2 * POOL_GROUP, s4, jnp.where(col < 3 * POOL_GROUP, s8, s16)))


def _pool_cnt(pos0, n):
    pos = pos0 + lax.broadcasted_iota(jnp.int32, (n, TOK), 0) + 1
    col = lax.broadcasted_iota(jnp.int32, (n, TOK), 1)
    w = jnp.where(col < POOL_GROUP, 2, jnp.where(col < 2 * POOL_GROUP, 4, jnp.where(col < 3 * POOL_GROUP, 8, 16)))
    return jnp.minimum(pos, w).astype(F32)


def _pool_fwd(p, halo, pos0, wbd, scale):
    n = p.shape[0]
    ext = jnp.concatenate([halo, p], axis=0)
    s2 = ext + pltpu.roll(ext, 1, 0)
    s4 = s2 + pltpu.roll(s2, 2, 0)
    s8 = s4 + pltpu.roll(s4, 4, 0)
    s16 = s8 + pltpu.roll(s8, 8, 0)
    win = _pool_select(s2, s4, s8, s16)[16:]
    cnt = _pool_cnt(pos0, n)
    diff = win / cnt - p
    y = _nn(diff, wbd)
    return y * scale, (diff, y, cnt)


def _pool_bwd(fres, dtok, nxt_ref, wbd, scale, dwbd_ref, dscale_ref):
    diff, y, cnt = fres
    n = diff.shape[0]
    _acc(dscale_ref, _rowsum(dtok * y))
    dy = dtok * scale
    ddiff = _nt(dy, wbd)
    dwbd_ref[...] += _tn(diff, dy)
    qv = ddiff / cnt
    ext = jnp.concatenate([qv, nxt_ref[...]], axis=0)
    m = n + 16
    s2 = ext + pltpu.roll(ext, m - 1, 0)
    s4 = s2 + pltpu.roll(s2, m - 2, 0)
    s8 = s4 + pltpu.roll(s4, m - 4, 0)
    s16 = s8 + pltpu.roll(s8, m - 8, 0)
    adj = _pool_select(s2, s4, s8, s16)[:n]
    nxt_ref[...] = qv[:16]
    return adj - ddiff


def _neg_expm1(x):
    return jnp.where(jnp.abs(x) < 1e-2, -x * (1.0 + x * (0.5 + x * (1.0 / 6.0))), 1.0 - jnp.exp(x))


def _lru_gates(xc, zx, za, ap, first):
    gx = _sigmoid(zx)
    ga = _sigmoid(za)
    sp = jnp.maximum(-ap, 0.0) + jnp.log(1.0 + jnp.exp(-jnp.abs(ap)))
    log_a = -LRU_C * ga * sp
    a = jnp.exp(log_a)
    mult = jnp.sqrt(_neg_expm1(2.0 * log_a))
    mult = jnp.where(first, 1.0, mult)
    return a, mult * gx * xc


def _scan_fwd(a, b, h0):
    n = a.shape[0]
    row = lax.broadcasted_iota(jnp.int32, a.shape, 0)
    s = 1
    while s < n:
        keep = row >= s
        b = b + a * jnp.where(keep, pltpu.roll(b, s, 0), 0.0)
        a = a * jnp.where(keep, pltpu.roll(a, s, 0), 1.0)
        s *= 2
    return b + a * h0


def _scan_bwd(an, d, dh_next):
    n = an.shape[0]
    row = lax.broadcasted_iota(jnp.int32, an.shape, 0)
    s = 1
    while s < n:
        keep = row < n - s
        d = d + an * jnp.where(keep, pltpu.roll(d, n - s, 0), 0.0)
        an = an * jnp.where(keep, pltpu.roll(an, n - s, 0), 1.0)
        s *= 2
    return d + an * dh_next


def _lru_conv(xb, halo, cw_ref, cb):
    ext = jnp.concatenate([halo, xb], axis=0)
    sh = [pltpu.roll(ext, 3 - j, 0)[8:] if j < 3 else xb for j in range(4)]
    xc = cb
    for j in range(4):
        xc = xc + cw_ref[pl.ds(j, 1), :] * sh[j]
    return xc, sh


def _lru_fwd(xb, halo, pos0, prm, h0):
    cw, cb, wgx, bgx, wga, bga, ap = prm
    n = xb.shape[0]
    xc, sh = _lru_conv(xb, halo, cw, cb[...])
    zx = jnp.concatenate([_nn(xc[:, h * HD:(h + 1) * HD], wgx[h]) for h in range(NH)], axis=1) + bgx[...]
    za = jnp.concatenate([_nn(xc[:, h * HD:(h + 1) * HD], wga[h]) for h in range(NH)], axis=1) + bga[...]
    first = (pos0 + lax.broadcasted_iota(jnp.int32, (n, 1), 0)) == 0
    a, b = _lru_gates(xc, zx, za, ap[...], first)
    hseq = _scan_fwd(a, b, h0)
    return hseq, (xc, sh, zx, za, first, a)


def _lru_bwd(fres, hseq, h0, dtok, prm, carry_refs, grad_refs):
    cw, cb, wgx, bgx, wga, bga, ap = prm
    xc, sh, zx, za, first, a = fres
    anext_ref, dhnext_ref, dxcnext_ref = carry_refs
    dcw_ref, dcb_ref, dwgx_ref, dbgx_ref, dwga_ref, dbga_ref, dap_ref = grad_refs
    n = xc.shape[0]
    an = jnp.where(lax.broadcasted_iota(jnp.int32, a.shape, 0) == n - 1, anext_ref[...], pltpu.roll(a, n - 1, 0))
    dh = _scan_bwd(an, dtok, dhnext_ref[...])
    hprev = jnp.where(lax.broadcasted_iota(jnp.int32, hseq.shape, 0) == 0, h0, pltpu.roll(hseq, 1, 0))
    da = dh * hprev
    anext_ref[...] = _row(a, 0)
    dhnext_ref[...] = _row(dh, 0)
    _, vjp = jax.vjp(lambda xc_, zx_, za_, ap_: _lru_gates(xc_, zx_, za_, ap_, first), xc, zx, za, ap[...])
    dxc, dzx, dza, dap = vjp((da, dh))
    _acc(dap_ref, dap)
    _acc(dbgx_ref, _rowsum(dzx))
    _acc(dbga_ref, _rowsum(dza))
    parts = []
    for h in range(NH):
        sl = slice(h * HD, (h + 1) * HD)
        parts.append(_nt(dzx[:, sl], wgx[h]) + _nt(dza[:, sl], wga[h]))
        dwgx_ref[h] += _tn(xc[:, sl], dzx[:, sl])
        dwga_ref[h] += _tn(xc[:, sl], dza[:, sl])
    dxc = dxc + jnp.concatenate(parts, axis=1)
    _acc(dcb_ref, _rowsum(dxc))
    for j in range(4):
        dcw_ref[pl.ds(j, 1), :] += _rowsum(dxc * sh[j])
    ext = jnp.concatenate([dxc, dxcnext_ref[...]], axis=0)
    m = n + 8
    dxb = cw[pl.ds(3, 1), :] * dxc
    for j in range(3):
        dxb = dxb + cw[pl.ds(j, 1), :] * pltpu.roll(ext, m - (3 - j), 0)[:n]
    dxcnext_ref[...] = dxc[:8]
    return dxb


def _layer_fwd(kind, layer, xprev, gprev, bprev, wt, wout, ks, vs, prm, ride=None):
    TS = TS_FWD[kind]
    _rows = functools.partial(_row_spec, ts=TS)
    S = xprev.shape[0]
    nt = S // TS
    N = wt.shape[0]
    nprm = len(prm)
    nch = TS // CH

    outs = [(jax.ShapeDtypeStruct((S, N), F32), _rows(N, nt, False)),
            (jax.ShapeDtypeStruct((S, D), F32), _rows(D, nt, False)),
            (jax.ShapeDtypeStruct((S, 1), F32), _rows(1, nt, False)),
            (jax.ShapeDtypeStruct((S, XHEADS * NMEM), MXU), _rows(XHEADS * NMEM, nt, False))]
    scratch = []
    if kind == 1:
        outs.append((jax.ShapeDtypeStruct((nt, NH, HD, HD), F32), pl.BlockSpec((None, NH, HD, HD), lambda i: (i, 0, 0, 0))))
        scratch = [pltpu.VMEM((NH, HD, HD), F32)]
    elif kind == 2:
        scratch = [pltpu.VMEM((16, TOK), F32)]
    elif kind == 3:
        outs.append((jax.ShapeDtypeStruct((nt * 8, TOK), F32), pl.BlockSpec((8, TOK), lambda i: (i, 0))))
        scratch = [pltpu.VMEM((8, TOK), F32), pltpu.VMEM((1, TOK), F32)]
    nout = len(outs)
    nscr = len(scratch)
    nride = len(ride.arrays) if ride else 0

    def body(*refs):
        x_ref, g_ref, b_ref, wt_ref, wout_ref, ks_ref, vs_ref = refs[:7]
        prm_refs = refs[7:7 + nprm]
        nin = 7 + nprm + nride
        ride_src = refs[7 + nprm:nin]
        out_refs = refs[nin:nin + nout]
        ride_dst = refs[nin + nout:nin + nout + nride]
        scr = refs[nin + nout + nride:nin + nout + nride + nscr]
        ride_sems = refs[nin + nout + nride + nscr:]
        proj_ref, xhat_ref, rstd_ref = out_refs[:3]
        i = pl.program_id(0)
        if ride:
            @pl.when(i == 0)
            def _():
                ride.start(ride_src, ride_dst, ride_sems)

        xin = x_ref[...] * g_ref[...] + b_ref[...]
        proj = _nt(xin, wt_ref[...])
        proj_ref[...] = proj
        if kind == 0:
            tok, _ = _gmlp_fwd(proj[:, :TOK], proj[:, TOK:2 * TOK], prm_refs[0], prm_refs[1])
        elif kind == 1:
            st_ref, = scr

            @pl.when(i == 0)
            def _():
                st_ref[...] = jnp.zeros_like(st_ref)

            out_refs[4][...] = st_ref[...]
            lb, _ = _lower_bound(prm_refs[0][...], layer)
            tok = _hgrn_fwd(proj[:, :TOK], proj[:, TOK:2 * TOK], proj[:, 2 * TOK:3 * TOK], lb, prm_refs[1][...],
                            st_ref)
        elif kind == 2:
            halo_ref, = scr

            @pl.when(i == 0)
            def _():
                halo_ref[...] = jnp.zeros_like(halo_ref)

            p = proj[:, :TOK]
            tok, _ = _pool_fwd(p, halo_ref[...], i * TS, prm_refs[0][...], prm_refs[1][...])
            halo_ref[...] = p[TS - 16:]
        else:
            halo_ref, h_ref = scr

            @pl.when(i == 0)
            def _():
                halo_ref[...] = jnp.zeros_like(halo_ref)
                h_ref[...] = jnp.zeros_like(h_ref)

            out_refs[4][...] = jnp.broadcast_to(h_ref[...], (8, TOK))
            xb = proj[:, :TOK]
            tok, _ = _lru_fwd(xb, halo_ref[...], i * TS, prm_refs, h_ref[...])
            halo_ref[...] = xb[TS - 8:]
            h_ref[...] = _row(tok, TS - 1)
        qx = proj[:, N - D - XW:N - D]
        gate = proj[:, N - D:]
        xo, ps = _xattn_fwd(qx, ks_ref, vs_ref)
        out_refs[3][...] = jnp.concatenate(ps, axis=1).astype(MXU)
        mixed = jnp.concatenate([tok, xo], axis=1) * (gate * _sigmoid(gate))
        z = ALPHA * xin + _nn(mixed, wout_ref[...])
        cen = z - jnp.mean(z, axis=-1, keepdims=True)
        rstd = lax.rsqrt(jnp.mean(cen * cen, axis=-1, keepdims=True) + LN_EPS)
        xhat_ref[...] = cen * rstd
        rstd_ref[...] = rstd
        if ride:
            @pl.when(i == nt - 1)
            def _():
                ride.wait(ride_src, ride_dst, ride_sems)

    ins = [(xprev, _rows(D, nt, False)), (gprev, _res(gprev)), (bprev, _res(bprev)), (wt, _res(wt)), (wout, _res(wout)),
           (ks, _res(ks)), (vs, _res(vs))] + [(p, _res(p)) for p in prm]
    if ride:
        ins += [(a, _ANY) for a in ride.arrays]
        outs += [(s, _ANY) for s in ride.out_shapes]
        scratch = scratch + ride.scratch
    return _call(body, f"layer{layer}_fwd", (nt,), ins, outs, scratch)


def _layer_bwd(kind, layer, up, is_last, xhat, rstd, probs, g_i, b_i, proj, wout, ks, vs, prm, extra, ride=None):
    TS = TS_BWD[kind]
    _rows = functools.partial(_row_spec, ts=TS)
    S = xhat.shape[0]
    nt = S // TS
    N = proj.shape[1]
    nprm = len(prm)
    nch = TS // CH

    ins = [(up, _rows(D, nt, True)), (xhat, _rows(D, nt, True)), (rstd, _rows(1, nt, True)), (g_i, _res(g_i)), (b_i, _res(b_i)),
           (proj, _rows(N, nt, True)), (wout, _res(wout)), (ks, _res(ks)), (vs, _res(vs)),
           (probs, _rows(XHEADS * NMEM, nt, True))] + [(p, _res(p)) for p in prm]
    nfixed = 10
    if kind == 1:
        ins.append((extra, pl.BlockSpec((None, NH, HD, HD), lambda i: (nt - 1 - i, 0, 0, 0))))
    elif kind == 2:
        hb = TS // 16
        ins.append((proj, pl.BlockSpec((16, TOK), lambda i: (jnp.maximum((nt - 1 - i) * hb - 1, 0), 0))))
    elif kind == 3:
        hb = TS // 8
        ins.append((proj, pl.BlockSpec((8, TOK), lambda i: (jnp.maximum((nt - 1 - i) * hb - 1, 0), 0))))
        ins.append((extra, pl.BlockSpec((8, TOK), lambda i: (nt - 1 - i, 0))))
    nin = len(ins)

    def acc(shape):
        return (jax.ShapeDtypeStruct(shape, F32), _res_sds(shape))

    outs = [(jax.ShapeDtypeStruct((S, D), F32), _rows(D, nt, True)),
            (jax.ShapeDtypeStruct((S, N), MXU), _rows(N, nt, True)),
            (jax.ShapeDtypeStruct((D, D), WIRE), _res_sds((D, D))),
            acc((XHEADS, NMEM, XW)), acc((XHEADS, NMEM, XW)), acc((1, D)), acc((1, D)), acc((1, HD))]
    scratch = []
    if kind == 0:
        outs += [acc((NH, HD, HD)), acc((NH, HD, HD))]
    elif kind == 1:
        outs += [acc((1, TOK)), acc((1, TOK))]
        scratch = [pltpu.VMEM((NH, HD, HD), F32)]
    elif kind == 2:
        outs += [acc((TOK, TOK)), acc((1, TOK))]
        scratch = [pltpu.VMEM((16, TOK), F32)]
    else:
        outs += [acc((4, TOK)), acc((1, TOK)), acc((NH, HD, HD)), acc((1, TOK)), acc((NH, HD, HD)), acc((1, TOK)), acc((1, TOK))]
        scratch = [pltpu.VMEM((1, TOK), F32), pltpu.VMEM((1, TOK), F32), pltpu.VMEM((8, TOK), F32)]
    scratch = scratch + [pltpu.VMEM((D, D), F32)]
    nout = len(outs)
    nscr = len(scratch)
    nride = len(ride.arrays) if ride else 0

    def body(*refs):
        up_ref, xhat_ref, rstd_ref, g_ref, b_ref, proj_ref, wout_ref, ks_ref, vs_ref, probs_ref = refs[:nfixed]
        prm_refs = refs[nfixed:nfixed + nprm]
        ext_refs = refs[nfixed + nprm:nin]
        ride_src = refs[nin:nin + nride]
        o0 = nin + nride
        out_refs = refs[o0:o0 + nout]
        ride_dst = refs[o0 + nout:o0 + nout + nride]
        scr = refs[o0 + nout + nride:o0 + nout + nride + nscr - 1]
        dwout_acc = refs[o0 + nout + nride + nscr - 1]
        ride_sems = refs[o0 + nout + nride + nscr:]
        dres_ref, dproj_ref, dwout_ref, dks_ref, dvs_ref, dg_ref, db_ref, loss_ref = out_refs[:8]
        pgrad = out_refs[8:]
        i = pl.program_id(0)
        tile = nt - 1 - i

        @pl.when(i == 0)
        def _():
            if ride:
                ride.start(ride_src, ride_dst, ride_sems)
            for r in out_refs[3:]:
                r[...] = jnp.zeros_like(r)
            dwout_acc[...] = jnp.zeros_like(dwout_acc)
            for r in scr:
                if kind != 1 or r is scr[0]:
                    r[...] = jnp.zeros_like(r)

        xhat_v = xhat_ref[...]
        if is_last:
            err = xhat_v * g_ref[...] + b_ref[...] - up_ref[...]
            dxo = err * (1.0 / D)
            loss_ref[...] += jnp.sum(0.5 * jnp.mean(err * err, axis=-1, keepdims=True), axis=0, keepdims=True)
        else:
            dxo = up_ref[...]
        _acc(dg_ref, _rowsum(dxo * xhat_v))
        _acc(db_ref, _rowsum(dxo))
        dxh = dxo * g_ref[...]
        dz = rstd_ref[...] * (dxh - jnp.mean(dxh, axis=-1, keepdims=True)
                              - xhat_v * jnp.mean(dxh * xhat_v, axis=-1, keepdims=True))
        dres_ref[...] = ALPHA * dz

        proj = proj_ref[...]
        qx = proj[:, N - D - XW:N - D]
        gate = proj[:, N - D:]
        sgate = _sigmoid(gate)
        silu = gate * sgate
        dmixed = _nt(dz, wout_ref[...])
        dcat = dmixed * silu
        dtok = dcat[:, :TOK]
        if kind == 0:
            u, v = proj[:, :TOK], proj[:, TOK:2 * TOK]
            tok, fres = _gmlp_fwd(u, v, prm_refs[0], prm_refs[1])
            du, dv = _gmlp_bwd(u, v, fres, dtok, pgrad[0], pgrad[1])
            dproj_ref[:, :TOK] = du.astype(MXU)
            dproj_ref[:, TOK:2 * TOK] = dv.astype(MXU)
        elif kind == 1:
            dst_ref, = scr
            lb, _ = _lower_bound(prm_refs[0][...], layer)
            q, fl, inp = proj[:, :TOK], proj[:, TOK:2 * TOK], proj[:, 2 * TOK:3 * TOK]
            tok, dq, dfl, dinp = _hgrn_bwd(q, fl, inp, lb, prm_refs[1][...], dtok, ext_refs[0], dst_ref, pgrad[1], pgrad[0])
            dproj_ref[:, :TOK] = dq.astype(MXU)
            dproj_ref[:, TOK:2 * TOK] = dfl.astype(MXU)
            dproj_ref[:, 2 * TOK:3 * TOK] = dinp.astype(MXU)
        elif kind == 2:
            p = proj[:, :TOK]
            halo = jnp.where(tile == 0, 0.0, ext_refs[0][...])
            tok, fres = _pool_fwd(p, halo, tile * TS, prm_refs[0][...], prm_refs[1][...])
            dp = _pool_bwd(fres, dtok, scr[0], prm_refs[0][...], prm_refs[1][...], pgrad[0], pgrad[1])
            dproj_ref[:, :TOK] = dp.astype(MXU)
        else:
            xb = proj[:, :TOK]
            halo = jnp.where(tile == 0, 0.0, ext_refs[0][...])
            h0 = ext_refs[1][0:1]
            tok, fres = _lru_fwd(xb, halo, tile * TS, prm_refs, h0)
            dxb = _lru_bwd(fres, tok, h0, dtok, prm_refs, scr, pgrad)
            dproj_ref[:, :TOK] = dxb.astype(MXU)
        ps = [probs_ref[:, h * NMEM:(h + 1) * NMEM].astype(F32) for h in range(XHEADS)]
        xo = _nn(ps[0], vs_ref[0])
        for h in range(1, XHEADS):
            xo = xo + _nn(ps[h], vs_ref[h])
        dqx = _xattn_bwd(qx, ps, dcat[:, TOK:], ks_ref, vs_ref, dks_ref, dvs_ref)
        cat = jnp.concatenate([tok, xo], axis=1)
        dwout_acc[...] += _tn(cat * silu, dz)
        dgate = dmixed * cat * (sgate * (1.0 + gate * (1.0 - sgate)))
        dproj_ref[:, N - D - XW:N - D] = dqx.astype(MXU)
        dproj_ref[:, N - D:] = dgate.astype(MXU)

        @pl.when(i == nt - 1)
        def _():
            dwout_ref[...] = dwout_acc[...].astype(WIRE)
            if ride:
                ride.wait(ride_src, ride_dst, ride_sems)

    if ride:
        ins += [(a, _ANY) for a in ride.arrays]
        outs += [(s, _ANY) for s in ride.out_shapes]
        scratch = scratch + ride.scratch
    return _call(body, f"layer{layer}_bwd", (nt,), ins, outs, scratch)


def _proj_bwd(layer, dproj, dres, xprev, gprev, bprev, wt, ride=None):
    S = xprev.shape[0]
    nt = S // TSB
    N = wt.shape[0]

    nride = len(ride.arrays) if ride else 0

    def body(*refs):
        dproj_ref, dres_ref, x_ref, g_ref, b_ref, wt_ref = refs[:6]
        ride_src = refs[6:6 + nride]
        dx_ref, dwt_ref = refs[6 + nride:8 + nride]
        ride_dst = refs[8 + nride:8 + 2 * nride]
        acc_ref = refs[8 + 2 * nride]
        ride_sems = refs[9 + 2 * nride:]

        @pl.when(pl.program_id(0) == 0)
        def _():
            if ride:
                ride.start(ride_src, ride_dst, ride_sems)
            acc_ref[...] = jnp.zeros_like(acc_ref)

        dp = dproj_ref[...]
        xin = x_ref[...] * g_ref[...] + b_ref[...]
        dx_ref[...] = dres_ref[...] + _nn(dp, wt_ref[...])
        acc_ref[...] += _tn(dp, xin)

        @pl.when(pl.program_id(0) == nt - 1)
        def _():
            dwt_ref[...] = acc_ref[...].astype(WIRE)
            if ride:
                ride.wait(ride_src, ride_dst, ride_sems)

    ins = [(dproj, _row_spec(N, nt, False, TSB)), (dres, _row_spec(D, nt, False, TSB)), (xprev, _row_spec(D, nt, False, TSB)),
           (gprev, _res(gprev)), (bprev, _res(bprev)), (wt, _res(wt))]
    outs = [(jax.ShapeDtypeStruct((S, D), F32), _row_spec(D, nt, False, TSB)),
            (jax.ShapeDtypeStruct((N, D), WIRE), _res_sds((N, D)))]
    scratch = [pltpu.VMEM((N, D), F32)]
    if ride:
        ins += [(a, _ANY) for a in ride.arrays]
        outs += [(s, _ANY) for s in ride.out_shapes]
        scratch = scratch + ride.scratch
    return _call(body, f"layer{layer}_projbwd", (nt,), ins, outs, scratch)


def _head_mask(h):
    col = lax.broadcasted_iota(jnp.int32, (1, XW), 1)
    return (col // 64) == h


def _kv_fwd(mem, wkv):
    def body(mem_ref, w_ref, ks_ref, vs_ref):
        kv = _nn(mem_ref[...], w_ref[...])
        k, v = kv[:, :XW], kv[:, XW:]
        for h in range(XHEADS):
            ks_ref[h] = jnp.where(_head_mask(h), k, 0.0).astype(MXU)
            vs_ref[h] = jnp.where(_head_mask(h), v, 0.0).astype(MXU)

    sds = jax.ShapeDtypeStruct((XHEADS, NMEM, XW), MXU)
    return pl.pallas_call(body, name="kv_fwd", out_shape=(sds, sds), compiler_params=_cparams())(mem, wkv)


def _kv_bwd(mem, dks_l, dvs_l):
    def body(mem_ref, *refs):
        dks_refs, dvs_refs, out_ref = refs[:DEPTH], refs[DEPTH:2 * DEPTH], refs[2 * DEPTH]
        dk = jnp.zeros((NMEM, XW), F32)
        dv = jnp.zeros((NMEM, XW), F32)
        for h in range(XHEADS):
            m = _head_mask(h)
            for l in range(DEPTH):
                dk = dk + jnp.where(m, dks_refs[l][h], 0.0)
                dv = dv + jnp.where(m, dvs_refs[l][h], 0.0)
        out_ref[...] = _tn(mem_ref[...], jnp.concatenate([dk, dv], axis=1)).astype(WIRE)

    return pl.pallas_call(body, name="kv_bwd", out_shape=jax.ShapeDtypeStruct((D, 2 * XW), WIRE),
                          compiler_params=_cparams())(mem, *dks_l, *dvs_l)


def _prep_weights(w_ins, w_out, wkv):
    def body(a_ref, b_ref, c_ref, d_ref, wo_ref, kv_ref, ao, bo, co, do, wo0, wo1, wo2, wo3, kvo):
        for src, dst in ((a_ref, ao), (b_ref, bo), (c_ref, co), (d_ref, do)):
            dst[...] = src[...].T.astype(MXU)
        for l, dst in enumerate((wo0, wo1, wo2, wo3)):
            dst[...] = wo_ref[l].astype(MXU)
        kvo[...] = kv_ref[...].astype(MXU)

    outs = [jax.ShapeDtypeStruct((w.shape[1], w.shape[0]), MXU) for w in w_ins]
    outs += [jax.ShapeDtypeStruct(w_out.shape[1:], MXU)] * DEPTH + [jax.ShapeDtypeStruct(wkv.shape, MXU)]
    return pl.pallas_call(body, name="prep_weights", out_shape=outs, compiler_params=_cparams())(*w_ins, w_out, wkv)


def _adam_math(w, g, m, v):
    m = B1 * m + (1.0 - B1) * g
    v = B2 * v + (1.0 - B2) * (g * g)
    m_hat = m / (1.0 - B1 ** STEP)
    v_hat = v / (1.0 - B2 ** STEP)
    delta = -LR * (m_hat / (jnp.sqrt(v_hat) + EPS) + WD * w)
    return delta, m, v


def _sum_adam(name, recv, w, m, v, transpose):
    rows, cols = recv.shape[1], recv.shape[2]

    def body(r_ref, w_ref, m_ref, v_ref, g_out, d_out, m_out, v_out, acc_ref):
        s = pl.program_id(0)

        @pl.when(s == 0)
        def _():
            acc_ref[...] = r_ref[...].astype(F32)

        @pl.when(s > 0)
        def _():
            acc_ref[...] += r_ref[...].astype(F32)

        @pl.when(s == NDEV - 1)
        def _():
            g = acc_ref[...].T if transpose else acc_ref[...]
            d, mn, vn = _adam_math(w_ref[...], g, m_ref[...], v_ref[...])
            g_out[...] = g
            d_out[...] = d
            m_out[...] = mn
            v_out[...] = vn

    sds = jax.ShapeDtypeStruct(w.shape, F32)
    ins = [(recv, pl.BlockSpec((None, rows, cols), lambda s: (s, 0, 0))), (w, _res(w)), (m, _res(m)), (v, _res(v))]
    outs = [(sds, _res_sds(w.shape))] * 4
    return _call(body, name, (NDEV,), ins, outs, [pltpu.VMEM((rows, cols), F32)])


def _bias_finalize(dbs_exp):
    def body(dbs_ref, dabs_ref):
        dabs_ref[...] = jnp.sum(dbs_ref[...], axis=-1)

    return pl.pallas_call(body, name="bias_finalize", out_shape=jax.ShapeDtypeStruct((NH, HD), F32),
                          compiler_params=_cparams())(dbs_exp)


def _lb_finalize(dlb, lb_logits):
    def body(dlb_ref, lg_ref, dlg_ref):
        total = jnp.zeros((DEPTH, TOK), F32)
        lg = lg_ref[...]
        e = jnp.exp(lg - jnp.max(lg, axis=0, keepdims=True))
        p = e / jnp.sum(e, axis=0, keepdims=True)
        row = lax.broadcasted_iota(jnp.int32, (DEPTH, TOK), 0)
        for layer in range(DEPTH):
            if layer % 4 != 1:
                continue
            dp = jnp.where((row >= 1) & (row <= layer), dlb_ref[...], 0.0)
            total = total + p * (dp - jnp.sum(p * dp, axis=0, keepdims=True))
        dlg_ref[...] = total

    return pl.pallas_call(body, name="lb_finalize", out_shape=jax.ShapeDtypeStruct((DEPTH, TOK), F32),
                          compiler_params=_cparams())(dlb, lb_logits)


def _small_sum_adam(name, gathered, w, m, v):
    rows = w.shape[0]

    def body(r_ref, w_ref, m_ref, v_ref, g_out, d_out, m_out, v_out):
        g = r_ref[0]
        for s in range(1, NDEV):
            g = g + r_ref[s]
        d, mn, vn = _adam_math(w_ref[...], g, m_ref[...], v_ref[...])
        g_out[...] = g
        d_out[...] = d
        m_out[...] = mn
        v_out[...] = vn

    sds = jax.ShapeDtypeStruct((rows, 128), F32)
    return pl.pallas_call(body, name=name, out_shape=(sds,) * 4, compiler_params=_cparams())(gathered, w, m, v)


def _group_adam(name, recvs, params):
    nk = len(recvs)

    def body(*refs):
        pos, oi = nk, nk + 3 * sum(p is not None for p in params)
        for k in range(nk):
            g = refs[k][0]
            for s in range(1, NDEV):
                g = g + refs[k][s]
            refs[oi][...] = g
            oi += 1
            if params[k] is not None:
                d, mn, vn = _adam_math(refs[pos][...], g, refs[pos + 1][...], refs[pos + 2][...])
                refs[oi][...] = d
                refs[oi + 1][...] = mn
                refs[oi + 2][...] = vn
                pos += 3
                oi += 3

    out_shape, counts = [], []
    for k in range(nk):
        counts.append(4 if params[k] is not None else 1)
        out_shape += [jax.ShapeDtypeStruct(recvs[k].shape[1:], F32)] * counts[-1]
    args = list(recvs) + [a for p in params if p is not None for a in p]
    flat = pl.pallas_call(body, name=name, out_shape=out_shape, compiler_params=_cparams())(*args)
    res, o = [], 0
    for cnt in counts:
        res.append(flat[o:o + cnt])
        o += cnt
    return res


def _adam_only(g, w, m, v):
    def body(g_ref, w_ref, m_ref, v_ref, d_out, m_out, v_out):
        d, mn, vn = _adam_math(w_ref[...], g_ref[...], m_ref[...], v_ref[...])
        d_out[...] = d
        m_out[...] = mn
        v_out[...] = vn

    sds = jax.ShapeDtypeStruct(w.shape, F32)
    return pl.pallas_call(body, name="shard_adam", out_shape=(sds,) * 3, compiler_params=_cparams())(g, w, m, v)


def _me_and_peers():
    x, y, c = lax.axis_index("x"), lax.axis_index("y"), lax.axis_index("c")
    me = 4 * x + 2 * y + c
    peers = []
    for k in range(1, NDEV):
        kx, ky, kc = (k >> 2) & 1, (k >> 1) & 1, k & 1
        px = x + kx - 2 * x * kx
        py = y + ky - 2 * y * ky
        pc = c + kc - 2 * c * kc
        peers.append(((px, py, pc), 4 * px + 2 * py + pc))
    return me, peers


_ANY = pl.BlockSpec(memory_space=pl.ANY)


class _Exchange:
    def __init__(self, arrays, split):
        self.arrays = list(arrays)
        self.split = list(split)
        n = len(self.arrays)
        self.out_shapes = []
        for a, sp in zip(self.arrays, self.split):
            rows = a.shape[0] // NDEV if sp else a.shape[0]
            self.out_shapes.append(jax.ShapeDtypeStruct((NDEV, rows, a.shape[1]), a.dtype))
        self.scratch = [pltpu.SemaphoreType.DMA((n, NDEV - 1)), pltpu.SemaphoreType.DMA((n, NDEV - 1)),
                        pltpu.SemaphoreType.DMA((n,))]

    def _block(self, src, t, d):
        if not self.split[t]:
            return src[t]
        rows = self.arrays[t].shape[0] // NDEV
        return src[t].at[pl.ds(d * rows, rows)]

    def start(self, src, dst, sems):
        send_sems, recv_sems, local_sems = sems
        me, peers = _me_and_peers()
        for t in range(len(self.arrays)):
            pltpu.make_async_copy(self._block(src, t, me), dst[t].at[me], local_sems.at[t]).start()
        for k, (dev, idx) in enumerate(peers):
            for t in range(len(self.arrays)):
                pltpu.make_async_remote_copy(src_ref=self._block(src, t, idx), dst_ref=dst[t].at[me],
                                             send_sem=send_sems.at[t, k], recv_sem=recv_sems.at[t, k],
                                             device_id=dev, device_id_type=pl.DeviceIdType.MESH).start()

    def wait(self, src, dst, sems):
        send_sems, recv_sems, local_sems = sems
        me, peers = _me_and_peers()

        def slot_copy(t, k, dev, idx):
            return pltpu.make_async_remote_copy(src_ref=dst[t].at[idx], dst_ref=dst[t].at[idx], send_sem=send_sems.at[t, k],
                                                recv_sem=recv_sems.at[t, k], device_id=dev,
                                                device_id_type=pl.DeviceIdType.MESH)

        for k, (dev, idx) in enumerate(peers):
            for t in range(len(self.arrays)):
                slot_copy(t, k, dev, idx).wait_recv()
        for k, (dev, idx) in enumerate(peers):
            for t in range(len(self.arrays)):
                slot_copy(t, k, dev, idx).wait_send()
        for t in range(len(self.arrays)):
            pltpu.make_async_copy(dst[t].at[me], dst[t].at[me], local_sems.at[t]).wait()

    def gather_by_chip(self, src, dst, sems):
        assert not any(self.split)
        send_sems, recv_sems, local_sems = sems
        n = len(self.arrays)
        x, y, c = lax.axis_index("x"), lax.axis_index("y"), lax.axis_index("c")
        me, sibling = 4 * x + 2 * y + c, (x, y, 1 - c)
        chips = [(1 - x, y), (x, 1 - y), (1 - x, 1 - y)]

        def index(chip, core):
            return 4 * chip[0] + 2 * chip[1] + core

        def copy(t, k, block, to, from_src):
            return pltpu.make_async_remote_copy(src_ref=src[t] if from_src else dst[t].at[block], dst_ref=dst[t].at[block],
                                                send_sem=send_sems.at[t, k], recv_sem=recv_sems.at[t, k],
                                                device_id=to, device_id_type=pl.DeviceIdType.MESH)

        local = [pltpu.make_async_copy(src[t], dst[t].at[me], local_sems.at[t]) for t in range(n)]
        for cp in local:
            cp.start()
        sends = []
        for t in range(n):
            sends.append(copy(t, 0, me, sibling, True))
            sends += [copy(t, 1 + j, me, (*chip, c), True) for j, chip in enumerate(chips)]
        for cp in sends:
            cp.start()
        for j, chip in enumerate(chips):
            for t in range(n):
                copy(t, 1 + j, index(chip, c), sibling, False).wait_recv()
                passed = copy(t, 4 + j, index(chip, c), sibling, False)
                passed.start()
                sends.append(passed)
        for t in range(n):
            copy(t, 0, index((x, y), 1 - c), sibling, False).wait_recv()
            for j, chip in enumerate(chips):
                copy(t, 4 + j, index(chip, 1 - c), sibling, False).wait_recv()
        for cp in sends:
            cp.wait_send()
        for cp in local:
            cp.wait()

    def run(self, name, by_chip=False):
        n = len(self.arrays)

        def body(*refs):
            src, dst, sems = refs[:n], refs[n:2 * n], refs[2 * n:]
            if by_chip:
                self.gather_by_chip(src, dst, sems)
                return
            self.start(src, dst, sems)
            self.wait(src, dst, sems)

        return pl.pallas_call(
            body, name=name, out_shape=self.out_shapes, in_specs=[_ANY] * n, out_specs=[_ANY] * n,
            scratch_shapes=self.scratch,
        )(*self.arrays)


SMALL = [("ln_g", (DEPTH, D), False), ("ln_b", (DEPTH, D), False), ("hgrn_lb_logits", (DEPTH, TOK), False),
         ("a_w_s", (1, NH, HD, HD), False), ("a_b_s", (1, NH, HD), False), ("b_norm_g", (1, TOK), True),
         ("c_w_pool", (1, 4, POOL_GROUP, POOL_GROUP), False), ("c_scale", (1, TOK), True),
         ("d_conv_w", (1, 4, TOK), True), ("d_conv_b", (1, TOK), True),
         ("d_w_gx", (1, NH, HD, HD), False), ("d_b_gx", (1, NH, HD), False),
         ("d_w_ga", (1, NH, HD, HD), False), ("d_b_ga", (1, NH, HD), False), ("d_a_param", (1, TOK), True)]


def _pack(parts, total_rows):
    flat = jnp.concatenate([p.reshape(-1).astype(F32) for p in parts])
    flat = jnp.pad(flat, (0, total_rows * 128 - flat.shape[0]))
    return flat.reshape(total_rows, 128)


def _size(shape):
    n = 1
    for s in shape:
        n *= s
    return n


def _rows_for(n):
    return -(-n // 1024) * 8


def kernel(x, mem, mem_kv_w, ln_g, ln_b, w_out, hgrn_lb_logits, a_w_in, a_w_s, a_b_s, b_w_in, b_norm_g, c_w_in, c_w_pool, c_scale, d_w_in, d_conv_w, d_conv_b, d_w_gx, d_b_gx, d_w_ga, d_b_ga, d_a_param, loss_target, m_mem_kv_w, m_ln_g, m_ln_b, m_w_out, m_hgrn_lb_logits, m_a_w_in, m_a_w_s, m_a_b_s, m_b_w_in, m_b_norm_g, m_c_w_in, m_c_w_pool, m_c_scale, m_d_w_in, m_d_conv_w, m_d_conv_b, m_d_w_gx, m_d_b_gx, m_d_w_ga, m_d_b_ga, m_d_a_param, v_mem_kv_w, v_ln_g, v_ln_b, v_w_out, v_hgrn_lb_logits, v_a_w_in, v_a_w_s, v_a_b_s, v_b_w_in, v_b_norm_g, v_c_w_in, v_c_w_pool, v_c_scale, v_d_w_in, v_d_conv_w, v_d_conv_b, v_d_w_gx, v_d_b_gx, v_d_w_ga, v_d_b_ga, v_d_a_param):
    W = dict(mem_kv_w=mem_kv_w, ln_g=ln_g, ln_b=ln_b, w_out=w_out, hgrn_lb_logits=hgrn_lb_logits, a_w_in=a_w_in, a_w_s=a_w_s,
             a_b_s=a_b_s, b_w_in=b_w_in, b_norm_g=b_norm_g, c_w_in=c_w_in, c_w_pool=c_w_pool, c_scale=c_scale, d_w_in=d_w_in,
             d_conv_w=d_conv_w, d_conv_b=d_conv_b, d_w_gx=d_w_gx, d_b_gx=d_b_gx, d_w_ga=d_w_ga, d_b_ga=d_b_ga, d_a_param=d_a_param)
    M = dict(mem_kv_w=m_mem_kv_w, ln_g=m_ln_g, ln_b=m_ln_b, w_out=m_w_out, hgrn_lb_logits=m_hgrn_lb_logits, a_w_in=m_a_w_in,
             a_w_s=m_a_w_s, a_b_s=m_a_b_s, b_w_in=m_b_w_in, b_norm_g=m_b_norm_g, c_w_in=m_c_w_in, c_w_pool=m_c_w_pool,
             c_scale=m_c_scale, d_w_in=m_d_w_in, d_conv_w=m_d_conv_w, d_conv_b=m_d_conv_b, d_w_gx=m_d_w_gx, d_b_gx=m_d_b_gx,
             d_w_ga=m_d_w_ga, d_b_ga=m_d_b_ga, d_a_param=m_d_a_param)
    V = dict(mem_kv_w=v_mem_kv_w, ln_g=v_ln_g, ln_b=v_ln_b, w_out=v_w_out, hgrn_lb_logits=v_hgrn_lb_logits, a_w_in=v_a_w_in,
             a_w_s=v_a_w_s, a_b_s=v_a_b_s, b_w_in=v_b_w_in, b_norm_g=v_b_norm_g, c_w_in=v_c_w_in, c_w_pool=v_c_w_pool,
             c_scale=v_c_scale, d_w_in=v_d_w_in, d_conv_w=v_d_conv_w, d_conv_b=v_d_conv_b, d_w_gx=v_d_w_gx, d_b_gx=v_d_b_gx,
             d_w_ga=v_d_w_ga, d_b_ga=v_d_b_ga, d_a_param=v_d_a_param)
    me = 4 * lax.axis_index("x") + 2 * lax.axis_index("y") + lax.axis_index("c")
    x2, mem2, tgt2 = x[0], mem[0], loss_target[0]
    in_names = ["a_w_in", "b_w_in", "c_w_in", "d_w_in"]

    shard_names = [n for n, _, sh in SMALL if sh]
    small_shard = _pack([W[n] for n in shard_names], 8)
    wts = _prep_weights([W[n][0] for n in in_names], w_out, mem_kv_w)
    wt_sh, wo_sh, wkv_sh = wts[:4], wts[4:8], wts[8]
    g0 = _Exchange([wt_sh[0], wo_sh[0], wkv_sh, small_shard], [False] * 4).run("gather_first", by_chip=True)
    wt_full = [g0[0].reshape(-1, D)]
    wout_full = [g0[1].reshape(D, D)]
    wkv_full = g0[2].reshape(D, 2 * XW)
    sm = g0[3].reshape(NDEV, 1024)
    full_small = {}
    off = 0
    for n, shape, _ in [s for s in SMALL if s[2]]:
        per = _size(shape) // NDEV
        blk = sm[:, off:off + per]
        if n == "d_conv_w":
            full_small[n] = blk.reshape(NDEV, 4, TOK // NDEV).transpose(1, 0, 2).reshape(4, TOK)
        else:
            full_small[n] = blk.reshape(1, TOK)
        off += per

    ks, vs = _kv_fwd(mem2, wkv_full)
    tri_bs = jnp.broadcast_to(a_b_s[0][:, :, None], (NH, HD, HD))
    wbd = jnp.zeros((TOK, TOK), F32)
    for g in range(4):
        wbd = lax.dynamic_update_slice(wbd, c_w_pool[0, g], (g * POOL_GROUP, g * POOL_GROUP))
    wbd = wbd.astype(MXU)
    prm = {0: [a_w_s[0], tri_bs],
           1: [hgrn_lb_logits, full_small["b_norm_g"]],
           2: [wbd, full_small["c_scale"]],
           3: [full_small["d_conv_w"], full_small["d_conv_b"], d_w_gx[0].astype(MXU), d_b_gx[0].reshape(1, TOK),
               d_w_ga[0].astype(MXU), d_b_ga[0].reshape(1, TOK), full_small["d_a_param"]]}
    ones = jnp.ones((1, D), F32)
    zeros = jnp.zeros((1, D), F32)
    xs, gs, bs = [x2], [ones], [zeros]
    saved = []
    for i in range(DEPTH):
        ride = _Exchange([wt_sh[i + 1], wo_sh[i + 1]], [False, False]) if i + 1 < DEPTH else None
        res = _layer_fwd(i, i, xs[i], gs[i], bs[i], wt_full[i], wout_full[i], ks, vs, prm[i], ride)
        if ride:
            wt_full.append(res[-2].reshape(-1, D))
            wout_full.append(res[-1].reshape(D, D))
            res = res[:-2]
        saved.append(res)
        xs.append(res[1])
        gs.append(ln_g[i:i + 1])
        bs.append(ln_b[i:i + 1])

    up = tgt2
    grads = {}
    dks_l, dvs_l, dwt_l, dwout_l, dlng_l, dlnb_l = [], [], [], [], [], []
    recv_wt, recv_wo = [None] * DEPTH, [None] * DEPTH
    loss_part = None
    sharded = {n for n, _, sh in SMALL if sh}
    group = {3: ["ln_g#3", "ln_b#3", "d_conv_w", "d_conv_b", "d_w_gx", "d_b_gx", "d_w_ga", "d_b_ga", "d_a_param"],
             2: ["ln_g#2", "ln_b#2", "c_w_pool", "c_scale"],
             1: ["ln_g#1", "ln_b#1", "hgrn_lb_logits", "b_norm_g"],
             0: ["ln_g#0", "ln_b#0", "a_w_s", "a_b_s", "loss"]}

    def small_of(l):
        return [grads[e].reshape(-1, grads[e].shape[-1]) for e in group[l]]

    recv_small = [None] * DEPTH
    for i in reversed(range(DEPTH)):
        res = saved[i]
        extra = res[4] if len(res) > 4 else None
        ride = None
        if i + 1 < DEPTH:
            smalls = small_of(i + 1)
            ride = _Exchange([dwt_l[-1], dwout_l[-1]] + smalls, [True, True] + [False] * len(smalls))
        out = _layer_bwd(i, i, up, i == DEPTH - 1, res[1], res[2], res[3], gs[i + 1], bs[i + 1], res[0], wout_full[i], ks, vs,
                         prm[i], extra, ride)
        if ride:
            nr = len(ride.arrays)
            recv_wt[i + 1], recv_wo[i + 1], recv_small[i + 1] = out[-nr], out[-nr + 1], out[-nr + 2:]
            out = out[:-nr]
        dres, dproj, dwout_i, dks_i, dvs_i, dg_i, db_i, loss_i = out[:8]
        pg = out[8:]
        if i == DEPTH - 1:
            grads["loss"] = loss_i
        dks_l.append(dks_i)
        dvs_l.append(dvs_i)
        dwout_l.append(dwout_i)
        grads[f"ln_g#{i}"], grads[f"ln_b#{i}"] = dg_i, db_i
        if i == 0:
            grads["a_w_s"], dbs_exp = pg
            grads["a_b_s"] = _bias_finalize(dbs_exp)
        elif i == 1:
            dlb, grads["b_norm_g"] = pg
            grads["hgrn_lb_logits"] = _lb_finalize(dlb, hgrn_lb_logits)
        elif i == 2:
            dwbd, grads["c_scale"] = pg
            grads["c_w_pool"] = jnp.stack([lax.dynamic_slice(dwbd, (g * POOL_GROUP, g * POOL_GROUP), (POOL_GROUP, POOL_GROUP))
                                           for g in range(4)])
        else:
            (grads["d_conv_w"], grads["d_conv_b"], grads["d_w_gx"], grads["d_b_gx"], grads["d_w_ga"], grads["d_b_ga"],
             grads["d_a_param"]) = pg
        ride = None
        if i == 0:
            dwkv = _kv_bwd(mem2, dks_l, dvs_l)
            smalls = small_of(0)
            ride = _Exchange([dwout_i, dwkv] + smalls, [True, True] + [False] * len(smalls))
        pb = _proj_bwd(i, dproj, dres, xs[i], gs[i], bs[i], wt_full[i], ride)
        up, dwt = pb[:2]
        if ride:
            recv_wo[0], recv_kv, recv_small[0] = pb[2], pb[3], pb[4:]
        dwt_l.append(dwt)
    grad_x = up[None]

    recv_wt[0], = _Exchange([dwt_l[-1]], [True]).run("scatter_last")

    outs = {}
    for t, n in enumerate(in_names):
        g, d, mn, vn = _sum_adam(f"adam_{n}", recv_wt[t], W[n][0], M[n][0], V[n][0], True)
        outs[n] = (g[None], d[None], mn[None], vn[None])
    wo_res = [_sum_adam(f"adam_w_out{l}", recv_wo[l], w_out[l], m_w_out[l], v_w_out[l], False) for l in range(DEPTH)]
    outs["w_out"] = tuple(jnp.stack([wo_res[l][j] for l in range(DEPTH)]) for j in range(4))
    outs["mem_kv_w"] = _sum_adam("adam_mem_kv_w", recv_kv, mem_kv_w, m_mem_kv_w, v_mem_kv_w, False)

    def entry_of(tree, e, like):
        if "#" in e:
            n, l = e.split("#")
            return tree[n][int(l):int(l) + 1]
        return tree[e].reshape(like.shape[1:])

    small = [{}, {}, {}, {}]
    for l in range(DEPTH):
        params = [None if (e == "loss" or e in sharded) else tuple(entry_of(t, e, r) for t in (W, M, V))
                  for e, r in zip(group[l], recv_small[l])]
        for e, res in zip(group[l], _group_adam(f"small_adam{l}", recv_small[l], params)):
            for j, a in enumerate(res):
                small[j][e] = a
    loss = small[0]["loss"][0, 0]
    for j in range(4):
        for n in ("ln_g", "ln_b"):
            small[j][n] = jnp.concatenate([small[j][f"{n}#{l}"] for l in range(DEPTH)], axis=0)
    g_small = small[0]
    for n, _, sh in SMALL:
        if not sh:
            outs[n] = tuple(small[j][n] for j in range(4))
    per = TOK // NDEV
    g_sh = {n: lax.dynamic_slice_in_dim(g_small[n], me * per, per, axis=1) for n, s, sh in SMALL if sh}
    gp = _pack([g_sh[n] for n in shard_names], 8)
    d_p, m_p, v_p = _adam_only(gp, small_shard, _pack([M[n] for n in shard_names], 8), _pack([V[n] for n in shard_names], 8))
    o = 0
    for n in shard_names:
        cnt = _size(W[n].shape)
        outs[n] = (g_sh[n],) + tuple(t.reshape(-1)[o:o + cnt].reshape(W[n].shape) for t in (d_p, m_p, v_p))
        o += cnt

    order = ["mem_kv_w", "ln_g", "ln_b", "w_out", "hgrn_lb_logits", "a_w_in", "a_w_s", "a_b_s", "b_w_in", "b_norm_g", "c_w_in",
             "c_w_pool", "c_scale", "d_w_in", "d_conv_w", "d_conv_b", "d_w_gx", "d_b_gx", "d_w_ga", "d_b_ga", "d_a_param"]
    result = [loss, grad_x]
    for j in range(4):
        result += [outs[n][j].reshape(W[n].shape) for n in order]
    return tuple(result)
```

```python
import functools

import jax
import jax.numpy as jnp
from jax import lax
from jax.experimental import pallas as pl
from jax.experimental.pallas import tpu as pltpu

F32 = jnp.float32
MXU = jnp.bfloat16
WIRE = jnp.bfloat16

D = 1024
TOK = 768
XW = 256
NMEM = 256
XHEADS = 4
XSCALE = 64 ** -0.5
NH = 6
HD = 128
CH = 16
POOL_WINDOWS = (2, 4, 8, 16)
POOL_GROUP = 192
DEPTH = 4
ALPHA = (2 * DEPTH) ** 0.25
LN_EPS = 1e-5
RMS_EPS = 1e-6
LRU_C = 8.0
B1, B2, LR, EPS, WD, STEP = 0.9, 0.999, 0.001, 1e-8, 0.01, 10

NDEV = 8
TS_FWD = {0: 512, 1: 256, 2: 512, 3: 256}
TS_BWD = {0: 512, 1: 256, 2: 512, 3: 256}
TSB = 512
VMEM_LIMIT = 58 * 1024 * 1024
VMEM_LIMIT_WIDE = 62 * 1024 * 1024

KIND_WIDTHS = {0: 2 * TOK + XW + D, 1: 3 * TOK + XW + D, 2: TOK + XW + D, 3: TOK + XW + D}


def _mm(a, b, ca, cb):
    return lax.dot_general(a.astype(MXU), b.astype(MXU), (((ca,), (cb,)), ((), ())), preferred_element_type=F32)


def _nn(a, b):
    return _mm(a, b, 1, 0)


def _nt(a, b):
    return _mm(a, b, 1, 1)


def _tn(a, b):
    return _mm(a, b, 0, 0)


def _bmm(a, b, ca, cb):
    return lax.dot_general(a.astype(MXU), b.astype(MXU), (((ca,), (cb,)), ((0,), (0,))), preferred_element_type=F32)


def _sigmoid(x):
    return 1.0 / (1.0 + jnp.exp(-x))


def _vjp1(fn, x, dy):
    return jax.vjp(fn, x)[1](dy)[0]


def _rowsum(x):
    return jnp.sum(x, axis=0, keepdims=True)


def _row(x, r):
    sel = lax.broadcasted_iota(jnp.int32, x.shape, 0) == r
    return jnp.sum(jnp.where(sel, x, 0.0), axis=0, keepdims=True)


def _acc(ref, val):
    ref[...] += val


def _cparams(sem=None, vmem=VMEM_LIMIT):
    return pltpu.CompilerParams(dimension_semantics=sem, vmem_limit_bytes=vmem)


def _res(a):
    nd = a.ndim
    return pl.BlockSpec(a.shape, lambda i: (0,) * nd)


def _res_sds(shape):
    nd = len(shape)
    return pl.BlockSpec(shape, lambda i: (0,) * nd)


def _row_spec(width, nt, rev, ts):
    if rev:
        return pl.BlockSpec((ts, width), lambda i: (nt - 1 - i, 0))
    return pl.BlockSpec((ts, width), lambda i: (i, 0))


def _call(body, name, grid, ins, outs, scratch=(), sem=("arbitrary",), vmem=VMEM_LIMIT):
    arrays = [a for a, _ in ins]
    return pl.pallas_call(
        body, name=name, grid=grid,
        in_specs=[s for _, s in ins],
        out_specs=[s for _, s in outs],
        out_shape=[o for o, _ in outs],
        scratch_shapes=list(scratch),
        compiler_params=_cparams(sem, vmem),
    )(*arrays)


def _xattn_fwd(qx, ks_ref, vs_ref):
    o = None
    ps = []
    for h in range(XHEADS):
        s = _nt(qx, ks_ref[h]) * XSCALE
        s = s - jnp.max(s, axis=-1, keepdims=True)
        e = jnp.exp(s)
        p = e * (1.0 / jnp.sum(e, axis=-1, keepdims=True))
        ps.append(p)
        oh = _nn(p, vs_ref[h])
        o = oh if o is None else o + oh
    return o, ps


def _xattn_bwd(qx, ps, dxo, ks_ref, vs_ref, dks_ref, dvs_ref):
    dq = None
    for h in range(XHEADS):
        p = ps[h]
        dp = _nt(dxo, vs_ref[h])
        ds = p * (dp - jnp.sum(dp * p, axis=-1, keepdims=True))
        dqh = _nn(ds, ks_ref[h]) * XSCALE
        dq = dqh if dq is None else dq + dqh
        dks_ref[h] += _tn(ds, qx) * XSCALE
        dvs_ref[h] += _tn(p, dxo)
    return dq


def _tril128():
    r = lax.broadcasted_iota(jnp.int32, (HD, HD), 0)
    c = lax.broadcasted_iota(jnp.int32, (HD, HD), 1)
    return c <= r


GELU_C = 0.7978845608028654
GELU_K = 0.044715


def _gelu(x):
    th = jnp.tanh(GELU_C * (x + GELU_K * (x * x * x)))
    return 0.5 * x * (1.0 + th), th


def _gelu_grad(x, th):
    return 0.5 * (1.0 + th) + 0.5 * x * (1.0 - th * th) * (GELU_C * (1.0 + 3.0 * GELU_K * (x * x)))


def _gmlp_fwd(u, v, ws_ref, bs_ref):
    ts = u.shape[0]
    ug, thu = _gelu(u)
    vg, thv = _gelu(v)
    tri = _tril128()
    toks, res = [], []
    for g in range(NH):
        sl = slice(g * HD, (g + 1) * HD)
        vgh = vg[:, sl]
        cen = vgh - jnp.mean(vgh, axis=-1, keepdims=True)
        rstd = lax.rsqrt(jnp.mean(cen * cen, axis=-1, keepdims=True) + LN_EPS)
        vn = cen * rstd
        w = jnp.where(tri, ws_ref[g], 0.0).astype(MXU)
        mix = jnp.concatenate([_nn(w, vn[n * HD:(n + 1) * HD]) + bs_ref[g] for n in range(ts // HD)], axis=0)
        toks.append(ug[:, sl] * mix)
        res.append((vn, rstd, mix, w))
    return jnp.concatenate(toks, axis=1), (ug, res, thu, thv)


def _gmlp_bwd(u, v, fres, dtok, dws_ref, dbs_ref):
    ts = u.shape[0]
    ug, res, thu, thv = fres
    tri = _tril128()
    dugs, dvgs = [], []
    for g in range(NH):
        sl = slice(g * HD, (g + 1) * HD)
        vn, rstd, mix, w = res[g]
        dmix = dtok[:, sl] * ug[:, sl]
        dugs.append(dtok[:, sl] * mix)
        dvn_rows = []
        dw = None
        dbs = None
        for n in range(ts // HD):
            dm = dmix[n * HD:(n + 1) * HD]
            dvn_rows.append(_tn(w, dm))
            t = _nt(dm, vn[n * HD:(n + 1) * HD])
            dw = t if dw is None else dw + t
            dbs = dm if dbs is None else dbs + dm
        dws_ref[g] += jnp.where(tri, dw, 0.0)
        dbs_ref[g] += dbs
        dvn = jnp.concatenate(dvn_rows, axis=0)
        dvgs.append(rstd * (dvn - jnp.mean(dvn, axis=-1, keepdims=True) - vn * jnp.mean(dvn * vn, axis=-1, keepdims=True)))
    du = jnp.concatenate(dugs, axis=1) * _gelu_grad(u, thu)
    dv = jnp.concatenate(dvgs, axis=1) * _gelu_grad(v, thv)
    return du, dv


def _chunk_cumsum(x):
    row = lax.broadcasted_iota(jnp.int32, x.shape, 0) % CH
    for s in (1, 2, 4, 8):
        x = x + jnp.where(row >= s, pltpu.roll(x, s, 0), 0.0)
    return x


def _chunk_revcumsum(x):
    n = x.shape[0]
    row = lax.broadcasted_iota(jnp.int32, x.shape, 0) % CH
    for s in (1, 2, 4, 8):
        x = x + jnp.where(row < CH - s, pltpu.roll(x, n - s, 0), 0.0)
    return x


def _chunk_sum(x):
    n, w = x.shape
    return jnp.sum(x.reshape(n // CH, CH, w), axis=1)


def _chunk_bcast(c, n):
    nch, w = c.shape
    return jnp.broadcast_to(c[:, None, :], (nch, CH, w)).reshape(n, w)


def _lower_bound(lb_logits, layer):
    lg = lb_logits
    e = jnp.exp(lg - jnp.max(lg, axis=0, keepdims=True))
    p = e / jnp.sum(e, axis=0, keepdims=True)
    row = lax.broadcasted_iota(jnp.int32, p.shape, 0)
    lb = jnp.sum(jnp.where((row >= 1) & (row <= layer), p, 0.0), axis=0, keepdims=True)
    return lb, p


def _hgrn_prep(q, fl, lb):
    n = q.shape[0]
    sg = _sigmoid(fl)
    f = lb + (1.0 - lb) * sg
    lf = jnp.log(f)
    k = 1.0 - f
    sq = _sigmoid(q)
    qf = q * sq
    g = _chunk_cumsum(lf)
    tot = _chunk_sum(lf)
    gl = _chunk_bcast(tot, n)
    eg = jnp.exp(g)
    eng = jnp.exp(-g)
    egl = jnp.exp(gl - g)
    return dict(sg=sg, f=f, k=k, sq=sq, qf=qf, eg=eg, eng=eng, egl=egl,
                qd=qf * eg, ki=k * eng, ke=k * egl, dch=jnp.exp(tot))


def _hgrn_mask():
    r = lax.broadcasted_iota(jnp.int32, (HD, HD), 0)
    c = lax.broadcasted_iota(jnp.int32, (HD, HD), 1)
    return (r // CH == c // CH) & (c <= r)


def _hgrn_states(v3, ke3, dch_h, st_in):
    nch = v3.shape[0]
    ut = _bmm(v3, ke3, 1, 1)
    dfull = jnp.broadcast_to(dch_h[:, None, :], (nch, HD, HD))
    st, sts = st_in, []
    for c in range(nch):
        sts.append(st)
        st = st * dfull[c] + ut[c]
    return jnp.stack(sts), st, dfull


def _hgrn_fwd(q, fl, inp, lb, ng, st_ref):
    n = q.shape[0]
    nch = n // CH
    pr = _hgrn_prep(q, fl, lb)
    mask = _hgrn_mask()
    toks = []
    for h in range(NH):
        sl = slice(h * HD, (h + 1) * HD)
        qd, ki, ke, v = pr["qd"][:, sl], pr["ki"][:, sl], pr["ke"][:, sl], inp[:, sl]
        qd3 = qd.astype(MXU).reshape(nch, CH, HD)
        v3 = v.astype(MXU).reshape(nch, CH, HD)
        ke3 = ke.astype(MXU).reshape(nch, CH, HD)
        sts, st_ref[h], _ = _hgrn_states(v3, ke3, pr["dch"][:, sl], st_ref[h])
        o = _bmm(qd3, sts, 2, 2).reshape(n, HD)
        intra = []
        for b in range(n // HD):
            bs = slice(b * HD, (b + 1) * HD)
            a = jnp.where(mask, _nt(qd[bs], ki[bs]), 0.0)
            intra.append(_nn(a, v[bs]))
        o = o + jnp.concatenate(intra, axis=0)
        r = lax.rsqrt(jnp.mean(o * o, axis=-1, keepdims=True) + RMS_EPS)
        toks.append(o * r * ng[:, sl])
    return jnp.concatenate(toks, axis=1)


def _hgrn_bwd(q, fl, inp, lb, ng, dtok, ststart_ref, dst_ref, dng_ref, dlb_ref):
    n = q.shape[0]
    nch = n // CH
    pr = _hgrn_prep(q, fl, lb)
    mask = _hgrn_mask()
    dqd_l, dki_l, dke_l, dv_l, ddch_l, dng_l, toks = [], [], [], [], [], [], []
    for h in range(NH):
        sl = slice(h * HD, (h + 1) * HD)
        qd, ki, ke, v = pr["qd"][:, sl], pr["ki"][:, sl], pr["ke"][:, sl], inp[:, sl]
        qd3 = qd.astype(MXU).reshape(nch, CH, HD)
        v3 = v.astype(MXU).reshape(nch, CH, HD)
        ke3 = ke.astype(MXU).reshape(nch, CH, HD)
        sts, _, dfull = _hgrn_states(v3, ke3, pr["dch"][:, sl], ststart_ref[h])
        o = _bmm(qd3, sts, 2, 2).reshape(n, HD)
        a_l = []
        intra = []
        for b in range(n // HD):
            bs = slice(b * HD, (b + 1) * HD)
            a = jnp.where(mask, _nt(qd[bs], ki[bs]), 0.0)
            a_l.append(a)
            intra.append(_nn(a, v[bs]))
        o = o + jnp.concatenate(intra, axis=0)
        r = lax.rsqrt(jnp.mean(o * o, axis=-1, keepdims=True) + RMS_EPS)
        toks.append(o * r * ng[:, sl])
        dt = dtok[:, sl]
        dng_l.append(_rowsum(dt * o * r))
        dn = dt * ng[:, sl]
        do = r * dn - o * (r * r * r) * jnp.mean(dn * o, axis=-1, keepdims=True)
        do3 = do.astype(MXU).reshape(nch, CH, HD)
        dqd_rows, dki_rows, dv_rows = [], [], []
        for b in range(n // HD):
            bs = slice(b * HD, (b + 1) * HD)
            da = jnp.where(mask, _nt(do[bs], v[bs]), 0.0)
            dqd_rows.append(_nn(da, ki[bs]))
            dki_rows.append(_tn(da, qd[bs]))
            dv_rows.append(_tn(a_l[b], do[bs]))
        dqd = jnp.concatenate(dqd_rows, axis=0) + _bmm(do3, sts, 2, 1).reshape(n, HD)
        dki = jnp.concatenate(dki_rows, axis=0)
        dv = jnp.concatenate(dv_rows, axis=0)
        wt = _bmm(do3, qd3, 1, 1)
        dst, dstn_l = dst_ref[h], [None] * nch
        for c in reversed(range(nch)):
            dstn_l[c] = dst
            dst = wt[c] + dst * dfull[c]
        dst_ref[h] = dst
        dstn = jnp.stack(dstn_l)
        dv = dv + _bmm(ke3, dstn, 2, 2).reshape(n, HD)
        dke = _bmm(v3, dstn, 2, 1).reshape(n, HD)
        ddch_l.append(jnp.sum(sts * dstn, axis=1))
        dqd_l.append(dqd)
        dki_l.append(dki)
        dke_l.append(dke)
        dv_l.append(dv)
    dqd = jnp.concatenate(dqd_l, axis=1)
    dki = jnp.concatenate(dki_l, axis=1)
    dke = jnp.concatenate(dke_l, axis=1)
    dinp = jnp.concatenate(dv_l, axis=1)
    ddch = jnp.concatenate(ddch_l, axis=1)
    _acc(dng_ref, jnp.concatenate(dng_l, axis=1))
    dqf = dqd * pr["eg"]
    dke_ke = dke * pr["ke"]
    dg = dqd * pr["qd"] - dki * pr["ki"] - dke_ke
    dk = dki * pr["eng"] + dke * pr["egl"]
    dgl = _chunk_sum(dke_ke) + ddch * pr["dch"]
    dlf = _chunk_revcumsum(dg) + _chunk_bcast(dgl, n)
    df = dlf / pr["f"] - dk
    sg = pr["sg"]
    dfl = df * (1.0 - lb) * sg * (1.0 - sg)
    _acc(dlb_ref, _rowsum(df * (1.0 - sg)))
    sq = pr["sq"]
    dq = dqf * (sq * (1.0 + q * (1.0 - sq)))
    return jnp.concatenate(toks, axis=1), dq, dfl, dinp


def _pool_select(s2, s4, s8, s16):
    col = lax.broadcasted_iota(jnp.int32, (1, TOK), 1)
    return jnp.where(col < POOL_GROUP, s2, jnp.where(col < 2 * POOL_GROUP, s4, jnp.where(col < 3 * POOL_GROUP, s8, s16)))


def _pool_cnt(pos0, n):
    pos = pos0 + lax.broadcasted_iota(jnp.int32, (n, TOK), 0) + 1
    col = lax.broadcasted_iota(jnp.int32, (n, TOK), 1)
    w = jnp.where(col < POOL_GROUP, 2, jnp.where(col < 2 * POOL_GROUP, 4, jnp.where(col < 3 * POOL_GROUP, 8, 16)))
    return jnp.minimum(pos, w).astype(F32)


def _pool_fwd(p, halo, pos0, wbd, scale):
    n = p.shape[0]
    ext = jnp.concatenate([halo, p], axis=0)
    s2 = ext + pltpu.roll(ext, 1, 0)
    s4 = s2 + pltpu.roll(s2, 2, 0)
    s8 = s4 + pltpu.roll(s4, 4, 0)
    s16 = s8 + pltpu.roll(s8, 8, 0)
    win = _pool_select(s2, s4, s8, s16)[16:]
    cnt = _pool_cnt(pos0, n)
    diff = win / cnt - p
    y = _nn(diff, wbd)
    return y * scale, (diff, y, cnt)


def _pool_bwd(fres, dtok, nxt_ref, wbd, scale, dwbd_ref, dscale_ref):
    diff, y, cnt = fres
    n = diff.shape[0]
    _acc(dscale_ref, _rowsum(dtok * y))
    dy = dtok * scale
    ddiff = _nt(dy, wbd)
    dwbd_ref[...] += _tn(diff, dy)
    qv = ddiff / cnt
    ext = jnp.concatenate([qv, nxt_ref[...]], axis=0)
    m = n + 16
    s2 = ext + pltpu.roll(ext, m - 1, 0)
    s4 = s2 + pltpu.roll(s2, m - 2, 0)
    s8 = s4 + pltpu.roll(s4, m - 4, 0)
    s16 = s8 + pltpu.roll(s8, m - 8, 0)
    adj = _pool_select(s2, s4, s8, s16)[:n]
    nxt_ref[...] = qv[:16]
    return adj - ddiff


def _neg_expm1(x):
    return jnp.where(jnp.abs(x) < 1e-2, -x * (1.0 + x * (0.5 + x * (1.0 / 6.0))), 1.0 - jnp.exp(x))


def _lru_gates(xc, zx, za, ap, first):
    gx = _sigmoid(zx)
    ga = _sigmoid(za)
    sp = jnp.maximum(-ap, 0.0) + jnp.log(1.0 + jnp.exp(-jnp.abs(ap)))
    log_a = -LRU_C * ga * sp
    a = jnp.exp(log_a)
    mult = jnp.sqrt(_neg_expm1(2.0 * log_a))
    mult = jnp.where(first, 1.0, mult)
    return a, mult * gx * xc, (gx, ga, sp, mult)


def _scan_fwd(a, b, h0):
    n = a.shape[0]
    row = lax.broadcasted_iota(jnp.int32, a.shape, 0)
    s = 1
    while s < n:
        keep = row >= s
        b = b + a * jnp.where(keep, pltpu.roll(b, s, 0), 0.0)
        a = a * jnp.where(keep, pltpu.roll(a, s, 0), 1.0)
        s *= 2
    return b + a * h0


def _scan_bwd(an, d, dh_next):
    n = an.shape[0]
    row = lax.broadcasted_iota(jnp.int32, an.shape, 0)
    s = 1
    while s < n:
        keep = row < n - s
        d = d + an * jnp.where(keep, pltpu.roll(d, n - s, 0), 0.0)
        an = an * jnp.where(keep, pltpu.roll(an, n - s, 0), 1.0)
        s *= 2
    return d + an * dh_next


def _lru_conv(xb, halo, cw_ref, cb):
    ext = jnp.concatenate([halo, xb], axis=0)
    sh = [pltpu.roll(ext, 3 - j, 0)[8:] if j < 3 else xb for j in range(4)]
    xc = cb
    for j in range(4):
        xc = xc + cw_ref[pl.ds(j, 1), :] * sh[j]
    return xc, sh


def _lru_fwd(xb, halo, pos0, prm, h0):
    cw, cb, wgx, bgx, wga, bga, ap = prm
    n = xb.shape[0]
    xc, sh = _lru_conv(xb, halo, cw, cb[...])
    zx = jnp.concatenate([_nn(xc[:, h * HD:(h + 1) * HD], wgx[h]) for h in range(NH)], axis=1) + bgx[...]
    za = jnp.concatenate([_nn(xc[:, h * HD:(h + 1) * HD], wga[h]) for h in range(NH)], axis=1) + bga[...]
    first = (pos0 + lax.broadcasted_iota(jnp.int32, (n, 1), 0)) == 0
    a, b, gates = _lru_gates(xc, zx, za, ap[...], first)
    hseq = _scan_fwd(a, b, h0)
    return hseq, (xc, sh, gates, first, a)


def _lru_bwd(fres, hseq, h0, dtok, prm, carry_refs, grad_refs):
    cw, cb, wgx, bgx, wga, bga, ap = prm
    xc, sh, (gx, ga, sp, mult), first, a = fres
    anext_ref, dhnext_ref, dxcnext_ref = carry_refs
    dcw_ref, dcb_ref, dwgx_ref, dbgx_ref, dwga_ref, dbga_ref, dap_ref = grad_refs
    n = xc.shape[0]
    an = jnp.where(lax.broadcasted_iota(jnp.int32, a.shape, 0) == n - 1, anext_ref[...], pltpu.roll(a, n - 1, 0))
    dh = _scan_bwd(an, dtok, dhnext_ref[...])
    hprev = jnp.where(lax.broadcasted_iota(jnp.int32, hseq.shape, 0) == 0, h0, pltpu.roll(hseq, 1, 0))
    da = dh * hprev
    anext_ref[...] = _row(a, 0)
    dhnext_ref[...] = _row(dh, 0)
    t = dh * xc
    dxc = dh * mult * gx
    dzx = t * mult * gx * (1.0 - gx)
    dlog_a = da * a - jnp.where(first, 0.0, t * gx * (a * a) / mult)
    dza = dlog_a * (-LRU_C * sp) * ga * (1.0 - ga)
    dap = _rowsum(dlog_a * ga) * (LRU_C * _sigmoid(-ap[...]))
    _acc(dap_ref, dap)
    _acc(dbgx_ref, _rowsum(dzx))
    _acc(dbga_ref, _rowsum(dza))
    parts = []
    for h in range(NH):
        sl = slice(h * HD, (h + 1) * HD)
        parts.append(_nt(dzx[:, sl], wgx[h]) + _nt(dza[:, sl], wga[h]))
        dwgx_ref[h] += _tn(xc[:, sl], dzx[:, sl])
        dwga_ref[h] += _tn(xc[:, sl], dza[:, sl])
    dxc = dxc + jnp.concatenate(parts, axis=1)
    _acc(dcb_ref, _rowsum(dxc))
    for j in range(4):
        dcw_ref[pl.ds(j, 1), :] += _rowsum(dxc * sh[j])
    ext = jnp.concatenate([dxc, dxcnext_ref[...]], axis=0)
    m = n + 8
    dxb = cw[pl.ds(3, 1), :] * dxc
    for j in range(3):
        dxb = dxb + cw[pl.ds(j, 1), :] * pltpu.roll(ext, m - (3 - j), 0)[:n]
    dxcnext_ref[...] = dxc[:8]
    return dxb


def _layer_fwd(kind, layer, xprev, gprev, bprev, wt, wout, ks, vs, prm, ride=None):
    TS = TS_FWD[kind]
    _rows = functools.partial(_row_spec, ts=TS)
    S = xprev.shape[0]
    nt = S // TS
    N = wt.shape[0]
    nprm = len(prm)
    nch = TS // CH

    outs = [(jax.ShapeDtypeStruct((S, N), F32), _rows(N, nt, False)),
            (jax.ShapeDtypeStruct((S, D), F32), _rows(D, nt, False)),
            (jax.ShapeDtypeStruct((S, 1), F32), _rows(1, nt, False)),
            (jax.ShapeDtypeStruct((S, XHEADS * NMEM), MXU), _rows(XHEADS * NMEM, nt, False))]
    scratch = []
    if kind == 1:
        outs.append((jax.ShapeDtypeStruct((nt, NH, HD, HD), F32), pl.BlockSpec((None, NH, HD, HD), lambda i: (i, 0, 0, 0))))
        scratch = [pltpu.VMEM((NH, HD, HD), F32)]
    elif kind == 2:
        scratch = [pltpu.VMEM((16, TOK), F32)]
    elif kind == 3:
        outs.append((jax.ShapeDtypeStruct((nt * 8, TOK), F32), pl.BlockSpec((8, TOK), lambda i: (i, 0))))
        scratch = [pltpu.VMEM((8, TOK), F32), pltpu.VMEM((1, TOK), F32)]
    nout = len(outs)
    nscr = len(scratch)
    nride = len(ride.arrays) if ride else 0

    def body(*refs):
        x_ref, g_ref, b_ref, wt_ref, wout_ref, ks_ref, vs_ref = refs[:7]
        prm_refs = refs[7:7 + nprm]
        nin = 7 + nprm + nride
        ride_src = refs[7 + nprm:nin]
        out_refs = refs[nin:nin + nout]
        ride_dst = refs[nin + nout:nin + nout + nride]
        scr = refs[nin + nout + nride:nin + nout + nride + nscr]
        ride_sems = refs[nin + nout + nride + nscr:]
        proj_ref, xhat_ref, rstd_ref = out_refs[:3]
        i = pl.program_id(0)
        if ride:
            @pl.when(i == 0)
            def _():
                ride.start(ride_src, ride_dst, ride_sems)

        xin = x_ref[...] * g_ref[...] + b_ref[...]
        proj = _nt(xin, wt_ref[...])
        proj_ref[...] = proj
        if kind == 0:
            tok, _ = _gmlp_fwd(proj[:, :TOK], proj[:, TOK:2 * TOK], prm_refs[0], prm_refs[1])
        elif kind == 1:
            st_ref, = scr

            @pl.when(i == 0)
            def _():
                st_ref[...] = jnp.zeros_like(st_ref)

            out_refs[4][...] = st_ref[...]
            lb, _ = _lower_bound(prm_refs[0][...], layer)
            tok = _hgrn_fwd(proj[:, :TOK], proj[:, TOK:2 * TOK], proj[:, 2 * TOK:3 * TOK], lb, prm_refs[1][...],
                            st_ref)
        elif kind == 2:
            halo_ref, = scr

            @pl.when(i == 0)
            def _():
                halo_ref[...] = jnp.zeros_like(halo_ref)

            p = proj[:, :TOK]
            tok, _ = _pool_fwd(p, halo_ref[...], i * TS, prm_refs[0][...], prm_refs[1][...])
            halo_ref[...] = p[TS - 16:]
        else:
            halo_ref, h_ref = scr

            @pl.when(i == 0)
            def _():
                halo_ref[...] = jnp.zeros_like(halo_ref)
                h_ref[...] = jnp.zeros_like(h_ref)

            out_refs[4][...] = jnp.broadcast_to(h_ref[...], (8, TOK))
            xb = proj[:, :TOK]
            tok, _ = _lru_fwd(xb, halo_ref[...], i * TS, prm_refs, h_ref[...])
            halo_ref[...] = xb[TS - 8:]
            h_ref[...] = _row(tok, TS - 1)
        qx = proj[:, N - D - XW:N - D]
        gate = proj[:, N - D:]
        xo, ps = _xattn_fwd(qx, ks_ref, vs_ref)
        out_refs[3][...] = jnp.concatenate(ps, axis=1).astype(MXU)
        mixed = jnp.concatenate([tok, xo], axis=1) * (gate * _sigmoid(gate))
        z = ALPHA * xin + _nn(mixed, wout_ref[...])
        cen = z - jnp.mean(z, axis=-1, keepdims=True)
        rstd = lax.rsqrt(jnp.mean(cen * cen, axis=-1, keepdims=True) + LN_EPS)
        xhat_ref[...] = cen * rstd
        rstd_ref[...] = rstd
        if ride:
            @pl.when(i == nt - 1)
            def _():
                ride.wait(ride_src, ride_dst, ride_sems)

    ins = [(xprev, _rows(D, nt, False)), (gprev, _res(gprev)), (bprev, _res(bprev)), (wt, _res(wt)), (wout, _res(wout)),
           (ks, _res(ks)), (vs, _res(vs))] + [(p, _res(p)) for p in prm]
    if ride:
        ins += [(a, _ANY) for a in ride.arrays]
        outs += [(s, _ANY) for s in ride.out_shapes]
        scratch = scratch + ride.scratch
    return _call(body, f"layer{layer}_fwd", (nt,), ins, outs, scratch)


def _layer_bwd(kind, layer, up, is_last, xhat, rstd, probs, g_i, b_i, proj, wout, ks, vs, prm, extra, ride=None):
    TS = TS_BWD[kind]
    _rows = functools.partial(_row_spec, ts=TS)
    S = xhat.shape[0]
    nt = S // TS
    N = proj.shape[1]
    nprm = len(prm)
    nch = TS // CH

    ins = [(up, _rows(D, nt, True)), (xhat, _rows(D, nt, True)), (rstd, _rows(1, nt, True)), (g_i, _res(g_i)), (b_i, _res(b_i)),
           (proj, _rows(N, nt, True)), (wout, _res(wout)), (ks, _res(ks)), (vs, _res(vs)),
           (probs, _rows(XHEADS * NMEM, nt, True))] + [(p, _res(p)) for p in prm]
    nfixed = 10
    if kind == 1:
        ins.append((extra, pl.BlockSpec((None, NH, HD, HD), lambda i: (nt - 1 - i, 0, 0, 0))))
    elif kind == 2:
        hb = TS // 16
        ins.append((proj, pl.BlockSpec((16, TOK), lambda i: (jnp.maximum((nt - 1 - i) * hb - 1, 0), 0))))
    elif kind == 3:
        hb = TS // 8
        ins.append((proj, pl.BlockSpec((8, TOK), lambda i: (jnp.maximum((nt - 1 - i) * hb - 1, 0), 0))))
        ins.append((extra, pl.BlockSpec((8, TOK), lambda i: (nt - 1 - i, 0))))
    nin = len(ins)

    def acc(shape):
        return (jax.ShapeDtypeStruct(shape, F32), _res_sds(shape))

    outs = [(jax.ShapeDtypeStruct((S, D), F32), _rows(D, nt, True)),
            (jax.ShapeDtypeStruct((S, N), MXU), _rows(N, nt, True)),
            (jax.ShapeDtypeStruct((D, D), WIRE), _res_sds((D, D))),
            acc((XHEADS, NMEM, XW)), acc((XHEADS, NMEM, XW)), acc((1, D)), acc((1, D)), acc((1, HD))]
    scratch = []
    if kind == 0:
        outs += [acc((NH, HD, HD)), acc((NH, HD, HD))]
    elif kind == 1:
        outs += [acc((1, TOK)), acc((1, TOK))]
        scratch = [pltpu.VMEM((NH, HD, HD), F32)]
    elif kind == 2:
        outs += [acc((TOK, TOK)), acc((1, TOK))]
        scratch = [pltpu.VMEM((16, TOK), F32)]
    else:
        outs += [acc((4, TOK)), acc((1, TOK)), acc((NH, HD, HD)), acc((1, TOK)), acc((NH, HD, HD)), acc((1, TOK)), acc((1, TOK))]
        scratch = [pltpu.VMEM((1, TOK), F32), pltpu.VMEM((1, TOK), F32), pltpu.VMEM((8, TOK), F32)]
    scratch = scratch + [pltpu.VMEM((D, D), F32)]
    nout = len(outs)
    nscr = len(scratch)
    nride = len(ride.arrays) if ride else 0

    def body(*refs):
        up_ref, xhat_ref, rstd_ref, g_ref, b_ref, proj_ref, wout_ref, ks_ref, vs_ref, probs_ref = refs[:nfixed]
        prm_refs = refs[nfixed:nfixed + nprm]
        ext_refs = refs[nfixed + nprm:nin]
        ride_src = refs[nin:nin + nride]
        o0 = nin + nride
        out_refs = refs[o0:o0 + nout]
        ride_dst = refs[o0 + nout:o0 + nout + nride]
        scr = refs[o0 + nout + nride:o0 + nout + nride + nscr - 1]
        dwout_acc = refs[o0 + nout + nride + nscr - 1]
        ride_sems = refs[o0 + nout + nride + nscr:]
        dres_ref, dproj_ref, dwout_ref, dks_ref, dvs_ref, dg_ref, db_ref, loss_ref = out_refs[:8]
        pgrad = out_refs[8:]
        i = pl.program_id(0)
        tile = nt - 1 - i

        @pl.when(i == 0)
        def _():
            if ride:
                ride.start(ride_src, ride_dst, ride_sems)
            for r in out_refs[3:]:
                r[...] = jnp.zeros_like(r)
            dwout_acc[...] = jnp.zeros_like(dwout_acc)
            for r in scr:
                if kind != 1 or r is scr[0]:
                    r[...] = jnp.zeros_like(r)

        xhat_v = xhat_ref[...]
        if is_last:
            err = xhat_v * g_ref[...] + b_ref[...] - up_ref[...]
            dxo = err * (1.0 / D)
            loss_ref[...] += jnp.sum(0.5 * jnp.mean(err * err, axis=-1, keepdims=True), axis=0, keepdims=True)
        else:
            dxo = up_ref[...]
        _acc(dg_ref, _rowsum(dxo * xhat_v))
        _acc(db_ref, _rowsum(dxo))
        dxh = dxo * g_ref[...]
        dz = rstd_ref[...] * (dxh - jnp.mean(dxh, axis=-1, keepdims=True)
                              - xhat_v * jnp.mean(dxh * xhat_v, axis=-1, keepdims=True))
        dres_ref[...] = ALPHA * dz

        proj = proj_ref[...]
        qx = proj[:, N - D - XW:N - D]
        gate = proj[:, N - D:]
        sgate = _sigmoid(gate)
        silu = gate * sgate
        dmixed = _nt(dz, wout_ref[...])
        dcat = dmixed * silu
        dtok = dcat[:, :TOK]
        if kind == 0:
            u, v = proj[:, :TOK], proj[:, TOK:2 * TOK]
            tok, fres = _gmlp_fwd(u, v, prm_refs[0], prm_refs[1])
            du, dv = _gmlp_bwd(u, v, fres, dtok, pgrad[0], pgrad[1])
            dproj_ref[:, :TOK] = du.astype(MXU)
            dproj_ref[:, TOK:2 * TOK] = dv.astype(MXU)
        elif kind == 1:
            dst_ref, = scr
            lb, _ = _lower_bound(prm_refs[0][...], layer)
            q, fl, inp = proj[:, :TOK], proj[:, TOK:2 * TOK], proj[:, 2 * TOK:3 * TOK]
            tok, dq, dfl, dinp = _hgrn_bwd(q, fl, inp, lb, prm_refs[1][...], dtok, ext_refs[0], dst_ref, pgrad[1], pgrad[0])
            dproj_ref[:, :TOK] = dq.astype(MXU)
            dproj_ref[:, TOK:2 * TOK] = dfl.astype(MXU)
            dproj_ref[:, 2 * TOK:3 * TOK] = dinp.astype(MXU)
        elif kind == 2:
            p = proj[:, :TOK]
            halo = jnp.where(tile == 0, 0.0, ext_refs[0][...])
            tok, fres = _pool_fwd(p, halo, tile * TS, prm_refs[0][...], prm_refs[1][...])
            dp = _pool_bwd(fres, dtok, scr[0], prm_refs[0][...], prm_refs[1][...], pgrad[0], pgrad[1])
            dproj_ref[:, :TOK] = dp.astype(MXU)
        else:
            xb = proj[:, :TOK]
            halo = jnp.where(tile == 0, 0.0, ext_refs[0][...])
            h0 = ext_refs[1][0:1]
            tok, fres = _lru_fwd(xb, halo, tile * TS, prm_refs, h0)
            dxb = _lru_bwd(fres, tok, h0, dtok, prm_refs, scr, pgrad)
            dproj_ref[:, :TOK] = dxb.astype(MXU)
        ps = [probs_ref[:, h * NMEM:(h + 1) * NMEM].astype(F32) for h in range(XHEADS)]
        xo = _nn(ps[0], vs_ref[0])
        for h in range(1, XHEADS):
            xo = xo + _nn(ps[h], vs_ref[h])
        dqx = _xattn_bwd(qx, ps, dcat[:, TOK:], ks_ref, vs_ref, dks_ref, dvs_ref)
        cat = jnp.concatenate([tok, xo], axis=1)
        dwout_acc[...] += _tn(cat * silu, dz)
        dgate = dmixed * cat * (sgate * (1.0 + gate * (1.0 - sgate)))
        dproj_ref[:, N - D - XW:N - D] = dqx.astype(MXU)
        dproj_ref[:, N - D:] = dgate.astype(MXU)

        @pl.when(i == nt - 1)
        def _():
            dwout_ref[...] = dwout_acc[...].astype(WIRE)
            if ride:
                ride.wait(ride_src, ride_dst, ride_sems)

    if ride:
        ins += [(a, _ANY) for a in ride.arrays]
        outs += [(s, _ANY) for s in ride.out_shapes]
        scratch = scratch + ride.scratch
    return _call(body, f"layer{layer}_bwd", (nt,), ins, outs, scratch, vmem=VMEM_LIMIT_WIDE if kind == 0 else VMEM_LIMIT)


def _proj_bwd(layer, dproj, dres, xprev, gprev, bprev, wt, ride=None):
    S = xprev.shape[0]
    nt = S // TSB
    N = wt.shape[0]

    nride = len(ride.arrays) if ride else 0

    def body(*refs):
        dproj_ref, dres_ref, x_ref, g_ref, b_ref, wt_ref = refs[:6]
        ride_src = refs[6:6 + nride]
        dx_ref, dwt_ref = refs[6 + nride:8 + nride]
        ride_dst = refs[8 + nride:8 + 2 * nride]
        acc_ref = refs[8 + 2 * nride]
        ride_sems = refs[9 + 2 * nride:]

        @pl.when(pl.program_id(0) == 0)
        def _():
            if ride:
                ride.start(ride_src, ride_dst, ride_sems)
            acc_ref[...] = jnp.zeros_like(acc_ref)

        dp = dproj_ref[...]
        xin = x_ref[...] * g_ref[...] + b_ref[...]
        dx_ref[...] = dres_ref[...] + _nn(dp, wt_ref[...])
        acc_ref[...] += _tn(dp, xin)

        @pl.when(pl.program_id(0) == nt - 1)
        def _():
            dwt_ref[...] = acc_ref[...].astype(WIRE)
            if ride:
                ride.wait(ride_src, ride_dst, ride_sems)

    ins = [(dproj, _row_spec(N, nt, False, TSB)), (dres, _row_spec(D, nt, False, TSB)), (xprev, _row_spec(D, nt, False, TSB)),
           (gprev, _res(gprev)), (bprev, _res(bprev)), (wt, _res(wt))]
    outs = [(jax.ShapeDtypeStruct((S, D), F32), _row_spec(D, nt, False, TSB)),
            (jax.ShapeDtypeStruct((N, D), WIRE), _res_sds((N, D)))]
    scratch = [pltpu.VMEM((N, D), F32)]
    if ride:
        ins += [(a, _ANY) for a in ride.arrays]
        outs += [(s, _ANY) for s in ride.out_shapes]
        scratch = scratch + ride.scratch
    return _call(body, f"layer{layer}_projbwd", (nt,), ins, outs, scratch)


def _head_mask(h):
    col = lax.broadcasted_iota(jnp.int32, (1, XW), 1)
    return (col // 64) == h


def _kv_fwd(mem, wkv):
    def body(mem_ref, w_ref, ks_ref, vs_ref):
        kv = _nn(mem_ref[...], w_ref[...])
        k, v = kv[:, :XW], kv[:, XW:]
        for h in range(XHEADS):
            ks_ref[h] = jnp.where(_head_mask(h), k, 0.0).astype(MXU)
            vs_ref[h] = jnp.where(_head_mask(h), v, 0.0).astype(MXU)

    sds = jax.ShapeDtypeStruct((XHEADS, NMEM, XW), MXU)
    return pl.pallas_call(body, name="kv_fwd", out_shape=(sds, sds), compiler_params=_cparams())(mem, wkv)


def _kv_bwd(mem, dks_l, dvs_l):
    def body(mem_ref, *refs):
        dks_refs, dvs_refs, out_ref = refs[:DEPTH], refs[DEPTH:2 * DEPTH], refs[2 * DEPTH]
        dk = jnp.zeros((NMEM, XW), F32)
        dv = jnp.zeros((NMEM, XW), F32)
        for h in range(XHEADS):
            m = _head_mask(h)
            for l in range(DEPTH):
                dk = dk + jnp.where(m, dks_refs[l][h], 0.0)
                dv = dv + jnp.where(m, dvs_refs[l][h], 0.0)
        out_ref[...] = _tn(mem_ref[...], jnp.concatenate([dk, dv], axis=1)).astype(WIRE)

    return pl.pallas_call(body, name="kv_bwd", out_shape=jax.ShapeDtypeStruct((D, 2 * XW), WIRE),
                          compiler_params=_cparams())(mem, *dks_l, *dvs_l)


def _prep_weights(w_ins, w_out, wkv):
    def body(a_ref, b_ref, c_ref, d_ref, wo_ref, kv_ref, ao, bo, co, do, wo0, wo1, wo2, wo3, kvo):
        for src, dst in ((a_ref, ao), (b_ref, bo), (c_ref, co), (d_ref, do)):
            dst[...] = src[...].T.astype(MXU)
        for l, dst in enumerate((wo0, wo1, wo2, wo3)):
            dst[...] = wo_ref[l].astype(MXU)
        kvo[...] = kv_ref[...].astype(MXU)

    outs = [jax.ShapeDtypeStruct((w.shape[1], w.shape[0]), MXU) for w in w_ins]
    outs += [jax.ShapeDtypeStruct(w_out.shape[1:], MXU)] * DEPTH + [jax.ShapeDtypeStruct(wkv.shape, MXU)]
    return pl.pallas_call(body, name="prep_weights", out_shape=outs, compiler_params=_cparams())(*w_ins, w_out, wkv)


def _adam_math(w, g, m, v):
    m = B1 * m + (1.0 - B1) * g
    v = B2 * v + (1.0 - B2) * (g * g)
    m_hat = m / (1.0 - B1 ** STEP)
    v_hat = v / (1.0 - B2 ** STEP)
    delta = -LR * (m_hat / (jnp.sqrt(v_hat) + EPS) + WD * w)
    return delta, m, v


def _sum_adam(name, recv, w, m, v, transpose):
    rows, cols = recv.shape[1], recv.shape[2]

    def body(r_ref, w_ref, m_ref, v_ref, g_out, d_out, m_out, v_out, acc_ref):
        s = pl.program_id(0)

        @pl.when(s == 0)
        def _():
            acc_ref[...] = r_ref[...].astype(F32)

        @pl.when(s > 0)
        def _():
            acc_ref[...] += r_ref[...].astype(F32)

        @pl.when(s == NDEV - 1)
        def _():
            g = acc_ref[...].T if transpose else acc_ref[...]
            d, mn, vn = _adam_math(w_ref[...], g, m_ref[...], v_ref[...])
            g_out[...] = g
            d_out[...] = d
            m_out[...] = mn
            v_out[...] = vn

    sds = jax.ShapeDtypeStruct(w.shape, F32)
    ins = [(recv, pl.BlockSpec((None, rows, cols), lambda s: (s, 0, 0))), (w, _res(w)), (m, _res(m)), (v, _res(v))]
    outs = [(sds, _res_sds(w.shape))] * 4
    return _call(body, name, (NDEV,), ins, outs, [pltpu.VMEM((rows, cols), F32)])


def _bias_finalize(dbs_exp):
    def body(dbs_ref, dabs_ref):
        dabs_ref[...] = jnp.sum(dbs_ref[...], axis=-1)

    return pl.pallas_call(body, name="bias_finalize", out_shape=jax.ShapeDtypeStruct((NH, HD), F32),
                          compiler_params=_cparams())(dbs_exp)


def _lb_finalize(dlb, lb_logits):
    def body(dlb_ref, lg_ref, dlg_ref):
        total = jnp.zeros((DEPTH, TOK), F32)
        lg = lg_ref[...]
        e = jnp.exp(lg - jnp.max(lg, axis=0, keepdims=True))
        p = e / jnp.sum(e, axis=0, keepdims=True)
        row = lax.broadcasted_iota(jnp.int32, (DEPTH, TOK), 0)
        for layer in range(DEPTH):
            if layer % 4 != 1:
                continue
            dp = jnp.where((row >= 1) & (row <= layer), dlb_ref[...], 0.0)
            total = total + p * (dp - jnp.sum(p * dp, axis=0, keepdims=True))
        dlg_ref[...] = total

    return pl.pallas_call(body, name="lb_finalize", out_shape=jax.ShapeDtypeStruct((DEPTH, TOK), F32),
                          compiler_params=_cparams())(dlb, lb_logits)


def _small_sum_adam(name, gathered, w, m, v):
    rows = w.shape[0]

    def body(r_ref, w_ref, m_ref, v_ref, g_out, d_out, m_out, v_out):
        g = r_ref[0]
        for s in range(1, NDEV):
            g = g + r_ref[s]
        d, mn, vn = _adam_math(w_ref[...], g, m_ref[...], v_ref[...])
        g_out[...] = g
        d_out[...] = d
        m_out[...] = mn
        v_out[...] = vn

    sds = jax.ShapeDtypeStruct((rows, 128), F32)
    return pl.pallas_call(body, name=name, out_shape=(sds,) * 4, compiler_params=_cparams())(gathered, w, m, v)


def _group_adam(name, recvs, params):
    nk = len(recvs)

    def body(*refs):
        pos, oi = nk, nk + 3 * sum(p is not None for p in params)
        for k in range(nk):
            g = refs[k][0]
            for s in range(1, NDEV):
                g = g + refs[k][s]
            refs[oi][...] = g
            oi += 1
            if params[k] is not None:
                d, mn, vn = _adam_math(refs[pos][...], g, refs[pos + 1][...], refs[pos + 2][...])
                refs[oi][...] = d
                refs[oi + 1][...] = mn
                refs[oi + 2][...] = vn
                pos += 3
                oi += 3

    out_shape, counts = [], []
    for k in range(nk):
        counts.append(4 if params[k] is not None else 1)
        out_shape += [jax.ShapeDtypeStruct(recvs[k].shape[1:], F32)] * counts[-1]
    args = list(recvs) + [a for p in params if p is not None for a in p]
    flat = pl.pallas_call(body, name=name, out_shape=out_shape, compiler_params=_cparams())(*args)
    res, o = [], 0
    for cnt in counts:
        res.append(flat[o:o + cnt])
        o += cnt
    return res


def _adam_only(g, w, m, v):
    def body(g_ref, w_ref, m_ref, v_ref, d_out, m_out, v_out):
        d, mn, vn = _adam_math(w_ref[...], g_ref[...], m_ref[...], v_ref[...])
        d_out[...] = d
        m_out[...] = mn
        v_out[...] = vn

    sds = jax.ShapeDtypeStruct(w.shape, F32)
    return pl.pallas_call(body, name="shard_adam", out_shape=(sds,) * 3, compiler_params=_cparams())(g, w, m, v)


def _me_and_peers():
    x, y, c = lax.axis_index("x"), lax.axis_index("y"), lax.axis_index("c")
    me = 4 * x + 2 * y + c
    peers = []
    for k in range(1, NDEV):
        kx, ky, kc = (k >> 2) & 1, (k >> 1) & 1, k & 1
        px = x + kx - 2 * x * kx
        py = y + ky - 2 * y * ky
        pc = c + kc - 2 * c * kc
        peers.append(((px, py, pc), 4 * px + 2 * py + pc))
    return me, peers


_ANY = pl.BlockSpec(memory_space=pl.ANY)


class _Exchange:
    def __init__(self, arrays, split):
        self.arrays = list(arrays)
        self.split = list(split)
        n = len(self.arrays)
        self.out_shapes = []
        for a, sp in zip(self.arrays, self.split):
            rows = a.shape[0] // NDEV if sp else a.shape[0]
            self.out_shapes.append(jax.ShapeDtypeStruct((NDEV, rows, a.shape[1]), a.dtype))
        self.scratch = [pltpu.SemaphoreType.DMA((n, NDEV - 1)), pltpu.SemaphoreType.DMA((n, NDEV - 1)),
                        pltpu.SemaphoreType.DMA((n,))]

    def _block(self, src, t, d):
        if not self.split[t]:
            return src[t]
        rows = self.arrays[t].shape[0] // NDEV
        return src[t].at[pl.ds(d * rows, rows)]

    def start(self, src, dst, sems):
        send_sems, recv_sems, local_sems = sems
        me, peers = _me_and_peers()
        for t in range(len(self.arrays)):
            pltpu.make_async_copy(self._block(src, t, me), dst[t].at[me], local_sems.at[t]).start()
        for k, (dev, idx) in enumerate(peers):
            for t in range(len(self.arrays)):
                pltpu.make_async_remote_copy(src_ref=self._block(src, t, idx), dst_ref=dst[t].at[me],
                                             send_sem=send_sems.at[t, k], recv_sem=recv_sems.at[t, k],
                                             device_id=dev, device_id_type=pl.DeviceIdType.MESH).start()

    def wait(self, src, dst, sems):
        send_sems, recv_sems, local_sems = sems
        me, peers = _me_and_peers()

        def slot_copy(t, k, dev, idx):
            return pltpu.make_async_remote_copy(src_ref=dst[t].at[idx], dst_ref=dst[t].at[idx], send_sem=send_sems.at[t, k],
                                                recv_sem=recv_sems.at[t, k], device_id=dev,
                                                device_id_type=pl.DeviceIdType.MESH)

        for k, (dev, idx) in enumerate(peers):
            for t in range(len(self.arrays)):
                slot_copy(t, k, dev, idx).wait_recv()
        for k, (dev, idx) in enumerate(peers):
            for t in range(len(self.arrays)):
                slot_copy(t, k, dev, idx).wait_send()
        for t in range(len(self.arrays)):
            pltpu.make_async_copy(dst[t].at[me], dst[t].at[me], local_sems.at[t]).wait()

    def gather_by_chip(self, src, dst, sems):
        assert not any(self.split)
        send_sems, recv_sems, local_sems = sems
        n = len(self.arrays)
        x, y, c = lax.axis_index("x"), lax.axis_index("y"), lax.axis_index("c")
        me, sibling = 4 * x + 2 * y + c, (x, y, 1 - c)
        chips = [(1 - x, y), (x, 1 - y), (1 - x, 1 - y)]

        def index(chip, core):
            return 4 * chip[0] + 2 * chip[1] + core

        def copy(t, k, block, to, from_src):
            return pltpu.make_async_remote_copy(src_ref=src[t] if from_src else dst[t].at[block], dst_ref=dst[t].at[block],
                                                send_sem=send_sems.at[t, k], recv_sem=recv_sems.at[t, k],
                                                device_id=to, device_id_type=pl.DeviceIdType.MESH)

        local = [pltpu.make_async_copy(src[t], dst[t].at[me], local_sems.at[t]) for t in range(n)]
        for cp in local:
            cp.start()
        sends = []
        for t in range(n):
            sends.append(copy(t, 0, me, sibling, True))
            sends += [copy(t, 1 + j, me, (*chip, c), True) for j, chip in enumerate(chips)]
        for cp in sends:
            cp.start()
        for j, chip in enumerate(chips):
            for t in range(n):
                copy(t, 1 + j, index(chip, c), sibling, False).wait_recv()
                passed = copy(t, 4 + j, index(chip, c), sibling, False)
                passed.start()
                sends.append(passed)
        for t in range(n):
            copy(t, 0, index((x, y), 1 - c), sibling, False).wait_recv()
            for j, chip in enumerate(chips):
                copy(t, 4 + j, index(chip, 1 - c), sibling, False).wait_recv()
        for cp in sends:
            cp.wait_send()
        for cp in local:
            cp.wait()

    def run(self, name, by_chip=False):
        n = len(self.arrays)

        def body(*refs):
            src, dst, sems = refs[:n], refs[n:2 * n], refs[2 * n:]
            if by_chip:
                self.gather_by_chip(src, dst, sems)
                return
            self.start(src, dst, sems)
            self.wait(src, dst, sems)

        return pl.pallas_call(
            body, name=name, out_shape=self.out_shapes, in_specs=[_ANY] * n, out_specs=[_ANY] * n,
            scratch_shapes=self.scratch,
        )(*self.arrays)


SMALL = [("ln_g", (DEPTH, D), False), ("ln_b", (DEPTH, D), False), ("hgrn_lb_logits", (DEPTH, TOK), False),
         ("a_w_s", (1, NH, HD, HD), False), ("a_b_s", (1, NH, HD), False), ("b_norm_g", (1, TOK), True),
         ("c_w_pool", (1, 4, POOL_GROUP, POOL_GROUP), False), ("c_scale", (1, TOK), True),
         ("d_conv_w", (1, 4, TOK), True), ("d_conv_b", (1, TOK), True),
         ("d_w_gx", (1, NH, HD, HD), False), ("d_b_gx", (1, NH, HD), False),
         ("d_w_ga", (1, NH, HD, HD), False), ("d_b_ga", (1, NH, HD), False), ("d_a_param", (1, TOK), True)]


def _pack(parts, total_rows):
    flat = jnp.concatenate([p.reshape(-1).astype(F32) for p in parts])
    flat = jnp.pad(flat, (0, total_rows * 128 - flat.shape[0]))
    return flat.reshape(total_rows, 128)


def _size(shape):
    n = 1
    for s in shape:
        n *= s
    return n


def _rows_for(n):
    return -(-n // 1024) * 8


def kernel(x, mem, mem_kv_w, ln_g, ln_b, w_out, hgrn_lb_logits, a_w_in, a_w_s, a_b_s, b_w_in, b_norm_g, c_w_in, c_w_pool, c_scale, d_w_in, d_conv_w, d_conv_b, d_w_gx, d_b_gx, d_w_ga, d_b_ga, d_a_param, loss_target, m_mem_kv_w, m_ln_g, m_ln_b, m_w_out, m_hgrn_lb_logits, m_a_w_in, m_a_w_s, m_a_b_s, m_b_w_in, m_b_norm_g, m_c_w_in, m_c_w_pool, m_c_scale, m_d_w_in, m_d_conv_w, m_d_conv_b, m_d_w_gx, m_d_b_gx, m_d_w_ga, m_d_b_ga, m_d_a_param, v_mem_kv_w, v_ln_g, v_ln_b, v_w_out, v_hgrn_lb_logits, v_a_w_in, v_a_w_s, v_a_b_s, v_b_w_in, v_b_norm_g, v_c_w_in, v_c_w_pool, v_c_scale, v_d_w_in, v_d_conv_w, v_d_conv_b, v_d_w_gx, v_d_b_gx, v_d_w_ga, v_d_b_ga, v_d_a_param):
    W = dict(mem_kv_w=mem_kv_w, ln_g=ln_g, ln_b=ln_b, w_out=w_out, hgrn_lb_logits=hgrn_lb_logits, a_w_in=a_w_in, a_w_s=a_w_s,
             a_b_s=a_b_s, b_w_in=b_w_in, b_norm_g=b_norm_g, c_w_in=c_w_in, c_w_pool=c_w_pool, c_scale=c_scale, d_w_in=d_w_in,
             d_conv_w=d_conv_w, d_conv_b=d_conv_b, d_w_gx=d_w_gx, d_b_gx=d_b_gx, d_w_ga=d_w_ga, d_b_ga=d_b_ga, d_a_param=d_a_param)
    M = dict(mem_kv_w=m_mem_kv_w, ln_g=m_ln_g, ln_b=m_ln_b, w_out=m_w_out, hgrn_lb_logits=m_hgrn_lb_logits, a_w_in=m_a_w_in,
             a_w_s=m_a_w_s, a_b_s=m_a_b_s, b_w_in=m_b_w_in, b_norm_g=m_b_norm_g, c_w_in=m_c_w_in, c_w_pool=m_c_w_pool,
             c_scale=m_c_scale, d_w_in=m_d_w_in, d_conv_w=m_d_conv_w, d_conv_b=m_d_conv_b, d_w_gx=m_d_w_gx, d_b_gx=m_d_b_gx,
             d_w_ga=m_d_w_ga, d_b_ga=m_d_b_ga, d_a_param=m_d_a_param)
    V = dict(mem_kv_w=v_mem_kv_w, ln_g=v_ln_g, ln_b=v_ln_b, w_out=v_w_out, hgrn_lb_logits=v_hgrn_lb_logits, a_w_in=v_a_w_in,
             a_w_s=v_a_w_s, a_b_s=v_a_b_s, b_w_in=v_b_w_in, b_norm_g=v_b_norm_g, c_w_in=v_c_w_in, c_w_pool=v_c_w_pool,
             c_scale=v_c_scale, d_w_in=v_d_w_in, d_conv_w=v_d_conv_w, d_conv_b=v_d_conv_b, d_w_gx=v_d_w_gx, d_b_gx=v_d_b_gx,
             d_w_ga=v_d_w_ga, d_b_ga=v_d_b_ga, d_a_param=v_d_a_param)
    me = 4 * lax.axis_index("x") + 2 * lax.axis_index("y") + lax.axis_index("c")
    x2, mem2, tgt2 = x[0], mem[0], loss_target[0]
    in_names = ["a_w_in", "b_w_in", "c_w_in", "d_w_in"]

    shard_names = [n for n, _, sh in SMALL if sh]
    small_shard = _pack([W[n] for n in shard_names], 8)
    wts = _prep_weights([W[n][0] for n in in_names], w_out, mem_kv_w)
    wt_sh, wo_sh, wkv_sh = wts[:4], wts[4:8], wts[8]
    g0 = _Exchange([wt_sh[0], wo_sh[0], wkv_sh, small_shard], [False] * 4).run("gather_first", by_chip=True)
    wt_full = [g0[0].reshape(-1, D)]
    wout_full = [g0[1].reshape(D, D)]
    wkv_full = g0[2].reshape(D, 2 * XW)
    sm = g0[3].reshape(NDEV, 1024)
    full_small = {}
    off = 0
    for n, shape, _ in [s for s in SMALL if s[2]]:
        per = _size(shape) // NDEV
        blk = sm[:, off:off + per]
        if n == "d_conv_w":
            full_small[n] = blk.reshape(NDEV, 4, TOK // NDEV).transpose(1, 0, 2).reshape(4, TOK)
        else:
            full_small[n] = blk.reshape(1, TOK)
        off += per

    ks, vs = _kv_fwd(mem2, wkv_full)
    tri_bs = jnp.broadcast_to(a_b_s[0][:, :, None], (NH, HD, HD))
    wbd = jnp.zeros((TOK, TOK), F32)
    for g in range(4):
        wbd = lax.dynamic_update_slice(wbd, c_w_pool[0, g], (g * POOL_GROUP, g * POOL_GROUP))
    wbd = wbd.astype(MXU)
    prm = {0: [a_w_s[0], tri_bs],
           1: [hgrn_lb_logits, full_small["b_norm_g"]],
           2: [wbd, full_small["c_scale"]],
           3: [full_small["d_conv_w"], full_small["d_conv_b"], d_w_gx[0].astype(MXU), d_b_gx[0].reshape(1, TOK),
               d_w_ga[0].astype(MXU), d_b_ga[0].reshape(1, TOK), full_small["d_a_param"]]}
    ones = jnp.ones((1, D), F32)
    zeros = jnp.zeros((1, D), F32)
    xs, gs, bs = [x2], [ones], [zeros]
    saved = []
    for i in range(DEPTH):
        ride = _Exchange([wt_sh[i + 1], wo_sh[i + 1]], [False, False]) if i + 1 < DEPTH else None
        res = _layer_fwd(i, i, xs[i], gs[i], bs[i], wt_full[i], wout_full[i], ks, vs, prm[i], ride)
        if ride:
            wt_full.append(res[-2].reshape(-1, D))
            wout_full.append(res[-1].reshape(D, D))
            res = res[:-2]
        saved.append(res)
        xs.append(res[1])
        gs.append(ln_g[i:i + 1])
        bs.append(ln_b[i:i + 1])

    up = tgt2
    grads = {}
    dks_l, dvs_l, dwt_l, dwout_l, dlng_l, dlnb_l = [], [], [], [], [], []
    recv_wt, recv_wo = [None] * DEPTH, [None] * DEPTH
    loss_part = None
    sharded = {n for n, _, sh in SMALL if sh}
    group = {3: ["ln_g#3", "ln_b#3", "d_conv_w", "d_conv_b", "d_w_gx", "d_b_gx", "d_w_ga", "d_b_ga", "d_a_param"],
             2: ["ln_g#2", "ln_b#2", "c_w_pool", "c_scale"],
             1: ["ln_g#1", "ln_b#1", "hgrn_lb_logits", "b_norm_g"],
             0: ["ln_g#0", "ln_b#0", "a_w_s", "a_b_s", "loss"]}

    def small_of(l):
        return [grads[e].reshape(-1, grads[e].shape[-1]) for e in group[l]]

    recv_small = [None] * DEPTH
    for i in reversed(range(DEPTH)):
        res = saved[i]
        extra = res[4] if len(res) > 4 else None
        ride = None
        if i + 1 < DEPTH:
            smalls = small_of(i + 1)
            ride = _Exchange([dwt_l[-1], dwout_l[-1]] + smalls, [True, True] + [False] * len(smalls))
        out = _layer_bwd(i, i, up, i == DEPTH - 1, res[1], res[2], res[3], gs[i + 1], bs[i + 1], res[0], wout_full[i], ks, vs,
                         prm[i], extra, ride)
        if ride:
            nr = len(ride.arrays)
            recv_wt[i + 1], recv_wo[i + 1], recv_small[i + 1] = out[-nr], out[-nr + 1], out[-nr + 2:]
            out = out[:-nr]
        dres, dproj, dwout_i, dks_i, dvs_i, dg_i, db_i, loss_i = out[:8]
        pg = out[8:]
        if i == DEPTH - 1:
            grads["loss"] = loss_i
        dks_l.append(dks_i)
        dvs_l.append(dvs_i)
        dwout_l.append(dwout_i)
        grads[f"ln_g#{i}"], grads[f"ln_b#{i}"] = dg_i, db_i
        if i == 0:
            grads["a_w_s"], dbs_exp = pg
            grads["a_b_s"] = _bias_finalize(dbs_exp)
        elif i == 1:
            dlb, grads["b_norm_g"] = pg
            grads["hgrn_lb_logits"] = _lb_finalize(dlb, hgrn_lb_logits)
        elif i == 2:
            dwbd, grads["c_scale"] = pg
            grads["c_w_pool"] = jnp.stack([lax.dynamic_slice(dwbd, (g * POOL_GROUP, g * POOL_GROUP), (POOL_GROUP, POOL_GROUP))
                                           for g in range(4)])
        else:
            (grads["d_conv_w"], grads["d_conv_b"], grads["d_w_gx"], grads["d_b_gx"], grads["d_w_ga"], grads["d_b_ga"],
             grads["d_a_param"]) = pg
        ride = None
        if i == 0:
            dwkv = _kv_bwd(mem2, dks_l, dvs_l)
            smalls = small_of(0)
            ride = _Exchange([dwout_i, dwkv] + smalls, [True, True] + [False] * len(smalls))
        pb = _proj_bwd(i, dproj, dres, xs[i], gs[i], bs[i], wt_full[i], ride)
        up, dwt = pb[:2]
        if ride:
            recv_wo[0], recv_kv, recv_small[0] = pb[2], pb[3], pb[4:]
        dwt_l.append(dwt)
    grad_x = up[None]

    recv_wt[0], = _Exchange([dwt_l[-1]], [True]).run("scatter_last")

    outs = {}
    for t, n in enumerate(in_names):
        g, d, mn, vn = _sum_adam(f"adam_{n}", recv_wt[t], W[n][0], M[n][0], V[n][0], True)
        outs[n] = (g[None], d[None], mn[None], vn[None])
    wo_res = [_sum_adam(f"adam_w_out{l}", recv_wo[l], w_out[l], m_w_out[l], v_w_out[l], False) for l in range(DEPTH)]
    outs["w_out"] = tuple(jnp.stack([wo_res[l][j] for l in range(DEPTH)]) for j in range(4))
    outs["mem_kv_w"] = _sum_adam("adam_mem_kv_w", recv_kv, mem_kv_w, m_mem_kv_w, v_mem_kv_w, False)

    def entry_of(tree, e, like):
        if "#" in e:
            n, l = e.split("#")
            return tree[n][int(l):int(l) + 1]
        return tree[e].reshape(like.shape[1:])

    small = [{}, {}, {}, {}]
    for l in range(DEPTH):
        params = [None if (e == "loss" or e in sharded) else tuple(entry_of(t, e, r) for t in (W, M, V))
                  for e, r in zip(group[l], recv_small[l])]
        for e, res in zip(group[l], _group_adam(f"small_adam{l}", recv_small[l], params)):
            for j, a in enumerate(res):
                small[j][e] = a
    loss = small[0]["loss"][0, 0]
    for j in range(4):
        for n in ("ln_g", "ln_b"):
            small[j][n] = jnp.concatenate([small[j][f"{n}#{l}"] for l in range(DEPTH)], axis=0)
    g_small = small[0]
    for n, _, sh in SMALL:
        if not sh:
            outs[n] = tuple(small[j][n] for j in range(4))
    per = TOK // NDEV
    g_sh = {n: lax.dynamic_slice_in_dim(g_small[n], me * per, per, axis=1) for n, s, sh in SMALL if sh}
    gp = _pack([g_sh[n] for n in shard_names], 8)
    d_p, m_p, v_p = _adam_only(gp, small_shard, _pack([M[n] for n in shard_names], 8), _pack([V[n] for n in shard_names], 8))
    o = 0
    for n in shard_names:
        cnt = _size(W[n].shape)
        outs[n] = (g_sh[n],) + tuple(t.reshape(-1)[o:o + cnt].reshape(W[n].shape) for t in (d_p, m_p, v_p))
        o += cnt

    order = ["mem_kv_w", "ln_g", "ln_b", "w_out", "hgrn_lb_logits", "a_w_in", "a_w_s", "a_b_s", "b_w_in", "b_norm_g", "c_w_in",
             "c_w_pool", "c_scale", "d_w_in", "d_conv_w", "d_conv_b", "d_w_gx", "d_b_gx", "d_w_ga", "d_b_ga", "d_a_param"]
    result = [loss, grad_x]
    for j in range(4):
        result += [outs[n][j].reshape(W[n].shape) for n in order]
    return tuple(result)
```

```python
import functools

import jax
import jax.numpy as jnp
from jax import lax
from jax.experimental import pallas as pl
from jax.experimental.pallas import tpu as pltpu

F32 = jnp.float32
MXU = jnp.bfloat16
WIRE = jnp.bfloat16

D = 1024
TOK = 768
XW = 256
NMEM = 256
XHEADS = 4
XSCALE = 64 ** -0.5
NH = 6
HD = 128
CH = 16
POOL_WINDOWS = (2, 4, 8, 16)
POOL_GROUP = 192
DEPTH = 4
ALPHA = (2 * DEPTH) ** 0.25
LN_EPS = 1e-5
RMS_EPS = 1e-6
LRU_C = 8.0
B1, B2, LR, EPS, WD, STEP = 0.9, 0.999, 0.001, 1e-8, 0.01, 10

NDEV = 8
TS_FWD = {0: 512, 1: 256, 2: 512, 3: 256}
TS_BWD = {0: 512, 1: 256, 2: 512, 3: 256}
TSB = 512
VMEM_LIMIT = 58 * 1024 * 1024
VMEM_LIMIT_WIDE = 62 * 1024 * 1024

KIND_WIDTHS = {0: 2 * TOK + XW + D, 1: 3 * TOK + XW + D, 2: TOK + XW + D, 3: TOK + XW + D}


def _mm(a, b, ca, cb):
    return lax.dot_general(a.astype(MXU), b.astype(MXU), (((ca,), (cb,)), ((), ())), preferred_element_type=F32)


def _nn(a, b):
    return _mm(a, b, 1, 0)


def _nt(a, b):
    return _mm(a, b, 1, 1)


def _tn(a, b):
    return _mm(a, b, 0, 0)


def _bmm(a, b, ca, cb):
    return lax.dot_general(a.astype(MXU), b.astype(MXU), (((ca,), (cb,)), ((0,), (0,))), preferred_element_type=F32)


def _sigmoid(x):
    return 1.0 / (1.0 + jnp.exp(-x))


def _vjp1(fn, x, dy):
    return jax.vjp(fn, x)[1](dy)[0]


def _rowsum(x):
    return jnp.sum(x, axis=0, keepdims=True)


def _row(x, r):
    sel = lax.broadcasted_iota(jnp.int32, x.shape, 0) == r
    return jnp.sum(jnp.where(sel, x, 0.0), axis=0, keepdims=True)


def _acc(ref, val):
    ref[...] += val


def _cparams(sem=None, vmem=VMEM_LIMIT):
    return pltpu.CompilerParams(dimension_semantics=sem, vmem_limit_bytes=vmem)


def _res(a):
    nd = a.ndim
    return pl.BlockSpec(a.shape, lambda i: (0,) * nd)


def _res_sds(shape):
    nd = len(shape)
    return pl.BlockSpec(shape, lambda i: (0,) * nd)


def _row_spec(width, nt, rev, ts):
    if rev:
        return pl.BlockSpec((ts, width), lambda i: (nt - 1 - i, 0))
    return pl.BlockSpec((ts, width), lambda i: (i, 0))


def _call(body, name, grid, ins, outs, scratch=(), sem=("arbitrary",), vmem=VMEM_LIMIT):
    arrays = [a for a, _ in ins]
    return pl.pallas_call(
        body, name=name, grid=grid,
        in_specs=[s for _, s in ins],
        out_specs=[s for _, s in outs],
        out_shape=[o for o, _ in outs],
        scratch_shapes=list(scratch),
        compiler_params=_cparams(sem, vmem),
    )(*arrays)


def _xattn_fwd(qx, ks_ref, vs_ref):
    o = None
    ps = []
    for h in range(XHEADS):
        s = _nt(qx, ks_ref[h]) * XSCALE
        s = s - jnp.max(s, axis=-1, keepdims=True)
        e = jnp.exp(s)
        p = e * (1.0 / jnp.sum(e, axis=-1, keepdims=True))
        ps.append(p)
        oh = _nn(p, vs_ref[h])
        o = oh if o is None else o + oh
    return o, ps


def _xattn_bwd(qx, ps, dxo, ks_ref, vs_ref, dks_ref, dvs_ref):
    dq = None
    for h in range(XHEADS):
        p = ps[h]
        dp = _nt(dxo, vs_ref[h])
        ds = p * (dp - jnp.sum(dp * p, axis=-1, keepdims=True))
        dqh = _nn(ds, ks_ref[h]) * XSCALE
        dq = dqh if dq is None else dq + dqh
        dks_ref[h] += _tn(ds, qx) * XSCALE
        dvs_ref[h] += _tn(p, dxo)
    return dq


def _tril128():
    r = lax.broadcasted_iota(jnp.int32, (HD, HD), 0)
    c = lax.broadcasted_iota(jnp.int32, (HD, HD), 1)
    return c <= r


GELU_C = 0.7978845608028654
GELU_K = 0.044715


def _gelu(x):
    th = jnp.tanh(GELU_C * (x + GELU_K * (x * x * x)))
    return 0.5 * x * (1.0 + th), th


def _gelu_grad(x, th):
    return 0.5 * (1.0 + th) + 0.5 * x * (1.0 - th * th) * (GELU_C * (1.0 + 3.0 * GELU_K * (x * x)))


def _gmlp_fwd(u, v, ws_ref, bs_ref):
    ts = u.shape[0]
    ug, thu = _gelu(u)
    vg, thv = _gelu(v)
    tri = _tril128()
    toks, res = [], []
    for g in range(NH):
        sl = slice(g * HD, (g + 1) * HD)
        vgh = vg[:, sl]
        cen = vgh - jnp.mean(vgh, axis=-1, keepdims=True)
        rstd = lax.rsqrt(jnp.mean(cen * cen, axis=-1, keepdims=True) + LN_EPS)
        vn = cen * rstd
        w = jnp.where(tri, ws_ref[g], 0.0).astype(MXU)
        mix = jnp.concatenate([_nn(w, vn[n * HD:(n + 1) * HD]) + bs_ref[g] for n in range(ts // HD)], axis=0)
        toks.append(ug[:, sl] * mix)
        res.append((vn, rstd, mix, w))
    return jnp.concatenate(toks, axis=1), (ug, res, thu, thv)


def _gmlp_bwd(u, v, fres, dtok, dws_ref, dbs_ref):
    ts = u.shape[0]
    ug, res, thu, thv = fres
    tri = _tril128()
    dugs, dvgs = [], []
    for g in range(NH):
        sl = slice(g * HD, (g + 1) * HD)
        vn, rstd, mix, w = res[g]
        dmix = dtok[:, sl] * ug[:, sl]
        dugs.append(dtok[:, sl] * mix)
        dvn_rows = []
        dw = None
        dbs = None
        for n in range(ts // HD):
            dm = dmix[n * HD:(n + 1) * HD]
            dvn_rows.append(_tn(w, dm))
            t = _nt(dm, vn[n * HD:(n + 1) * HD])
            dw = t if dw is None else dw + t
            dbs = dm if dbs is None else dbs + dm
        dws_ref[g] += jnp.where(tri, dw, 0.0)
        dbs_ref[g] += dbs
        dvn = jnp.concatenate(dvn_rows, axis=0)
        dvgs.append(rstd * (dvn - jnp.mean(dvn, axis=-1, keepdims=True) - vn * jnp.mean(dvn * vn, axis=-1, keepdims=True)))
    du = jnp.concatenate(dugs, axis=1) * _gelu_grad(u, thu)
    dv = jnp.concatenate(dvgs, axis=1) * _gelu_grad(v, thv)
    return du, dv


def _chunk_cumsum(x):
    row = lax.broadcasted_iota(jnp.int32, x.shape, 0) % CH
    for s in (1, 2, 4, 8):
        x = x + jnp.where(row >= s, pltpu.roll(x, s, 0), 0.0)
    return x


def _chunk_revcumsum(x):
    n = x.shape[0]
    row = lax.broadcasted_iota(jnp.int32, x.shape, 0) % CH
    for s in (1, 2, 4, 8):
        x = x + jnp.where(row < CH - s, pltpu.roll(x, n - s, 0), 0.0)
    return x


def _chunk_sum(x):
    n, w = x.shape
    return jnp.sum(x.reshape(n // CH, CH, w), axis=1)


def _chunk_bcast(c, n):
    nch, w = c.shape
    return jnp.broadcast_to(c[:, None, :], (nch, CH, w)).reshape(n, w)


def _lower_bound(lb_logits, layer):
    lg = lb_logits
    e = jnp.exp(lg - jnp.max(lg, axis=0, keepdims=True))
    p = e / jnp.sum(e, axis=0, keepdims=True)
    row = lax.broadcasted_iota(jnp.int32, p.shape, 0)
    lb = jnp.sum(jnp.where((row >= 1) & (row <= layer), p, 0.0), axis=0, keepdims=True)
    return lb, p


def _hgrn_prep(q, fl, lb):
    n = q.shape[0]
    sg = _sigmoid(fl)
    f = lb + (1.0 - lb) * sg
    lf = jnp.log(f)
    sq = _sigmoid(q)
    g = _chunk_cumsum(lf)
    tot = _chunk_sum(lf)
    gl = _chunk_bcast(tot, n)
    eg = jnp.exp(g)
    eng = jnp.exp(-g)
    egl = jnp.exp(gl - g)
    k = 1.0 - f
    qf = q * sq
    return dict(sg=sg, f=f, k=k, sq=sq, qf=qf, eg=eg, eng=eng, egl=egl,
                qd=qf * eg, ki=k * eng, ke=k * egl, dch=jnp.exp(tot))


def _hgrn_mask():
    r = lax.broadcasted_iota(jnp.int32, (HD, HD), 0)
    c = lax.broadcasted_iota(jnp.int32, (HD, HD), 1)
    return (r // CH == c // CH) & (c <= r)


def _hgrn_states(v3, ke3, dch_h, st_in):
    nch = v3.shape[0]
    ut = _bmm(v3, ke3, 1, 1)
    dfull = jnp.broadcast_to(dch_h[:, None, :], (nch, HD, HD))
    st, sts = st_in, []
    for c in range(nch):
        sts.append(st)
        st = st * dfull[c] + ut[c]
    return jnp.stack(sts), st, dfull


def _hgrn_fwd(q, fl, inp, lb, ng, st_ref):
    n = q.shape[0]
    nch = n // CH
    pr = _hgrn_prep(q, fl, lb)
    mask = _hgrn_mask()
    toks = []
    qd_m, ki_m, ke_m, v_m = (t.astype(MXU) for t in (pr["qd"], pr["ki"], pr["ke"], inp))
    for h in range(NH):
        sl = slice(h * HD, (h + 1) * HD)
        qd, ki, ke, v = qd_m[:, sl], ki_m[:, sl], ke_m[:, sl], v_m[:, sl]
        qd3 = qd.reshape(nch, CH, HD)
        v3 = v.reshape(nch, CH, HD)
        ke3 = ke.reshape(nch, CH, HD)
        sts, st_ref[h], _ = _hgrn_states(v3, ke3, pr["dch"][:, sl], st_ref[h])
        o = _bmm(qd3, sts, 2, 2).reshape(n, HD)
        intra = []
        for b in range(n // HD):
            bs = slice(b * HD, (b + 1) * HD)
            a = jnp.where(mask, _nt(qd[bs], ki[bs]), 0.0)
            intra.append(_nn(a, v[bs]))
        o = o + jnp.concatenate(intra, axis=0)
        r = lax.rsqrt(jnp.mean(o * o, axis=-1, keepdims=True) + RMS_EPS)
        toks.append(o * r * ng[:, sl])
    return jnp.concatenate(toks, axis=1), pr


def _hgrn_bwd(q, pr, inp, lb, ng, dtok, ststart_ref, dst_ref, dng_ref, dlb_ref):
    n = q.shape[0]
    nch = n // CH
    mask = _hgrn_mask()
    dqd_l, dki_l, dke_l, dv_l, ddch_l, dng_l, toks = [], [], [], [], [], [], []
    qd_m, ki_m, ke_m, v_m = (t.astype(MXU) for t in (pr["qd"], pr["ki"], pr["ke"], inp))
    for h in range(NH):
        sl = slice(h * HD, (h + 1) * HD)
        qd, ki, ke, v = qd_m[:, sl], ki_m[:, sl], ke_m[:, sl], v_m[:, sl]
        qd3 = qd.reshape(nch, CH, HD)
        v3 = v.reshape(nch, CH, HD)
        ke3 = ke.reshape(nch, CH, HD)
        sts, _, dfull = _hgrn_states(v3, ke3, pr["dch"][:, sl], ststart_ref[h])
        sts_m = sts.astype(MXU)
        o = _bmm(qd3, sts_m, 2, 2).reshape(n, HD)
        a_l = []
        intra = []
        for b in range(n // HD):
            bs = slice(b * HD, (b + 1) * HD)
            a = jnp.where(mask, _nt(qd[bs], ki[bs]), 0.0).astype(MXU)
            a_l.append(a)
            intra.append(_nn(a, v[bs]))
        o = o + jnp.concatenate(intra, axis=0)
        r = lax.rsqrt(jnp.mean(o * o, axis=-1, keepdims=True) + RMS_EPS)
        toks.append(o * r * ng[:, sl])
        dt = dtok[:, sl]
        dng_l.append(_rowsum(dt * o * r))
        dn = dt * ng[:, sl]
        do = r * dn - o * (r * r * r) * jnp.mean(dn * o, axis=-1, keepdims=True)
        do_m = do.astype(MXU)
        do3 = do_m.reshape(nch, CH, HD)
        dqd_rows, dki_rows, dv_rows = [], [], []
        for b in range(n // HD):
            bs = slice(b * HD, (b + 1) * HD)
            da = jnp.where(mask, _nt(do_m[bs], v[bs]), 0.0).astype(MXU)
            dqd_rows.append(_nn(da, ki[bs]))
            dki_rows.append(_tn(da, qd[bs]))
            dv_rows.append(_tn(a_l[b], do_m[bs]))
        dqd = jnp.concatenate(dqd_rows, axis=0) + _bmm(do3, sts_m, 2, 1).reshape(n, HD)
        dki = jnp.concatenate(dki_rows, axis=0)
        dv = jnp.concatenate(dv_rows, axis=0)
        wt = _bmm(do3, qd3, 1, 1)
        dst, dstn_l = dst_ref[h], [None] * nch
        for c in reversed(range(nch)):
            dstn_l[c] = dst
            dst = wt[c] + dst * dfull[c]
        dst_ref[h] = dst
        dstn = jnp.stack(dstn_l)
        dstn_m = dstn.astype(MXU)
        dv = dv + _bmm(ke3, dstn_m, 2, 2).reshape(n, HD)
        dke = _bmm(v3, dstn_m, 2, 1).reshape(n, HD)
        ddch_l.append(jnp.sum(sts * dstn, axis=1))
        dqd_l.append(dqd)
        dki_l.append(dki)
        dke_l.append(dke)
        dv_l.append(dv)
    dqd = jnp.concatenate(dqd_l, axis=1)
    dki = jnp.concatenate(dki_l, axis=1)
    dke = jnp.concatenate(dke_l, axis=1)
    dinp = jnp.concatenate(dv_l, axis=1)
    ddch = jnp.concatenate(ddch_l, axis=1)
    _acc(dng_ref, jnp.concatenate(dng_l, axis=1))
    dqf = dqd * pr["eg"]
    dke_ke = dke * pr["ke"]
    dg = dqd * pr["qd"] - dki * pr["ki"] - dke_ke
    dk = dki * pr["eng"] + dke * pr["egl"]
    dgl = _chunk_sum(dke_ke) + ddch * pr["dch"]
    dlf = _chunk_revcumsum(dg) + _chunk_bcast(dgl, n)
    df = dlf / pr["f"] - dk
    sg = pr["sg"]
    dfl = df * (1.0 - lb) * sg * (1.0 - sg)
    _acc(dlb_ref, _rowsum(df * (1.0 - sg)))
    sq = pr["sq"]
    dq = dqf * (sq * (1.0 + q * (1.0 - sq)))
    return jnp.concatenate(toks, axis=1), dq, dfl, dinp


def _pool_select(s2, s4, s8, s16):
    col = lax.broadcasted_iota(jnp.int32, (1, TOK), 1)
    return jnp.where(col < POOL_GROUP, s2, jnp.where(col < 2 * POOL_GROUP, s4, jnp.where(col < 3 * POOL_GROUP, s8, s16)))


def _pool_cnt(pos0, n):
    pos = pos0 + lax.broadcasted_iota(jnp.int32, (n, TOK), 0) + 1
    col = lax.broadcasted_iota(jnp.int32, (n, TOK), 1)
    w = jnp.where(col < POOL_GROUP, 2, jnp.where(col < 2 * POOL_GROUP, 4, jnp.where(col < 3 * POOL_GROUP, 8, 16)))
    return jnp.minimum(pos, w).astype(F32)


def _pool_fwd(p, halo, pos0, wbd, scale):
    n = p.shape[0]
    ext = jnp.concatenate([halo, p], axis=0)
    s2 = ext + pltpu.roll(ext, 1, 0)
    s4 = s2 + pltpu.roll(s2, 2, 0)
    s8 = s4 + pltpu.roll(s4, 4, 0)
    s16 = s8 + pltpu.roll(s8, 8, 0)
    win = _pool_select(s2, s4, s8, s16)[16:]
    cnt = _pool_cnt(pos0, n)
    diff = win / cnt - p
    y = _nn(diff, wbd)
    return y * scale, (diff, y, cnt)


def _pool_bwd(fres, dtok, nxt_ref, wbd, scale, dwbd_ref, dscale_ref):
    diff, y, cnt = fres
    n = diff.shape[0]
    _acc(dscale_ref, _rowsum(dtok * y))
    dy = dtok * scale
    ddiff = _nt(dy, wbd)
    dwbd_ref[...] += _tn(diff, dy)
    qv = ddiff / cnt
    ext = jnp.concatenate([qv, nxt_ref[...]], axis=0)
    m = n + 16
    s2 = ext + pltpu.roll(ext, m - 1, 0)
    s4 = s2 + pltpu.roll(s2, m - 2, 0)
    s8 = s4 + pltpu.roll(s4, m - 4, 0)
    s16 = s8 + pltpu.roll(s8, m - 8, 0)
    adj = _pool_select(s2, s4, s8, s16)[:n]
    nxt_ref[...] = qv[:16]
    return adj - ddiff


def _neg_expm1(x):
    return jnp.where(jnp.abs(x) < 1e-2, -x * (1.0 + x * (0.5 + x * (1.0 / 6.0))), 1.0 - jnp.exp(x))


def _softplus_neg(ap):
    return jnp.maximum(-ap, 0.0) + jnp.log(1.0 + jnp.exp(-jnp.abs(ap)))


def _lru_gates(xc, zx, za, ap, first):
    gx = _sigmoid(zx)
    ga = _sigmoid(za)
    sp = _softplus_neg(ap)
    log_a = -LRU_C * ga * sp
    a = jnp.exp(log_a)
    mult = jnp.sqrt(_neg_expm1(2.0 * log_a))
    mult = jnp.where(first, 1.0, mult)
    return a, mult * gx * xc, (gx, ga, sp, mult)


def _scan_fwd(a, b, h0):
    n = a.shape[0]
    row = lax.broadcasted_iota(jnp.int32, a.shape, 0)
    s = 1
    while s < n:
        keep = row >= s
        b = b + a * jnp.where(keep, pltpu.roll(b, s, 0), 0.0)
        a = a * jnp.where(keep, pltpu.roll(a, s, 0), 1.0)
        s *= 2
    return b + a * h0


def _scan_bwd(an, d, dh_next):
    n = an.shape[0]
    row = lax.broadcasted_iota(jnp.int32, an.shape, 0)
    s = 1
    while s < n:
        keep = row < n - s
        d = d + an * jnp.where(keep, pltpu.roll(d, n - s, 0), 0.0)
        an = an * jnp.where(keep, pltpu.roll(an, n - s, 0), 1.0)
        s *= 2
    return d + an * dh_next


def _lru_conv(xb, halo, cw_ref, cb):
    ext = jnp.concatenate([halo, xb], axis=0)
    sh = [pltpu.roll(ext, 3 - j, 0)[8:] if j < 3 else xb for j in range(4)]
    xc = cb
    for j in range(4):
        xc = xc + cw_ref[pl.ds(j, 1), :] * sh[j]
    return xc, sh


def _lru_fwd(xb, halo, pos0, prm, h0):
    cw, cb, wgx, bgx, wga, bga, ap = prm
    n = xb.shape[0]
    xc, sh = _lru_conv(xb, halo, cw, cb[...])
    zx = jnp.concatenate([_nn(xc[:, h * HD:(h + 1) * HD], wgx[h]) for h in range(NH)], axis=1) + bgx[...]
    za = jnp.concatenate([_nn(xc[:, h * HD:(h + 1) * HD], wga[h]) for h in range(NH)], axis=1) + bga[...]
    first = (pos0 + lax.broadcasted_iota(jnp.int32, (n, 1), 0)) == 0
    a, b, gates = _lru_gates(xc, zx, za, ap[...], first)
    hseq = _scan_fwd(a, b, h0)
    return hseq, (xc, sh, gates, first, a)


def _lru_bwd(fres, hseq, h0, dtok, prm, carry_refs, grad_refs):
    cw, cb, wgx, bgx, wga, bga, ap = prm
    xc, sh, (gx, ga, sp, mult), first, a = fres
    anext_ref, dhnext_ref, dxcnext_ref = carry_refs
    dcw_ref, dcb_ref, dwgx_ref, dbgx_ref, dwga_ref, dbga_ref, dap_ref = grad_refs
    n = xc.shape[0]
    an = jnp.where(lax.broadcasted_iota(jnp.int32, a.shape, 0) == n - 1, anext_ref[...], pltpu.roll(a, n - 1, 0))
    dh = _scan_bwd(an, dtok, dhnext_ref[...])
    hprev = jnp.where(lax.broadcasted_iota(jnp.int32, hseq.shape, 0) == 0, h0, pltpu.roll(hseq, 1, 0))
    da = dh * hprev
    anext_ref[...] = _row(a, 0)
    dhnext_ref[...] = _row(dh, 0)
    t = dh * xc
    dxc = dh * mult * gx
    dzx = t * mult * gx * (1.0 - gx)
    dlog_a = da * a - jnp.where(first, 0.0, t * gx * (a * a) / mult)
    dza = dlog_a * (-LRU_C * sp) * ga * (1.0 - ga)
    dap = _rowsum(dlog_a * ga) * (LRU_C * _sigmoid(-ap[...]))
    _acc(dap_ref, dap)
    _acc(dbgx_ref, _rowsum(dzx))
    _acc(dbga_ref, _rowsum(dza))
    parts = []
    for h in range(NH):
        sl = slice(h * HD, (h + 1) * HD)
        parts.append(_nt(dzx[:, sl], wgx[h]) + _nt(dza[:, sl], wga[h]))
        dwgx_ref[h] += _tn(xc[:, sl], dzx[:, sl])
        dwga_ref[h] += _tn(xc[:, sl], dza[:, sl])
    dxc = dxc + jnp.concatenate(parts, axis=1)
    _acc(dcb_ref, _rowsum(dxc))
    for j in range(4):
        dcw_ref[pl.ds(j, 1), :] += _rowsum(dxc * sh[j])
    ext = jnp.concatenate([dxc, dxcnext_ref[...]], axis=0)
    m = n + 8
    dxb = cw[pl.ds(3, 1), :] * dxc
    for j in range(3):
        dxb = dxb + cw[pl.ds(j, 1), :] * pltpu.roll(ext, m - (3 - j), 0)[:n]
    dxcnext_ref[...] = dxc[:8]
    return dxb


def _layer_fwd(kind, layer, xprev, gprev, bprev, wt, wout, ks, vs, prm, ride=None):
    TS = TS_FWD[kind]
    _rows = functools.partial(_row_spec, ts=TS)
    S = xprev.shape[0]
    nt = S // TS
    N = wt.shape[0]
    nprm = len(prm)
    nch = TS // CH

    outs = [(jax.ShapeDtypeStruct((S, N), F32), _rows(N, nt, False)),
            (jax.ShapeDtypeStruct((S, D), F32), _rows(D, nt, False)),
            (jax.ShapeDtypeStruct((S, 1), F32), _rows(1, nt, False)),
            (jax.ShapeDtypeStruct((S, XHEADS * NMEM), MXU), _rows(XHEADS * NMEM, nt, False))]
    scratch = []
    if kind == 1:
        outs.append((jax.ShapeDtypeStruct((nt, NH, HD, HD), F32), pl.BlockSpec((None, NH, HD, HD), lambda i: (i, 0, 0, 0))))
        scratch = [pltpu.VMEM((NH, HD, HD), F32)]
    elif kind == 2:
        scratch = [pltpu.VMEM((16, TOK), F32)]
    elif kind == 3:
        outs.append((jax.ShapeDtypeStruct((nt * 8, TOK), F32), pl.BlockSpec((8, TOK), lambda i: (i, 0))))
        outs += [(jax.ShapeDtypeStruct((S, TOK), F32), _rows(TOK, nt, False))] * 4
        scratch = [pltpu.VMEM((8, TOK), F32), pltpu.VMEM((1, TOK), F32)]
    nout = len(outs)
    nscr = len(scratch)
    nride = len(ride.arrays) if ride else 0

    def body(*refs):
        x_ref, g_ref, b_ref, wt_ref, wout_ref, ks_ref, vs_ref = refs[:7]
        prm_refs = refs[7:7 + nprm]
        nin = 7 + nprm + nride
        ride_src = refs[7 + nprm:nin]
        out_refs = refs[nin:nin + nout]
        ride_dst = refs[nin + nout:nin + nout + nride]
        scr = refs[nin + nout + nride:nin + nout + nride + nscr]
        ride_sems = refs[nin + nout + nride + nscr:]
        proj_ref, xhat_ref, rstd_ref = out_refs[:3]
        i = pl.program_id(0)
        if ride:
            @pl.when(i == 0)
            def _():
                ride.start(ride_src, ride_dst, ride_sems)

        xin = x_ref[...] * g_ref[...] + b_ref[...]
        proj = _nt(xin, wt_ref[...])
        proj_ref[...] = proj
        if kind == 0:
            tok, _ = _gmlp_fwd(proj[:, :TOK], proj[:, TOK:2 * TOK], prm_refs[0], prm_refs[1])
        elif kind == 1:
            st_ref, = scr

            @pl.when(i == 0)
            def _():
                st_ref[...] = jnp.zeros_like(st_ref)

            out_refs[4][...] = st_ref[...]
            lb, _ = _lower_bound(prm_refs[0][...], layer)
            tok, _ = _hgrn_fwd(proj[:, :TOK], proj[:, TOK:2 * TOK], proj[:, 2 * TOK:3 * TOK], lb, prm_refs[1][...],
                               st_ref)
        elif kind == 2:
            halo_ref, = scr

            @pl.when(i == 0)
            def _():
                halo_ref[...] = jnp.zeros_like(halo_ref)

            p = proj[:, :TOK]
            tok, _ = _pool_fwd(p, halo_ref[...], i * TS, prm_refs[0][...], prm_refs[1][...])
            halo_ref[...] = p[TS - 16:]
        else:
            halo_ref, h_ref = scr

            @pl.when(i == 0)
            def _():
                halo_ref[...] = jnp.zeros_like(halo_ref)
                h_ref[...] = jnp.zeros_like(h_ref)

            out_refs[4][...] = jnp.broadcast_to(h_ref[...], (8, TOK))
            xb = proj[:, :TOK]
            tok, fres = _lru_fwd(xb, halo_ref[...], i * TS, prm_refs, h_ref[...])
            gx, ga, _, mult = fres[2]
            for r, val in zip(out_refs[5:9], (tok, gx, ga, mult)):
                r[...] = val
            halo_ref[...] = xb[TS - 8:]
            h_ref[...] = _row(tok, TS - 1)
        qx = proj[:, N - D - XW:N - D]
        gate = proj[:, N - D:]
        xo, ps = _xattn_fwd(qx, ks_ref, vs_ref)
        out_refs[3][...] = jnp.concatenate(ps, axis=1).astype(MXU)
        mixed = jnp.concatenate([tok, xo], axis=1) * (gate * _sigmoid(gate))
        z = ALPHA * xin + _nn(mixed, wout_ref[...])
        cen = z - jnp.mean(z, axis=-1, keepdims=True)
        rstd = lax.rsqrt(jnp.mean(cen * cen, axis=-1, keepdims=True) + LN_EPS)
        xhat_ref[...] = cen * rstd
        rstd_ref[...] = rstd
        if ride:
            @pl.when(i == nt - 1)
            def _():
                ride.wait(ride_src, ride_dst, ride_sems)

    ins = [(xprev, _rows(D, nt, False)), (gprev, _res(gprev)), (bprev, _res(bprev)), (wt, _res(wt)), (wout, _res(wout)),
           (ks, _res(ks)), (vs, _res(vs))] + [(p, _res(p)) for p in prm]
    if ride:
        ins += [(a, _ANY) for a in ride.arrays]
        outs += [(s, _ANY) for s in ride.out_shapes]
        scratch = scratch + ride.scratch
    return _call(body, f"layer{layer}_fwd", (nt,), ins, outs, scratch)


def _layer_bwd(kind, layer, up, is_last, xhat, rstd, probs, g_i, b_i, proj, wout, ks, vs, prm, extra, ride=None):
    TS = TS_BWD[kind]
    _rows = functools.partial(_row_spec, ts=TS)
    S = xhat.shape[0]
    nt = S // TS
    N = proj.shape[1]
    nprm = len(prm)
    nch = TS // CH

    ins = [(up, _rows(D, nt, True)), (xhat, _rows(D, nt, True)), (rstd, _rows(1, nt, True)), (g_i, _res(g_i)), (b_i, _res(b_i)),
           (proj, _rows(N, nt, True)), (wout, _res(wout)), (ks, _res(ks)), (vs, _res(vs)),
           (probs, _rows(XHEADS * NMEM, nt, True))] + [(p, _res(p)) for p in prm]
    nfixed = 10
    if kind == 1:
        ins.append((extra, pl.BlockSpec((None, NH, HD, HD), lambda i: (nt - 1 - i, 0, 0, 0))))
    elif kind == 2:
        hb = TS // 16
        ins.append((proj, pl.BlockSpec((16, TOK), lambda i: (jnp.maximum((nt - 1 - i) * hb - 1, 0), 0))))
    elif kind == 3:
        hb = TS // 8
        ins.append((proj, pl.BlockSpec((8, TOK), lambda i: (jnp.maximum((nt - 1 - i) * hb - 1, 0), 0))))
        ins.append((extra[0], pl.BlockSpec((8, TOK), lambda i: (nt - 1 - i, 0))))
        ins += [(e, _rows(TOK, nt, True)) for e in extra[1:]]
    nin = len(ins)

    def acc(shape):
        return (jax.ShapeDtypeStruct(shape, F32), _res_sds(shape))

    outs = [(jax.ShapeDtypeStruct((S, D), F32), _rows(D, nt, True)),
            (jax.ShapeDtypeStruct((S, N), MXU), _rows(N, nt, True)),
            (jax.ShapeDtypeStruct((D, D), WIRE), _res_sds((D, D))),
            acc((XHEADS, NMEM, XW)), acc((XHEADS, NMEM, XW)), acc((1, D)), acc((1, D)), acc((1, HD))]
    scratch = []
    if kind == 0:
        outs += [acc((NH, HD, HD)), acc((NH, HD, HD))]
    elif kind == 1:
        outs += [acc((1, TOK)), acc((1, TOK))]
        scratch = [pltpu.VMEM((NH, HD, HD), F32)]
    elif kind == 2:
        outs += [acc((TOK, TOK)), acc((1, TOK))]
        scratch = [pltpu.VMEM((16, TOK), F32)]
    else:
        outs += [acc((4, TOK)), acc((1, TOK)), acc((NH, HD, HD)), acc((1, TOK)), acc((NH, HD, HD)), acc((1, TOK)), acc((1, TOK))]
        scratch = [pltpu.VMEM((1, TOK), F32), pltpu.VMEM((1, TOK), F32), pltpu.VMEM((8, TOK), F32)]
    scratch = scratch + [pltpu.VMEM((D, D), F32)]
    nout = len(outs)
    nscr = len(scratch)
    nride = len(ride.arrays) if ride else 0

    def body(*refs):
        up_ref, xhat_ref, rstd_ref, g_ref, b_ref, proj_ref, wout_ref, ks_ref, vs_ref, probs_ref = refs[:nfixed]
        prm_refs = refs[nfixed:nfixed + nprm]
        ext_refs = refs[nfixed + nprm:nin]
        ride_src = refs[nin:nin + nride]
        o0 = nin + nride
        out_refs = refs[o0:o0 + nout]
        ride_dst = refs[o0 + nout:o0 + nout + nride]
        scr = refs[o0 + nout + nride:o0 + nout + nride + nscr - 1]
        dwout_acc = refs[o0 + nout + nride + nscr - 1]
        ride_sems = refs[o0 + nout + nride + nscr:]
        dres_ref, dproj_ref, dwout_ref, dks_ref, dvs_ref, dg_ref, db_ref, loss_ref = out_refs[:8]
        pgrad = out_refs[8:]
        i = pl.program_id(0)
        tile = nt - 1 - i

        @pl.when(i == 0)
        def _():
            if ride:
                ride.start(ride_src, ride_dst, ride_sems)
            for r in out_refs[3:]:
                r[...] = jnp.zeros_like(r)
            dwout_acc[...] = jnp.zeros_like(dwout_acc)
            for r in scr:
                if kind != 1 or r is scr[0]:
                    r[...] = jnp.zeros_like(r)

        xhat_v = xhat_ref[...]
        if is_last:
            err = xhat_v * g_ref[...] + b_ref[...] - up_ref[...]
            dxo = err * (1.0 / D)
            loss_ref[...] += jnp.sum(0.5 * jnp.mean(err * err, axis=-1, keepdims=True), axis=0, keepdims=True)
        else:
            dxo = up_ref[...]
        _acc(dg_ref, _rowsum(dxo * xhat_v))
        _acc(db_ref, _rowsum(dxo))
        dxh = dxo * g_ref[...]
        dz = rstd_ref[...] * (dxh - jnp.mean(dxh, axis=-1, keepdims=True)
                              - xhat_v * jnp.mean(dxh * xhat_v, axis=-1, keepdims=True))
        dres_ref[...] = ALPHA * dz

        proj = proj_ref[...]
        qx = proj[:, N - D - XW:N - D]
        gate = proj[:, N - D:]
        sgate = _sigmoid(gate)
        silu = gate * sgate
        dmixed = _nt(dz, wout_ref[...])
        dcat = dmixed * silu
        dtok = dcat[:, :TOK]
        if kind == 0:
            u, v = proj[:, :TOK], proj[:, TOK:2 * TOK]
            tok, fres = _gmlp_fwd(u, v, prm_refs[0], prm_refs[1])
            du, dv = _gmlp_bwd(u, v, fres, dtok, pgrad[0], pgrad[1])
            dproj_ref[:, :TOK] = du.astype(MXU)
            dproj_ref[:, TOK:2 * TOK] = dv.astype(MXU)
        elif kind == 1:
            dst_ref, = scr
            lb, _ = _lower_bound(prm_refs[0][...], layer)
            q, fl, inp = proj[:, :TOK], proj[:, TOK:2 * TOK], proj[:, 2 * TOK:3 * TOK]
            pr = _hgrn_prep(q, fl, lb)
            tok, dq, dfl, dinp = _hgrn_bwd(q, pr, inp, lb, prm_refs[1][...], dtok, ext_refs[0], dst_ref, pgrad[1], pgrad[0])
            dproj_ref[:, :TOK] = dq.astype(MXU)
            dproj_ref[:, TOK:2 * TOK] = dfl.astype(MXU)
            dproj_ref[:, 2 * TOK:3 * TOK] = dinp.astype(MXU)
        elif kind == 2:
            p = proj[:, :TOK]
            halo = jnp.where(tile == 0, 0.0, ext_refs[0][...])
            tok, fres = _pool_fwd(p, halo, tile * TS, prm_refs[0][...], prm_refs[1][...])
            dp = _pool_bwd(fres, dtok, scr[0], prm_refs[0][...], prm_refs[1][...], pgrad[0], pgrad[1])
            dproj_ref[:, :TOK] = dp.astype(MXU)
        else:
            xb = proj[:, :TOK]
            halo = jnp.where(tile == 0, 0.0, ext_refs[0][...])
            h0 = ext_refs[1][0:1]
            tok, gx, ga, mult = (r[...] for r in ext_refs[2:6])
            xc, sh = _lru_conv(xb, halo, prm_refs[0], prm_refs[1][...])
            sp = _softplus_neg(prm_refs[6][...])
            first = (tile * TS + lax.broadcasted_iota(jnp.int32, (TS, 1), 0)) == 0
            fres = (xc, sh, (gx, ga, sp, mult), first, jnp.exp(-LRU_C * ga * sp))
            dxb = _lru_bwd(fres, tok, h0, dtok, prm_refs, scr, pgrad)
            dproj_ref[:, :TOK] = dxb.astype(MXU)
        ps = [probs_ref[:, h * NMEM:(h + 1) * NMEM].astype(F32) for h in range(XHEADS)]
        xo = _nn(ps[0], vs_ref[0])
        for h in range(1, XHEADS):
            xo = xo + _nn(ps[h], vs_ref[h])
        dqx = _xattn_bwd(qx, ps, dcat[:, TOK:], ks_ref, vs_ref, dks_ref, dvs_ref)
        cat = jnp.concatenate([tok, xo], axis=1)
        dwout_acc[...] += _tn(cat * silu, dz)
        dgate = dmixed * cat * (sgate * (1.0 + gate * (1.0 - sgate)))
        dproj_ref[:, N - D - XW:N - D] = dqx.astype(MXU)
        dproj_ref[:, N - D:] = dgate.astype(MXU)

        @pl.when(i == nt - 1)
        def _():
            dwout_ref[...] = dwout_acc[...].astype(WIRE)
            if ride:
                ride.wait(ride_src, ride_dst, ride_sems)

    if ride:
        ins += [(a, _ANY) for a in ride.arrays]
        outs += [(s, _ANY) for s in ride.out_shapes]
        scratch = scratch + ride.scratch
    return _call(body, f"layer{layer}_bwd", (nt,), ins, outs, scratch, vmem=VMEM_LIMIT_WIDE if kind == 0 else VMEM_LIMIT)


def _proj_bwd(layer, dproj, dres, xprev, gprev, bprev, wt, ride=None):
    S = xprev.shape[0]
    nt = S // TSB
    N = wt.shape[0]

    nride = len(ride.arrays) if ride else 0

    def body(*refs):
        dproj_ref, dres_ref, x_ref, g_ref, b_ref, wt_ref = refs[:6]
        ride_src = refs[6:6 + nride]
        dx_ref, dwt_ref = refs[6 + nride:8 + nride]
        ride_dst = refs[8 + nride:8 + 2 * nride]
        acc_ref = refs[8 + 2 * nride]
        ride_sems = refs[9 + 2 * nride:]

        @pl.when(pl.program_id(0) == 0)
        def _():
            if ride:
                ride.start(ride_src, ride_dst, ride_sems)
            acc_ref[...] = jnp.zeros_like(acc_ref)

        dp = dproj_ref[...]
        xin = x_ref[...] * g_ref[...] + b_ref[...]
        dx_ref[...] = dres_ref[...] + _nn(dp, wt_ref[...])
        acc_ref[...] += _tn(dp, xin)

        @pl.when(pl.program_id(0) == nt - 1)
        def _():
            dwt_ref[...] = acc_ref[...].astype(WIRE)
            if ride:
                ride.wait(ride_src, ride_dst, ride_sems)

    ins = [(dproj, _row_spec(N, nt, False, TSB)), (dres, _row_spec(D, nt, False, TSB)), (xprev, _row_spec(D, nt, False, TSB)),
           (gprev, _res(gprev)), (bprev, _res(bprev)), (wt, _res(wt))]
    outs = [(jax.ShapeDtypeStruct((S, D), F32), _row_spec(D, nt, False, TSB)),
            (jax.ShapeDtypeStruct((N, D), WIRE), _res_sds((N, D)))]
    scratch = [pltpu.VMEM((N, D), F32)]
    if ride:
        ins += [(a, _ANY) for a in ride.arrays]
        outs += [(s, _ANY) for s in ride.out_shapes]
        scratch = scratch + ride.scratch
    return _call(body, f"layer{layer}_projbwd", (nt,), ins, outs, scratch)


def _head_mask(h):
    col = lax.broadcasted_iota(jnp.int32, (1, XW), 1)
    return (col // 64) == h


def _kv_fwd(mem, wkv):
    def body(mem_ref, w_ref, ks_ref, vs_ref):
        kv = _nn(mem_ref[...], w_ref[...])
        k, v = kv[:, :XW], kv[:, XW:]
        for h in range(XHEADS):
            ks_ref[h] = jnp.where(_head_mask(h), k, 0.0).astype(MXU)
            vs_ref[h] = jnp.where(_head_mask(h), v, 0.0).astype(MXU)

    sds = jax.ShapeDtypeStruct((XHEADS, NMEM, XW), MXU)
    return pl.pallas_call(body, name="kv_fwd", out_shape=(sds, sds), compiler_params=_cparams())(mem, wkv)


def _kv_bwd(mem, dks_l, dvs_l):
    def body(mem_ref, *refs):
        dks_refs, dvs_refs, out_ref = refs[:DEPTH], refs[DEPTH:2 * DEPTH], refs[2 * DEPTH]
        dk = jnp.zeros((NMEM, XW), F32)
        dv = jnp.zeros((NMEM, XW), F32)
        for h in range(XHEADS):
            m = _head_mask(h)
            for l in range(DEPTH):
                dk = dk + jnp.where(m, dks_refs[l][h], 0.0)
                dv = dv + jnp.where(m, dvs_refs[l][h], 0.0)
        out_ref[...] = _tn(mem_ref[...], jnp.concatenate([dk, dv], axis=1)).astype(WIRE)

    return pl.pallas_call(body, name="kv_bwd", out_shape=jax.ShapeDtypeStruct((D, 2 * XW), WIRE),
                          compiler_params=_cparams())(mem, *dks_l, *dvs_l)


def _prep_weights(w_ins, w_out, wkv):
    def body(a_ref, b_ref, c_ref, d_ref, wo_ref, kv_ref, ao, bo, co, do, wo0, wo1, wo2, wo3, kvo):
        for src, dst in ((a_ref, ao), (b_ref, bo), (c_ref, co), (d_ref, do)):
            dst[...] = src[...].T.astype(MXU)
        for l, dst in enumerate((wo0, wo1, wo2, wo3)):
            dst[...] = wo_ref[l].astype(MXU)
        kvo[...] = kv_ref[...].astype(MXU)

    outs = [jax.ShapeDtypeStruct((w.shape[1], w.shape[0]), MXU) for w in w_ins]
    outs += [jax.ShapeDtypeStruct(w_out.shape[1:], MXU)] * DEPTH + [jax.ShapeDtypeStruct(wkv.shape, MXU)]
    return pl.pallas_call(body, name="prep_weights", out_shape=outs, compiler_params=_cparams())(*w_ins, w_out, wkv)


def _adam_math(w, g, m, v):
    m = B1 * m + (1.0 - B1) * g
    v = B2 * v + (1.0 - B2) * (g * g)
    m_hat = m / (1.0 - B1 ** STEP)
    v_hat = v / (1.0 - B2 ** STEP)
    delta = -LR * (m_hat / (jnp.sqrt(v_hat) + EPS) + WD * w)
    return delta, m, v


def _sum_adam(name, recv, w, m, v, transpose):
    rows, cols = recv.shape[1], recv.shape[2]

    def body(r_ref, w_ref, m_ref, v_ref, g_out, d_out, m_out, v_out, acc_ref):
        s = pl.program_id(0)

        @pl.when(s == 0)
        def _():
            acc_ref[...] = r_ref[...].astype(F32)

        @pl.when(s > 0)
        def _():
            acc_ref[...] += r_ref[...].astype(F32)

        @pl.when(s == NDEV - 1)
        def _():
            g = acc_ref[...].T if transpose else acc_ref[...]
            d, mn, vn = _adam_math(w_ref[...], g, m_ref[...], v_ref[...])
            g_out[...] = g
            d_out[...] = d
            m_out[...] = mn
            v_out[...] = vn

    sds = jax.ShapeDtypeStruct(w.shape, F32)
    ins = [(recv, pl.BlockSpec((None, rows, cols), lambda s: (s, 0, 0))), (w, _res(w)), (m, _res(m)), (v, _res(v))]
    outs = [(sds, _res_sds(w.shape))] * 4
    return _call(body, name, (NDEV,), ins, outs, [pltpu.VMEM((rows, cols), F32)])


def _bias_finalize(dbs_exp):
    def body(dbs_ref, dabs_ref):
        dabs_ref[...] = jnp.sum(dbs_ref[...], axis=-1)

    return pl.pallas_call(body, name="bias_finalize", out_shape=jax.ShapeDtypeStruct((NH, HD), F32),
                          compiler_params=_cparams())(dbs_exp)


def _lb_finalize(dlb, lb_logits):
    def body(dlb_ref, lg_ref, dlg_ref):
        total = jnp.zeros((DEPTH, TOK), F32)
        lg = lg_ref[...]
        e = jnp.exp(lg - jnp.max(lg, axis=0, keepdims=True))
        p = e / jnp.sum(e, axis=0, keepdims=True)
        row = lax.broadcasted_iota(jnp.int32, (DEPTH, TOK), 0)
        for layer in range(DEPTH):
            if layer % 4 != 1:
                continue
            dp = jnp.where((row >= 1) & (row <= layer), dlb_ref[...], 0.0)
            total = total + p * (dp - jnp.sum(p * dp, axis=0, keepdims=True))
        dlg_ref[...] = total

    return pl.pallas_call(body, name="lb_finalize", out_shape=jax.ShapeDtypeStruct((DEPTH, TOK), F32),
                          compiler_params=_cparams())(dlb, lb_logits)


def _small_sum_adam(name, gathered, w, m, v):
    rows = w.shape[0]

    def body(r_ref, w_ref, m_ref, v_ref, g_out, d_out, m_out, v_out):
        g = r_ref[0]
        for s in range(1, NDEV):
            g = g + r_ref[s]
        d, mn, vn = _adam_math(w_ref[...], g, m_ref[...], v_ref[...])
        g_out[...] = g
        d_out[...] = d
        m_out[...] = mn
        v_out[...] = vn

    sds = jax.ShapeDtypeStruct((rows, 128), F32)
    return pl.pallas_call(body, name=name, out_shape=(sds,) * 4, compiler_params=_cparams())(gathered, w, m, v)


def _group_adam(name, recvs, params):
    nk = len(recvs)

    def body(*refs):
        pos, oi = nk, nk + 3 * sum(p is not None for p in params)
        for k in range(nk):
            g = refs[k][0]
            for s in range(1, NDEV):
                g = g + refs[k][s]
            refs[oi][...] = g
            oi += 1
            if params[k] is not None:
                d, mn, vn = _adam_math(refs[pos][...], g, refs[pos + 1][...], refs[pos + 2][...])
                refs[oi][...] = d
                refs[oi + 1][...] = mn
                refs[oi + 2][...] = vn
                pos += 3
                oi += 3

    out_shape, counts = [], []
    for k in range(nk):
        counts.append(4 if params[k] is not None else 1)
        out_shape += [jax.ShapeDtypeStruct(recvs[k].shape[1:], F32)] * counts[-1]
    args = list(recvs) + [a for p in params if p is not None for a in p]
    flat = pl.pallas_call(body, name=name, out_shape=out_shape, compiler_params=_cparams())(*args)
    res, o = [], 0
    for cnt in counts:
        res.append(flat[o:o + cnt])
        o += cnt
    return res


def _adam_only(g, w, m, v):
    def body(g_ref, w_ref, m_ref, v_ref, d_out, m_out, v_out):
        d, mn, vn = _adam_math(w_ref[...], g_ref[...], m_ref[...], v_ref[...])
        d_out[...] = d
        m_out[...] = mn
        v_out[...] = vn

    sds = jax.ShapeDtypeStruct(w.shape, F32)
    return pl.pallas_call(body, name="shard_adam", out_shape=(sds,) * 3, compiler_params=_cparams())(g, w, m, v)


def _me_and_peers():
    x, y, c = lax.axis_index("x"), lax.axis_index("y"), lax.axis_index("c")
    me = 4 * x + 2 * y + c
    peers = []
    for k in range(1, NDEV):
        kx, ky, kc = (k >> 2) & 1, (k >> 1) & 1, k & 1
        px = x + kx - 2 * x * kx
        py = y + ky - 2 * y * ky
        pc = c + kc - 2 * c * kc
        peers.append(((px, py, pc), 4 * px + 2 * py + pc))
    return me, peers


_ANY = pl.BlockSpec(memory_space=pl.ANY)


class _Exchange:
    def __init__(self, arrays, split):
        self.arrays = list(arrays)
        self.split = list(split)
        n = len(self.arrays)
        self.out_shapes = []
        for a, sp in zip(self.arrays, self.split):
            rows = a.shape[0] // NDEV if sp else a.shape[0]
            self.out_shapes.append(jax.ShapeDtypeStruct((NDEV, rows, a.shape[1]), a.dtype))
        self.scratch = [pltpu.SemaphoreType.DMA((n, NDEV - 1)), pltpu.SemaphoreType.DMA((n, NDEV - 1)),
                        pltpu.SemaphoreType.DMA((n,))]

    def _block(self, src, t, d):
        if not self.split[t]:
            return src[t]
        rows = self.arrays[t].shape[0] // NDEV
        return src[t].at[pl.ds(d * rows, rows)]

    def start(self, src, dst, sems):
        send_sems, recv_sems, local_sems = sems
        me, peers = _me_and_peers()
        for t in range(len(self.arrays)):
            pltpu.make_async_copy(self._block(src, t, me), dst[t].at[me], local_sems.at[t]).start()
        for k, (dev, idx) in enumerate(peers):
            for t in range(len(self.arrays)):
                pltpu.make_async_remote_copy(src_ref=self._block(src, t, idx), dst_ref=dst[t].at[me],
                                             send_sem=send_sems.at[t, k], recv_sem=recv_sems.at[t, k],
                                             device_id=dev, device_id_type=pl.DeviceIdType.MESH).start()

    def wait(self, src, dst, sems):
        send_sems, recv_sems, local_sems = sems
        me, peers = _me_and_peers()

        def slot_copy(t, k, dev, idx):
            return pltpu.make_async_remote_copy(src_ref=dst[t].at[idx], dst_ref=dst[t].at[idx], send_sem=send_sems.at[t, k],
                                                recv_sem=recv_sems.at[t, k], device_id=dev,
                                                device_id_type=pl.DeviceIdType.MESH)

        for k, (dev, idx) in enumerate(peers):
            for t in range(len(self.arrays)):
                slot_copy(t, k, dev, idx).wait_recv()
        for k, (dev, idx) in enumerate(peers):
            for t in range(len(self.arrays)):
                slot_copy(t, k, dev, idx).wait_send()
        for t in range(len(self.arrays)):
            pltpu.make_async_copy(dst[t].at[me], dst[t].at[me], local_sems.at[t]).wait()

    def gather_by_chip(self, src, dst, sems):
        assert not any(self.split)
        send_sems, recv_sems, local_sems = sems
        n = len(self.arrays)
        x, y, c = lax.axis_index("x"), lax.axis_index("y"), lax.axis_index("c")
        me, sibling = 4 * x + 2 * y + c, (x, y, 1 - c)
        chips = [(1 - x, y), (x, 1 - y), (1 - x, 1 - y)]

        def index(chip, core):
            return 4 * chip[0] + 2 * chip[1] + core

        def copy(t, k, block, to, from_src):
            return pltpu.make_async_remote_copy(src_ref=src[t] if from_src else dst[t].at[block], dst_ref=dst[t].at[block],
                                                send_sem=send_sems.at[t, k], recv_sem=recv_sems.at[t, k],
                                                device_id=to, device_id_type=pl.DeviceIdType.MESH)

        local = [pltpu.make_async_copy(src[t], dst[t].at[me], local_sems.at[t]) for t in range(n)]
        for cp in local:
            cp.start()
        sends = []
        for t in range(n):
            sends.append(copy(t, 0, me, sibling, True))
            sends += [copy(t, 1 + j, me, (*chip, c), True) for j, chip in enumerate(chips)]
        for cp in sends:
            cp.start()
        for j, chip in enumerate(chips):
            for t in range(n):
                copy(t, 1 + j, index(chip, c), sibling, False).wait_recv()
                passed = copy(t, 4 + j, index(chip, c), sibling, False)
                passed.start()
                sends.append(passed)
        for t in range(n):
            copy(t, 0, index((x, y), 1 - c), sibling, False).wait_recv()
            for j, chip in enumerate(chips):
                copy(t, 4 + j, index(chip, 1 - c), sibling, False).wait_recv()
        for cp in sends:
            cp.wait_send()
        for cp in local:
            cp.wait()

    def run(self, name, by_chip=False):
        n = len(self.arrays)

        def body(*refs):
            src, dst, sems = refs[:n], refs[n:2 * n], refs[2 * n:]
            if by_chip:
                self.gather_by_chip(src, dst, sems)
                return
            self.start(src, dst, sems)
            self.wait(src, dst, sems)

        return pl.pallas_call(
            body, name=name, out_shape=self.out_shapes, in_specs=[_ANY] * n, out_specs=[_ANY] * n,
            scratch_shapes=self.scratch,
        )(*self.arrays)


SMALL = [("ln_g", (DEPTH, D), False), ("ln_b", (DEPTH, D), False), ("hgrn_lb_logits", (DEPTH, TOK), False),
         ("a_w_s", (1, NH, HD, HD), False), ("a_b_s", (1, NH, HD), False), ("b_norm_g", (1, TOK), True),
         ("c_w_pool", (1, 4, POOL_GROUP, POOL_GROUP), False), ("c_scale", (1, TOK), True),
         ("d_conv_w", (1, 4, TOK), True), ("d_conv_b", (1, TOK), True),
         ("d_w_gx", (1, NH, HD, HD), False), ("d_b_gx", (1, NH, HD), False),
         ("d_w_ga", (1, NH, HD, HD), False), ("d_b_ga", (1, NH, HD), False), ("d_a_param", (1, TOK), True)]


def _pack(parts, total_rows):
    flat = jnp.concatenate([p.reshape(-1).astype(F32) for p in parts])
    flat = jnp.pad(flat, (0, total_rows * 128 - flat.shape[0]))
    return flat.reshape(total_rows, 128)


def _size(shape):
    n = 1
    for s in shape:
        n *= s
    return n


def _rows_for(n):
    return -(-n // 1024) * 8


def kernel(x, mem, mem_kv_w, ln_g, ln_b, w_out, hgrn_lb_logits, a_w_in, a_w_s, a_b_s, b_w_in, b_norm_g, c_w_in, c_w_pool, c_scale, d_w_in, d_conv_w, d_conv_b, d_w_gx, d_b_gx, d_w_ga, d_b_ga, d_a_param, loss_target, m_mem_kv_w, m_ln_g, m_ln_b, m_w_out, m_hgrn_lb_logits, m_a_w_in, m_a_w_s, m_a_b_s, m_b_w_in, m_b_norm_g, m_c_w_in, m_c_w_pool, m_c_scale, m_d_w_in, m_d_conv_w, m_d_conv_b, m_d_w_gx, m_d_b_gx, m_d_w_ga, m_d_b_ga, m_d_a_param, v_mem_kv_w, v_ln_g, v_ln_b, v_w_out, v_hgrn_lb_logits, v_a_w_in, v_a_w_s, v_a_b_s, v_b_w_in, v_b_norm_g, v_c_w_in, v_c_w_pool, v_c_scale, v_d_w_in, v_d_conv_w, v_d_conv_b, v_d_w_gx, v_d_b_gx, v_d_w_ga, v_d_b_ga, v_d_a_param):
    W = dict(mem_kv_w=mem_kv_w, ln_g=ln_g, ln_b=ln_b, w_out=w_out, hgrn_lb_logits=hgrn_lb_logits, a_w_in=a_w_in, a_w_s=a_w_s,
             a_b_s=a_b_s, b_w_in=b_w_in, b_norm_g=b_norm_g, c_w_in=c_w_in, c_w_pool=c_w_pool, c_scale=c_scale, d_w_in=d_w_in,
             d_conv_w=d_conv_w, d_conv_b=d_conv_b, d_w_gx=d_w_gx, d_b_gx=d_b_gx, d_w_ga=d_w_ga, d_b_ga=d_b_ga, d_a_param=d_a_param)
    M = dict(mem_kv_w=m_mem_kv_w, ln_g=m_ln_g, ln_b=m_ln_b, w_out=m_w_out, hgrn_lb_logits=m_hgrn_lb_logits, a_w_in=m_a_w_in,
             a_w_s=m_a_w_s, a_b_s=m_a_b_s, b_w_in=m_b_w_in, b_norm_g=m_b_norm_g, c_w_in=m_c_w_in, c_w_pool=m_c_w_pool,
             c_scale=m_c_scale, d_w_in=m_d_w_in, d_conv_w=m_d_conv_w, d_conv_b=m_d_conv_b, d_w_gx=m_d_w_gx, d_b_gx=m_d_b_gx,
             d_w_ga=m_d_w_ga, d_b_ga=m_d_b_ga, d_a_param=m_d_a_param)
    V = dict(mem_kv_w=v_mem_kv_w, ln_g=v_ln_g, ln_b=v_ln_b, w_out=v_w_out, hgrn_lb_logits=v_hgrn_lb_logits, a_w_in=v_a_w_in,
             a_w_s=v_a_w_s, a_b_s=v_a_b_s, b_w_in=v_b_w_in, b_norm_g=v_b_norm_g, c_w_in=v_c_w_in, c_w_pool=v_c_w_pool,
             c_scale=v_c_scale, d_w_in=v_d_w_in, d_conv_w=v_d_conv_w, d_conv_b=v_d_conv_b, d_w_gx=v_d_w_gx, d_b_gx=v_d_b_gx,
             d_w_ga=v_d_w_ga, d_b_ga=v_d_b_ga, d_a_param=v_d_a_param)
    me = 4 * lax.axis_index("x") + 2 * lax.axis_index("y") + lax.axis_index("c")
    x2, mem2, tgt2 = x[0], mem[0], loss_target[0]
    in_names = ["a_w_in", "b_w_in", "c_w_in", "d_w_in"]

    shard_names = [n for n, _, sh in SMALL if sh]
    small_shard = _pack([W[n] for n in shard_names], 8)
    wts = _prep_weights([W[n][0] for n in in_names], w_out, mem_kv_w)
    wt_sh, wo_sh, wkv_sh = wts[:4], wts[4:8], wts[8]
    g0 = _Exchange([wt_sh[0], wo_sh[0], wkv_sh, small_shard], [False] * 4).run("gather_first", by_chip=True)
    wt_full = [g0[0].reshape(-1, D)]
    wout_full = [g0[1].reshape(D, D)]
    wkv_full = g0[2].reshape(D, 2 * XW)
    sm = g0[3].reshape(NDEV, 1024)
    full_small = {}
    off = 0
    for n, shape, _ in [s for s in SMALL if s[2]]:
        per = _size(shape) // NDEV
        blk = sm[:, off:off + per]
        if n == "d_conv_w":
            full_small[n] = blk.reshape(NDEV, 4, TOK // NDEV).transpose(1, 0, 2).reshape(4, TOK)
        else:
            full_small[n] = blk.reshape(1, TOK)
        off += per

    ks, vs = _kv_fwd(mem2, wkv_full)
    tri_bs = jnp.broadcast_to(a_b_s[0][:, :, None], (NH, HD, HD))
    wbd = jnp.zeros((TOK, TOK), F32)
    for g in range(4):
        wbd = lax.dynamic_update_slice(wbd, c_w_pool[0, g], (g * POOL_GROUP, g * POOL_GROUP))
    wbd = wbd.astype(MXU)
    prm = {0: [a_w_s[0], tri_bs],
           1: [hgrn_lb_logits, full_small["b_norm_g"]],
           2: [wbd, full_small["c_scale"]],
           3: [full_small["d_conv_w"], full_small["d_conv_b"], d_w_gx[0].astype(MXU), d_b_gx[0].reshape(1, TOK),
               d_w_ga[0].astype(MXU), d_b_ga[0].reshape(1, TOK), full_small["d_a_param"]]}
    ones = jnp.ones((1, D), F32)
    zeros = jnp.zeros((1, D), F32)
    xs, gs, bs = [x2], [ones], [zeros]
    saved = []
    for i in range(DEPTH):
        ride = _Exchange([wt_sh[i + 1], wo_sh[i + 1]], [False, False]) if i + 1 < DEPTH else None
        res = _layer_fwd(i, i, xs[i], gs[i], bs[i], wt_full[i], wout_full[i], ks, vs, prm[i], ride)
        if ride:
            wt_full.append(res[-2].reshape(-1, D))
            wout_full.append(res[-1].reshape(D, D))
            res = res[:-2]
        saved.append(res)
        xs.append(res[1])
        gs.append(ln_g[i:i + 1])
        bs.append(ln_b[i:i + 1])

    up = tgt2
    grads = {}
    dks_l, dvs_l, dwt_l, dwout_l, dlng_l, dlnb_l = [], [], [], [], [], []
    recv_wt, recv_wo = [None] * DEPTH, [None] * DEPTH
    loss_part = None
    sharded = {n for n, _, sh in SMALL if sh}
    group = {3: ["ln_g#3", "ln_b#3", "d_conv_w", "d_conv_b", "d_w_gx", "d_b_gx", "d_w_ga", "d_b_ga", "d_a_param"],
             2: ["ln_g#2", "ln_b#2", "c_w_pool", "c_scale"],
             1: ["ln_g#1", "ln_b#1", "hgrn_lb_logits", "b_norm_g"],
             0: ["ln_g#0", "ln_b#0", "a_w_s", "a_b_s", "loss"]}

    def small_of(l):
        return [grads[e].reshape(-1, grads[e].shape[-1]) for e in group[l]]

    recv_small = [None] * DEPTH
    for i in reversed(range(DEPTH)):
        res = saved[i]
        extra = None if len(res) <= 4 else (res[4] if len(res) == 5 else res[4:])
        ride = None
        if i + 1 < DEPTH:
            smalls = small_of(i + 1)
            ride = _Exchange([dwt_l[-1], dwout_l[-1]] + smalls, [True, True] + [False] * len(smalls))
        out = _layer_bwd(i, i, up, i == DEPTH - 1, res[1], res[2], res[3], gs[i + 1], bs[i + 1], res[0], wout_full[i], ks, vs,
                         prm[i], extra, ride)
        if ride:
            nr = len(ride.arrays)
            recv_wt[i + 1], recv_wo[i + 1], recv_small[i + 1] = out[-nr], out[-nr + 1], out[-nr + 2:]
            out = out[:-nr]
        dres, dproj, dwout_i, dks_i, dvs_i, dg_i, db_i, loss_i = out[:8]
        pg = out[8:]
        if i == DEPTH - 1:
            grads["loss"] = loss_i
        dks_l.append(dks_i)
        dvs_l.append(dvs_i)
        dwout_l.append(dwout_i)
        grads[f"ln_g#{i}"], grads[f"ln_b#{i}"] = dg_i, db_i
        if i == 0:
            grads["a_w_s"], dbs_exp = pg
            grads["a_b_s"] = _bias_finalize(dbs_exp)
        elif i == 1:
            dlb, grads["b_norm_g"] = pg
            grads["hgrn_lb_logits"] = _lb_finalize(dlb, hgrn_lb_logits)
        elif i == 2:
            dwbd, grads["c_scale"] = pg
            grads["c_w_pool"] = jnp.stack([lax.dynamic_slice(dwbd, (g * POOL_GROUP, g * POOL_GROUP), (POOL_GROUP, POOL_GROUP))
                                           for g in range(4)])
        else:
            (grads["d_conv_w"], grads["d_conv_b"], grads["d_w_gx"], grads["d_b_gx"], grads["d_w_ga"], grads["d_b_ga"],
             grads["d_a_param"]) = pg
        ride = None
        if i == 0:
            dwkv = _kv_bwd(mem2, dks_l, dvs_l)
            smalls = small_of(0)
            ride = _Exchange([dwout_i, dwkv] + smalls, [True, True] + [False] * len(smalls))
        pb = _proj_bwd(i, dproj, dres, xs[i], gs[i], bs[i], wt_full[i], ride)
        up, dwt = pb[:2]
        if ride:
            recv_wo[0], recv_kv, recv_small[0] = pb[2], pb[3], pb[4:]
        dwt_l.append(dwt)
    grad_x = up[None]

    recv_wt[0], = _Exchange([dwt_l[-1]], [True]).run("scatter_last")

    outs = {}
    for t, n in enumerate(in_names):
        g, d, mn, vn = _sum_adam(f"adam_{n}", recv_wt[t], W[n][0], M[n][0], V[n][0], True)
        outs[n] = (g[None], d[None], mn[None], vn[None])
    wo_res = [_sum_adam(f"adam_w_out{l}", recv_wo[l], w_out[l], m_w_out[l], v_w_out[l], False) for l in range(DEPTH)]
    outs["w_out"] = tuple(jnp.stack([wo_res[l][j] for l in range(DEPTH)]) for j in range(4))
    outs["mem_kv_w"] = _sum_adam("adam_mem_kv_w", recv_kv, mem_kv_w, m_mem_kv_w, v_mem_kv_w, False)

    def entry_of(tree, e, like):
        if "#" in e:
            n, l = e.split("#")
            return tree[n][int(l):int(l) + 1]
        return tree[e].reshape(like.shape[1:])

    small = [{}, {}, {}, {}]
    for l in range(DEPTH):
        params = [None if (e == "loss" or e in sharded) else tuple(entry_of(t, e, r) for t in (W, M, V))
                  for e, r in zip(group[l], recv_small[l])]
        for e, res in zip(group[l], _group_adam(f"small_adam{l}", recv_small[l], params)):
            for j, a in enumerate(res):
                small[j][e] = a
    loss = small[0]["loss"][0, 0]
    for j in range(4):
        for n in ("ln_g", "ln_b"):
            small[j][n] = jnp.concatenate([small[j][f"{n}#{l}"] for l in range(DEPTH)], axis=0)
    g_small = small[0]
    for n, _, sh in SMALL:
        if not sh:
            outs[n] = tuple(small[j][n] for j in range(4))
    per = TOK // NDEV
    g_sh = {n: lax.dynamic_slice_in_dim(g_small[n], me * per, per, axis=1) for n, s, sh in SMALL if sh}
    gp = _pack([g_sh[n] for n in shard_names], 8)
    d_p, m_p, v_p = _adam_only(gp, small_shard, _pack([M[n] for n in shard_names], 8), _pack([V[n] for n in shard_names], 8))
    o = 0
    for n in shard_names:
        cnt = _size(W[n].shape)
        outs[n] = (g_sh[n],) + tuple(t.reshape(-1)[o:o + cnt].reshape(W[n].shape) for t in (d_p, m_p, v_p))
        o += cnt

    order = ["mem_kv_w", "ln_g", "ln_b", "w_out", "hgrn_lb_logits", "a_w_in", "a_w_s", "a_b_s", "b_w_in", "b_norm_g", "c_w_in",
             "c_w_pool", "c_scale", "d_w_in", "d_conv_w", "d_conv_b", "d_w_gx", "d_b_gx", "d_w_ga", "d_b_ga", "d_a_param"]
    result = [loss, grad_x]
    for j in range(4):
        result += [outs[n][j].reshape(W[n].shape) for n in order]
    return tuple(result)
```

```python
import functools

import jax
import jax.numpy as jnp
from jax import lax
from jax.experimental import pallas as pl
from jax.experimental.pallas import tpu as pltpu

F32 = jnp.float32
MXU = jnp.bfloat16
WIRE = jnp.bfloat16

D = 1024
TOK = 768
XW = 256
NMEM = 256
XHEADS = 4
XSCALE = 64 ** -0.5
NH = 6
HD = 128
CH = 16
POOL_WINDOWS = (2, 4, 8, 16)
POOL_GROUP = 192
DEPTH = 4
ALPHA = (2 * DEPTH) ** 0.25
LN_EPS = 1e-5
RMS_EPS = 1e-6
LRU_C = 8.0
B1, B2, LR, EPS, WD, STEP = 0.9, 0.999, 0.001, 1e-8, 0.01, 10

NDEV = 8
TS_FWD = {0: 512, 1: 256, 2: 512, 3: 256}
TS_BWD = {0: 512, 1: 256, 2: 512, 3: 256}
TSB = 512
VMEM_LIMIT = 58 * 1024 * 1024
VMEM_LIMIT_WIDE = 62 * 1024 * 1024

KIND_WIDTHS = {0: 2 * TOK + XW + D, 1: 3 * TOK + XW + D, 2: TOK + XW + D, 3: TOK + XW + D}


def _mm(a, b, ca, cb):
    return lax.dot_general(a.astype(MXU), b.astype(MXU), (((ca,), (cb,)), ((), ())), preferred_element_type=F32)


def _nn(a, b):
    return _mm(a, b, 1, 0)


def _nt(a, b):
    return _mm(a, b, 1, 1)


def _tn(a, b):
    return _mm(a, b, 0, 0)


def _bmm(a, b, ca, cb):
    return lax.dot_general(a.astype(MXU), b.astype(MXU), (((ca,), (cb,)), ((0,), (0,))), preferred_element_type=F32)


def _sigmoid(x):
    return 1.0 / (1.0 + jnp.exp(-x))


def _vjp1(fn, x, dy):
    return jax.vjp(fn, x)[1](dy)[0]


def _rowsum(x):
    return jnp.sum(x, axis=0, keepdims=True)


def _row(x, r):
    sel = lax.broadcasted_iota(jnp.int32, x.shape, 0) == r
    return jnp.sum(jnp.where(sel, x, 0.0), axis=0, keepdims=True)


def _acc(ref, val):
    ref[...] += val


def _cparams(sem=None, vmem=VMEM_LIMIT):
    return pltpu.CompilerParams(dimension_semantics=sem, vmem_limit_bytes=vmem)


def _res(a):
    nd = a.ndim
    return pl.BlockSpec(a.shape, lambda i: (0,) * nd)


def _res_sds(shape):
    nd = len(shape)
    return pl.BlockSpec(shape, lambda i: (0,) * nd)


def _row_spec(width, nt, rev, ts):
    if rev:
        return pl.BlockSpec((ts, width), lambda i: (nt - 1 - i, 0))
    return pl.BlockSpec((ts, width), lambda i: (i, 0))


def _call(body, name, grid, ins, outs, scratch=(), sem=("arbitrary",), vmem=VMEM_LIMIT):
    arrays = [a for a, _ in ins]
    return pl.pallas_call(
        body, name=name, grid=grid,
        in_specs=[s for _, s in ins],
        out_specs=[s for _, s in outs],
        out_shape=[o for o, _ in outs],
        scratch_shapes=list(scratch),
        compiler_params=_cparams(sem, vmem),
    )(*arrays)


def _xattn_fwd(qx, ks_ref, vs_ref):
    o = None
    ps = []
    for h in range(XHEADS):
        s = _nt(qx, ks_ref[h]) * XSCALE
        s = s - jnp.max(s, axis=-1, keepdims=True)
        e = jnp.exp(s)
        p = e * (1.0 / jnp.sum(e, axis=-1, keepdims=True))
        ps.append(p)
        oh = _nn(p, vs_ref[h])
        o = oh if o is None else o + oh
    return o, ps


def _xattn_bwd(qx, ps, dxo, ks_ref, vs_ref, dks_ref, dvs_ref):
    dq = None
    for h in range(XHEADS):
        p = ps[h]
        dp = _nt(dxo, vs_ref[h])
        ds = p * (dp - jnp.sum(dp * p, axis=-1, keepdims=True))
        dqh = _nn(ds, ks_ref[h]) * XSCALE
        dq = dqh if dq is None else dq + dqh
        dks_ref[h] += _tn(ds, qx) * XSCALE
        dvs_ref[h] += _tn(p, dxo)
    return dq


def _tril128():
    r = lax.broadcasted_iota(jnp.int32, (HD, HD), 0)
    c = lax.broadcasted_iota(jnp.int32, (HD, HD), 1)
    return c <= r


GELU_C = 0.7978845608028654
GELU_K = 0.044715


def _gelu(x):
    th = jnp.tanh(GELU_C * (x + GELU_K * (x * x * x)))
    return 0.5 * x * (1.0 + th), th


def _gelu_grad(x, th):
    return 0.5 * (1.0 + th) + 0.5 * x * (1.0 - th * th) * (GELU_C * (1.0 + 3.0 * GELU_K * (x * x)))


def _gmlp_fwd(u, v, ws_ref, bs_ref):
    ts = u.shape[0]
    ug, thu = _gelu(u)
    vg, thv = _gelu(v)
    tri = _tril128()
    toks, res = [], []
    for g in range(NH):
        sl = slice(g * HD, (g + 1) * HD)
        vgh = vg[:, sl]
        cen = vgh - jnp.mean(vgh, axis=-1, keepdims=True)
        rstd = lax.rsqrt(jnp.mean(cen * cen, axis=-1, keepdims=True) + LN_EPS)
        vn = cen * rstd
        w = jnp.where(tri, ws_ref[g], 0.0).astype(MXU)
        mix = jnp.concatenate([_nn(w, vn[n * HD:(n + 1) * HD]) + bs_ref[g] for n in range(ts // HD)], axis=0)
        toks.append(ug[:, sl] * mix)
        res.append((vn, rstd, mix, w))
    return jnp.concatenate(toks, axis=1), (ug, res, thu, thv)


def _gmlp_bwd(u, v, fres, dtok, dws_ref, dbs_ref):
    ts = u.shape[0]
    ug, res, thu, thv = fres
    tri = _tril128()
    dugs, dvgs = [], []
    for g in range(NH):
        sl = slice(g * HD, (g + 1) * HD)
        vn, rstd, mix, w = res[g]
        dmix = dtok[:, sl] * ug[:, sl]
        dugs.append(dtok[:, sl] * mix)
        dvn_rows = []
        dw = None
        dbs = None
        for n in range(ts // HD):
            dm = dmix[n * HD:(n + 1) * HD]
            dvn_rows.append(_tn(w, dm))
            t = _nt(dm, vn[n * HD:(n + 1) * HD])
            dw = t if dw is None else dw + t
            dbs = dm if dbs is None else dbs + dm
        dws_ref[g] += jnp.where(tri, dw, 0.0)
        dbs_ref[g] += dbs
        dvn = jnp.concatenate(dvn_rows, axis=0)
        dvgs.append(rstd * (dvn - jnp.mean(dvn, axis=-1, keepdims=True) - vn * jnp.mean(dvn * vn, axis=-1, keepdims=True)))
    du = jnp.concatenate(dugs, axis=1) * _gelu_grad(u, thu)
    dv = jnp.concatenate(dvgs, axis=1) * _gelu_grad(v, thv)
    return du, dv


def _chunk_cumsum(x):
    row = lax.broadcasted_iota(jnp.int32, x.shape, 0) % CH
    for s in (1, 2, 4, 8):
        x = x + jnp.where(row >= s, pltpu.roll(x, s, 0), 0.0)
    return x


def _chunk_revcumsum(x):
    n = x.shape[0]
    row = lax.broadcasted_iota(jnp.int32, x.shape, 0) % CH
    for s in (1, 2, 4, 8):
        x = x + jnp.where(row < CH - s, pltpu.roll(x, n - s, 0), 0.0)
    return x


def _chunk_sum(x):
    n, w = x.shape
    return jnp.sum(x.reshape(n // CH, CH, w), axis=1)


def _chunk_bcast(c, n):
    nch, w = c.shape
    return jnp.broadcast_to(c[:, None, :], (nch, CH, w)).reshape(n, w)


def _lower_bound(lb_logits, layer):
    lg = lb_logits
    e = jnp.exp(lg - jnp.max(lg, axis=0, keepdims=True))
    p = e / jnp.sum(e, axis=0, keepdims=True)
    row = lax.broadcasted_iota(jnp.int32, p.shape, 0)
    lb = jnp.sum(jnp.where((row >= 1) & (row <= layer), p, 0.0), axis=0, keepdims=True)
    return lb, p


def _hgrn_prep(q, fl, lb):
    n = q.shape[0]
    sg = _sigmoid(fl)
    f = lb + (1.0 - lb) * sg
    lf = jnp.log(f)
    sq = _sigmoid(q)
    g = _chunk_cumsum(lf)
    tot = _chunk_sum(lf)
    gl = _chunk_bcast(tot, n)
    eg = jnp.exp(g)
    eng = jnp.exp(-g)
    egl = jnp.exp(gl - g)
    k = 1.0 - f
    qf = q * sq
    return dict(sg=sg, f=f, k=k, sq=sq, qf=qf, eg=eg, eng=eng, egl=egl,
                qd=qf * eg, ki=k * eng, ke=k * egl, dch=jnp.exp(tot))


def _hgrn_mask():
    r = lax.broadcasted_iota(jnp.int32, (HD, HD), 0)
    c = lax.broadcasted_iota(jnp.int32, (HD, HD), 1)
    return (r // CH == c // CH) & (c <= r)


def _hgrn_states(v3, ke3, dch_h, st_in):
    nch = v3.shape[0]
    ut = _bmm(v3, ke3, 1, 1)
    dfull = jnp.broadcast_to(dch_h[:, None, :], (nch, HD, HD))
    st, sts = st_in, []
    for c in range(nch):
        sts.append(st)
        st = st * dfull[c] + ut[c]
    return jnp.stack(sts), st, dfull


def _hgrn_fwd(q, fl, inp, lb, ng, st_ref):
    n = q.shape[0]
    nch = n // CH
    pr = _hgrn_prep(q, fl, lb)
    mask = _hgrn_mask()
    toks, o_l, a_l = [], [], []
    qd_m, ki_m, ke_m, v_m = (t.astype(MXU) for t in (pr["qd"], pr["ki"], pr["ke"], inp))
    for h in range(NH):
        sl = slice(h * HD, (h + 1) * HD)
        qd, ki, ke, v = qd_m[:, sl], ki_m[:, sl], ke_m[:, sl], v_m[:, sl]
        qd3 = qd.reshape(nch, CH, HD)
        v3 = v.reshape(nch, CH, HD)
        ke3 = ke.reshape(nch, CH, HD)
        sts, st_ref[h], _ = _hgrn_states(v3, ke3, pr["dch"][:, sl], st_ref[h])
        o = _bmm(qd3, sts, 2, 2).reshape(n, HD)
        intra, scores = [], []
        for b in range(n // HD):
            bs = slice(b * HD, (b + 1) * HD)
            a = jnp.where(mask, _nt(qd[bs], ki[bs]), 0.0).astype(MXU)
            scores.append(a)
            intra.append(_nn(a, v[bs]))
        o = o + jnp.concatenate(intra, axis=0)
        r = lax.rsqrt(jnp.mean(o * o, axis=-1, keepdims=True) + RMS_EPS)
        toks.append(o * r * ng[:, sl])
        o_l.append(o)
        a_l.append(jnp.concatenate(scores, axis=0))
    return jnp.concatenate(toks, axis=1), jnp.concatenate(o_l, axis=1), jnp.concatenate(a_l, axis=1)


def _hgrn_bwd(q, pr, inp, lb, ng, dtok, o_all, a_all, ststart_ref, dst_ref, dng_ref, dlb_ref):
    n = q.shape[0]
    nch = n // CH
    mask = _hgrn_mask()
    dqd_l, dki_l, dke_l, dv_l, ddch_l, dng_l, toks = [], [], [], [], [], [], []
    qd_m, ki_m, ke_m, v_m = (t.astype(MXU) for t in (pr["qd"], pr["ki"], pr["ke"], inp))
    for h in range(NH):
        sl = slice(h * HD, (h + 1) * HD)
        qd, ki, ke, v = qd_m[:, sl], ki_m[:, sl], ke_m[:, sl], v_m[:, sl]
        qd3 = qd.reshape(nch, CH, HD)
        v3 = v.reshape(nch, CH, HD)
        ke3 = ke.reshape(nch, CH, HD)
        sts, _, dfull = _hgrn_states(v3, ke3, pr["dch"][:, sl], ststart_ref[h])
        sts_m = sts.astype(MXU)
        o = o_all[:, sl]
        a_l = [a_all[b * HD:(b + 1) * HD, sl] for b in range(n // HD)]
        r = lax.rsqrt(jnp.mean(o * o, axis=-1, keepdims=True) + RMS_EPS)
        toks.append(o * r * ng[:, sl])
        dt = dtok[:, sl]
        dng_l.append(_rowsum(dt * o * r))
        dn = dt * ng[:, sl]
        do = r * dn - o * (r * r * r) * jnp.mean(dn * o, axis=-1, keepdims=True)
        do_m = do.astype(MXU)
        do3 = do_m.reshape(nch, CH, HD)
        dqd_rows, dki_rows, dv_rows = [], [], []
        for b in range(n // HD):
            bs = slice(b * HD, (b + 1) * HD)
            da = jnp.where(mask, _nt(do_m[bs], v[bs]), 0.0).astype(MXU)
            dqd_rows.append(_nn(da, ki[bs]))
            dki_rows.append(_tn(da, qd[bs]))
            dv_rows.append(_tn(a_l[b], do_m[bs]))
        dqd = jnp.concatenate(dqd_rows, axis=0) + _bmm(do3, sts_m, 2, 1).reshape(n, HD)
        dki = jnp.concatenate(dki_rows, axis=0)
        dv = jnp.concatenate(dv_rows, axis=0)
        wt = _bmm(do3, qd3, 1, 1)
        dst, dstn_l = dst_ref[h], [None] * nch
        for c in reversed(range(nch)):
            dstn_l[c] = dst
            dst = wt[c] + dst * dfull[c]
        dst_ref[h] = dst
        dstn = jnp.stack(dstn_l)
        dstn_m = dstn.astype(MXU)
        dv = dv + _bmm(ke3, dstn_m, 2, 2).reshape(n, HD)
        dke = _bmm(v3, dstn_m, 2, 1).reshape(n, HD)
        ddch_l.append(jnp.sum(sts * dstn, axis=1))
        dqd_l.append(dqd)
        dki_l.append(dki)
        dke_l.append(dke)
        dv_l.append(dv)
    dqd = jnp.concatenate(dqd_l, axis=1)
    dki = jnp.concatenate(dki_l, axis=1)
    dke = jnp.concatenate(dke_l, axis=1)
    dinp = jnp.concatenate(dv_l, axis=1)
    ddch = jnp.concatenate(ddch_l, axis=1)
    _acc(dng_ref, jnp.concatenate(dng_l, axis=1))
    dqf = dqd * pr["eg"]
    dke_ke = dke * pr["ke"]
    dg = dqd * pr["qd"] - dki * pr["ki"] - dke_ke
    dk = dki * pr["eng"] + dke * pr["egl"]
    dgl = _chunk_sum(dke_ke) + ddch * pr["dch"]
    dlf = _chunk_revcumsum(dg) + _chunk_bcast(dgl, n)
    df = dlf / pr["f"] - dk
    sg = pr["sg"]
    dfl = df * (1.0 - lb) * sg * (1.0 - sg)
    _acc(dlb_ref, _rowsum(df * (1.0 - sg)))
    sq = pr["sq"]
    dq = dqf * (sq * (1.0 + q * (1.0 - sq)))
    return jnp.concatenate(toks, axis=1), dq, dfl, dinp


def _pool_select(s2, s4, s8, s16):
    col = lax.broadcasted_iota(jnp.int32, (1, TOK), 1)
    return jnp.where(col < POOL_GROUP, s2, jnp.where(col < 2 * POOL_GROUP, s4, jnp.where(col < 3 * POOL_GROUP, s8, s16)))


def _pool_cnt(pos0, n):
    pos = pos0 + lax.broadcasted_iota(jnp.int32, (n, TOK), 0) + 1
    col = lax.broadcasted_iota(jnp.int32, (n, TOK), 1)
    w = jnp.where(col < POOL_GROUP, 2, jnp.where(col < 2 * POOL_GROUP, 4, jnp.where(col < 3 * POOL_GROUP, 8, 16)))
    return jnp.minimum(pos, w).astype(F32)


def _pool_fwd(p, halo, pos0, wbd, scale):
    n = p.shape[0]
    ext = jnp.concatenate([halo, p], axis=0)
    s2 = ext + pltpu.roll(ext, 1, 0)
    s4 = s2 + pltpu.roll(s2, 2, 0)
    s8 = s4 + pltpu.roll(s4, 4, 0)
    s16 = s8 + pltpu.roll(s8, 8, 0)
    win = _pool_select(s2, s4, s8, s16)[16:]
    cnt = _pool_cnt(pos0, n)
    diff = win / cnt - p
    y = _nn(diff, wbd)
    return y * scale, (diff, y, cnt)


def _pool_bwd(fres, dtok, nxt_ref, wbd, scale, dwbd_ref, dscale_ref):
    diff, y, cnt = fres
    n = diff.shape[0]
    _acc(dscale_ref, _rowsum(dtok * y))
    dy = dtok * scale
    ddiff = _nt(dy, wbd)
    dwbd_ref[...] += _tn(diff, dy)
    qv = ddiff / cnt
    ext = jnp.concatenate([qv, nxt_ref[...]], axis=0)
    m = n + 16
    s2 = ext + pltpu.roll(ext, m - 1, 0)
    s4 = s2 + pltpu.roll(s2, m - 2, 0)
    s8 = s4 + pltpu.roll(s4, m - 4, 0)
    s16 = s8 + pltpu.roll(s8, m - 8, 0)
    adj = _pool_select(s2, s4, s8, s16)[:n]
    nxt_ref[...] = qv[:16]
    return adj - ddiff


def _neg_expm1(x):
    return jnp.where(jnp.abs(x) < 1e-2, -x * (1.0 + x * (0.5 + x * (1.0 / 6.0))), 1.0 - jnp.exp(x))


def _softplus_neg(ap):
    return jnp.maximum(-ap, 0.0) + jnp.log(1.0 + jnp.exp(-jnp.abs(ap)))


def _lru_gates(xc, zx, za, ap, first):
    gx = _sigmoid(zx)
    ga = _sigmoid(za)
    sp = _softplus_neg(ap)
    log_a = -LRU_C * ga * sp
    a = jnp.exp(log_a)
    mult = jnp.sqrt(_neg_expm1(2.0 * log_a))
    mult = jnp.where(first, 1.0, mult)
    return a, mult * gx * xc, (gx, ga, sp, mult)


def _scan_fwd(a, b, h0):
    n = a.shape[0]
    row = lax.broadcasted_iota(jnp.int32, a.shape, 0)
    s = 1
    while s < n:
        keep = row >= s
        b = b + a * jnp.where(keep, pltpu.roll(b, s, 0), 0.0)
        a = a * jnp.where(keep, pltpu.roll(a, s, 0), 1.0)
        s *= 2
    return b + a * h0


def _scan_bwd(an, d, dh_next):
    n = an.shape[0]
    row = lax.broadcasted_iota(jnp.int32, an.shape, 0)
    s = 1
    while s < n:
        keep = row < n - s
        d = d + an * jnp.where(keep, pltpu.roll(d, n - s, 0), 0.0)
        an = an * jnp.where(keep, pltpu.roll(an, n - s, 0), 1.0)
        s *= 2
    return d + an * dh_next


def _lru_conv(xb, halo, cw_ref, cb):
    ext = jnp.concatenate([halo, xb], axis=0)
    sh = [pltpu.roll(ext, 3 - j, 0)[8:] if j < 3 else xb for j in range(4)]
    xc = cb
    for j in range(4):
        xc = xc + cw_ref[pl.ds(j, 1), :] * sh[j]
    return xc, sh


def _lru_fwd(xb, halo, pos0, prm, h0):
    cw, cb, wgx, bgx, wga, bga, ap = prm
    n = xb.shape[0]
    xc, sh = _lru_conv(xb, halo, cw, cb[...])
    zx = jnp.concatenate([_nn(xc[:, h * HD:(h + 1) * HD], wgx[h]) for h in range(NH)], axis=1) + bgx[...]
    za = jnp.concatenate([_nn(xc[:, h * HD:(h + 1) * HD], wga[h]) for h in range(NH)], axis=1) + bga[...]
    first = (pos0 + lax.broadcasted_iota(jnp.int32, (n, 1), 0)) == 0
    a, b, gates = _lru_gates(xc, zx, za, ap[...], first)
    hseq = _scan_fwd(a, b, h0)
    return hseq, (xc, sh, gates, first, a)


def _lru_bwd(fres, hseq, h0, dtok, prm, carry_refs, grad_refs):
    cw, cb, wgx, bgx, wga, bga, ap = prm
    xc, sh, (gx, ga, sp, mult), first, a = fres
    anext_ref, dhnext_ref, dxcnext_ref = carry_refs
    dcw_ref, dcb_ref, dwgx_ref, dbgx_ref, dwga_ref, dbga_ref, dap_ref = grad_refs
    n = xc.shape[0]
    an = jnp.where(lax.broadcasted_iota(jnp.int32, a.shape, 0) == n - 1, anext_ref[...], pltpu.roll(a, n - 1, 0))
    dh = _scan_bwd(an, dtok, dhnext_ref[...])
    hprev = jnp.where(lax.broadcasted_iota(jnp.int32, hseq.shape, 0) == 0, h0, pltpu.roll(hseq, 1, 0))
    da = dh * hprev
    anext_ref[...] = _row(a, 0)
    dhnext_ref[...] = _row(dh, 0)
    t = dh * xc
    dxc = dh * mult * gx
    dzx = t * mult * gx * (1.0 - gx)
    dlog_a = da * a - jnp.where(first, 0.0, t * gx * (a * a) / mult)
    dza = dlog_a * (-LRU_C * sp) * ga * (1.0 - ga)
    dap = _rowsum(dlog_a * ga) * (LRU_C * _sigmoid(-ap[...]))
    _acc(dap_ref, dap)
    _acc(dbgx_ref, _rowsum(dzx))
    _acc(dbga_ref, _rowsum(dza))
    parts = []
    for h in range(NH):
        sl = slice(h * HD, (h + 1) * HD)
        parts.append(_nt(dzx[:, sl], wgx[h]) + _nt(dza[:, sl], wga[h]))
        dwgx_ref[h] += _tn(xc[:, sl], dzx[:, sl])
        dwga_ref[h] += _tn(xc[:, sl], dza[:, sl])
    dxc = dxc + jnp.concatenate(parts, axis=1)
    _acc(dcb_ref, _rowsum(dxc))
    for j in range(4):
        dcw_ref[pl.ds(j, 1), :] += _rowsum(dxc * sh[j])
    ext = jnp.concatenate([dxc, dxcnext_ref[...]], axis=0)
    m = n + 8
    dxb = cw[pl.ds(3, 1), :] * dxc
    for j in range(3):
        dxb = dxb + cw[pl.ds(j, 1), :] * pltpu.roll(ext, m - (3 - j), 0)[:n]
    dxcnext_ref[...] = dxc[:8]
    return dxb


def _layer_fwd(kind, layer, xprev, gprev, bprev, wt, wout, ks, vs, prm, ride=None):
    TS = TS_FWD[kind]
    _rows = functools.partial(_row_spec, ts=TS)
    S = xprev.shape[0]
    nt = S // TS
    N = wt.shape[0]
    nprm = len(prm)
    nch = TS // CH

    outs = [(jax.ShapeDtypeStruct((S, N), F32), _rows(N, nt, False)),
            (jax.ShapeDtypeStruct((S, D), F32), _rows(D, nt, False)),
            (jax.ShapeDtypeStruct((S, 1), F32), _rows(1, nt, False)),
            (jax.ShapeDtypeStruct((S, XHEADS * NMEM), MXU), _rows(XHEADS * NMEM, nt, False))]
    scratch = []
    if kind == 1:
        outs.append((jax.ShapeDtypeStruct((nt, NH, HD, HD), F32), pl.BlockSpec((None, NH, HD, HD), lambda i: (i, 0, 0, 0))))
        outs.append((jax.ShapeDtypeStruct((S, TOK), F32), _rows(TOK, nt, False)))
        outs.append((jax.ShapeDtypeStruct((S, TOK), MXU), _rows(TOK, nt, False)))
        scratch = [pltpu.VMEM((NH, HD, HD), F32)]
    elif kind == 2:
        outs += [(jax.ShapeDtypeStruct((S, TOK), F32), _rows(TOK, nt, False))] * 2
        scratch = [pltpu.VMEM((16, TOK), F32)]
    elif kind == 3:
        outs.append((jax.ShapeDtypeStruct((nt * 8, TOK), F32), pl.BlockSpec((8, TOK), lambda i: (i, 0))))
        outs += [(jax.ShapeDtypeStruct((S, TOK), F32), _rows(TOK, nt, False))] * 4
        scratch = [pltpu.VMEM((8, TOK), F32), pltpu.VMEM((1, TOK), F32)]
    nout = len(outs)
    nscr = len(scratch)
    nride = len(ride.arrays) if ride else 0

    def body(*refs):
        x_ref, g_ref, b_ref, wt_ref, wout_ref, ks_ref, vs_ref = refs[:7]
        prm_refs = refs[7:7 + nprm]
        nin = 7 + nprm + nride
        ride_src = refs[7 + nprm:nin]
        out_refs = refs[nin:nin + nout]
        ride_dst = refs[nin + nout:nin + nout + nride]
        scr = refs[nin + nout + nride:nin + nout + nride + nscr]
        ride_sems = refs[nin + nout + nride + nscr:]
        proj_ref, xhat_ref, rstd_ref = out_refs[:3]
        i = pl.program_id(0)
        if ride:
            @pl.when(i == 0)
            def _():
                ride.start(ride_src, ride_dst, ride_sems)

        xin = x_ref[...] * g_ref[...] + b_ref[...]
        proj = _nt(xin, wt_ref[...])
        proj_ref[...] = proj
        if kind == 0:
            tok, _ = _gmlp_fwd(proj[:, :TOK], proj[:, TOK:2 * TOK], prm_refs[0], prm_refs[1])
        elif kind == 1:
            st_ref, = scr

            @pl.when(i == 0)
            def _():
                st_ref[...] = jnp.zeros_like(st_ref)

            out_refs[4][...] = st_ref[...]
            lb, _ = _lower_bound(prm_refs[0][...], layer)
            tok, out_refs[5][...], out_refs[6][...] = _hgrn_fwd(proj[:, :TOK], proj[:, TOK:2 * TOK], proj[:, 2 * TOK:3 * TOK],
                                                                lb, prm_refs[1][...], st_ref)
        elif kind == 2:
            halo_ref, = scr

            @pl.when(i == 0)
            def _():
                halo_ref[...] = jnp.zeros_like(halo_ref)

            p = proj[:, :TOK]
            tok, (diff, y, _) = _pool_fwd(p, halo_ref[...], i * TS, prm_refs[0][...], prm_refs[1][...])
            out_refs[4][...] = diff
            out_refs[5][...] = y
            halo_ref[...] = p[TS - 16:]
        else:
            halo_ref, h_ref = scr

            @pl.when(i == 0)
            def _():
                halo_ref[...] = jnp.zeros_like(halo_ref)
                h_ref[...] = jnp.zeros_like(h_ref)

            out_refs[4][...] = jnp.broadcast_to(h_ref[...], (8, TOK))
            xb = proj[:, :TOK]
            tok, fres = _lru_fwd(xb, halo_ref[...], i * TS, prm_refs, h_ref[...])
            gx, ga, _, mult = fres[2]
            for r, val in zip(out_refs[5:9], (tok, gx, ga, mult)):
                r[...] = val
            halo_ref[...] = xb[TS - 8:]
            h_ref[...] = _row(tok, TS - 1)
        qx = proj[:, N - D - XW:N - D]
        gate = proj[:, N - D:]
        xo, ps = _xattn_fwd(qx, ks_ref, vs_ref)
        out_refs[3][...] = jnp.concatenate(ps, axis=1).astype(MXU)
        mixed = jnp.concatenate([tok, xo], axis=1) * (gate * _sigmoid(gate))
        z = ALPHA * xin + _nn(mixed, wout_ref[...])
        cen = z - jnp.mean(z, axis=-1, keepdims=True)
        rstd = lax.rsqrt(jnp.mean(cen * cen, axis=-1, keepdims=True) + LN_EPS)
        xhat_ref[...] = cen * rstd
        rstd_ref[...] = rstd
        if ride:
            @pl.when(i == nt - 1)
            def _():
                ride.wait(ride_src, ride_dst, ride_sems)

    ins = [(xprev, _rows(D, nt, False)), (gprev, _res(gprev)), (bprev, _res(bprev)), (wt, _res(wt)), (wout, _res(wout)),
           (ks, _res(ks)), (vs, _res(vs))] + [(p, _res(p)) for p in prm]
    if ride:
        ins += [(a, _ANY) for a in ride.arrays]
        outs += [(s, _ANY) for s in ride.out_shapes]
        scratch = scratch + ride.scratch
    return _call(body, f"layer{layer}_fwd", (nt,), ins, outs, scratch)


def _layer_bwd(kind, layer, up, is_last, xhat, rstd, probs, g_i, b_i, proj, wout, ks, vs, prm, extra, ride=None):
    TS = TS_BWD[kind]
    _rows = functools.partial(_row_spec, ts=TS)
    S = xhat.shape[0]
    nt = S // TS
    N = proj.shape[1]
    nprm = len(prm)
    nch = TS // CH

    ins = [(up, _rows(D, nt, True)), (xhat, _rows(D, nt, True)), (rstd, _rows(1, nt, True)), (g_i, _res(g_i)), (b_i, _res(b_i)),
           (proj, _rows(N, nt, True)), (wout, _res(wout)), (ks, _res(ks)), (vs, _res(vs)),
           (probs, _rows(XHEADS * NMEM, nt, True))] + [(p, _res(p)) for p in prm]
    nfixed = 10
    if kind == 1:
        ins.append((extra[0], pl.BlockSpec((None, NH, HD, HD), lambda i: (nt - 1 - i, 0, 0, 0))))
        ins += [(e, _rows(TOK, nt, True)) for e in extra[1:]]
    elif kind == 2:
        ins += [(e, _rows(TOK, nt, True)) for e in extra]
    elif kind == 3:
        hb = TS // 8
        ins.append((proj, pl.BlockSpec((8, TOK), lambda i: (jnp.maximum((nt - 1 - i) * hb - 1, 0), 0))))
        ins.append((extra[0], pl.BlockSpec((8, TOK), lambda i: (nt - 1 - i, 0))))
        ins += [(e, _rows(TOK, nt, True)) for e in extra[1:]]
    nin = len(ins)

    def acc(shape):
        return (jax.ShapeDtypeStruct(shape, F32), _res_sds(shape))

    outs = [(jax.ShapeDtypeStruct((S, D), F32), _rows(D, nt, True)),
            (jax.ShapeDtypeStruct((S, N), MXU), _rows(N, nt, True)),
            (jax.ShapeDtypeStruct((D, D), WIRE), _res_sds((D, D))),
            acc((XHEADS, NMEM, XW)), acc((XHEADS, NMEM, XW)), acc((1, D)), acc((1, D)), acc((1, HD))]
    scratch = []
    if kind == 0:
        outs += [acc((NH, HD, HD)), acc((NH, HD, HD))]
    elif kind == 1:
        outs += [acc((1, TOK)), acc((1, TOK))]
        scratch = [pltpu.VMEM((NH, HD, HD), F32)]
    elif kind == 2:
        outs += [acc((TOK, TOK)), acc((1, TOK))]
        scratch = [pltpu.VMEM((16, TOK), F32)]
    else:
        outs += [acc((4, TOK)), acc((1, TOK)), acc((NH, HD, HD)), acc((1, TOK)), acc((NH, HD, HD)), acc((1, TOK)), acc((1, TOK))]
        scratch = [pltpu.VMEM((1, TOK), F32), pltpu.VMEM((1, TOK), F32), pltpu.VMEM((8, TOK), F32)]
    scratch = scratch + [pltpu.VMEM((D, D), F32)]
    nout = len(outs)
    nscr = len(scratch)
    nride = len(ride.arrays) if ride else 0

    def body(*refs):
        up_ref, xhat_ref, rstd_ref, g_ref, b_ref, proj_ref, wout_ref, ks_ref, vs_ref, probs_ref = refs[:nfixed]
        prm_refs = refs[nfixed:nfixed + nprm]
        ext_refs = refs[nfixed + nprm:nin]
        ride_src = refs[nin:nin + nride]
        o0 = nin + nride
        out_refs = refs[o0:o0 + nout]
        ride_dst = refs[o0 + nout:o0 + nout + nride]
        scr = refs[o0 + nout + nride:o0 + nout + nride + nscr - 1]
        dwout_acc = refs[o0 + nout + nride + nscr - 1]
        ride_sems = refs[o0 + nout + nride + nscr:]
        dres_ref, dproj_ref, dwout_ref, dks_ref, dvs_ref, dg_ref, db_ref, loss_ref = out_refs[:8]
        pgrad = out_refs[8:]
        i = pl.program_id(0)
        tile = nt - 1 - i

        @pl.when(i == 0)
        def _():
            if ride:
                ride.start(ride_src, ride_dst, ride_sems)
            for r in out_refs[3:]:
                r[...] = jnp.zeros_like(r)
            dwout_acc[...] = jnp.zeros_like(dwout_acc)
            for r in scr:
                if kind != 1 or r is scr[0]:
                    r[...] = jnp.zeros_like(r)

        xhat_v = xhat_ref[...]
        if is_last:
            err = xhat_v * g_ref[...] + b_ref[...] - up_ref[...]
            dxo = err * (1.0 / D)
            loss_ref[...] += jnp.sum(0.5 * jnp.mean(err * err, axis=-1, keepdims=True), axis=0, keepdims=True)
        else:
            dxo = up_ref[...]
        _acc(dg_ref, _rowsum(dxo * xhat_v))
        _acc(db_ref, _rowsum(dxo))
        dxh = dxo * g_ref[...]
        dz = rstd_ref[...] * (dxh - jnp.mean(dxh, axis=-1, keepdims=True)
                              - xhat_v * jnp.mean(dxh * xhat_v, axis=-1, keepdims=True))
        dres_ref[...] = ALPHA * dz

        proj = proj_ref[...]
        qx = proj[:, N - D - XW:N - D]
        gate = proj[:, N - D:]
        sgate = _sigmoid(gate)
        silu = gate * sgate
        dmixed = _nt(dz, wout_ref[...])
        dcat = dmixed * silu
        dtok = dcat[:, :TOK]
        if kind == 0:
            u, v = proj[:, :TOK], proj[:, TOK:2 * TOK]
            tok, fres = _gmlp_fwd(u, v, prm_refs[0], prm_refs[1])
            du, dv = _gmlp_bwd(u, v, fres, dtok, pgrad[0], pgrad[1])
            dproj_ref[:, :TOK] = du.astype(MXU)
            dproj_ref[:, TOK:2 * TOK] = dv.astype(MXU)
        elif kind == 1:
            dst_ref, = scr
            lb, _ = _lower_bound(prm_refs[0][...], layer)
            q, fl, inp = proj[:, :TOK], proj[:, TOK:2 * TOK], proj[:, 2 * TOK:3 * TOK]
            pr = _hgrn_prep(q, fl, lb)
            tok, dq, dfl, dinp = _hgrn_bwd(q, pr, inp, lb, prm_refs[1][...], dtok, ext_refs[1][...], ext_refs[2][...],
                                           ext_refs[0], dst_ref, pgrad[1], pgrad[0])
            dproj_ref[:, :TOK] = dq.astype(MXU)
            dproj_ref[:, TOK:2 * TOK] = dfl.astype(MXU)
            dproj_ref[:, 2 * TOK:3 * TOK] = dinp.astype(MXU)
        elif kind == 2:
            diff, y = ext_refs[0][...], ext_refs[1][...]
            tok = y * prm_refs[1][...]
            fres = (diff, y, _pool_cnt(tile * TS, TS))
            dp = _pool_bwd(fres, dtok, scr[0], prm_refs[0][...], prm_refs[1][...], pgrad[0], pgrad[1])
            dproj_ref[:, :TOK] = dp.astype(MXU)
        else:
            xb = proj[:, :TOK]
            halo = jnp.where(tile == 0, 0.0, ext_refs[0][...])
            h0 = ext_refs[1][0:1]
            tok, gx, ga, mult = (r[...] for r in ext_refs[2:6])
            xc, sh = _lru_conv(xb, halo, prm_refs[0], prm_refs[1][...])
            sp = _softplus_neg(prm_refs[6][...])
            first = (tile * TS + lax.broadcasted_iota(jnp.int32, (TS, 1), 0)) == 0
            fres = (xc, sh, (gx, ga, sp, mult), first, jnp.exp(-LRU_C * ga * sp))
            dxb = _lru_bwd(fres, tok, h0, dtok, prm_refs, scr, pgrad)
            dproj_ref[:, :TOK] = dxb.astype(MXU)
        ps = [probs_ref[:, h * NMEM:(h + 1) * NMEM].astype(F32) for h in range(XHEADS)]
        xo = _nn(ps[0], vs_ref[0])
        for h in range(1, XHEADS):
            xo = xo + _nn(ps[h], vs_ref[h])
        dqx = _xattn_bwd(qx, ps, dcat[:, TOK:], ks_ref, vs_ref, dks_ref, dvs_ref)
        cat = jnp.concatenate([tok, xo], axis=1)
        dwout_acc[...] += _tn(cat * silu, dz)
        dgate = dmixed * cat * (sgate * (1.0 + gate * (1.0 - sgate)))
        dproj_ref[:, N - D - XW:N - D] = dqx.astype(MXU)
        dproj_ref[:, N - D:] = dgate.astype(MXU)

        @pl.when(i == nt - 1)
        def _():
            dwout_ref[...] = dwout_acc[...].astype(WIRE)
            if ride:
                ride.wait(ride_src, ride_dst, ride_sems)

    if ride:
        ins += [(a, _ANY) for a in ride.arrays]
        outs += [(s, _ANY) for s in ride.out_shapes]
        scratch = scratch + ride.scratch
    return _call(body, f"layer{layer}_bwd", (nt,), ins, outs, scratch, vmem=VMEM_LIMIT_WIDE if kind == 0 else VMEM_LIMIT)


def _proj_bwd(layer, dproj, dres, xprev, gprev, bprev, wt, ride=None):
    S = xprev.shape[0]
    nt = S // TSB
    N = wt.shape[0]

    nride = len(ride.arrays) if ride else 0

    def body(*refs):
        dproj_ref, dres_ref, x_ref, g_ref, b_ref, wt_ref = refs[:6]
        ride_src = refs[6:6 + nride]
        dx_ref, dwt_ref = refs[6 + nride:8 + nride]
        ride_dst = refs[8 + nride:8 + 2 * nride]
        acc_ref = refs[8 + 2 * nride]
        ride_sems = refs[9 + 2 * nride:]

        @pl.when(pl.program_id(0) == 0)
        def _():
            if ride:
                ride.start(ride_src, ride_dst, ride_sems)
            acc_ref[...] = jnp.zeros_like(acc_ref)

        dp = dproj_ref[...]
        xin = x_ref[...] * g_ref[...] + b_ref[...]
        dx_ref[...] = dres_ref[...] + _nn(dp, wt_ref[...])
        acc_ref[...] += _tn(dp, xin)

        @pl.when(pl.program_id(0) == nt - 1)
        def _():
            dwt_ref[...] = acc_ref[...].astype(WIRE)
            if ride:
                ride.wait(ride_src, ride_dst, ride_sems)

    ins = [(dproj, _row_spec(N, nt, False, TSB)), (dres, _row_spec(D, nt, False, TSB)), (xprev, _row_spec(D, nt, False, TSB)),
           (gprev, _res(gprev)), (bprev, _res(bprev)), (wt, _res(wt))]
    outs = [(jax.ShapeDtypeStruct((S, D), F32), _row_spec(D, nt, False, TSB)),
            (jax.ShapeDtypeStruct((N, D), WIRE), _res_sds((N, D)))]
    scratch = [pltpu.VMEM((N, D), F32)]
    if ride:
        ins += [(a, _ANY) for a in ride.arrays]
        outs += [(s, _ANY) for s in ride.out_shapes]
        scratch = scratch + ride.scratch
    return _call(body, f"layer{layer}_projbwd", (nt,), ins, outs, scratch)


def _head_mask(h):
    col = lax.broadcasted_iota(jnp.int32, (1, XW), 1)
    return (col // 64) == h


def _kv_fwd(mem, wkv):
    def body(mem_ref, w_ref, ks_ref, vs_ref):
        kv = _nn(mem_ref[...], w_ref[...])
        k, v = kv[:, :XW], kv[:, XW:]
        for h in range(XHEADS):
            ks_ref[h] = jnp.where(_head_mask(h), k, 0.0).astype(MXU)
            vs_ref[h] = jnp.where(_head_mask(h), v, 0.0).astype(MXU)

    sds = jax.ShapeDtypeStruct((XHEADS, NMEM, XW), MXU)
    return pl.pallas_call(body, name="kv_fwd", out_shape=(sds, sds), compiler_params=_cparams())(mem, wkv)


def _kv_bwd(mem, dks_l, dvs_l):
    def body(mem_ref, *refs):
        dks_refs, dvs_refs, out_ref = refs[:DEPTH], refs[DEPTH:2 * DEPTH], refs[2 * DEPTH]
        dk = jnp.zeros((NMEM, XW), F32)
        dv = jnp.zeros((NMEM, XW), F32)
        for h in range(XHEADS):
            m = _head_mask(h)
            for l in range(DEPTH):
                dk = dk + jnp.where(m, dks_refs[l][h], 0.0)
                dv = dv + jnp.where(m, dvs_refs[l][h], 0.0)
        out_ref[...] = _tn(mem_ref[...], jnp.concatenate([dk, dv], axis=1)).astype(WIRE)

    return pl.pallas_call(body, name="kv_bwd", out_shape=jax.ShapeDtypeStruct((D, 2 * XW), WIRE),
                          compiler_params=_cparams())(mem, *dks_l, *dvs_l)


def _prep_weights(w_ins, w_out, wkv):
    def body(a_ref, b_ref, c_ref, d_ref, wo_ref, kv_ref, ao, bo, co, do, wo0, wo1, wo2, wo3, kvo):
        for src, dst in ((a_ref, ao), (b_ref, bo), (c_ref, co), (d_ref, do)):
            dst[...] = src[...].T.astype(MXU)
        for l, dst in enumerate((wo0, wo1, wo2, wo3)):
            dst[...] = wo_ref[l].astype(MXU)
        kvo[...] = kv_ref[...].astype(MXU)

    outs = [jax.ShapeDtypeStruct((w.shape[1], w.shape[0]), MXU) for w in w_ins]
    outs += [jax.ShapeDtypeStruct(w_out.shape[1:], MXU)] * DEPTH + [jax.ShapeDtypeStruct(wkv.shape, MXU)]
    return pl.pallas_call(body, name="prep_weights", out_shape=outs, compiler_params=_cparams())(*w_ins, w_out, wkv)


def _adam_math(w, g, m, v):
    m = B1 * m + (1.0 - B1) * g
    v = B2 * v + (1.0 - B2) * (g * g)
    m_hat = m / (1.0 - B1 ** STEP)
    v_hat = v / (1.0 - B2 ** STEP)
    delta = -LR * (m_hat / (jnp.sqrt(v_hat) + EPS) + WD * w)
    return delta, m, v


def _sum_adam(name, recv, w, m, v, transpose):
    rows, cols = recv.shape[1], recv.shape[2]

    def body(r_ref, w_ref, m_ref, v_ref, g_out, d_out, m_out, v_out, acc_ref):
        s = pl.program_id(0)

        @pl.when(s == 0)
        def _():
            acc_ref[...] = r_ref[...].astype(F32)

        @pl.when(s > 0)
        def _():
            acc_ref[...] += r_ref[...].astype(F32)

        @pl.when(s == NDEV - 1)
        def _():
            g = acc_ref[...].T if transpose else acc_ref[...]
            d, mn, vn = _adam_math(w_ref[...], g, m_ref[...], v_ref[...])
            g_out[...] = g
            d_out[...] = d
            m_out[...] = mn
            v_out[...] = vn

    sds = jax.ShapeDtypeStruct(w.shape, F32)
    ins = [(recv, pl.BlockSpec((None, rows, cols), lambda s: (s, 0, 0))), (w, _res(w)), (m, _res(m)), (v, _res(v))]
    outs = [(sds, _res_sds(w.shape))] * 4
    return _call(body, name, (NDEV,), ins, outs, [pltpu.VMEM((rows, cols), F32)])


def _bias_finalize(dbs_exp):
    def body(dbs_ref, dabs_ref):
        dabs_ref[...] = jnp.sum(dbs_ref[...], axis=-1)

    return pl.pallas_call(body, name="bias_finalize", out_shape=jax.ShapeDtypeStruct((NH, HD), F32),
                          compiler_params=_cparams())(dbs_exp)


def _lb_finalize(dlb, lb_logits):
    def body(dlb_ref, lg_ref, dlg_ref):
        total = jnp.zeros((DEPTH, TOK), F32)
        lg = lg_ref[...]
        e = jnp.exp(lg - jnp.max(lg, axis=0, keepdims=True))
        p = e / jnp.sum(e, axis=0, keepdims=True)
        row = lax.broadcasted_iota(jnp.int32, (DEPTH, TOK), 0)
        for layer in range(DEPTH):
            if layer % 4 != 1:
                continue
            dp = jnp.where((row >= 1) & (row <= layer), dlb_ref[...], 0.0)
            total = total + p * (dp - jnp.sum(p * dp, axis=0, keepdims=True))
        dlg_ref[...] = total

    return pl.pallas_call(body, name="lb_finalize", out_shape=jax.ShapeDtypeStruct((DEPTH, TOK), F32),
                          compiler_params=_cparams())(dlb, lb_logits)


def _small_sum_adam(name, gathered, w, m, v):
    rows = w.shape[0]

    def body(r_ref, w_ref, m_ref, v_ref, g_out, d_out, m_out, v_out):
        g = r_ref[0]
        for s in range(1, NDEV):
            g = g + r_ref[s]
        d, mn, vn = _adam_math(w_ref[...], g, m_ref[...], v_ref[...])
        g_out[...] = g
        d_out[...] = d
        m_out[...] = mn
        v_out[...] = vn

    sds = jax.ShapeDtypeStruct((rows, 128), F32)
    return pl.pallas_call(body, name=name, out_shape=(sds,) * 4, compiler_params=_cparams())(gathered, w, m, v)


def _group_adam(name, recvs, params):
    nk = len(recvs)

    def body(*refs):
        pos, oi = nk, nk + 3 * sum(p is not None for p in params)
        for k in range(nk):
            g = refs[k][0]
            for s in range(1, NDEV):
                g = g + refs[k][s]
            refs[oi][...] = g
            oi += 1
            if params[k] is not None:
                d, mn, vn = _adam_math(refs[pos][...], g, refs[pos + 1][...], refs[pos + 2][...])
                refs[oi][...] = d
                refs[oi + 1][...] = mn
                refs[oi + 2][...] = vn
                pos += 3
                oi += 3

    out_shape, counts = [], []
    for k in range(nk):
        counts.append(4 if params[k] is not None else 1)
        out_shape += [jax.ShapeDtypeStruct(recvs[k].shape[1:], F32)] * counts[-1]
    args = list(recvs) + [a for p in params if p is not None for a in p]
    flat = pl.pallas_call(body, name=name, out_shape=out_shape, compiler_params=_cparams())(*args)
    res, o = [], 0
    for cnt in counts:
        res.append(flat[o:o + cnt])
        o += cnt
    return res


def _adam_only(g, w, m, v):
    def body(g_ref, w_ref, m_ref, v_ref, d_out, m_out, v_out):
        d, mn, vn = _adam_math(w_ref[...], g_ref[...], m_ref[...], v_ref[...])
        d_out[...] = d
        m_out[...] = mn
        v_out[...] = vn

    sds = jax.ShapeDtypeStruct(w.shape, F32)
    return pl.pallas_call(body, name="shard_adam", out_shape=(sds,) * 3, compiler_params=_cparams())(g, w, m, v)


def _me_and_peers():
    x, y, c = lax.axis_index("x"), lax.axis_index("y"), lax.axis_index("c")
    me = 4 * x + 2 * y + c
    peers = []
    for k in range(1, NDEV):
        kx, ky, kc = (k >> 2) & 1, (k >> 1) & 1, k & 1
        px = x + kx - 2 * x * kx
        py = y + ky - 2 * y * ky
        pc = c + kc - 2 * c * kc
        peers.append(((px, py, pc), 4 * px + 2 * py + pc))
    return me, peers


_ANY = pl.BlockSpec(memory_space=pl.ANY)


class _Exchange:
    def __init__(self, arrays, split):
        self.arrays = list(arrays)
        self.split = list(split)
        n = len(self.arrays)
        self.out_shapes = []
        for a, sp in zip(self.arrays, self.split):
            rows = a.shape[0] // NDEV if sp else a.shape[0]
            self.out_shapes.append(jax.ShapeDtypeStruct((NDEV, rows, a.shape[1]), a.dtype))
        self.scratch = [pltpu.SemaphoreType.DMA((n, NDEV - 1)), pltpu.SemaphoreType.DMA((n, NDEV - 1)),
                        pltpu.SemaphoreType.DMA((n,))]

    def _block(self, src, t, d):
        if not self.split[t]:
            return src[t]
        rows = self.arrays[t].shape[0] // NDEV
        return src[t].at[pl.ds(d * rows, rows)]

    def start(self, src, dst, sems):
        send_sems, recv_sems, local_sems = sems
        me, peers = _me_and_peers()
        for t in range(len(self.arrays)):
            pltpu.make_async_copy(self._block(src, t, me), dst[t].at[me], local_sems.at[t]).start()
        for k, (dev, idx) in enumerate(peers):
            for t in range(len(self.arrays)):
                pltpu.make_async_remote_copy(src_ref=self._block(src, t, idx), dst_ref=dst[t].at[me],
                                             send_sem=send_sems.at[t, k], recv_sem=recv_sems.at[t, k],
                                             device_id=dev, device_id_type=pl.DeviceIdType.MESH).start()

    def wait(self, src, dst, sems):
        send_sems, recv_sems, local_sems = sems
        me, peers = _me_and_peers()

        def slot_copy(t, k, dev, idx):
            return pltpu.make_async_remote_copy(src_ref=dst[t].at[idx], dst_ref=dst[t].at[idx], send_sem=send_sems.at[t, k],
                                                recv_sem=recv_sems.at[t, k], device_id=dev,
                                                device_id_type=pl.DeviceIdType.MESH)

        for k, (dev, idx) in enumerate(peers):
            for t in range(len(self.arrays)):
                slot_copy(t, k, dev, idx).wait_recv()
        for k, (dev, idx) in enumerate(peers):
            for t in range(len(self.arrays)):
                slot_copy(t, k, dev, idx).wait_send()
        for t in range(len(self.arrays)):
            pltpu.make_async_copy(dst[t].at[me], dst[t].at[me], local_sems.at[t]).wait()

    def gather_by_chip(self, src, dst, sems):
        assert not any(self.split)
        send_sems, recv_sems, local_sems = sems
        n = len(self.arrays)
        x, y, c = lax.axis_index("x"), lax.axis_index("y"), lax.axis_index("c")
        me, sibling = 4 * x + 2 * y + c, (x, y, 1 - c)
        chips = [(1 - x, y), (x, 1 - y), (1 - x, 1 - y)]

        def index(chip, core):
            return 4 * chip[0] + 2 * chip[1] + core

        def copy(t, k, block, to, from_src):
            return pltpu.make_async_remote_copy(src_ref=src[t] if from_src else dst[t].at[block], dst_ref=dst[t].at[block],
                                                send_sem=send_sems.at[t, k], recv_sem=recv_sems.at[t, k],
                                                device_id=to, device_id_type=pl.DeviceIdType.MESH)

        local = [pltpu.make_async_copy(src[t], dst[t].at[me], local_sems.at[t]) for t in range(n)]
        for cp in local:
            cp.start()
        sends = []
        for t in range(n):
            sends.append(copy(t, 0, me, sibling, True))
            sends += [copy(t, 1 + j, me, (*chip, c), True) for j, chip in enumerate(chips)]
        for cp in sends:
            cp.start()
        for j, chip in enumerate(chips):
            for t in range(n):
                copy(t, 1 + j, index(chip, c), sibling, False).wait_recv()
                passed = copy(t, 4 + j, index(chip, c), sibling, False)
                passed.start()
                sends.append(passed)
        for t in range(n):
            copy(t, 0, index((x, y), 1 - c), sibling, False).wait_recv()
            for j, chip in enumerate(chips):
                copy(t, 4 + j, index(chip, 1 - c), sibling, False).wait_recv()
        for cp in sends:
            cp.wait_send()
        for cp in local:
            cp.wait()

    def run(self, name, by_chip=False):
        n = len(self.arrays)

        def body(*refs):
            src, dst, sems = refs[:n], refs[n:2 * n], refs[2 * n:]
            if by_chip:
                self.gather_by_chip(src, dst, sems)
                return
            self.start(src, dst, sems)
            self.wait(src, dst, sems)

        return pl.pallas_call(
            body, name=name, out_shape=self.out_shapes, in_specs=[_ANY] * n, out_specs=[_ANY] * n,
            scratch_shapes=self.scratch,
        )(*self.arrays)


SMALL = [("ln_g", (DEPTH, D), False), ("ln_b", (DEPTH, D), False), ("hgrn_lb_logits", (DEPTH, TOK), False),
         ("a_w_s", (1, NH, HD, HD), False), ("a_b_s", (1, NH, HD), False), ("b_norm_g", (1, TOK), True),
         ("c_w_pool", (1, 4, POOL_GROUP, POOL_GROUP), False), ("c_scale", (1, TOK), True),
         ("d_conv_w", (1, 4, TOK), True), ("d_conv_b", (1, TOK), True),
         ("d_w_gx", (1, NH, HD, HD), False), ("d_b_gx", (1, NH, HD), False),
         ("d_w_ga", (1, NH, HD, HD), False), ("d_b_ga", (1, NH, HD), False), ("d_a_param", (1, TOK), True)]


def _pack(parts, total_rows):
    flat = jnp.concatenate([p.reshape(-1).astype(F32) for p in parts])
    flat = jnp.pad(flat, (0, total_rows * 128 - flat.shape[0]))
    return flat.reshape(total_rows, 128)


def _size(shape):
    n = 1
    for s in shape:
        n *= s
    return n


def _rows_for(n):
    return -(-n // 1024) * 8


def kernel(x, mem, mem_kv_w, ln_g, ln_b, w_out, hgrn_lb_logits, a_w_in, a_w_s, a_b_s, b_w_in, b_norm_g, c_w_in, c_w_pool, c_scale, d_w_in, d_conv_w, d_conv_b, d_w_gx, d_b_gx, d_w_ga, d_b_ga, d_a_param, loss_target, m_mem_kv_w, m_ln_g, m_ln_b, m_w_out, m_hgrn_lb_logits, m_a_w_in, m_a_w_s, m_a_b_s, m_b_w_in, m_b_norm_g, m_c_w_in, m_c_w_pool, m_c_scale, m_d_w_in, m_d_conv_w, m_d_conv_b, m_d_w_gx, m_d_b_gx, m_d_w_ga, m_d_b_ga, m_d_a_param, v_mem_kv_w, v_ln_g, v_ln_b, v_w_out, v_hgrn_lb_logits, v_a_w_in, v_a_w_s, v_a_b_s, v_b_w_in, v_b_norm_g, v_c_w_in, v_c_w_pool, v_c_scale, v_d_w_in, v_d_conv_w, v_d_conv_b, v_d_w_gx, v_d_b_gx, v_d_w_ga, v_d_b_ga, v_d_a_param):
    W = dict(mem_kv_w=mem_kv_w, ln_g=ln_g, ln_b=ln_b, w_out=w_out, hgrn_lb_logits=hgrn_lb_logits, a_w_in=a_w_in, a_w_s=a_w_s,
             a_b_s=a_b_s, b_w_in=b_w_in, b_norm_g=b_norm_g, c_w_in=c_w_in, c_w_pool=c_w_pool, c_scale=c_scale, d_w_in=d_w_in,
             d_conv_w=d_conv_w, d_conv_b=d_conv_b, d_w_gx=d_w_gx, d_b_gx=d_b_gx, d_w_ga=d_w_ga, d_b_ga=d_b_ga, d_a_param=d_a_param)
    M = dict(mem_kv_w=m_mem_kv_w, ln_g=m_ln_g, ln_b=m_ln_b, w_out=m_w_out, hgrn_lb_logits=m_hgrn_lb_logits, a_w_in=m_a_w_in,
             a_w_s=m_a_w_s, a_b_s=m_a_b_s, b_w_in=m_b_w_in, b_norm_g=m_b_norm_g, c_w_in=m_c_w_in, c_w_pool=m_c_w_pool,
             c_scale=m_c_scale, d_w_in=m_d_w_in, d_conv_w=m_d_conv_w, d_conv_b=m_d_conv_b, d_w_gx=m_d_w_gx, d_b_gx=m_d_b_gx,
             d_w_ga=m_d_w_ga, d_b_ga=m_d_b_ga, d_a_param=m_d_a_param)
    V = dict(mem_kv_w=v_mem_kv_w, ln_g=v_ln_g, ln_b=v_ln_b, w_out=v_w_out, hgrn_lb_logits=v_hgrn_lb_logits, a_w_in=v_a_w_in,
             a_w_s=v_a_w_s, a_b_s=v_a_b_s, b_w_in=v_b_w_in, b_norm_g=v_b_norm_g, c_w_in=v_c_w_in, c_w_pool=v_c_w_pool,
             c_scale=v_c_scale, d_w_in=v_d_w_in, d_conv_w=v_d_conv_w, d_conv_b=v_d_conv_b, d_w_gx=v_d_w_gx, d_b_gx=v_d_b_gx,
             d_w_ga=v_d_w_ga, d_b_ga=v_d_b_ga, d_a_param=v_d_a_param)
    me = 4 * lax.axis_index("x") + 2 * lax.axis_index("y") + lax.axis_index("c")
    x2, mem2, tgt2 = x[0], mem[0], loss_target[0]
    in_names = ["a_w_in", "b_w_in", "c_w_in", "d_w_in"]

    shard_names = [n for n, _, sh in SMALL if sh]
    small_shard = _pack([W[n] for n in shard_names], 8)
    wts = _prep_weights([W[n][0] for n in in_names], w_out, mem_kv_w)
    wt_sh, wo_sh, wkv_sh = wts[:4], wts[4:8], wts[8]
    g0 = _Exchange([wt_sh[0], wo_sh[0], wkv_sh, small_shard], [False] * 4).run("gather_first", by_chip=True)
    wt_full = [g0[0].reshape(-1, D)]
    wout_full = [g0[1].reshape(D, D)]
    wkv_full = g0[2].reshape(D, 2 * XW)
    sm = g0[3].reshape(NDEV, 1024)
    full_small = {}
    off = 0
    for n, shape, _ in [s for s in SMALL if s[2]]:
        per = _size(shape) // NDEV
        blk = sm[:, off:off + per]
        if n == "d_conv_w":
            full_small[n] = blk.reshape(NDEV, 4, TOK // NDEV).transpose(1, 0, 2).reshape(4, TOK)
        else:
            full_small[n] = blk.reshape(1, TOK)
        off += per

    ks, vs = _kv_fwd(mem2, wkv_full)
    tri_bs = jnp.broadcast_to(a_b_s[0][:, :, None], (NH, HD, HD))
    wbd = jnp.zeros((TOK, TOK), F32)
    for g in range(4):
        wbd = lax.dynamic_update_slice(wbd, c_w_pool[0, g], (g * POOL_GROUP, g * POOL_GROUP))
    wbd = wbd.astype(MXU)
    prm = {0: [a_w_s[0], tri_bs],
           1: [hgrn_lb_logits, full_small["b_norm_g"]],
           2: [wbd, full_small["c_scale"]],
           3: [full_small["d_conv_w"], full_small["d_conv_b"], d_w_gx[0].astype(MXU), d_b_gx[0].reshape(1, TOK),
               d_w_ga[0].astype(MXU), d_b_ga[0].reshape(1, TOK), full_small["d_a_param"]]}
    ones = jnp.ones((1, D), F32)
    zeros = jnp.zeros((1, D), F32)
    xs, gs, bs = [x2], [ones], [zeros]
    saved = []
    for i in range(DEPTH):
        ride = _Exchange([wt_sh[i + 1], wo_sh[i + 1]], [False, False]) if i + 1 < DEPTH else None
        res = _layer_fwd(i, i, xs[i], gs[i], bs[i], wt_full[i], wout_full[i], ks, vs, prm[i], ride)
        if ride:
            wt_full.append(res[-2].reshape(-1, D))
            wout_full.append(res[-1].reshape(D, D))
            res = res[:-2]
        saved.append(res)
        xs.append(res[1])
        gs.append(ln_g[i:i + 1])
        bs.append(ln_b[i:i + 1])

    up = tgt2
    grads = {}
    dks_l, dvs_l, dwt_l, dwout_l, dlng_l, dlnb_l = [], [], [], [], [], []
    recv_wt, recv_wo = [None] * DEPTH, [None] * DEPTH
    loss_part = None
    sharded = {n for n, _, sh in SMALL if sh}
    group = {3: ["ln_g#3", "ln_b#3", "d_conv_w", "d_conv_b", "d_w_gx", "d_b_gx", "d_w_ga", "d_b_ga", "d_a_param"],
             2: ["ln_g#2", "ln_b#2", "c_w_pool", "c_scale"],
             1: ["ln_g#1", "ln_b#1", "hgrn_lb_logits", "b_norm_g"],
             0: ["ln_g#0", "ln_b#0", "a_w_s", "a_b_s", "loss"]}

    def small_of(l):
        return [grads[e].reshape(-1, grads[e].shape[-1]) for e in group[l]]

    recv_small = [None] * DEPTH
    for i in reversed(range(DEPTH)):
        res = saved[i]
        extra = None if len(res) <= 4 else (res[4] if len(res) == 5 else res[4:])
        ride = None
        if i + 1 < DEPTH:
            smalls = small_of(i + 1)
            ride = _Exchange([dwt_l[-1], dwout_l[-1]] + smalls, [True, True] + [False] * len(smalls))
        out = _layer_bwd(i, i, up, i == DEPTH - 1, res[1], res[2], res[3], gs[i + 1], bs[i + 1], res[0], wout_full[i], ks, vs,
                         prm[i], extra, ride)
        if ride:
            nr = len(ride.arrays)
            recv_wt[i + 1], recv_wo[i + 1], recv_small[i + 1] = out[-nr], out[-nr + 1], out[-nr + 2:]
            out = out[:-nr]
        dres, dproj, dwout_i, dks_i, dvs_i, dg_i, db_i, loss_i = out[:8]
        pg = out[8:]
        if i == DEPTH - 1:
            grads["loss"] = loss_i
        dks_l.append(dks_i)
        dvs_l.append(dvs_i)
        dwout_l.append(dwout_i)
        grads[f"ln_g#{i}"], grads[f"ln_b#{i}"] = dg_i, db_i
        if i == 0:
            grads["a_w_s"], dbs_exp = pg
            grads["a_b_s"] = _bias_finalize(dbs_exp)
        elif i == 1:
            dlb, grads["b_norm_g"] = pg
            grads["hgrn_lb_logits"] = _lb_finalize(dlb, hgrn_lb_logits)
        elif i == 2:
            dwbd, grads["c_scale"] = pg
            grads["c_w_pool"] = jnp.stack([lax.dynamic_slice(dwbd, (g * POOL_GROUP, g * POOL_GROUP), (POOL_GROUP, POOL_GROUP))
                                           for g in range(4)])
        else:
            (grads["d_conv_w"], grads["d_conv_b"], grads["d_w_gx"], grads["d_b_gx"], grads["d_w_ga"], grads["d_b_ga"],
             grads["d_a_param"]) = pg
        ride = None
        if i == 0:
            dwkv = _kv_bwd(mem2, dks_l, dvs_l)
            smalls = small_of(0)
            ride = _Exchange([dwout_i, dwkv] + smalls, [True, True] + [False] * len(smalls))
        pb = _proj_bwd(i, dproj, dres, xs[i], gs[i], bs[i], wt_full[i], ride)
        up, dwt = pb[:2]
        if ride:
            recv_wo[0], recv_kv, recv_small[0] = pb[2], pb[3], pb[4:]
        dwt_l.append(dwt)
    grad_x = up[None]

    recv_wt[0], = _Exchange([dwt_l[-1]], [True]).run("scatter_last")

    outs = {}
    for t, n in enumerate(in_names):
        g, d, mn, vn = _sum_adam(f"adam_{n}", recv_wt[t], W[n][0], M[n][0], V[n][0], True)
        outs[n] = (g[None], d[None], mn[None], vn[None])
    wo_res = [_sum_adam(f"adam_w_out{l}", recv_wo[l], w_out[l], m_w_out[l], v_w_out[l], False) for l in range(DEPTH)]
    outs["w_out"] = tuple(jnp.stack([wo_res[l][j] for l in range(DEPTH)]) for j in range(4))
    outs["mem_kv_w"] = _sum_adam("adam_mem_kv_w", recv_kv, mem_kv_w, m_mem_kv_w, v_mem_kv_w, False)

    def entry_of(tree, e, like):
        if "#" in e:
            n, l = e.split("#")
            return tree[n][int(l):int(l) + 1]
        return tree[e].reshape(like.shape[1:])

    small = [{}, {}, {}, {}]
    for l in range(DEPTH):
        params = [None if (e == "loss" or e in sharded) else tuple(entry_of(t, e, r) for t in (W, M, V))
                  for e, r in zip(group[l], recv_small[l])]
        for e, res in zip(group[l], _group_adam(f"small_adam{l}", recv_small[l], params)):
            for j, a in enumerate(res):
                small[j][e] = a
    loss = small[0]["loss"][0, 0]
    for j in range(4):
        for n in ("ln_g", "ln_b"):
            small[j][n] = jnp.concatenate([small[j][f"{n}#{l}"] for l in range(DEPTH)], axis=0)
    g_small = small[0]
    for n, _, sh in SMALL:
        if not sh:
            outs[n] = tuple(small[j][n] for j in range(4))
    per = TOK // NDEV
    g_sh = {n: lax.dynamic_slice_in_dim(g_small[n], me * per, per, axis=1) for n, s, sh in SMALL if sh}
    gp = _pack([g_sh[n] for n in shard_names], 8)
    d_p, m_p, v_p = _adam_only(gp, small_shard, _pack([M[n] for n in shard_names], 8), _pack([V[n] for n in shard_names], 8))
    o = 0
    for n in shard_names:
        cnt = _size(W[n].shape)
        outs[n] = (g_sh[n],) + tuple(t.reshape(-1)[o:o + cnt].reshape(W[n].shape) for t in (d_p, m_p, v_p))
        o += cnt

    order = ["mem_kv_w", "ln_g", "ln_b", "w_out", "hgrn_lb_logits", "a_w_in", "a_w_s", "a_b_s", "b_w_in", "b_norm_g", "c_w_in",
             "c_w_pool", "c_scale", "d_w_in", "d_conv_w", "d_conv_b", "d_w_gx", "d_b_gx", "d_w_ga", "d_b_ga", "d_a_param"]
    result = [loss, grad_x]
    for j in range(4):
        result += [outs[n][j].reshape(W[n].shape) for n in order]
    return tuple(result)
```

```python
import functools

import jax
import jax.numpy as jnp
from jax import lax
from jax.experimental import pallas as pl
from jax.experimental.pallas import tpu as pltpu

F32 = jnp.float32
MXU = jnp.bfloat16
WIRE = jnp.bfloat16

D = 1024
TOK = 768
XW = 256
NMEM = 256
XHEADS = 4
XSCALE = 64 ** -0.5
NH = 6
HD = 128
CH = 16
POOL_WINDOWS = (2, 4, 8, 16)
POOL_GROUP = 192
DEPTH = 4
ALPHA = (2 * DEPTH) ** 0.25
LN_EPS = 1e-5
RMS_EPS = 1e-6
LRU_C = 8.0
B1, B2, LR, EPS, WD, STEP = 0.9, 0.999, 0.001, 1e-8, 0.01, 10

NDEV = 8
TS_FWD = {0: 512, 1: 256, 2: 512, 3: 256}
TS_BWD = {0: 512, 1: 256, 2: 512, 3: 256}
TSB = 512
VMEM_LIMIT = 58 * 1024 * 1024
VMEM_LIMIT_WIDE = 62 * 1024 * 1024

KIND_WIDTHS = {0: 2 * TOK + XW + D, 1: 3 * TOK + XW + D, 2: TOK + XW + D, 3: TOK + XW + D}


def _mm(a, b, ca, cb):
    return lax.dot_general(a.astype(MXU), b.astype(MXU), (((ca,), (cb,)), ((), ())), preferred_element_type=F32)


def _nn(a, b):
    return _mm(a, b, 1, 0)


def _nt(a, b):
    return _mm(a, b, 1, 1)


def _tn(a, b):
    return _mm(a, b, 0, 0)


def _bmm(a, b, ca, cb):
    return lax.dot_general(a.astype(MXU), b.astype(MXU), (((ca,), (cb,)), ((0,), (0,))), preferred_element_type=F32)


def _sigmoid(x):
    return 1.0 / (1.0 + jnp.exp(-x))


def _vjp1(fn, x, dy):
    return jax.vjp(fn, x)[1](dy)[0]


def _rowsum(x):
    return jnp.sum(x, axis=0, keepdims=True)


def _row(x, r):
    sel = lax.broadcasted_iota(jnp.int32, x.shape, 0) == r
    return jnp.sum(jnp.where(sel, x, 0.0), axis=0, keepdims=True)


def _acc(ref, val):
    ref[...] += val


def _cparams(sem=None, vmem=VMEM_LIMIT):
    return pltpu.CompilerParams(dimension_semantics=sem, vmem_limit_bytes=vmem)


def _res(a):
    nd = a.ndim
    return pl.BlockSpec(a.shape, lambda i: (0,) * nd)


def _res_sds(shape):
    nd = len(shape)
    return pl.BlockSpec(shape, lambda i: (0,) * nd)


def _row_spec(width, nt, rev, ts):
    if rev:
        return pl.BlockSpec((ts, width), lambda i: (nt - 1 - i, 0))
    return pl.BlockSpec((ts, width), lambda i: (i, 0))


def _call(body, name, grid, ins, outs, scratch=(), sem=("arbitrary",), vmem=VMEM_LIMIT):
    arrays = [a for a, _ in ins]
    return pl.pallas_call(
        body, name=name, grid=grid,
        in_specs=[s for _, s in ins],
        out_specs=[s for _, s in outs],
        out_shape=[o for o, _ in outs],
        scratch_shapes=list(scratch),
        compiler_params=_cparams(sem, vmem),
    )(*arrays)


def _xattn_fwd(qx, ks_ref, vs_ref):
    o = None
    ps = []
    for h in range(XHEADS):
        s = _nt(qx, ks_ref[h]) * XSCALE
        s = s - jnp.max(s, axis=-1, keepdims=True)
        e = jnp.exp(s)
        p = e * (1.0 / jnp.sum(e, axis=-1, keepdims=True))
        ps.append(p)
        oh = _nn(p, vs_ref[h])
        o = oh if o is None else o + oh
    return o, ps


def _xattn_bwd(qx, ps, dxo, ks_ref, vs_ref, dks_ref, dvs_ref):
    dq = None
    for h in range(XHEADS):
        p = ps[h]
        dp = _nt(dxo, vs_ref[h])
        ds = p * (dp - jnp.sum(dp * p, axis=-1, keepdims=True))
        dqh = _nn(ds, ks_ref[h]) * XSCALE
        dq = dqh if dq is None else dq + dqh
        dks_ref[h] += _tn(ds, qx) * XSCALE
        dvs_ref[h] += _tn(p, dxo)
    return dq


def _tril128():
    r = lax.broadcasted_iota(jnp.int32, (HD, HD), 0)
    c = lax.broadcasted_iota(jnp.int32, (HD, HD), 1)
    return c <= r


GELU_C = 0.7978845608028654
GELU_K = 0.044715


def _gelu(x):
    th = jnp.tanh(GELU_C * (x + GELU_K * (x * x * x)))
    return 0.5 * x * (1.0 + th), th


def _gelu_grad(x, th):
    return 0.5 * (1.0 + th) + 0.5 * x * (1.0 - th * th) * (GELU_C * (1.0 + 3.0 * GELU_K * (x * x)))


def _gmlp_fwd(u, v, ws_ref, bs_ref):
    ts = u.shape[0]
    ug, thu = _gelu(u)
    vg, thv = _gelu(v)
    tri = _tril128()
    toks, res = [], []
    for g in range(NH):
        sl = slice(g * HD, (g + 1) * HD)
        vgh = vg[:, sl]
        cen = vgh - jnp.mean(vgh, axis=-1, keepdims=True)
        rstd = lax.rsqrt(jnp.mean(cen * cen, axis=-1, keepdims=True) + LN_EPS)
        vn = cen * rstd
        w = jnp.where(tri, ws_ref[g], 0.0).astype(MXU)
        mix = jnp.concatenate([_nn(w, vn[n * HD:(n + 1) * HD]) + bs_ref[g] for n in range(ts // HD)], axis=0)
        toks.append(ug[:, sl] * mix)
        res.append((vn, rstd, mix, w))
    return jnp.concatenate(toks, axis=1), (ug, res, thu, thv)


def _gmlp_bwd(u, v, fres, dtok, dws_ref, dbs_ref):
    ts = u.shape[0]
    ug, res, thu, thv = fres
    tri = _tril128()
    dugs, dvgs = [], []
    for g in range(NH):
        sl = slice(g * HD, (g + 1) * HD)
        vn, rstd, mix, w = res[g]
        dmix = dtok[:, sl] * ug[:, sl]
        dugs.append(dtok[:, sl] * mix)
        dvn_rows = []
        dw = None
        dbs = None
        for n in range(ts // HD):
            dm = dmix[n * HD:(n + 1) * HD]
            dvn_rows.append(_tn(w, dm))
            t = _nt(dm, vn[n * HD:(n + 1) * HD])
            dw = t if dw is None else dw + t
            dbs = dm if dbs is None else dbs + dm
        dws_ref[g] += jnp.where(tri, dw, 0.0)
        dbs_ref[g] += dbs
        dvn = jnp.concatenate(dvn_rows, axis=0)
        dvgs.append(rstd * (dvn - jnp.mean(dvn, axis=-1, keepdims=True) - vn * jnp.mean(dvn * vn, axis=-1, keepdims=True)))
    du = jnp.concatenate(dugs, axis=1) * _gelu_grad(u, thu)
    dv = jnp.concatenate(dvgs, axis=1) * _gelu_grad(v, thv)
    return du, dv


def _chunk_cumsum(x):
    row = lax.broadcasted_iota(jnp.int32, x.shape, 0) % CH
    for s in (1, 2, 4, 8):
        x = x + jnp.where(row >= s, pltpu.roll(x, s, 0), 0.0)
    return x


def _chunk_revcumsum(x):
    n = x.shape[0]
    row = lax.broadcasted_iota(jnp.int32, x.shape, 0) % CH
    for s in (1, 2, 4, 8):
        x = x + jnp.where(row < CH - s, pltpu.roll(x, n - s, 0), 0.0)
    return x


def _chunk_sum(x):
    n, w = x.shape
    return jnp.sum(x.reshape(n // CH, CH, w), axis=1)


def _chunk_bcast(c, n):
    nch, w = c.shape
    return jnp.broadcast_to(c[:, None, :], (nch, CH, w)).reshape(n, w)


def _lower_bound(lb_logits, layer):
    lg = lb_logits
    e = jnp.exp(lg - jnp.max(lg, axis=0, keepdims=True))
    p = e / jnp.sum(e, axis=0, keepdims=True)
    row = lax.broadcasted_iota(jnp.int32, p.shape, 0)
    lb = jnp.sum(jnp.where((row >= 1) & (row <= layer), p, 0.0), axis=0, keepdims=True)
    return lb, p


def _hgrn_prep(q, fl, lb):
    n = q.shape[0]
    sg = _sigmoid(fl)
    f = lb + (1.0 - lb) * sg
    lf = jnp.log(f)
    sq = _sigmoid(q)
    g = _chunk_cumsum(lf)
    tot = _chunk_sum(lf)
    gl = _chunk_bcast(tot, n)
    eg = jnp.exp(g)
    eng = jnp.exp(-g)
    egl = jnp.exp(gl - g)
    k = 1.0 - f
    qf = q * sq
    return dict(sg=sg, f=f, k=k, sq=sq, qf=qf, eg=eg, eng=eng, egl=egl,
                qd=qf * eg, ki=k * eng, ke=k * egl, dch=jnp.exp(tot))


def _hgrn_mask():
    r = lax.broadcasted_iota(jnp.int32, (HD, HD), 0)
    c = lax.broadcasted_iota(jnp.int32, (HD, HD), 1)
    return (r // CH == c // CH) & (c <= r)


def _hgrn_states(v3, ke3, dch_h, st_in):
    nch = v3.shape[0]
    ut = _bmm(v3, ke3, 1, 1)
    dfull = jnp.broadcast_to(dch_h[:, None, :], (nch, HD, HD))
    st, sts = st_in, []
    for c in range(nch):
        sts.append(st)
        st = st * dfull[c] + ut[c]
    return jnp.stack(sts), st, dfull


def _hgrn_fwd(q, fl, inp, lb, ng, st_ref, sts_ref):
    n = q.shape[0]
    nch = n // CH
    pr = _hgrn_prep(q, fl, lb)
    mask = _hgrn_mask()
    toks, o_l, a_l = [], [], []
    qd_m, ki_m, ke_m, v_m = (t.astype(MXU) for t in (pr["qd"], pr["ki"], pr["ke"], inp))
    for h in range(NH):
        sl = slice(h * HD, (h + 1) * HD)
        qd, ki, ke, v = qd_m[:, sl], ki_m[:, sl], ke_m[:, sl], v_m[:, sl]
        qd3 = qd.reshape(nch, CH, HD)
        v3 = v.reshape(nch, CH, HD)
        ke3 = ke.reshape(nch, CH, HD)
        sts, st_ref[h], _ = _hgrn_states(v3, ke3, pr["dch"][:, sl], st_ref[h])
        sts_ref[h] = sts
        o = _bmm(qd3, sts, 2, 2).reshape(n, HD)
        intra, scores = [], []
        for b in range(n // HD):
            bs = slice(b * HD, (b + 1) * HD)
            a = jnp.where(mask, _nt(qd[bs], ki[bs]), 0.0).astype(MXU)
            scores.append(a)
            intra.append(_nn(a, v[bs]))
        o = o + jnp.concatenate(intra, axis=0)
        r = lax.rsqrt(jnp.mean(o * o, axis=-1, keepdims=True) + RMS_EPS)
        toks.append(o * r * ng[:, sl])
        o_l.append(o)
        a_l.append(jnp.concatenate(scores, axis=0))
    return jnp.concatenate(toks, axis=1), jnp.concatenate(o_l, axis=1), jnp.concatenate(a_l, axis=1)


def _hgrn_bwd(q, pr, inp, lb, ng, dtok, o_all, a_all, ststart_ref, dst_ref, dng_ref, dlb_ref):
    n = q.shape[0]
    nch = n // CH
    mask = _hgrn_mask()
    dqd_l, dki_l, dke_l, dv_l, ddch_l, dng_l, toks = [], [], [], [], [], [], []
    qd_m, ki_m, ke_m, v_m = (t.astype(MXU) for t in (pr["qd"], pr["ki"], pr["ke"], inp))
    for h in range(NH):
        sl = slice(h * HD, (h + 1) * HD)
        qd, ki, ke, v = qd_m[:, sl], ki_m[:, sl], ke_m[:, sl], v_m[:, sl]
        qd3 = qd.reshape(nch, CH, HD)
        v3 = v.reshape(nch, CH, HD)
        ke3 = ke.reshape(nch, CH, HD)
        sts = ststart_ref[h]
        dfull = jnp.broadcast_to(pr["dch"][:, sl][:, None, :], (nch, HD, HD))
        sts_m = sts.astype(MXU)
        o = o_all[:, sl]
        a_l = [a_all[b * HD:(b + 1) * HD, sl] for b in range(n // HD)]
        r = lax.rsqrt(jnp.mean(o * o, axis=-1, keepdims=True) + RMS_EPS)
        toks.append(o * r * ng[:, sl])
        dt = dtok[:, sl]
        dng_l.append(_rowsum(dt * o * r))
        dn = dt * ng[:, sl]
        do = r * dn - o * (r * r * r) * jnp.mean(dn * o, axis=-1, keepdims=True)
        do_m = do.astype(MXU)
        do3 = do_m.reshape(nch, CH, HD)
        dqd_rows, dki_rows, dv_rows = [], [], []
        for b in range(n // HD):
            bs = slice(b * HD, (b + 1) * HD)
            da = jnp.where(mask, _nt(do_m[bs], v[bs]), 0.0).astype(MXU)
            dqd_rows.append(_nn(da, ki[bs]))
            dki_rows.append(_tn(da, qd[bs]))
            dv_rows.append(_tn(a_l[b], do_m[bs]))
        dqd = jnp.concatenate(dqd_rows, axis=0) + _bmm(do3, sts_m, 2, 1).reshape(n, HD)
        dki = jnp.concatenate(dki_rows, axis=0)
        dv = jnp.concatenate(dv_rows, axis=0)
        wt = _bmm(do3, qd3, 1, 1)
        dst, dstn_l = dst_ref[h], [None] * nch
        for c in reversed(range(nch)):
            dstn_l[c] = dst
            dst = wt[c] + dst * dfull[c]
        dst_ref[h] = dst
        dstn = jnp.stack(dstn_l)
        dstn_m = dstn.astype(MXU)
        dv = dv + _bmm(ke3, dstn_m, 2, 2).reshape(n, HD)
        dke = _bmm(v3, dstn_m, 2, 1).reshape(n, HD)
        ddch_l.append(jnp.sum(sts * dstn, axis=1))
        dqd_l.append(dqd)
        dki_l.append(dki)
        dke_l.append(dke)
        dv_l.append(dv)
    dqd = jnp.concatenate(dqd_l, axis=1)
    dki = jnp.concatenate(dki_l, axis=1)
    dke = jnp.concatenate(dke_l, axis=1)
    dinp = jnp.concatenate(dv_l, axis=1)
    ddch = jnp.concatenate(ddch_l, axis=1)
    _acc(dng_ref, jnp.concatenate(dng_l, axis=1))
    dqf = dqd * pr["eg"]
    dke_ke = dke * pr["ke"]
    dg = dqd * pr["qd"] - dki * pr["ki"] - dke_ke
    dk = dki * pr["eng"] + dke * pr["egl"]
    dgl = _chunk_sum(dke_ke) + ddch * pr["dch"]
    dlf = _chunk_revcumsum(dg) + _chunk_bcast(dgl, n)
    df = dlf / pr["f"] - dk
    sg = pr["sg"]
    dfl = df * (1.0 - lb) * sg * (1.0 - sg)
    _acc(dlb_ref, _rowsum(df * (1.0 - sg)))
    sq = pr["sq"]
    dq = dqf * (sq * (1.0 + q * (1.0 - sq)))
    return jnp.concatenate(toks, axis=1), dq, dfl, dinp


def _pool_select(s2, s4, s8, s16):
    col = lax.broadcasted_iota(jnp.int32, (1, TOK), 1)
    return jnp.where(col < POOL_GROUP, s2, jnp.where(col < 2 * POOL_GROUP, s4, jnp.where(col < 3 * POOL_GROUP, s8, s16)))


def _pool_cnt(pos0, n):
    pos = pos0 + lax.broadcasted_iota(jnp.int32, (n, TOK), 0) + 1
    col = lax.broadcasted_iota(jnp.int32, (n, TOK), 1)
    w = jnp.where(col < POOL_GROUP, 2, jnp.where(col < 2 * POOL_GROUP, 4, jnp.where(col < 3 * POOL_GROUP, 8, 16)))
    return jnp.minimum(pos, w).astype(F32)


def _pool_fwd(p, halo, pos0, wbd, scale):
    n = p.shape[0]
    ext = jnp.concatenate([halo, p], axis=0)
    s2 = ext + pltpu.roll(ext, 1, 0)
    s4 = s2 + pltpu.roll(s2, 2, 0)
    s8 = s4 + pltpu.roll(s4, 4, 0)
    s16 = s8 + pltpu.roll(s8, 8, 0)
    win = _pool_select(s2, s4, s8, s16)[16:]
    cnt = _pool_cnt(pos0, n)
    diff = win / cnt - p
    y = _nn(diff, wbd)
    return y * scale, (diff, y, cnt)


def _pool_bwd(fres, dtok, nxt_ref, wbd, scale, dwbd_ref, dscale_ref):
    diff, y, cnt = fres
    n = diff.shape[0]
    _acc(dscale_ref, _rowsum(dtok * y))
    dy = dtok * scale
    ddiff = _nt(dy, wbd)
    dwbd_ref[...] += _tn(diff, dy)
    qv = ddiff / cnt
    ext = jnp.concatenate([qv, nxt_ref[...]], axis=0)
    m = n + 16
    s2 = ext + pltpu.roll(ext, m - 1, 0)
    s4 = s2 + pltpu.roll(s2, m - 2, 0)
    s8 = s4 + pltpu.roll(s4, m - 4, 0)
    s16 = s8 + pltpu.roll(s8, m - 8, 0)
    adj = _pool_select(s2, s4, s8, s16)[:n]
    nxt_ref[...] = qv[:16]
    return adj - ddiff


def _neg_expm1(x):
    return jnp.where(jnp.abs(x) < 1e-2, -x * (1.0 + x * (0.5 + x * (1.0 / 6.0))), 1.0 - jnp.exp(x))


def _softplus_neg(ap):
    return jnp.maximum(-ap, 0.0) + jnp.log(1.0 + jnp.exp(-jnp.abs(ap)))


def _lru_gates(xc, zx, za, ap, first):
    gx = _sigmoid(zx)
    ga = _sigmoid(za)
    sp = _softplus_neg(ap)
    log_a = -LRU_C * ga * sp
    a = jnp.exp(log_a)
    mult = jnp.sqrt(_neg_expm1(2.0 * log_a))
    mult = jnp.where(first, 1.0, mult)
    return a, mult * gx * xc, (gx, ga, sp, mult)


def _scan_fwd(a, b, h0):
    n = a.shape[0]
    row = lax.broadcasted_iota(jnp.int32, a.shape, 0)
    s = 1
    while s < n:
        keep = row >= s
        b = b + a * jnp.where(keep, pltpu.roll(b, s, 0), 0.0)
        a = a * jnp.where(keep, pltpu.roll(a, s, 0), 1.0)
        s *= 2
    return b + a * h0


def _scan_bwd(an, d, dh_next):
    n = an.shape[0]
    row = lax.broadcasted_iota(jnp.int32, an.shape, 0)
    s = 1
    while s < n:
        keep = row < n - s
        d = d + an * jnp.where(keep, pltpu.roll(d, n - s, 0), 0.0)
        an = an * jnp.where(keep, pltpu.roll(an, n - s, 0), 1.0)
        s *= 2
    return d + an * dh_next


def _lru_conv(xb, halo, cw_ref, cb):
    ext = jnp.concatenate([halo, xb], axis=0)
    sh = [pltpu.roll(ext, 3 - j, 0)[8:] if j < 3 else xb for j in range(4)]
    xc = cb
    for j in range(4):
        xc = xc + cw_ref[pl.ds(j, 1), :] * sh[j]
    return xc, sh


def _lru_fwd(xb, halo, pos0, prm, h0):
    cw, cb, wgx, bgx, wga, bga, ap = prm
    n = xb.shape[0]
    xc, sh = _lru_conv(xb, halo, cw, cb[...])
    zx = jnp.concatenate([_nn(xc[:, h * HD:(h + 1) * HD], wgx[h]) for h in range(NH)], axis=1) + bgx[...]
    za = jnp.concatenate([_nn(xc[:, h * HD:(h + 1) * HD], wga[h]) for h in range(NH)], axis=1) + bga[...]
    first = (pos0 + lax.broadcasted_iota(jnp.int32, (n, 1), 0)) == 0
    a, b, gates = _lru_gates(xc, zx, za, ap[...], first)
    hseq = _scan_fwd(a, b, h0)
    return hseq, (xc, sh, gates, first, a)


def _lru_bwd(fres, hseq, h0, dtok, prm, carry_refs, grad_refs):
    cw, cb, wgx, bgx, wga, bga, ap = prm
    xc, sh, (gx, ga, sp, mult), first, a = fres
    anext_ref, dhnext_ref, dxcnext_ref = carry_refs
    dcw_ref, dcb_ref, dwgx_ref, dbgx_ref, dwga_ref, dbga_ref, dap_ref = grad_refs
    n = xc.shape[0]
    an = jnp.where(lax.broadcasted_iota(jnp.int32, a.shape, 0) == n - 1, anext_ref[...], pltpu.roll(a, n - 1, 0))
    dh = _scan_bwd(an, dtok, dhnext_ref[...])
    hprev = jnp.where(lax.broadcasted_iota(jnp.int32, hseq.shape, 0) == 0, h0, pltpu.roll(hseq, 1, 0))
    da = dh * hprev
    anext_ref[...] = _row(a, 0)
    dhnext_ref[...] = _row(dh, 0)
    t = dh * xc
    dxc = dh * mult * gx
    dzx = t * mult * gx * (1.0 - gx)
    dlog_a = da * a - jnp.where(first, 0.0, t * gx * (a * a) / mult)
    dza = dlog_a * (-LRU_C * sp) * ga * (1.0 - ga)
    dap = _rowsum(dlog_a * ga) * (LRU_C * _sigmoid(-ap[...]))
    _acc(dap_ref, dap)
    _acc(dbgx_ref, _rowsum(dzx))
    _acc(dbga_ref, _rowsum(dza))
    parts = []
    for h in range(NH):
        sl = slice(h * HD, (h + 1) * HD)
        parts.append(_nt(dzx[:, sl], wgx[h]) + _nt(dza[:, sl], wga[h]))
        dwgx_ref[h] += _tn(xc[:, sl], dzx[:, sl])
        dwga_ref[h] += _tn(xc[:, sl], dza[:, sl])
    dxc = dxc + jnp.concatenate(parts, axis=1)
    _acc(dcb_ref, _rowsum(dxc))
    for j in range(4):
        dcw_ref[pl.ds(j, 1), :] += _rowsum(dxc * sh[j])
    ext = jnp.concatenate([dxc, dxcnext_ref[...]], axis=0)
    m = n + 8
    dxb = cw[pl.ds(3, 1), :] * dxc
    for j in range(3):
        dxb = dxb + cw[pl.ds(j, 1), :] * pltpu.roll(ext, m - (3 - j), 0)[:n]
    dxcnext_ref[...] = dxc[:8]
    return dxb


def _layer_fwd(kind, layer, xprev, gprev, bprev, wt, wout, ks, vs, prm, ride=None):
    TS = TS_FWD[kind]
    _rows = functools.partial(_row_spec, ts=TS)
    S = xprev.shape[0]
    nt = S // TS
    N = wt.shape[0]
    nprm = len(prm)
    nch = TS // CH

    outs = [(jax.ShapeDtypeStruct((S, N), F32), _rows(N, nt, False)),
            (jax.ShapeDtypeStruct((S, D), F32), _rows(D, nt, False)),
            (jax.ShapeDtypeStruct((S, 1), F32), _rows(1, nt, False)),
            (jax.ShapeDtypeStruct((S, XHEADS * NMEM), MXU), _rows(XHEADS * NMEM, nt, False))]
    scratch = []
    if kind == 1:
        outs.append((jax.ShapeDtypeStruct((nt, NH, nch, HD, HD), F32),
                     pl.BlockSpec((None, NH, nch, HD, HD), lambda i: (i, 0, 0, 0, 0))))
        outs.append((jax.ShapeDtypeStruct((S, TOK), F32), _rows(TOK, nt, False)))
        outs.append((jax.ShapeDtypeStruct((S, TOK), MXU), _rows(TOK, nt, False)))
        scratch = [pltpu.VMEM((NH, HD, HD), F32)]
    elif kind == 2:
        outs += [(jax.ShapeDtypeStruct((S, TOK), F32), _rows(TOK, nt, False))] * 2
        scratch = [pltpu.VMEM((16, TOK), F32)]
    elif kind == 3:
        outs.append((jax.ShapeDtypeStruct((nt * 8, TOK), F32), pl.BlockSpec((8, TOK), lambda i: (i, 0))))
        outs += [(jax.ShapeDtypeStruct((S, TOK), F32), _rows(TOK, nt, False))] * 4
        scratch = [pltpu.VMEM((8, TOK), F32), pltpu.VMEM((1, TOK), F32)]
    nout = len(outs)
    nscr = len(scratch)
    nride = len(ride.arrays) if ride else 0

    def body(*refs):
        x_ref, g_ref, b_ref, wt_ref, wout_ref, ks_ref, vs_ref = refs[:7]
        prm_refs = refs[7:7 + nprm]
        nin = 7 + nprm + nride
        ride_src = refs[7 + nprm:nin]
        out_refs = refs[nin:nin + nout]
        ride_dst = refs[nin + nout:nin + nout + nride]
        scr = refs[nin + nout + nride:nin + nout + nride + nscr]
        ride_sems = refs[nin + nout + nride + nscr:]
        proj_ref, xhat_ref, rstd_ref = out_refs[:3]
        i = pl.program_id(0)
        if ride:
            @pl.when(i == 0)
            def _():
                ride.start(ride_src, ride_dst, ride_sems)

        xin = x_ref[...] * g_ref[...] + b_ref[...]
        proj = _nt(xin, wt_ref[...])
        proj_ref[...] = proj
        if kind == 0:
            tok, _ = _gmlp_fwd(proj[:, :TOK], proj[:, TOK:2 * TOK], prm_refs[0], prm_refs[1])
        elif kind == 1:
            st_ref, = scr

            @pl.when(i == 0)
            def _():
                st_ref[...] = jnp.zeros_like(st_ref)

            lb, _ = _lower_bound(prm_refs[0][...], layer)
            tok, out_refs[5][...], out_refs[6][...] = _hgrn_fwd(proj[:, :TOK], proj[:, TOK:2 * TOK], proj[:, 2 * TOK:3 * TOK],
                                                                lb, prm_refs[1][...], st_ref, out_refs[4])
        elif kind == 2:
            halo_ref, = scr

            @pl.when(i == 0)
            def _():
                halo_ref[...] = jnp.zeros_like(halo_ref)

            p = proj[:, :TOK]
            tok, (diff, y, _) = _pool_fwd(p, halo_ref[...], i * TS, prm_refs[0][...], prm_refs[1][...])
            out_refs[4][...] = diff
            out_refs[5][...] = y
            halo_ref[...] = p[TS - 16:]
        else:
            halo_ref, h_ref = scr

            @pl.when(i == 0)
            def _():
                halo_ref[...] = jnp.zeros_like(halo_ref)
                h_ref[...] = jnp.zeros_like(h_ref)

            out_refs[4][...] = jnp.broadcast_to(h_ref[...], (8, TOK))
            xb = proj[:, :TOK]
            tok, fres = _lru_fwd(xb, halo_ref[...], i * TS, prm_refs, h_ref[...])
            gx, ga, _, mult = fres[2]
            for r, val in zip(out_refs[5:9], (tok, gx, ga, mult)):
                r[...] = val
            halo_ref[...] = xb[TS - 8:]
            h_ref[...] = _row(tok, TS - 1)
        qx = proj[:, N - D - XW:N - D]
        gate = proj[:, N - D:]
        xo, ps = _xattn_fwd(qx, ks_ref, vs_ref)
        out_refs[3][...] = jnp.concatenate(ps, axis=1).astype(MXU)
        mixed = jnp.concatenate([tok, xo], axis=1) * (gate * _sigmoid(gate))
        z = ALPHA * xin + _nn(mixed, wout_ref[...])
        cen = z - jnp.mean(z, axis=-1, keepdims=True)
        rstd = lax.rsqrt(jnp.mean(cen * cen, axis=-1, keepdims=True) + LN_EPS)
        xhat_ref[...] = cen * rstd
        rstd_ref[...] = rstd
        if ride:
            @pl.when(i == nt - 1)
            def _():
                ride.wait(ride_src, ride_dst, ride_sems)

    ins = [(xprev, _rows(D, nt, False)), (gprev, _res(gprev)), (bprev, _res(bprev)), (wt, _res(wt)), (wout, _res(wout)),
           (ks, _res(ks)), (vs, _res(vs))] + [(p, _res(p)) for p in prm]
    if ride:
        ins += [(a, _ANY) for a in ride.arrays]
        outs += [(s, _ANY) for s in ride.out_shapes]
        scratch = scratch + ride.scratch
    return _call(body, f"layer{layer}_fwd", (nt,), ins, outs, scratch)


def _layer_bwd(kind, layer, up, is_last, xhat, rstd, probs, g_i, b_i, proj, wout, ks, vs, prm, extra, ride=None):
    TS = TS_BWD[kind]
    _rows = functools.partial(_row_spec, ts=TS)
    S = xhat.shape[0]
    nt = S // TS
    N = proj.shape[1]
    nprm = len(prm)
    nch = TS // CH

    ins = [(up, _rows(D, nt, True)), (xhat, _rows(D, nt, True)), (rstd, _rows(1, nt, True)), (g_i, _res(g_i)), (b_i, _res(b_i)),
           (proj, _rows(N, nt, True)), (wout, _res(wout)), (ks, _res(ks)), (vs, _res(vs)),
           (probs, _rows(XHEADS * NMEM, nt, True))] + [(p, _res(p)) for p in prm]
    nfixed = 10
    if kind == 1:
        ins.append((extra[0], pl.BlockSpec((None, NH, nch, HD, HD), lambda i: (nt - 1 - i, 0, 0, 0, 0))))
        ins += [(e, _rows(TOK, nt, True)) for e in extra[1:]]
    elif kind == 2:
        ins += [(e, _rows(TOK, nt, True)) for e in extra]
    elif kind == 3:
        hb = TS // 8
        ins.append((proj, pl.BlockSpec((8, TOK), lambda i: (jnp.maximum((nt - 1 - i) * hb - 1, 0), 0))))
        ins.append((extra[0], pl.BlockSpec((8, TOK), lambda i: (nt - 1 - i, 0))))
        ins += [(e, _rows(TOK, nt, True)) for e in extra[1:]]
    nin = len(ins)

    def acc(shape):
        return (jax.ShapeDtypeStruct(shape, F32), _res_sds(shape))

    outs = [(jax.ShapeDtypeStruct((S, D), F32), _rows(D, nt, True)),
            (jax.ShapeDtypeStruct((S, N), MXU), _rows(N, nt, True)),
            (jax.ShapeDtypeStruct((D, D), WIRE), _res_sds((D, D))),
            acc((XHEADS, NMEM, XW)), acc((XHEADS, NMEM, XW)), acc((1, D)), acc((1, D)), acc((1, HD))]
    scratch = []
    if kind == 0:
        outs += [acc((NH, HD, HD)), acc((NH, HD, HD))]
    elif kind == 1:
        outs += [acc((1, TOK)), acc((1, TOK))]
        scratch = [pltpu.VMEM((NH, HD, HD), F32)]
    elif kind == 2:
        outs += [acc((TOK, TOK)), acc((1, TOK))]
        scratch = [pltpu.VMEM((16, TOK), F32)]
    else:
        outs += [acc((4, TOK)), acc((1, TOK)), acc((NH, HD, HD)), acc((1, TOK)), acc((NH, HD, HD)), acc((1, TOK)), acc((1, TOK))]
        scratch = [pltpu.VMEM((1, TOK), F32), pltpu.VMEM((1, TOK), F32), pltpu.VMEM((8, TOK), F32)]
    scratch = scratch + [pltpu.VMEM((D, D), F32)]
    nout = len(outs)
    nscr = len(scratch)
    nride = len(ride.arrays) if ride else 0

    def body(*refs):
        up_ref, xhat_ref, rstd_ref, g_ref, b_ref, proj_ref, wout_ref, ks_ref, vs_ref, probs_ref = refs[:nfixed]
        prm_refs = refs[nfixed:nfixed + nprm]
        ext_refs = refs[nfixed + nprm:nin]
        ride_src = refs[nin:nin + nride]
        o0 = nin + nride
        out_refs = refs[o0:o0 + nout]
        ride_dst = refs[o0 + nout:o0 + nout + nride]
        scr = refs[o0 + nout + nride:o0 + nout + nride + nscr - 1]
        dwout_acc = refs[o0 + nout + nride + nscr - 1]
        ride_sems = refs[o0 + nout + nride + nscr:]
        dres_ref, dproj_ref, dwout_ref, dks_ref, dvs_ref, dg_ref, db_ref, loss_ref = out_refs[:8]
        pgrad = out_refs[8:]
        i = pl.program_id(0)
        tile = nt - 1 - i

        @pl.when(i == 0)
        def _():
            if ride:
                ride.start(ride_src, ride_dst, ride_sems)
            for r in out_refs[3:]:
                r[...] = jnp.zeros_like(r)
            dwout_acc[...] = jnp.zeros_like(dwout_acc)
            for r in scr:
                if kind != 1 or r is scr[0]:
                    r[...] = jnp.zeros_like(r)

        xhat_v = xhat_ref[...]
        if is_last:
            err = xhat_v * g_ref[...] + b_ref[...] - up_ref[...]
            dxo = err * (1.0 / D)
            loss_ref[...] += jnp.sum(0.5 * jnp.mean(err * err, axis=-1, keepdims=True), axis=0, keepdims=True)
        else:
            dxo = up_ref[...]
        _acc(dg_ref, _rowsum(dxo * xhat_v))
        _acc(db_ref, _rowsum(dxo))
        dxh = dxo * g_ref[...]
        dz = rstd_ref[...] * (dxh - jnp.mean(dxh, axis=-1, keepdims=True)
                              - xhat_v * jnp.mean(dxh * xhat_v, axis=-1, keepdims=True))
        dres_ref[...] = ALPHA * dz

        proj = proj_ref[...]
        qx = proj[:, N - D - XW:N - D]
        gate = proj[:, N - D:]
        sgate = _sigmoid(gate)
        silu = gate * sgate
        dmixed = _nt(dz, wout_ref[...])
        dcat = dmixed * silu
        dtok = dcat[:, :TOK]
        if kind == 0:
            u, v = proj[:, :TOK], proj[:, TOK:2 * TOK]
            tok, fres = _gmlp_fwd(u, v, prm_refs[0], prm_refs[1])
            du, dv = _gmlp_bwd(u, v, fres, dtok, pgrad[0], pgrad[1])
            dproj_ref[:, :TOK] = du.astype(MXU)
            dproj_ref[:, TOK:2 * TOK] = dv.astype(MXU)
        elif kind == 1:
            dst_ref, = scr
            lb, _ = _lower_bound(prm_refs[0][...], layer)
            q, fl, inp = proj[:, :TOK], proj[:, TOK:2 * TOK], proj[:, 2 * TOK:3 * TOK]
            pr = _hgrn_prep(q, fl, lb)
            tok, dq, dfl, dinp = _hgrn_bwd(q, pr, inp, lb, prm_refs[1][...], dtok, ext_refs[1][...], ext_refs[2][...],
                                           ext_refs[0], dst_ref, pgrad[1], pgrad[0])
            dproj_ref[:, :TOK] = dq.astype(MXU)
            dproj_ref[:, TOK:2 * TOK] = dfl.astype(MXU)
            dproj_ref[:, 2 * TOK:3 * TOK] = dinp.astype(MXU)
        elif kind == 2:
            diff, y = ext_refs[0][...], ext_refs[1][...]
            tok = y * prm_refs[1][...]
            fres = (diff, y, _pool_cnt(tile * TS, TS))
            dp = _pool_bwd(fres, dtok, scr[0], prm_refs[0][...], prm_refs[1][...], pgrad[0], pgrad[1])
            dproj_ref[:, :TOK] = dp.astype(MXU)
        else:
            xb = proj[:, :TOK]
            halo = jnp.where(tile == 0, 0.0, ext_refs[0][...])
            h0 = ext_refs[1][0:1]
            tok, gx, ga, mult = (r[...] for r in ext_refs[2:6])
            xc, sh = _lru_conv(xb, halo, prm_refs[0], prm_refs[1][...])
            sp = _softplus_neg(prm_refs[6][...])
            first = (tile * TS + lax.broadcasted_iota(jnp.int32, (TS, 1), 0)) == 0
            fres = (xc, sh, (gx, ga, sp, mult), first, jnp.exp(-LRU_C * ga * sp))
            dxb = _lru_bwd(fres, tok, h0, dtok, prm_refs, scr, pgrad)
            dproj_ref[:, :TOK] = dxb.astype(MXU)
        ps = [probs_ref[:, h * NMEM:(h + 1) * NMEM].astype(F32) for h in range(XHEADS)]
        xo = _nn(ps[0], vs_ref[0])
        for h in range(1, XHEADS):
            xo = xo + _nn(ps[h], vs_ref[h])
        dqx = _xattn_bwd(qx, ps, dcat[:, TOK:], ks_ref, vs_ref, dks_ref, dvs_ref)
        cat = jnp.concatenate([tok, xo], axis=1)
        dwout_acc[...] += _tn(cat * silu, dz)
        dgate = dmixed * cat * (sgate * (1.0 + gate * (1.0 - sgate)))
        dproj_ref[:, N - D - XW:N - D] = dqx.astype(MXU)
        dproj_ref[:, N - D:] = dgate.astype(MXU)

        @pl.when(i == nt - 1)
        def _():
            dwout_ref[...] = dwout_acc[...].astype(WIRE)
            if ride:
                ride.wait(ride_src, ride_dst, ride_sems)

    if ride:
        ins += [(a, _ANY) for a in ride.arrays]
        outs += [(s, _ANY) for s in ride.out_shapes]
        scratch = scratch + ride.scratch
    return _call(body, f"layer{layer}_bwd", (nt,), ins, outs, scratch, vmem=VMEM_LIMIT_WIDE if kind == 0 else VMEM_LIMIT)


def _proj_bwd(layer, dproj, dres, xprev, gprev, bprev, wt, ride=None):
    S = xprev.shape[0]
    nt = S // TSB
    N = wt.shape[0]

    nride = len(ride.arrays) if ride else 0

    def body(*refs):
        dproj_ref, dres_ref, x_ref, g_ref, b_ref, wt_ref = refs[:6]
        ride_src = refs[6:6 + nride]
        dx_ref, dwt_ref = refs[6 + nride:8 + nride]
        ride_dst = refs[8 + nride:8 + 2 * nride]
        acc_ref = refs[8 + 2 * nride]
        ride_sems = refs[9 + 2 * nride:]

        @pl.when(pl.program_id(0) == 0)
        def _():
            if ride:
                ride.start(ride_src, ride_dst, ride_sems)
            acc_ref[...] = jnp.zeros_like(acc_ref)

        dp = dproj_ref[...]
        xin = x_ref[...] * g_ref[...] + b_ref[...]
        dx_ref[...] = dres_ref[...] + _nn(dp, wt_ref[...])
        acc_ref[...] += _tn(dp, xin)

        @pl.when(pl.program_id(0) == nt - 1)
        def _():
            dwt_ref[...] = acc_ref[...].astype(WIRE)
            if ride:
                ride.wait(ride_src, ride_dst, ride_sems)

    ins = [(dproj, _row_spec(N, nt, False, TSB)), (dres, _row_spec(D, nt, False, TSB)), (xprev, _row_spec(D, nt, False, TSB)),
           (gprev, _res(gprev)), (bprev, _res(bprev)), (wt, _res(wt))]
    outs = [(jax.ShapeDtypeStruct((S, D), F32), _row_spec(D, nt, False, TSB)),
            (jax.ShapeDtypeStruct((N, D), WIRE), _res_sds((N, D)))]
    scratch = [pltpu.VMEM((N, D), F32)]
    if ride:
        ins += [(a, _ANY) for a in ride.arrays]
        outs += [(s, _ANY) for s in ride.out_shapes]
        scratch = scratch + ride.scratch
    return _call(body, f"layer{layer}_projbwd", (nt,), ins, outs, scratch)


def _head_mask(h):
    col = lax.broadcasted_iota(jnp.int32, (1, XW), 1)
    return (col // 64) == h


def _kv_fwd(mem, wkv):
    def body(mem_ref, w_ref, ks_ref, vs_ref):
        kv = _nn(mem_ref[...], w_ref[...])
        k, v = kv[:, :XW], kv[:, XW:]
        for h in range(XHEADS):
            ks_ref[h] = jnp.where(_head_mask(h), k, 0.0).astype(MXU)
            vs_ref[h] = jnp.where(_head_mask(h), v, 0.0).astype(MXU)

    sds = jax.ShapeDtypeStruct((XHEADS, NMEM, XW), MXU)
    return pl.pallas_call(body, name="kv_fwd", out_shape=(sds, sds), compiler_params=_cparams())(mem, wkv)


def _kv_bwd(mem, dks_l, dvs_l):
    def body(mem_ref, *refs):
        dks_refs, dvs_refs, out_ref = refs[:DEPTH], refs[DEPTH:2 * DEPTH], refs[2 * DEPTH]
        dk = jnp.zeros((NMEM, XW), F32)
        dv = jnp.zeros((NMEM, XW), F32)
        for h in range(XHEADS):
            m = _head_mask(h)
            for l in range(DEPTH):
                dk = dk + jnp.where(m, dks_refs[l][h], 0.0)
                dv = dv + jnp.where(m, dvs_refs[l][h], 0.0)
        out_ref[...] = _tn(mem_ref[...], jnp.concatenate([dk, dv], axis=1)).astype(WIRE)

    return pl.pallas_call(body, name="kv_bwd", out_shape=jax.ShapeDtypeStruct((D, 2 * XW), WIRE),
                          compiler_params=_cparams())(mem, *dks_l, *dvs_l)


def _prep_weights(w_ins, w_out, wkv):
    def body(a_ref, b_ref, c_ref, d_ref, wo_ref, kv_ref, ao, bo, co, do, wo0, wo1, wo2, wo3, kvo):
        for src, dst in ((a_ref, ao), (b_ref, bo), (c_ref, co), (d_ref, do)):
            dst[...] = src[...].T.astype(MXU)
        for l, dst in enumerate((wo0, wo1, wo2, wo3)):
            dst[...] = wo_ref[l].astype(MXU)
        kvo[...] = kv_ref[...].astype(MXU)

    outs = [jax.ShapeDtypeStruct((w.shape[1], w.shape[0]), MXU) for w in w_ins]
    outs += [jax.ShapeDtypeStruct(w_out.shape[1:], MXU)] * DEPTH + [jax.ShapeDtypeStruct(wkv.shape, MXU)]
    return pl.pallas_call(body, name="prep_weights", out_shape=outs, compiler_params=_cparams())(*w_ins, w_out, wkv)


def _adam_math(w, g, m, v):
    m = B1 * m + (1.0 - B1) * g
    v = B2 * v + (1.0 - B2) * (g * g)
    m_hat = m / (1.0 - B1 ** STEP)
    v_hat = v / (1.0 - B2 ** STEP)
    delta = -LR * (m_hat / (jnp.sqrt(v_hat) + EPS) + WD * w)
    return delta, m, v


def _sum_adam(name, recv, w, m, v, transpose):
    rows, cols = recv.shape[1], recv.shape[2]

    def body(r_ref, w_ref, m_ref, v_ref, g_out, d_out, m_out, v_out, acc_ref):
        s = pl.program_id(0)

        @pl.when(s == 0)
        def _():
            acc_ref[...] = r_ref[...].astype(F32)

        @pl.when(s > 0)
        def _():
            acc_ref[...] += r_ref[...].astype(F32)

        @pl.when(s == NDEV - 1)
        def _():
            g = acc_ref[...].T if transpose else acc_ref[...]
            d, mn, vn = _adam_math(w_ref[...], g, m_ref[...], v_ref[...])
            g_out[...] = g
            d_out[...] = d
            m_out[...] = mn
            v_out[...] = vn

    sds = jax.ShapeDtypeStruct(w.shape, F32)
    ins = [(recv, pl.BlockSpec((None, rows, cols), lambda s: (s, 0, 0))), (w, _res(w)), (m, _res(m)), (v, _res(v))]
    outs = [(sds, _res_sds(w.shape))] * 4
    return _call(body, name, (NDEV,), ins, outs, [pltpu.VMEM((rows, cols), F32)])


def _bias_finalize(dbs_exp):
    def body(dbs_ref, dabs_ref):
        dabs_ref[...] = jnp.sum(dbs_ref[...], axis=-1)

    return pl.pallas_call(body, name="bias_finalize", out_shape=jax.ShapeDtypeStruct((NH, HD), F32),
                          compiler_params=_cparams())(dbs_exp)


def _lb_finalize(dlb, lb_logits):
    def body(dlb_ref, lg_ref, dlg_ref):
        total = jnp.zeros((DEPTH, TOK), F32)
        lg = lg_ref[...]
        e = jnp.exp(lg - jnp.max(lg, axis=0, keepdims=True))
        p = e / jnp.sum(e, axis=0, keepdims=True)
        row = lax.broadcasted_iota(jnp.int32, (DEPTH, TOK), 0)
        for layer in range(DEPTH):
            if layer % 4 != 1:
                continue
            dp = jnp.where((row >= 1) & (row <= layer), dlb_ref[...], 0.0)
            total = total + p * (dp - jnp.sum(p * dp, axis=0, keepdims=True))
        dlg_ref[...] = total

    return pl.pallas_call(body, name="lb_finalize", out_shape=jax.ShapeDtypeStruct((DEPTH, TOK), F32),
                          compiler_params=_cparams())(dlb, lb_logits)


def _small_sum_adam(name, gathered, w, m, v):
    rows = w.shape[0]

    def body(r_ref, w_ref, m_ref, v_ref, g_out, d_out, m_out, v_out):
        g = r_ref[0]
        for s in range(1, NDEV):
            g = g + r_ref[s]
        d, mn, vn = _adam_math(w_ref[...], g, m_ref[...], v_ref[...])
        g_out[...] = g
        d_out[...] = d
        m_out[...] = mn
        v_out[...] = vn

    sds = jax.ShapeDtypeStruct((rows, 128), F32)
    return pl.pallas_call(body, name=name, out_shape=(sds,) * 4, compiler_params=_cparams())(gathered, w, m, v)


def _group_adam(name, recvs, params):
    nk = len(recvs)

    def body(*refs):
        pos, oi = nk, nk + 3 * sum(p is not None for p in params)
        for k in range(nk):
            g = refs[k][0]
            for s in range(1, NDEV):
                g = g + refs[k][s]
            refs[oi][...] = g
            oi += 1
            if params[k] is not None:
                d, mn, vn = _adam_math(refs[pos][...], g, refs[pos + 1][...], refs[pos + 2][...])
                refs[oi][...] = d
                refs[oi + 1][...] = mn
                refs[oi + 2][...] = vn
                pos += 3
                oi += 3

    out_shape, counts = [], []
    for k in range(nk):
        counts.append(4 if params[k] is not None else 1)
        out_shape += [jax.ShapeDtypeStruct(recvs[k].shape[1:], F32)] * counts[-1]
    args = list(recvs) + [a for p in params if p is not None for a in p]
    flat = pl.pallas_call(body, name=name, out_shape=out_shape, compiler_params=_cparams())(*args)
    res, o = [], 0
    for cnt in counts:
        res.append(flat[o:o + cnt])
        o += cnt
    return res


def _adam_only(g, w, m, v):
    def body(g_ref, w_ref, m_ref, v_ref, d_out, m_out, v_out):
        d, mn, vn = _adam_math(w_ref[...], g_ref[...], m_ref[...], v_ref[...])
        d_out[...] = d
        m_out[...] = mn
        v_out[...] = vn

    sds = jax.ShapeDtypeStruct(w.shape, F32)
    return pl.pallas_call(body, name="shard_adam", out_shape=(sds,) * 3, compiler_params=_cparams())(g, w, m, v)


def _me_and_peers():
    x, y, c = lax.axis_index("x"), lax.axis_index("y"), lax.axis_index("c")
    me = 4 * x + 2 * y + c
    peers = []
    for k in range(1, NDEV):
        kx, ky, kc = (k >> 2) & 1, (k >> 1) & 1, k & 1
        px = x + kx - 2 * x * kx
        py = y + ky - 2 * y * ky
        pc = c + kc - 2 * c * kc
        peers.append(((px, py, pc), 4 * px + 2 * py + pc))
    return me, peers


_ANY = pl.BlockSpec(memory_space=pl.ANY)


class _Exchange:
    def __init__(self, arrays, split):
        self.arrays = list(arrays)
        self.split = list(split)
        n = len(self.arrays)
        self.out_shapes = []
        for a, sp in zip(self.arrays, self.split):
            rows = a.shape[0] // NDEV if sp else a.shape[0]
            self.out_shapes.append(jax.ShapeDtypeStruct((NDEV, rows, a.shape[1]), a.dtype))
        self.scratch = [pltpu.SemaphoreType.DMA((n, NDEV - 1)), pltpu.SemaphoreType.DMA((n, NDEV - 1)),
                        pltpu.SemaphoreType.DMA((n,))]

    def _block(self, src, t, d):
        if not self.split[t]:
            return src[t]
        rows = self.arrays[t].shape[0] // NDEV
        return src[t].at[pl.ds(d * rows, rows)]

    def start(self, src, dst, sems):
        send_sems, recv_sems, local_sems = sems
        me, peers = _me_and_peers()
        for t in range(len(self.arrays)):
            pltpu.make_async_copy(self._block(src, t, me), dst[t].at[me], local_sems.at[t]).start()
        for k, (dev, idx) in enumerate(peers):
            for t in range(len(self.arrays)):
                pltpu.make_async_remote_copy(src_ref=self._block(src, t, idx), dst_ref=dst[t].at[me],
                                             send_sem=send_sems.at[t, k], recv_sem=recv_sems.at[t, k],
                                             device_id=dev, device_id_type=pl.DeviceIdType.MESH).start()

    def wait(self, src, dst, sems):
        send_sems, recv_sems, local_sems = sems
        me, peers = _me_and_peers()

        def slot_copy(t, k, dev, idx):
            return pltpu.make_async_remote_copy(src_ref=dst[t].at[idx], dst_ref=dst[t].at[idx], send_sem=send_sems.at[t, k],
                                                recv_sem=recv_sems.at[t, k], device_id=dev,
                                                device_id_type=pl.DeviceIdType.MESH)

        for k, (dev, idx) in enumerate(peers):
            for t in range(len(self.arrays)):
                slot_copy(t, k, dev, idx).wait_recv()
        for k, (dev, idx) in enumerate(peers):
            for t in range(len(self.arrays)):
                slot_copy(t, k, dev, idx).wait_send()
        for t in range(len(self.arrays)):
            pltpu.make_async_copy(dst[t].at[me], dst[t].at[me], local_sems.at[t]).wait()

    def gather_by_chip(self, src, dst, sems):
        assert not any(self.split)
        send_sems, recv_sems, local_sems = sems
        n = len(self.arrays)
        x, y, c = lax.axis_index("x"), lax.axis_index("y"), lax.axis_index("c")
        me, sibling = 4 * x + 2 * y + c, (x, y, 1 - c)
        chips = [(1 - x, y), (x, 1 - y), (1 - x, 1 - y)]

        def index(chip, core):
            return 4 * chip[0] + 2 * chip[1] + core

        def copy(t, k, block, to, from_src):
            return pltpu.make_async_remote_copy(src_ref=src[t] if from_src else dst[t].at[block], dst_ref=dst[t].at[block],
                                                send_sem=send_sems.at[t, k], recv_sem=recv_sems.at[t, k],
                                                device_id=to, device_id_type=pl.DeviceIdType.MESH)

        local = [pltpu.make_async_copy(src[t], dst[t].at[me], local_sems.at[t]) for t in range(n)]
        for cp in local:
            cp.start()
        sends = []
        for t in range(n):
            sends.append(copy(t, 0, me, sibling, True))
            sends += [copy(t, 1 + j, me, (*chip, c), True) for j, chip in enumerate(chips)]
        for cp in sends:
            cp.start()
        for j, chip in enumerate(chips):
            for t in range(n):
                copy(t, 1 + j, index(chip, c), sibling, False).wait_recv()
                passed = copy(t, 4 + j, index(chip, c), sibling, False)
                passed.start()
                sends.append(passed)
        for t in range(n):
            copy(t, 0, index((x, y), 1 - c), sibling, False).wait_recv()
            for j, chip in enumerate(chips):
                copy(t, 4 + j, index(chip, 1 - c), sibling, False).wait_recv()
        for cp in sends:
            cp.wait_send()
        for cp in local:
            cp.wait()

    def run(self, name, by_chip=False):
        n = len(self.arrays)

        def body(*refs):
            src, dst, sems = refs[:n], refs[n:2 * n], refs[2 * n:]
            if by_chip:
                self.gather_by_chip(src, dst, sems)
                return
            self.start(src, dst, sems)
            self.wait(src, dst, sems)

        return pl.pallas_call(
            body, name=name, out_shape=self.out_shapes, in_specs=[_ANY] * n, out_specs=[_ANY] * n,
            scratch_shapes=self.scratch,
        )(*self.arrays)


SMALL = [("ln_g", (DEPTH, D), False), ("ln_b", (DEPTH, D), False), ("hgrn_lb_logits", (DEPTH, TOK), False),
         ("a_w_s", (1, NH, HD, HD), False), ("a_b_s", (1, NH, HD), False), ("b_norm_g", (1, TOK), True),
         ("c_w_pool", (1, 4, POOL_GROUP, POOL_GROUP), False), ("c_scale", (1, TOK), True),
         ("d_conv_w", (1, 4, TOK), True), ("d_conv_b", (1, TOK), True),
         ("d_w_gx", (1, NH, HD, HD), False), ("d_b_gx", (1, NH, HD), False),
         ("d_w_ga", (1, NH, HD, HD), False), ("d_b_ga", (1, NH, HD), False), ("d_a_param", (1, TOK), True)]


def _pack(parts, total_rows):
    flat = jnp.concatenate([p.reshape(-1).astype(F32) for p in parts])
    flat = jnp.pad(flat, (0, total_rows * 128 - flat.shape[0]))
    return flat.reshape(total_rows, 128)


def _size(shape):
    n = 1
    for s in shape:
        n *= s
    return n


def _rows_for(n):
    return -(-n // 1024) * 8


def kernel(x, mem, mem_kv_w, ln_g, ln_b, w_out, hgrn_lb_logits, a_w_in, a_w_s, a_b_s, b_w_in, b_norm_g, c_w_in, c_w_pool, c_scale, d_w_in, d_conv_w, d_conv_b, d_w_gx, d_b_gx, d_w_ga, d_b_ga, d_a_param, loss_target, m_mem_kv_w, m_ln_g, m_ln_b, m_w_out, m_hgrn_lb_logits, m_a_w_in, m_a_w_s, m_a_b_s, m_b_w_in, m_b_norm_g, m_c_w_in, m_c_w_pool, m_c_scale, m_d_w_in, m_d_conv_w, m_d_conv_b, m_d_w_gx, m_d_b_gx, m_d_w_ga, m_d_b_ga, m_d_a_param, v_mem_kv_w, v_ln_g, v_ln_b, v_w_out, v_hgrn_lb_logits, v_a_w_in, v_a_w_s, v_a_b_s, v_b_w_in, v_b_norm_g, v_c_w_in, v_c_w_pool, v_c_scale, v_d_w_in, v_d_conv_w, v_d_conv_b, v_d_w_gx, v_d_b_gx, v_d_w_ga, v_d_b_ga, v_d_a_param):
    W = dict(mem_kv_w=mem_kv_w, ln_g=ln_g, ln_b=ln_b, w_out=w_out, hgrn_lb_logits=hgrn_lb_logits, a_w_in=a_w_in, a_w_s=a_w_s,
             a_b_s=a_b_s, b_w_in=b_w_in, b_norm_g=b_norm_g, c_w_in=c_w_in, c_w_pool=c_w_pool, c_scale=c_scale, d_w_in=d_w_in,
             d_conv_w=d_conv_w, d_conv_b=d_conv_b, d_w_gx=d_w_gx, d_b_gx=d_b_gx, d_w_ga=d_w_ga, d_b_ga=d_b_ga, d_a_param=d_a_param)
    M = dict(mem_kv_w=m_mem_kv_w, ln_g=m_ln_g, ln_b=m_ln_b, w_out=m_w_out, hgrn_lb_logits=m_hgrn_lb_logits, a_w_in=m_a_w_in,
             a_w_s=m_a_w_s, a_b_s=m_a_b_s, b_w_in=m_b_w_in, b_norm_g=m_b_norm_g, c_w_in=m_c_w_in, c_w_pool=m_c_w_pool,
             c_scale=m_c_scale, d_w_in=m_d_w_in, d_conv_w=m_d_conv_w, d_conv_b=m_d_conv_b, d_w_gx=m_d_w_gx, d_b_gx=m_d_b_gx,
             d_w_ga=m_d_w_ga, d_b_ga=m_d_b_ga, d_a_param=m_d_a_param)
    V = dict(mem_kv_w=v_mem_kv_w, ln_g=v_ln_g, ln_b=v_ln_b, w_out=v_w_out, hgrn_lb_logits=v_hgrn_lb_logits, a_w_in=v_a_w_in,
             a_w_s=v_a_w_s, a_b_s=v_a_b_s, b_w_in=v_b_w_in, b_norm_g=v_b_norm_g, c_w_in=v_c_w_in, c_w_pool=v_c_w_pool,
             c_scale=v_c_scale, d_w_in=v_d_w_in, d_conv_w=v_d_conv_w, d_conv_b=v_d_conv_b, d_w_gx=v_d_w_gx, d_b_gx=v_d_b_gx,
             d_w_ga=v_d_w_ga, d_b_ga=v_d_b_ga, d_a_param=v_d_a_param)
    me = 4 * lax.axis_index("x") + 2 * lax.axis_index("y") + lax.axis_index("c")
    x2, mem2, tgt2 = x[0], mem[0], loss_target[0]
    in_names = ["a_w_in", "b_w_in", "c_w_in", "d_w_in"]

    shard_names = [n for n, _, sh in SMALL if sh]
    small_shard = _pack([W[n] for n in shard_names], 8)
    wts = _prep_weights([W[n][0] for n in in_names], w_out, mem_kv_w)
    wt_sh, wo_sh, wkv_sh = wts[:4], wts[4:8], wts[8]
    g0 = _Exchange([wt_sh[0], wo_sh[0], wkv_sh, small_shard], [False] * 4).run("gather_first", by_chip=True)
    wt_full = [g0[0].reshape(-1, D)]
    wout_full = [g0[1].reshape(D, D)]
    wkv_full = g0[2].reshape(D, 2 * XW)
    sm = g0[3].reshape(NDEV, 1024)
    full_small = {}
    off = 0
    for n, shape, _ in [s for s in SMALL if s[2]]:
        per = _size(shape) // NDEV
        blk = sm[:, off:off + per]
        if n == "d_conv_w":
            full_small[n] = blk.reshape(NDEV, 4, TOK // NDEV).transpose(1, 0, 2).reshape(4, TOK)
        else:
            full_small[n] = blk.reshape(1, TOK)
        off += per

    ks, vs = _kv_fwd(mem2, wkv_full)
    tri_bs = jnp.broadcast_to(a_b_s[0][:, :, None], (NH, HD, HD))
    wbd = jnp.zeros((TOK, TOK), F32)
    for g in range(4):
        wbd = lax.dynamic_update_slice(wbd, c_w_pool[0, g], (g * POOL_GROUP, g * POOL_GROUP))
    wbd = wbd.astype(MXU)
    prm = {0: [a_w_s[0], tri_bs],
           1: [hgrn_lb_logits, full_small["b_norm_g"]],
           2: [wbd, full_small["c_scale"]],
           3: [full_small["d_conv_w"], full_small["d_conv_b"], d_w_gx[0].astype(MXU), d_b_gx[0].reshape(1, TOK),
               d_w_ga[0].astype(MXU), d_b_ga[0].reshape(1, TOK), full_small["d_a_param"]]}
    ones = jnp.ones((1, D), F32)
    zeros = jnp.zeros((1, D), F32)
    xs, gs, bs = [x2], [ones], [zeros]
    saved = []
    for i in range(DEPTH):
        ride = _Exchange([wt_sh[i + 1], wo_sh[i + 1]], [False, False]) if i + 1 < DEPTH else None
        res = _layer_fwd(i, i, xs[i], gs[i], bs[i], wt_full[i], wout_full[i], ks, vs, prm[i], ride)
        if ride:
            wt_full.append(res[-2].reshape(-1, D))
            wout_full.append(res[-1].reshape(D, D))
            res = res[:-2]
        saved.append(res)
        xs.append(res[1])
        gs.append(ln_g[i:i + 1])
        bs.append(ln_b[i:i + 1])

    up = tgt2
    grads = {}
    dks_l, dvs_l, dwt_l, dwout_l, dlng_l, dlnb_l = [], [], [], [], [], []
    recv_wt, recv_wo = [None] * DEPTH, [None] * DEPTH
    loss_part = None
    sharded = {n for n, _, sh in SMALL if sh}
    group = {3: ["ln_g#3", "ln_b#3", "d_conv_w", "d_conv_b", "d_w_gx", "d_b_gx", "d_w_ga", "d_b_ga", "d_a_param"],
             2: ["ln_g#2", "ln_b#2", "c_w_pool", "c_scale"],
             1: ["ln_g#1", "ln_b#1", "hgrn_lb_logits", "b_norm_g"],
             0: ["ln_g#0", "ln_b#0", "a_w_s", "a_b_s", "loss"]}

    def small_of(l):
        return [grads[e].reshape(-1, grads[e].shape[-1]) for e in group[l]]

    recv_small = [None] * DEPTH
    for i in reversed(range(DEPTH)):
        res = saved[i]
        extra = None if len(res) <= 4 else (res[4] if len(res) == 5 else res[4:])
        ride = None
        if i + 1 < DEPTH:
            smalls = small_of(i + 1)
            ride = _Exchange([dwt_l[-1], dwout_l[-1]] + smalls, [True, True] + [False] * len(smalls))
        out = _layer_bwd(i, i, up, i == DEPTH - 1, res[1], res[2], res[3], gs[i + 1], bs[i + 1], res[0], wout_full[i], ks, vs,
                         prm[i], extra, ride)
        if ride:
            nr = len(ride.arrays)
            recv_wt[i + 1], recv_wo[i + 1], recv_small[i + 1] = out[-nr], out[-nr + 1], out[-nr + 2:]
            out = out[:-nr]
        dres, dproj, dwout_i, dks_i, dvs_i, dg_i, db_i, loss_i = out[:8]
        pg = out[8:]
        if i == DEPTH - 1:
            grads["loss"] = loss_i
        dks_l.append(dks_i)
        dvs_l.append(dvs_i)
        dwout_l.append(dwout_i)
        grads[f"ln_g#{i}"], grads[f"ln_b#{i}"] = dg_i, db_i
        if i == 0:
            grads["a_w_s"], dbs_exp = pg
            grads["a_b_s"] = _bias_finalize(dbs_exp)
        elif i == 1:
            dlb, grads["b_norm_g"] = pg
            grads["hgrn_lb_logits"] = _lb_finalize(dlb, hgrn_lb_logits)
        elif i == 2:
            dwbd, grads["c_scale"] = pg
            grads["c_w_pool"] = jnp.stack([lax.dynamic_slice(dwbd, (g * POOL_GROUP, g * POOL_GROUP), (POOL_GROUP, POOL_GROUP))
                                           for g in range(4)])
        else:
            (grads["d_conv_w"], grads["d_conv_b"], grads["d_w_gx"], grads["d_b_gx"], grads["d_w_ga"], grads["d_b_ga"],
             grads["d_a_param"]) = pg
        ride = None
        if i == 0:
            dwkv = _kv_bwd(mem2, dks_l, dvs_l)
            smalls = small_of(0)
            ride = _Exchange([dwout_i, dwkv] + smalls, [True, True] + [False] * len(smalls))
        pb = _proj_bwd(i, dproj, dres, xs[i], gs[i], bs[i], wt_full[i], ride)
        up, dwt = pb[:2]
        if ride:
            recv_wo[0], recv_kv, recv_small[0] = pb[2], pb[3], pb[4:]
        dwt_l.append(dwt)
    grad_x = up[None]

    recv_wt[0], = _Exchange([dwt_l[-1]], [True]).run("scatter_last")

    outs = {}
    for t, n in enumerate(in_names):
        g, d, mn, vn = _sum_adam(f"adam_{n}", recv_wt[t], W[n][0], M[n][0], V[n][0], True)
        outs[n] = (g[None], d[None], mn[None], vn[None])
    wo_res = [_sum_adam(f"adam_w_out{l}", recv_wo[l], w_out[l], m_w_out[l], v_w_out[l], False) for l in range(DEPTH)]
    outs["w_out"] = tuple(jnp.stack([wo_res[l][j] for l in range(DEPTH)]) for j in range(4))
    outs["mem_kv_w"] = _sum_adam("adam_mem_kv_w", recv_kv, mem_kv_w, m_mem_kv_w, v_mem_kv_w, False)

    def entry_of(tree, e, like):
        if "#" in e:
            n, l = e.split("#")
            return tree[n][int(l):int(l) + 1]
        return tree[e].reshape(like.shape[1:])

    small = [{}, {}, {}, {}]
    for l in range(DEPTH):
        params = [None if (e == "loss" or e in sharded) else tuple(entry_of(t, e, r) for t in (W, M, V))
                  for e, r in zip(group[l], recv_small[l])]
        for e, res in zip(group[l], _group_adam(f"small_adam{l}", recv_small[l], params)):
            for j, a in enumerate(res):
                small[j][e] = a
    loss = small[0]["loss"][0, 0]
    for j in range(4):
        for n in ("ln_g", "ln_b"):
            small[j][n] = jnp.concatenate([small[j][f"{n}#{l}"] for l in range(DEPTH)], axis=0)
    g_small = small[0]
    for n, _, sh in SMALL:
        if not sh:
            outs[n] = tuple(small[j][n] for j in range(4))
    per = TOK // NDEV
    g_sh = {n: lax.dynamic_slice_in_dim(g_small[n], me * per, per, axis=1) for n, s, sh in SMALL if sh}
    gp = _pack([g_sh[n] for n in shard_names], 8)
    d_p, m_p, v_p = _adam_only(gp, small_shard, _pack([M[n] for n in shard_names], 8), _pack([V[n] for n in shard_names], 8))
    o = 0
    for n in shard_names:
        cnt = _size(W[n].shape)
        outs[n] = (g_sh[n],) + tuple(t.reshape(-1)[o:o + cnt].reshape(W[n].shape) for t in (d_p, m_p, v_p))
        o += cnt

    order = ["mem_kv_w", "ln_g", "ln_b", "w_out", "hgrn_lb_logits", "a_w_in", "a_w_s", "a_b_s", "b_w_in", "b_norm_g", "c_w_in",
             "c_w_pool", "c_scale", "d_w_in", "d_conv_w", "d_conv_b", "d_w_gx", "d_b_gx", "d_w_ga", "d_b_ga", "d_a_param"]
    result = [loss, grad_x]
    for j in range(4):
        result += [outs[n][j].reshape(W[n].shape) for n in order]
    return tuple(result)
```

```python
import functools

import jax
import jax.numpy as jnp
from jax import lax
from jax.experimental import pallas as pl
from jax.experimental.pallas import tpu as pltpu

F32 = jnp.float32
MXU = jnp.bfloat16
WIRE = jnp.bfloat16

D = 1024
TOK = 768
XW = 256
NMEM = 256
XHEADS = 4
XSCALE = 64 ** -0.5
NH = 6
HD = 128
CH = 16
POOL_WINDOWS = (2, 4, 8, 16)
POOL_GROUP = 192
DEPTH = 4
ALPHA = (2 * DEPTH) ** 0.25
LN_EPS = 1e-5
RMS_EPS = 1e-6
LRU_C = 8.0
B1, B2, LR, EPS, WD, STEP = 0.9, 0.999, 0.001, 1e-8, 0.01, 10

NDEV = 8
TS_FWD = {0: 512, 1: 256, 2: 512, 3: 256}
TS_BWD = {0: 512, 1: 256, 2: 512, 3: 256}
TSB = 512
VMEM_LIMIT = 58 * 1024 * 1024
VMEM_LIMIT_WIDE = 62 * 1024 * 1024

KIND_WIDTHS = {0: 2 * TOK + XW + D, 1: 3 * TOK + XW + D, 2: TOK + XW + D, 3: TOK + XW + D}


def _mm(a, b, ca, cb):
    return lax.dot_general(a.astype(MXU), b.astype(MXU), (((ca,), (cb,)), ((), ())), preferred_element_type=F32)


def _nn(a, b):
    return _mm(a, b, 1, 0)


def _nt(a, b):
    return _mm(a, b, 1, 1)


def _tn(a, b):
    return _mm(a, b, 0, 0)


def _bmm(a, b, ca, cb):
    return lax.dot_general(a.astype(MXU), b.astype(MXU), (((ca,), (cb,)), ((0,), (0,))), preferred_element_type=F32)


def _sigmoid(x):
    return 1.0 / (1.0 + jnp.exp(-x))


def _vjp1(fn, x, dy):
    return jax.vjp(fn, x)[1](dy)[0]


def _rowsum(x):
    return jnp.sum(x, axis=0, keepdims=True)


def _row(x, r):
    sel = lax.broadcasted_iota(jnp.int32, x.shape, 0) == r
    return jnp.sum(jnp.where(sel, x, 0.0), axis=0, keepdims=True)


def _acc(ref, val):
    ref[...] += val


def _cparams(sem=None, vmem=VMEM_LIMIT):
    return pltpu.CompilerParams(dimension_semantics=sem, vmem_limit_bytes=vmem)


def _res(a):
    nd = a.ndim
    return pl.BlockSpec(a.shape, lambda i: (0,) * nd)


def _res_sds(shape):
    nd = len(shape)
    return pl.BlockSpec(shape, lambda i: (0,) * nd)


def _row_spec(width, nt, rev, ts):
    if rev:
        return pl.BlockSpec((ts, width), lambda i: (nt - 1 - i, 0))
    return pl.BlockSpec((ts, width), lambda i: (i, 0))


def _call(body, name, grid, ins, outs, scratch=(), sem=("arbitrary",), vmem=VMEM_LIMIT):
    arrays = [a for a, _ in ins]
    return pl.pallas_call(
        body, name=name, grid=grid,
        in_specs=[s for _, s in ins],
        out_specs=[s for _, s in outs],
        out_shape=[o for o, _ in outs],
        scratch_shapes=list(scratch),
        compiler_params=_cparams(sem, vmem),
    )(*arrays)


def _xattn_fwd(qx, ks_ref, vs_ref):
    o = None
    ps = []
    for h in range(XHEADS):
        s = _nt(qx, ks_ref[h]) * XSCALE
        s = s - jnp.max(s, axis=-1, keepdims=True)
        e = jnp.exp(s)
        p = e * (1.0 / jnp.sum(e, axis=-1, keepdims=True))
        ps.append(p)
        oh = _nn(p, vs_ref[h])
        o = oh if o is None else o + oh
    return o, ps


def _xattn_bwd(qx, ps, dxo, ks_ref, vs_ref, dks_ref, dvs_ref):
    dq = None
    for h in range(XHEADS):
        p = ps[h]
        dp = _nt(dxo, vs_ref[h])
        ds = p * (dp - jnp.sum(dp * p, axis=-1, keepdims=True))
        dqh = _nn(ds, ks_ref[h]) * XSCALE
        dq = dqh if dq is None else dq + dqh
        dks_ref[h] += _tn(ds, qx) * XSCALE
        dvs_ref[h] += _tn(p, dxo)
    return dq


def _tril128():
    r = lax.broadcasted_iota(jnp.int32, (HD, HD), 0)
    c = lax.broadcasted_iota(jnp.int32, (HD, HD), 1)
    return c <= r


GELU_C = 0.7978845608028654
GELU_K = 0.044715


def _gelu(x):
    th = jnp.tanh(GELU_C * (x + GELU_K * (x * x * x)))
    return 0.5 * x * (1.0 + th), th


def _gelu_grad(x, th):
    return 0.5 * (1.0 + th) + 0.5 * x * (1.0 - th * th) * (GELU_C * (1.0 + 3.0 * GELU_K * (x * x)))


def _gmlp_fwd(u, v, ws_ref, bs_ref):
    ts = u.shape[0]
    ug, thu = _gelu(u)
    vg, thv = _gelu(v)
    tri = _tril128()
    toks, res = [], []
    for g in range(NH):
        sl = slice(g * HD, (g + 1) * HD)
        vgh = vg[:, sl]
        cen = vgh - jnp.mean(vgh, axis=-1, keepdims=True)
        rstd = lax.rsqrt(jnp.mean(cen * cen, axis=-1, keepdims=True) + LN_EPS)
        vn = cen * rstd
        w = jnp.where(tri, ws_ref[g], 0.0).astype(MXU)
        mix = jnp.concatenate([_nn(w, vn[n * HD:(n + 1) * HD]) + bs_ref[g] for n in range(ts // HD)], axis=0)
        toks.append(ug[:, sl] * mix)
        res.append((vn, rstd, mix, w))
    return jnp.concatenate(toks, axis=1), (ug, res, thu, thv)


def _gmlp_bwd(u, v, fres, dtok, dws_ref, dbs_ref):
    ts = u.shape[0]
    ug, res, thu, thv = fres
    tri = _tril128()
    dugs, dvgs = [], []
    for g in range(NH):
        sl = slice(g * HD, (g + 1) * HD)
        vn, rstd, mix, w = res[g]
        dmix = dtok[:, sl] * ug[:, sl]
        dugs.append(dtok[:, sl] * mix)
        dvn_rows = []
        dw = None
        dbs = None
        for n in range(ts // HD):
            dm = dmix[n * HD:(n + 1) * HD]
            dvn_rows.append(_tn(w, dm))
            t = _nt(dm, vn[n * HD:(n + 1) * HD])
            dw = t if dw is None else dw + t
            dbs = dm if dbs is None else dbs + dm
        dws_ref[g] += jnp.where(tri, dw, 0.0)
        dbs_ref[g] += dbs
        dvn = jnp.concatenate(dvn_rows, axis=0)
        dvgs.append(rstd * (dvn - jnp.mean(dvn, axis=-1, keepdims=True) - vn * jnp.mean(dvn * vn, axis=-1, keepdims=True)))
    du = jnp.concatenate(dugs, axis=1) * _gelu_grad(u, thu)
    dv = jnp.concatenate(dvgs, axis=1) * _gelu_grad(v, thv)
    return du, dv


def _chunk_cumsum(x):
    row = lax.broadcasted_iota(jnp.int32, x.shape, 0) % CH
    for s in (1, 2, 4, 8):
        x = x + jnp.where(row >= s, pltpu.roll(x, s, 0), 0.0)
    return x


def _chunk_revcumsum(x):
    n = x.shape[0]
    row = lax.broadcasted_iota(jnp.int32, x.shape, 0) % CH
    for s in (1, 2, 4, 8):
        x = x + jnp.where(row < CH - s, pltpu.roll(x, n - s, 0), 0.0)
    return x


def _chunk_sum(x):
    n, w = x.shape
    return jnp.sum(x.reshape(n // CH, CH, w), axis=1)


def _chunk_bcast(c, n):
    nch, w = c.shape
    return jnp.broadcast_to(c[:, None, :], (nch, CH, w)).reshape(n, w)


def _lower_bound(lb_logits, layer):
    lg = lb_logits
    e = jnp.exp(lg - jnp.max(lg, axis=0, keepdims=True))
    p = e / jnp.sum(e, axis=0, keepdims=True)
    row = lax.broadcasted_iota(jnp.int32, p.shape, 0)
    lb = jnp.sum(jnp.where((row >= 1) & (row <= layer), p, 0.0), axis=0, keepdims=True)
    return lb, p


def _hgrn_prep(q, fl, lb):
    n = q.shape[0]
    sg = _sigmoid(fl)
    f = lb + (1.0 - lb) * sg
    lf = jnp.log(f)
    sq = _sigmoid(q)
    g = _chunk_cumsum(lf)
    tot = _chunk_sum(lf)
    gl = _chunk_bcast(tot, n)
    eg = jnp.exp(g)
    eng = jnp.exp(-g)
    egl = jnp.exp(gl - g)
    k = 1.0 - f
    qf = q * sq
    return dict(sg=sg, f=f, k=k, sq=sq, qf=qf, eg=eg, eng=eng, egl=egl,
                qd=qf * eg, ki=k * eng, ke=k * egl, dch=jnp.exp(tot))


def _hgrn_mask():
    r = lax.broadcasted_iota(jnp.int32, (HD, HD), 0)
    c = lax.broadcasted_iota(jnp.int32, (HD, HD), 1)
    return (r // CH == c // CH) & (c <= r)


def _hgrn_states(v3, ke3, dch_h, st_in):
    nch = v3.shape[0]
    ut = _bmm(v3, ke3, 1, 1)
    dfull = jnp.broadcast_to(dch_h[:, None, :], (nch, HD, HD))
    st, sts = st_in, []
    for c in range(nch):
        sts.append(st)
        st = st * dfull[c] + ut[c]
    return jnp.stack(sts), st, dfull


def _hgrn_fwd(q, fl, inp, lb, ng, st_ref, sts_ref):
    n = q.shape[0]
    nch = n // CH
    pr = _hgrn_prep(q, fl, lb)
    mask = _hgrn_mask()
    toks, o_l, a_l = [], [], []
    qd_m, ki_m, ke_m, v_m = (t.astype(MXU) for t in (pr["qd"], pr["ki"], pr["ke"], inp))
    for h in range(NH):
        sl = slice(h * HD, (h + 1) * HD)
        qd, ki, ke, v = qd_m[:, sl], ki_m[:, sl], ke_m[:, sl], v_m[:, sl]
        qd3 = qd.reshape(nch, CH, HD)
        v3 = v.reshape(nch, CH, HD)
        ke3 = ke.reshape(nch, CH, HD)
        sts, st_ref[h], _ = _hgrn_states(v3, ke3, pr["dch"][:, sl], st_ref[h])
        sts_ref[h] = sts
        o = _bmm(qd3, sts, 2, 2).reshape(n, HD)
        intra, scores = [], []
        for b in range(n // HD):
            bs = slice(b * HD, (b + 1) * HD)
            a = jnp.where(mask, _nt(qd[bs], ki[bs]), 0.0).astype(MXU)
            scores.append(a)
            intra.append(_nn(a, v[bs]))
        o = o + jnp.concatenate(intra, axis=0)
        r = lax.rsqrt(jnp.mean(o * o, axis=-1, keepdims=True) + RMS_EPS)
        toks.append(o * r * ng[:, sl])
        o_l.append(o)
        a_l.append(jnp.concatenate(scores, axis=0))
    return jnp.concatenate(toks, axis=1), jnp.concatenate(o_l, axis=1), jnp.concatenate(a_l, axis=1)


def _hgrn_bwd(q, pr, inp, lb, ng, dtok, o_all, a_all, ststart_ref, dst_ref, dng_ref, dlb_ref):
    n = q.shape[0]
    nch = n // CH
    mask = _hgrn_mask()
    dqd_l, dki_l, dke_l, dv_l, ddch_l, dng_l, toks = [], [], [], [], [], [], []
    qd_m, ki_m, ke_m, v_m = (t.astype(MXU) for t in (pr["qd"], pr["ki"], pr["ke"], inp))
    for h in range(NH):
        sl = slice(h * HD, (h + 1) * HD)
        qd, ki, ke, v = qd_m[:, sl], ki_m[:, sl], ke_m[:, sl], v_m[:, sl]
        qd3 = qd.reshape(nch, CH, HD)
        v3 = v.reshape(nch, CH, HD)
        ke3 = ke.reshape(nch, CH, HD)
        sts = ststart_ref[h]
        dfull = jnp.broadcast_to(pr["dch"][:, sl][:, None, :], (nch, HD, HD))
        sts_m = sts.astype(MXU)
        o = o_all[:, sl]
        a_l = [a_all[b * HD:(b + 1) * HD, sl] for b in range(n // HD)]
        r = lax.rsqrt(jnp.mean(o * o, axis=-1, keepdims=True) + RMS_EPS)
        toks.append(o * r * ng[:, sl])
        dt = dtok[:, sl]
        dng_l.append(_rowsum(dt * o * r))
        dn = dt * ng[:, sl]
        do = r * dn - o * (r * r * r) * jnp.mean(dn * o, axis=-1, keepdims=True)
        do_m = do.astype(MXU)
        do3 = do_m.reshape(nch, CH, HD)
        dqd_rows, dki_rows, dv_rows = [], [], []
        for b in range(n // HD):
            bs = slice(b * HD, (b + 1) * HD)
            da = jnp.where(mask, _nt(do_m[bs], v[bs]), 0.0).astype(MXU)
            dqd_rows.append(_nn(da, ki[bs]))
            dki_rows.append(_tn(da, qd[bs]))
            dv_rows.append(_tn(a_l[b], do_m[bs]))
        dqd = jnp.concatenate(dqd_rows, axis=0) + _bmm(do3, sts_m, 2, 1).reshape(n, HD)
        dki = jnp.concatenate(dki_rows, axis=0)
        dv = jnp.concatenate(dv_rows, axis=0)
        wt = _bmm(do3, qd3, 1, 1)
        dst, dstn_l = dst_ref[h], [None] * nch
        for c in reversed(range(nch)):
            dstn_l[c] = dst
            dst = wt[c] + dst * dfull[c]
        dst_ref[h] = dst
        dstn = jnp.stack(dstn_l)
        dstn_m = dstn.astype(MXU)
        dv = dv + _bmm(ke3, dstn_m, 2, 2).reshape(n, HD)
        dke = _bmm(v3, dstn_m, 2, 1).reshape(n, HD)
        ddch_l.append(jnp.sum(sts * dstn, axis=1))
        dqd_l.append(dqd)
        dki_l.append(dki)
        dke_l.append(dke)
        dv_l.append(dv)
    dqd = jnp.concatenate(dqd_l, axis=1)
    dki = jnp.concatenate(dki_l, axis=1)
    dke = jnp.concatenate(dke_l, axis=1)
    dinp = jnp.concatenate(dv_l, axis=1)
    ddch = jnp.concatenate(ddch_l, axis=1)
    _acc(dng_ref, jnp.concatenate(dng_l, axis=1))
    dqf = dqd * pr["eg"]
    dke_ke = dke * pr["ke"]
    dg = dqd * pr["qd"] - dki * pr["ki"] - dke_ke
    dk = dki * pr["eng"] + dke * pr["egl"]
    dgl = _chunk_sum(dke_ke) + ddch * pr["dch"]
    dlf = _chunk_revcumsum(dg) + _chunk_bcast(dgl, n)
    df = dlf / pr["f"] - dk
    sg = pr["sg"]
    dfl = df * (1.0 - lb) * sg * (1.0 - sg)
    _acc(dlb_ref, _rowsum(df * (1.0 - sg)))
    sq = pr["sq"]
    dq = dqf * (sq * (1.0 + q * (1.0 - sq)))
    return jnp.concatenate(toks, axis=1), dq, dfl, dinp


def _pool_select(s2, s4, s8, s16):
    col = lax.broadcasted_iota(jnp.int32, (1, TOK), 1)
    return jnp.where(col < POOL_GROUP, s2, jnp.where(col < 2 * POOL_GROUP, s4, jnp.where(col < 3 * POOL_GROUP, s8, s16)))


def _pool_cnt(pos0, n):
    pos = pos0 + lax.broadcasted_iota(jnp.int32, (n, TOK), 0) + 1
    col = lax.broadcasted_iota(jnp.int32, (n, TOK), 1)
    w = jnp.where(col < POOL_GROUP, 2, jnp.where(col < 2 * POOL_GROUP, 4, jnp.where(col < 3 * POOL_GROUP, 8, 16)))
    return jnp.minimum(pos, w).astype(F32)


def _pool_fwd(p, halo, pos0, wbd, scale):
    n = p.shape[0]
    ext = jnp.concatenate([halo, p], axis=0)
    s2 = ext + pltpu.roll(ext, 1, 0)
    s4 = s2 + pltpu.roll(s2, 2, 0)
    s8 = s4 + pltpu.roll(s4, 4, 0)
    s16 = s8 + pltpu.roll(s8, 8, 0)
    win = _pool_select(s2, s4, s8, s16)[16:]
    cnt = _pool_cnt(pos0, n)
    diff = win / cnt - p
    y = _nn(diff, wbd)
    return y * scale, (diff, y, cnt)


def _pool_bwd(fres, dtok, nxt_ref, wbd, scale, dwbd_ref, dscale_ref):
    diff, y, cnt = fres
    n = diff.shape[0]
    _acc(dscale_ref, _rowsum(dtok * y))
    dy = dtok * scale
    ddiff = _nt(dy, wbd)
    dwbd_ref[...] += _tn(diff, dy)
    qv = ddiff / cnt
    ext = jnp.concatenate([qv, nxt_ref[...]], axis=0)
    m = n + 16
    s2 = ext + pltpu.roll(ext, m - 1, 0)
    s4 = s2 + pltpu.roll(s2, m - 2, 0)
    s8 = s4 + pltpu.roll(s4, m - 4, 0)
    s16 = s8 + pltpu.roll(s8, m - 8, 0)
    adj = _pool_select(s2, s4, s8, s16)[:n]
    nxt_ref[...] = qv[:16]
    return adj - ddiff


def _neg_expm1(x):
    return jnp.where(jnp.abs(x) < 1e-2, -x * (1.0 + x * (0.5 + x * (1.0 / 6.0))), 1.0 - jnp.exp(x))


def _softplus_neg(ap):
    return jnp.maximum(-ap, 0.0) + jnp.log(1.0 + jnp.exp(-jnp.abs(ap)))


def _lru_gates(xc, zx, za, ap, first):
    gx = _sigmoid(zx)
    ga = _sigmoid(za)
    sp = _softplus_neg(ap)
    log_a = -LRU_C * ga * sp
    a = jnp.exp(log_a)
    mult = jnp.sqrt(_neg_expm1(2.0 * log_a))
    mult = jnp.where(first, 1.0, mult)
    return a, mult * gx * xc, (gx, ga, sp, mult)


def _scan_fwd(a, b, h0):
    n = a.shape[0]
    row = lax.broadcasted_iota(jnp.int32, a.shape, 0)
    s = 1
    while s < n:
        keep = row >= s
        b = b + a * jnp.where(keep, pltpu.roll(b, s, 0), 0.0)
        a = a * jnp.where(keep, pltpu.roll(a, s, 0), 1.0)
        s *= 2
    return b + a * h0


def _scan_bwd(an, d, dh_next):
    n = an.shape[0]
    row = lax.broadcasted_iota(jnp.int32, an.shape, 0)
    s = 1
    while s < n:
        keep = row < n - s
        d = d + an * jnp.where(keep, pltpu.roll(d, n - s, 0), 0.0)
        an = an * jnp.where(keep, pltpu.roll(an, n - s, 0), 1.0)
        s *= 2
    return d + an * dh_next


def _lru_conv(xb, halo, cw_ref, cb):
    ext = jnp.concatenate([halo, xb], axis=0)
    sh = [pltpu.roll(ext, 3 - j, 0)[8:] if j < 3 else xb for j in range(4)]
    xc = cb
    for j in range(4):
        xc = xc + cw_ref[pl.ds(j, 1), :] * sh[j]
    return xc, sh


def _lru_fwd(xb, halo, pos0, prm, h0):
    cw, cb, wgx, bgx, wga, bga, ap = prm
    n = xb.shape[0]
    xc, sh = _lru_conv(xb, halo, cw, cb[...])
    zx = jnp.concatenate([_nn(xc[:, h * HD:(h + 1) * HD], wgx[h]) for h in range(NH)], axis=1) + bgx[...]
    za = jnp.concatenate([_nn(xc[:, h * HD:(h + 1) * HD], wga[h]) for h in range(NH)], axis=1) + bga[...]
    first = (pos0 + lax.broadcasted_iota(jnp.int32, (n, 1), 0)) == 0
    a, b, gates = _lru_gates(xc, zx, za, ap[...], first)
    hseq = _scan_fwd(a, b, h0)
    return hseq, (xc, sh, gates, first, a)


def _lru_bwd(fres, hseq, h0, dtok, prm, carry_refs, grad_refs):
    cw, cb, wgx, bgx, wga, bga, ap = prm
    xc, sh, (gx, ga, sp, mult), first, a = fres
    anext_ref, dhnext_ref, dxcnext_ref = carry_refs
    dcw_ref, dcb_ref, dwgx_ref, dbgx_ref, dwga_ref, dbga_ref, dap_ref = grad_refs
    n = xc.shape[0]
    an = jnp.where(lax.broadcasted_iota(jnp.int32, a.shape, 0) == n - 1, anext_ref[...], pltpu.roll(a, n - 1, 0))
    dh = _scan_bwd(an, dtok, dhnext_ref[...])
    hprev = jnp.where(lax.broadcasted_iota(jnp.int32, hseq.shape, 0) == 0, h0, pltpu.roll(hseq, 1, 0))
    da = dh * hprev
    anext_ref[...] = _row(a, 0)
    dhnext_ref[...] = _row(dh, 0)
    t = dh * xc
    dxc = dh * mult * gx
    dzx = t * mult * gx * (1.0 - gx)
    dlog_a = da * a - jnp.where(first, 0.0, t * gx * (a * a) / mult)
    dza = dlog_a * (-LRU_C * sp) * ga * (1.0 - ga)
    dap = _rowsum(dlog_a * ga) * (LRU_C * _sigmoid(-ap[...]))
    _acc(dap_ref, dap)
    _acc(dbgx_ref, _rowsum(dzx))
    _acc(dbga_ref, _rowsum(dza))
    parts = []
    for h in range(NH):
        sl = slice(h * HD, (h + 1) * HD)
        parts.append(_nt(dzx[:, sl], wgx[h]) + _nt(dza[:, sl], wga[h]))
        dwgx_ref[h] += _tn(xc[:, sl], dzx[:, sl])
        dwga_ref[h] += _tn(xc[:, sl], dza[:, sl])
    dxc = dxc + jnp.concatenate(parts, axis=1)
    _acc(dcb_ref, _rowsum(dxc))
    for j in range(4):
        dcw_ref[pl.ds(j, 1), :] += _rowsum(dxc * sh[j])
    ext = jnp.concatenate([dxc, dxcnext_ref[...]], axis=0)
    m = n + 8
    dxb = cw[pl.ds(3, 1), :] * dxc
    for j in range(3):
        dxb = dxb + cw[pl.ds(j, 1), :] * pltpu.roll(ext, m - (3 - j), 0)[:n]
    dxcnext_ref[...] = dxc[:8]
    return dxb


def _layer_fwd(kind, layer, xprev, gprev, bprev, wt, wout, ks, vs, prm, ride=None):
    TS = TS_FWD[kind]
    _rows = functools.partial(_row_spec, ts=TS)
    S = xprev.shape[0]
    nt = S // TS
    N = wt.shape[0]
    nprm = len(prm)
    nch = TS // CH

    outs = [(jax.ShapeDtypeStruct((S, N), F32), _rows(N, nt, False)),
            (jax.ShapeDtypeStruct((S, D), F32), _rows(D, nt, False)),
            (jax.ShapeDtypeStruct((S, 1), F32), _rows(1, nt, False)),
            (jax.ShapeDtypeStruct((S, XHEADS * NMEM), MXU), _rows(XHEADS * NMEM, nt, False))]
    scratch = []
    if kind == 1:
        outs.append((jax.ShapeDtypeStruct((nt, NH, nch, HD, HD), F32),
                     pl.BlockSpec((None, NH, nch, HD, HD), lambda i: (i, 0, 0, 0, 0))))
        outs.append((jax.ShapeDtypeStruct((S, TOK), F32), _rows(TOK, nt, False)))
        outs.append((jax.ShapeDtypeStruct((S, TOK), MXU), _rows(TOK, nt, False)))
        scratch = [pltpu.VMEM((NH, HD, HD), F32)]
    elif kind == 2:
        outs += [(jax.ShapeDtypeStruct((S, TOK), F32), _rows(TOK, nt, False))] * 2
        scratch = [pltpu.VMEM((16, TOK), F32)]
    elif kind == 3:
        outs.append((jax.ShapeDtypeStruct((nt * 8, TOK), F32), pl.BlockSpec((8, TOK), lambda i: (i, 0))))
        outs += [(jax.ShapeDtypeStruct((S, TOK), F32), _rows(TOK, nt, False))] * 4
        scratch = [pltpu.VMEM((8, TOK), F32), pltpu.VMEM((1, TOK), F32)]
    nout = len(outs)
    nscr = len(scratch)
    nride = len(ride.arrays) if ride else 0

    def body(*refs):
        x_ref, g_ref, b_ref, wt_ref, wout_ref, ks_ref, vs_ref = refs[:7]
        prm_refs = refs[7:7 + nprm]
        nin = 7 + nprm + nride
        ride_src = refs[7 + nprm:nin]
        out_refs = refs[nin:nin + nout]
        ride_dst = refs[nin + nout:nin + nout + nride]
        scr = refs[nin + nout + nride:nin + nout + nride + nscr]
        ride_sems = refs[nin + nout + nride + nscr:]
        proj_ref, xhat_ref, rstd_ref = out_refs[:3]
        i = pl.program_id(0)
        if ride:
            @pl.when(i == 0)
            def _():
                ride.start(ride_src, ride_dst, ride_sems)

        xin = x_ref[...] * g_ref[...] + b_ref[...]
        proj = _nt(xin, wt_ref[...])
        proj_ref[...] = proj
        if kind == 0:
            tok, _ = _gmlp_fwd(proj[:, :TOK], proj[:, TOK:2 * TOK], prm_refs[0], prm_refs[1])
        elif kind == 1:
            st_ref, = scr

            @pl.when(i == 0)
            def _():
                st_ref[...] = jnp.zeros_like(st_ref)

            lb, _ = _lower_bound(prm_refs[0][...], layer)
            tok, out_refs[5][...], out_refs[6][...] = _hgrn_fwd(proj[:, :TOK], proj[:, TOK:2 * TOK], proj[:, 2 * TOK:3 * TOK],
                                                                lb, prm_refs[1][...], st_ref, out_refs[4])
        elif kind == 2:
            halo_ref, = scr

            @pl.when(i == 0)
            def _():
                halo_ref[...] = jnp.zeros_like(halo_ref)

            p = proj[:, :TOK]
            tok, (diff, y, _) = _pool_fwd(p, halo_ref[...], i * TS, prm_refs[0][...], prm_refs[1][...])
            out_refs[4][...] = diff
            out_refs[5][...] = y
            halo_ref[...] = p[TS - 16:]
        else:
            halo_ref, h_ref = scr

            @pl.when(i == 0)
            def _():
                halo_ref[...] = jnp.zeros_like(halo_ref)
                h_ref[...] = jnp.zeros_like(h_ref)

            out_refs[4][...] = jnp.broadcast_to(h_ref[...], (8, TOK))
            xb = proj[:, :TOK]
            tok, fres = _lru_fwd(xb, halo_ref[...], i * TS, prm_refs, h_ref[...])
            gx, ga, _, mult = fres[2]
            for r, val in zip(out_refs[5:9], (tok, gx, ga, mult)):
                r[...] = val
            halo_ref[...] = xb[TS - 8:]
            h_ref[...] = _row(tok, TS - 1)
        qx = proj[:, N - D - XW:N - D]
        gate = proj[:, N - D:]
        xo, ps = _xattn_fwd(qx, ks_ref, vs_ref)
        out_refs[3][...] = jnp.concatenate(ps, axis=1).astype(MXU)
        mixed = jnp.concatenate([tok, xo], axis=1) * (gate * _sigmoid(gate))
        z = ALPHA * xin + _nn(mixed, wout_ref[...])
        cen = z - jnp.mean(z, axis=-1, keepdims=True)
        rstd = lax.rsqrt(jnp.mean(cen * cen, axis=-1, keepdims=True) + LN_EPS)
        xhat_ref[...] = cen * rstd
        rstd_ref[...] = rstd
        if ride:
            @pl.when(i == nt - 1)
            def _():
                ride.wait(ride_src, ride_dst, ride_sems)

    ins = [(xprev, _rows(D, nt, False)), (gprev, _res(gprev)), (bprev, _res(bprev)), (wt, _res(wt)), (wout, _res(wout)),
           (ks, _res(ks)), (vs, _res(vs))] + [(p, _res(p)) for p in prm]
    if ride:
        ins += [(a, _ANY) for a in ride.arrays]
        outs += [(s, _ANY) for s in ride.out_shapes]
        scratch = scratch + ride.scratch
    return _call(body, f"layer{layer}_fwd", (nt,), ins, outs, scratch)


def _layer_bwd(kind, layer, up, is_last, xhat, rstd, probs, g_i, b_i, proj, wout, ks, vs, prm, extra, ride=None):
    TS = TS_BWD[kind]
    _rows = functools.partial(_row_spec, ts=TS)
    S = xhat.shape[0]
    nt = S // TS
    N = proj.shape[1]
    nprm = len(prm)
    nch = TS // CH

    ins = [(up, _rows(D, nt, True)), (xhat, _rows(D, nt, True)), (rstd, _rows(1, nt, True)), (g_i, _res(g_i)), (b_i, _res(b_i)),
           (proj, _rows(N, nt, True)), (wout, _res(wout)), (ks, _res(ks)), (vs, _res(vs)),
           (probs, _rows(XHEADS * NMEM, nt, True))] + [(p, _res(p)) for p in prm]
    nfixed = 10
    if kind == 1:
        ins.append((extra[0], pl.BlockSpec((None, NH, nch, HD, HD), lambda i: (nt - 1 - i, 0, 0, 0, 0))))
        ins += [(e, _rows(TOK, nt, True)) for e in extra[1:]]
    elif kind == 2:
        ins += [(e, _rows(TOK, nt, True)) for e in extra]
    elif kind == 3:
        hb = TS // 8
        ins.append((proj, pl.BlockSpec((8, TOK), lambda i: (jnp.maximum((nt - 1 - i) * hb - 1, 0), 0))))
        ins.append((extra[0], pl.BlockSpec((8, TOK), lambda i: (nt - 1 - i, 0))))
        ins += [(e, _rows(TOK, nt, True)) for e in extra[1:]]
    nin = len(ins)

    def acc(shape):
        return (jax.ShapeDtypeStruct(shape, F32), _res_sds(shape))

    outs = [(jax.ShapeDtypeStruct((S, D), F32), _rows(D, nt, True)),
            (jax.ShapeDtypeStruct((S, N), MXU), _rows(N, nt, True)),
            (jax.ShapeDtypeStruct((D, D), WIRE), _res_sds((D, D))),
            acc((XHEADS, NMEM, XW)), acc((XHEADS, NMEM, XW)), acc((1, D)), acc((1, D)), acc((1, HD))]
    scratch = []
    if kind == 0:
        outs += [acc((NH, HD, HD)), acc((NH, HD, HD))]
    elif kind == 1:
        outs += [acc((1, TOK)), acc((1, TOK))]
        scratch = [pltpu.VMEM((NH, HD, HD), F32)]
    elif kind == 2:
        outs += [acc((TOK, TOK)), acc((1, TOK))]
        scratch = [pltpu.VMEM((16, TOK), F32)]
    else:
        outs += [acc((4, TOK)), acc((1, TOK)), acc((NH, HD, HD)), acc((1, TOK)), acc((NH, HD, HD)), acc((1, TOK)), acc((1, TOK))]
        scratch = [pltpu.VMEM((1, TOK), F32), pltpu.VMEM((1, TOK), F32), pltpu.VMEM((8, TOK), F32)]
    scratch = scratch + [pltpu.VMEM((D, D), F32)]
    nout = len(outs)
    nscr = len(scratch)
    nride = len(ride.arrays) if ride else 0

    def body(*refs):
        up_ref, xhat_ref, rstd_ref, g_ref, b_ref, proj_ref, wout_ref, ks_ref, vs_ref, probs_ref = refs[:nfixed]
        prm_refs = refs[nfixed:nfixed + nprm]
        ext_refs = refs[nfixed + nprm:nin]
        ride_src = refs[nin:nin + nride]
        o0 = nin + nride
        out_refs = refs[o0:o0 + nout]
        ride_dst = refs[o0 + nout:o0 + nout + nride]
        scr = refs[o0 + nout + nride:o0 + nout + nride + nscr - 1]
        dwout_acc = refs[o0 + nout + nride + nscr - 1]
        ride_sems = refs[o0 + nout + nride + nscr:]
        dres_ref, dproj_ref, dwout_ref, dks_ref, dvs_ref, dg_ref, db_ref, loss_ref = out_refs[:8]
        pgrad = out_refs[8:]
        i = pl.program_id(0)
        tile = nt - 1 - i

        @pl.when(i == 0)
        def _():
            if ride:
                ride.start(ride_src, ride_dst, ride_sems)
            for r in out_refs[3:]:
                r[...] = jnp.zeros_like(r)
            dwout_acc[...] = jnp.zeros_like(dwout_acc)
            for r in scr:
                if kind != 1 or r is scr[0]:
                    r[...] = jnp.zeros_like(r)

        xhat_v = xhat_ref[...]
        if is_last:
            err = xhat_v * g_ref[...] + b_ref[...] - up_ref[...]
            dxo = err * (1.0 / D)
            loss_ref[...] += jnp.sum(0.5 * jnp.mean(err * err, axis=-1, keepdims=True), axis=0, keepdims=True)
        else:
            dxo = up_ref[...]
        _acc(dg_ref, _rowsum(dxo * xhat_v))
        _acc(db_ref, _rowsum(dxo))
        dxh = dxo * g_ref[...]
        dz = rstd_ref[...] * (dxh - jnp.mean(dxh, axis=-1, keepdims=True)
                              - xhat_v * jnp.mean(dxh * xhat_v, axis=-1, keepdims=True))
        dres_ref[...] = ALPHA * dz

        proj = proj_ref[...]
        qx = proj[:, N - D - XW:N - D]
        gate = proj[:, N - D:]
        sgate = _sigmoid(gate)
        silu = gate * sgate
        dmixed = _nt(dz, wout_ref[...])
        dcat = dmixed * silu
        dtok = dcat[:, :TOK]
        if kind == 0:
            u, v = proj[:, :TOK], proj[:, TOK:2 * TOK]
            tok, fres = _gmlp_fwd(u, v, prm_refs[0], prm_refs[1])
            du, dv = _gmlp_bwd(u, v, fres, dtok, pgrad[0], pgrad[1])
            dproj_ref[:, :TOK] = du.astype(MXU)
            dproj_ref[:, TOK:2 * TOK] = dv.astype(MXU)
        elif kind == 1:
            dst_ref, = scr
            lb, _ = _lower_bound(prm_refs[0][...], layer)
            q, fl, inp = proj[:, :TOK], proj[:, TOK:2 * TOK], proj[:, 2 * TOK:3 * TOK]
            pr = _hgrn_prep(q, fl, lb)
            tok, dq, dfl, dinp = _hgrn_bwd(q, pr, inp, lb, prm_refs[1][...], dtok, ext_refs[1][...], ext_refs[2][...],
                                           ext_refs[0], dst_ref, pgrad[1], pgrad[0])
            dproj_ref[:, :TOK] = dq.astype(MXU)
            dproj_ref[:, TOK:2 * TOK] = dfl.astype(MXU)
            dproj_ref[:, 2 * TOK:3 * TOK] = dinp.astype(MXU)
        elif kind == 2:
            diff, y = ext_refs[0][...], ext_refs[1][...]
            tok = y * prm_refs[1][...]
            fres = (diff, y, _pool_cnt(tile * TS, TS))
            dp = _pool_bwd(fres, dtok, scr[0], prm_refs[0][...], prm_refs[1][...], pgrad[0], pgrad[1])
            dproj_ref[:, :TOK] = dp.astype(MXU)
        else:
            xb = proj[:, :TOK]
            halo = jnp.where(tile == 0, 0.0, ext_refs[0][...])
            h0 = ext_refs[1][0:1]
            tok, gx, ga, mult = (r[...] for r in ext_refs[2:6])
            xc, sh = _lru_conv(xb, halo, prm_refs[0], prm_refs[1][...])
            sp = _softplus_neg(prm_refs[6][...])
            first = (tile * TS + lax.broadcasted_iota(jnp.int32, (TS, 1), 0)) == 0
            fres = (xc, sh, (gx, ga, sp, mult), first, jnp.exp(-LRU_C * ga * sp))
            dxb = _lru_bwd(fres, tok, h0, dtok, prm_refs, scr, pgrad)
            dproj_ref[:, :TOK] = dxb.astype(MXU)
        ps = [probs_ref[:, h * NMEM:(h + 1) * NMEM].astype(F32) for h in range(XHEADS)]
        xo = _nn(ps[0], vs_ref[0])
        for h in range(1, XHEADS):
            xo = xo + _nn(ps[h], vs_ref[h])
        dqx = _xattn_bwd(qx, ps, dcat[:, TOK:], ks_ref, vs_ref, dks_ref, dvs_ref)
        cat = jnp.concatenate([tok, xo], axis=1)
        dwout_acc[...] += _tn(cat * silu, dz)
        dgate = dmixed * cat * (sgate * (1.0 + gate * (1.0 - sgate)))
        dproj_ref[:, N - D - XW:N - D] = dqx.astype(MXU)
        dproj_ref[:, N - D:] = dgate.astype(MXU)

        @pl.when(i == nt - 1)
        def _():
            dwout_ref[...] = dwout_acc[...].astype(WIRE)
            if ride:
                ride.wait(ride_src, ride_dst, ride_sems)

    if ride:
        ins += [(a, _ANY) for a in ride.arrays]
        outs += [(s, _ANY) for s in ride.out_shapes]
        scratch = scratch + ride.scratch
    return _call(body, f"layer{layer}_bwd", (nt,), ins, outs, scratch, vmem=VMEM_LIMIT_WIDE if kind == 0 else VMEM_LIMIT)


def _proj_bwd(layer, dproj, dres, xprev, gprev, bprev, wt, ride=None):
    S = xprev.shape[0]
    nt = S // TSB
    N = wt.shape[0]

    nride = len(ride.arrays) if ride else 0

    def body(*refs):
        dproj_ref, dres_ref, x_ref, g_ref, b_ref, wt_ref = refs[:6]
        ride_src = refs[6:6 + nride]
        dx_ref, dwt_ref = refs[6 + nride:8 + nride]
        ride_dst = refs[8 + nride:8 + 2 * nride]
        acc_ref = refs[8 + 2 * nride]
        ride_sems = refs[9 + 2 * nride:]

        @pl.when(pl.program_id(0) == 0)
        def _():
            if ride:
                ride.start(ride_src, ride_dst, ride_sems)
            acc_ref[...] = jnp.zeros_like(acc_ref)

        dp = dproj_ref[...]
        xin = x_ref[...] * g_ref[...] + b_ref[...]
        dx_ref[...] = dres_ref[...] + _nn(dp, wt_ref[...])
        acc_ref[...] += _tn(dp, xin)

        @pl.when(pl.program_id(0) == nt - 1)
        def _():
            dwt_ref[...] = acc_ref[...].astype(WIRE)
            if ride:
                ride.wait(ride_src, ride_dst, ride_sems)

    ins = [(dproj, _row_spec(N, nt, False, TSB)), (dres, _row_spec(D, nt, False, TSB)), (xprev, _row_spec(D, nt, False, TSB)),
           (gprev, _res(gprev)), (bprev, _res(bprev)), (wt, _res(wt))]
    outs = [(jax.ShapeDtypeStruct((S, D), F32), _row_spec(D, nt, False, TSB)),
            (jax.ShapeDtypeStruct((N, D), WIRE), _res_sds((N, D)))]
    scratch = [pltpu.VMEM((N, D), F32)]
    if ride:
        ins += [(a, _ANY) for a in ride.arrays]
        outs += [(s, _ANY) for s in ride.out_shapes]
        scratch = scratch + ride.scratch
    return _call(body, f"layer{layer}_projbwd", (nt,), ins, outs, scratch)


def _head_mask(h):
    col = lax.broadcasted_iota(jnp.int32, (1, XW), 1)
    return (col // 64) == h


def _kv_fwd(mem, wkv):
    def body(mem_ref, w_ref, ks_ref, vs_ref):
        kv = _nn(mem_ref[...], w_ref[...])
        k, v = kv[:, :XW], kv[:, XW:]
        for h in range(XHEADS):
            ks_ref[h] = jnp.where(_head_mask(h), k, 0.0).astype(MXU)
            vs_ref[h] = jnp.where(_head_mask(h), v, 0.0).astype(MXU)

    sds = jax.ShapeDtypeStruct((XHEADS, NMEM, XW), MXU)
    return pl.pallas_call(body, name="kv_fwd", out_shape=(sds, sds), compiler_params=_cparams())(mem, wkv)


def _kv_bwd(mem, dks_l, dvs_l):
    def body(mem_ref, *refs):
        dks_refs, dvs_refs, out_ref = refs[:DEPTH], refs[DEPTH:2 * DEPTH], refs[2 * DEPTH]
        dk = jnp.zeros((NMEM, XW), F32)
        dv = jnp.zeros((NMEM, XW), F32)
        for h in range(XHEADS):
            m = _head_mask(h)
            for l in range(DEPTH):
                dk = dk + jnp.where(m, dks_refs[l][h], 0.0)
                dv = dv + jnp.where(m, dvs_refs[l][h], 0.0)
        out_ref[...] = _tn(mem_ref[...], jnp.concatenate([dk, dv], axis=1)).astype(WIRE)

    return pl.pallas_call(body, name="kv_bwd", out_shape=jax.ShapeDtypeStruct((D, 2 * XW), WIRE),
                          compiler_params=_cparams())(mem, *dks_l, *dvs_l)


def _prep_weights(w_ins, flip, w_out, wkv):
    def body(a_ref, b_ref, c_ref, d_ref, wo_ref, kv_ref, ao, bo, co, do, wo0, wo1, wo2, wo3, kvo):
        for t, (src, dst) in enumerate(((a_ref, ao), (b_ref, bo), (c_ref, co), (d_ref, do))):
            dst[...] = (src[...].T if flip[t] else src[...]).astype(MXU)
        for l, dst in enumerate((wo0, wo1, wo2, wo3)):
            dst[...] = wo_ref[l].astype(MXU)
        kvo[...] = kv_ref[...].astype(MXU)

    outs = [jax.ShapeDtypeStruct(w.shape[::-1] if flip[t] else w.shape, MXU) for t, w in enumerate(w_ins)]
    outs += [jax.ShapeDtypeStruct(w_out.shape[1:], MXU)] * DEPTH + [jax.ShapeDtypeStruct(wkv.shape, MXU)]
    return pl.pallas_call(body, name="prep_weights", out_shape=outs, compiler_params=_cparams())(*w_ins, w_out, wkv)


def _adam_math(w, g, m, v):
    m = B1 * m + (1.0 - B1) * g
    v = B2 * v + (1.0 - B2) * (g * g)
    m_hat = m / (1.0 - B1 ** STEP)
    v_hat = v / (1.0 - B2 ** STEP)
    delta = -LR * (m_hat / (jnp.sqrt(v_hat) + EPS) + WD * w)
    return delta, m, v


def _sum_adam(name, recv, w, m, v, transpose):
    rows, cols = recv.shape[1], recv.shape[2]

    def body(r_ref, w_ref, m_ref, v_ref, g_out, d_out, m_out, v_out, acc_ref):
        s = pl.program_id(0)

        @pl.when(s == 0)
        def _():
            acc_ref[...] = r_ref[...].astype(F32)

        @pl.when(s > 0)
        def _():
            acc_ref[...] += r_ref[...].astype(F32)

        @pl.when(s == NDEV - 1)
        def _():
            g = acc_ref[...].T if transpose else acc_ref[...]
            d, mn, vn = _adam_math(w_ref[...], g, m_ref[...], v_ref[...])
            g_out[...] = g
            d_out[...] = d
            m_out[...] = mn
            v_out[...] = vn

    sds = jax.ShapeDtypeStruct(w.shape, F32)
    ins = [(recv, pl.BlockSpec((None, rows, cols), lambda s: (s, 0, 0))), (w, _res(w)), (m, _res(m)), (v, _res(v))]
    outs = [(sds, _res_sds(w.shape))] * 4
    return _call(body, name, (NDEV,), ins, outs, [pltpu.VMEM((rows, cols), F32)])


def _bias_finalize(dbs_exp):
    def body(dbs_ref, dabs_ref):
        dabs_ref[...] = jnp.sum(dbs_ref[...], axis=-1)

    return pl.pallas_call(body, name="bias_finalize", out_shape=jax.ShapeDtypeStruct((NH, HD), F32),
                          compiler_params=_cparams())(dbs_exp)


def _lb_finalize(dlb, lb_logits):
    def body(dlb_ref, lg_ref, dlg_ref):
        total = jnp.zeros((DEPTH, TOK), F32)
        lg = lg_ref[...]
        e = jnp.exp(lg - jnp.max(lg, axis=0, keepdims=True))
        p = e / jnp.sum(e, axis=0, keepdims=True)
        row = lax.broadcasted_iota(jnp.int32, (DEPTH, TOK), 0)
        for layer in range(DEPTH):
            if layer % 4 != 1:
                continue
            dp = jnp.where((row >= 1) & (row <= layer), dlb_ref[...], 0.0)
            total = total + p * (dp - jnp.sum(p * dp, axis=0, keepdims=True))
        dlg_ref[...] = total

    return pl.pallas_call(body, name="lb_finalize", out_shape=jax.ShapeDtypeStruct((DEPTH, TOK), F32),
                          compiler_params=_cparams())(dlb, lb_logits)


def _small_sum_adam(name, gathered, w, m, v):
    rows = w.shape[0]

    def body(r_ref, w_ref, m_ref, v_ref, g_out, d_out, m_out, v_out):
        g = r_ref[0]
        for s in range(1, NDEV):
            g = g + r_ref[s]
        d, mn, vn = _adam_math(w_ref[...], g, m_ref[...], v_ref[...])
        g_out[...] = g
        d_out[...] = d
        m_out[...] = mn
        v_out[...] = vn

    sds = jax.ShapeDtypeStruct((rows, 128), F32)
    return pl.pallas_call(body, name=name, out_shape=(sds,) * 4, compiler_params=_cparams())(gathered, w, m, v)


def _group_adam(name, recvs, params):
    nk = len(recvs)

    def body(*refs):
        pos, oi = nk, nk + 3 * sum(p is not None for p in params)
        for k in range(nk):
            g = refs[k][0]
            for s in range(1, NDEV):
                g = g + refs[k][s]
            refs[oi][...] = g
            oi += 1
            if params[k] is not None:
                d, mn, vn = _adam_math(refs[pos][...], g, refs[pos + 1][...], refs[pos + 2][...])
                refs[oi][...] = d
                refs[oi + 1][...] = mn
                refs[oi + 2][...] = vn
                pos += 3
                oi += 3

    out_shape, counts = [], []
    for k in range(nk):
        counts.append(4 if params[k] is not None else 1)
        out_shape += [jax.ShapeDtypeStruct(recvs[k].shape[1:], F32)] * counts[-1]
    args = list(recvs) + [a for p in params if p is not None for a in p]
    flat = pl.pallas_call(body, name=name, out_shape=out_shape, compiler_params=_cparams())(*args)
    res, o = [], 0
    for cnt in counts:
        res.append(flat[o:o + cnt])
        o += cnt
    return res


def _adam_only(g, w, m, v):
    def body(g_ref, w_ref, m_ref, v_ref, d_out, m_out, v_out):
        d, mn, vn = _adam_math(w_ref[...], g_ref[...], m_ref[...], v_ref[...])
        d_out[...] = d
        m_out[...] = mn
        v_out[...] = vn

    sds = jax.ShapeDtypeStruct(w.shape, F32)
    return pl.pallas_call(body, name="shard_adam", out_shape=(sds,) * 3, compiler_params=_cparams())(g, w, m, v)


def _me_and_peers():
    x, y, c = lax.axis_index("x"), lax.axis_index("y"), lax.axis_index("c")
    me = 4 * x + 2 * y + c
    peers = []
    for k in range(1, NDEV):
        kx, ky, kc = (k >> 2) & 1, (k >> 1) & 1, k & 1
        px = x + kx - 2 * x * kx
        py = y + ky - 2 * y * ky
        pc = c + kc - 2 * c * kc
        peers.append(((px, py, pc), 4 * px + 2 * py + pc))
    return me, peers


_ANY = pl.BlockSpec(memory_space=pl.ANY)


class _Exchange:
    def __init__(self, arrays, split):
        self.arrays = list(arrays)
        self.split = list(split)
        n = len(self.arrays)
        self.out_shapes = []
        for a, sp in zip(self.arrays, self.split):
            rows = a.shape[0] // NDEV if sp else a.shape[0]
            self.out_shapes.append(jax.ShapeDtypeStruct((NDEV, rows, a.shape[1]), a.dtype))
        self.scratch = [pltpu.SemaphoreType.DMA((n, NDEV - 1)), pltpu.SemaphoreType.DMA((n, NDEV - 1)),
                        pltpu.SemaphoreType.DMA((n,))]

    def _block(self, src, t, d):
        if not self.split[t]:
            return src[t]
        rows = self.arrays[t].shape[0] // NDEV
        return src[t].at[pl.ds(d * rows, rows)]

    def start(self, src, dst, sems):
        send_sems, recv_sems, local_sems = sems
        me, peers = _me_and_peers()
        for t in range(len(self.arrays)):
            pltpu.make_async_copy(self._block(src, t, me), dst[t].at[me], local_sems.at[t]).start()
        for k, (dev, idx) in enumerate(peers):
            for t in range(len(self.arrays)):
                pltpu.make_async_remote_copy(src_ref=self._block(src, t, idx), dst_ref=dst[t].at[me],
                                             send_sem=send_sems.at[t, k], recv_sem=recv_sems.at[t, k],
                                             device_id=dev, device_id_type=pl.DeviceIdType.MESH).start()

    def wait(self, src, dst, sems):
        send_sems, recv_sems, local_sems = sems
        me, peers = _me_and_peers()

        def slot_copy(t, k, dev, idx):
            return pltpu.make_async_remote_copy(src_ref=dst[t].at[idx], dst_ref=dst[t].at[idx], send_sem=send_sems.at[t, k],
                                                recv_sem=recv_sems.at[t, k], device_id=dev,
                                                device_id_type=pl.DeviceIdType.MESH)

        for k, (dev, idx) in enumerate(peers):
            for t in range(len(self.arrays)):
                slot_copy(t, k, dev, idx).wait_recv()
        for k, (dev, idx) in enumerate(peers):
            for t in range(len(self.arrays)):
                slot_copy(t, k, dev, idx).wait_send()
        for t in range(len(self.arrays)):
            pltpu.make_async_copy(dst[t].at[me], dst[t].at[me], local_sems.at[t]).wait()

    def gather_by_chip(self, src, dst, sems):
        assert not any(self.split)
        send_sems, recv_sems, local_sems = sems
        n = len(self.arrays)
        x, y, c = lax.axis_index("x"), lax.axis_index("y"), lax.axis_index("c")
        me, sibling = 4 * x + 2 * y + c, (x, y, 1 - c)
        chips = [(1 - x, y), (x, 1 - y), (1 - x, 1 - y)]

        def index(chip, core):
            return 4 * chip[0] + 2 * chip[1] + core

        def copy(t, k, block, to, from_src):
            return pltpu.make_async_remote_copy(src_ref=src[t] if from_src else dst[t].at[block], dst_ref=dst[t].at[block],
                                                send_sem=send_sems.at[t, k], recv_sem=recv_sems.at[t, k],
                                                device_id=to, device_id_type=pl.DeviceIdType.MESH)

        local = [pltpu.make_async_copy(src[t], dst[t].at[me], local_sems.at[t]) for t in range(n)]
        for cp in local:
            cp.start()
        sends = []
        for t in range(n):
            sends.append(copy(t, 0, me, sibling, True))
            sends += [copy(t, 1 + j, me, (*chip, c), True) for j, chip in enumerate(chips)]
        for cp in sends:
            cp.start()
        for j, chip in enumerate(chips):
            for t in range(n):
                copy(t, 1 + j, index(chip, c), sibling, False).wait_recv()
                passed = copy(t, 4 + j, index(chip, c), sibling, False)
                passed.start()
                sends.append(passed)
        for t in range(n):
            copy(t, 0, index((x, y), 1 - c), sibling, False).wait_recv()
            for j, chip in enumerate(chips):
                copy(t, 4 + j, index(chip, 1 - c), sibling, False).wait_recv()
        for cp in sends:
            cp.wait_send()
        for cp in local:
            cp.wait()

    def run(self, name, by_chip=False):
        n = len(self.arrays)

        def body(*refs):
            src, dst, sems = refs[:n], refs[n:2 * n], refs[2 * n:]
            if by_chip:
                self.gather_by_chip(src, dst, sems)
                return
            self.start(src, dst, sems)
            self.wait(src, dst, sems)

        return pl.pallas_call(
            body, name=name, out_shape=self.out_shapes, in_specs=[_ANY] * n, out_specs=[_ANY] * n,
            scratch_shapes=self.scratch,
        )(*self.arrays)


SMALL = [("ln_g", (DEPTH, D), False), ("ln_b", (DEPTH, D), False), ("hgrn_lb_logits", (DEPTH, TOK), False),
         ("a_w_s", (1, NH, HD, HD), False), ("a_b_s", (1, NH, HD), False), ("b_norm_g", (1, TOK), True),
         ("c_w_pool", (1, 4, POOL_GROUP, POOL_GROUP), False), ("c_scale", (1, TOK), True),
         ("d_conv_w", (1, 4, TOK), True), ("d_conv_b", (1, TOK), True),
         ("d_w_gx", (1, NH, HD, HD), False), ("d_b_gx", (1, NH, HD), False),
         ("d_w_ga", (1, NH, HD, HD), False), ("d_b_ga", (1, NH, HD), False), ("d_a_param", (1, TOK), True)]


def _pack(parts, total_rows):
    flat = jnp.concatenate([p.reshape(-1).astype(F32) for p in parts])
    flat = jnp.pad(flat, (0, total_rows * 128 - flat.shape[0]))
    return flat.reshape(total_rows, 128)


def _size(shape):
    n = 1
    for s in shape:
        n *= s
    return n


def _rows_for(n):
    return -(-n // 1024) * 8


def kernel(x, mem, mem_kv_w, ln_g, ln_b, w_out, hgrn_lb_logits, a_w_in, a_w_s, a_b_s, b_w_in, b_norm_g, c_w_in, c_w_pool, c_scale, d_w_in, d_conv_w, d_conv_b, d_w_gx, d_b_gx, d_w_ga, d_b_ga, d_a_param, loss_target, m_mem_kv_w, m_ln_g, m_ln_b, m_w_out, m_hgrn_lb_logits, m_a_w_in, m_a_w_s, m_a_b_s, m_b_w_in, m_b_norm_g, m_c_w_in, m_c_w_pool, m_c_scale, m_d_w_in, m_d_conv_w, m_d_conv_b, m_d_w_gx, m_d_b_gx, m_d_w_ga, m_d_b_ga, m_d_a_param, v_mem_kv_w, v_ln_g, v_ln_b, v_w_out, v_hgrn_lb_logits, v_a_w_in, v_a_w_s, v_a_b_s, v_b_w_in, v_b_norm_g, v_c_w_in, v_c_w_pool, v_c_scale, v_d_w_in, v_d_conv_w, v_d_conv_b, v_d_w_gx, v_d_b_gx, v_d_w_ga, v_d_b_ga, v_d_a_param):
    W = dict(mem_kv_w=mem_kv_w, ln_g=ln_g, ln_b=ln_b, w_out=w_out, hgrn_lb_logits=hgrn_lb_logits, a_w_in=a_w_in, a_w_s=a_w_s,
             a_b_s=a_b_s, b_w_in=b_w_in, b_norm_g=b_norm_g, c_w_in=c_w_in, c_w_pool=c_w_pool, c_scale=c_scale, d_w_in=d_w_in,
             d_conv_w=d_conv_w, d_conv_b=d_conv_b, d_w_gx=d_w_gx, d_b_gx=d_b_gx, d_w_ga=d_w_ga, d_b_ga=d_b_ga, d_a_param=d_a_param)
    M = dict(mem_kv_w=m_mem_kv_w, ln_g=m_ln_g, ln_b=m_ln_b, w_out=m_w_out, hgrn_lb_logits=m_hgrn_lb_logits, a_w_in=m_a_w_in,
             a_w_s=m_a_w_s, a_b_s=m_a_b_s, b_w_in=m_b_w_in, b_norm_g=m_b_norm_g, c_w_in=m_c_w_in, c_w_pool=m_c_w_pool,
             c_scale=m_c_scale, d_w_in=m_d_w_in, d_conv_w=m_d_conv_w, d_conv_b=m_d_conv_b, d_w_gx=m_d_w_gx, d_b_gx=m_d_b_gx,
             d_w_ga=m_d_w_ga, d_b_ga=m_d_b_ga, d_a_param=m_d_a_param)
    V = dict(mem_kv_w=v_mem_kv_w, ln_g=v_ln_g, ln_b=v_ln_b, w_out=v_w_out, hgrn_lb_logits=v_hgrn_lb_logits, a_w_in=v_a_w_in,
             a_w_s=v_a_w_s, a_b_s=v_a_b_s, b_w_in=v_b_w_in, b_norm_g=v_b_norm_g, c_w_in=v_c_w_in, c_w_pool=v_c_w_pool,
             c_scale=v_c_scale, d_w_in=v_d_w_in, d_conv_w=v_d_conv_w, d_conv_b=v_d_conv_b, d_w_gx=v_d_w_gx, d_b_gx=v_d_b_gx,
             d_w_ga=v_d_w_ga, d_b_ga=v_d_b_ga, d_a_param=v_d_a_param)
    me = 4 * lax.axis_index("x") + 2 * lax.axis_index("y") + lax.axis_index("c")
    x2, mem2, tgt2 = x[0], mem[0], loss_target[0]
    in_names = ["a_w_in", "b_w_in", "c_w_in", "d_w_in"]

    shard_names = [n for n, _, sh in SMALL if sh]
    small_shard = _pack([W[n] for n in shard_names], 8)
    flip = [W[n].shape[2] % 128 == 0 for n in in_names]

    def shard2d(tree, t):
        a = tree[in_names[t]][0]
        return a if flip[t] else jnp.swapaxes(a, 0, 1)

    wts = _prep_weights([shard2d(W, t) for t in range(DEPTH)], flip, w_out, mem_kv_w)
    wt_sh, wo_sh, wkv_sh = wts[:4], wts[4:8], wts[8]
    g0 = _Exchange([wt_sh[0], wo_sh[0], wkv_sh, small_shard], [False] * 4).run("gather_first", by_chip=True)
    wt_full = [g0[0].reshape(-1, D)]
    wout_full = [g0[1].reshape(D, D)]
    wkv_full = g0[2].reshape(D, 2 * XW)
    sm = g0[3].reshape(NDEV, 1024)
    full_small = {}
    off = 0
    for n, shape, _ in [s for s in SMALL if s[2]]:
        per = _size(shape) // NDEV
        blk = sm[:, off:off + per]
        if n == "d_conv_w":
            full_small[n] = blk.reshape(NDEV, 4, TOK // NDEV).transpose(1, 0, 2).reshape(4, TOK)
        else:
            full_small[n] = blk.reshape(1, TOK)
        off += per

    ks, vs = _kv_fwd(mem2, wkv_full)
    tri_bs = jnp.broadcast_to(a_b_s[0][:, :, None], (NH, HD, HD))
    wbd = jnp.zeros((TOK, TOK), F32)
    for g in range(4):
        wbd = lax.dynamic_update_slice(wbd, c_w_pool[0, g], (g * POOL_GROUP, g * POOL_GROUP))
    wbd = wbd.astype(MXU)
    prm = {0: [a_w_s[0], tri_bs],
           1: [hgrn_lb_logits, full_small["b_norm_g"]],
           2: [wbd, full_small["c_scale"]],
           3: [full_small["d_conv_w"], full_small["d_conv_b"], d_w_gx[0].astype(MXU), d_b_gx[0].reshape(1, TOK),
               d_w_ga[0].astype(MXU), d_b_ga[0].reshape(1, TOK), full_small["d_a_param"]]}
    ones = jnp.ones((1, D), F32)
    zeros = jnp.zeros((1, D), F32)
    xs, gs, bs = [x2], [ones], [zeros]
    saved = []
    for i in range(DEPTH):
        ride = _Exchange([wt_sh[i + 1], wo_sh[i + 1]], [False, False]) if i + 1 < DEPTH else None
        res = _layer_fwd(i, i, xs[i], gs[i], bs[i], wt_full[i], wout_full[i], ks, vs, prm[i], ride)
        if ride:
            wt_full.append(res[-2].reshape(-1, D))
            wout_full.append(res[-1].reshape(D, D))
            res = res[:-2]
        saved.append(res)
        xs.append(res[1])
        gs.append(ln_g[i:i + 1])
        bs.append(ln_b[i:i + 1])

    up = tgt2
    grads = {}
    dks_l, dvs_l, dwt_l, dwout_l, dlng_l, dlnb_l = [], [], [], [], [], []
    recv_wt, recv_wo = [None] * DEPTH, [None] * DEPTH
    loss_part = None
    sharded = {n for n, _, sh in SMALL if sh}
    group = {3: ["ln_g#3", "ln_b#3", "d_conv_w", "d_conv_b", "d_w_gx", "d_b_gx", "d_w_ga", "d_b_ga", "d_a_param"],
             2: ["ln_g#2", "ln_b#2", "c_w_pool", "c_scale"],
             1: ["ln_g#1", "ln_b#1", "hgrn_lb_logits", "b_norm_g"],
             0: ["ln_g#0", "ln_b#0", "a_w_s", "a_b_s", "loss"]}

    def small_of(l):
        return [grads[e].reshape(-1, grads[e].shape[-1]) for e in group[l]]

    recv_small = [None] * DEPTH
    for i in reversed(range(DEPTH)):
        res = saved[i]
        extra = None if len(res) <= 4 else (res[4] if len(res) == 5 else res[4:])
        ride = None
        if i + 1 < DEPTH:
            smalls = small_of(i + 1)
            ride = _Exchange([dwt_l[-1], dwout_l[-1]] + smalls, [True, True] + [False] * len(smalls))
        out = _layer_bwd(i, i, up, i == DEPTH - 1, res[1], res[2], res[3], gs[i + 1], bs[i + 1], res[0], wout_full[i], ks, vs,
                         prm[i], extra, ride)
        if ride:
            nr = len(ride.arrays)
            recv_wt[i + 1], recv_wo[i + 1], recv_small[i + 1] = out[-nr], out[-nr + 1], out[-nr + 2:]
            out = out[:-nr]
        dres, dproj, dwout_i, dks_i, dvs_i, dg_i, db_i, loss_i = out[:8]
        pg = out[8:]
        if i == DEPTH - 1:
            grads["loss"] = loss_i
        dks_l.append(dks_i)
        dvs_l.append(dvs_i)
        dwout_l.append(dwout_i)
        grads[f"ln_g#{i}"], grads[f"ln_b#{i}"] = dg_i, db_i
        if i == 0:
            grads["a_w_s"], dbs_exp = pg
            grads["a_b_s"] = _bias_finalize(dbs_exp)
        elif i == 1:
            dlb, grads["b_norm_g"] = pg
            grads["hgrn_lb_logits"] = _lb_finalize(dlb, hgrn_lb_logits)
        elif i == 2:
            dwbd, grads["c_scale"] = pg
            grads["c_w_pool"] = jnp.stack([lax.dynamic_slice(dwbd, (g * POOL_GROUP, g * POOL_GROUP), (POOL_GROUP, POOL_GROUP))
                                           for g in range(4)])
        else:
            (grads["d_conv_w"], grads["d_conv_b"], grads["d_w_gx"], grads["d_b_gx"], grads["d_w_ga"], grads["d_b_ga"],
             grads["d_a_param"]) = pg
        ride = None
        if i == 0:
            dwkv = _kv_bwd(mem2, dks_l, dvs_l)
            smalls = small_of(0)
            ride = _Exchange([dwout_i, dwkv] + smalls, [True, True] + [False] * len(smalls))
        pb = _proj_bwd(i, dproj, dres, xs[i], gs[i], bs[i], wt_full[i], ride)
        up, dwt = pb[:2]
        if ride:
            recv_wo[0], recv_kv, recv_small[0] = pb[2], pb[3], pb[4:]
        dwt_l.append(dwt)
    grad_x = up[None]

    recv_wt[0], = _Exchange([dwt_l[-1]], [True]).run("scatter_last")

    outs = {}
    for t, n in enumerate(in_names):
        res = _sum_adam(f"adam_{n}", recv_wt[t], shard2d(W, t), shard2d(M, t), shard2d(V, t), flip[t])
        outs[n] = tuple((o if flip[t] else jnp.swapaxes(o, 0, 1))[None] for o in res)
    wo_res = [_sum_adam(f"adam_w_out{l}", recv_wo[l], w_out[l], m_w_out[l], v_w_out[l], False) for l in range(DEPTH)]
    outs["w_out"] = tuple(jnp.stack([wo_res[l][j] for l in range(DEPTH)]) for j in range(4))
    outs["mem_kv_w"] = _sum_adam("adam_mem_kv_w", recv_kv, mem_kv_w, m_mem_kv_w, v_mem_kv_w, False)

    def entry_of(tree, e, like):
        if "#" in e:
            n, l = e.split("#")
            return tree[n][int(l):int(l) + 1]
        return tree[e].reshape(like.shape[1:])

    small = [{}, {}, {}, {}]
    for l in range(DEPTH):
        params = [None if (e == "loss" or e in sharded) else tuple(entry_of(t, e, r) for t in (W, M, V))
                  for e, r in zip(group[l], recv_small[l])]
        for e, res in zip(group[l], _group_adam(f"small_adam{l}", recv_small[l], params)):
            for j, a in enumerate(res):
                small[j][e] = a
    loss = small[0]["loss"][0, 0]
    for j in range(4):
        for n in ("ln_g", "ln_b"):
            small[j][n] = jnp.concatenate([small[j][f"{n}#{l}"] for l in range(DEPTH)], axis=0)
    g_small = small[0]
    for n, _, sh in SMALL:
        if not sh:
            outs[n] = tuple(small[j][n] for j in range(4))
    per = TOK // NDEV
    g_sh = {n: lax.dynamic_slice_in_dim(g_small[n], me * per, per, axis=1) for n, s, sh in SMALL if sh}
    gp = _pack([g_sh[n] for n in shard_names], 8)
    d_p, m_p, v_p = _adam_only(gp, small_shard, _pack([M[n] for n in shard_names], 8), _pack([V[n] for n in shard_names], 8))
    o = 0
    for n in shard_names:
        cnt = _size(W[n].shape)
        outs[n] = (g_sh[n],) + tuple(t.reshape(-1)[o:o + cnt].reshape(W[n].shape) for t in (d_p, m_p, v_p))
        o += cnt

    order = ["mem_kv_w", "ln_g", "ln_b", "w_out", "hgrn_lb_logits", "a_w_in", "a_w_s", "a_b_s", "b_w_in", "b_norm_g", "c_w_in",
             "c_w_pool", "c_scale", "d_w_in", "d_conv_w", "d_conv_b", "d_w_gx", "d_b_gx", "d_w_ga", "d_b_ga", "d_a_param"]
    result = [loss, grad_x]
    for j in range(4):
        result += [outs[n][j].reshape(W[n].shape) for n in order]
    return tuple(result)
```

```python
import functools

import jax
import jax.numpy as jnp
from jax import lax
from jax.experimental import pallas as pl
from jax.experimental.pallas import tpu as pltpu

F32 = jnp.float32
MXU = jnp.bfloat16
WIRE = jnp.bfloat16

D = 1024
TOK = 768
XW = 256
NMEM = 256
XHEADS = 4
XSCALE = 64 ** -0.5
NH = 6
HD = 128
CH = 16
POOL_WINDOWS = (2, 4, 8, 16)
POOL_GROUP = 192
DEPTH = 4
ALPHA = (2 * DEPTH) ** 0.25
LN_EPS = 1e-5
RMS_EPS = 1e-6
LRU_C = 8.0
B1, B2, LR, EPS, WD, STEP = 0.9, 0.999, 0.001, 1e-8, 0.01, 10

NDEV = 8
TS_FWD = {0: 512, 1: 256, 2: 512, 3: 256}
TS_BWD = {0: 512, 1: 256, 2: 512, 3: 256}
TSB = 512
VMEM_LIMIT = 58 * 1024 * 1024
VMEM_LIMIT_WIDE = 62 * 1024 * 1024

KIND_WIDTHS = {0: 2 * TOK + XW + D, 1: 3 * TOK + XW + D, 2: TOK + XW + D, 3: TOK + XW + D}


def _mm(a, b, ca, cb):
    return lax.dot_general(a.astype(MXU), b.astype(MXU), (((ca,), (cb,)), ((), ())), preferred_element_type=F32)


def _nn(a, b):
    return _mm(a, b, 1, 0)


def _nt(a, b):
    return _mm(a, b, 1, 1)


def _tn(a, b):
    return _mm(a, b, 0, 0)


def _bmm(a, b, ca, cb):
    return lax.dot_general(a.astype(MXU), b.astype(MXU), (((ca,), (cb,)), ((0,), (0,))), preferred_element_type=F32)


def _sigmoid(x):
    return 1.0 / (1.0 + jnp.exp(-x))


def _vjp1(fn, x, dy):
    return jax.vjp(fn, x)[1](dy)[0]


def _rowsum(x):
    return jnp.sum(x, axis=0, keepdims=True)


def _row(x, r):
    sel = lax.broadcasted_iota(jnp.int32, x.shape, 0) == r
    return jnp.sum(jnp.where(sel, x, 0.0), axis=0, keepdims=True)


def _acc(ref, val):
    ref[...] += val


def _cparams(sem=None, vmem=VMEM_LIMIT):
    return pltpu.CompilerParams(dimension_semantics=sem, vmem_limit_bytes=vmem)


def _res(a):
    nd = a.ndim
    return pl.BlockSpec(a.shape, lambda i: (0,) * nd)


def _res_sds(shape):
    nd = len(shape)
    return pl.BlockSpec(shape, lambda i: (0,) * nd)


def _row_spec(width, nt, rev, ts):
    if rev:
        return pl.BlockSpec((ts, width), lambda i: (nt - 1 - i, 0))
    return pl.BlockSpec((ts, width), lambda i: (i, 0))


def _call(body, name, grid, ins, outs, scratch=(), sem=("arbitrary",), vmem=VMEM_LIMIT):
    arrays = [a for a, _ in ins]
    return pl.pallas_call(
        body, name=name, grid=grid,
        in_specs=[s for _, s in ins],
        out_specs=[s for _, s in outs],
        out_shape=[o for o, _ in outs],
        scratch_shapes=list(scratch),
        compiler_params=_cparams(sem, vmem),
    )(*arrays)


def _xattn_fwd(qx, ks_ref, vs_ref):
    o = None
    ps = []
    for h in range(XHEADS):
        s = _nt(qx, ks_ref[h]) * XSCALE
        s = s - jnp.max(s, axis=-1, keepdims=True)
        e = jnp.exp(s)
        p = e * (1.0 / jnp.sum(e, axis=-1, keepdims=True))
        ps.append(p)
        oh = _nn(p, vs_ref[h])
        o = oh if o is None else o + oh
    return o, ps


def _xattn_bwd(qx, ps, dxo, ks_ref, vs_ref, dks_ref, dvs_ref):
    dq = None
    for h in range(XHEADS):
        p = ps[h]
        dp = _nt(dxo, vs_ref[h])
        ds = p * (dp - jnp.sum(dp * p, axis=-1, keepdims=True))
        dqh = _nn(ds, ks_ref[h]) * XSCALE
        dq = dqh if dq is None else dq + dqh
        dks_ref[h] += _tn(ds, qx) * XSCALE
        dvs_ref[h] += _tn(p, dxo)
    return dq


def _tril128():
    r = lax.broadcasted_iota(jnp.int32, (HD, HD), 0)
    c = lax.broadcasted_iota(jnp.int32, (HD, HD), 1)
    return c <= r


GELU_C = 0.7978845608028654
GELU_K = 0.044715


def _gelu(x):
    th = jnp.tanh(GELU_C * (x + GELU_K * (x * x * x)))
    return 0.5 * x * (1.0 + th), th


def _gelu_grad(x, th):
    return 0.5 * (1.0 + th) + 0.5 * x * (1.0 - th * th) * (GELU_C * (1.0 + 3.0 * GELU_K * (x * x)))


def _gmlp_fwd(u, v, ws_ref, bs_ref):
    ts = u.shape[0]
    ug, thu = _gelu(u)
    vg, thv = _gelu(v)
    tri = _tril128()
    toks, res = [], []
    for g in range(NH):
        sl = slice(g * HD, (g + 1) * HD)
        vgh = vg[:, sl]
        cen = vgh - jnp.mean(vgh, axis=-1, keepdims=True)
        rstd = lax.rsqrt(jnp.mean(cen * cen, axis=-1, keepdims=True) + LN_EPS)
        vn = cen * rstd
        w = jnp.where(tri, ws_ref[g], 0.0).astype(MXU)
        mix = jnp.concatenate([_nn(w, vn[n * HD:(n + 1) * HD]) + bs_ref[g] for n in range(ts // HD)], axis=0)
        toks.append(ug[:, sl] * mix)
        res.append((vn, rstd, mix, w))
    return jnp.concatenate(toks, axis=1), (ug, res, thu, thv)


def _gmlp_bwd(u, v, fres, dtok, dws_ref, dbs_ref):
    ts = u.shape[0]
    ug, res, thu, thv = fres
    tri = _tril128()
    dugs, dvgs = [], []
    for g in range(NH):
        sl = slice(g * HD, (g + 1) * HD)
        vn, rstd, mix, w = res[g]
        dmix = dtok[:, sl] * ug[:, sl]
        dugs.append(dtok[:, sl] * mix)
        dvn_rows = []
        dw = None
        dbs = None
        for n in range(ts // HD):
            dm = dmix[n * HD:(n + 1) * HD]
            dvn_rows.append(_tn(w, dm))
            t = _nt(dm, vn[n * HD:(n + 1) * HD])
            dw = t if dw is None else dw + t
            dbs = dm if dbs is None else dbs + dm
        dws_ref[g] += jnp.where(tri, dw, 0.0)
        dbs_ref[g] += dbs
        dvn = jnp.concatenate(dvn_rows, axis=0)
        dvgs.append(rstd * (dvn - jnp.mean(dvn, axis=-1, keepdims=True) - vn * jnp.mean(dvn * vn, axis=-1, keepdims=True)))
    du = jnp.concatenate(dugs, axis=1) * _gelu_grad(u, thu)
    dv = jnp.concatenate(dvgs, axis=1) * _gelu_grad(v, thv)
    return du, dv


def _chunk_cumsum(x):
    row = lax.broadcasted_iota(jnp.int32, x.shape, 0) % CH
    for s in (1, 2, 4, 8):
        x = x + jnp.where(row >= s, pltpu.roll(x, s, 0), 0.0)
    return x


def _chunk_revcumsum(x):
    n = x.shape[0]
    row = lax.broadcasted_iota(jnp.int32, x.shape, 0) % CH
    for s in (1, 2, 4, 8):
        x = x + jnp.where(row < CH - s, pltpu.roll(x, n - s, 0), 0.0)
    return x


def _chunk_sum(x):
    n, w = x.shape
    return jnp.sum(x.reshape(n // CH, CH, w), axis=1)


def _chunk_bcast(c, n):
    nch, w = c.shape
    return jnp.broadcast_to(c[:, None, :], (nch, CH, w)).reshape(n, w)


def _lower_bound(lb_logits, layer):
    lg = lb_logits
    e = jnp.exp(lg - jnp.max(lg, axis=0, keepdims=True))
    p = e / jnp.sum(e, axis=0, keepdims=True)
    row = lax.broadcasted_iota(jnp.int32, p.shape, 0)
    lb = jnp.sum(jnp.where((row >= 1) & (row <= layer), p, 0.0), axis=0, keepdims=True)
    return lb, p


def _hgrn_prep(q, fl, lb):
    n = q.shape[0]
    sg = _sigmoid(fl)
    f = lb + (1.0 - lb) * sg
    lf = jnp.log(f)
    sq = _sigmoid(q)
    g = _chunk_cumsum(lf)
    tot = _chunk_sum(lf)
    gl = _chunk_bcast(tot, n)
    eg = jnp.exp(g)
    eng = jnp.exp(-g)
    egl = jnp.exp(gl - g)
    k = 1.0 - f
    qf = q * sq
    return dict(sg=sg, f=f, k=k, sq=sq, qf=qf, eg=eg, eng=eng, egl=egl,
                qd=qf * eg, ki=k * eng, ke=k * egl, dch=jnp.exp(tot))


def _hgrn_mask():
    r = lax.broadcasted_iota(jnp.int32, (HD, HD), 0)
    c = lax.broadcasted_iota(jnp.int32, (HD, HD), 1)
    return (r // CH == c // CH) & (c <= r)


def _hgrn_states(v3, ke3, dch_h, st_in):
    nch = v3.shape[0]
    ut = _bmm(v3, ke3, 1, 1)
    dfull = jnp.broadcast_to(dch_h[:, None, :], (nch, HD, HD))
    st, sts = st_in, []
    for c in range(nch):
        sts.append(st)
        st = st * dfull[c] + ut[c]
    return jnp.stack(sts), st, dfull


def _hgrn_fwd(q, fl, inp, lb, ng, st_ref, sts_ref):
    n = q.shape[0]
    nch = n // CH
    pr = _hgrn_prep(q, fl, lb)
    mask = _hgrn_mask()
    toks, o_l, a_l = [], [], []
    qd_m, ki_m, ke_m, v_m = (t.astype(MXU) for t in (pr["qd"], pr["ki"], pr["ke"], inp))
    for h in range(NH):
        sl = slice(h * HD, (h + 1) * HD)
        qd, ki, ke, v = qd_m[:, sl], ki_m[:, sl], ke_m[:, sl], v_m[:, sl]
        qd3 = qd.reshape(nch, CH, HD)
        v3 = v.reshape(nch, CH, HD)
        ke3 = ke.reshape(nch, CH, HD)
        sts, st_ref[h], _ = _hgrn_states(v3, ke3, pr["dch"][:, sl], st_ref[h])
        sts_ref[h] = sts
        o = _bmm(qd3, sts, 2, 2).reshape(n, HD)
        intra, scores = [], []
        for b in range(n // HD):
            bs = slice(b * HD, (b + 1) * HD)
            a = jnp.where(mask, _nt(qd[bs], ki[bs]), 0.0).astype(MXU)
            scores.append(a)
            intra.append(_nn(a, v[bs]))
        o = o + jnp.concatenate(intra, axis=0)
        r = lax.rsqrt(jnp.mean(o * o, axis=-1, keepdims=True) + RMS_EPS)
        toks.append(o * r * ng[:, sl])
        o_l.append(o)
        a_l.append(jnp.concatenate(scores, axis=0))
    return jnp.concatenate(toks, axis=1), jnp.concatenate(o_l, axis=1), jnp.concatenate(a_l, axis=1)


def _hgrn_bwd(q, pr, inp, lb, ng, dtok, o_all, a_all, ststart_ref, dst_ref, dng_ref, dlb_ref):
    n = q.shape[0]
    nch = n // CH
    mask = _hgrn_mask()
    dqd_l, dki_l, dke_l, dv_l, ddch_l, dng_l, toks = [], [], [], [], [], [], []
    qd_m, ki_m, ke_m, v_m = (t.astype(MXU) for t in (pr["qd"], pr["ki"], pr["ke"], inp))
    for h in range(NH):
        sl = slice(h * HD, (h + 1) * HD)
        qd, ki, ke, v = qd_m[:, sl], ki_m[:, sl], ke_m[:, sl], v_m[:, sl]
        qd3 = qd.reshape(nch, CH, HD)
        v3 = v.reshape(nch, CH, HD)
        ke3 = ke.reshape(nch, CH, HD)
        sts = ststart_ref[h]
        dfull = jnp.broadcast_to(pr["dch"][:, sl][:, None, :], (nch, HD, HD))
        sts_m = sts.astype(MXU)
        o = o_all[:, sl]
        a_l = [a_all[b * HD:(b + 1) * HD, sl] for b in range(n // HD)]
        r = lax.rsqrt(jnp.mean(o * o, axis=-1, keepdims=True) + RMS_EPS)
        toks.append(o * r * ng[:, sl])
        dt = dtok[:, sl]
        dng_l.append(_rowsum(dt * o * r))
        dn = dt * ng[:, sl]
        do = r * dn - o * (r * r * r) * jnp.mean(dn * o, axis=-1, keepdims=True)
        do_m = do.astype(MXU)
        do3 = do_m.reshape(nch, CH, HD)
        dqd_rows, dki_rows, dv_rows = [], [], []
        for b in range(n // HD):
            bs = slice(b * HD, (b + 1) * HD)
            da = jnp.where(mask, _nt(do_m[bs], v[bs]), 0.0).astype(MXU)
            dqd_rows.append(_nn(da, ki[bs]))
            dki_rows.append(_tn(da, qd[bs]))
            dv_rows.append(_tn(a_l[b], do_m[bs]))
        dqd = jnp.concatenate(dqd_rows, axis=0) + _bmm(do3, sts_m, 2, 1).reshape(n, HD)
        dki = jnp.concatenate(dki_rows, axis=0)
        dv = jnp.concatenate(dv_rows, axis=0)
        wt = _bmm(do3, qd3, 1, 1)
        dst, dstn_l = dst_ref[h], [None] * nch
        for c in reversed(range(nch)):
            dstn_l[c] = dst
            dst = wt[c] + dst * dfull[c]
        dst_ref[h] = dst
        dstn = jnp.stack(dstn_l)
        dstn_m = dstn.astype(MXU)
        dv = dv + _bmm(ke3, dstn_m, 2, 2).reshape(n, HD)
        dke = _bmm(v3, dstn_m, 2, 1).reshape(n, HD)
        ddch_l.append(jnp.sum(sts * dstn, axis=1))
        dqd_l.append(dqd)
        dki_l.append(dki)
        dke_l.append(dke)
        dv_l.append(dv)
    dqd = jnp.concatenate(dqd_l, axis=1)
    dki = jnp.concatenate(dki_l, axis=1)
    dke = jnp.concatenate(dke_l, axis=1)
    dinp = jnp.concatenate(dv_l, axis=1)
    ddch = jnp.concatenate(ddch_l, axis=1)
    _acc(dng_ref, jnp.concatenate(dng_l, axis=1))
    dqf = dqd * pr["eg"]
    dke_ke = dke * pr["ke"]
    dg = dqd * pr["qd"] - dki * pr["ki"] - dke_ke
    dk = dki * pr["eng"] + dke * pr["egl"]
    dgl = _chunk_sum(dke_ke) + ddch * pr["dch"]
    dlf = _chunk_revcumsum(dg) + _chunk_bcast(dgl, n)
    df = dlf / pr["f"] - dk
    sg = pr["sg"]
    dfl = df * (1.0 - lb) * sg * (1.0 - sg)
    _acc(dlb_ref, _rowsum(df * (1.0 - sg)))
    sq = pr["sq"]
    dq = dqf * (sq * (1.0 + q * (1.0 - sq)))
    return jnp.concatenate(toks, axis=1), dq, dfl, dinp


def _pool_select(s2, s4, s8, s16):
    col = lax.broadcasted_iota(jnp.int32, (1, TOK), 1)
    return jnp.where(col < POOL_GROUP, s2, jnp.where(col < 2 * POOL_GROUP, s4, jnp.where(col < 3 * POOL_GROUP, s8, s16)))


def _pool_cnt(pos0, n):
    pos = pos0 + lax.broadcasted_iota(jnp.int32, (n, TOK), 0) + 1
    col = lax.broadcasted_iota(jnp.int32, (n, TOK), 1)
    w = jnp.where(col < POOL_GROUP, 2, jnp.where(col < 2 * POOL_GROUP, 4, jnp.where(col < 3 * POOL_GROUP, 8, 16)))
    return jnp.minimum(pos, w).astype(F32)


def _pool_fwd(p, halo, pos0, wbd, scale):
    n = p.shape[0]
    ext = jnp.concatenate([halo, p], axis=0)
    s2 = ext + pltpu.roll(ext, 1, 0)
    s4 = s2 + pltpu.roll(s2, 2, 0)
    s8 = s4 + pltpu.roll(s4, 4, 0)
    s16 = s8 + pltpu.roll(s8, 8, 0)
    win = _pool_select(s2, s4, s8, s16)[16:]
    cnt = _pool_cnt(pos0, n)
    diff = win / cnt - p
    y = _nn(diff, wbd)
    return y * scale, (diff, y, cnt)


def _pool_bwd(fres, dtok, nxt_ref, wbd, scale, dwbd_ref, dscale_ref):
    diff, y, cnt = fres
    n = diff.shape[0]
    _acc(dscale_ref, _rowsum(dtok * y))
    dy = dtok * scale
    ddiff = _nt(dy, wbd)
    dwbd_ref[...] += _tn(diff, dy)
    qv = ddiff / cnt
    ext = jnp.concatenate([qv, nxt_ref[...]], axis=0)
    m = n + 16
    s2 = ext + pltpu.roll(ext, m - 1, 0)
    s4 = s2 + pltpu.roll(s2, m - 2, 0)
    s8 = s4 + pltpu.roll(s4, m - 4, 0)
    s16 = s8 + pltpu.roll(s8, m - 8, 0)
    adj = _pool_select(s2, s4, s8, s16)[:n]
    nxt_ref[...] = qv[:16]
    return adj - ddiff


def _neg_expm1(x):
    return jnp.where(jnp.abs(x) < 1e-2, -x * (1.0 + x * (0.5 + x * (1.0 / 6.0))), 1.0 - jnp.exp(x))


def _softplus_neg(ap):
    return jnp.maximum(-ap, 0.0) + jnp.log(1.0 + jnp.exp(-jnp.abs(ap)))


def _lru_gates(xc, zx, za, ap, first):
    gx = _sigmoid(zx)
    ga = _sigmoid(za)
    sp = _softplus_neg(ap)
    log_a = -LRU_C * ga * sp
    a = jnp.exp(log_a)
    mult = jnp.sqrt(_neg_expm1(2.0 * log_a))
    mult = jnp.where(first, 1.0, mult)
    return a, mult * gx * xc, (gx, ga, sp, mult)


def _scan_fwd(a, b, h0):
    n = a.shape[0]
    row = lax.broadcasted_iota(jnp.int32, a.shape, 0)
    s = 1
    while s < n:
        keep = row >= s
        b = b + a * jnp.where(keep, pltpu.roll(b, s, 0), 0.0)
        a = a * jnp.where(keep, pltpu.roll(a, s, 0), 1.0)
        s *= 2
    return b + a * h0


def _scan_bwd(an, d, dh_next):
    n = an.shape[0]
    row = lax.broadcasted_iota(jnp.int32, an.shape, 0)
    s = 1
    while s < n:
        keep = row < n - s
        d = d + an * jnp.where(keep, pltpu.roll(d, n - s, 0), 0.0)
        an = an * jnp.where(keep, pltpu.roll(an, n - s, 0), 1.0)
        s *= 2
    return d + an * dh_next


def _lru_conv(xb, halo, cw_ref, cb):
    ext = jnp.concatenate([halo, xb], axis=0)
    sh = [pltpu.roll(ext, 3 - j, 0)[8:] if j < 3 else xb for j in range(4)]
    xc = cb
    for j in range(4):
        xc = xc + cw_ref[pl.ds(j, 1), :] * sh[j]
    return xc, sh


def _lru_fwd(xb, halo, pos0, prm, h0):
    cw, cb, wgx, bgx, wga, bga, ap = prm
    n = xb.shape[0]
    xc, sh = _lru_conv(xb, halo, cw, cb[...])
    zx = jnp.concatenate([_nn(xc[:, h * HD:(h + 1) * HD], wgx[h]) for h in range(NH)], axis=1) + bgx[...]
    za = jnp.concatenate([_nn(xc[:, h * HD:(h + 1) * HD], wga[h]) for h in range(NH)], axis=1) + bga[...]
    first = (pos0 + lax.broadcasted_iota(jnp.int32, (n, 1), 0)) == 0
    a, b, gates = _lru_gates(xc, zx, za, ap[...], first)
    hseq = _scan_fwd(a, b, h0)
    return hseq, (xc, sh, gates, first, a)


def _lru_bwd(fres, hseq, h0, dtok, prm, carry_refs, grad_refs):
    cw, cb, wgx, bgx, wga, bga, ap = prm
    xc, sh, (gx, ga, sp, mult), first, a = fres
    anext_ref, dhnext_ref, dxcnext_ref = carry_refs
    dcw_ref, dcb_ref, dwgx_ref, dbgx_ref, dwga_ref, dbga_ref, dap_ref = grad_refs
    n = xc.shape[0]
    an = jnp.where(lax.broadcasted_iota(jnp.int32, a.shape, 0) == n - 1, anext_ref[...], pltpu.roll(a, n - 1, 0))
    dh = _scan_bwd(an, dtok, dhnext_ref[...])
    hprev = jnp.where(lax.broadcasted_iota(jnp.int32, hseq.shape, 0) == 0, h0, pltpu.roll(hseq, 1, 0))
    da = dh * hprev
    anext_ref[...] = _row(a, 0)
    dhnext_ref[...] = _row(dh, 0)
    t = dh * xc
    dxc = dh * mult * gx
    dzx = t * mult * gx * (1.0 - gx)
    dlog_a = da * a - jnp.where(first, 0.0, t * gx * (a * a) / mult)
    dza = dlog_a * (-LRU_C * sp) * ga * (1.0 - ga)
    dap = _rowsum(dlog_a * ga) * (LRU_C * _sigmoid(-ap[...]))
    _acc(dap_ref, dap)
    _acc(dbgx_ref, _rowsum(dzx))
    _acc(dbga_ref, _rowsum(dza))
    parts = []
    for h in range(NH):
        sl = slice(h * HD, (h + 1) * HD)
        parts.append(_nt(dzx[:, sl], wgx[h]) + _nt(dza[:, sl], wga[h]))
        dwgx_ref[h] += _tn(xc[:, sl], dzx[:, sl])
        dwga_ref[h] += _tn(xc[:, sl], dza[:, sl])
    dxc = dxc + jnp.concatenate(parts, axis=1)
    _acc(dcb_ref, _rowsum(dxc))
    for j in range(4):
        dcw_ref[pl.ds(j, 1), :] += _rowsum(dxc * sh[j])
    ext = jnp.concatenate([dxc, dxcnext_ref[...]], axis=0)
    m = n + 8
    dxb = cw[pl.ds(3, 1), :] * dxc
    for j in range(3):
        dxb = dxb + cw[pl.ds(j, 1), :] * pltpu.roll(ext, m - (3 - j), 0)[:n]
    dxcnext_ref[...] = dxc[:8]
    return dxb


def _layer_fwd(kind, layer, xprev, gprev, bprev, wt, wout, ks, vs, prm, ride=None):
    TS = TS_FWD[kind]
    _rows = functools.partial(_row_spec, ts=TS)
    S = xprev.shape[0]
    nt = S // TS
    N = wt.shape[0]
    nprm = len(prm)
    nch = TS // CH

    outs = [(jax.ShapeDtypeStruct((S, N), F32), _rows(N, nt, False)),
            (jax.ShapeDtypeStruct((S, D), F32), _rows(D, nt, False)),
            (jax.ShapeDtypeStruct((S, 1), F32), _rows(1, nt, False)),
            (jax.ShapeDtypeStruct((S, XHEADS * NMEM), MXU), _rows(XHEADS * NMEM, nt, False))]
    scratch = []
    if kind == 1:
        outs.append((jax.ShapeDtypeStruct((nt, NH, nch, HD, HD), F32),
                     pl.BlockSpec((None, NH, nch, HD, HD), lambda i: (i, 0, 0, 0, 0))))
        outs.append((jax.ShapeDtypeStruct((S, TOK), F32), _rows(TOK, nt, False)))
        outs.append((jax.ShapeDtypeStruct((S, TOK), MXU), _rows(TOK, nt, False)))
        scratch = [pltpu.VMEM((NH, HD, HD), F32)]
    elif kind == 2:
        outs += [(jax.ShapeDtypeStruct((S, TOK), F32), _rows(TOK, nt, False))] * 2
        scratch = [pltpu.VMEM((16, TOK), F32)]
    elif kind == 3:
        outs.append((jax.ShapeDtypeStruct((nt * 8, TOK), F32), pl.BlockSpec((8, TOK), lambda i: (i, 0))))
        outs += [(jax.ShapeDtypeStruct((S, TOK), F32), _rows(TOK, nt, False))] * 4
        scratch = [pltpu.VMEM((8, TOK), F32), pltpu.VMEM((1, TOK), F32)]
    nout = len(outs)
    nscr = len(scratch)
    nride = len(ride.arrays) if ride else 0

    def body(*refs):
        x_ref, g_ref, b_ref, wt_ref, wout_ref, ks_ref, vs_ref = refs[:7]
        prm_refs = refs[7:7 + nprm]
        nin = 7 + nprm + nride
        ride_src = refs[7 + nprm:nin]
        out_refs = refs[nin:nin + nout]
        ride_dst = refs[nin + nout:nin + nout + nride]
        scr = refs[nin + nout + nride:nin + nout + nride + nscr]
        ride_sems = refs[nin + nout + nride + nscr:]
        proj_ref, xhat_ref, rstd_ref = out_refs[:3]
        i = pl.program_id(0)
        if ride:
            @pl.when(i == 0)
            def _():
                ride.start(ride_src, ride_dst, ride_sems)

        xin = x_ref[...] * g_ref[...] + b_ref[...]
        proj = _nt(xin, wt_ref[...])
        proj_ref[...] = proj
        if kind == 0:
            tok, _ = _gmlp_fwd(proj[:, :TOK], proj[:, TOK:2 * TOK], prm_refs[0], prm_refs[1])
        elif kind == 1:
            st_ref, = scr

            @pl.when(i == 0)
            def _():
                st_ref[...] = jnp.zeros_like(st_ref)

            lb, _ = _lower_bound(prm_refs[0][...], layer)
            tok, out_refs[5][...], out_refs[6][...] = _hgrn_fwd(proj[:, :TOK], proj[:, TOK:2 * TOK], proj[:, 2 * TOK:3 * TOK],
                                                                lb, prm_refs[1][...], st_ref, out_refs[4])
        elif kind == 2:
            halo_ref, = scr

            @pl.when(i == 0)
            def _():
                halo_ref[...] = jnp.zeros_like(halo_ref)

            p = proj[:, :TOK]
            tok, (diff, y, _) = _pool_fwd(p, halo_ref[...], i * TS, prm_refs[0][...], prm_refs[1][...])
            out_refs[4][...] = diff
            out_refs[5][...] = y
            halo_ref[...] = p[TS - 16:]
        else:
            halo_ref, h_ref = scr

            @pl.when(i == 0)
            def _():
                halo_ref[...] = jnp.zeros_like(halo_ref)
                h_ref[...] = jnp.zeros_like(h_ref)

            out_refs[4][...] = jnp.broadcast_to(h_ref[...], (8, TOK))
            xb = proj[:, :TOK]
            tok, fres = _lru_fwd(xb, halo_ref[...], i * TS, prm_refs, h_ref[...])
            gx, ga, _, mult = fres[2]
            for r, val in zip(out_refs[5:9], (tok, gx, ga, mult)):
                r[...] = val
            halo_ref[...] = xb[TS - 8:]
            h_ref[...] = _row(tok, TS - 1)
        qx = proj[:, N - D - XW:N - D]
        gate = proj[:, N - D:]
        xo, ps = _xattn_fwd(qx, ks_ref, vs_ref)
        out_refs[3][...] = jnp.concatenate(ps, axis=1).astype(MXU)
        mixed = jnp.concatenate([tok, xo], axis=1) * (gate * _sigmoid(gate))
        z = ALPHA * xin + _nn(mixed, wout_ref[...])
        cen = z - jnp.mean(z, axis=-1, keepdims=True)
        rstd = lax.rsqrt(jnp.mean(cen * cen, axis=-1, keepdims=True) + LN_EPS)
        xhat_ref[...] = cen * rstd
        rstd_ref[...] = rstd
        if ride:
            @pl.when(i == nt - 1)
            def _():
                ride.wait(ride_src, ride_dst, ride_sems)

    ins = [(xprev, _rows(D, nt, False)), (gprev, _res(gprev)), (bprev, _res(bprev)), (wt, _res(wt)), (wout, _res(wout)),
           (ks, _res(ks)), (vs, _res(vs))] + [(p, _res(p)) for p in prm]
    if ride:
        ins += [(a, _ANY) for a in ride.arrays]
        outs += [(s, _ANY) for s in ride.out_shapes]
        scratch = scratch + ride.scratch
    return _call(body, f"layer{layer}_fwd", (nt,), ins, outs, scratch)


def _layer_bwd(kind, layer, up, is_last, xhat, rstd, probs, g_i, b_i, proj, wout, ks, vs, prm, extra, ride=None):
    TS = TS_BWD[kind]
    _rows = functools.partial(_row_spec, ts=TS)
    S = xhat.shape[0]
    nt = S // TS
    N = proj.shape[1]
    nprm = len(prm)
    nch = TS // CH

    ins = [(up, _rows(D, nt, True)), (xhat, _rows(D, nt, True)), (rstd, _rows(1, nt, True)), (g_i, _res(g_i)), (b_i, _res(b_i)),
           (proj, _rows(N, nt, True)), (wout, _res(wout)), (ks, _res(ks)), (vs, _res(vs)),
           (probs, _rows(XHEADS * NMEM, nt, True))] + [(p, _res(p)) for p in prm]
    nfixed = 10
    if kind == 1:
        ins.append((extra[0], pl.BlockSpec((None, NH, nch, HD, HD), lambda i: (nt - 1 - i, 0, 0, 0, 0))))
        ins += [(e, _rows(TOK, nt, True)) for e in extra[1:]]
    elif kind == 2:
        ins += [(e, _rows(TOK, nt, True)) for e in extra]
    elif kind == 3:
        hb = TS // 8
        ins.append((proj, pl.BlockSpec((8, TOK), lambda i: (jnp.maximum((nt - 1 - i) * hb - 1, 0), 0))))
        ins.append((extra[0], pl.BlockSpec((8, TOK), lambda i: (nt - 1 - i, 0))))
        ins += [(e, _rows(TOK, nt, True)) for e in extra[1:]]
    nin = len(ins)

    def acc(shape):
        return (jax.ShapeDtypeStruct(shape, F32), _res_sds(shape))

    outs = [(jax.ShapeDtypeStruct((S, D), F32), _rows(D, nt, True)),
            (jax.ShapeDtypeStruct((S, N), MXU), _rows(N, nt, True)),
            (jax.ShapeDtypeStruct((D, D), WIRE), _res_sds((D, D))),
            acc((XHEADS, NMEM, XW)), acc((XHEADS, NMEM, XW)), acc((1, D)), acc((1, D)), acc((1, HD))]
    scratch = []
    if kind == 0:
        outs += [acc((NH, HD, HD)), acc((NH, HD, HD))]
    elif kind == 1:
        outs += [acc((1, TOK)), acc((1, TOK))]
        scratch = [pltpu.VMEM((NH, HD, HD), F32)]
    elif kind == 2:
        outs += [acc((TOK, TOK)), acc((1, TOK))]
        scratch = [pltpu.VMEM((16, TOK), F32)]
    else:
        outs += [acc((4, TOK)), acc((1, TOK)), acc((NH, HD, HD)), acc((1, TOK)), acc((NH, HD, HD)), acc((1, TOK)), acc((1, TOK))]
        scratch = [pltpu.VMEM((1, TOK), F32), pltpu.VMEM((1, TOK), F32), pltpu.VMEM((8, TOK), F32)]
    scratch = scratch + [pltpu.VMEM((D, D), F32)]
    nout = len(outs)
    nscr = len(scratch)
    nride = len(ride.arrays) if ride else 0

    def body(*refs):
        up_ref, xhat_ref, rstd_ref, g_ref, b_ref, proj_ref, wout_ref, ks_ref, vs_ref, probs_ref = refs[:nfixed]
        prm_refs = refs[nfixed:nfixed + nprm]
        ext_refs = refs[nfixed + nprm:nin]
        ride_src = refs[nin:nin + nride]
        o0 = nin + nride
        out_refs = refs[o0:o0 + nout]
        ride_dst = refs[o0 + nout:o0 + nout + nride]
        scr = refs[o0 + nout + nride:o0 + nout + nride + nscr - 1]
        dwout_acc = refs[o0 + nout + nride + nscr - 1]
        ride_sems = refs[o0 + nout + nride + nscr:]
        dres_ref, dproj_ref, dwout_ref, dks_ref, dvs_ref, dg_ref, db_ref, loss_ref = out_refs[:8]
        pgrad = out_refs[8:]
        i = pl.program_id(0)
        tile = nt - 1 - i

        @pl.when(i == 0)
        def _():
            if ride:
                ride.start(ride_src, ride_dst, ride_sems)
            for r in out_refs[3:]:
                r[...] = jnp.zeros_like(r)
            dwout_acc[...] = jnp.zeros_like(dwout_acc)
            for r in scr:
                if kind != 1 or r is scr[0]:
                    r[...] = jnp.zeros_like(r)

        xhat_v = xhat_ref[...]
        if is_last:
            err = xhat_v * g_ref[...] + b_ref[...] - up_ref[...]
            dxo = err * (1.0 / D)
            loss_ref[...] += jnp.sum(0.5 * jnp.mean(err * err, axis=-1, keepdims=True), axis=0, keepdims=True)
        else:
            dxo = up_ref[...]
        _acc(dg_ref, _rowsum(dxo * xhat_v))
        _acc(db_ref, _rowsum(dxo))
        dxh = dxo * g_ref[...]
        dz = rstd_ref[...] * (dxh - jnp.mean(dxh, axis=-1, keepdims=True)
                              - xhat_v * jnp.mean(dxh * xhat_v, axis=-1, keepdims=True))
        dres_ref[...] = ALPHA * dz

        proj = proj_ref[...]
        qx = proj[:, N - D - XW:N - D]
        gate = proj[:, N - D:]
        sgate = _sigmoid(gate)
        silu = gate * sgate
        dmixed = _nt(dz, wout_ref[...])
        dcat = dmixed * silu
        dtok = dcat[:, :TOK]
        if kind == 0:
            u, v = proj[:, :TOK], proj[:, TOK:2 * TOK]
            tok, fres = _gmlp_fwd(u, v, prm_refs[0], prm_refs[1])
            du, dv = _gmlp_bwd(u, v, fres, dtok, pgrad[0], pgrad[1])
            dproj_ref[:, :TOK] = du.astype(MXU)
            dproj_ref[:, TOK:2 * TOK] = dv.astype(MXU)
        elif kind == 1:
            dst_ref, = scr
            lb, _ = _lower_bound(prm_refs[0][...], layer)
            q, fl, inp = proj[:, :TOK], proj[:, TOK:2 * TOK], proj[:, 2 * TOK:3 * TOK]
            pr = _hgrn_prep(q, fl, lb)
            tok, dq, dfl, dinp = _hgrn_bwd(q, pr, inp, lb, prm_refs[1][...], dtok, ext_refs[1][...], ext_refs[2][...],
                                           ext_refs[0], dst_ref, pgrad[1], pgrad[0])
            dproj_ref[:, :TOK] = dq.astype(MXU)
            dproj_ref[:, TOK:2 * TOK] = dfl.astype(MXU)
            dproj_ref[:, 2 * TOK:3 * TOK] = dinp.astype(MXU)
        elif kind == 2:
            diff, y = ext_refs[0][...], ext_refs[1][...]
            tok = y * prm_refs[1][...]
            fres = (diff, y, _pool_cnt(tile * TS, TS))
            dp = _pool_bwd(fres, dtok, scr[0], prm_refs[0][...], prm_refs[1][...], pgrad[0], pgrad[1])
            dproj_ref[:, :TOK] = dp.astype(MXU)
        else:
            xb = proj[:, :TOK]
            halo = jnp.where(tile == 0, 0.0, ext_refs[0][...])
            h0 = ext_refs[1][0:1]
            tok, gx, ga, mult = (r[...] for r in ext_refs[2:6])
            xc, sh = _lru_conv(xb, halo, prm_refs[0], prm_refs[1][...])
            sp = _softplus_neg(prm_refs[6][...])
            first = (tile * TS + lax.broadcasted_iota(jnp.int32, (TS, 1), 0)) == 0
            fres = (xc, sh, (gx, ga, sp, mult), first, jnp.exp(-LRU_C * ga * sp))
            dxb = _lru_bwd(fres, tok, h0, dtok, prm_refs, scr, pgrad)
            dproj_ref[:, :TOK] = dxb.astype(MXU)
        ps = [probs_ref[:, h * NMEM:(h + 1) * NMEM].astype(F32) for h in range(XHEADS)]
        xo = _nn(ps[0], vs_ref[0])
        for h in range(1, XHEADS):
            xo = xo + _nn(ps[h], vs_ref[h])
        dqx = _xattn_bwd(qx, ps, dcat[:, TOK:], ks_ref, vs_ref, dks_ref, dvs_ref)
        cat = jnp.concatenate([tok, xo], axis=1)
        dwout_acc[...] += _tn(cat * silu, dz)
        dgate = dmixed * cat * (sgate * (1.0 + gate * (1.0 - sgate)))
        dproj_ref[:, N - D - XW:N - D] = dqx.astype(MXU)
        dproj_ref[:, N - D:] = dgate.astype(MXU)

        @pl.when(i == nt - 1)
        def _():
            dwout_ref[...] = dwout_acc[...].astype(WIRE)
            if ride:
                ride.wait(ride_src, ride_dst, ride_sems)

    if ride:
        ins += [(a, _ANY) for a in ride.arrays]
        outs += [(s, _ANY) for s in ride.out_shapes]
        scratch = scratch + ride.scratch
    return _call(body, f"layer{layer}_bwd", (nt,), ins, outs, scratch, vmem=VMEM_LIMIT_WIDE if kind == 0 else VMEM_LIMIT)


def _proj_bwd(layer, dproj, dres, xprev, gprev, bprev, wt, ride=None):
    S = xprev.shape[0]
    nt = S // TSB
    N = wt.shape[0]

    nride = len(ride.arrays) if ride else 0

    def body(*refs):
        dproj_ref, dres_ref, x_ref, g_ref, b_ref, wt_ref = refs[:6]
        ride_src = refs[6:6 + nride]
        dx_ref, dwt_ref = refs[6 + nride:8 + nride]
        ride_dst = refs[8 + nride:8 + 2 * nride]
        acc_ref = refs[8 + 2 * nride]
        ride_sems = refs[9 + 2 * nride:]

        @pl.when(pl.program_id(0) == 0)
        def _():
            if ride:
                ride.start(ride_src, ride_dst, ride_sems)
            acc_ref[...] = jnp.zeros_like(acc_ref)

        dp = dproj_ref[...]
        xin = x_ref[...] * g_ref[...] + b_ref[...]
        dx_ref[...] = dres_ref[...] + _nn(dp, wt_ref[...])
        acc_ref[...] += _tn(dp, xin)

        @pl.when(pl.program_id(0) == nt - 1)
        def _():
            dwt_ref[...] = acc_ref[...].astype(WIRE)
            if ride:
                ride.wait(ride_src, ride_dst, ride_sems)

    ins = [(dproj, _row_spec(N, nt, False, TSB)), (dres, _row_spec(D, nt, False, TSB)), (xprev, _row_spec(D, nt, False, TSB)),
           (gprev, _res(gprev)), (bprev, _res(bprev)), (wt, _res(wt))]
    outs = [(jax.ShapeDtypeStruct((S, D), F32), _row_spec(D, nt, False, TSB)),
            (jax.ShapeDtypeStruct((N, D), WIRE), _res_sds((N, D)))]
    scratch = [pltpu.VMEM((N, D), F32)]
    if ride:
        ins += [(a, _ANY) for a in ride.arrays]
        outs += [(s, _ANY) for s in ride.out_shapes]
        scratch = scratch + ride.scratch
    return _call(body, f"layer{layer}_projbwd", (nt,), ins, outs, scratch)


def _head_mask(h):
    col = lax.broadcasted_iota(jnp.int32, (1, XW), 1)
    return (col // 64) == h


def _kv_fwd(mem, wkv):
    def body(mem_ref, w_ref, ks_ref, vs_ref):
        kv = _nn(mem_ref[...], w_ref[...])
        k, v = kv[:, :XW], kv[:, XW:]
        for h in range(XHEADS):
            ks_ref[h] = jnp.where(_head_mask(h), k, 0.0).astype(MXU)
            vs_ref[h] = jnp.where(_head_mask(h), v, 0.0).astype(MXU)

    sds = jax.ShapeDtypeStruct((XHEADS, NMEM, XW), MXU)
    return pl.pallas_call(body, name="kv_fwd", out_shape=(sds, sds), compiler_params=_cparams())(mem, wkv)


def _kv_bwd(mem, dks_l, dvs_l):
    def body(mem_ref, *refs):
        dks_refs, dvs_refs, out_ref = refs[:DEPTH], refs[DEPTH:2 * DEPTH], refs[2 * DEPTH]
        dk = jnp.zeros((NMEM, XW), F32)
        dv = jnp.zeros((NMEM, XW), F32)
        for h in range(XHEADS):
            m = _head_mask(h)
            for l in range(DEPTH):
                dk = dk + jnp.where(m, dks_refs[l][h], 0.0)
                dv = dv + jnp.where(m, dvs_refs[l][h], 0.0)
        out_ref[...] = _tn(mem_ref[...], jnp.concatenate([dk, dv], axis=1)).astype(WIRE)

    return pl.pallas_call(body, name="kv_bwd", out_shape=jax.ShapeDtypeStruct((D, 2 * XW), WIRE),
                          compiler_params=_cparams())(mem, *dks_l, *dvs_l)


def _prep_weights(w_ins, flip, w_out, wkv):
    def body(a_ref, b_ref, c_ref, d_ref, wo_ref, kv_ref, ao, bo, co, do, wo0, wo1, wo2, wo3, kvo):
        for t, (src, dst) in enumerate(((a_ref, ao), (b_ref, bo), (c_ref, co), (d_ref, do))):
            dst[...] = (src[...].T if flip[t] else src[...]).astype(MXU)
        for l, dst in enumerate((wo0, wo1, wo2, wo3)):
            dst[...] = wo_ref[l].astype(MXU)
        kvo[...] = kv_ref[...].astype(MXU)

    outs = [jax.ShapeDtypeStruct(w.shape[::-1] if flip[t] else w.shape, MXU) for t, w in enumerate(w_ins)]
    outs += [jax.ShapeDtypeStruct(w_out.shape[1:], MXU)] * DEPTH + [jax.ShapeDtypeStruct(wkv.shape, MXU)]
    return pl.pallas_call(body, name="prep_weights", out_shape=outs, compiler_params=_cparams())(*w_ins, w_out, wkv)


def _adam_math(w, g, m, v):
    m = B1 * m + (1.0 - B1) * g
    v = B2 * v + (1.0 - B2) * (g * g)
    m_hat = m / (1.0 - B1 ** STEP)
    v_hat = v / (1.0 - B2 ** STEP)
    delta = -LR * (m_hat / (jnp.sqrt(v_hat) + EPS) + WD * w)
    return delta, m, v


def _sum_adam(name, recv, w, m, v, transpose):
    rows, cols = recv.shape[1], recv.shape[2]

    def body(r_ref, w_ref, m_ref, v_ref, g_out, d_out, m_out, v_out, acc_ref):
        s = pl.program_id(0)

        @pl.when(s == 0)
        def _():
            acc_ref[...] = r_ref[...].astype(F32)

        @pl.when(s > 0)
        def _():
            acc_ref[...] += r_ref[...].astype(F32)

        @pl.when(s == NDEV - 1)
        def _():
            g = acc_ref[...].T if transpose else acc_ref[...]
            d, mn, vn = _adam_math(w_ref[...], g, m_ref[...], v_ref[...])
            g_out[...] = g
            d_out[...] = d
            m_out[...] = mn
            v_out[...] = vn

    sds = jax.ShapeDtypeStruct(w.shape, F32)
    ins = [(recv, pl.BlockSpec((None, rows, cols), lambda s: (s, 0, 0))), (w, _res(w)), (m, _res(m)), (v, _res(v))]
    outs = [(sds, _res_sds(w.shape))] * 4
    return _call(body, name, (NDEV,), ins, outs, [pltpu.VMEM((rows, cols), F32)])


def _bias_finalize(dbs_exp):
    def body(dbs_ref, dabs_ref):
        dabs_ref[...] = jnp.sum(dbs_ref[...], axis=-1)

    return pl.pallas_call(body, name="bias_finalize", out_shape=jax.ShapeDtypeStruct((NH, HD), F32),
                          compiler_params=_cparams())(dbs_exp)


def _lb_finalize(dlb, lb_logits):
    def body(dlb_ref, lg_ref, dlg_ref):
        total = jnp.zeros((DEPTH, TOK), F32)
        lg = lg_ref[...]
        e = jnp.exp(lg - jnp.max(lg, axis=0, keepdims=True))
        p = e / jnp.sum(e, axis=0, keepdims=True)
        row = lax.broadcasted_iota(jnp.int32, (DEPTH, TOK), 0)
        for layer in range(DEPTH):
            if layer % 4 != 1:
                continue
            dp = jnp.where((row >= 1) & (row <= layer), dlb_ref[...], 0.0)
            total = total + p * (dp - jnp.sum(p * dp, axis=0, keepdims=True))
        dlg_ref[...] = total

    return pl.pallas_call(body, name="lb_finalize", out_shape=jax.ShapeDtypeStruct((DEPTH, TOK), F32),
                          compiler_params=_cparams())(dlb, lb_logits)


def _small_sum_adam(name, gathered, w, m, v):
    rows = w.shape[0]

    def body(r_ref, w_ref, m_ref, v_ref, g_out, d_out, m_out, v_out):
        g = r_ref[0]
        for s in range(1, NDEV):
            g = g + r_ref[s]
        d, mn, vn = _adam_math(w_ref[...], g, m_ref[...], v_ref[...])
        g_out[...] = g
        d_out[...] = d
        m_out[...] = mn
        v_out[...] = vn

    sds = jax.ShapeDtypeStruct((rows, 128), F32)
    return pl.pallas_call(body, name=name, out_shape=(sds,) * 4, compiler_params=_cparams())(gathered, w, m, v)


def _sum_adam_stacked(name, recvs, w, m, v):
    nl, rows, cols = w.shape

    def body(*refs):
        r_refs = refs[:nl]
        w_ref, m_ref, v_ref, g_out, d_out, m_out, v_out, acc_ref = refs[nl:]
        s = pl.program_id(0)

        @pl.when(s == 0)
        def _():
            for l in range(nl):
                acc_ref[l] = r_refs[l][...].astype(F32)

        @pl.when(s > 0)
        def _():
            for l in range(nl):
                acc_ref[l] += r_refs[l][...].astype(F32)

        @pl.when(s == NDEV - 1)
        def _():
            g = acc_ref[...]
            d, mn, vn = _adam_math(w_ref[...], g, m_ref[...], v_ref[...])
            g_out[...] = g
            d_out[...] = d
            m_out[...] = mn
            v_out[...] = vn

    sds = jax.ShapeDtypeStruct(w.shape, F32)
    ins = [(r, pl.BlockSpec((None, rows, cols), lambda s: (s, 0, 0))) for r in recvs] + [(w, _res(w)), (m, _res(m)), (v, _res(v))]
    outs = [(sds, _res_sds(w.shape))] * 4
    return _call(body, name, (NDEV,), ins, outs, [pltpu.VMEM(w.shape, F32)])


def _group_adam(name, recvs, params):
    nk = len(recvs)

    def body(*refs):
        pos, oi = nk, nk + 3 * sum(p is not None for p in params)
        for k in range(nk):
            g = refs[k][0]
            for s in range(1, NDEV):
                g = g + refs[k][s]
            refs[oi][...] = g
            oi += 1
            if params[k] is not None:
                d, mn, vn = _adam_math(refs[pos][...], g, refs[pos + 1][...], refs[pos + 2][...])
                refs[oi][...] = d
                refs[oi + 1][...] = mn
                refs[oi + 2][...] = vn
                pos += 3
                oi += 3

    out_shape, counts = [], []
    for k in range(nk):
        counts.append(4 if params[k] is not None else 1)
        out_shape += [jax.ShapeDtypeStruct(recvs[k].shape[1:], F32)] * counts[-1]
    args = list(recvs) + [a for p in params if p is not None for a in p]
    flat = pl.pallas_call(body, name=name, out_shape=out_shape, compiler_params=_cparams())(*args)
    res, o = [], 0
    for cnt in counts:
        res.append(flat[o:o + cnt])
        o += cnt
    return res


def _adam_only(g, w, m, v):
    def body(g_ref, w_ref, m_ref, v_ref, d_out, m_out, v_out):
        d, mn, vn = _adam_math(w_ref[...], g_ref[...], m_ref[...], v_ref[...])
        d_out[...] = d
        m_out[...] = mn
        v_out[...] = vn

    sds = jax.ShapeDtypeStruct(w.shape, F32)
    return pl.pallas_call(body, name="shard_adam", out_shape=(sds,) * 3, compiler_params=_cparams())(g, w, m, v)


def _me_and_peers():
    x, y, c = lax.axis_index("x"), lax.axis_index("y"), lax.axis_index("c")
    me = 4 * x + 2 * y + c
    peers = []
    for k in range(1, NDEV):
        kx, ky, kc = (k >> 2) & 1, (k >> 1) & 1, k & 1
        px = x + kx - 2 * x * kx
        py = y + ky - 2 * y * ky
        pc = c + kc - 2 * c * kc
        peers.append(((px, py, pc), 4 * px + 2 * py + pc))
    return me, peers


_ANY = pl.BlockSpec(memory_space=pl.ANY)


class _Exchange:
    def __init__(self, arrays, split):
        self.arrays = list(arrays)
        self.split = list(split)
        n = len(self.arrays)
        self.out_shapes = []
        for a, sp in zip(self.arrays, self.split):
            rows = a.shape[0] // NDEV if sp else a.shape[0]
            self.out_shapes.append(jax.ShapeDtypeStruct((NDEV, rows, a.shape[1]), a.dtype))
        self.scratch = [pltpu.SemaphoreType.DMA((n, NDEV - 1)), pltpu.SemaphoreType.DMA((n, NDEV - 1)),
                        pltpu.SemaphoreType.DMA((n,))]

    def _block(self, src, t, d):
        if not self.split[t]:
            return src[t]
        rows = self.arrays[t].shape[0] // NDEV
        return src[t].at[pl.ds(d * rows, rows)]

    def start(self, src, dst, sems):
        send_sems, recv_sems, local_sems = sems
        me, peers = _me_and_peers()
        for t in range(len(self.arrays)):
            pltpu.make_async_copy(self._block(src, t, me), dst[t].at[me], local_sems.at[t]).start()
        for k, (dev, idx) in enumerate(peers):
            for t in range(len(self.arrays)):
                pltpu.make_async_remote_copy(src_ref=self._block(src, t, idx), dst_ref=dst[t].at[me],
                                             send_sem=send_sems.at[t, k], recv_sem=recv_sems.at[t, k],
                                             device_id=dev, device_id_type=pl.DeviceIdType.MESH).start()

    def wait(self, src, dst, sems):
        send_sems, recv_sems, local_sems = sems
        me, peers = _me_and_peers()

        def slot_copy(t, k, dev, idx):
            return pltpu.make_async_remote_copy(src_ref=dst[t].at[idx], dst_ref=dst[t].at[idx], send_sem=send_sems.at[t, k],
                                                recv_sem=recv_sems.at[t, k], device_id=dev,
                                                device_id_type=pl.DeviceIdType.MESH)

        for k, (dev, idx) in enumerate(peers):
            for t in range(len(self.arrays)):
                slot_copy(t, k, dev, idx).wait_recv()
        for k, (dev, idx) in enumerate(peers):
            for t in range(len(self.arrays)):
                slot_copy(t, k, dev, idx).wait_send()
        for t in range(len(self.arrays)):
            pltpu.make_async_copy(dst[t].at[me], dst[t].at[me], local_sems.at[t]).wait()

    def gather_by_chip(self, src, dst, sems):
        assert not any(self.split)
        send_sems, recv_sems, local_sems = sems
        n = len(self.arrays)
        x, y, c = lax.axis_index("x"), lax.axis_index("y"), lax.axis_index("c")
        me, sibling = 4 * x + 2 * y + c, (x, y, 1 - c)
        chips = [(1 - x, y), (x, 1 - y), (1 - x, 1 - y)]

        def index(chip, core):
            return 4 * chip[0] + 2 * chip[1] + core

        def copy(t, k, block, to, from_src):
            return pltpu.make_async_remote_copy(src_ref=src[t] if from_src else dst[t].at[block], dst_ref=dst[t].at[block],
                                                send_sem=send_sems.at[t, k], recv_sem=recv_sems.at[t, k],
                                                device_id=to, device_id_type=pl.DeviceIdType.MESH)

        local = [pltpu.make_async_copy(src[t], dst[t].at[me], local_sems.at[t]) for t in range(n)]
        for cp in local:
            cp.start()
        sends = []
        for t in range(n):
            sends.append(copy(t, 0, me, sibling, True))
            sends += [copy(t, 1 + j, me, (*chip, c), True) for j, chip in enumerate(chips)]
        for cp in sends:
            cp.start()
        for j, chip in enumerate(chips):
            for t in range(n):
                copy(t, 1 + j, index(chip, c), sibling, False).wait_recv()
                passed = copy(t, 4 + j, index(chip, c), sibling, False)
                passed.start()
                sends.append(passed)
        for t in range(n):
            copy(t, 0, index((x, y), 1 - c), sibling, False).wait_recv()
            for j, chip in enumerate(chips):
                copy(t, 4 + j, index(chip, 1 - c), sibling, False).wait_recv()
        for cp in sends:
            cp.wait_send()
        for cp in local:
            cp.wait()

    def run(self, name, by_chip=False):
        n = len(self.arrays)

        def body(*refs):
            src, dst, sems = refs[:n], refs[n:2 * n], refs[2 * n:]
            if by_chip:
                self.gather_by_chip(src, dst, sems)
                return
            self.start(src, dst, sems)
            self.wait(src, dst, sems)

        return pl.pallas_call(
            body, name=name, out_shape=self.out_shapes, in_specs=[_ANY] * n, out_specs=[_ANY] * n,
            scratch_shapes=self.scratch,
        )(*self.arrays)


SMALL = [("ln_g", (DEPTH, D), False), ("ln_b", (DEPTH, D), False), ("hgrn_lb_logits", (DEPTH, TOK), False),
         ("a_w_s", (1, NH, HD, HD), False), ("a_b_s", (1, NH, HD), False), ("b_norm_g", (1, TOK), True),
         ("c_w_pool", (1, 4, POOL_GROUP, POOL_GROUP), False), ("c_scale", (1, TOK), True),
         ("d_conv_w", (1, 4, TOK), True), ("d_conv_b", (1, TOK), True),
         ("d_w_gx", (1, NH, HD, HD), False), ("d_b_gx", (1, NH, HD), False),
         ("d_w_ga", (1, NH, HD, HD), False), ("d_b_ga", (1, NH, HD), False), ("d_a_param", (1, TOK), True)]


def _pack(parts, total_rows):
    flat = jnp.concatenate([p.reshape(-1).astype(F32) for p in parts])
    flat = jnp.pad(flat, (0, total_rows * 128 - flat.shape[0]))
    return flat.reshape(total_rows, 128)


def _size(shape):
    n = 1
    for s in shape:
        n *= s
    return n


def _rows_for(n):
    return -(-n // 1024) * 8


def kernel(x, mem, mem_kv_w, ln_g, ln_b, w_out, hgrn_lb_logits, a_w_in, a_w_s, a_b_s, b_w_in, b_norm_g, c_w_in, c_w_pool, c_scale, d_w_in, d_conv_w, d_conv_b, d_w_gx, d_b_gx, d_w_ga, d_b_ga, d_a_param, loss_target, m_mem_kv_w, m_ln_g, m_ln_b, m_w_out, m_hgrn_lb_logits, m_a_w_in, m_a_w_s, m_a_b_s, m_b_w_in, m_b_norm_g, m_c_w_in, m_c_w_pool, m_c_scale, m_d_w_in, m_d_conv_w, m_d_conv_b, m_d_w_gx, m_d_b_gx, m_d_w_ga, m_d_b_ga, m_d_a_param, v_mem_kv_w, v_ln_g, v_ln_b, v_w_out, v_hgrn_lb_logits, v_a_w_in, v_a_w_s, v_a_b_s, v_b_w_in, v_b_norm_g, v_c_w_in, v_c_w_pool, v_c_scale, v_d_w_in, v_d_conv_w, v_d_conv_b, v_d_w_gx, v_d_b_gx, v_d_w_ga, v_d_b_ga, v_d_a_param):
    W = dict(mem_kv_w=mem_kv_w, ln_g=ln_g, ln_b=ln_b, w_out=w_out, hgrn_lb_logits=hgrn_lb_logits, a_w_in=a_w_in, a_w_s=a_w_s,
             a_b_s=a_b_s, b_w_in=b_w_in, b_norm_g=b_norm_g, c_w_in=c_w_in, c_w_pool=c_w_pool, c_scale=c_scale, d_w_in=d_w_in,
             d_conv_w=d_conv_w, d_conv_b=d_conv_b, d_w_gx=d_w_gx, d_b_gx=d_b_gx, d_w_ga=d_w_ga, d_b_ga=d_b_ga, d_a_param=d_a_param)
    M = dict(mem_kv_w=m_mem_kv_w, ln_g=m_ln_g, ln_b=m_ln_b, w_out=m_w_out, hgrn_lb_logits=m_hgrn_lb_logits, a_w_in=m_a_w_in,
             a_w_s=m_a_w_s, a_b_s=m_a_b_s, b_w_in=m_b_w_in, b_norm_g=m_b_norm_g, c_w_in=m_c_w_in, c_w_pool=m_c_w_pool,
             c_scale=m_c_scale, d_w_in=m_d_w_in, d_conv_w=m_d_conv_w, d_conv_b=m_d_conv_b, d_w_gx=m_d_w_gx, d_b_gx=m_d_b_gx,
             d_w_ga=m_d_w_ga, d_b_ga=m_d_b_ga, d_a_param=m_d_a_param)
    V = dict(mem_kv_w=v_mem_kv_w, ln_g=v_ln_g, ln_b=v_ln_b, w_out=v_w_out, hgrn_lb_logits=v_hgrn_lb_logits, a_w_in=v_a_w_in,
             a_w_s=v_a_w_s, a_b_s=v_a_b_s, b_w_in=v_b_w_in, b_norm_g=v_b_norm_g, c_w_in=v_c_w_in, c_w_pool=v_c_w_pool,
             c_scale=v_c_scale, d_w_in=v_d_w_in, d_conv_w=v_d_conv_w, d_conv_b=v_d_conv_b, d_w_gx=v_d_w_gx, d_b_gx=v_d_b_gx,
             d_w_ga=v_d_w_ga, d_b_ga=v_d_b_ga, d_a_param=v_d_a_param)
    me = 4 * lax.axis_index("x") + 2 * lax.axis_index("y") + lax.axis_index("c")
    x2, mem2, tgt2 = x[0], mem[0], loss_target[0]
    in_names = ["a_w_in", "b_w_in", "c_w_in", "d_w_in"]

    shard_names = [n for n, _, sh in SMALL if sh]
    small_shard = _pack([W[n] for n in shard_names], 8)
    flip = [W[n].shape[2] % 128 == 0 for n in in_names]

    def shard2d(tree, t):
        a = tree[in_names[t]][0]
        return a if flip[t] else jnp.swapaxes(a, 0, 1)

    wts = _prep_weights([shard2d(W, t) for t in range(DEPTH)], flip, w_out, mem_kv_w)
    wt_sh, wo_sh, wkv_sh = wts[:4], wts[4:8], wts[8]
    g0 = _Exchange([wt_sh[0], wo_sh[0], wkv_sh, small_shard], [False] * 4).run("gather_first", by_chip=True)
    wt_full = [g0[0].reshape(-1, D)]
    wout_full = [g0[1].reshape(D, D)]
    wkv_full = g0[2].reshape(D, 2 * XW)
    sm = g0[3].reshape(NDEV, 1024)
    full_small = {}
    off = 0
    for n, shape, _ in [s for s in SMALL if s[2]]:
        per = _size(shape) // NDEV
        blk = sm[:, off:off + per]
        if n == "d_conv_w":
            full_small[n] = blk.reshape(NDEV, 4, TOK // NDEV).transpose(1, 0, 2).reshape(4, TOK)
        else:
            full_small[n] = blk.reshape(1, TOK)
        off += per

    ks, vs = _kv_fwd(mem2, wkv_full)
    tri_bs = jnp.broadcast_to(a_b_s[0][:, :, None], (NH, HD, HD))
    wbd = jnp.zeros((TOK, TOK), F32)
    for g in range(4):
        wbd = lax.dynamic_update_slice(wbd, c_w_pool[0, g], (g * POOL_GROUP, g * POOL_GROUP))
    wbd = wbd.astype(MXU)
    prm = {0: [a_w_s[0], tri_bs],
           1: [hgrn_lb_logits, full_small["b_norm_g"]],
           2: [wbd, full_small["c_scale"]],
           3: [full_small["d_conv_w"], full_small["d_conv_b"], d_w_gx[0].astype(MXU), d_b_gx[0].reshape(1, TOK),
               d_w_ga[0].astype(MXU), d_b_ga[0].reshape(1, TOK), full_small["d_a_param"]]}
    ones = jnp.ones((1, D), F32)
    zeros = jnp.zeros((1, D), F32)
    xs, gs, bs = [x2], [ones], [zeros]
    saved = []
    for i in range(DEPTH):
        ride = _Exchange([wt_sh[i + 1], wo_sh[i + 1]], [False, False]) if i + 1 < DEPTH else None
        res = _layer_fwd(i, i, xs[i], gs[i], bs[i], wt_full[i], wout_full[i], ks, vs, prm[i], ride)
        if ride:
            wt_full.append(res[-2].reshape(-1, D))
            wout_full.append(res[-1].reshape(D, D))
            res = res[:-2]
        saved.append(res)
        xs.append(res[1])
        gs.append(ln_g[i:i + 1])
        bs.append(ln_b[i:i + 1])

    up = tgt2
    grads = {}
    dks_l, dvs_l, dwt_l, dwout_l, dlng_l, dlnb_l = [], [], [], [], [], []
    recv_wt, recv_wo = [None] * DEPTH, [None] * DEPTH
    loss_part = None
    sharded = {n for n, _, sh in SMALL if sh}
    group = {3: ["ln_g#3", "ln_b#3", "d_conv_w", "d_conv_b", "d_w_gx", "d_b_gx", "d_w_ga", "d_b_ga", "d_a_param"],
             2: ["ln_g#2", "ln_b#2", "c_w_pool", "c_scale"],
             1: ["ln_g#1", "ln_b#1", "hgrn_lb_logits", "b_norm_g"],
             0: ["ln_g#0", "ln_b#0", "a_w_s", "a_b_s", "loss"]}

    def small_of(l):
        return [grads[e].reshape(-1, grads[e].shape[-1]) for e in group[l]]

    recv_small = [None] * DEPTH
    for i in reversed(range(DEPTH)):
        res = saved[i]
        extra = None if len(res) <= 4 else (res[4] if len(res) == 5 else res[4:])
        ride = None
        if i + 1 < DEPTH:
            smalls = small_of(i + 1)
            ride = _Exchange([dwt_l[-1], dwout_l[-1]] + smalls, [True, True] + [False] * len(smalls))
        out = _layer_bwd(i, i, up, i == DEPTH - 1, res[1], res[2], res[3], gs[i + 1], bs[i + 1], res[0], wout_full[i], ks, vs,
                         prm[i], extra, ride)
        if ride:
            nr = len(ride.arrays)
            recv_wt[i + 1], recv_wo[i + 1], recv_small[i + 1] = out[-nr], out[-nr + 1], out[-nr + 2:]
            out = out[:-nr]
        dres, dproj, dwout_i, dks_i, dvs_i, dg_i, db_i, loss_i = out[:8]
        pg = out[8:]
        if i == DEPTH - 1:
            grads["loss"] = loss_i
        dks_l.append(dks_i)
        dvs_l.append(dvs_i)
        dwout_l.append(dwout_i)
        grads[f"ln_g#{i}"], grads[f"ln_b#{i}"] = dg_i, db_i
        if i == 0:
            grads["a_w_s"], dbs_exp = pg
            grads["a_b_s"] = _bias_finalize(dbs_exp)
        elif i == 1:
            dlb, grads["b_norm_g"] = pg
            grads["hgrn_lb_logits"] = _lb_finalize(dlb, hgrn_lb_logits)
        elif i == 2:
            dwbd, grads["c_scale"] = pg
            grads["c_w_pool"] = jnp.stack([lax.dynamic_slice(dwbd, (g * POOL_GROUP, g * POOL_GROUP), (POOL_GROUP, POOL_GROUP))
                                           for g in range(4)])
        else:
            (grads["d_conv_w"], grads["d_conv_b"], grads["d_w_gx"], grads["d_b_gx"], grads["d_w_ga"], grads["d_b_ga"],
             grads["d_a_param"]) = pg
        ride = None
        if i == 0:
            dwkv = _kv_bwd(mem2, dks_l, dvs_l)
            smalls = small_of(0)
            ride = _Exchange([dwout_i, dwkv] + smalls, [True, True] + [False] * len(smalls))
        pb = _proj_bwd(i, dproj, dres, xs[i], gs[i], bs[i], wt_full[i], ride)
        up, dwt = pb[:2]
        if ride:
            recv_wo[0], recv_kv, recv_small[0] = pb[2], pb[3], pb[4:]
        dwt_l.append(dwt)
    grad_x = up[None]

    recv_wt[0], = _Exchange([dwt_l[-1]], [True]).run("scatter_last")

    outs = {}
    for t, n in enumerate(in_names):
        res = _sum_adam(f"adam_{n}", recv_wt[t], shard2d(W, t), shard2d(M, t), shard2d(V, t), flip[t])
        outs[n] = tuple((o if flip[t] else jnp.swapaxes(o, 0, 1))[None] for o in res)
    outs["w_out"] = tuple(_sum_adam_stacked("adam_w_out", recv_wo, w_out, m_w_out, v_w_out))
    outs["mem_kv_w"] = _sum_adam("adam_mem_kv_w", recv_kv, mem_kv_w, m_mem_kv_w, v_mem_kv_w, False)

    def entry_of(tree, e, like):
        if "#" in e:
            n, l = e.split("#")
            return tree[n][int(l):int(l) + 1]
        return tree[e].reshape(like.shape[1:])

    small = [{}, {}, {}, {}]
    for l in range(DEPTH):
        params = [None if (e == "loss" or e in sharded) else tuple(entry_of(t, e, r) for t in (W, M, V))
                  for e, r in zip(group[l], recv_small[l])]
        for e, res in zip(group[l], _group_adam(f"small_adam{l}", recv_small[l], params)):
            for j, a in enumerate(res):
                small[j][e] = a
    loss = small[0]["loss"][0, 0]
    for j in range(4):
        for n in ("ln_g", "ln_b"):
            small[j][n] = jnp.concatenate([small[j][f"{n}#{l}"] for l in range(DEPTH)], axis=0)
    g_small = small[0]
    for n, _, sh in SMALL:
        if not sh:
            outs[n] = tuple(small[j][n] for j in range(4))
    per = TOK // NDEV
    g_sh = {n: lax.dynamic_slice_in_dim(g_small[n], me * per, per, axis=1) for n, s, sh in SMALL if sh}
    gp = _pack([g_sh[n] for n in shard_names], 8)
    d_p, m_p, v_p = _adam_only(gp, small_shard, _pack([M[n] for n in shard_names], 8), _pack([V[n] for n in shard_names], 8))
    o = 0
    for n in shard_names:
        cnt = _size(W[n].shape)
        outs[n] = (g_sh[n],) + tuple(t.reshape(-1)[o:o + cnt].reshape(W[n].shape) for t in (d_p, m_p, v_p))
        o += cnt

    order = ["mem_kv_w", "ln_g", "ln_b", "w_out", "hgrn_lb_logits", "a_w_in", "a_w_s", "a_b_s", "b_w_in", "b_norm_g", "c_w_in",
             "c_w_pool", "c_scale", "d_w_in", "d_conv_w", "d_conv_b", "d_w_gx", "d_b_gx", "d_w_ga", "d_b_ga", "d_a_param"]
    result = [loss, grad_x]
    for j in range(4):
        result += [outs[n][j].reshape(W[n].shape) for n in order]
    return tuple(result)
```

```python
import functools

import jax
import jax.numpy as jnp
from jax import lax
from jax.experimental import pallas as pl
from jax.experimental.pallas import tpu as pltpu

F32 = jnp.float32
MXU = jnp.bfloat16
WIRE = jnp.bfloat16

D = 1024
TOK = 768
XW = 256
NMEM = 256
XHEADS = 4
XSCALE = 64 ** -0.5
NH = 6
HD = 128
CH = 16
POOL_WINDOWS = (2, 4, 8, 16)
POOL_GROUP = 192
DEPTH = 4
ALPHA = (2 * DEPTH) ** 0.25
LN_EPS = 1e-5
RMS_EPS = 1e-6
LRU_C = 8.0
B1, B2, LR, EPS, WD, STEP = 0.9, 0.999, 0.001, 1e-8, 0.01, 10

NDEV = 8
TS_FWD = {0: 512, 1: 256, 2: 512, 3: 256}
TS_BWD = {0: 512, 1: 256, 2: 512, 3: 256}
TSB = 512
VMEM_LIMIT = 58 * 1024 * 1024
VMEM_LIMIT_WIDE = 62 * 1024 * 1024

KIND_WIDTHS = {0: 2 * TOK + XW + D, 1: 3 * TOK + XW + D, 2: TOK + XW + D, 3: TOK + XW + D}


def _mm(a, b, ca, cb):
    return lax.dot_general(a.astype(MXU), b.astype(MXU), (((ca,), (cb,)), ((), ())), preferred_element_type=F32)


def _nn(a, b):
    return _mm(a, b, 1, 0)


def _nt(a, b):
    return _mm(a, b, 1, 1)


def _tn(a, b):
    return _mm(a, b, 0, 0)


def _bmm(a, b, ca, cb):
    return lax.dot_general(a.astype(MXU), b.astype(MXU), (((ca,), (cb,)), ((0,), (0,))), preferred_element_type=F32)


def _sigmoid(x):
    return 1.0 / (1.0 + jnp.exp(-x))


def _vjp1(fn, x, dy):
    return jax.vjp(fn, x)[1](dy)[0]


def _rowsum(x):
    return jnp.sum(x, axis=0, keepdims=True)


def _row(x, r):
    sel = lax.broadcasted_iota(jnp.int32, x.shape, 0) == r
    return jnp.sum(jnp.where(sel, x, 0.0), axis=0, keepdims=True)


def _acc(ref, val):
    ref[...] += val


def _cparams(sem=None, vmem=VMEM_LIMIT):
    return pltpu.CompilerParams(dimension_semantics=sem, vmem_limit_bytes=vmem)


def _res(a):
    nd = a.ndim
    return pl.BlockSpec(a.shape, lambda i: (0,) * nd)


def _res_sds(shape):
    nd = len(shape)
    return pl.BlockSpec(shape, lambda i: (0,) * nd)


def _row_spec(width, nt, rev, ts):
    if rev:
        return pl.BlockSpec((ts, width), lambda i: (nt - 1 - i, 0))
    return pl.BlockSpec((ts, width), lambda i: (i, 0))


def _call(body, name, grid, ins, outs, scratch=(), sem=("arbitrary",), vmem=VMEM_LIMIT):
    arrays = [a for a, _ in ins]
    return pl.pallas_call(
        body, name=name, grid=grid,
        in_specs=[s for _, s in ins],
        out_specs=[s for _, s in outs],
        out_shape=[o for o, _ in outs],
        scratch_shapes=list(scratch),
        compiler_params=_cparams(sem, vmem),
    )(*arrays)


def _xattn_fwd(qx, ks_ref, vs_ref):
    o = None
    ps = []
    for h in range(XHEADS):
        s = _nt(qx, ks_ref[h]) * XSCALE
        s = s - jnp.max(s, axis=-1, keepdims=True)
        e = jnp.exp(s)
        p = e * (1.0 / jnp.sum(e, axis=-1, keepdims=True))
        ps.append(p)
        oh = _nn(p, vs_ref[h])
        o = oh if o is None else o + oh
    return o, ps


def _xattn_bwd(qx, ps, dxo, ks_ref, vs_ref, dks_ref, dvs_ref):
    dq = None
    for h in range(XHEADS):
        p = ps[h]
        dp = _nt(dxo, vs_ref[h])
        ds = p * (dp - jnp.sum(dp * p, axis=-1, keepdims=True))
        dqh = _nn(ds, ks_ref[h]) * XSCALE
        dq = dqh if dq is None else dq + dqh
        dks_ref[h] += _tn(ds, qx) * XSCALE
        dvs_ref[h] += _tn(p, dxo)
    return dq


def _tril128():
    r = lax.broadcasted_iota(jnp.int32, (HD, HD), 0)
    c = lax.broadcasted_iota(jnp.int32, (HD, HD), 1)
    return c <= r


GELU_C = 0.7978845608028654
GELU_K = 0.044715


def _gelu(x):
    th = jnp.tanh(GELU_C * (x + GELU_K * (x * x * x)))
    return 0.5 * x * (1.0 + th), th


def _gelu_grad(x, th):
    return 0.5 * (1.0 + th) + 0.5 * x * (1.0 - th * th) * (GELU_C * (1.0 + 3.0 * GELU_K * (x * x)))


def _gmlp_fwd(u, v, ws_ref, bs_ref):
    ts = u.shape[0]
    ug, thu = _gelu(u)
    vg, thv = _gelu(v)
    tri = _tril128()
    toks, res = [], []
    for g in range(NH):
        sl = slice(g * HD, (g + 1) * HD)
        vgh = vg[:, sl]
        cen = vgh - jnp.mean(vgh, axis=-1, keepdims=True)
        rstd = lax.rsqrt(jnp.mean(cen * cen, axis=-1, keepdims=True) + LN_EPS)
        vn = cen * rstd
        w = jnp.where(tri, ws_ref[g], 0.0).astype(MXU)
        mix = jnp.concatenate([_nn(w, vn[n * HD:(n + 1) * HD]) + bs_ref[g] for n in range(ts // HD)], axis=0)
        toks.append(ug[:, sl] * mix)
        res.append((vn, rstd, mix, w))
    return jnp.concatenate(toks, axis=1), (ug, res, thu, thv)


def _gmlp_bwd(u, v, fres, dtok, dws_ref, dbs_ref):
    ts = u.shape[0]
    ug, res, thu, thv = fres
    tri = _tril128()
    dugs, dvgs = [], []
    for g in range(NH):
        sl = slice(g * HD, (g + 1) * HD)
        vn, rstd, mix, w = res[g]
        dmix = dtok[:, sl] * ug[:, sl]
        dugs.append(dtok[:, sl] * mix)
        dvn_rows = []
        dw = None
        dbs = None
        for n in range(ts // HD):
            dm = dmix[n * HD:(n + 1) * HD]
            dvn_rows.append(_tn(w, dm))
            t = _nt(dm, vn[n * HD:(n + 1) * HD])
            dw = t if dw is None else dw + t
            dbs = dm if dbs is None else dbs + dm
        dws_ref[g] += jnp.where(tri, dw, 0.0)
        dbs_ref[g] += dbs
        dvn = jnp.concatenate(dvn_rows, axis=0)
        dvgs.append(rstd * (dvn - jnp.mean(dvn, axis=-1, keepdims=True) - vn * jnp.mean(dvn * vn, axis=-1, keepdims=True)))
    du = jnp.concatenate(dugs, axis=1) * _gelu_grad(u, thu)
    dv = jnp.concatenate(dvgs, axis=1) * _gelu_grad(v, thv)
    return du, dv


def _chunk_cumsum(x):
    row = lax.broadcasted_iota(jnp.int32, x.shape, 0) % CH
    for s in (1, 2, 4, 8):
        x = x + jnp.where(row >= s, pltpu.roll(x, s, 0), 0.0)
    return x


def _chunk_revcumsum(x):
    n = x.shape[0]
    row = lax.broadcasted_iota(jnp.int32, x.shape, 0) % CH
    for s in (1, 2, 4, 8):
        x = x + jnp.where(row < CH - s, pltpu.roll(x, n - s, 0), 0.0)
    return x


def _chunk_sum(x):
    n, w = x.shape
    return jnp.sum(x.reshape(n // CH, CH, w), axis=1)


def _chunk_bcast(c, n):
    nch, w = c.shape
    return jnp.broadcast_to(c[:, None, :], (nch, CH, w)).reshape(n, w)


def _lower_bound(lb_logits, layer):
    lg = lb_logits
    e = jnp.exp(lg - jnp.max(lg, axis=0, keepdims=True))
    p = e / jnp.sum(e, axis=0, keepdims=True)
    row = lax.broadcasted_iota(jnp.int32, p.shape, 0)
    lb = jnp.sum(jnp.where((row >= 1) & (row <= layer), p, 0.0), axis=0, keepdims=True)
    return lb, p


def _hgrn_prep(q, fl, lb):
    n = q.shape[0]
    sg = _sigmoid(fl)
    f = lb + (1.0 - lb) * sg
    lf = jnp.log(f)
    sq = _sigmoid(q)
    g = _chunk_cumsum(lf)
    tot = _chunk_sum(lf)
    gl = _chunk_bcast(tot, n)
    eg = jnp.exp(g)
    eng = jnp.exp(-g)
    egl = jnp.exp(gl - g)
    k = 1.0 - f
    qf = q * sq
    return dict(sg=sg, f=f, k=k, sq=sq, qf=qf, eg=eg, eng=eng, egl=egl,
                qd=qf * eg, ki=k * eng, ke=k * egl, dch=jnp.exp(tot))


def _hgrn_mask():
    r = lax.broadcasted_iota(jnp.int32, (HD, HD), 0)
    c = lax.broadcasted_iota(jnp.int32, (HD, HD), 1)
    return (r // CH == c // CH) & (c <= r)


def _hgrn_states(v3, ke3, dch_h, st_in):
    nch = v3.shape[0]
    ut = _bmm(v3, ke3, 1, 1)
    dfull = jnp.broadcast_to(dch_h[:, None, :], (nch, HD, HD))
    st, sts = st_in, []
    for c in range(nch):
        sts.append(st)
        st = st * dfull[c] + ut[c]
    return jnp.stack(sts), st, dfull


def _hgrn_fwd(q, fl, inp, lb, ng, st_ref, sts_ref):
    n = q.shape[0]
    nch = n // CH
    pr = _hgrn_prep(q, fl, lb)
    mask = _hgrn_mask()
    toks, o_l, a_l = [], [], []
    qd_m, ki_m, ke_m, v_m = (t.astype(MXU) for t in (pr["qd"], pr["ki"], pr["ke"], inp))
    for h in range(NH):
        sl = slice(h * HD, (h + 1) * HD)
        qd, ki, ke, v = qd_m[:, sl], ki_m[:, sl], ke_m[:, sl], v_m[:, sl]
        qd3 = qd.reshape(nch, CH, HD)
        v3 = v.reshape(nch, CH, HD)
        ke3 = ke.reshape(nch, CH, HD)
        sts, st_ref[h], _ = _hgrn_states(v3, ke3, pr["dch"][:, sl], st_ref[h])
        sts_ref[h] = sts
        o = _bmm(qd3, sts, 2, 2).reshape(n, HD)
        intra, scores = [], []
        for b in range(n // HD):
            bs = slice(b * HD, (b + 1) * HD)
            a = jnp.where(mask, _nt(qd[bs], ki[bs]), 0.0).astype(MXU)
            scores.append(a)
            intra.append(_nn(a, v[bs]))
        o = o + jnp.concatenate(intra, axis=0)
        r = lax.rsqrt(jnp.mean(o * o, axis=-1, keepdims=True) + RMS_EPS)
        toks.append(o * r * ng[:, sl])
        o_l.append(o)
        a_l.append(jnp.concatenate(scores, axis=0))
    return jnp.concatenate(toks, axis=1), jnp.concatenate(o_l, axis=1), jnp.concatenate(a_l, axis=1)


def _hgrn_bwd(q, pr, inp, lb, ng, dtok, o_all, a_all, ststart_ref, dst_ref, dng_ref, dlb_ref):
    n = q.shape[0]
    nch = n // CH
    mask = _hgrn_mask()
    dqd_l, dki_l, dke_l, dv_l, ddch_l, dng_l, toks = [], [], [], [], [], [], []
    qd_m, ki_m, ke_m, v_m = (t.astype(MXU) for t in (pr["qd"], pr["ki"], pr["ke"], inp))
    for h in range(NH):
        sl = slice(h * HD, (h + 1) * HD)
        qd, ki, ke, v = qd_m[:, sl], ki_m[:, sl], ke_m[:, sl], v_m[:, sl]
        qd3 = qd.reshape(nch, CH, HD)
        v3 = v.reshape(nch, CH, HD)
        ke3 = ke.reshape(nch, CH, HD)
        sts = ststart_ref[h]
        dfull = jnp.broadcast_to(pr["dch"][:, sl][:, None, :], (nch, HD, HD))
        sts_m = sts.astype(MXU)
        o = o_all[:, sl]
        a_l = [a_all[b * HD:(b + 1) * HD, sl] for b in range(n // HD)]
        r = lax.rsqrt(jnp.mean(o * o, axis=-1, keepdims=True) + RMS_EPS)
        toks.append(o * r * ng[:, sl])
        dt = dtok[:, sl]
        dng_l.append(_rowsum(dt * o * r))
        dn = dt * ng[:, sl]
        do = r * dn - o * (r * r * r) * jnp.mean(dn * o, axis=-1, keepdims=True)
        do_m = do.astype(MXU)
        do3 = do_m.reshape(nch, CH, HD)
        dqd_rows, dki_rows, dv_rows = [], [], []
        for b in range(n // HD):
            bs = slice(b * HD, (b + 1) * HD)
            da = jnp.where(mask, _nt(do_m[bs], v[bs]), 0.0).astype(MXU)
            dqd_rows.append(_nn(da, ki[bs]))
            dki_rows.append(_tn(da, qd[bs]))
            dv_rows.append(_tn(a_l[b], do_m[bs]))
        dqd = jnp.concatenate(dqd_rows, axis=0) + _bmm(do3, sts_m, 2, 1).reshape(n, HD)
        dki = jnp.concatenate(dki_rows, axis=0)
        dv = jnp.concatenate(dv_rows, axis=0)
        wt = _bmm(do3, qd3, 1, 1)
        dst, dstn_l = dst_ref[h], [None] * nch
        for c in reversed(range(nch)):
            dstn_l[c] = dst
            dst = wt[c] + dst * dfull[c]
        dst_ref[h] = dst
        dstn = jnp.stack(dstn_l)
        dstn_m = dstn.astype(MXU)
        dv = dv + _bmm(ke3, dstn_m, 2, 2).reshape(n, HD)
        dke = _bmm(v3, dstn_m, 2, 1).reshape(n, HD)
        ddch_l.append(jnp.sum(sts * dstn, axis=1))
        dqd_l.append(dqd)
        dki_l.append(dki)
        dke_l.append(dke)
        dv_l.append(dv)
    dqd = jnp.concatenate(dqd_l, axis=1)
    dki = jnp.concatenate(dki_l, axis=1)
    dke = jnp.concatenate(dke_l, axis=1)
    dinp = jnp.concatenate(dv_l, axis=1)
    ddch = jnp.concatenate(ddch_l, axis=1)
    _acc(dng_ref, jnp.concatenate(dng_l, axis=1))
    dqf = dqd * pr["eg"]
    dke_ke = dke * pr["ke"]
    dg = dqd * pr["qd"] - dki * pr["ki"] - dke_ke
    dk = dki * pr["eng"] + dke * pr["egl"]
    dgl = _chunk_sum(dke_ke) + ddch * pr["dch"]
    dlf = _chunk_revcumsum(dg) + _chunk_bcast(dgl, n)
    df = dlf / pr["f"] - dk
    sg = pr["sg"]
    dfl = df * (1.0 - lb) * sg * (1.0 - sg)
    _acc(dlb_ref, _rowsum(df * (1.0 - sg)))
    sq = pr["sq"]
    dq = dqf * (sq * (1.0 + q * (1.0 - sq)))
    return jnp.concatenate(toks, axis=1), dq, dfl, dinp


def _pool_select(s2, s4, s8, s16):
    col = lax.broadcasted_iota(jnp.int32, (1, TOK), 1)
    return jnp.where(col < POOL_GROUP, s2, jnp.where(col < 2 * POOL_GROUP, s4, jnp.where(col < 3 * POOL_GROUP, s8, s16)))


def _pool_cnt(pos0, n):
    pos = pos0 + lax.broadcasted_iota(jnp.int32, (n, TOK), 0) + 1
    col = lax.broadcasted_iota(jnp.int32, (n, TOK), 1)
    w = jnp.where(col < POOL_GROUP, 2, jnp.where(col < 2 * POOL_GROUP, 4, jnp.where(col < 3 * POOL_GROUP, 8, 16)))
    return jnp.minimum(pos, w).astype(F32)


def _pool_fwd(p, halo, pos0, wbd, scale):
    n = p.shape[0]
    ext = jnp.concatenate([halo, p], axis=0)
    s2 = ext + pltpu.roll(ext, 1, 0)
    s4 = s2 + pltpu.roll(s2, 2, 0)
    s8 = s4 + pltpu.roll(s4, 4, 0)
    s16 = s8 + pltpu.roll(s8, 8, 0)
    win = _pool_select(s2, s4, s8, s16)[16:]
    cnt = _pool_cnt(pos0, n)
    diff = win / cnt - p
    y = _nn(diff, wbd)
    return y * scale, (diff, y, cnt)


def _pool_bwd(fres, dtok, nxt_ref, wbd, scale, dwbd_ref, dscale_ref):
    diff, y, cnt = fres
    n = diff.shape[0]
    _acc(dscale_ref, _rowsum(dtok * y))
    dy = dtok * scale
    ddiff = _nt(dy, wbd)
    dwbd_ref[...] += _tn(diff, dy)
    qv = ddiff / cnt
    ext = jnp.concatenate([qv, nxt_ref[...]], axis=0)
    m = n + 16
    s2 = ext + pltpu.roll(ext, m - 1, 0)
    s4 = s2 + pltpu.roll(s2, m - 2, 0)
    s8 = s4 + pltpu.roll(s4, m - 4, 0)
    s16 = s8 + pltpu.roll(s8, m - 8, 0)
    adj = _pool_select(s2, s4, s8, s16)[:n]
    nxt_ref[...] = qv[:16]
    return adj - ddiff


def _neg_expm1(x):
    return jnp.where(jnp.abs(x) < 1e-2, -x * (1.0 + x * (0.5 + x * (1.0 / 6.0))), 1.0 - jnp.exp(x))


def _softplus_neg(ap):
    return jnp.maximum(-ap, 0.0) + jnp.log(1.0 + jnp.exp(-jnp.abs(ap)))


def _lru_gates(xc, zx, za, ap, first):
    gx = _sigmoid(zx)
    ga = _sigmoid(za)
    sp = _softplus_neg(ap)
    log_a = -LRU_C * ga * sp
    a = jnp.exp(log_a)
    mult = jnp.sqrt(_neg_expm1(2.0 * log_a))
    mult = jnp.where(first, 1.0, mult)
    return a, mult * gx * xc, (gx, ga, sp, mult)


def _scan_fwd(a, b, h0):
    n = a.shape[0]
    row = lax.broadcasted_iota(jnp.int32, a.shape, 0)
    s = 1
    while s < n:
        keep = row >= s
        b = b + a * jnp.where(keep, pltpu.roll(b, s, 0), 0.0)
        a = a * jnp.where(keep, pltpu.roll(a, s, 0), 1.0)
        s *= 2
    return b + a * h0


def _scan_bwd(an, d, dh_next):
    n = an.shape[0]
    row = lax.broadcasted_iota(jnp.int32, an.shape, 0)
    s = 1
    while s < n:
        keep = row < n - s
        d = d + an * jnp.where(keep, pltpu.roll(d, n - s, 0), 0.0)
        an = an * jnp.where(keep, pltpu.roll(an, n - s, 0), 1.0)
        s *= 2
    return d + an * dh_next


def _lru_conv(xb, halo, cw_ref, cb):
    ext = jnp.concatenate([halo, xb], axis=0)
    sh = [pltpu.roll(ext, 3 - j, 0)[8:] if j < 3 else xb for j in range(4)]
    xc = cb
    for j in range(4):
        xc = xc + cw_ref[pl.ds(j, 1), :] * sh[j]
    return xc, sh


def _lru_fwd(xb, halo, pos0, prm, h0):
    cw, cb, wgx, bgx, wga, bga, ap = prm
    n = xb.shape[0]
    xc, sh = _lru_conv(xb, halo, cw, cb[...])
    zx = jnp.concatenate([_nn(xc[:, h * HD:(h + 1) * HD], wgx[h]) for h in range(NH)], axis=1) + bgx[...]
    za = jnp.concatenate([_nn(xc[:, h * HD:(h + 1) * HD], wga[h]) for h in range(NH)], axis=1) + bga[...]
    first = (pos0 + lax.broadcasted_iota(jnp.int32, (n, 1), 0)) == 0
    a, b, gates = _lru_gates(xc, zx, za, ap[...], first)
    hseq = _scan_fwd(a, b, h0)
    return hseq, (xc, sh, gates, first, a)


def _lru_bwd(fres, hseq, h0, dtok, prm, carry_refs, grad_refs):
    cw, cb, wgx, bgx, wga, bga, ap = prm
    xc, sh, (gx, ga, sp, mult), first, a = fres
    anext_ref, dhnext_ref, dxcnext_ref = carry_refs
    dcw_ref, dcb_ref, dwgx_ref, dbgx_ref, dwga_ref, dbga_ref, dap_ref = grad_refs
    n = xc.shape[0]
    an = jnp.where(lax.broadcasted_iota(jnp.int32, a.shape, 0) == n - 1, anext_ref[...], pltpu.roll(a, n - 1, 0))
    dh = _scan_bwd(an, dtok, dhnext_ref[...])
    hprev = jnp.where(lax.broadcasted_iota(jnp.int32, hseq.shape, 0) == 0, h0, pltpu.roll(hseq, 1, 0))
    da = dh * hprev
    anext_ref[...] = _row(a, 0)
    dhnext_ref[...] = _row(dh, 0)
    t = dh * xc
    dxc = dh * mult * gx
    dzx = t * mult * gx * (1.0 - gx)
    dlog_a = da * a - jnp.where(first, 0.0, t * gx * (a * a) / mult)
    dza = dlog_a * (-LRU_C * sp) * ga * (1.0 - ga)
    dap = _rowsum(dlog_a * ga) * (LRU_C * _sigmoid(-ap[...]))
    _acc(dap_ref, dap)
    _acc(dbgx_ref, _rowsum(dzx))
    _acc(dbga_ref, _rowsum(dza))
    parts = []
    for h in range(NH):
        sl = slice(h * HD, (h + 1) * HD)
        parts.append(_nt(dzx[:, sl], wgx[h]) + _nt(dza[:, sl], wga[h]))
        dwgx_ref[h] += _tn(xc[:, sl], dzx[:, sl])
        dwga_ref[h] += _tn(xc[:, sl], dza[:, sl])
    dxc = dxc + jnp.concatenate(parts, axis=1)
    _acc(dcb_ref, _rowsum(dxc))
    for j in range(4):
        dcw_ref[pl.ds(j, 1), :] += _rowsum(dxc * sh[j])
    ext = jnp.concatenate([dxc, dxcnext_ref[...]], axis=0)
    m = n + 8
    dxb = cw[pl.ds(3, 1), :] * dxc
    for j in range(3):
        dxb = dxb + cw[pl.ds(j, 1), :] * pltpu.roll(ext, m - (3 - j), 0)[:n]
    dxcnext_ref[...] = dxc[:8]
    return dxb


def _layer_fwd(kind, layer, xprev, gprev, bprev, wt, wout, ks, vs, prm, ride=None):
    TS = TS_FWD[kind]
    _rows = functools.partial(_row_spec, ts=TS)
    S = xprev.shape[0]
    nt = S // TS
    N = wt.shape[0]
    nprm = len(prm)
    nch = TS // CH

    outs = [(jax.ShapeDtypeStruct((S, N), F32), _rows(N, nt, False)),
            (jax.ShapeDtypeStruct((S, D), F32), _rows(D, nt, False)),
            (jax.ShapeDtypeStruct((S, 1), F32), _rows(1, nt, False)),
            (jax.ShapeDtypeStruct((S, XHEADS * NMEM), MXU), _rows(XHEADS * NMEM, nt, False))]
    scratch = []
    if kind == 1:
        outs.append((jax.ShapeDtypeStruct((nt, NH, nch, HD, HD), F32),
                     pl.BlockSpec((None, NH, nch, HD, HD), lambda i: (i, 0, 0, 0, 0))))
        outs.append((jax.ShapeDtypeStruct((S, TOK), F32), _rows(TOK, nt, False)))
        outs.append((jax.ShapeDtypeStruct((S, TOK), MXU), _rows(TOK, nt, False)))
        scratch = [pltpu.VMEM((NH, HD, HD), F32)]
    elif kind == 2:
        outs += [(jax.ShapeDtypeStruct((S, TOK), F32), _rows(TOK, nt, False))] * 2
        scratch = [pltpu.VMEM((16, TOK), F32)]
    elif kind == 3:
        outs.append((jax.ShapeDtypeStruct((nt * 8, TOK), F32), pl.BlockSpec((8, TOK), lambda i: (i, 0))))
        outs += [(jax.ShapeDtypeStruct((S, TOK), F32), _rows(TOK, nt, False))] * 4
        scratch = [pltpu.VMEM((8, TOK), F32), pltpu.VMEM((1, TOK), F32)]
    outs.append((jax.ShapeDtypeStruct((S, XW), F32), _rows(XW, nt, False)))
    nout = len(outs)
    nscr = len(scratch)
    nride = len(ride.arrays) if ride else 0

    def body(*refs):
        x_ref, g_ref, b_ref, wt_ref, wout_ref, ks_ref, vs_ref = refs[:7]
        prm_refs = refs[7:7 + nprm]
        nin = 7 + nprm + nride
        ride_src = refs[7 + nprm:nin]
        out_refs = refs[nin:nin + nout]
        ride_dst = refs[nin + nout:nin + nout + nride]
        scr = refs[nin + nout + nride:nin + nout + nride + nscr]
        ride_sems = refs[nin + nout + nride + nscr:]
        proj_ref, xhat_ref, rstd_ref = out_refs[:3]
        i = pl.program_id(0)
        if ride:
            @pl.when(i == 0)
            def _():
                ride.start(ride_src, ride_dst, ride_sems)

        xin = x_ref[...] * g_ref[...] + b_ref[...]
        proj = _nt(xin, wt_ref[...])
        proj_ref[...] = proj
        if kind == 0:
            tok, _ = _gmlp_fwd(proj[:, :TOK], proj[:, TOK:2 * TOK], prm_refs[0], prm_refs[1])
        elif kind == 1:
            st_ref, = scr

            @pl.when(i == 0)
            def _():
                st_ref[...] = jnp.zeros_like(st_ref)

            lb, _ = _lower_bound(prm_refs[0][...], layer)
            tok, out_refs[5][...], out_refs[6][...] = _hgrn_fwd(proj[:, :TOK], proj[:, TOK:2 * TOK], proj[:, 2 * TOK:3 * TOK],
                                                                lb, prm_refs[1][...], st_ref, out_refs[4])
        elif kind == 2:
            halo_ref, = scr

            @pl.when(i == 0)
            def _():
                halo_ref[...] = jnp.zeros_like(halo_ref)

            p = proj[:, :TOK]
            tok, (diff, y, _) = _pool_fwd(p, halo_ref[...], i * TS, prm_refs[0][...], prm_refs[1][...])
            out_refs[4][...] = diff
            out_refs[5][...] = y
            halo_ref[...] = p[TS - 16:]
        else:
            halo_ref, h_ref = scr

            @pl.when(i == 0)
            def _():
                halo_ref[...] = jnp.zeros_like(halo_ref)
                h_ref[...] = jnp.zeros_like(h_ref)

            out_refs[4][...] = jnp.broadcast_to(h_ref[...], (8, TOK))
            xb = proj[:, :TOK]
            tok, fres = _lru_fwd(xb, halo_ref[...], i * TS, prm_refs, h_ref[...])
            gx, ga, _, mult = fres[2]
            for r, val in zip(out_refs[5:9], (tok, gx, ga, mult)):
                r[...] = val
            halo_ref[...] = xb[TS - 8:]
            h_ref[...] = _row(tok, TS - 1)
        qx = proj[:, N - D - XW:N - D]
        gate = proj[:, N - D:]
        xo, ps = _xattn_fwd(qx, ks_ref, vs_ref)
        out_refs[3][...] = jnp.concatenate(ps, axis=1).astype(MXU)
        out_refs[nout - 1][...] = xo
        mixed = jnp.concatenate([tok, xo], axis=1) * (gate * _sigmoid(gate))
        z = ALPHA * xin + _nn(mixed, wout_ref[...])
        cen = z - jnp.mean(z, axis=-1, keepdims=True)
        rstd = lax.rsqrt(jnp.mean(cen * cen, axis=-1, keepdims=True) + LN_EPS)
        xhat_ref[...] = cen * rstd
        rstd_ref[...] = rstd
        if ride:
            @pl.when(i == nt - 1)
            def _():
                ride.wait(ride_src, ride_dst, ride_sems)

    ins = [(xprev, _rows(D, nt, False)), (gprev, _res(gprev)), (bprev, _res(bprev)), (wt, _res(wt)), (wout, _res(wout)),
           (ks, _res(ks)), (vs, _res(vs))] + [(p, _res(p)) for p in prm]
    if ride:
        ins += [(a, _ANY) for a in ride.arrays]
        outs += [(s, _ANY) for s in ride.out_shapes]
        scratch = scratch + ride.scratch
    return _call(body, f"layer{layer}_fwd", (nt,), ins, outs, scratch)


def _layer_bwd(kind, layer, up, is_last, xhat, rstd, probs, xo_saved, g_i, b_i, proj, wout, ks, vs, prm, extra, ride=None):
    TS = TS_BWD[kind]
    _rows = functools.partial(_row_spec, ts=TS)
    S = xhat.shape[0]
    nt = S // TS
    N = proj.shape[1]
    nprm = len(prm)
    nch = TS // CH

    ins = [(up, _rows(D, nt, True)), (xhat, _rows(D, nt, True)), (rstd, _rows(1, nt, True)), (g_i, _res(g_i)), (b_i, _res(b_i)),
           (proj, _rows(N, nt, True)), (wout, _res(wout)), (ks, _res(ks)), (vs, _res(vs)),
           (probs, _rows(XHEADS * NMEM, nt, True)), (xo_saved, _rows(XW, nt, True))] + [(p, _res(p)) for p in prm]
    nfixed = 11
    if kind == 1:
        ins.append((extra[0], pl.BlockSpec((None, NH, nch, HD, HD), lambda i: (nt - 1 - i, 0, 0, 0, 0))))
        ins += [(e, _rows(TOK, nt, True)) for e in extra[1:]]
    elif kind == 2:
        ins += [(e, _rows(TOK, nt, True)) for e in extra]
    elif kind == 3:
        hb = TS // 8
        ins.append((proj, pl.BlockSpec((8, TOK), lambda i: (jnp.maximum((nt - 1 - i) * hb - 1, 0), 0))))
        ins.append((extra[0], pl.BlockSpec((8, TOK), lambda i: (nt - 1 - i, 0))))
        ins += [(e, _rows(TOK, nt, True)) for e in extra[1:]]
    nin = len(ins)

    def acc(shape):
        return (jax.ShapeDtypeStruct(shape, F32), _res_sds(shape))

    outs = [(jax.ShapeDtypeStruct((S, D), F32), _rows(D, nt, True)),
            (jax.ShapeDtypeStruct((S, N), MXU), _rows(N, nt, True)),
            (jax.ShapeDtypeStruct((D, D), WIRE), _res_sds((D, D))),
            acc((XHEADS, NMEM, XW)), acc((XHEADS, NMEM, XW)), acc((1, D)), acc((1, D)), acc((1, HD))]
    scratch = []
    if kind == 0:
        outs += [acc((NH, HD, HD)), acc((NH, HD, HD))]
    elif kind == 1:
        outs += [acc((1, TOK)), acc((1, TOK))]
        scratch = [pltpu.VMEM((NH, HD, HD), F32)]
    elif kind == 2:
        outs += [acc((TOK, TOK)), acc((1, TOK))]
        scratch = [pltpu.VMEM((16, TOK), F32)]
    else:
        outs += [acc((4, TOK)), acc((1, TOK)), acc((NH, HD, HD)), acc((1, TOK)), acc((NH, HD, HD)), acc((1, TOK)), acc((1, TOK))]
        scratch = [pltpu.VMEM((1, TOK), F32), pltpu.VMEM((1, TOK), F32), pltpu.VMEM((8, TOK), F32)]
    scratch = scratch + [pltpu.VMEM((D, D), F32)]
    nout = len(outs)
    nscr = len(scratch)
    nride = len(ride.arrays) if ride else 0

    def body(*refs):
        up_ref, xhat_ref, rstd_ref, g_ref, b_ref, proj_ref, wout_ref, ks_ref, vs_ref, probs_ref, xo_ref = refs[:nfixed]
        prm_refs = refs[nfixed:nfixed + nprm]
        ext_refs = refs[nfixed + nprm:nin]
        ride_src = refs[nin:nin + nride]
        o0 = nin + nride
        out_refs = refs[o0:o0 + nout]
        ride_dst = refs[o0 + nout:o0 + nout + nride]
        scr = refs[o0 + nout + nride:o0 + nout + nride + nscr - 1]
        dwout_acc = refs[o0 + nout + nride + nscr - 1]
        ride_sems = refs[o0 + nout + nride + nscr:]
        dres_ref, dproj_ref, dwout_ref, dks_ref, dvs_ref, dg_ref, db_ref, loss_ref = out_refs[:8]
        pgrad = out_refs[8:]
        i = pl.program_id(0)
        tile = nt - 1 - i

        @pl.when(i == 0)
        def _():
            if ride:
                ride.start(ride_src, ride_dst, ride_sems)
            for r in out_refs[3:]:
                r[...] = jnp.zeros_like(r)
            dwout_acc[...] = jnp.zeros_like(dwout_acc)
            for r in scr:
                if kind != 1 or r is scr[0]:
                    r[...] = jnp.zeros_like(r)

        xhat_v = xhat_ref[...]
        if is_last:
            err = xhat_v * g_ref[...] + b_ref[...] - up_ref[...]
            dxo = err * (1.0 / D)
            loss_ref[...] += jnp.sum(0.5 * jnp.mean(err * err, axis=-1, keepdims=True), axis=0, keepdims=True)
        else:
            dxo = up_ref[...]
        _acc(dg_ref, _rowsum(dxo * xhat_v))
        _acc(db_ref, _rowsum(dxo))
        dxh = dxo * g_ref[...]
        dz = rstd_ref[...] * (dxh - jnp.mean(dxh, axis=-1, keepdims=True)
                              - xhat_v * jnp.mean(dxh * xhat_v, axis=-1, keepdims=True))
        dres_ref[...] = ALPHA * dz

        proj = proj_ref[...]
        qx = proj[:, N - D - XW:N - D]
        gate = proj[:, N - D:]
        sgate = _sigmoid(gate)
        silu = gate * sgate
        dmixed = _nt(dz, wout_ref[...])
        dcat = dmixed * silu
        dtok = dcat[:, :TOK]
        if kind == 0:
            u, v = proj[:, :TOK], proj[:, TOK:2 * TOK]
            tok, fres = _gmlp_fwd(u, v, prm_refs[0], prm_refs[1])
            du, dv = _gmlp_bwd(u, v, fres, dtok, pgrad[0], pgrad[1])
            dproj_ref[:, :TOK] = du.astype(MXU)
            dproj_ref[:, TOK:2 * TOK] = dv.astype(MXU)
        elif kind == 1:
            dst_ref, = scr
            lb, _ = _lower_bound(prm_refs[0][...], layer)
            q, fl, inp = proj[:, :TOK], proj[:, TOK:2 * TOK], proj[:, 2 * TOK:3 * TOK]
            pr = _hgrn_prep(q, fl, lb)
            tok, dq, dfl, dinp = _hgrn_bwd(q, pr, inp, lb, prm_refs[1][...], dtok, ext_refs[1][...], ext_refs[2][...],
                                           ext_refs[0], dst_ref, pgrad[1], pgrad[0])
            dproj_ref[:, :TOK] = dq.astype(MXU)
            dproj_ref[:, TOK:2 * TOK] = dfl.astype(MXU)
            dproj_ref[:, 2 * TOK:3 * TOK] = dinp.astype(MXU)
        elif kind == 2:
            diff, y = ext_refs[0][...], ext_refs[1][...]
            tok = y * prm_refs[1][...]
            fres = (diff, y, _pool_cnt(tile * TS, TS))
            dp = _pool_bwd(fres, dtok, scr[0], prm_refs[0][...], prm_refs[1][...], pgrad[0], pgrad[1])
            dproj_ref[:, :TOK] = dp.astype(MXU)
        else:
            xb = proj[:, :TOK]
            halo = jnp.where(tile == 0, 0.0, ext_refs[0][...])
            h0 = ext_refs[1][0:1]
            tok, gx, ga, mult = (r[...] for r in ext_refs[2:6])
            xc, sh = _lru_conv(xb, halo, prm_refs[0], prm_refs[1][...])
            sp = _softplus_neg(prm_refs[6][...])
            first = (tile * TS + lax.broadcasted_iota(jnp.int32, (TS, 1), 0)) == 0
            fres = (xc, sh, (gx, ga, sp, mult), first, jnp.exp(-LRU_C * ga * sp))
            dxb = _lru_bwd(fres, tok, h0, dtok, prm_refs, scr, pgrad)
            dproj_ref[:, :TOK] = dxb.astype(MXU)
        ps = [probs_ref[:, h * NMEM:(h + 1) * NMEM].astype(F32) for h in range(XHEADS)]
        xo = xo_ref[...]
        dqx = _xattn_bwd(qx, ps, dcat[:, TOK:], ks_ref, vs_ref, dks_ref, dvs_ref)
        cat = jnp.concatenate([tok, xo], axis=1)
        dwout_acc[...] += _tn(cat * silu, dz)
        dgate = dmixed * cat * (sgate * (1.0 + gate * (1.0 - sgate)))
        dproj_ref[:, N - D - XW:N - D] = dqx.astype(MXU)
        dproj_ref[:, N - D:] = dgate.astype(MXU)

        @pl.when(i == nt - 1)
        def _():
            dwout_ref[...] = dwout_acc[...].astype(WIRE)
            if ride:
                ride.wait(ride_src, ride_dst, ride_sems)

    if ride:
        ins += [(a, _ANY) for a in ride.arrays]
        outs += [(s, _ANY) for s in ride.out_shapes]
        scratch = scratch + ride.scratch
    return _call(body, f"layer{layer}_bwd", (nt,), ins, outs, scratch, vmem=VMEM_LIMIT_WIDE if kind == 0 else VMEM_LIMIT)


def _proj_bwd(layer, dproj, dres, xprev, gprev, bprev, wt, ride=None):
    S = xprev.shape[0]
    nt = S // TSB
    N = wt.shape[0]

    nride = len(ride.arrays) if ride else 0

    def body(*refs):
        dproj_ref, dres_ref, x_ref, g_ref, b_ref, wt_ref = refs[:6]
        ride_src = refs[6:6 + nride]
        dx_ref, dwt_ref = refs[6 + nride:8 + nride]
        ride_dst = refs[8 + nride:8 + 2 * nride]
        acc_ref = refs[8 + 2 * nride]
        ride_sems = refs[9 + 2 * nride:]

        @pl.when(pl.program_id(0) == 0)
        def _():
            if ride:
                ride.start(ride_src, ride_dst, ride_sems)
            acc_ref[...] = jnp.zeros_like(acc_ref)

        dp = dproj_ref[...]
        xin = x_ref[...] * g_ref[...] + b_ref[...]
        dx_ref[...] = dres_ref[...] + _nn(dp, wt_ref[...])
        acc_ref[...] += _tn(dp, xin)

        @pl.when(pl.program_id(0) == nt - 1)
        def _():
            dwt_ref[...] = acc_ref[...].astype(WIRE)
            if ride:
                ride.wait(ride_src, ride_dst, ride_sems)

    ins = [(dproj, _row_spec(N, nt, False, TSB)), (dres, _row_spec(D, nt, False, TSB)), (xprev, _row_spec(D, nt, False, TSB)),
           (gprev, _res(gprev)), (bprev, _res(bprev)), (wt, _res(wt))]
    outs = [(jax.ShapeDtypeStruct((S, D), F32), _row_spec(D, nt, False, TSB)),
            (jax.ShapeDtypeStruct((N, D), WIRE), _res_sds((N, D)))]
    scratch = [pltpu.VMEM((N, D), F32)]
    if ride:
        ins += [(a, _ANY) for a in ride.arrays]
        outs += [(s, _ANY) for s in ride.out_shapes]
        scratch = scratch + ride.scratch
    return _call(body, f"layer{layer}_projbwd", (nt,), ins, outs, scratch)


def _head_mask(h):
    col = lax.broadcasted_iota(jnp.int32, (1, XW), 1)
    return (col // 64) == h


def _kv_fwd(mem, wkv):
    def body(mem_ref, w_ref, ks_ref, vs_ref):
        kv = _nn(mem_ref[...], w_ref[...])
        k, v = kv[:, :XW], kv[:, XW:]
        for h in range(XHEADS):
            ks_ref[h] = jnp.where(_head_mask(h), k, 0.0).astype(MXU)
            vs_ref[h] = jnp.where(_head_mask(h), v, 0.0).astype(MXU)

    sds = jax.ShapeDtypeStruct((XHEADS, NMEM, XW), MXU)
    return pl.pallas_call(body, name="kv_fwd", out_shape=(sds, sds), compiler_params=_cparams())(mem, wkv)


def _kv_bwd(mem, dks_l, dvs_l):
    def body(mem_ref, *refs):
        dks_refs, dvs_refs, out_ref = refs[:DEPTH], refs[DEPTH:2 * DEPTH], refs[2 * DEPTH]
        dk = jnp.zeros((NMEM, XW), F32)
        dv = jnp.zeros((NMEM, XW), F32)
        for h in range(XHEADS):
            m = _head_mask(h)
            for l in range(DEPTH):
                dk = dk + jnp.where(m, dks_refs[l][h], 0.0)
                dv = dv + jnp.where(m, dvs_refs[l][h], 0.0)
        out_ref[...] = _tn(mem_ref[...], jnp.concatenate([dk, dv], axis=1)).astype(WIRE)

    return pl.pallas_call(body, name="kv_bwd", out_shape=jax.ShapeDtypeStruct((D, 2 * XW), WIRE),
                          compiler_params=_cparams())(mem, *dks_l, *dvs_l)


def _prep_weights(w_ins, flip, w_out, wkv):
    def body(a_ref, b_ref, c_ref, d_ref, wo_ref, kv_ref, ao, bo, co, do, wo0, wo1, wo2, wo3, kvo):
        for t, (src, dst) in enumerate(((a_ref, ao), (b_ref, bo), (c_ref, co), (d_ref, do))):
            dst[...] = (src[...].T if flip[t] else src[...]).astype(MXU)
        for l, dst in enumerate((wo0, wo1, wo2, wo3)):
            dst[...] = wo_ref[l].astype(MXU)
        kvo[...] = kv_ref[...].astype(MXU)

    outs = [jax.ShapeDtypeStruct(w.shape[::-1] if flip[t] else w.shape, MXU) for t, w in enumerate(w_ins)]
    outs += [jax.ShapeDtypeStruct(w_out.shape[1:], MXU)] * DEPTH + [jax.ShapeDtypeStruct(wkv.shape, MXU)]
    return pl.pallas_call(body, name="prep_weights", out_shape=outs, compiler_params=_cparams())(*w_ins, w_out, wkv)


def _adam_math(w, g, m, v):
    m = B1 * m + (1.0 - B1) * g
    v = B2 * v + (1.0 - B2) * (g * g)
    m_hat = m / (1.0 - B1 ** STEP)
    v_hat = v / (1.0 - B2 ** STEP)
    delta = -LR * (m_hat / (jnp.sqrt(v_hat) + EPS) + WD * w)
    return delta, m, v


def _sum_adam(name, recv, w, m, v, transpose):
    rows, cols = recv.shape[1], recv.shape[2]

    def body(r_ref, w_ref, m_ref, v_ref, g_out, d_out, m_out, v_out, acc_ref):
        s = pl.program_id(0)

        @pl.when(s == 0)
        def _():
            acc_ref[...] = r_ref[...].astype(F32)

        @pl.when(s > 0)
        def _():
            acc_ref[...] += r_ref[...].astype(F32)

        @pl.when(s == NDEV - 1)
        def _():
            g = acc_ref[...].T if transpose else acc_ref[...]
            d, mn, vn = _adam_math(w_ref[...], g, m_ref[...], v_ref[...])
            g_out[...] = g
            d_out[...] = d
            m_out[...] = mn
            v_out[...] = vn

    sds = jax.ShapeDtypeStruct(w.shape, F32)
    ins = [(recv, pl.BlockSpec((None, rows, cols), lambda s: (s, 0, 0))), (w, _res(w)), (m, _res(m)), (v, _res(v))]
    outs = [(sds, _res_sds(w.shape))] * 4
    return _call(body, name, (NDEV,), ins, outs, [pltpu.VMEM((rows, cols), F32)])


def _bias_finalize(dbs_exp):
    def body(dbs_ref, dabs_ref):
        dabs_ref[...] = jnp.sum(dbs_ref[...], axis=-1)

    return pl.pallas_call(body, name="bias_finalize", out_shape=jax.ShapeDtypeStruct((NH, HD), F32),
                          compiler_params=_cparams())(dbs_exp)


def _lb_finalize(dlb, lb_logits):
    def body(dlb_ref, lg_ref, dlg_ref):
        total = jnp.zeros((DEPTH, TOK), F32)
        lg = lg_ref[...]
        e = jnp.exp(lg - jnp.max(lg, axis=0, keepdims=True))
        p = e / jnp.sum(e, axis=0, keepdims=True)
        row = lax.broadcasted_iota(jnp.int32, (DEPTH, TOK), 0)
        for layer in range(DEPTH):
            if layer % 4 != 1:
                continue
            dp = jnp.where((row >= 1) & (row <= layer), dlb_ref[...], 0.0)
            total = total + p * (dp - jnp.sum(p * dp, axis=0, keepdims=True))
        dlg_ref[...] = total

    return pl.pallas_call(body, name="lb_finalize", out_shape=jax.ShapeDtypeStruct((DEPTH, TOK), F32),
                          compiler_params=_cparams())(dlb, lb_logits)


def _small_sum_adam(name, gathered, w, m, v):
    rows = w.shape[0]

    def body(r_ref, w_ref, m_ref, v_ref, g_out, d_out, m_out, v_out):
        g = r_ref[0]
        for s in range(1, NDEV):
            g = g + r_ref[s]
        d, mn, vn = _adam_math(w_ref[...], g, m_ref[...], v_ref[...])
        g_out[...] = g
        d_out[...] = d
        m_out[...] = mn
        v_out[...] = vn

    sds = jax.ShapeDtypeStruct((rows, 128), F32)
    return pl.pallas_call(body, name=name, out_shape=(sds,) * 4, compiler_params=_cparams())(gathered, w, m, v)


def _sum_adam_stacked(name, recvs, w, m, v):
    nl, rows, cols = w.shape

    def body(*refs):
        r_refs = refs[:nl]
        w_ref, m_ref, v_ref, g_out, d_out, m_out, v_out, acc_ref = refs[nl:]
        s = pl.program_id(0)

        @pl.when(s == 0)
        def _():
            for l in range(nl):
                acc_ref[l] = r_refs[l][...].astype(F32)

        @pl.when(s > 0)
        def _():
            for l in range(nl):
                acc_ref[l] += r_refs[l][...].astype(F32)

        @pl.when(s == NDEV - 1)
        def _():
            g = acc_ref[...]
            d, mn, vn = _adam_math(w_ref[...], g, m_ref[...], v_ref[...])
            g_out[...] = g
            d_out[...] = d
            m_out[...] = mn
            v_out[...] = vn

    sds = jax.ShapeDtypeStruct(w.shape, F32)
    ins = [(r, pl.BlockSpec((None, rows, cols), lambda s: (s, 0, 0))) for r in recvs] + [(w, _res(w)), (m, _res(m)), (v, _res(v))]
    outs = [(sds, _res_sds(w.shape))] * 4
    return _call(body, name, (NDEV,), ins, outs, [pltpu.VMEM(w.shape, F32)])


def _group_adam(name, recvs, params):
    nk = len(recvs)

    def body(*refs):
        pos, oi = nk, nk + 3 * sum(p is not None for p in params)
        for k in range(nk):
            g = refs[k][0]
            for s in range(1, NDEV):
                g = g + refs[k][s]
            refs[oi][...] = g
            oi += 1
            if params[k] is not None:
                d, mn, vn = _adam_math(refs[pos][...], g, refs[pos + 1][...], refs[pos + 2][...])
                refs[oi][...] = d
                refs[oi + 1][...] = mn
                refs[oi + 2][...] = vn
                pos += 3
                oi += 3

    out_shape, counts = [], []
    for k in range(nk):
        counts.append(4 if params[k] is not None else 1)
        out_shape += [jax.ShapeDtypeStruct(recvs[k].shape[1:], F32)] * counts[-1]
    args = list(recvs) + [a for p in params if p is not None for a in p]
    flat = pl.pallas_call(body, name=name, out_shape=out_shape, compiler_params=_cparams())(*args)
    res, o = [], 0
    for cnt in counts:
        res.append(flat[o:o + cnt])
        o += cnt
    return res


def _adam_only(g, w, m, v):
    def body(g_ref, w_ref, m_ref, v_ref, d_out, m_out, v_out):
        d, mn, vn = _adam_math(w_ref[...], g_ref[...], m_ref[...], v_ref[...])
        d_out[...] = d
        m_out[...] = mn
        v_out[...] = vn

    sds = jax.ShapeDtypeStruct(w.shape, F32)
    return pl.pallas_call(body, name="shard_adam", out_shape=(sds,) * 3, compiler_params=_cparams())(g, w, m, v)


def _me_and_peers():
    x, y, c = lax.axis_index("x"), lax.axis_index("y"), lax.axis_index("c")
    me = 4 * x + 2 * y + c
    peers = []
    for k in range(1, NDEV):
        kx, ky, kc = (k >> 2) & 1, (k >> 1) & 1, k & 1
        px = x + kx - 2 * x * kx
        py = y + ky - 2 * y * ky
        pc = c + kc - 2 * c * kc
        peers.append(((px, py, pc), 4 * px + 2 * py + pc))
    return me, peers


_ANY = pl.BlockSpec(memory_space=pl.ANY)


class _Exchange:
    def __init__(self, arrays, split):
        self.arrays = list(arrays)
        self.split = list(split)
        n = len(self.arrays)
        self.out_shapes = []
        for a, sp in zip(self.arrays, self.split):
            rows = a.shape[0] // NDEV if sp else a.shape[0]
            self.out_shapes.append(jax.ShapeDtypeStruct((NDEV, rows, a.shape[1]), a.dtype))
        self.scratch = [pltpu.SemaphoreType.DMA((n, NDEV - 1)), pltpu.SemaphoreType.DMA((n, NDEV - 1)),
                        pltpu.SemaphoreType.DMA((n,))]

    def _block(self, src, t, d):
        if not self.split[t]:
            return src[t]
        rows = self.arrays[t].shape[0] // NDEV
        return src[t].at[pl.ds(d * rows, rows)]

    def start(self, src, dst, sems):
        send_sems, recv_sems, local_sems = sems
        me, peers = _me_and_peers()
        for t in range(len(self.arrays)):
            pltpu.make_async_copy(self._block(src, t, me), dst[t].at[me], local_sems.at[t]).start()
        for k, (dev, idx) in enumerate(peers):
            for t in range(len(self.arrays)):
                pltpu.make_async_remote_copy(src_ref=self._block(src, t, idx), dst_ref=dst[t].at[me],
                                             send_sem=send_sems.at[t, k], recv_sem=recv_sems.at[t, k],
                                             device_id=dev, device_id_type=pl.DeviceIdType.MESH).start()

    def wait(self, src, dst, sems):
        send_sems, recv_sems, local_sems = sems
        me, peers = _me_and_peers()

        def slot_copy(t, k, dev, idx):
            return pltpu.make_async_remote_copy(src_ref=dst[t].at[idx], dst_ref=dst[t].at[idx], send_sem=send_sems.at[t, k],
                                                recv_sem=recv_sems.at[t, k], device_id=dev,
                                                device_id_type=pl.DeviceIdType.MESH)

        for k, (dev, idx) in enumerate(peers):
            for t in range(len(self.arrays)):
                slot_copy(t, k, dev, idx).wait_recv()
        for k, (dev, idx) in enumerate(peers):
            for t in range(len(self.arrays)):
                slot_copy(t, k, dev, idx).wait_send()
        for t in range(len(self.arrays)):
            pltpu.make_async_copy(dst[t].at[me], dst[t].at[me], local_sems.at[t]).wait()

    def gather_by_chip(self, src, dst, sems):
        assert not any(self.split)
        send_sems, recv_sems, local_sems = sems
        n = len(self.arrays)
        x, y, c = lax.axis_index("x"), lax.axis_index("y"), lax.axis_index("c")
        me, sibling = 4 * x + 2 * y + c, (x, y, 1 - c)
        chips = [(1 - x, y), (x, 1 - y), (1 - x, 1 - y)]

        def index(chip, core):
            return 4 * chip[0] + 2 * chip[1] + core

        def copy(t, k, block, to, from_src):
            return pltpu.make_async_remote_copy(src_ref=src[t] if from_src else dst[t].at[block], dst_ref=dst[t].at[block],
                                                send_sem=send_sems.at[t, k], recv_sem=recv_sems.at[t, k],
                                                device_id=to, device_id_type=pl.DeviceIdType.MESH)

        local = [pltpu.make_async_copy(src[t], dst[t].at[me], local_sems.at[t]) for t in range(n)]
        for cp in local:
            cp.start()
        sends = []
        for t in range(n):
            sends.append(copy(t, 0, me, sibling, True))
            sends += [copy(t, 1 + j, me, (*chip, c), True) for j, chip in enumerate(chips)]
        for cp in sends:
            cp.start()
        for j, chip in enumerate(chips):
            for t in range(n):
                copy(t, 1 + j, index(chip, c), sibling, False).wait_recv()
                passed = copy(t, 4 + j, index(chip, c), sibling, False)
                passed.start()
                sends.append(passed)
        for t in range(n):
            copy(t, 0, index((x, y), 1 - c), sibling, False).wait_recv()
            for j, chip in enumerate(chips):
                copy(t, 4 + j, index(chip, 1 - c), sibling, False).wait_recv()
        for cp in sends:
            cp.wait_send()
        for cp in local:
            cp.wait()

    def run(self, name, by_chip=False):
        n = len(self.arrays)

        def body(*refs):
            src, dst, sems = refs[:n], refs[n:2 * n], refs[2 * n:]
            if by_chip:
                self.gather_by_chip(src, dst, sems)
                return
            self.start(src, dst, sems)
            self.wait(src, dst, sems)

        return pl.pallas_call(
            body, name=name, out_shape=self.out_shapes, in_specs=[_ANY] * n, out_specs=[_ANY] * n,
            scratch_shapes=self.scratch,
        )(*self.arrays)


SMALL = [("ln_g", (DEPTH, D), False), ("ln_b", (DEPTH, D), False), ("hgrn_lb_logits", (DEPTH, TOK), False),
         ("a_w_s", (1, NH, HD, HD), False), ("a_b_s", (1, NH, HD), False), ("b_norm_g", (1, TOK), True),
         ("c_w_pool", (1, 4, POOL_GROUP, POOL_GROUP), False), ("c_scale", (1, TOK), True),
         ("d_conv_w", (1, 4, TOK), True), ("d_conv_b", (1, TOK), True),
         ("d_w_gx", (1, NH, HD, HD), False), ("d_b_gx", (1, NH, HD), False),
         ("d_w_ga", (1, NH, HD, HD), False), ("d_b_ga", (1, NH, HD), False), ("d_a_param", (1, TOK), True)]


def _pack(parts, total_rows):
    flat = jnp.concatenate([p.reshape(-1).astype(F32) for p in parts])
    flat = jnp.pad(flat, (0, total_rows * 128 - flat.shape[0]))
    return flat.reshape(total_rows, 128)


def _size(shape):
    n = 1
    for s in shape:
        n *= s
    return n


def _rows_for(n):
    return -(-n // 1024) * 8


def kernel(x, mem, mem_kv_w, ln_g, ln_b, w_out, hgrn_lb_logits, a_w_in, a_w_s, a_b_s, b_w_in, b_norm_g, c_w_in, c_w_pool, c_scale, d_w_in, d_conv_w, d_conv_b, d_w_gx, d_b_gx, d_w_ga, d_b_ga, d_a_param, loss_target, m_mem_kv_w, m_ln_g, m_ln_b, m_w_out, m_hgrn_lb_logits, m_a_w_in, m_a_w_s, m_a_b_s, m_b_w_in, m_b_norm_g, m_c_w_in, m_c_w_pool, m_c_scale, m_d_w_in, m_d_conv_w, m_d_conv_b, m_d_w_gx, m_d_b_gx, m_d_w_ga, m_d_b_ga, m_d_a_param, v_mem_kv_w, v_ln_g, v_ln_b, v_w_out, v_hgrn_lb_logits, v_a_w_in, v_a_w_s, v_a_b_s, v_b_w_in, v_b_norm_g, v_c_w_in, v_c_w_pool, v_c_scale, v_d_w_in, v_d_conv_w, v_d_conv_b, v_d_w_gx, v_d_b_gx, v_d_w_ga, v_d_b_ga, v_d_a_param):
    W = dict(mem_kv_w=mem_kv_w, ln_g=ln_g, ln_b=ln_b, w_out=w_out, hgrn_lb_logits=hgrn_lb_logits, a_w_in=a_w_in, a_w_s=a_w_s,
             a_b_s=a_b_s, b_w_in=b_w_in, b_norm_g=b_norm_g, c_w_in=c_w_in, c_w_pool=c_w_pool, c_scale=c_scale, d_w_in=d_w_in,
             d_conv_w=d_conv_w, d_conv_b=d_conv_b, d_w_gx=d_w_gx, d_b_gx=d_b_gx, d_w_ga=d_w_ga, d_b_ga=d_b_ga, d_a_param=d_a_param)
    M = dict(mem_kv_w=m_mem_kv_w, ln_g=m_ln_g, ln_b=m_ln_b, w_out=m_w_out, hgrn_lb_logits=m_hgrn_lb_logits, a_w_in=m_a_w_in,
             a_w_s=m_a_w_s, a_b_s=m_a_b_s, b_w_in=m_b_w_in, b_norm_g=m_b_norm_g, c_w_in=m_c_w_in, c_w_pool=m_c_w_pool,
             c_scale=m_c_scale, d_w_in=m_d_w_in, d_conv_w=m_d_conv_w, d_conv_b=m_d_conv_b, d_w_gx=m_d_w_gx, d_b_gx=m_d_b_gx,
             d_w_ga=m_d_w_ga, d_b_ga=m_d_b_ga, d_a_param=m_d_a_param)
    V = dict(mem_kv_w=v_mem_kv_w, ln_g=v_ln_g, ln_b=v_ln_b, w_out=v_w_out, hgrn_lb_logits=v_hgrn_lb_logits, a_w_in=v_a_w_in,
             a_w_s=v_a_w_s, a_b_s=v_a_b_s, b_w_in=v_b_w_in, b_norm_g=v_b_norm_g, c_w_in=v_c_w_in, c_w_pool=v_c_w_pool,
             c_scale=v_c_scale, d_w_in=v_d_w_in, d_conv_w=v_d_conv_w, d_conv_b=v_d_conv_b, d_w_gx=v_d_w_gx, d_b_gx=v_d_b_gx,
             d_w_ga=v_d_w_ga, d_b_ga=v_d_b_ga, d_a_param=v_d_a_param)
    me = 4 * lax.axis_index("x") + 2 * lax.axis_index("y") + lax.axis_index("c")
    x2, mem2, tgt2 = x[0], mem[0], loss_target[0]
    in_names = ["a_w_in", "b_w_in", "c_w_in", "d_w_in"]

    shard_names = [n for n, _, sh in SMALL if sh]
    small_shard = _pack([W[n] for n in shard_names], 8)
    flip = [W[n].shape[2] % 128 == 0 for n in in_names]

    def shard2d(tree, t):
        a = tree[in_names[t]][0]
        return a if flip[t] else jnp.swapaxes(a, 0, 1)

    wts = _prep_weights([shard2d(W, t) for t in range(DEPTH)], flip, w_out, mem_kv_w)
    wt_sh, wo_sh, wkv_sh = wts[:4], wts[4:8], wts[8]
    g0 = _Exchange([wt_sh[0], wo_sh[0], wkv_sh, small_shard], [False] * 4).run("gather_first", by_chip=True)
    wt_full = [g0[0].reshape(-1, D)]
    wout_full = [g0[1].reshape(D, D)]
    wkv_full = g0[2].reshape(D, 2 * XW)
    sm = g0[3].reshape(NDEV, 1024)
    full_small = {}
    off = 0
    for n, shape, _ in [s for s in SMALL if s[2]]:
        per = _size(shape) // NDEV
        blk = sm[:, off:off + per]
        if n == "d_conv_w":
            full_small[n] = blk.reshape(NDEV, 4, TOK // NDEV).transpose(1, 0, 2).reshape(4, TOK)
        else:
            full_small[n] = blk.reshape(1, TOK)
        off += per

    ks, vs = _kv_fwd(mem2, wkv_full)
    tri_bs = jnp.broadcast_to(a_b_s[0][:, :, None], (NH, HD, HD))
    wbd = jnp.zeros((TOK, TOK), F32)
    for g in range(4):
        wbd = lax.dynamic_update_slice(wbd, c_w_pool[0, g], (g * POOL_GROUP, g * POOL_GROUP))
    wbd = wbd.astype(MXU)
    prm = {0: [a_w_s[0], tri_bs],
           1: [hgrn_lb_logits, full_small["b_norm_g"]],
           2: [wbd, full_small["c_scale"]],
           3: [full_small["d_conv_w"], full_small["d_conv_b"], d_w_gx[0].astype(MXU), d_b_gx[0].reshape(1, TOK),
               d_w_ga[0].astype(MXU), d_b_ga[0].reshape(1, TOK), full_small["d_a_param"]]}
    ones = jnp.ones((1, D), F32)
    zeros = jnp.zeros((1, D), F32)
    xs, gs, bs = [x2], [ones], [zeros]
    saved = []
    for i in range(DEPTH):
        ride = _Exchange([wt_sh[i + 1], wo_sh[i + 1]], [False, False]) if i + 1 < DEPTH else None
        res = _layer_fwd(i, i, xs[i], gs[i], bs[i], wt_full[i], wout_full[i], ks, vs, prm[i], ride)
        if ride:
            wt_full.append(res[-2].reshape(-1, D))
            wout_full.append(res[-1].reshape(D, D))
            res = res[:-2]
        saved.append(res)
        xs.append(res[1])
        gs.append(ln_g[i:i + 1])
        bs.append(ln_b[i:i + 1])

    up = tgt2
    grads = {}
    dks_l, dvs_l, dwt_l, dwout_l, dlng_l, dlnb_l = [], [], [], [], [], []
    recv_wt, recv_wo = [None] * DEPTH, [None] * DEPTH
    loss_part = None
    sharded = {n for n, _, sh in SMALL if sh}
    group = {3: ["ln_g#3", "ln_b#3", "d_conv_w", "d_conv_b", "d_w_gx", "d_b_gx", "d_w_ga", "d_b_ga", "d_a_param"],
             2: ["ln_g#2", "ln_b#2", "c_w_pool", "c_scale"],
             1: ["ln_g#1", "ln_b#1", "hgrn_lb_logits", "b_norm_g"],
             0: ["ln_g#0", "ln_b#0", "a_w_s", "a_b_s", "loss"]}

    def small_of(l):
        return [grads[e].reshape(-1, grads[e].shape[-1]) for e in group[l]]

    recv_small = [None] * DEPTH
    for i in reversed(range(DEPTH)):
        res, xo_saved = saved[i][:-1], saved[i][-1]
        extra = None if len(res) <= 4 else (res[4] if len(res) == 5 else res[4:])
        ride = None
        if i + 1 < DEPTH:
            smalls = small_of(i + 1)
            ride = _Exchange([dwt_l[-1], dwout_l[-1]] + smalls, [True, True] + [False] * len(smalls))
        out = _layer_bwd(i, i, up, i == DEPTH - 1, res[1], res[2], res[3], xo_saved, gs[i + 1], bs[i + 1], res[0], wout_full[i],
                         ks, vs, prm[i], extra, ride)
        if ride:
            nr = len(ride.arrays)
            recv_wt[i + 1], recv_wo[i + 1], recv_small[i + 1] = out[-nr], out[-nr + 1], out[-nr + 2:]
            out = out[:-nr]
        dres, dproj, dwout_i, dks_i, dvs_i, dg_i, db_i, loss_i = out[:8]
        pg = out[8:]
        if i == DEPTH - 1:
            grads["loss"] = loss_i
        dks_l.append(dks_i)
        dvs_l.append(dvs_i)
        dwout_l.append(dwout_i)
        grads[f"ln_g#{i}"], grads[f"ln_b#{i}"] = dg_i, db_i
        if i == 0:
            grads["a_w_s"], dbs_exp = pg
            grads["a_b_s"] = _bias_finalize(dbs_exp)
        elif i == 1:
            dlb, grads["b_norm_g"] = pg
            grads["hgrn_lb_logits"] = _lb_finalize(dlb, hgrn_lb_logits)
        elif i == 2:
            dwbd, grads["c_scale"] = pg
            grads["c_w_pool"] = jnp.stack([lax.dynamic_slice(dwbd, (g * POOL_GROUP, g * POOL_GROUP), (POOL_GROUP, POOL_GROUP))
                                           for g in range(4)])
        else:
            (grads["d_conv_w"], grads["d_conv_b"], grads["d_w_gx"], grads["d_b_gx"], grads["d_w_ga"], grads["d_b_ga"],
             grads["d_a_param"]) = pg
        ride = None
        if i == 0:
            dwkv = _kv_bwd(mem2, dks_l, dvs_l)
            smalls = small_of(0)
            ride = _Exchange([dwout_i, dwkv] + smalls, [True, True] + [False] * len(smalls))
        pb = _proj_bwd(i, dproj, dres, xs[i], gs[i], bs[i], wt_full[i], ride)
        up, dwt = pb[:2]
        if ride:
            recv_wo[0], recv_kv, recv_small[0] = pb[2], pb[3], pb[4:]
        dwt_l.append(dwt)
    grad_x = up[None]

    recv_wt[0], = _Exchange([dwt_l[-1]], [True]).run("scatter_last")

    outs = {}
    for t, n in enumerate(in_names):
        res = _sum_adam(f"adam_{n}", recv_wt[t], shard2d(W, t), shard2d(M, t), shard2d(V, t), flip[t])
        outs[n] = tuple((o if flip[t] else jnp.swapaxes(o, 0, 1))[None] for o in res)
    outs["w_out"] = tuple(_sum_adam_stacked("adam_w_out", recv_wo, w_out, m_w_out, v_w_out))
    outs["mem_kv_w"] = _sum_adam("adam_mem_kv_w", recv_kv, mem_kv_w, m_mem_kv_w, v_mem_kv_w, False)

    def entry_of(tree, e, like):
        if "#" in e:
            n, l = e.split("#")
            return tree[n][int(l):int(l) + 1]
        return tree[e].reshape(like.shape[1:])

    small = [{}, {}, {}, {}]
    for l in range(DEPTH):
        params = [None if (e == "loss" or e in sharded) else tuple(entry_of(t, e, r) for t in (W, M, V))
                  for e, r in zip(group[l], recv_small[l])]
        for e, res in zip(group[l], _group_adam(f"small_adam{l}", recv_small[l], params)):
            for j, a in enumerate(res):
                small[j][e] = a
    loss = small[0]["loss"][0, 0]
    for j in range(4):
        for n in ("ln_g", "ln_b"):
            small[j][n] = jnp.concatenate([small[j][f"{n}#{l}"] for l in range(DEPTH)], axis=0)
    g_small = small[0]
    for n, _, sh in SMALL:
        if not sh:
            outs[n] = tuple(small[j][n] for j in range(4))
    per = TOK // NDEV
    g_sh = {n: lax.dynamic_slice_in_dim(g_small[n], me * per, per, axis=1) for n, s, sh in SMALL if sh}
    gp = _pack([g_sh[n] for n in shard_names], 8)
    d_p, m_p, v_p = _adam_only(gp, small_shard, _pack([M[n] for n in shard_names], 8), _pack([V[n] for n in shard_names], 8))
    o = 0
    for n in shard_names:
        cnt = _size(W[n].shape)
        outs[n] = (g_sh[n],) + tuple(t.reshape(-1)[o:o + cnt].reshape(W[n].shape) for t in (d_p, m_p, v_p))
        o += cnt

    order = ["mem_kv_w", "ln_g", "ln_b", "w_out", "hgrn_lb_logits", "a_w_in", "a_w_s", "a_b_s", "b_w_in", "b_norm_g", "c_w_in",
             "c_w_pool", "c_scale", "d_w_in", "d_conv_w", "d_conv_b", "d_w_gx", "d_b_gx", "d_w_ga", "d_b_ga", "d_a_param"]
    result = [loss, grad_x]
    for j in range(4):
        result += [outs[n][j].reshape(W[n].shape) for n in order]
    return tuple(result)
```

```python
import functools

import jax
import jax.numpy as jnp
from jax import lax
from jax.experimental import pallas as pl
from jax.experimental.pallas import tpu as pltpu

F32 = jnp.float32
MXU = jnp.bfloat16
WIRE = jnp.bfloat16

D = 1024
TOK = 768
XW = 256
NMEM = 256
XHEADS = 4
XSCALE = 64 ** -0.5
NH = 6
HD = 128
CH = 16
POOL_WINDOWS = (2, 4, 8, 16)
POOL_GROUP = 192
DEPTH = 4
ALPHA = (2 * DEPTH) ** 0.25
LN_EPS = 1e-5
RMS_EPS = 1e-6
LRU_C = 8.0
B1, B2, LR, EPS, WD, STEP = 0.9, 0.999, 0.001, 1e-8, 0.01, 10

NDEV = 8
TS_FWD = {0: 512, 1: 256, 2: 512, 3: 256}
TS_BWD = {0: 512, 1: 256, 2: 512, 3: 256}
TSB = 512
VMEM_LIMIT = 58 * 1024 * 1024
VMEM_LIMIT_WIDE = 62 * 1024 * 1024

KIND_WIDTHS = {0: 2 * TOK + XW + D, 1: 3 * TOK + XW + D, 2: TOK + XW + D, 3: TOK + XW + D}


def _mm(a, b, ca, cb):
    return lax.dot_general(a.astype(MXU), b.astype(MXU), (((ca,), (cb,)), ((), ())), preferred_element_type=F32)


def _nn(a, b):
    return _mm(a, b, 1, 0)


def _nt(a, b):
    return _mm(a, b, 1, 1)


def _tn(a, b):
    return _mm(a, b, 0, 0)


def _bmm(a, b, ca, cb):
    return lax.dot_general(a.astype(MXU), b.astype(MXU), (((ca,), (cb,)), ((0,), (0,))), preferred_element_type=F32)


def _sigmoid(x):
    return 1.0 / (1.0 + jnp.exp(-x))


def _vjp1(fn, x, dy):
    return jax.vjp(fn, x)[1](dy)[0]


def _rowsum(x):
    return jnp.sum(x, axis=0, keepdims=True)


def _row(x, r):
    sel = lax.broadcasted_iota(jnp.int32, x.shape, 0) == r
    return jnp.sum(jnp.where(sel, x, 0.0), axis=0, keepdims=True)


def _acc(ref, val):
    ref[...] += val


def _cparams(sem=None, vmem=VMEM_LIMIT):
    return pltpu.CompilerParams(dimension_semantics=sem, vmem_limit_bytes=vmem)


def _res(a):
    nd = a.ndim
    return pl.BlockSpec(a.shape, lambda i: (0,) * nd)


def _res_sds(shape):
    nd = len(shape)
    return pl.BlockSpec(shape, lambda i: (0,) * nd)


def _row_spec(width, nt, rev, ts):
    if rev:
        return pl.BlockSpec((ts, width), lambda i: (nt - 1 - i, 0))
    return pl.BlockSpec((ts, width), lambda i: (i, 0))


def _call(body, name, grid, ins, outs, scratch=(), sem=("arbitrary",), vmem=VMEM_LIMIT):
    arrays = [a for a, _ in ins]
    return pl.pallas_call(
        body, name=name, grid=grid,
        in_specs=[s for _, s in ins],
        out_specs=[s for _, s in outs],
        out_shape=[o for o, _ in outs],
        scratch_shapes=list(scratch),
        compiler_params=_cparams(sem, vmem),
    )(*arrays)


def _xattn_fwd(qx, ks_ref, vs_ref):
    o = None
    ps = []
    for h in range(XHEADS):
        s = _nt(qx, ks_ref[h]) * XSCALE
        s = s - jnp.max(s, axis=-1, keepdims=True)
        e = jnp.exp(s)
        p = e * (1.0 / jnp.sum(e, axis=-1, keepdims=True))
        ps.append(p)
        oh = _nn(p, vs_ref[h])
        o = oh if o is None else o + oh
    return o, ps


def _xattn_bwd(qx, ps, dxo, ks_ref, vs_ref, dks_ref, dvs_ref):
    dq = None
    for h in range(XHEADS):
        p = ps[h]
        dp = _nt(dxo, vs_ref[h])
        ds = p * (dp - jnp.sum(dp * p, axis=-1, keepdims=True))
        dqh = _nn(ds, ks_ref[h]) * XSCALE
        dq = dqh if dq is None else dq + dqh
        dks_ref[h] += _tn(ds, qx) * XSCALE
        dvs_ref[h] += _tn(p, dxo)
    return dq


def _tril128():
    r = lax.broadcasted_iota(jnp.int32, (HD, HD), 0)
    c = lax.broadcasted_iota(jnp.int32, (HD, HD), 1)
    return c <= r


GELU_C = 0.7978845608028654
GELU_K = 0.044715


def _gelu(x):
    th = jnp.tanh(GELU_C * (x + GELU_K * (x * x * x)))
    return 0.5 * x * (1.0 + th), th


def _gelu_grad(x, th):
    return 0.5 * (1.0 + th) + 0.5 * x * (1.0 - th * th) * (GELU_C * (1.0 + 3.0 * GELU_K * (x * x)))


def _gmlp_fwd(u, v, ws_ref, bs_ref):
    ts = u.shape[0]
    ug, thu = _gelu(u)
    vg, thv = _gelu(v)
    tri = _tril128()
    toks, res = [], []
    for g in range(NH):
        sl = slice(g * HD, (g + 1) * HD)
        vgh = vg[:, sl]
        cen = vgh - jnp.mean(vgh, axis=-1, keepdims=True)
        rstd = lax.rsqrt(jnp.mean(cen * cen, axis=-1, keepdims=True) + LN_EPS)
        vn = cen * rstd
        w = jnp.where(tri, ws_ref[g], 0.0).astype(MXU)
        mix = jnp.concatenate([_nn(w, vn[n * HD:(n + 1) * HD]) + bs_ref[g] for n in range(ts // HD)], axis=0)
        toks.append(ug[:, sl] * mix)
        res.append((vn, rstd, mix, w))
    return jnp.concatenate(toks, axis=1), (ug, res, thu, thv)


def _gmlp_bwd(u, v, fres, dtok, dws_ref, dbs_ref):
    ts = u.shape[0]
    ug, res, thu, thv = fres
    tri = _tril128()
    dugs, dvgs = [], []
    for g in range(NH):
        sl = slice(g * HD, (g + 1) * HD)
        vn, rstd, mix, w = res[g]
        dmix = dtok[:, sl] * ug[:, sl]
        dugs.append(dtok[:, sl] * mix)
        dvn_rows = []
        dw = None
        dbs = None
        for n in range(ts // HD):
            dm = dmix[n * HD:(n + 1) * HD]
            dvn_rows.append(_tn(w, dm))
            t = _nt(dm, vn[n * HD:(n + 1) * HD])
            dw = t if dw is None else dw + t
            dbs = dm if dbs is None else dbs + dm
        dws_ref[g] += jnp.where(tri, dw, 0.0)
        dbs_ref[g] += dbs
        dvn = jnp.concatenate(dvn_rows, axis=0)
        dvgs.append(rstd * (dvn - jnp.mean(dvn, axis=-1, keepdims=True) - vn * jnp.mean(dvn * vn, axis=-1, keepdims=True)))
    du = jnp.concatenate(dugs, axis=1) * _gelu_grad(u, thu)
    dv = jnp.concatenate(dvgs, axis=1) * _gelu_grad(v, thv)
    return du, dv


def _chunk_cumsum(x):
    row = lax.broadcasted_iota(jnp.int32, x.shape, 0) % CH
    for s in (1, 2, 4, 8):
        x = x + jnp.where(row >= s, pltpu.roll(x, s, 0), 0.0)
    return x


def _chunk_revcumsum(x):
    n = x.shape[0]
    row = lax.broadcasted_iota(jnp.int32, x.shape, 0) % CH
    for s in (1, 2, 4, 8):
        x = x + jnp.where(row < CH - s, pltpu.roll(x, n - s, 0), 0.0)
    return x


def _chunk_sum(x):
    n, w = x.shape
    return jnp.sum(x.reshape(n // CH, CH, w), axis=1)


def _chunk_bcast(c, n):
    nch, w = c.shape
    return jnp.broadcast_to(c[:, None, :], (nch, CH, w)).reshape(n, w)


def _lower_bound(lb_logits, layer):
    lg = lb_logits
    e = jnp.exp(lg - jnp.max(lg, axis=0, keepdims=True))
    p = e / jnp.sum(e, axis=0, keepdims=True)
    row = lax.broadcasted_iota(jnp.int32, p.shape, 0)
    lb = jnp.sum(jnp.where((row >= 1) & (row <= layer), p, 0.0), axis=0, keepdims=True)
    return lb, p


def _hgrn_prep(q, fl, lb):
    n = q.shape[0]
    sg = _sigmoid(fl)
    f = lb + (1.0 - lb) * sg
    lf = jnp.log(f)
    sq = _sigmoid(q)
    g = _chunk_cumsum(lf)
    tot = _chunk_sum(lf)
    gl = _chunk_bcast(tot, n)
    eg = jnp.exp(g)
    eng = jnp.exp(-g)
    egl = jnp.exp(gl - g)
    k = 1.0 - f
    qf = q * sq
    return dict(sg=sg, f=f, k=k, sq=sq, qf=qf, eg=eg, eng=eng, egl=egl,
                qd=qf * eg, ki=k * eng, ke=k * egl, dch=jnp.exp(tot))


def _hgrn_mask():
    r = lax.broadcasted_iota(jnp.int32, (HD, HD), 0)
    c = lax.broadcasted_iota(jnp.int32, (HD, HD), 1)
    return (r // CH == c // CH) & (c <= r)


def _hgrn_states(v3, ke3, dch_h, st_in):
    nch = v3.shape[0]
    ut = _bmm(v3, ke3, 1, 1)
    dfull = jnp.broadcast_to(dch_h[:, None, :], (nch, HD, HD))
    st, sts = st_in, []
    for c in range(nch):
        sts.append(st)
        st = st * dfull[c] + ut[c]
    return jnp.stack(sts), st, dfull


def _hgrn_fwd(q, fl, inp, lb, ng, st_ref, sts_ref):
    n = q.shape[0]
    nch = n // CH
    mask = _hgrn_mask()
    toks, o_l, a_l = [], [], []
    for h in range(NH):
        sl = slice(h * HD, (h + 1) * HD)
        pr = _hgrn_prep(q[:, sl], fl[:, sl], lb[:, sl])
        qd, ki, ke, v = (t.astype(MXU) for t in (pr["qd"], pr["ki"], pr["ke"], inp[:, sl]))
        qd3 = qd.reshape(nch, CH, HD)
        v3 = v.reshape(nch, CH, HD)
        ke3 = ke.reshape(nch, CH, HD)
        sts, st_ref[h], _ = _hgrn_states(v3, ke3, pr["dch"], st_ref[h])
        sts_ref[h] = sts
        o = _bmm(qd3, sts, 2, 2).reshape(n, HD)
        intra, scores = [], []
        for b in range(n // HD):
            bs = slice(b * HD, (b + 1) * HD)
            a = jnp.where(mask, _nt(qd[bs], ki[bs]), 0.0).astype(MXU)
            scores.append(a)
            intra.append(_nn(a, v[bs]))
        o = o + jnp.concatenate(intra, axis=0)
        r = lax.rsqrt(jnp.mean(o * o, axis=-1, keepdims=True) + RMS_EPS)
        toks.append(o * r * ng[:, sl])
        o_l.append(o)
        a_l.append(jnp.concatenate(scores, axis=0))
    return jnp.concatenate(toks, axis=1), jnp.concatenate(o_l, axis=1), jnp.concatenate(a_l, axis=1)


def _hgrn_bwd(q, pr, inp, lb, ng, dtok, o_all, a_all, ststart_ref, dst_ref, dng_ref, dlb_ref):
    n = q.shape[0]
    nch = n // CH
    mask = _hgrn_mask()
    dqd_l, dki_l, dke_l, dv_l, ddch_l, dng_l, toks = [], [], [], [], [], [], []
    qd_m, ki_m, ke_m, v_m = (t.astype(MXU) for t in (pr["qd"], pr["ki"], pr["ke"], inp))
    for h in range(NH):
        sl = slice(h * HD, (h + 1) * HD)
        qd, ki, ke, v = qd_m[:, sl], ki_m[:, sl], ke_m[:, sl], v_m[:, sl]
        qd3 = qd.reshape(nch, CH, HD)
        v3 = v.reshape(nch, CH, HD)
        ke3 = ke.reshape(nch, CH, HD)
        sts = ststart_ref[h]
        dfull = jnp.broadcast_to(pr["dch"][:, sl][:, None, :], (nch, HD, HD))
        sts_m = sts.astype(MXU)
        o = o_all[:, sl]
        a_l = [a_all[b * HD:(b + 1) * HD, sl] for b in range(n // HD)]
        r = lax.rsqrt(jnp.mean(o * o, axis=-1, keepdims=True) + RMS_EPS)
        toks.append(o * r * ng[:, sl])
        dt = dtok[:, sl]
        dng_l.append(_rowsum(dt * o * r))
        dn = dt * ng[:, sl]
        do = r * dn - o * (r * r * r) * jnp.mean(dn * o, axis=-1, keepdims=True)
        do_m = do.astype(MXU)
        do3 = do_m.reshape(nch, CH, HD)
        dqd_rows, dki_rows, dv_rows = [], [], []
        for b in range(n // HD):
            bs = slice(b * HD, (b + 1) * HD)
            da = jnp.where(mask, _nt(do_m[bs], v[bs]), 0.0).astype(MXU)
            dqd_rows.append(_nn(da, ki[bs]))
            dki_rows.append(_tn(da, qd[bs]))
            dv_rows.append(_tn(a_l[b], do_m[bs]))
        dqd = jnp.concatenate(dqd_rows, axis=0) + _bmm(do3, sts_m, 2, 1).reshape(n, HD)
        dki = jnp.concatenate(dki_rows, axis=0)
        dv = jnp.concatenate(dv_rows, axis=0)
        wt = _bmm(do3, qd3, 1, 1)
        dst, dstn_l = dst_ref[h], [None] * nch
        for c in reversed(range(nch)):
            dstn_l[c] = dst
            dst = wt[c] + dst * dfull[c]
        dst_ref[h] = dst
        dstn = jnp.stack(dstn_l)
        dstn_m = dstn.astype(MXU)
        dv = dv + _bmm(ke3, dstn_m, 2, 2).reshape(n, HD)
        dke = _bmm(v3, dstn_m, 2, 1).reshape(n, HD)
        ddch_l.append(jnp.sum(sts * dstn, axis=1))
        dqd_l.append(dqd)
        dki_l.append(dki)
        dke_l.append(dke)
        dv_l.append(dv)
    dqd = jnp.concatenate(dqd_l, axis=1)
    dki = jnp.concatenate(dki_l, axis=1)
    dke = jnp.concatenate(dke_l, axis=1)
    dinp = jnp.concatenate(dv_l, axis=1)
    ddch = jnp.concatenate(ddch_l, axis=1)
    _acc(dng_ref, jnp.concatenate(dng_l, axis=1))
    dqf = dqd * pr["eg"]
    dke_ke = dke * pr["ke"]
    dg = dqd * pr["qd"] - dki * pr["ki"] - dke_ke
    dk = dki * pr["eng"] + dke * pr["egl"]
    dgl = _chunk_sum(dke_ke) + ddch * pr["dch"]
    dlf = _chunk_revcumsum(dg) + _chunk_bcast(dgl, n)
    df = dlf / pr["f"] - dk
    sg = pr["sg"]
    dfl = df * (1.0 - lb) * sg * (1.0 - sg)
    _acc(dlb_ref, _rowsum(df * (1.0 - sg)))
    sq = pr["sq"]
    dq = dqf * (sq * (1.0 + q * (1.0 - sq)))
    return jnp.concatenate(toks, axis=1), dq, dfl, dinp


def _pool_select(s2, s4, s8, s16):
    col = lax.broadcasted_iota(jnp.int32, (1, TOK), 1)
    return jnp.where(col < POOL_GROUP, s2, jnp.where(col < 2 * POOL_GROUP, s4, jnp.where(col < 3 * POOL_GROUP, s8, s16)))


def _pool_cnt(pos0, n):
    pos = pos0 + lax.broadcasted_iota(jnp.int32, (n, TOK), 0) + 1
    col = lax.broadcasted_iota(jnp.int32, (n, TOK), 1)
    w = jnp.where(col < POOL_GROUP, 2, jnp.where(col < 2 * POOL_GROUP, 4, jnp.where(col < 3 * POOL_GROUP, 8, 16)))
    return jnp.minimum(pos, w).astype(F32)


def _pool_fwd(p, halo, pos0, wbd, scale):
    n = p.shape[0]
    ext = jnp.concatenate([halo, p], axis=0)
    s2 = ext + pltpu.roll(ext, 1, 0)
    s4 = s2 + pltpu.roll(s2, 2, 0)
    s8 = s4 + pltpu.roll(s4, 4, 0)
    s16 = s8 + pltpu.roll(s8, 8, 0)
    win = _pool_select(s2, s4, s8, s16)[16:]
    cnt = _pool_cnt(pos0, n)
    diff = win / cnt - p
    y = _nn(diff, wbd)
    return y * scale, (diff, y, cnt)


def _pool_bwd(fres, dtok, nxt_ref, wbd, scale, dwbd_ref, dscale_ref):
    diff, y, cnt = fres
    n = diff.shape[0]
    _acc(dscale_ref, _rowsum(dtok * y))
    dy = dtok * scale
    ddiff = _nt(dy, wbd)
    dwbd_ref[...] += _tn(diff, dy)
    qv = ddiff / cnt
    ext = jnp.concatenate([qv, nxt_ref[...]], axis=0)
    m = n + 16
    s2 = ext + pltpu.roll(ext, m - 1, 0)
    s4 = s2 + pltpu.roll(s2, m - 2, 0)
    s8 = s4 + pltpu.roll(s4, m - 4, 0)
    s16 = s8 + pltpu.roll(s8, m - 8, 0)
    adj = _pool_select(s2, s4, s8, s16)[:n]
    nxt_ref[...] = qv[:16]
    return adj - ddiff


def _neg_expm1(x):
    return jnp.where(jnp.abs(x) < 1e-2, -x * (1.0 + x * (0.5 + x * (1.0 / 6.0))), 1.0 - jnp.exp(x))


def _softplus_neg(ap):
    return jnp.maximum(-ap, 0.0) + jnp.log(1.0 + jnp.exp(-jnp.abs(ap)))


def _lru_gates(xc, zx, za, ap, first):
    gx = _sigmoid(zx)
    ga = _sigmoid(za)
    sp = _softplus_neg(ap)
    log_a = -LRU_C * ga * sp
    a = jnp.exp(log_a)
    mult = jnp.sqrt(_neg_expm1(2.0 * log_a))
    mult = jnp.where(first, 1.0, mult)
    return a, mult * gx * xc, (gx, ga, sp, mult)


def _scan_fwd(a, b, h0):
    n = a.shape[0]
    row = lax.broadcasted_iota(jnp.int32, a.shape, 0)
    s = 1
    while s < n:
        keep = row >= s
        b = b + a * jnp.where(keep, pltpu.roll(b, s, 0), 0.0)
        a = a * jnp.where(keep, pltpu.roll(a, s, 0), 1.0)
        s *= 2
    return b + a * h0


def _scan_bwd(an, d, dh_next):
    n = an.shape[0]
    row = lax.broadcasted_iota(jnp.int32, an.shape, 0)
    s = 1
    while s < n:
        keep = row < n - s
        d = d + an * jnp.where(keep, pltpu.roll(d, n - s, 0), 0.0)
        an = an * jnp.where(keep, pltpu.roll(an, n - s, 0), 1.0)
        s *= 2
    return d + an * dh_next


def _lru_conv(xb, halo, cw_ref, cb):
    ext = jnp.concatenate([halo, xb], axis=0)
    sh = [pltpu.roll(ext, 3 - j, 0)[8:] if j < 3 else xb for j in range(4)]
    xc = cb
    for j in range(4):
        xc = xc + cw_ref[pl.ds(j, 1), :] * sh[j]
    return xc, sh


def _lru_fwd(xb, halo, pos0, prm, h0):
    cw, cb, wgx, bgx, wga, bga, ap = prm
    n = xb.shape[0]
    first = (pos0 + lax.broadcasted_iota(jnp.int32, (n, 1), 0)) == 0
    hs, gxs, gas, mults = [], [], [], []
    for h in range(NH):
        sl = slice(h * HD, (h + 1) * HD)
        ext = jnp.concatenate([halo[:, sl], xb[:, sl]], axis=0)
        xc = cb[:, sl]
        for j in range(4):
            xc = xc + cw[pl.ds(j, 1), sl] * (pltpu.roll(ext, 3 - j, 0)[8:] if j < 3 else xb[:, sl])
        zx = _nn(xc, wgx[h]) + bgx[:, sl]
        za = _nn(xc, wga[h]) + bga[:, sl]
        a, b, (gx, ga, _, mult) = _lru_gates(xc, zx, za, ap[:, sl], first)
        hs.append(_scan_fwd(a, b, h0[:, sl]))
        gxs.append(gx)
        gas.append(ga)
        mults.append(mult)
    cat = functools.partial(jnp.concatenate, axis=1)
    return cat(hs), (None, None, (cat(gxs), cat(gas), None, cat(mults)), first, None)


def _lru_bwd(fres, hseq, h0, dtok, prm, carry_refs, grad_refs):
    cw, cb, wgx, bgx, wga, bga, ap = prm
    xc, sh, (gx, ga, sp, mult), first, a = fres
    anext_ref, dhnext_ref, dxcnext_ref = carry_refs
    dcw_ref, dcb_ref, dwgx_ref, dbgx_ref, dwga_ref, dbga_ref, dap_ref = grad_refs
    n = xc.shape[0]
    an = jnp.where(lax.broadcasted_iota(jnp.int32, a.shape, 0) == n - 1, anext_ref[...], pltpu.roll(a, n - 1, 0))
    dh = _scan_bwd(an, dtok, dhnext_ref[...])
    hprev = jnp.where(lax.broadcasted_iota(jnp.int32, hseq.shape, 0) == 0, h0, pltpu.roll(hseq, 1, 0))
    da = dh * hprev
    anext_ref[...] = _row(a, 0)
    dhnext_ref[...] = _row(dh, 0)
    t = dh * xc
    dxc = dh * mult * gx
    dzx = t * mult * gx * (1.0 - gx)
    dlog_a = da * a - jnp.where(first, 0.0, t * gx * (a * a) / mult)
    dza = dlog_a * (-LRU_C * sp) * ga * (1.0 - ga)
    dap = _rowsum(dlog_a * ga) * (LRU_C * _sigmoid(-ap[...]))
    _acc(dap_ref, dap)
    _acc(dbgx_ref, _rowsum(dzx))
    _acc(dbga_ref, _rowsum(dza))
    parts = []
    for h in range(NH):
        sl = slice(h * HD, (h + 1) * HD)
        parts.append(_nt(dzx[:, sl], wgx[h]) + _nt(dza[:, sl], wga[h]))
        dwgx_ref[h] += _tn(xc[:, sl], dzx[:, sl])
        dwga_ref[h] += _tn(xc[:, sl], dza[:, sl])
    dxc = dxc + jnp.concatenate(parts, axis=1)
    _acc(dcb_ref, _rowsum(dxc))
    for j in range(4):
        dcw_ref[pl.ds(j, 1), :] += _rowsum(dxc * sh[j])
    ext = jnp.concatenate([dxc, dxcnext_ref[...]], axis=0)
    m = n + 8
    dxb = cw[pl.ds(3, 1), :] * dxc
    for j in range(3):
        dxb = dxb + cw[pl.ds(j, 1), :] * pltpu.roll(ext, m - (3 - j), 0)[:n]
    dxcnext_ref[...] = dxc[:8]
    return dxb


def _layer_fwd(kind, layer, xprev, gprev, bprev, wt, wout, ks, vs, prm, ride=None):
    TS = TS_FWD[kind]
    _rows = functools.partial(_row_spec, ts=TS)
    S = xprev.shape[0]
    nt = S // TS
    N = wt.shape[0]
    nprm = len(prm)
    nch = TS // CH

    outs = [(jax.ShapeDtypeStruct((S, N), F32), _rows(N, nt, False)),
            (jax.ShapeDtypeStruct((S, D), F32), _rows(D, nt, False)),
            (jax.ShapeDtypeStruct((S, 1), F32), _rows(1, nt, False)),
            (jax.ShapeDtypeStruct((S, XHEADS * NMEM), MXU), _rows(XHEADS * NMEM, nt, False))]
    scratch = []
    if kind == 1:
        outs.append((jax.ShapeDtypeStruct((nt, NH, nch, HD, HD), F32),
                     pl.BlockSpec((None, NH, nch, HD, HD), lambda i: (i, 0, 0, 0, 0))))
        outs.append((jax.ShapeDtypeStruct((S, TOK), F32), _rows(TOK, nt, False)))
        outs.append((jax.ShapeDtypeStruct((S, TOK), MXU), _rows(TOK, nt, False)))
        scratch = [pltpu.VMEM((NH, HD, HD), F32)]
    elif kind == 2:
        outs += [(jax.ShapeDtypeStruct((S, TOK), F32), _rows(TOK, nt, False))] * 2
        scratch = [pltpu.VMEM((16, TOK), F32)]
    elif kind == 3:
        outs.append((jax.ShapeDtypeStruct((nt * 8, TOK), F32), pl.BlockSpec((8, TOK), lambda i: (i, 0))))
        outs += [(jax.ShapeDtypeStruct((S, TOK), F32), _rows(TOK, nt, False))] * 4
        scratch = [pltpu.VMEM((8, TOK), F32), pltpu.VMEM((1, TOK), F32)]
    nout = len(outs)
    nscr = len(scratch)
    nride = len(ride.arrays) if ride else 0

    def body(*refs):
        x_ref, g_ref, b_ref, wt_ref, wout_ref, ks_ref, vs_ref = refs[:7]
        prm_refs = refs[7:7 + nprm]
        nin = 7 + nprm + nride
        ride_src = refs[7 + nprm:nin]
        out_refs = refs[nin:nin + nout]
        ride_dst = refs[nin + nout:nin + nout + nride]
        scr = refs[nin + nout + nride:nin + nout + nride + nscr]
        ride_sems = refs[nin + nout + nride + nscr:]
        proj_ref, xhat_ref, rstd_ref = out_refs[:3]
        i = pl.program_id(0)
        if ride:
            @pl.when(i == 0)
            def _():
                ride.start(ride_src, ride_dst, ride_sems)

        xin = x_ref[...] * g_ref[...] + b_ref[...]
        proj = _nt(xin, wt_ref[...])
        proj_ref[...] = proj
        if kind == 0:
            tok, _ = _gmlp_fwd(proj[:, :TOK], proj[:, TOK:2 * TOK], prm_refs[0], prm_refs[1])
        elif kind == 1:
            st_ref, = scr

            @pl.when(i == 0)
            def _():
                st_ref[...] = jnp.zeros_like(st_ref)

            lb, _ = _lower_bound(prm_refs[0][...], layer)
            tok, out_refs[5][...], out_refs[6][...] = _hgrn_fwd(proj[:, :TOK], proj[:, TOK:2 * TOK], proj[:, 2 * TOK:3 * TOK],
                                                                lb, prm_refs[1][...], st_ref, out_refs[4])
        elif kind == 2:
            halo_ref, = scr

            @pl.when(i == 0)
            def _():
                halo_ref[...] = jnp.zeros_like(halo_ref)

            p = proj[:, :TOK]
            tok, (diff, y, _) = _pool_fwd(p, halo_ref[...], i * TS, prm_refs[0][...], prm_refs[1][...])
            out_refs[4][...] = diff
            out_refs[5][...] = y
            halo_ref[...] = p[TS - 16:]
        else:
            halo_ref, h_ref = scr

            @pl.when(i == 0)
            def _():
                halo_ref[...] = jnp.zeros_like(halo_ref)
                h_ref[...] = jnp.zeros_like(h_ref)

            out_refs[4][...] = jnp.broadcast_to(h_ref[...], (8, TOK))
            xb = proj[:, :TOK]
            tok, fres = _lru_fwd(xb, halo_ref[...], i * TS, prm_refs, h_ref[...])
            gx, ga, _, mult = fres[2]
            for r, val in zip(out_refs[5:9], (tok, gx, ga, mult)):
                r[...] = val
            halo_ref[...] = xb[TS - 8:]
            h_ref[...] = _row(tok, TS - 1)
        qx = proj[:, N - D - XW:N - D]
        gate = proj[:, N - D:]
        xo, ps = _xattn_fwd(qx, ks_ref, vs_ref)
        out_refs[3][...] = jnp.concatenate(ps, axis=1).astype(MXU)
        mixed = jnp.concatenate([tok, xo], axis=1) * (gate * _sigmoid(gate))
        z = ALPHA * xin + _nn(mixed, wout_ref[...])
        cen = z - jnp.mean(z, axis=-1, keepdims=True)
        rstd = lax.rsqrt(jnp.mean(cen * cen, axis=-1, keepdims=True) + LN_EPS)
        xhat_ref[...] = cen * rstd
        rstd_ref[...] = rstd
        if ride:
            @pl.when(i == nt - 1)
            def _():
                ride.wait(ride_src, ride_dst, ride_sems)

    ins = [(xprev, _rows(D, nt, False)), (gprev, _res(gprev)), (bprev, _res(bprev)), (wt, _res(wt)), (wout, _res(wout)),
           (ks, _res(ks)), (vs, _res(vs))] + [(p, _res(p)) for p in prm]
    if ride:
        ins += [(a, _ANY) for a in ride.arrays]
        outs += [(s, _ANY) for s in ride.out_shapes]
        scratch = scratch + ride.scratch
    return _call(body, f"layer{layer}_fwd", (nt,), ins, outs, scratch)


def _layer_bwd(kind, layer, up, is_last, xhat, rstd, probs, g_i, b_i, proj, wout, ks, vs, prm, extra, ride=None):
    TS = TS_BWD[kind]
    _rows = functools.partial(_row_spec, ts=TS)
    S = xhat.shape[0]
    nt = S // TS
    N = proj.shape[1]
    nprm = len(prm)
    nch = TS // CH

    ins = [(up, _rows(D, nt, True)), (xhat, _rows(D, nt, True)), (rstd, _rows(1, nt, True)), (g_i, _res(g_i)), (b_i, _res(b_i)),
           (proj, _rows(N, nt, True)), (wout, _res(wout)), (ks, _res(ks)), (vs, _res(vs)),
           (probs, _rows(XHEADS * NMEM, nt, True))] + [(p, _res(p)) for p in prm]
    nfixed = 10
    if kind == 1:
        ins.append((extra[0], pl.BlockSpec((None, NH, nch, HD, HD), lambda i: (nt - 1 - i, 0, 0, 0, 0))))
        ins += [(e, _rows(TOK, nt, True)) for e in extra[1:]]
    elif kind == 2:
        ins += [(e, _rows(TOK, nt, True)) for e in extra]
    elif kind == 3:
        hb = TS // 8
        ins.append((proj, pl.BlockSpec((8, TOK), lambda i: (jnp.maximum((nt - 1 - i) * hb - 1, 0), 0))))
        ins.append((extra[0], pl.BlockSpec((8, TOK), lambda i: (nt - 1 - i, 0))))
        ins += [(e, _rows(TOK, nt, True)) for e in extra[1:]]
    nin = len(ins)

    def acc(shape):
        return (jax.ShapeDtypeStruct(shape, F32), _res_sds(shape))

    outs = [(jax.ShapeDtypeStruct((S, D), F32), _rows(D, nt, True)),
            (jax.ShapeDtypeStruct((S, N), MXU), _rows(N, nt, True)),
            (jax.ShapeDtypeStruct((D, D), WIRE), _res_sds((D, D))),
            acc((XHEADS, NMEM, XW)), acc((XHEADS, NMEM, XW)), acc((1, D)), acc((1, D)), acc((1, HD))]
    scratch = []
    if kind == 0:
        outs += [acc((NH, HD, HD)), acc((NH, HD, HD))]
    elif kind == 1:
        outs += [acc((1, TOK)), acc((1, TOK))]
        scratch = [pltpu.VMEM((NH, HD, HD), F32)]
    elif kind == 2:
        outs += [acc((TOK, TOK)), acc((1, TOK))]
        scratch = [pltpu.VMEM((16, TOK), F32)]
    else:
        outs += [acc((4, TOK)), acc((1, TOK)), acc((NH, HD, HD)), acc((1, TOK)), acc((NH, HD, HD)), acc((1, TOK)), acc((1, TOK))]
        scratch = [pltpu.VMEM((1, TOK), F32), pltpu.VMEM((1, TOK), F32), pltpu.VMEM((8, TOK), F32)]
    scratch = scratch + [pltpu.VMEM((D, D), F32)]
    nout = len(outs)
    nscr = len(scratch)
    nride = len(ride.arrays) if ride else 0

    def body(*refs):
        up_ref, xhat_ref, rstd_ref, g_ref, b_ref, proj_ref, wout_ref, ks_ref, vs_ref, probs_ref = refs[:nfixed]
        prm_refs = refs[nfixed:nfixed + nprm]
        ext_refs = refs[nfixed + nprm:nin]
        ride_src = refs[nin:nin + nride]
        o0 = nin + nride
        out_refs = refs[o0:o0 + nout]
        ride_dst = refs[o0 + nout:o0 + nout + nride]
        scr = refs[o0 + nout + nride:o0 + nout + nride + nscr - 1]
        dwout_acc = refs[o0 + nout + nride + nscr - 1]
        ride_sems = refs[o0 + nout + nride + nscr:]
        dres_ref, dproj_ref, dwout_ref, dks_ref, dvs_ref, dg_ref, db_ref, loss_ref = out_refs[:8]
        pgrad = out_refs[8:]
        i = pl.program_id(0)
        tile = nt - 1 - i

        @pl.when(i == 0)
        def _():
            if ride:
                ride.start(ride_src, ride_dst, ride_sems)
            for r in out_refs[3:]:
                r[...] = jnp.zeros_like(r)
            dwout_acc[...] = jnp.zeros_like(dwout_acc)
            for r in scr:
                if kind != 1 or r is scr[0]:
                    r[...] = jnp.zeros_like(r)

        xhat_v = xhat_ref[...]
        if is_last:
            err = xhat_v * g_ref[...] + b_ref[...] - up_ref[...]
            dxo = err * (1.0 / D)
            loss_ref[...] += jnp.sum(0.5 * jnp.mean(err * err, axis=-1, keepdims=True), axis=0, keepdims=True)
        else:
            dxo = up_ref[...]
        _acc(dg_ref, _rowsum(dxo * xhat_v))
        _acc(db_ref, _rowsum(dxo))
        dxh = dxo * g_ref[...]
        dz = rstd_ref[...] * (dxh - jnp.mean(dxh, axis=-1, keepdims=True)
                              - xhat_v * jnp.mean(dxh * xhat_v, axis=-1, keepdims=True))
        dres_ref[...] = ALPHA * dz

        proj = proj_ref[...]
        qx = proj[:, N - D - XW:N - D]
        gate = proj[:, N - D:]
        sgate = _sigmoid(gate)
        silu = gate * sgate
        dmixed = _nt(dz, wout_ref[...])
        dcat = dmixed * silu
        dtok = dcat[:, :TOK]
        if kind == 0:
            u, v = proj[:, :TOK], proj[:, TOK:2 * TOK]
            tok, fres = _gmlp_fwd(u, v, prm_refs[0], prm_refs[1])
            du, dv = _gmlp_bwd(u, v, fres, dtok, pgrad[0], pgrad[1])
            dproj_ref[:, :TOK] = du.astype(MXU)
            dproj_ref[:, TOK:2 * TOK] = dv.astype(MXU)
        elif kind == 1:
            dst_ref, = scr
            lb, _ = _lower_bound(prm_refs[0][...], layer)
            q, fl, inp = proj[:, :TOK], proj[:, TOK:2 * TOK], proj[:, 2 * TOK:3 * TOK]
            pr = _hgrn_prep(q, fl, lb)
            tok, dq, dfl, dinp = _hgrn_bwd(q, pr, inp, lb, prm_refs[1][...], dtok, ext_refs[1][...], ext_refs[2][...],
                                           ext_refs[0], dst_ref, pgrad[1], pgrad[0])
            dproj_ref[:, :TOK] = dq.astype(MXU)
            dproj_ref[:, TOK:2 * TOK] = dfl.astype(MXU)
            dproj_ref[:, 2 * TOK:3 * TOK] = dinp.astype(MXU)
        elif kind == 2:
            diff, y = ext_refs[0][...], ext_refs[1][...]
            tok = y * prm_refs[1][...]
            fres = (diff, y, _pool_cnt(tile * TS, TS))
            dp = _pool_bwd(fres, dtok, scr[0], prm_refs[0][...], prm_refs[1][...], pgrad[0], pgrad[1])
            dproj_ref[:, :TOK] = dp.astype(MXU)
        else:
            xb = proj[:, :TOK]
            halo = jnp.where(tile == 0, 0.0, ext_refs[0][...])
            h0 = ext_refs[1][0:1]
            tok, gx, ga, mult = (r[...] for r in ext_refs[2:6])
            xc, sh = _lru_conv(xb, halo, prm_refs[0], prm_refs[1][...])
            sp = _softplus_neg(prm_refs[6][...])
            first = (tile * TS + lax.broadcasted_iota(jnp.int32, (TS, 1), 0)) == 0
            fres = (xc, sh, (gx, ga, sp, mult), first, jnp.exp(-LRU_C * ga * sp))
            dxb = _lru_bwd(fres, tok, h0, dtok, prm_refs, scr, pgrad)
            dproj_ref[:, :TOK] = dxb.astype(MXU)
        ps = [probs_ref[:, h * NMEM:(h + 1) * NMEM].astype(F32) for h in range(XHEADS)]
        xo = _nn(ps[0], vs_ref[0])
        for h in range(1, XHEADS):
            xo = xo + _nn(ps[h], vs_ref[h])
        dqx = _xattn_bwd(qx, ps, dcat[:, TOK:], ks_ref, vs_ref, dks_ref, dvs_ref)
        cat = jnp.concatenate([tok, xo], axis=1)
        dwout_acc[...] += _tn(cat * silu, dz)
        dgate = dmixed * cat * (sgate * (1.0 + gate * (1.0 - sgate)))
        dproj_ref[:, N - D - XW:N - D] = dqx.astype(MXU)
        dproj_ref[:, N - D:] = dgate.astype(MXU)

        @pl.when(i == nt - 1)
        def _():
            dwout_ref[...] = dwout_acc[...].astype(WIRE)
            if ride:
                ride.wait(ride_src, ride_dst, ride_sems)

    if ride:
        ins += [(a, _ANY) for a in ride.arrays]
        outs += [(s, _ANY) for s in ride.out_shapes]
        scratch = scratch + ride.scratch
    return _call(body, f"layer{layer}_bwd", (nt,), ins, outs, scratch, vmem=VMEM_LIMIT_WIDE if kind == 0 else VMEM_LIMIT)


def _proj_bwd(layer, dproj, dres, xprev, gprev, bprev, wt, ride=None):
    S = xprev.shape[0]
    nt = S // TSB
    N = wt.shape[0]

    nride = len(ride.arrays) if ride else 0

    def body(*refs):
        dproj_ref, dres_ref, x_ref, g_ref, b_ref, wt_ref = refs[:6]
        ride_src = refs[6:6 + nride]
        dx_ref, dwt_ref = refs[6 + nride:8 + nride]
        ride_dst = refs[8 + nride:8 + 2 * nride]
        acc_ref = refs[8 + 2 * nride]
        ride_sems = refs[9 + 2 * nride:]

        @pl.when(pl.program_id(0) == 0)
        def _():
            if ride:
                ride.start(ride_src, ride_dst, ride_sems)
            acc_ref[...] = jnp.zeros_like(acc_ref)

        dp = dproj_ref[...]
        xin = x_ref[...] * g_ref[...] + b_ref[...]
        dx_ref[...] = dres_ref[...] + _nn(dp, wt_ref[...])
        acc_ref[...] += _tn(dp, xin)

        @pl.when(pl.program_id(0) == nt - 1)
        def _():
            dwt_ref[...] = acc_ref[...].astype(WIRE)
            if ride:
                ride.wait(ride_src, ride_dst, ride_sems)

    ins = [(dproj, _row_spec(N, nt, False, TSB)), (dres, _row_spec(D, nt, False, TSB)), (xprev, _row_spec(D, nt, False, TSB)),
           (gprev, _res(gprev)), (bprev, _res(bprev)), (wt, _res(wt))]
    outs = [(jax.ShapeDtypeStruct((S, D), F32), _row_spec(D, nt, False, TSB)),
            (jax.ShapeDtypeStruct((N, D), WIRE), _res_sds((N, D)))]
    scratch = [pltpu.VMEM((N, D), F32)]
    if ride:
        ins += [(a, _ANY) for a in ride.arrays]
        outs += [(s, _ANY) for s in ride.out_shapes]
        scratch = scratch + ride.scratch
    return _call(body, f"layer{layer}_projbwd", (nt,), ins, outs, scratch)


def _head_mask(h):
    col = lax.broadcasted_iota(jnp.int32, (1, XW), 1)
    return (col // 64) == h


def _kv_fwd(mem, wkv):
    def body(mem_ref, w_ref, ks_ref, vs_ref):
        kv = _nn(mem_ref[...], w_ref[...])
        k, v = kv[:, :XW], kv[:, XW:]
        for h in range(XHEADS):
            ks_ref[h] = jnp.where(_head_mask(h), k, 0.0).astype(MXU)
            vs_ref[h] = jnp.where(_head_mask(h), v, 0.0).astype(MXU)

    sds = jax.ShapeDtypeStruct((XHEADS, NMEM, XW), MXU)
    return pl.pallas_call(body, name="kv_fwd", out_shape=(sds, sds), compiler_params=_cparams())(mem, wkv)


def _kv_bwd(mem, dks_l, dvs_l):
    def body(mem_ref, *refs):
        dks_refs, dvs_refs, out_ref = refs[:DEPTH], refs[DEPTH:2 * DEPTH], refs[2 * DEPTH]
        dk = jnp.zeros((NMEM, XW), F32)
        dv = jnp.zeros((NMEM, XW), F32)
        for h in range(XHEADS):
            m = _head_mask(h)
            for l in range(DEPTH):
                dk = dk + jnp.where(m, dks_refs[l][h], 0.0)
                dv = dv + jnp.where(m, dvs_refs[l][h], 0.0)
        out_ref[...] = _tn(mem_ref[...], jnp.concatenate([dk, dv], axis=1)).astype(WIRE)

    return pl.pallas_call(body, name="kv_bwd", out_shape=jax.ShapeDtypeStruct((D, 2 * XW), WIRE),
                          compiler_params=_cparams())(mem, *dks_l, *dvs_l)


def _prep_weights(w_ins, flip, w_out, wkv):
    def body(a_ref, b_ref, c_ref, d_ref, wo_ref, kv_ref, ao, bo, co, do, wo0, wo1, wo2, wo3, kvo):
        for t, (src, dst) in enumerate(((a_ref, ao), (b_ref, bo), (c_ref, co), (d_ref, do))):
            dst[...] = (src[...].T if flip[t] else src[...]).astype(MXU)
        for l, dst in enumerate((wo0, wo1, wo2, wo3)):
            dst[...] = wo_ref[l].astype(MXU)
        kvo[...] = kv_ref[...].astype(MXU)

    outs = [jax.ShapeDtypeStruct(w.shape[::-1] if flip[t] else w.shape, MXU) for t, w in enumerate(w_ins)]
    outs += [jax.ShapeDtypeStruct(w_out.shape[1:], MXU)] * DEPTH + [jax.ShapeDtypeStruct(wkv.shape, MXU)]
    return pl.pallas_call(body, name="prep_weights", out_shape=outs, compiler_params=_cparams())(*w_ins, w_out, wkv)


def _adam_math(w, g, m, v):
    m = B1 * m + (1.0 - B1) * g
    v = B2 * v + (1.0 - B2) * (g * g)
    m_hat = m / (1.0 - B1 ** STEP)
    v_hat = v / (1.0 - B2 ** STEP)
    delta = -LR * (m_hat / (jnp.sqrt(v_hat) + EPS) + WD * w)
    return delta, m, v


def _sum_adam(name, recv, w, m, v, transpose):
    rows, cols = recv.shape[1], recv.shape[2]

    def body(r_ref, w_ref, m_ref, v_ref, g_out, d_out, m_out, v_out, acc_ref):
        s = pl.program_id(0)

        @pl.when(s == 0)
        def _():
            acc_ref[...] = r_ref[...].astype(F32)

        @pl.when(s > 0)
        def _():
            acc_ref[...] += r_ref[...].astype(F32)

        @pl.when(s == NDEV - 1)
        def _():
            g = acc_ref[...].T if transpose else acc_ref[...]
            d, mn, vn = _adam_math(w_ref[...], g, m_ref[...], v_ref[...])
            g_out[...] = g
            d_out[...] = d
            m_out[...] = mn
            v_out[...] = vn

    sds = jax.ShapeDtypeStruct(w.shape, F32)
    ins = [(recv, pl.BlockSpec((None, rows, cols), lambda s: (s, 0, 0))), (w, _res(w)), (m, _res(m)), (v, _res(v))]
    outs = [(sds, _res_sds(w.shape))] * 4
    return _call(body, name, (NDEV,), ins, outs, [pltpu.VMEM((rows, cols), F32)])


def _bias_finalize(dbs_exp):
    def body(dbs_ref, dabs_ref):
        dabs_ref[...] = jnp.sum(dbs_ref[...], axis=-1)

    return pl.pallas_call(body, name="bias_finalize", out_shape=jax.ShapeDtypeStruct((NH, HD), F32),
                          compiler_params=_cparams())(dbs_exp)


def _lb_finalize(dlb, lb_logits):
    def body(dlb_ref, lg_ref, dlg_ref):
        total = jnp.zeros((DEPTH, TOK), F32)
        lg = lg_ref[...]
        e = jnp.exp(lg - jnp.max(lg, axis=0, keepdims=True))
        p = e / jnp.sum(e, axis=0, keepdims=True)
        row = lax.broadcasted_iota(jnp.int32, (DEPTH, TOK), 0)
        for layer in range(DEPTH):
            if layer % 4 != 1:
                continue
            dp = jnp.where((row >= 1) & (row <= layer), dlb_ref[...], 0.0)
            total = total + p * (dp - jnp.sum(p * dp, axis=0, keepdims=True))
        dlg_ref[...] = total

    return pl.pallas_call(body, name="lb_finalize", out_shape=jax.ShapeDtypeStruct((DEPTH, TOK), F32),
                          compiler_params=_cparams())(dlb, lb_logits)


def _small_sum_adam(name, gathered, w, m, v):
    rows = w.shape[0]

    def body(r_ref, w_ref, m_ref, v_ref, g_out, d_out, m_out, v_out):
        g = r_ref[0]
        for s in range(1, NDEV):
            g = g + r_ref[s]
        d, mn, vn = _adam_math(w_ref[...], g, m_ref[...], v_ref[...])
        g_out[...] = g
        d_out[...] = d
        m_out[...] = mn
        v_out[...] = vn

    sds = jax.ShapeDtypeStruct((rows, 128), F32)
    return pl.pallas_call(body, name=name, out_shape=(sds,) * 4, compiler_params=_cparams())(gathered, w, m, v)


def _sum_adam_stacked(name, recvs, w, m, v):
    nl, rows, cols = w.shape

    def body(*refs):
        r_refs = refs[:nl]
        w_ref, m_ref, v_ref, g_out, d_out, m_out, v_out, acc_ref = refs[nl:]
        s = pl.program_id(0)

        @pl.when(s == 0)
        def _():
            for l in range(nl):
                acc_ref[l] = r_refs[l][...].astype(F32)

        @pl.when(s > 0)
        def _():
            for l in range(nl):
                acc_ref[l] += r_refs[l][...].astype(F32)

        @pl.when(s == NDEV - 1)
        def _():
            g = acc_ref[...]
            d, mn, vn = _adam_math(w_ref[...], g, m_ref[...], v_ref[...])
            g_out[...] = g
            d_out[...] = d
            m_out[...] = mn
            v_out[...] = vn

    sds = jax.ShapeDtypeStruct(w.shape, F32)
    ins = [(r, pl.BlockSpec((None, rows, cols), lambda s: (s, 0, 0))) for r in recvs] + [(w, _res(w)), (m, _res(m)), (v, _res(v))]
    outs = [(sds, _res_sds(w.shape))] * 4
    return _call(body, name, (NDEV,), ins, outs, [pltpu.VMEM(w.shape, F32)])


def _group_adam(name, recvs, params):
    nk = len(recvs)

    def body(*refs):
        pos, oi = nk, nk + 3 * sum(p is not None for p in params)
        for k in range(nk):
            g = refs[k][0]
            for s in range(1, NDEV):
                g = g + refs[k][s]
            refs[oi][...] = g
            oi += 1
            if params[k] is not None:
                d, mn, vn = _adam_math(refs[pos][...], g, refs[pos + 1][...], refs[pos + 2][...])
                refs[oi][...] = d
                refs[oi + 1][...] = mn
                refs[oi + 2][...] = vn
                pos += 3
                oi += 3

    out_shape, counts = [], []
    for k in range(nk):
        counts.append(4 if params[k] is not None else 1)
        out_shape += [jax.ShapeDtypeStruct(recvs[k].shape[1:], F32)] * counts[-1]
    args = list(recvs) + [a for p in params if p is not None for a in p]
    flat = pl.pallas_call(body, name=name, out_shape=out_shape, compiler_params=_cparams())(*args)
    res, o = [], 0
    for cnt in counts:
        res.append(flat[o:o + cnt])
        o += cnt
    return res


def _adam_only(g, w, m, v):
    def body(g_ref, w_ref, m_ref, v_ref, d_out, m_out, v_out):
        d, mn, vn = _adam_math(w_ref[...], g_ref[...], m_ref[...], v_ref[...])
        d_out[...] = d
        m_out[...] = mn
        v_out[...] = vn

    sds = jax.ShapeDtypeStruct(w.shape, F32)
    return pl.pallas_call(body, name="shard_adam", out_shape=(sds,) * 3, compiler_params=_cparams())(g, w, m, v)


def _me_and_peers():
    x, y, c = lax.axis_index("x"), lax.axis_index("y"), lax.axis_index("c")
    me = 4 * x + 2 * y + c
    peers = []
    for k in range(1, NDEV):
        kx, ky, kc = (k >> 2) & 1, (k >> 1) & 1, k & 1
        px = x + kx - 2 * x * kx
        py = y + ky - 2 * y * ky
        pc = c + kc - 2 * c * kc
        peers.append(((px, py, pc), 4 * px + 2 * py + pc))
    return me, peers


_ANY = pl.BlockSpec(memory_space=pl.ANY)


class _Exchange:
    def __init__(self, arrays, split):
        self.arrays = list(arrays)
        self.split = list(split)
        n = len(self.arrays)
        self.out_shapes = []
        for a, sp in zip(self.arrays, self.split):
            rows = a.shape[0] // NDEV if sp else a.shape[0]
            self.out_shapes.append(jax.ShapeDtypeStruct((NDEV, rows, a.shape[1]), a.dtype))
        self.scratch = [pltpu.SemaphoreType.DMA((n, NDEV - 1)), pltpu.SemaphoreType.DMA((n, NDEV - 1)),
                        pltpu.SemaphoreType.DMA((n,))]

    def _block(self, src, t, d):
        if not self.split[t]:
            return src[t]
        rows = self.arrays[t].shape[0] // NDEV
        return src[t].at[pl.ds(d * rows, rows)]

    def start(self, src, dst, sems):
        send_sems, recv_sems, local_sems = sems
        me, peers = _me_and_peers()
        for t in range(len(self.arrays)):
            pltpu.make_async_copy(self._block(src, t, me), dst[t].at[me], local_sems.at[t]).start()
        for k, (dev, idx) in enumerate(peers):
            for t in range(len(self.arrays)):
                pltpu.make_async_remote_copy(src_ref=self._block(src, t, idx), dst_ref=dst[t].at[me],
                                             send_sem=send_sems.at[t, k], recv_sem=recv_sems.at[t, k],
                                             device_id=dev, device_id_type=pl.DeviceIdType.MESH).start()

    def wait(self, src, dst, sems):
        send_sems, recv_sems, local_sems = sems
        me, peers = _me_and_peers()

        def slot_copy(t, k, dev, idx):
            return pltpu.make_async_remote_copy(src_ref=dst[t].at[idx], dst_ref=dst[t].at[idx], send_sem=send_sems.at[t, k],
                                                recv_sem=recv_sems.at[t, k], device_id=dev,
                                                device_id_type=pl.DeviceIdType.MESH)

        for k, (dev, idx) in enumerate(peers):
            for t in range(len(self.arrays)):
                slot_copy(t, k, dev, idx).wait_recv()
        for k, (dev, idx) in enumerate(peers):
            for t in range(len(self.arrays)):
                slot_copy(t, k, dev, idx).wait_send()
        for t in range(len(self.arrays)):
            pltpu.make_async_copy(dst[t].at[me], dst[t].at[me], local_sems.at[t]).wait()

    def gather_by_chip(self, src, dst, sems):
        assert not any(self.split)
        send_sems, recv_sems, local_sems = sems
        n = len(self.arrays)
        x, y, c = lax.axis_index("x"), lax.axis_index("y"), lax.axis_index("c")
        me, sibling = 4 * x + 2 * y + c, (x, y, 1 - c)
        chips = [(1 - x, y), (x, 1 - y), (1 - x, 1 - y)]

        def index(chip, core):
            return 4 * chip[0] + 2 * chip[1] + core

        def copy(t, k, block, to, from_src):
            return pltpu.make_async_remote_copy(src_ref=src[t] if from_src else dst[t].at[block], dst_ref=dst[t].at[block],
                                                send_sem=send_sems.at[t, k], recv_sem=recv_sems.at[t, k],
                                                device_id=to, device_id_type=pl.DeviceIdType.MESH)

        local = [pltpu.make_async_copy(src[t], dst[t].at[me], local_sems.at[t]) for t in range(n)]
        for cp in local:
            cp.start()
        sends = []
        for t in range(n):
            sends.append(copy(t, 0, me, sibling, True))
            sends += [copy(t, 1 + j, me, (*chip, c), True) for j, chip in enumerate(chips)]
        for cp in sends:
            cp.start()
        for j, chip in enumerate(chips):
            for t in range(n):
                copy(t, 1 + j, index(chip, c), sibling, False).wait_recv()
                passed = copy(t, 4 + j, index(chip, c), sibling, False)
                passed.start()
                sends.append(passed)
        for t in range(n):
            copy(t, 0, index((x, y), 1 - c), sibling, False).wait_recv()
            for j, chip in enumerate(chips):
                copy(t, 4 + j, index(chip, 1 - c), sibling, False).wait_recv()
        for cp in sends:
            cp.wait_send()
        for cp in local:
            cp.wait()

    def run(self, name, by_chip=False):
        n = len(self.arrays)

        def body(*refs):
            src, dst, sems = refs[:n], refs[n:2 * n], refs[2 * n:]
            if by_chip:
                self.gather_by_chip(src, dst, sems)
                return
            self.start(src, dst, sems)
            self.wait(src, dst, sems)

        return pl.pallas_call(
            body, name=name, out_shape=self.out_shapes, in_specs=[_ANY] * n, out_specs=[_ANY] * n,
            scratch_shapes=self.scratch,
        )(*self.arrays)


SMALL = [("ln_g", (DEPTH, D), False), ("ln_b", (DEPTH, D), False), ("hgrn_lb_logits", (DEPTH, TOK), False),
         ("a_w_s", (1, NH, HD, HD), False), ("a_b_s", (1, NH, HD), False), ("b_norm_g", (1, TOK), True),
         ("c_w_pool", (1, 4, POOL_GROUP, POOL_GROUP), False), ("c_scale", (1, TOK), True),
         ("d_conv_w", (1, 4, TOK), True), ("d_conv_b", (1, TOK), True),
         ("d_w_gx", (1, NH, HD, HD), False), ("d_b_gx", (1, NH, HD), False),
         ("d_w_ga", (1, NH, HD, HD), False), ("d_b_ga", (1, NH, HD), False), ("d_a_param", (1, TOK), True)]


def _pack(parts, total_rows):
    flat = jnp.concatenate([p.reshape(-1).astype(F32) for p in parts])
    flat = jnp.pad(flat, (0, total_rows * 128 - flat.shape[0]))
    return flat.reshape(total_rows, 128)


def _size(shape):
    n = 1
    for s in shape:
        n *= s
    return n


def _rows_for(n):
    return -(-n // 1024) * 8


def kernel(x, mem, mem_kv_w, ln_g, ln_b, w_out, hgrn_lb_logits, a_w_in, a_w_s, a_b_s, b_w_in, b_norm_g, c_w_in, c_w_pool, c_scale, d_w_in, d_conv_w, d_conv_b, d_w_gx, d_b_gx, d_w_ga, d_b_ga, d_a_param, loss_target, m_mem_kv_w, m_ln_g, m_ln_b, m_w_out, m_hgrn_lb_logits, m_a_w_in, m_a_w_s, m_a_b_s, m_b_w_in, m_b_norm_g, m_c_w_in, m_c_w_pool, m_c_scale, m_d_w_in, m_d_conv_w, m_d_conv_b, m_d_w_gx, m_d_b_gx, m_d_w_ga, m_d_b_ga, m_d_a_param, v_mem_kv_w, v_ln_g, v_ln_b, v_w_out, v_hgrn_lb_logits, v_a_w_in, v_a_w_s, v_a_b_s, v_b_w_in, v_b_norm_g, v_c_w_in, v_c_w_pool, v_c_scale, v_d_w_in, v_d_conv_w, v_d_conv_b, v_d_w_gx, v_d_b_gx, v_d_w_ga, v_d_b_ga, v_d_a_param):
    W = dict(mem_kv_w=mem_kv_w, ln_g=ln_g, ln_b=ln_b, w_out=w_out, hgrn_lb_logits=hgrn_lb_logits, a_w_in=a_w_in, a_w_s=a_w_s,
             a_b_s=a_b_s, b_w_in=b_w_in, b_norm_g=b_norm_g, c_w_in=c_w_in, c_w_pool=c_w_pool, c_scale=c_scale, d_w_in=d_w_in,
             d_conv_w=d_conv_w, d_conv_b=d_conv_b, d_w_gx=d_w_gx, d_b_gx=d_b_gx, d_w_ga=d_w_ga, d_b_ga=d_b_ga, d_a_param=d_a_param)
    M = dict(mem_kv_w=m_mem_kv_w, ln_g=m_ln_g, ln_b=m_ln_b, w_out=m_w_out, hgrn_lb_logits=m_hgrn_lb_logits, a_w_in=m_a_w_in,
             a_w_s=m_a_w_s, a_b_s=m_a_b_s, b_w_in=m_b_w_in, b_norm_g=m_b_norm_g, c_w_in=m_c_w_in, c_w_pool=m_c_w_pool,
             c_scale=m_c_scale, d_w_in=m_d_w_in, d_conv_w=m_d_conv_w, d_conv_b=m_d_conv_b, d_w_gx=m_d_w_gx, d_b_gx=m_d_b_gx,
             d_w_ga=m_d_w_ga, d_b_ga=m_d_b_ga, d_a_param=m_d_a_param)
    V = dict(mem_kv_w=v_mem_kv_w, ln_g=v_ln_g, ln_b=v_ln_b, w_out=v_w_out, hgrn_lb_logits=v_hgrn_lb_logits, a_w_in=v_a_w_in,
             a_w_s=v_a_w_s, a_b_s=v_a_b_s, b_w_in=v_b_w_in, b_norm_g=v_b_norm_g, c_w_in=v_c_w_in, c_w_pool=v_c_w_pool,
             c_scale=v_c_scale, d_w_in=v_d_w_in, d_conv_w=v_d_conv_w, d_conv_b=v_d_conv_b, d_w_gx=v_d_w_gx, d_b_gx=v_d_b_gx,
             d_w_ga=v_d_w_ga, d_b_ga=v_d_b_ga, d_a_param=v_d_a_param)
    me = 4 * lax.axis_index("x") + 2 * lax.axis_index("y") + lax.axis_index("c")
    x2, mem2, tgt2 = x[0], mem[0], loss_target[0]
    in_names = ["a_w_in", "b_w_in", "c_w_in", "d_w_in"]

    shard_names = [n for n, _, sh in SMALL if sh]
    small_shard = _pack([W[n] for n in shard_names], 8)
    flip = [W[n].shape[2] % 128 == 0 for n in in_names]

    def shard2d(tree, t):
        a = tree[in_names[t]][0]
        return a if flip[t] else jnp.swapaxes(a, 0, 1)

    wts = _prep_weights([shard2d(W, t) for t in range(DEPTH)], flip, w_out, mem_kv_w)
    wt_sh, wo_sh, wkv_sh = wts[:4], wts[4:8], wts[8]
    g0 = _Exchange([wt_sh[0], wo_sh[0], wkv_sh, small_shard], [False] * 4).run("gather_first", by_chip=True)
    wt_full = [g0[0].reshape(-1, D)]
    wout_full = [g0[1].reshape(D, D)]
    wkv_full = g0[2].reshape(D, 2 * XW)
    sm = g0[3].reshape(NDEV, 1024)
    full_small = {}
    off = 0
    for n, shape, _ in [s for s in SMALL if s[2]]:
        per = _size(shape) // NDEV
        blk = sm[:, off:off + per]
        if n == "d_conv_w":
            full_small[n] = blk.reshape(NDEV, 4, TOK // NDEV).transpose(1, 0, 2).reshape(4, TOK)
        else:
            full_small[n] = blk.reshape(1, TOK)
        off += per

    ks, vs = _kv_fwd(mem2, wkv_full)
    tri_bs = jnp.broadcast_to(a_b_s[0][:, :, None], (NH, HD, HD))
    wbd = jnp.zeros((TOK, TOK), F32)
    for g in range(4):
        wbd = lax.dynamic_update_slice(wbd, c_w_pool[0, g], (g * POOL_GROUP, g * POOL_GROUP))
    wbd = wbd.astype(MXU)
    prm = {0: [a_w_s[0], tri_bs],
           1: [hgrn_lb_logits, full_small["b_norm_g"]],
           2: [wbd, full_small["c_scale"]],
           3: [full_small["d_conv_w"], full_small["d_conv_b"], d_w_gx[0].astype(MXU), d_b_gx[0].reshape(1, TOK),
               d_w_ga[0].astype(MXU), d_b_ga[0].reshape(1, TOK), full_small["d_a_param"]]}
    ones = jnp.ones((1, D), F32)
    zeros = jnp.zeros((1, D), F32)
    xs, gs, bs = [x2], [ones], [zeros]
    saved = []
    for i in range(DEPTH):
        ride = _Exchange([wt_sh[i + 1], wo_sh[i + 1]], [False, False]) if i + 1 < DEPTH else None
        res = _layer_fwd(i, i, xs[i], gs[i], bs[i], wt_full[i], wout_full[i], ks, vs, prm[i], ride)
        if ride:
            wt_full.append(res[-2].reshape(-1, D))
            wout_full.append(res[-1].reshape(D, D))
            res = res[:-2]
        saved.append(res)
        xs.append(res[1])
        gs.append(ln_g[i:i + 1])
        bs.append(ln_b[i:i + 1])

    up = tgt2
    grads = {}
    dks_l, dvs_l, dwt_l, dwout_l, dlng_l, dlnb_l = [], [], [], [], [], []
    recv_wt, recv_wo = [None] * DEPTH, [None] * DEPTH
    loss_part = None
    sharded = {n for n, _, sh in SMALL if sh}
    group = {3: ["ln_g#3", "ln_b#3", "d_conv_w", "d_conv_b", "d_w_gx", "d_b_gx", "d_w_ga", "d_b_ga", "d_a_param"],
             2: ["ln_g#2", "ln_b#2", "c_w_pool", "c_scale"],
             1: ["ln_g#1", "ln_b#1", "hgrn_lb_logits", "b_norm_g"],
             0: ["ln_g#0", "ln_b#0", "a_w_s", "a_b_s", "loss"]}

    def small_of(l):
        return [grads[e].reshape(-1, grads[e].shape[-1]) for e in group[l]]

    recv_small = [None] * DEPTH
    for i in reversed(range(DEPTH)):
        res = saved[i]
        extra = None if len(res) <= 4 else (res[4] if len(res) == 5 else res[4:])
        ride = None
        if i + 1 < DEPTH:
            smalls = small_of(i + 1)
            ride = _Exchange([dwt_l[-1], dwout_l[-1]] + smalls, [True, True] + [False] * len(smalls))
        out = _layer_bwd(i, i, up, i == DEPTH - 1, res[1], res[2], res[3], gs[i + 1], bs[i + 1], res[0], wout_full[i], ks, vs,
                         prm[i], extra, ride)
        if ride:
            nr = len(ride.arrays)
            recv_wt[i + 1], recv_wo[i + 1], recv_small[i + 1] = out[-nr], out[-nr + 1], out[-nr + 2:]
            out = out[:-nr]
        dres, dproj, dwout_i, dks_i, dvs_i, dg_i, db_i, loss_i = out[:8]
        pg = out[8:]
        if i == DEPTH - 1:
            grads["loss"] = loss_i
        dks_l.append(dks_i)
        dvs_l.append(dvs_i)
        dwout_l.append(dwout_i)
        grads[f"ln_g#{i}"], grads[f"ln_b#{i}"] = dg_i, db_i
        if i == 0:
            grads["a_w_s"], dbs_exp = pg
            grads["a_b_s"] = _bias_finalize(dbs_exp)
        elif i == 1:
            dlb, grads["b_norm_g"] = pg
            grads["hgrn_lb_logits"] = _lb_finalize(dlb, hgrn_lb_logits)
        elif i == 2:
            dwbd, grads["c_scale"] = pg
            grads["c_w_pool"] = jnp.stack([lax.dynamic_slice(dwbd, (g * POOL_GROUP, g * POOL_GROUP), (POOL_GROUP, POOL_GROUP))
                                           for g in range(4)])
        else:
            (grads["d_conv_w"], grads["d_conv_b"], grads["d_w_gx"], grads["d_b_gx"], grads["d_w_ga"], grads["d_b_ga"],
             grads["d_a_param"]) = pg
        ride = None
        if i == 0:
            dwkv = _kv_bwd(mem2, dks_l, dvs_l)
            smalls = small_of(0)
            ride = _Exchange([dwout_i, dwkv] + smalls, [True, True] + [False] * len(smalls))
        pb = _proj_bwd(i, dproj, dres, xs[i], gs[i], bs[i], wt_full[i], ride)
        up, dwt = pb[:2]
        if ride:
            recv_wo[0], recv_kv, recv_small[0] = pb[2], pb[3], pb[4:]
        dwt_l.append(dwt)
    grad_x = up[None]

    recv_wt[0], = _Exchange([dwt_l[-1]], [True]).run("scatter_last")

    outs = {}
    for t, n in enumerate(in_names):
        res = _sum_adam(f"adam_{n}", recv_wt[t], shard2d(W, t), shard2d(M, t), shard2d(V, t), flip[t])
        outs[n] = tuple((o if flip[t] else jnp.swapaxes(o, 0, 1))[None] for o in res)
    outs["w_out"] = tuple(_sum_adam_stacked("adam_w_out", recv_wo, w_out, m_w_out, v_w_out))
    outs["mem_kv_w"] = _sum_adam("adam_mem_kv_w", recv_kv, mem_kv_w, m_mem_kv_w, v_mem_kv_w, False)

    def entry_of(tree, e, like):
        if "#" in e:
            n, l = e.split("#")
            return tree[n][int(l):int(l) + 1]
        return tree[e].reshape(like.shape[1:])

    small = [{}, {}, {}, {}]
    for l in range(DEPTH):
        params = [None if (e == "loss" or e in sharded) else tuple(entry_of(t, e, r) for t in (W, M, V))
                  for e, r in zip(group[l], recv_small[l])]
        for e, res in zip(group[l], _group_adam(f"small_adam{l}", recv_small[l], params)):
            for j, a in enumerate(res):
                small[j][e] = a
    loss = small[0]["loss"][0, 0]
    for j in range(4):
        for n in ("ln_g", "ln_b"):
            small[j][n] = jnp.concatenate([small[j][f"{n}#{l}"] for l in range(DEPTH)], axis=0)
    g_small = small[0]
    for n, _, sh in SMALL:
        if not sh:
            outs[n] = tuple(small[j][n] for j in range(4))
    per = TOK // NDEV
    g_sh = {n: lax.dynamic_slice_in_dim(g_small[n], me * per, per, axis=1) for n, s, sh in SMALL if sh}
    gp = _pack([g_sh[n] for n in shard_names], 8)
    d_p, m_p, v_p = _adam_only(gp, small_shard, _pack([M[n] for n in shard_names], 8), _pack([V[n] for n in shard_names], 8))
    o = 0
    for n in shard_names:
        cnt = _size(W[n].shape)
        outs[n] = (g_sh[n],) + tuple(t.reshape(-1)[o:o + cnt].reshape(W[n].shape) for t in (d_p, m_p, v_p))
        o += cnt

    order = ["mem_kv_w", "ln_g", "ln_b", "w_out", "hgrn_lb_logits", "a_w_in", "a_w_s", "a_b_s", "b_w_in", "b_norm_g", "c_w_in",
             "c_w_pool", "c_scale", "d_w_in", "d_conv_w", "d_conv_b", "d_w_gx", "d_b_gx", "d_w_ga", "d_b_ga", "d_a_param"]
    result = [loss, grad_x]
    for j in range(4):
        result += [outs[n][j].reshape(W[n].shape) for n in order]
    return tuple(result)
```
